```python
import numpy as np
import jax, jax.numpy as jnp
from jax import lax

D_MODEL = 1024
BATCH = 2
SEQ = 8192
DEPTH = 1

CONV_CH = 512
CONV_GROUPS = 8
CONV_K = 3
N_HEADS = 8
N_KV_GROUPS = 2
HEADS_PER_GROUP = N_HEADS // N_KV_GROUPS
HEAD_DIM = 64
ATTN_WIDTH = N_HEADS * HEAD_DIM
KV_WIDTH = N_KV_GROUPS * HEAD_DIM
MIX_WIDTH = CONV_CH + ATTN_WIDTH
ROPE_THETA = 500000.0
ROT_DIM = HEAD_DIM // 4
CMP_BLOCK = 32
CMP_STRIDE = 16
CMP_HIDDEN = 256
SLC_BLOCK = 64
SLC_TOP_N = 16
WINDOW = 512
Q_BLOCK = 128
D_FF = 2816
MACARON_W = 0.5
N_MOD = 9
IN_WIDTH = 3 * CONV_CH + ATTN_WIDTH + 6 * KV_WIDTH + 3 * N_HEADS
EPS = 1e-6
NEG = -1e30
BIG = 1e9

kernel_name = "hymba_conv_nsa_macaron_adaln"


def _rmsnorm(x, g):
    xf = x.astype(jnp.float32)
    y = xf * lax.rsqrt(jnp.mean(xf * xf, axis=-1, keepdims=True) + EPS)
    return (y * g.astype(jnp.float32)).astype(x.dtype)


def _group_rmsnorm(y, g, n_groups):
    shp = y.shape
    yg = y.reshape(shp[:-1] + (n_groups, shp[-1] // n_groups)).astype(jnp.float32)
    yg = yg * lax.rsqrt(jnp.mean(yg * yg, axis=-1, keepdims=True) + EPS)
    return (yg.reshape(shp) * g.astype(jnp.float32)).astype(y.dtype)


def _modulate(h, shift, scale):
    return h * (1.0 + scale) + shift


def _swiglu(h, w_gate, w_up, w_down):
    return (jax.nn.silu(h @ w_gate) * (h @ w_up)) @ w_down


def _partial_rope(t, positions):
    half = ROT_DIM // 2
    freqs = jnp.power(ROPE_THETA, -2.0 * jnp.arange(half, dtype=jnp.float32) / ROT_DIM)
    ang = positions.astype(jnp.float32)[:, :, None] * freqs
    cos = jnp.cos(ang)[:, :, None, :].astype(t.dtype)
    sin = jnp.sin(ang)[:, :, None, :].astype(t.dtype)
    t1, t2, rest = t[..., :half], t[..., half:ROT_DIM], t[..., ROT_DIM:]
    return jnp.concatenate([t1 * cos - t2 * sin, t2 * cos + t1 * sin, rest], axis=-1)


def _compress(k, pos_emb, w1, w2):
    S = k.shape[2]
    n_cmp = (S - CMP_BLOCK) // CMP_STRIDE + 1
    idx = np.arange(n_cmp)[:, None] * CMP_STRIDE + np.arange(CMP_BLOCK)[None, :]
    blocks = k[:, :, idx] + pos_emb
    flat = blocks.reshape(k.shape[0], k.shape[1], n_cmp, CMP_BLOCK * k.shape[3])
    return jax.nn.silu(flat @ w1) @ w2


def _cmp_slc_overlap(n_cmp, n_slc):
    c0 = np.arange(n_cmp) * CMP_STRIDE
    c1 = c0 + CMP_BLOCK - 1
    s0 = np.arange(n_slc) * SLC_BLOCK
    s1 = s0 + SLC_BLOCK - 1
    return ((c0[:, None] <= s1[None, :]) & (c1[:, None] >= s0[None, :])).astype(np.float32)


def _masked_softmax(s, mask):
    return jax.nn.softmax(jnp.where(mask, s, NEG), axis=-1)


def _nsa_attention(q, kc, vc, ks, vs, kw, vw, gates):
    B, G, HPG, S, dk = q.shape
    n_cmp = kc.shape[2]
    n_slc = S // SLC_BLOCK
    top_n = min(SLC_TOP_N, n_slc)
    scale = dk ** -0.5
    cmp_end = jnp.arange(n_cmp) * CMP_STRIDE + CMP_BLOCK - 1
    overlap = jnp.asarray(_cmp_slc_overlap(n_cmp, n_slc))
    ks_blk = ks.reshape(B, G, n_slc, SLC_BLOCK, dk)
    vs_blk = vs.reshape(B, G, n_slc, SLC_BLOCK, dk)
    kw_pad = jnp.pad(kw, ((0, 0), (0, 0), (WINDOW, 0), (0, 0)))
    vw_pad = jnp.pad(vw, ((0, 0), (0, 0), (WINDOW, 0), (0, 0)))
    b_idx = jnp.arange(B)[:, None, None, None]
    g_idx = jnp.arange(G)[None, :, None, None]
    blk = jnp.arange(n_slc)
    in_blk = jnp.arange(SLC_BLOCK)
    win_off = jnp.arange(WINDOW + Q_BLOCK) - WINDOW

    def one_block(qb):
        t0 = qb * Q_BLOCK
        t = t0 + jnp.arange(Q_BLOCK)
        qblk = lax.dynamic_slice_in_dim(q, t0, Q_BLOCK, axis=3)
        gblk = jax.nn.sigmoid(lax.dynamic_slice_in_dim(gates, t0, Q_BLOCK, axis=3).astype(jnp.float32))

        s = jnp.einsum('bghqd,bgnd->bghqn', qblk, kc).astype(jnp.float32) * scale
        valid = cmp_end[None, :] <= t[:, None]
        p_cmp = _masked_softmax(s, valid) * valid
        o_cmp = jnp.einsum('bghqn,bgnd->bghqd', p_cmp.astype(vc.dtype), vc)

        imp = jnp.einsum('bghqn,nj->bgqj', p_cmp, overlap)
        cur = t // SLC_BLOCK
        future = blk[None, :] > cur[:, None]
        forced = (blk[None, :] == 0) | (blk[None, :] == cur[:, None]) | (blk[None, :] == cur[:, None] - 1)
        score = jnp.where(future, -BIG, jnp.where(forced, BIG, imp))
        _, idx = lax.top_k(score, top_n)
        k_sel = ks_blk[b_idx, g_idx, idx].reshape(B, G, Q_BLOCK, top_n * SLC_BLOCK, dk)
        v_sel = vs_blk[b_idx, g_idx, idx].reshape(B, G, Q_BLOCK, top_n * SLC_BLOCK, dk)
        pos_sel = (idx[..., None] * SLC_BLOCK + in_blk).reshape(B, G, Q_BLOCK, top_n * SLC_BLOCK)
        mask_sel = (pos_sel <= t[None, None, :, None])[:, :, None]
        s = jnp.einsum('bghqd,bgqkd->bghqk', qblk, k_sel).astype(jnp.float32) * scale
        p = _masked_softmax(s, mask_sel)
        o_slc = jnp.einsum('bghqk,bgqkd->bghqd', p.astype(v_sel.dtype), v_sel)

        k_win = lax.dynamic_slice_in_dim(kw_pad, t0, WINDOW + Q_BLOCK, axis=2)
        v_win = lax.dynamic_slice_in_dim(vw_pad, t0, WINDOW + Q_BLOCK, axis=2)
        pos_win = t0 + win_off
        dist = t[:, None] - pos_win[None, :]
        mask_win = (pos_win[None, :] >= 0) & (dist >= 0) & (dist < WINDOW)
        s = jnp.einsum('bghqd,bgkd->bghqk', qblk, k_win).astype(jnp.float32) * scale
        p = _masked_softmax(s, mask_win)
        o_win = jnp.einsum('bghqk,bgkd->bghqd', p.astype(v_win.dtype), v_win)

        o = gblk[..., 0:1] * o_cmp + gblk[..., 1:2] * o_slc + gblk[..., 2:3] * o_win
        return o.astype(q.dtype)

    out = lax.map(one_block, jnp.arange(S // Q_BLOCK))
    return out.transpose(1, 0, 4, 2, 3, 5).reshape(B, S, G * HPG * dk)


def _hybrid_mixer(h, positions, w_in, conv_w, cmp_pos_k, cmp_pos_v, w_cmpk1, w_cmpk2,
                  w_cmpv1, w_cmpv2, g_out_conv, g_out_attn, w_out):
    B, S, _ = h.shape
    proj = h @ w_in
    sizes = [CONV_CH, CONV_CH, CONV_CH, ATTN_WIDTH] + [KV_WIDTH] * 6 + [3 * N_HEADS]
    cuts = np.cumsum(sizes)[:-1].tolist()
    cb, cc, cx, q, kc, vc, ks, vs, kw, vw, gt = jnp.split(proj, cuts, axis=-1)

    u = cc * cx
    v = lax.conv_general_dilated(u, conv_w[:, None, :], window_strides=(1,),
                                 padding=[(CONV_K - 1, 0)],
                                 dimension_numbers=('NWC', 'WIO', 'NWC'),
                                 feature_group_count=CONV_CH)
    y_conv = cb * v

    qh = _partial_rope(q.reshape(B, S, N_HEADS, HEAD_DIM), positions)
    qh = qh.reshape(B, S, N_KV_GROUPS, HEADS_PER_GROUP, HEAD_DIM).transpose(0, 2, 3, 1, 4)

    def kv_heads(t, rope):
        t = t.reshape(B, S, N_KV_GROUPS, HEAD_DIM)
        if rope:
            t = _partial_rope(t, positions)
        return t.transpose(0, 2, 1, 3)

    kc_c = _compress(kv_heads(kc, True), cmp_pos_k, w_cmpk1, w_cmpk2)
    vc_c = _compress(kv_heads(vc, False), cmp_pos_v, w_cmpv1, w_cmpv2)
    gates = gt.reshape(B, S, N_KV_GROUPS, HEADS_PER_GROUP, 3).transpose(0, 2, 3, 1, 4)
    y_attn = _nsa_attention(qh, kc_c, vc_c, kv_heads(ks, True), kv_heads(vs, False),
                            kv_heads(kw, True), kv_heads(vw, False), gates)

    y = jnp.concatenate([_group_rmsnorm(y_conv, g_out_conv, CONV_GROUPS),
                         _group_rmsnorm(y_attn, g_out_attn, N_HEADS)], axis=-1)
    return y @ w_out


def setup_inputs(seed: int = 0) -> dict:
    key = jax.random.key(seed)
    ks = jax.random.split(key, 32)
    f32 = jnp.float32

    def nrm(k, shape, s):
        return jax.random.normal(k, shape, f32) * s

    def gain(k, shape):
        return 1.0 + 0.02 * jax.random.normal(k, shape, f32)

    L, D = DEPTH, D_MODEL
    return {
        "x": nrm(ks[0], (BATCH, SEQ, D), 1.0),
        "c": nrm(ks[1], (BATCH, D), 1.0),
        "positions": jnp.tile(jnp.arange(SEQ, dtype=jnp.int32)[None, :], (BATCH, 1)),
        "w_ada": nrm(ks[2], (L, D, N_MOD * D), 0.5 * D ** -0.5),
        "b_ada": nrm(ks[3], (L, N_MOD * D), 0.1),
        "g_ffn1": gain(ks[4], (L, D)),
        "w1_gate": nrm(ks[5], (L, D, D_FF), D ** -0.5),
        "w1_up": nrm(ks[6], (L, D, D_FF), D ** -0.5),
        "w1_down": nrm(ks[7], (L, D_FF, D), D_FF ** -0.5),
        "g_mix": gain(ks[8], (L, D)),
        "w_in": nrm(ks[9], (L, D, IN_WIDTH), D ** -0.5),
        "conv_w": nrm(ks[10], (L, CONV_K, CONV_CH), CONV_K ** -0.5),
        "cmp_pos_k": nrm(ks[11], (L, CMP_BLOCK, HEAD_DIM), 0.5),
        "cmp_pos_v": nrm(ks[12], (L, CMP_BLOCK, HEAD_DIM), 0.5),
        "w_cmpk1": nrm(ks[13], (L, CMP_BLOCK * HEAD_DIM, CMP_HIDDEN), (CMP_BLOCK * HEAD_DIM) ** -0.5),
        "w_cmpk2": nrm(ks[14], (L, CMP_HIDDEN, HEAD_DIM), CMP_HIDDEN ** -0.5),
        "w_cmpv1": nrm(ks[15], (L, CMP_BLOCK * HEAD_DIM, CMP_HIDDEN), (CMP_BLOCK * HEAD_DIM) ** -0.5),
        "w_cmpv2": nrm(ks[16], (L, CMP_HIDDEN, HEAD_DIM), CMP_HIDDEN ** -0.5),
        "g_out_conv": gain(ks[17], (L, CONV_CH)),
        "g_out_attn": gain(ks[18], (L, ATTN_WIDTH)),
        "w_out": nrm(ks[19], (L, MIX_WIDTH, D), MIX_WIDTH ** -0.5),
        "g_ffn2": gain(ks[20], (L, D)),
        "w2_gate": nrm(ks[21], (L, D, D_FF), D ** -0.5),
        "w2_up": nrm(ks[22], (L, D, D_FF), D ** -0.5),
        "w2_down": nrm(ks[23], (L, D_FF, D), D_FF ** -0.5),
        "g_final": gain(ks[24], (D,)),
    }


def reference(x, c, positions, w_ada, b_ada, g_ffn1, w1_gate, w1_up, w1_down, g_mix, w_in,
              conv_w, cmp_pos_k, cmp_pos_v, w_cmpk1, w_cmpk2, w_cmpv1, w_cmpv2, g_out_conv,
              g_out_attn, w_out, g_ffn2, w2_gate, w2_up, w2_down, g_final):
    B = x.shape[0]
    c_act = jax.nn.silu(c)
    for l in range(DEPTH):
        mod = (c_act @ w_ada[l] + b_ada[l]).reshape(B, N_MOD, D_MODEL)[:, :, None, :]
        sh1, sc1, gt1, sh2, sc2, gt2, sh3, sc3, gt3 = [mod[:, i] for i in range(N_MOD)]
        h = _modulate(_rmsnorm(x, g_ffn1[l]), sh1, sc1)
        x = x + MACARON_W * gt1 * _swiglu(h, w1_gate[l], w1_up[l], w1_down[l])
        h = _modulate(_rmsnorm(x, g_mix[l]), sh2, sc2)
        x = x + gt2 * _hybrid_mixer(h, positions, w_in[l], conv_w[l], cmp_pos_k[l], cmp_pos_v[l],
                                    w_cmpk1[l], w_cmpk2[l], w_cmpv1[l], w_cmpv2[l],
                                    g_out_conv[l], g_out_attn[l], w_out[l])
        h = _modulate(_rmsnorm(x, g_ffn2[l]), sh3, sc3)
        x = x + MACARON_W * gt3 * _swiglu(h, w2_gate[l], w2_up[l], w2_down[l])
    return _rmsnorm(x, g_final)
```

```python
import functools

import numpy as np
import jax
import jax.numpy as jnp
from jax import lax
from jax.experimental import pallas as pl
from jax.experimental.pallas import tpu as pltpu

F32 = jnp.float32
BF16 = jnp.bfloat16

CONV_CH = 512
CONV_GROUPS = 8
N_HEADS = 8
N_KV_GROUPS = 2
HPG = N_HEADS // N_KV_GROUPS
HEAD_DIM = 64
ATTN_WIDTH = N_HEADS * HEAD_DIM
KV_WIDTH = N_KV_GROUPS * HEAD_DIM
ROPE_THETA = 500000.0
ROT_DIM = HEAD_DIM // 4
ROT_HALF = ROT_DIM // 2
CMP_BLOCK = 32
CMP_STRIDE = 16
CMP_HIDDEN = 256
SLC_BLOCK = 64
SLC_TOP_N = 16
WINDOW = 512
Q_BLOCK = 128
MACARON_W = 0.5
N_MOD = 9
EPS = 1e-6
NEG = -1e30
BIG = 1e9

LANES = 128
VMEM_LIMIT = 56 * 1024 * 1024

TOKEN_TILE = 512
FF_TILE = 256
KEY_CHUNK = 512
WIN_KEYS = WINDOW + Q_BLOCK
GATE_PAD = LANES


def _dot(a, b):
    return jnp.dot(a, b, preferred_element_type=F32)


def _rms(x, g):
    return x * lax.rsqrt(jnp.mean(x * x, axis=-1, keepdims=True) + EPS) * g


def _split_bf16(x):
    hi = x.astype(BF16)
    lo = (x - hi.astype(F32)).astype(BF16)
    return hi, lo


def _ada_kernel(c_ref, w_ref, b_ref, o_ref):
    c = c_ref[...]
    c_act = c * jax.nn.sigmoid(c)
    o_ref[...] = jnp.dot(c_act, w_ref[...], preferred_element_type=F32,
                         precision=lax.Precision.HIGHEST) + b_ref[...]


def _ada_call(c_pad, w_ada, b_ada):
    rows, d = c_pad.shape
    n = w_ada.shape[1]
    tn = 1024
    return pl.pallas_call(
        _ada_kernel,
        grid=(n // tn,),
        in_specs=[pl.BlockSpec((rows, d), lambda j: (0, 0)),
                  pl.BlockSpec((d, tn), lambda j: (0, j)),
                  pl.BlockSpec((1, tn), lambda j: (0, j))],
        out_specs=pl.BlockSpec((rows, tn), lambda j: (0, j)),
        out_shape=jax.ShapeDtypeStruct((rows, n), F32),
        compiler_params=pltpu.CompilerParams(dimension_semantics=("arbitrary",),
                                             vmem_limit_bytes=VMEM_LIMIT),
        name="adaln_mod",
    )(c_pad, w_ada, b_ada)


def _ffn_core(x, shift, scale, gate, g, wg_ref, wu_ref, wd_ref):
    h = _rms(x, g) * (1.0 + scale) + shift
    hb = h.astype(BF16)
    d_ff = wg_ref.shape[1]
    acc = None
    for j in range(d_ff // FF_TILE):
        sl = slice(j * FF_TILE, (j + 1) * FF_TILE)
        gg = _dot(hb, wg_ref[:, sl])
        uu = _dot(hb, wu_ref[:, sl])
        a = (gg * jax.nn.sigmoid(gg) * uu).astype(BF16)
        d = _dot(a, wd_ref[sl, :])
        acc = d if acc is None else acc + d
    return x + (MACARON_W * gate) * acc


def _ffn_kernel(x_ref, mod_ref, g_ref, wg_ref, wu_ref, wd_ref, o_ref):
    o_ref[...] = _ffn_core(x_ref[...], mod_ref[0:1, :], mod_ref[1:2, :], mod_ref[2:3, :],
                           g_ref[...], wg_ref, wu_ref, wd_ref)


def _const_spec(shape):
    nd = len(shape)
    return pl.BlockSpec(shape, lambda *_: (0,) * nd, pipeline_mode=pl.Buffered(1))


def _ffn_call(x, mod, g, wg, wu, wd):
    B, S, D = x.shape
    tm = TOKEN_TILE
    return pl.pallas_call(
        _ffn_kernel,
        grid=(B, S // tm),
        in_specs=[pl.BlockSpec((None, tm, D), lambda b, i: (b, i, 0)),
                  pl.BlockSpec((None, N_MOD, D), lambda b, i: (b, 0, 0)),
                  _const_spec(g.shape), _const_spec(wg.shape), _const_spec(wu.shape),
                  _const_spec(wd.shape)],
        out_specs=pl.BlockSpec((None, tm, D), lambda b, i: (b, i, 0)),
        out_shape=jax.ShapeDtypeStruct((B, S, D), F32),
        compiler_params=pltpu.CompilerParams(dimension_semantics=("parallel", "parallel"),
                                             vmem_limit_bytes=VMEM_LIMIT),
        name="ffn1",
    )(x, mod, g, wg, wu, wd)


def _inproj_kernel(x_ref, pos_ref, mod_ref, gmix_ref, win_ref, convw_ref, gconv_ref, freq_ref,
                   gmat_ref, yc_ref, q_ref, kc_ref, vc_ref, ksT_ref, vs_ref, kwT_ref, vw_ref,
                   gt_ref, carry_ref):
    tm = x_ref.shape[0]

    @pl.when(pl.program_id(1) == 0)
    def _():
        carry_ref[...] = jnp.zeros_like(carry_ref)

    x = x_ref[...]
    h = _rms(x, gmix_ref[...]) * (1.0 + mod_ref[4:5, :]) + mod_ref[3:4, :]
    hb = h.astype(BF16)

    def proj(c0, width):
        return _dot(hb, win_ref[:, c0:c0 + width])

    cb = proj(0, CONV_CH)
    u = proj(CONV_CH, CONV_CH) * proj(2 * CONV_CH, CONV_CH)
    row = lax.broadcasted_iota(jnp.int32, (tm, 1), 0)
    prev1 = carry_ref[7:8, :]
    prev2 = carry_ref[6:7, :]
    u1 = jnp.where(row >= 1, pltpu.roll(u, 1, 0), prev1)
    u2 = jnp.where(row >= 2, pltpu.roll(u, 2, 0), jnp.where(row == 1, prev1, prev2))
    carry_ref[...] = u[tm - 8:tm, :]
    v = convw_ref[0:1, :] * u2 + convw_ref[1:2, :] * u1 + convw_ref[2:3, :] * u
    y = cb * v
    hi, lo = _split_bf16(y * y)
    ms = _dot(hi, gmat_ref[...]) + _dot(lo, gmat_ref[...])
    yc_ref[...] = (y * lax.rsqrt(ms + EPS) * gconv_ref[...]).astype(yc_ref.dtype)

    ang = pos_ref[...].astype(F32) * freq_ref[...]
    d = lax.broadcasted_iota(jnp.int32, (1, LANES), 1) & (HEAD_DIM - 1)
    cos_t = jnp.where(d < ROT_DIM, jnp.cos(ang), 1.0)
    sin_raw = jnp.sin(ang)
    sin_t = jnp.where(d < ROT_HALF, -sin_raw, jnp.where(d < ROT_DIM, sin_raw, 0.0))
    first_half = d < ROT_HALF

    def rope(t):
        outs = []
        for j in range(t.shape[1] // LANES):
            tj = t[:, j * LANES:(j + 1) * LANES]
            partner = jnp.where(first_half, pltpu.roll(tj, LANES - ROT_HALF, 1),
                                pltpu.roll(tj, ROT_HALF, 1))
            outs.append(tj * cos_t + partner * sin_t)
        return outs[0] if len(outs) == 1 else jnp.concatenate(outs, axis=1)

    c0 = 3 * CONV_CH
    q = rope(proj(c0, ATTN_WIDTH)) * (HEAD_DIM ** -0.5)
    q_ref[...] = q.astype(q_ref.dtype)
    c0 += ATTN_WIDTH
    kc_ref[...] = rope(proj(c0, KV_WIDTH))
    vc_ref[...] = proj(c0 + KV_WIDTH, KV_WIDTH)
    ks = rope(proj(c0 + 2 * KV_WIDTH, KV_WIDTH))
    ksT_ref[...] = ks.T.astype(ksT_ref.dtype)
    vs = proj(c0 + 3 * KV_WIDTH, KV_WIDTH).astype(vs_ref.dtype)
    kw = rope(proj(c0 + 4 * KV_WIDTH, KV_WIDTH))
    kwT_ref[...] = kw.T.astype(kwT_ref.dtype)
    vw = proj(c0 + 5 * KV_WIDTH, KV_WIDTH).astype(vw_ref.dtype)
    for g in range(N_KV_GROUPS):
        vs_ref[g] = vs[:, g * HEAD_DIM:(g + 1) * HEAD_DIM]
        vw_ref[g] = vw[:, g * HEAD_DIM:(g + 1) * HEAD_DIM]
    gt_ref[...] = jax.nn.sigmoid(proj(c0 + 6 * KV_WIDTH, N_KV_GROUPS * GATE_PAD))


def _inproj_call(x, pos3, mod, gmix, win, convw, gconv, freq, gmat):
    B, S, D = x.shape
    tm = TOKEN_TILE
    tok = lambda w: pl.BlockSpec((None, tm, w), lambda b, i: (b, i, 0))
    kT = pl.BlockSpec((None, KV_WIDTH, tm), lambda b, i: (b, 0, i))
    vsp = pl.BlockSpec((None, N_KV_GROUPS, tm, HEAD_DIM), lambda b, i: (b, 0, i, 0))
    out_shapes = (
        jax.ShapeDtypeStruct((B, S, CONV_CH), BF16),
        jax.ShapeDtypeStruct((B, S, ATTN_WIDTH), BF16),
        jax.ShapeDtypeStruct((B, S, KV_WIDTH), F32),
        jax.ShapeDtypeStruct((B, S, KV_WIDTH), F32),
        jax.ShapeDtypeStruct((B, KV_WIDTH, S), BF16),
        jax.ShapeDtypeStruct((B, N_KV_GROUPS, S, HEAD_DIM), BF16),
        jax.ShapeDtypeStruct((B, KV_WIDTH, S), BF16),
        jax.ShapeDtypeStruct((B, N_KV_GROUPS, S, HEAD_DIM), BF16),
        jax.ShapeDtypeStruct((B, S, N_KV_GROUPS * GATE_PAD), F32),
    )
    return pl.pallas_call(
        _inproj_kernel,
        grid=(B, S // tm),
        in_specs=[tok(D),
                  pl.BlockSpec((None, tm, 1), lambda b, i: (b, i, 0)),
                  pl.BlockSpec((None, N_MOD, D), lambda b, i: (b, 0, 0)),
                  _const_spec(gmix.shape), _const_spec(win.shape), _const_spec(convw.shape),
                  _const_spec(gconv.shape), _const_spec(freq.shape), _const_spec(gmat.shape)],
        out_specs=(tok(CONV_CH), tok(ATTN_WIDTH), tok(KV_WIDTH), tok(KV_WIDTH), kT, vsp, kT, vsp,
                   tok(N_KV_GROUPS * GATE_PAD)),
        out_shape=out_shapes,
        scratch_shapes=[pltpu.VMEM((8, CONV_CH), F32)],
        compiler_params=pltpu.CompilerParams(dimension_semantics=("arbitrary", "arbitrary"),
                                             vmem_limit_bytes=VMEM_LIMIT),
        name="mixer_inproj",
    )(x, pos3, mod, gmix, win, convw, gconv, freq, gmat)


def _compress_kernel(kf_ref, vf_ref, pek_ref, pev_ref, wk1_ref, wk2_ref, wv1_ref, wv2_ref,
                     kcT_ref, vcc_ref):
    n = kf_ref.shape[0]

    def mlp(f, pe_ref, w1_ref, w2_ref):
        a = _dot((f + pe_ref[0:1, :]).astype(BF16), w1_ref[0])
        b = _dot((f + pe_ref[1:2, :]).astype(BF16), w1_ref[1])
        hpre = a + pltpu.roll(b, n - 1, 0)
        hid = (hpre * jax.nn.sigmoid(hpre)).astype(BF16)
        return _dot(hid, w2_ref[...])

    kcT_ref[...] = mlp(kf_ref[...], pek_ref, wk1_ref, wk2_ref).T.astype(kcT_ref.dtype)
    vc = mlp(vf_ref[...], pev_ref, wv1_ref, wv2_ref).astype(vcc_ref.dtype)
    for g in range(N_KV_GROUPS):
        vcc_ref[g] = vc[:, g * HEAD_DIM:(g + 1) * HEAD_DIM]


def _compress_call(kf, vf, pek, pev, wk1, wk2, wv1, wv2):
    B, n, width = kf.shape
    flat = pl.BlockSpec((None, n, width), lambda b: (b, 0, 0))
    return pl.pallas_call(
        _compress_kernel,
        grid=(B,),
        in_specs=[flat, flat, _const_spec(pek.shape), _const_spec(pev.shape),
                  _const_spec(wk1.shape), _const_spec(wk2.shape), _const_spec(wv1.shape),
                  _const_spec(wv2.shape)],
        out_specs=(pl.BlockSpec((None, KV_WIDTH, n), lambda b: (b, 0, 0)),
                   pl.BlockSpec((None, N_KV_GROUPS, n, HEAD_DIM), lambda b: (b, 0, 0, 0))),
        out_shape=(jax.ShapeDtypeStruct((B, KV_WIDTH, n), BF16),
                   jax.ShapeDtypeStruct((B, N_KV_GROUPS, n, HEAD_DIM), BF16)),
        compiler_params=pltpu.CompilerParams(dimension_semantics=("arbitrary",),
                                             vmem_limit_bytes=VMEM_LIMIT),
        name="kv_compress",
    )(kf, vf, pek, pev, wk1, wk2, wv1, wv2)


def _attn_kernel(q_ref, gt_ref, kcT_ref, vcc_ref, ksT_ref, vs_ref, kwT_ref, vw_ref, ovl_ref,
                 e_ref, ga_ref, o_ref):
    QB = Q_BLOCK
    n_cmp = kcT_ref.shape[1]
    n_slc = ovl_ref.shape[1]
    t0 = pl.program_id(2) * QB
    tq = t0 + lax.broadcasted_iota(jnp.int32, (QB, 1), 0)

    q = q_ref[...]
    qh = jnp.concatenate([q[:, h * HEAD_DIM:(h + 1) * HEAD_DIM] for h in range(HPG)], axis=0)

    s = _dot(qh, kcT_ref[...]).reshape(HPG, QB, n_cmp)
    cmp_end = lax.broadcasted_iota(jnp.int32, (1, n_cmp), 1) * CMP_STRIDE + (CMP_BLOCK - 1)
    valid = (cmp_end <= tq)[None]
    s = jnp.where(valid, s, NEG)
    m = jnp.max(s, axis=-1, keepdims=True)
    e = jnp.where(valid, jnp.exp(s - m), 0.0)
    l = jnp.sum(e, axis=-1, keepdims=True)
    p_cmp = e / jnp.where(l > 0.0, l, 1.0)
    o_cmp = _dot(p_cmp.reshape(HPG * QB, n_cmp).astype(BF16), vcc_ref[...])

    hi, lo = _split_bf16(jnp.sum(p_cmp, axis=0))
    imp = _dot(hi, ovl_ref[...]) + _dot(lo, ovl_ref[...])
    blk = lax.broadcasted_iota(jnp.int32, (1, n_slc), 1)
    blk_f = blk.astype(F32)
    cur = tq // SLC_BLOCK
    future = blk > cur
    forced = (blk == 0) | (blk == cur) | (blk == cur - 1)
    score = jnp.where(future, -BIG, jnp.where(forced, BIG, imp))
    picked = jnp.zeros(score.shape, dtype=jnp.bool_)
    for _ in range(min(SLC_TOP_N, n_slc)):
        best = jnp.max(score, axis=-1, keepdims=True)
        first = jnp.min(jnp.where(score == best, blk_f, float(n_slc)), axis=-1, keepdims=True)
        hit = blk_f == first
        picked = picked | hit
        score = jnp.where(hit, -jnp.inf, score)
    sel_bias = jnp.where(picked & jnp.logical_not(future), 0.0, NEG).astype(BF16)

    KC = KEY_CHUNK

    def chunk(c, carry):
        m, l, acc = carry
        k0 = pl.multiple_of(c * KC, KC)
        bias = _dot(sel_bias, e_ref[:, pl.ds(k0, KC)])
        kpos = k0 + lax.broadcasted_iota(jnp.int32, (1, KC), 1)
        bias = jnp.where(kpos <= tq, bias, NEG)
        s = _dot(qh, ksT_ref[:, pl.ds(k0, KC)]).reshape(HPG, QB, KC) + bias[None]
        m_new = jnp.maximum(m, jnp.max(s, axis=-1, keepdims=True))
        alpha = jnp.exp(m - m_new)
        p = jnp.exp(s - m_new)
        l = alpha * l + jnp.sum(p, axis=-1, keepdims=True)
        pv = _dot(p.reshape(HPG * QB, KC).astype(BF16), vs_ref[pl.ds(k0, KC), :])
        return m_new, l, alpha * acc + pv.reshape(HPG, QB, HEAD_DIM)

    init = (jnp.full((HPG, QB, 1), NEG, F32), jnp.zeros((HPG, QB, 1), F32),
            jnp.zeros((HPG, QB, HEAD_DIM), F32))
    _, l, acc = lax.fori_loop(0, (t0 + QB - 1) // KC + 1, chunk, init)
    o_slc = acc / l

    w0 = pl.multiple_of(jnp.maximum(t0 - WINDOW, 0), QB)
    dist = tq - (w0 + lax.broadcasted_iota(jnp.int32, (1, WIN_KEYS), 1))
    in_win = ((dist >= 0) & (dist < WINDOW))[None]
    s = _dot(qh, kwT_ref[:, pl.ds(w0, WIN_KEYS)]).reshape(HPG, QB, WIN_KEYS)
    s = jnp.where(in_win, s, NEG)
    p = jnp.exp(s - jnp.max(s, axis=-1, keepdims=True))
    l = jnp.sum(p, axis=-1, keepdims=True)
    o_win = _dot(p.reshape(HPG * QB, WIN_KEYS).astype(BF16),
                 vw_ref[pl.ds(w0, WIN_KEYS), :]).reshape(HPG, QB, HEAD_DIM) / l

    o_cmp = o_cmp.reshape(HPG, QB, HEAD_DIM)
    gts = gt_ref[...]
    outs = []
    for h in range(HPG):
        o = (gts[:, 3 * h:3 * h + 1] * o_cmp[h] + gts[:, 3 * h + 1:3 * h + 2] * o_slc[h]
             + gts[:, 3 * h + 2:3 * h + 3] * o_win[h])
        o = o * lax.rsqrt(jnp.mean(o * o, axis=-1, keepdims=True) + EPS)
        outs.append(o * ga_ref[:, h * HEAD_DIM:(h + 1) * HEAD_DIM])
    o_ref[...] = jnp.concatenate(outs, axis=1).astype(o_ref.dtype)


def _attn_call(q, gates, kcT, vcc, ksT, vs, kwT, vw, ovl, expand, g_attn):
    B, S, _ = q.shape
    n_cmp = kcT.shape[2]
    gw = HPG * HEAD_DIM
    kT = lambda n: pl.BlockSpec((None, HEAD_DIM, n), lambda b, g, i: (b, g, 0))
    vv = lambda n: pl.BlockSpec((None, None, n, HEAD_DIM), lambda b, g, i: (b, g, 0, 0))
    return pl.pallas_call(
        _attn_kernel,
        grid=(B, N_KV_GROUPS, S // Q_BLOCK),
        in_specs=[pl.BlockSpec((None, Q_BLOCK, gw), lambda b, g, i: (b, i, g)),
                  pl.BlockSpec((None, Q_BLOCK, GATE_PAD), lambda b, g, i: (b, i, g)),
                  kT(n_cmp), vv(n_cmp), kT(S), vv(S), kT(S), vv(S),
                  pl.BlockSpec(ovl.shape, lambda b, g, i: (0, 0)),
                  pl.BlockSpec(expand.shape, lambda b, g, i: (0, 0)),
                  pl.BlockSpec((1, gw), lambda b, g, i: (0, g))],
        out_specs=pl.BlockSpec((None, Q_BLOCK, gw), lambda b, g, i: (b, i, g)),
        out_shape=jax.ShapeDtypeStruct((B, S, ATTN_WIDTH), BF16),
        compiler_params=pltpu.CompilerParams(
            dimension_semantics=("parallel", "parallel", "arbitrary"),
            vmem_limit_bytes=VMEM_LIMIT),
        name="nsa_attention",
    )(q, gates, kcT, vcc, ksT, vs, kwT, vw, ovl, expand, g_attn)


def _out_kernel(x_ref, yc_ref, ya_ref, mod_ref, wo_ref, g_ref, wg_ref, wu_ref, wd_ref, gf_ref,
                o_ref):
    mix = _dot(yc_ref[...], wo_ref[0:CONV_CH, :]) + _dot(ya_ref[...], wo_ref[CONV_CH:, :])
    x = x_ref[...] + mod_ref[5:6, :] * mix
    x = _ffn_core(x, mod_ref[6:7, :], mod_ref[7:8, :], mod_ref[8:9, :], g_ref[...], wg_ref,
                  wu_ref, wd_ref)
    o_ref[...] = _rms(x, gf_ref[...])


def _out_call(x, yc, ya, mod, wo, g, wg, wu, wd, gf):
    B, S, D = x.shape
    tm = TOKEN_TILE
    tok = lambda w: pl.BlockSpec((None, tm, w), lambda b, i: (b, i, 0))
    return pl.pallas_call(
        _out_kernel,
        grid=(B, S // tm),
        in_specs=[tok(D), tok(CONV_CH), tok(ATTN_WIDTH),
                  pl.BlockSpec((None, N_MOD, D), lambda b, i: (b, 0, 0)),
                  _const_spec(wo.shape), _const_spec(g.shape), _const_spec(wg.shape),
                  _const_spec(wu.shape), _const_spec(wd.shape), _const_spec(gf.shape)],
        out_specs=tok(D),
        out_shape=jax.ShapeDtypeStruct((B, S, D), F32),
        compiler_params=pltpu.CompilerParams(dimension_semantics=("parallel", "parallel"),
                                             vmem_limit_bytes=VMEM_LIMIT),
        name="outproj_ffn2",
    )(x, yc, ya, mod, wo, g, wg, wu, wd, gf)


def _compress_weights(pos_emb, w1, w2):
    half = CMP_BLOCK // 2
    G = N_KV_GROUPS
    eye = jnp.eye(G, dtype=F32)
    w1r = w1.reshape(2, half, HEAD_DIM, CMP_HIDDEN)
    w1big = jnp.einsum('pldc,gh->plgdhc', w1r, eye).reshape(2, half * G * HEAD_DIM, G * CMP_HIDDEN)
    w2big = jnp.einsum('cd,gh->gchd', w2, eye).reshape(G * CMP_HIDDEN, G * HEAD_DIM)
    pe = jnp.broadcast_to(pos_emb.reshape(2, half, 1, HEAD_DIM), (2, half, G, HEAD_DIM))
    return pe.reshape(2, half * G * HEAD_DIM), w1big.astype(BF16), w2big.astype(BF16)


def kernel(x, c, positions, w_ada, b_ada, g_ffn1, w1_gate, w1_up, w1_down, g_mix, w_in, conv_w, cmp_pos_k, cmp_pos_v, w_cmpk1, w_cmpk2, w_cmpv1, w_cmpv2, g_out_conv, g_out_attn, w_out, g_ffn2, w2_gate, w2_up, w2_down, g_final):
    B, S, D = x.shape
    depth = w_ada.shape[0]
    n_slc = S // SLC_BLOCK
    half = CMP_BLOCK // 2
    n_half = S // half

    c_pad = jnp.pad(c, ((0, 8 - B), (0, 0)))
    row = lambda a: a.reshape(1, -1)

    freq_half = jnp.power(ROPE_THETA, -2.0 * jnp.arange(ROT_HALF, dtype=F32) / ROT_DIM)
    freq = jnp.tile(freq_half, LANES // ROT_HALF).reshape(1, LANES)
    gidx = np.arange(CONV_CH) // (CONV_CH // CONV_GROUPS)
    gmat = jnp.asarray((gidx[:, None] == gidx[None, :]) / (CONV_CH // CONV_GROUPS), dtype=BF16)
    c0 = np.arange(n_half) * CMP_STRIDE
    s0 = np.arange(n_slc) * SLC_BLOCK
    ovl = ((c0[:, None] <= s0[None, :] + SLC_BLOCK - 1) & (c0[:, None] + CMP_BLOCK - 1 >= s0[None, :]))
    ovl = jnp.asarray(ovl, dtype=BF16)
    expand = jnp.asarray(np.arange(n_slc)[:, None] == (np.arange(S) // SLC_BLOCK)[None, :], dtype=BF16)
    pos3 = positions.reshape(B, S, 1)

    for l in range(depth):
        mod = _ada_call(c_pad, w_ada[l], row(b_ada[l]))[:B].reshape(B, N_MOD, D)

        x = _ffn_call(x, mod, row(g_ffn1[l]), w1_gate[l].astype(BF16), w1_up[l].astype(BF16),
                      w1_down[l].astype(BF16))

        n_main = w_in.shape[2] - N_KV_GROUPS * 3 * HPG
        gate_cols = [jnp.pad(w_in[l][:, n_main + g * 3 * HPG:n_main + (g + 1) * 3 * HPG],
                             ((0, 0), (0, GATE_PAD - 3 * HPG))) for g in range(N_KV_GROUPS)]
        win = jnp.concatenate([w_in[l][:, :n_main]] + gate_cols, axis=1).astype(BF16)
        (yc, q, kc, vc, ksT, vs, kwT, vw, gates) = _inproj_call(
            x, pos3, mod, row(g_mix[l]), win, conv_w[l], row(g_out_conv[l]), freq, gmat)

        pek, wk1, wk2 = _compress_weights(cmp_pos_k[l], w_cmpk1[l], w_cmpk2[l])
        pev, wv1, wv2 = _compress_weights(cmp_pos_v[l], w_cmpv1[l], w_cmpv2[l])
        kcT, vcc = _compress_call(kc.reshape(B, n_half, half * KV_WIDTH),
                                  vc.reshape(B, n_half, half * KV_WIDTH),
                                  pek, pev, wk1, wk2, wv1, wv2)

        ya = _attn_call(q, gates, kcT, vcc, ksT, vs, kwT, vw, ovl, expand, row(g_out_attn[l]))

        last = l == depth - 1
        assert last, "final norm is fused into the last layer's output kernel"
        x = _out_call(x, yc, ya, mod, w_out[l].astype(BF16), row(g_ffn2[l]),
                      w2_gate[l].astype(BF16), w2_up[l].astype(BF16), w2_down[l].astype(BF16),
                      row(g_final))
    return x
```

```python
import functools
import math

import numpy as np
import jax
import jax.numpy as jnp
from jax import lax
from jax.experimental import pallas as pl
from jax.experimental.pallas import tpu as pltpu

F32 = jnp.float32
BF16 = jnp.bfloat16

CONV_CH = 512
CONV_GROUPS = 8
N_HEADS = 8
N_KV_GROUPS = 2
HPG = N_HEADS // N_KV_GROUPS
HEAD_DIM = 64
ATTN_WIDTH = N_HEADS * HEAD_DIM
KV_WIDTH = N_KV_GROUPS * HEAD_DIM
ROPE_THETA = 500000.0
ROT_DIM = HEAD_DIM // 4
ROT_HALF = ROT_DIM // 2
CMP_BLOCK = 32
CMP_STRIDE = 16
CMP_HIDDEN = 256
SLC_BLOCK = 64
SLC_TOP_N = 16
WINDOW = 512
Q_BLOCK = 128
MACARON_W = 0.5
N_MOD = 9
EPS = 1e-6
NEG = -1e30
BIG = 1e9

LANES = 128
MXU_DEPTH = 256
VMEM_LIMIT = 56 * 1024 * 1024

TOKEN_TILE = 512
FF_TILE = 256
KEY_CHUNK = 512
WIN_KEYS = WINDOW + Q_BLOCK
GATE_PAD = LANES
QW = HPG * Q_BLOCK
Q_SCALE = HEAD_DIM ** -0.5 * math.log2(math.e)
M_FLOOR = -1e20


def _dot(a, b):
    return jnp.dot(a, b, preferred_element_type=F32)


def _rms(x, g):
    return x * lax.rsqrt(jnp.mean(x * x, axis=-1, keepdims=True) + EPS) * g


def _split_bf16(x):
    hi = x.astype(BF16)
    lo = (x - hi.astype(F32)).astype(BF16)
    return hi, lo


def _const_spec(shape):
    nd = len(shape)
    return pl.BlockSpec(shape, lambda *_: (0,) * nd, pipeline_mode=pl.Buffered(1))


def _ada_kernel(c_ref, w_ref, b_ref, o_ref):
    c = c_ref[...]
    c_act = c * jax.nn.sigmoid(c)
    o_ref[...] = jnp.dot(c_act, w_ref[...], preferred_element_type=F32,
                         precision=lax.Precision.HIGHEST) + b_ref[...]


def _ada_call(c_pad, w_ada, b_ada):
    rows, d = c_pad.shape
    n = w_ada.shape[1]
    tn = 1024
    return pl.pallas_call(
        _ada_kernel,
        grid=(n // tn,),
        in_specs=[pl.BlockSpec((rows, d), lambda j: (0, 0)),
                  pl.BlockSpec((d, tn), lambda j: (0, j)),
                  pl.BlockSpec((1, tn), lambda j: (0, j))],
        out_specs=pl.BlockSpec((rows, tn), lambda j: (0, j)),
        out_shape=jax.ShapeDtypeStruct((rows, n), F32),
        compiler_params=pltpu.CompilerParams(dimension_semantics=("arbitrary",),
                                             vmem_limit_bytes=VMEM_LIMIT),
        name="adaln_mod",
    )(c_pad, w_ada, b_ada)


def _ffn_core(x, shift, scale, gate, g, wg_ref, wu_ref, wd_ref):
    h = _rms(x, g) * (1.0 + scale) + shift
    hb = h.astype(BF16)
    d_ff = wg_ref.shape[1]
    acc = None
    for j in range(d_ff // FF_TILE):
        sl = slice(j * FF_TILE, (j + 1) * FF_TILE)
        gg = _dot(hb, wg_ref[:, sl])
        uu = _dot(hb, wu_ref[:, sl])
        a = (gg * jax.nn.sigmoid(gg) * uu).astype(BF16)
        d = _dot(a, wd_ref[sl, :])
        acc = d if acc is None else acc + d
    return x + (MACARON_W * gate) * acc


def _ffn_kernel(x_ref, mod_ref, g_ref, wg_ref, wu_ref, wd_ref, o_ref):
    o_ref[...] = _ffn_core(x_ref[...], mod_ref[0:1, :], mod_ref[1:2, :], mod_ref[2:3, :],
                           g_ref[...], wg_ref, wu_ref, wd_ref)


def _ffn_call(x, mod, g, wg, wu, wd):
    B, S, D = x.shape
    tm = TOKEN_TILE
    return pl.pallas_call(
        _ffn_kernel,
        grid=(B, S // tm),
        in_specs=[pl.BlockSpec((None, tm, D), lambda b, i: (b, i, 0)),
                  pl.BlockSpec((None, N_MOD, D), lambda b, i: (b, 0, 0)),
                  _const_spec(g.shape), _const_spec(wg.shape), _const_spec(wu.shape),
                  _const_spec(wd.shape)],
        out_specs=pl.BlockSpec((None, tm, D), lambda b, i: (b, i, 0)),
        out_shape=jax.ShapeDtypeStruct((B, S, D), F32),
        compiler_params=pltpu.CompilerParams(dimension_semantics=("parallel", "parallel"),
                                             vmem_limit_bytes=VMEM_LIMIT),
        name="ffn1",
    )(x, mod, g, wg, wu, wd)


def _inproj_kernel(x_ref, pos_ref, mod_ref, gmix_ref, win_ref, convw_ref, gconv_ref, freq_ref,
                   gmat_ref, onehot_ref, yc_ref, qT_ref, kc_ref, vc_ref, ksa_ref, vsT_ref, kw_ref,
                   vwT_ref, gT_ref, carry_ref):
    tm = x_ref.shape[0]

    @pl.when(pl.program_id(1) == 0)
    def _():
        carry_ref[...] = jnp.zeros_like(carry_ref)

    x = x_ref[...]
    h = _rms(x, gmix_ref[...]) * (1.0 + mod_ref[4:5, :]) + mod_ref[3:4, :]
    hb = h.astype(BF16)

    def proj(c0, width):
        return _dot(hb, win_ref[:, c0:c0 + width])

    cb = proj(0, CONV_CH)
    u = proj(CONV_CH, CONV_CH) * proj(2 * CONV_CH, CONV_CH)
    row = lax.broadcasted_iota(jnp.int32, (tm, 1), 0)
    prev1 = carry_ref[7:8, :]
    prev2 = carry_ref[6:7, :]
    u1 = jnp.where(row >= 1, pltpu.roll(u, 1, 0), prev1)
    u2 = jnp.where(row >= 2, pltpu.roll(u, 2, 0), jnp.where(row == 1, prev1, prev2))
    carry_ref[...] = u[tm - 8:tm, :]
    v = convw_ref[0:1, :] * u2 + convw_ref[1:2, :] * u1 + convw_ref[2:3, :] * u
    y = cb * v
    hi, lo = _split_bf16(y * y)
    ms = _dot(hi, gmat_ref[...]) + _dot(lo, gmat_ref[...])
    yc_ref[...] = (y * lax.rsqrt(ms + EPS) * gconv_ref[...]).astype(yc_ref.dtype)

    ang = pos_ref[...].astype(F32) * freq_ref[...]
    d = lax.broadcasted_iota(jnp.int32, (1, LANES), 1) & (HEAD_DIM - 1)
    cos_t = jnp.where(d < ROT_DIM, jnp.cos(ang), 1.0)
    sin_raw = jnp.sin(ang)
    sin_t = jnp.where(d < ROT_HALF, -sin_raw, jnp.where(d < ROT_DIM, sin_raw, 0.0))
    first_half = d < ROT_HALF

    def rope(t):
        outs = []
        for j in range(t.shape[1] // LANES):
            tj = t[:, j * LANES:(j + 1) * LANES]
            partner = jnp.where(first_half, pltpu.roll(tj, LANES - ROT_HALF, 1),
                                pltpu.roll(tj, ROT_HALF, 1))
            outs.append(tj * cos_t + partner * sin_t)
        return outs[0] if len(outs) == 1 else jnp.concatenate(outs, axis=1)

    c0 = 3 * CONV_CH
    q = rope(proj(c0, ATTN_WIDTH)) * Q_SCALE
    qT_ref[...] = q.T.astype(qT_ref.dtype)
    c0 += ATTN_WIDTH
    kc_ref[...] = rope(proj(c0, KV_WIDTH))
    vc_ref[...] = proj(c0 + KV_WIDTH, KV_WIDTH)
    ks = rope(proj(c0 + 2 * KV_WIDTH, KV_WIDTH)).astype(BF16)
    vsT_ref[...] = proj(c0 + 3 * KV_WIDTH, KV_WIDTH).T.astype(vsT_ref.dtype)
    kw = rope(proj(c0 + 4 * KV_WIDTH, KV_WIDTH)).astype(BF16)
    vwT_ref[...] = proj(c0 + 5 * KV_WIDTH, KV_WIDTH).T.astype(vwT_ref.dtype)
    pad = jnp.zeros((tm, MXU_DEPTH - LANES - HEAD_DIM), BF16)
    for g in range(N_KV_GROUPS):
        kg = ks[:, g * HEAD_DIM:(g + 1) * HEAD_DIM]
        ksa_ref[g] = jnp.concatenate([onehot_ref[...], kg, pad], axis=1)
        kw_ref[g] = kw[:, g * HEAD_DIM:(g + 1) * HEAD_DIM]
    gT_ref[...] = jax.nn.sigmoid(proj(c0 + 6 * KV_WIDTH, N_KV_GROUPS * GATE_PAD)).T


def _inproj_call(x, pos3, mod, gmix, win, convw, gconv, freq, gmat, onehot):
    B, S, D = x.shape
    tm = TOKEN_TILE
    tok = lambda w: pl.BlockSpec((None, tm, w), lambda b, i: (b, i, 0))
    tr = lambda w: pl.BlockSpec((None, w, tm), lambda b, i: (b, 0, i))
    grp = lambda w: pl.BlockSpec((None, N_KV_GROUPS, tm, w), lambda b, i: (b, 0, i, 0))
    out_shapes = (
        jax.ShapeDtypeStruct((B, S, CONV_CH), BF16),
        jax.ShapeDtypeStruct((B, ATTN_WIDTH, S), BF16),
        jax.ShapeDtypeStruct((B, S, KV_WIDTH), F32),
        jax.ShapeDtypeStruct((B, S, KV_WIDTH), F32),
        jax.ShapeDtypeStruct((B, N_KV_GROUPS, S, MXU_DEPTH), BF16),
        jax.ShapeDtypeStruct((B, KV_WIDTH, S), BF16),
        jax.ShapeDtypeStruct((B, N_KV_GROUPS, S, HEAD_DIM), BF16),
        jax.ShapeDtypeStruct((B, KV_WIDTH, S), BF16),
        jax.ShapeDtypeStruct((B, N_KV_GROUPS * GATE_PAD, S), F32),
    )
    return pl.pallas_call(
        _inproj_kernel,
        grid=(B, S // tm),
        in_specs=[tok(D),
                  pl.BlockSpec((None, tm, 1), lambda b, i: (b, i, 0)),
                  pl.BlockSpec((None, N_MOD, D), lambda b, i: (b, 0, 0)),
                  _const_spec(gmix.shape), _const_spec(win.shape), _const_spec(convw.shape),
                  _const_spec(gconv.shape), _const_spec(freq.shape), _const_spec(gmat.shape),
                  pl.BlockSpec((tm, onehot.shape[1]), lambda b, i: (i, 0))],
        out_specs=(tok(CONV_CH), tr(ATTN_WIDTH), tok(KV_WIDTH), tok(KV_WIDTH), grp(MXU_DEPTH),
                   tr(KV_WIDTH), grp(HEAD_DIM), tr(KV_WIDTH), tr(N_KV_GROUPS * GATE_PAD)),
        out_shape=out_shapes,
        scratch_shapes=[pltpu.VMEM((8, CONV_CH), F32)],
        compiler_params=pltpu.CompilerParams(dimension_semantics=("arbitrary", "arbitrary"),
                                             vmem_limit_bytes=VMEM_LIMIT),
        name="mixer_inproj",
    )(x, pos3, mod, gmix, win, convw, gconv, freq, gmat, onehot)


def _compress_kernel(kf_ref, vf_ref, pek_ref, pev_ref, wk1_ref, wk2_ref, wv1_ref, wv2_ref,
                     kcc_ref, vcT_ref):
    n = kf_ref.shape[0]

    def mlp(f, pe_ref, w1_ref, w2_ref):
        a = _dot((f + pe_ref[0:1, :]).astype(BF16), w1_ref[0])
        b = _dot((f + pe_ref[1:2, :]).astype(BF16), w1_ref[1])
        hpre = a + pltpu.roll(b, n - 1, 0)
        hid = (hpre * jax.nn.sigmoid(hpre)).astype(BF16)
        return _dot(hid, w2_ref[...])

    kc = mlp(kf_ref[...], pek_ref, wk1_ref, wk2_ref).astype(kcc_ref.dtype)
    for g in range(N_KV_GROUPS):
        kcc_ref[g] = kc[:, g * HEAD_DIM:(g + 1) * HEAD_DIM]
    vcT_ref[...] = mlp(vf_ref[...], pev_ref, wv1_ref, wv2_ref).T.astype(vcT_ref.dtype)


def _compress_call(kf, vf, pek, pev, wk1, wk2, wv1, wv2):
    B, n, width = kf.shape
    flat = pl.BlockSpec((None, n, width), lambda b: (b, 0, 0))
    return pl.pallas_call(
        _compress_kernel,
        grid=(B,),
        in_specs=[flat, flat, _const_spec(pek.shape), _const_spec(pev.shape),
                  _const_spec(wk1.shape), _const_spec(wk2.shape), _const_spec(wv1.shape),
                  _const_spec(wv2.shape)],
        out_specs=(pl.BlockSpec((None, N_KV_GROUPS, n, HEAD_DIM), lambda b: (b, 0, 0, 0)),
                   pl.BlockSpec((None, KV_WIDTH, n), lambda b: (b, 0, 0))),
        out_shape=(jax.ShapeDtypeStruct((B, N_KV_GROUPS, n, HEAD_DIM), BF16),
                   jax.ShapeDtypeStruct((B, KV_WIDTH, n), BF16)),
        compiler_params=pltpu.CompilerParams(dimension_semantics=("arbitrary",),
                                             vmem_limit_bytes=VMEM_LIMIT),
        name="kv_compress",
    )(kf, vf, pek, pev, wk1, wk2, wv1, wv2)


def _lane_tiles(x, n):
    return jnp.concatenate([x] * n, axis=1)


def _softmax_cols(s):
    m = jnp.maximum(jnp.max(s, axis=0, keepdims=True), M_FLOOR)
    p = jnp.exp2(s - m)
    l = jnp.sum(p, axis=0, keepdims=True)
    return p, 1.0 / jnp.where(l > 0.0, l, 1.0)


def _attn_kernel(top_n, qT_ref, gT_ref, kcc_ref, vcT_ref, ksa_ref, vsT_ref, kw_ref, vwT_ref,
                 ovlT_ref, ga_ref, o_ref, s0_ref, s1_ref, rhs_ref):
    QB = Q_BLOCK
    n_cmp = kcc_ref.shape[0]
    n_slc = ovlT_ref.shape[0]
    t0 = pl.program_id(2) * QB
    tq = t0 + lax.broadcasted_iota(jnp.int32, (1, QB), 1)

    qT = qT_ref[...]
    qcat = jnp.concatenate([qT[h * HEAD_DIM:(h + 1) * HEAD_DIM, :] for h in range(HPG)], axis=1)

    w0 = pl.multiple_of(jnp.maximum(t0 - WINDOW, 0), QB)
    s_cmp = _dot(kcc_ref[...], qcat)
    s_win = _dot(kw_ref[pl.ds(w0, WIN_KEYS), :], qcat)

    cmp_end = lax.broadcasted_iota(jnp.int32, (n_cmp, 1), 0) * CMP_STRIDE + (CMP_BLOCK - 1)
    vbias = jnp.where(cmp_end <= tq, 0.0, NEG)
    p, rl = _softmax_cols(s_cmp + _lane_tiles(vbias, HPG))
    p_cmp = p * rl
    o_cmp = _dot(vcT_ref[...], p_cmp.astype(BF16))

    psum = p_cmp[:, 0:QB]
    for h in range(1, HPG):
        psum = psum + p_cmp[:, h * QB:(h + 1) * QB]
    hi, lo = _split_bf16(psum)
    imp = _dot(ovlT_ref[...], hi) + _dot(ovlT_ref[...], lo)

    dist = tq - (w0 + lax.broadcasted_iota(jnp.int32, (WIN_KEYS, 1), 0))
    wbias = jnp.where((dist >= 0) & (dist < WINDOW), 0.0, NEG)
    p, rl = _softmax_cols(s_win + _lane_tiles(wbias, HPG))
    o_win = _dot(vwT_ref[:, pl.ds(w0, WIN_KEYS)], p.astype(BF16)) * rl

    blk =lax.broadcasted_iota(jnp.int32, (n_slc, 1), 0)
    blk_f = blk.astype(F32)
    cur = tq // SLC_BLOCK
    future = blk > cur
    forced = (blk == 0) | (blk == cur) | (blk == cur - 1)
    score = jnp.where(future, -BIG, jnp.where(forced, BIG, imp))
    picked = jnp.zeros(score.shape, F32)
    for _ in range(top_n):
        best = jnp.max(score, axis=0, keepdims=True)
        first = jnp.min(jnp.where(score == best, blk_f, float(n_slc)), axis=0, keepdims=True)
        hit = blk_f == first
        picked = jnp.where(hit, 1.0, picked)
        score = jnp.where(hit, -jnp.inf, score)
    sel_bias = jnp.where(future, NEG, jnp.where(picked > 0.0, 0.0, NEG)).astype(BF16)
    rhs_ref[...] = jnp.concatenate([_lane_tiles(sel_bias, HPG), qcat,
                                    jnp.zeros((MXU_DEPTH - n_slc - HEAD_DIM, QW), BF16)], axis=0)

    KC = KEY_CHUNK
    last = t0 // KC

    def scores(c, dst_ref):
        k0 = pl.multiple_of(jnp.minimum(c, last) * KC, KC)
        dst_ref[...] = _dot(ksa_ref[pl.ds(k0, KC), :], rhs_ref[...])

    def consume(c, src_ref, carry, causal):
        m, l, acc = carry
        k0 = pl.multiple_of(c * KC, KC)
        s = src_ref[...]
        if causal:
            kpos = k0 + lax.broadcasted_iota(jnp.int32, (KC, 1), 0)
            s = s + _lane_tiles(jnp.where(kpos <= tq, 0.0, NEG), HPG)
        m_new = jnp.maximum(m, jnp.max(s, axis=0, keepdims=True))
        alpha = jnp.exp2(m - m_new)
        p = jnp.exp2(s - m_new)
        l = alpha * l + jnp.sum(p, axis=0, keepdims=True)
        acc = alpha * acc + _dot(vsT_ref[:, pl.ds(k0, KC)], p.astype(BF16))
        return m_new, l, acc

    def pair(i, carry):
        scores(2 * i + 1, s1_ref)
        carry = consume(2 * i, s0_ref, carry, False)
        scores(2 * i + 2, s0_ref)
        return consume(2 * i + 1, s1_ref, carry, True)

    scores(0, s0_ref)
    init = (jnp.full((1, QW), NEG, F32), jnp.zeros((1, QW), F32), jnp.zeros((HEAD_DIM, QW), F32))
    carry = lax.fori_loop(0, (last + 1) // 2, pair, init)
    carry = lax.cond(last % 2 == 0, lambda cr: consume(last, s0_ref, cr, True), lambda cr: cr, carry)
    o_slc = carry[2] * (1.0 / carry[1])

    gts = gT_ref[...]
    outs = []
    for h in range(HPG):
        sl = slice(h * QB, (h + 1) * QB)
        o = (gts[3 * h:3 * h + 1, :] * o_cmp[:, sl] + gts[3 * h + 1:3 * h + 2, :] * o_slc[:, sl]
             + gts[3 * h + 2:3 * h + 3, :] * o_win[:, sl])
        o = o * lax.rsqrt(jnp.mean(o * o, axis=0, keepdims=True) + EPS)
        outs.append((o * ga_ref[h * HEAD_DIM:(h + 1) * HEAD_DIM, :]).T)
    o_ref[...] = jnp.concatenate(outs, axis=1).astype(o_ref.dtype)


def _attn_call(top_n, qT, gatesT, kcc, vcT, ksa, vsT, kw, vwT, ovlT, g_attn_col):
    B, _, S = qT.shape
    n_cmp = kcc.shape[2]
    gw = HPG * HEAD_DIM
    rows = lambda n, w: pl.BlockSpec((None, None, n, w), lambda b, g, i: (b, g, 0, 0))
    cols = lambda n: pl.BlockSpec((None, HEAD_DIM, n), lambda b, g, i: (b, g, 0))
    return pl.pallas_call(
        functools.partial(_attn_kernel, top_n),
        grid=(B, N_KV_GROUPS, S // Q_BLOCK),
        in_specs=[pl.BlockSpec((None, gw, Q_BLOCK), lambda b, g, i: (b, g, i)),
                  pl.BlockSpec((None, GATE_PAD, Q_BLOCK), lambda b, g, i: (b, g, i)),
                  rows(n_cmp, HEAD_DIM), cols(n_cmp), rows(S, MXU_DEPTH), cols(S),
                  rows(S, HEAD_DIM), cols(S),
                  pl.BlockSpec(ovlT.shape, lambda b, g, i: (0, 0)),
                  pl.BlockSpec((gw, 1), lambda b, g, i: (g, 0))],
        out_specs=pl.BlockSpec((None, Q_BLOCK, gw), lambda b, g, i: (b, i, g)),
        out_shape=jax.ShapeDtypeStruct((B, S, ATTN_WIDTH), BF16),
        scratch_shapes=[pltpu.VMEM((KEY_CHUNK, QW), F32), pltpu.VMEM((KEY_CHUNK, QW), F32),
                        pltpu.VMEM((MXU_DEPTH, QW), BF16)],
        compiler_params=pltpu.CompilerParams(
            dimension_semantics=("parallel", "parallel", "arbitrary"),
            vmem_limit_bytes=VMEM_LIMIT),
        name="nsa_attention",
    )(qT, gatesT, kcc, vcT, ksa, vsT, kw, vwT, ovlT, g_attn_col)


def _out_kernel(x_ref, yc_ref, ya_ref, mod_ref, wo_ref, g_ref, wg_ref, wu_ref, wd_ref, gf_ref,
                o_ref):
    mix = _dot(yc_ref[...], wo_ref[0:CONV_CH, :]) + _dot(ya_ref[...], wo_ref[CONV_CH:, :])
    x = x_ref[...] + mod_ref[5:6, :] * mix
    x = _ffn_core(x, mod_ref[6:7, :], mod_ref[7:8, :], mod_ref[8:9, :], g_ref[...], wg_ref,
                  wu_ref, wd_ref)
    o_ref[...] = _rms(x, gf_ref[...])


def _out_call(x, yc, ya, mod, wo, g, wg, wu, wd, gf):
    B, S, D = x.shape
    tm = TOKEN_TILE
    tok = lambda w: pl.BlockSpec((None, tm, w), lambda b, i: (b, i, 0))
    return pl.pallas_call(
        _out_kernel,
        grid=(B, S // tm),
        in_specs=[tok(D), tok(CONV_CH), tok(ATTN_WIDTH),
                  pl.BlockSpec((None, N_MOD, D), lambda b, i: (b, 0, 0)),
                  _const_spec(wo.shape), _const_spec(g.shape), _const_spec(wg.shape),
                  _const_spec(wu.shape), _const_spec(wd.shape), _const_spec(gf.shape)],
        out_specs=tok(D),
        out_shape=jax.ShapeDtypeStruct((B, S, D), F32),
        compiler_params=pltpu.CompilerParams(dimension_semantics=("parallel", "parallel"),
                                             vmem_limit_bytes=VMEM_LIMIT),
        name="outproj_ffn2",
    )(x, yc, ya, mod, wo, g, wg, wu, wd, gf)


def _compress_weights(pos_emb, w1, w2):
    half = CMP_BLOCK // 2
    G = N_KV_GROUPS
    eye = jnp.eye(G, dtype=F32)
    w1r = w1.reshape(2, half, HEAD_DIM, CMP_HIDDEN)
    w1big = jnp.einsum('pldc,gh->plgdhc', w1r, eye).reshape(2, half * G * HEAD_DIM, G * CMP_HIDDEN)
    w2big = jnp.einsum('cd,gh->gchd', w2, eye).reshape(G * CMP_HIDDEN, G * HEAD_DIM)
    pe = jnp.broadcast_to(pos_emb.reshape(2, half, 1, HEAD_DIM), (2, half, G, HEAD_DIM))
    return pe.reshape(2, half * G * HEAD_DIM), w1big.astype(BF16), w2big.astype(BF16)


def kernel(x, c, positions, w_ada, b_ada, g_ffn1, w1_gate, w1_up, w1_down, g_mix, w_in, conv_w, cmp_pos_k, cmp_pos_v, w_cmpk1, w_cmpk2, w_cmpv1, w_cmpv2, g_out_conv, g_out_attn, w_out, g_ffn2, w2_gate, w2_up, w2_down, g_final):
    B, S, D = x.shape
    depth = w_ada.shape[0]
    n_slc = S // SLC_BLOCK
    half = CMP_BLOCK // 2
    n_half = S // half
    assert n_slc <= LANES, "selection-block one-hot is one lane tile wide"

    c_pad = jnp.pad(c, ((0, 8 - B), (0, 0)))
    row = lambda a: a.reshape(1, -1)

    freq_half = jnp.power(ROPE_THETA, -2.0 * jnp.arange(ROT_HALF, dtype=F32) / ROT_DIM)
    freq = jnp.tile(freq_half, LANES // ROT_HALF).reshape(1, LANES)
    gidx = np.arange(CONV_CH) // (CONV_CH // CONV_GROUPS)
    gmat = jnp.asarray((gidx[:, None] == gidx[None, :]) / (CONV_CH // CONV_GROUPS), dtype=BF16)
    c0 = np.arange(n_half) * CMP_STRIDE
    s0 = np.arange(LANES) * SLC_BLOCK
    ovlT = ((c0[None, :] <= s0[:, None] + SLC_BLOCK - 1) & (c0[None, :] + CMP_BLOCK - 1 >= s0[:, None]))
    ovlT = jnp.asarray(ovlT, dtype=BF16)
    onehot = jnp.asarray((np.arange(S) // SLC_BLOCK)[:, None] == np.arange(LANES)[None, :], dtype=BF16)
    pos3 = positions.reshape(B, S, 1)

    for l in range(depth):
        mod = _ada_call(c_pad, w_ada[l], row(b_ada[l]))[:B].reshape(B, N_MOD, D)

        x = _ffn_call(x, mod, row(g_ffn1[l]), w1_gate[l].astype(BF16), w1_up[l].astype(BF16),
                      w1_down[l].astype(BF16))

        n_main = w_in.shape[2] - N_KV_GROUPS * 3 * HPG
        gate_cols = [jnp.pad(w_in[l][:, n_main + g * 3 * HPG:n_main + (g + 1) * 3 * HPG],
                             ((0, 0), (0, GATE_PAD - 3 * HPG))) for g in range(N_KV_GROUPS)]
        win = jnp.concatenate([w_in[l][:, :n_main]] + gate_cols, axis=1).astype(BF16)
        (yc, qT, kc, vc, ksa, vsT, kw, vwT, gatesT) = _inproj_call(
            x, pos3, mod, row(g_mix[l]), win, conv_w[l], row(g_out_conv[l]), freq, gmat, onehot)

        pek, wk1, wk2 = _compress_weights(cmp_pos_k[l], w_cmpk1[l], w_cmpk2[l])
        pev, wv1, wv2 = _compress_weights(cmp_pos_v[l], w_cmpv1[l], w_cmpv2[l])
        kcc, vcT = _compress_call(kc.reshape(B, n_half, half * KV_WIDTH),
                                  vc.reshape(B, n_half, half * KV_WIDTH),
                                  pek, pev, wk1, wk2, wv1, wv2)

        ya = _attn_call(min(SLC_TOP_N, n_slc), qT, gatesT, kcc, vcT, ksa, vsT, kw, vwT, ovlT,
                        g_out_attn[l].reshape(ATTN_WIDTH, 1))

        assert l == depth - 1, "final norm is fused into the last layer's output kernel"
        x = _out_call(x, yc, ya, mod, w_out[l].astype(BF16), row(g_ffn2[l]),
                      w2_gate[l].astype(BF16), w2_up[l].astype(BF16), w2_down[l].astype(BF16),
                      row(g_final))
    return x
```

```python
import functools
import math

import numpy as np
import jax
import jax.numpy as jnp
from jax import lax
from jax.experimental import pallas as pl
from jax.experimental.pallas import tpu as pltpu

F32 = jnp.float32
BF16 = jnp.bfloat16

CONV_CH = 512
CONV_GROUPS = 8
N_HEADS = 8
N_KV_GROUPS = 2
HPG = N_HEADS // N_KV_GROUPS
HEAD_DIM = 64
ATTN_WIDTH = N_HEADS * HEAD_DIM
KV_WIDTH = N_KV_GROUPS * HEAD_DIM
ROPE_THETA = 500000.0
ROT_DIM = HEAD_DIM // 4
ROT_HALF = ROT_DIM // 2
CMP_BLOCK = 32
CMP_STRIDE = 16
CMP_HIDDEN = 256
SLC_BLOCK = 64
SLC_TOP_N = 16
WINDOW = 512
Q_BLOCK = 128
MACARON_W = 0.5
N_MOD = 9
EPS = 1e-6
NEG = -1e30
BIG = 1e9

LANES = 128
MXU_DEPTH = 256
VMEM_LIMIT = 56 * 1024 * 1024

TOKEN_TILE = 512
FF_TILE = 256
KEY_CHUNK = 512
WIN_KEYS = WINDOW + Q_BLOCK
GATE_PAD = LANES
QW = HPG * Q_BLOCK
ONES_ROWS = 16
VT_ROWS = HEAD_DIM + ONES_ROWS
ROW_TILE = 32
Q_SCALE = HEAD_DIM ** -0.5 * math.log2(math.e)
M_FLOOR = -1e20


def _dot(a, b):
    return jnp.dot(a, b, preferred_element_type=F32)


def _rms(x, g):
    return x * lax.rsqrt(jnp.mean(x * x, axis=-1, keepdims=True) + EPS) * g


def _split_bf16(x):
    hi = x.astype(BF16)
    lo = (x - hi.astype(F32)).astype(BF16)
    return hi, lo


def _const_spec(shape):
    nd = len(shape)
    return pl.BlockSpec(shape, lambda *_: (0,) * nd, pipeline_mode=pl.Buffered(1))


def _ada_kernel(c_ref, w_ref, b_ref, o_ref):
    c = c_ref[...]
    c_act = c * jax.nn.sigmoid(c)
    o_ref[...] = jnp.dot(c_act, w_ref[...], preferred_element_type=F32,
                         precision=lax.Precision.HIGHEST) + b_ref[...]


def _ada_call(c_pad, w_ada, b_ada):
    rows, d = c_pad.shape
    n = w_ada.shape[1]
    tn = 1024
    return pl.pallas_call(
        _ada_kernel,
        grid=(n // tn,),
        in_specs=[pl.BlockSpec((rows, d), lambda j: (0, 0)),
                  pl.BlockSpec((d, tn), lambda j: (0, j)),
                  pl.BlockSpec((1, tn), lambda j: (0, j))],
        out_specs=pl.BlockSpec((rows, tn), lambda j: (0, j)),
        out_shape=jax.ShapeDtypeStruct((rows, n), F32),
        compiler_params=pltpu.CompilerParams(dimension_semantics=("arbitrary",),
                                             vmem_limit_bytes=VMEM_LIMIT),
        name="adaln_mod",
    )(c_pad, w_ada, b_ada)


def _ffn_core(x, shift, scale, gate, g, wg_ref, wu_ref, wd_ref):
    h = _rms(x, g) * (1.0 + scale) + shift
    hb = h.astype(BF16)
    d_ff = wg_ref.shape[1]
    acc = None
    for j in range(d_ff // FF_TILE):
        sl = slice(j * FF_TILE, (j + 1) * FF_TILE)
        gg = _dot(hb, wg_ref[:, sl])
        uu = _dot(hb, wu_ref[:, sl])
        a = (gg * jax.nn.sigmoid(gg) * uu).astype(BF16)
        d = _dot(a, wd_ref[sl, :])
        acc = d if acc is None else acc + d
    return x + (MACARON_W * gate) * acc


def _ffn_kernel(x_ref, mod_ref, g_ref, wg_ref, wu_ref, wd_ref, o_ref):
    o_ref[...] = _ffn_core(x_ref[...], mod_ref[0:1, :], mod_ref[1:2, :], mod_ref[2:3, :],
                           g_ref[...], wg_ref, wu_ref, wd_ref)


def _ffn_call(x, mod, g, wg, wu, wd):
    B, S, D = x.shape
    tm = TOKEN_TILE
    return pl.pallas_call(
        _ffn_kernel,
        grid=(B, S // tm),
        in_specs=[pl.BlockSpec((None, tm, D), lambda b, i: (b, i, 0)),
                  pl.BlockSpec((None, N_MOD, D), lambda b, i: (b, 0, 0)),
                  _const_spec(g.shape), _const_spec(wg.shape), _const_spec(wu.shape),
                  _const_spec(wd.shape)],
        out_specs=pl.BlockSpec((None, tm, D), lambda b, i: (b, i, 0)),
        out_shape=jax.ShapeDtypeStruct((B, S, D), F32),
        compiler_params=pltpu.CompilerParams(dimension_semantics=("parallel", "parallel"),
                                             vmem_limit_bytes=VMEM_LIMIT),
        name="ffn1",
    )(x, mod, g, wg, wu, wd)


def _inproj_kernel(x_ref, pos_ref, mod_ref, gmix_ref, win_ref, convw_ref, gconv_ref, freq_ref,
                   gmat_ref, onehot_ref, yc_ref, qT_ref, kc_ref, vc_ref, ksa_ref, vsT_ref, kw_ref,
                   vwT_ref, gT_ref, carry_ref):
    tm = x_ref.shape[0]

    @pl.when(pl.program_id(1) == 0)
    def _():
        carry_ref[...] = jnp.zeros_like(carry_ref)

    x = x_ref[...]
    h = _rms(x, gmix_ref[...]) * (1.0 + mod_ref[4:5, :]) + mod_ref[3:4, :]
    hb = h.astype(BF16)

    def proj(c0, width):
        return _dot(hb, win_ref[:, c0:c0 + width])

    cb = proj(0, CONV_CH)
    u = proj(CONV_CH, CONV_CH) * proj(2 * CONV_CH, CONV_CH)
    row = lax.broadcasted_iota(jnp.int32, (tm, 1), 0)
    prev1 = carry_ref[7:8, :]
    prev2 = carry_ref[6:7, :]
    u1 = jnp.where(row >= 1, pltpu.roll(u, 1, 0), prev1)
    u2 = jnp.where(row >= 2, pltpu.roll(u, 2, 0), jnp.where(row == 1, prev1, prev2))
    carry_ref[...] = u[tm - 8:tm, :]
    v = convw_ref[0:1, :] * u2 + convw_ref[1:2, :] * u1 + convw_ref[2:3, :] * u
    y = cb * v
    hi, lo = _split_bf16(y * y)
    ms = _dot(hi, gmat_ref[...]) + _dot(lo, gmat_ref[...])
    yc_ref[...] = (y * lax.rsqrt(ms + EPS) * gconv_ref[...]).astype(yc_ref.dtype)

    ang = pos_ref[...].astype(F32) * freq_ref[...]
    d = lax.broadcasted_iota(jnp.int32, (1, LANES), 1) & (HEAD_DIM - 1)
    cos_t = jnp.where(d < ROT_DIM, jnp.cos(ang), 1.0)
    sin_raw = jnp.sin(ang)
    sin_t = jnp.where(d < ROT_HALF, -sin_raw, jnp.where(d < ROT_DIM, sin_raw, 0.0))
    first_half = d < ROT_HALF

    def rope(t):
        outs = []
        for j in range(t.shape[1] // LANES):
            tj = t[:, j * LANES:(j + 1) * LANES]
            partner = jnp.where(first_half, pltpu.roll(tj, LANES - ROT_HALF, 1),
                                pltpu.roll(tj, ROT_HALF, 1))
            outs.append(tj * cos_t + partner * sin_t)
        return outs[0] if len(outs) == 1 else jnp.concatenate(outs, axis=1)

    c0 = 3 * CONV_CH
    q = rope(proj(c0, ATTN_WIDTH)) * Q_SCALE
    qT_ref[...] = q.T.astype(qT_ref.dtype)
    c0 += ATTN_WIDTH
    kc_ref[...] = rope(proj(c0, KV_WIDTH))
    vc_ref[...] = proj(c0 + KV_WIDTH, KV_WIDTH)
    ks = rope(proj(c0 + 2 * KV_WIDTH, KV_WIDTH)).astype(BF16)
    vsT = proj(c0 + 3 * KV_WIDTH, KV_WIDTH).T.astype(BF16)
    kw = rope(proj(c0 + 4 * KV_WIDTH, KV_WIDTH)).astype(BF16)
    vwT = proj(c0 + 5 * KV_WIDTH, KV_WIDTH).T.astype(BF16)
    pad = jnp.zeros((tm, MXU_DEPTH - LANES - HEAD_DIM), BF16)
    ones = jnp.ones((ONES_ROWS, tm), BF16)
    for g in range(N_KV_GROUPS):
        kg = ks[:, g * HEAD_DIM:(g + 1) * HEAD_DIM]
        ksa_ref[g] = jnp.concatenate([onehot_ref[...], kg, pad], axis=1)
        kw_ref[g] = kw[:, g * HEAD_DIM:(g + 1) * HEAD_DIM]
        vsT_ref[g] = jnp.concatenate([vsT[g * HEAD_DIM:(g + 1) * HEAD_DIM, :], ones], axis=0)
        vwT_ref[g] = jnp.concatenate([vwT[g * HEAD_DIM:(g + 1) * HEAD_DIM, :], ones], axis=0)
    gT_ref[...] = jax.nn.sigmoid(proj(c0 + 6 * KV_WIDTH, N_KV_GROUPS * GATE_PAD)).T


def _inproj_call(x, pos3, mod, gmix, win, convw, gconv, freq, gmat, onehot):
    B, S, D = x.shape
    tm = TOKEN_TILE
    tok = lambda w: pl.BlockSpec((None, tm, w), lambda b, i: (b, i, 0))
    tr = lambda w: pl.BlockSpec((None, w, tm), lambda b, i: (b, 0, i))
    grp = lambda w: pl.BlockSpec((None, N_KV_GROUPS, tm, w), lambda b, i: (b, 0, i, 0))
    grpT = pl.BlockSpec((None, N_KV_GROUPS, VT_ROWS, tm), lambda b, i: (b, 0, 0, i))
    out_shapes = (
        jax.ShapeDtypeStruct((B, S, CONV_CH), BF16),
        jax.ShapeDtypeStruct((B, ATTN_WIDTH, S), BF16),
        jax.ShapeDtypeStruct((B, S, KV_WIDTH), F32),
        jax.ShapeDtypeStruct((B, S, KV_WIDTH), F32),
        jax.ShapeDtypeStruct((B, N_KV_GROUPS, S, MXU_DEPTH), BF16),
        jax.ShapeDtypeStruct((B, N_KV_GROUPS, VT_ROWS, S), BF16),
        jax.ShapeDtypeStruct((B, N_KV_GROUPS, S, HEAD_DIM), BF16),
        jax.ShapeDtypeStruct((B, N_KV_GROUPS, VT_ROWS, S), BF16),
        jax.ShapeDtypeStruct((B, N_KV_GROUPS * GATE_PAD, S), F32),
    )
    return pl.pallas_call(
        _inproj_kernel,
        grid=(B, S // tm),
        in_specs=[tok(D),
                  pl.BlockSpec((None, tm, 1), lambda b, i: (b, i, 0)),
                  pl.BlockSpec((None, N_MOD, D), lambda b, i: (b, 0, 0)),
                  _const_spec(gmix.shape), _const_spec(win.shape), _const_spec(convw.shape),
                  _const_spec(gconv.shape), _const_spec(freq.shape), _const_spec(gmat.shape),
                  pl.BlockSpec((tm, onehot.shape[1]), lambda b, i: (i, 0))],
        out_specs=(tok(CONV_CH), tr(ATTN_WIDTH), tok(KV_WIDTH), tok(KV_WIDTH), grp(MXU_DEPTH),
                   grpT, grp(HEAD_DIM), grpT, tr(N_KV_GROUPS * GATE_PAD)),
        out_shape=out_shapes,
        scratch_shapes=[pltpu.VMEM((8, CONV_CH), F32)],
        compiler_params=pltpu.CompilerParams(dimension_semantics=("arbitrary", "arbitrary"),
                                             vmem_limit_bytes=VMEM_LIMIT),
        name="mixer_inproj",
    )(x, pos3, mod, gmix, win, convw, gconv, freq, gmat, onehot)


def _compress_kernel(kf_ref, vf_ref, pek_ref, pev_ref, wk1_ref, wk2_ref, wv1_ref, wv2_ref,
                     kcc_ref, vcT_ref):
    n = kf_ref.shape[0]

    def mlp(f, pe_ref, w1_ref, w2_ref):
        a = _dot((f + pe_ref[0:1, :]).astype(BF16), w1_ref[0])
        b = _dot((f + pe_ref[1:2, :]).astype(BF16), w1_ref[1])
        hpre = a + pltpu.roll(b, n - 1, 0)
        hid = (hpre * jax.nn.sigmoid(hpre)).astype(BF16)
        return _dot(hid, w2_ref[...])

    kc = mlp(kf_ref[...], pek_ref, wk1_ref, wk2_ref).astype(kcc_ref.dtype)
    for g in range(N_KV_GROUPS):
        kcc_ref[g] = kc[:, g * HEAD_DIM:(g + 1) * HEAD_DIM]
    vcT = mlp(vf_ref[...], pev_ref, wv1_ref, wv2_ref).T.astype(vcT_ref.dtype)
    ones = jnp.ones((ONES_ROWS, n), vcT_ref.dtype)
    for g in range(N_KV_GROUPS):
        vcT_ref[g] = jnp.concatenate([vcT[g * HEAD_DIM:(g + 1) * HEAD_DIM, :], ones], axis=0)


def _compress_call(kf, vf, pek, pev, wk1, wk2, wv1, wv2):
    B, n, width = kf.shape
    flat = pl.BlockSpec((None, n, width), lambda b: (b, 0, 0))
    return pl.pallas_call(
        _compress_kernel,
        grid=(B,),
        in_specs=[flat, flat, _const_spec(pek.shape), _const_spec(pev.shape),
                  _const_spec(wk1.shape), _const_spec(wk2.shape), _const_spec(wv1.shape),
                  _const_spec(wv2.shape)],
        out_specs=(pl.BlockSpec((None, N_KV_GROUPS, n, HEAD_DIM), lambda b: (b, 0, 0, 0)),
                   pl.BlockSpec((None, N_KV_GROUPS, VT_ROWS, n), lambda b: (b, 0, 0, 0))),
        out_shape=(jax.ShapeDtypeStruct((B, N_KV_GROUPS, n, HEAD_DIM), BF16),
                   jax.ShapeDtypeStruct((B, N_KV_GROUPS, VT_ROWS, n), BF16)),
        compiler_params=pltpu.CompilerParams(dimension_semantics=("arbitrary",),
                                             vmem_limit_bytes=VMEM_LIMIT),
        name="kv_compress",
    )(kf, vf, pek, pev, wk1, wk2, wv1, wv2)


def _lane_tiles(x, n):
    return jnp.concatenate([x] * n, axis=1)


def _col_max(s_ref, n_rows, bias_fn):
    groups = ROW_TILE // 8
    mx = [jnp.full((8, s_ref.shape[1]), NEG, F32)] * groups
    for r in range(0, n_rows, ROW_TILE):
        x = s_ref[r:r + ROW_TILE, :]
        if bias_fn is not None:
            x = x + bias_fn(r)
            s_ref[r:r + ROW_TILE, :] = x
        mx = [jnp.maximum(mx[i], x[8 * i:8 * (i + 1), :]) for i in range(groups)]
    while len(mx) > 1:
        mx = [jnp.maximum(a, b) for a, b in zip(mx[0::2], mx[1::2])]
    return jnp.max(mx[0], axis=0, keepdims=True)


def _col_exp2(s_ref, p_ref, n_rows, m, keep_f32=False):
    for r in range(0, n_rows, ROW_TILE):
        p = jnp.exp2(s_ref[r:r + ROW_TILE, :] - m)
        if keep_f32:
            s_ref[r:r + ROW_TILE, :] = p
        p_ref[r:r + ROW_TILE, :] = p.astype(p_ref.dtype)


def _recip_pos(l):
    return 1.0 / jnp.where(l > 0.0, l, 1.0)


def _attn_kernel(top_n, qT_ref, gT_ref, kcc_ref, vcT_ref, ksa_ref, vsT_ref, kw_ref, vwT_ref,
                 ovlT_ref, ga_ref, o_ref, s0_ref, s1_ref, p0_ref, p1_ref, sc_ref, pc_ref, sw_ref,
                 pw_ref, ph_ref, pl_ref, rhs_ref):
    QB = Q_BLOCK
    n_cmp = kcc_ref.shape[0]
    n_slc = ovlT_ref.shape[0]
    t0 = pl.program_id(2) * QB
    tq = t0 + lax.broadcasted_iota(jnp.int32, (1, QB), 1)
    rows = lax.broadcasted_iota(jnp.int32, (ROW_TILE, 1), 0)

    qT = qT_ref[...]
    qcat = jnp.concatenate([qT[h * HEAD_DIM:(h + 1) * HEAD_DIM, :] for h in range(HPG)], axis=1)

    w0 = pl.multiple_of(jnp.maximum(t0 - WINDOW, 0), QB)
    sc_ref[...] = _dot(kcc_ref[...], qcat)
    sw_ref[...] = _dot(kw_ref[pl.ds(w0, WIN_KEYS), :], qcat)

    def cmp_bias(r):
        cmp_end = (r + rows) * CMP_STRIDE + (CMP_BLOCK - 1)
        return _lane_tiles(jnp.where(cmp_end <= tq, 0.0, NEG), HPG)

    m = jnp.maximum(_col_max(sc_ref, n_cmp, cmp_bias), M_FLOOR)
    _col_exp2(sc_ref, pc_ref, n_cmp, m, keep_f32=True)
    o_cmp = _dot(vcT_ref[...], pc_ref[...])
    rl = _recip_pos(o_cmp[HEAD_DIM:HEAD_DIM + 1, :])
    o_cmp = o_cmp[0:HEAD_DIM, :] * rl

    for r in range(0, n_cmp, ROW_TILE):
        pn = sc_ref[r:r + ROW_TILE, :] * rl
        psum = pn[:, 0:QB]
        for h in range(1, HPG):
            psum = psum + pn[:, h * QB:(h + 1) * QB]
        hi, lo = _split_bf16(psum)
        ph_ref[r:r + ROW_TILE, :] = hi
        pl_ref[r:r + ROW_TILE, :] = lo
    imp = _dot(ovlT_ref[...], ph_ref[...]) + _dot(ovlT_ref[...], pl_ref[...])

    def win_bias(r):
        dist = tq - (w0 + r + rows)
        return _lane_tiles(jnp.where((dist >= 0) & (dist < WINDOW), 0.0, NEG), HPG)

    m = _col_max(sw_ref, WIN_KEYS, win_bias)
    _col_exp2(sw_ref, pw_ref, WIN_KEYS, m)
    o_win = _dot(vwT_ref[:, pl.ds(w0, WIN_KEYS)], pw_ref[...])
    o_win = o_win[0:HEAD_DIM, :] * (1.0 / o_win[HEAD_DIM:HEAD_DIM + 1, :])

    blk = lax.broadcasted_iota(jnp.int32, (n_slc, 1), 0)
    cur = tq // SLC_BLOCK
    future = blk > cur
    forced = (blk == 0) | (blk == cur) | (blk == cur - 1)
    score0 = jnp.where(future, -BIG, jnp.where(forced, BIG, imp))
    score = score0
    cum = jnp.zeros((1, QB), F32)
    thr = jnp.zeros((1, QB), F32)
    above = jnp.zeros((1, QB), F32)
    for _ in range(top_n):
        best = jnp.max(score, axis=0, keepdims=True)
        eq = score == best
        unfilled = cum < top_n
        thr = jnp.where(unfilled, best, thr)
        above = jnp.where(unfilled, cum, above)
        cum = cum + jnp.sum(jnp.where(eq, 1.0, 0.0), axis=0, keepdims=True)
        score = jnp.where(eq, -jnp.inf, score)
    ties = score0 == thr
    lower = jnp.where(lax.broadcasted_iota(jnp.int32, (1, n_slc), 1) < blk, 1.0, 0.0).astype(BF16)
    rank = _dot(lower, jnp.where(ties, 1.0, 0.0).astype(BF16))
    picked = (score0 > thr) | (ties & (rank < top_n - above))
    sel_bias = jnp.where(future, NEG, jnp.where(picked, 0.0, NEG)).astype(BF16)
    rhs_ref[...] = jnp.concatenate([_lane_tiles(sel_bias, HPG), qcat,
                                    jnp.zeros((MXU_DEPTH - n_slc - HEAD_DIM, QW), BF16)], axis=0)

    KC = KEY_CHUNK
    last = t0 // KC

    def scores(c, dst_ref):
        k0 = pl.multiple_of(jnp.minimum(c, last) * KC, KC)
        dst_ref[...] = _dot(ksa_ref[pl.ds(k0, KC), :], rhs_ref[...])

    def weighted_values(c, p_ref):
        k0 = pl.multiple_of(jnp.clip(c, 0, last) * KC, KC)
        return _dot(vsT_ref[:, pl.ds(k0, KC)], p_ref[...])

    def softmax(c, src_ref, p_ref, m, acc, causal):
        def causal_bias(r):
            return _lane_tiles(jnp.where(c * KC + r + rows <= tq, 0.0, NEG), HPG)

        m_new = jnp.maximum(m, _col_max(src_ref, KC, causal_bias if causal else None))
        _col_exp2(src_ref, p_ref, KC, m_new)
        return m_new, jnp.exp2(m - m_new) * acc

    def pair(i, carry):
        m, acc = carry
        pending = weighted_values(2 * i - 1, p1_ref)
        scores(2 * i + 1, s1_ref)
        m, acc = softmax(2 * i, s0_ref, p0_ref, m, acc + pending, False)
        pending = weighted_values(2 * i, p0_ref)
        scores(2 * i + 2, s0_ref)
        return softmax(2 * i + 1, s1_ref, p1_ref, m, acc + pending, True)

    p1_ref[...] = jnp.zeros_like(p1_ref)
    scores(0, s0_ref)
    init = (jnp.full((1, QW), NEG, F32), jnp.zeros((VT_ROWS, QW), F32))
    n_pairs = (last + 1) // 2
    m, acc = lax.fori_loop(0, n_pairs, pair, init)
    acc = acc + weighted_values(2 * n_pairs - 1, p1_ref)

    def diagonal(carry):
        m, acc = softmax(last, s0_ref, p0_ref, carry[0], carry[1], True)
        return m, acc + weighted_values(last, p0_ref)

    _, acc = lax.cond(last % 2 == 0, diagonal, lambda cr: cr, (m, acc))
    o_slc = acc[0:HEAD_DIM, :] * (1.0 / acc[HEAD_DIM:HEAD_DIM + 1, :])

    gts = gT_ref[...]
    outs = []
    for h in range(HPG):
        sl = slice(h * QB, (h + 1) * QB)
        o = (gts[3 * h:3 * h + 1, :] * o_cmp[:, sl] + gts[3 * h + 1:3 * h + 2, :] * o_slc[:, sl]
             + gts[3 * h + 2:3 * h + 3, :] * o_win[:, sl])
        o = o * lax.rsqrt(jnp.mean(o * o, axis=0, keepdims=True) + EPS)
        outs.append((o * ga_ref[h * HEAD_DIM:(h + 1) * HEAD_DIM, :]).T)
    o_ref[...] = jnp.concatenate(outs, axis=1).astype(o_ref.dtype)


def _attn_call(top_n, qT, gatesT, kcc, vcT, ksa, vsT, kw, vwT, ovlT, g_attn_col):
    B, _, S = qT.shape
    n_cmp = kcc.shape[2]
    gw = HPG * HEAD_DIM
    rows = lambda n, w: pl.BlockSpec((None, None, n, w), lambda b, g, i: (b, g, 0, 0))
    cols = lambda n: pl.BlockSpec((None, None, VT_ROWS, n), lambda b, g, i: (b, g, 0, 0))
    return pl.pallas_call(
        functools.partial(_attn_kernel, top_n),
        grid=(B, N_KV_GROUPS, S // Q_BLOCK),
        in_specs=[pl.BlockSpec((None, gw, Q_BLOCK), lambda b, g, i: (b, g, i)),
                  pl.BlockSpec((None, GATE_PAD, Q_BLOCK), lambda b, g, i: (b, g, i)),
                  rows(n_cmp, HEAD_DIM), cols(n_cmp), rows(S, MXU_DEPTH), cols(S),
                  rows(S, HEAD_DIM), cols(S),
                  pl.BlockSpec(ovlT.shape, lambda b, g, i: (0, 0)),
                  pl.BlockSpec((gw, 1), lambda b, g, i: (g, 0))],
        out_specs=pl.BlockSpec((None, Q_BLOCK, gw), lambda b, g, i: (b, i, g)),
        out_shape=jax.ShapeDtypeStruct((B, S, ATTN_WIDTH), BF16),
        scratch_shapes=[pltpu.VMEM((KEY_CHUNK, QW), F32), pltpu.VMEM((KEY_CHUNK, QW), F32),
                        pltpu.VMEM((KEY_CHUNK, QW), BF16), pltpu.VMEM((KEY_CHUNK, QW), BF16),
                        pltpu.VMEM((n_cmp, QW), F32), pltpu.VMEM((n_cmp, QW), BF16),
                        pltpu.VMEM((WIN_KEYS, QW), F32), pltpu.VMEM((WIN_KEYS, QW), BF16),
                        pltpu.VMEM((n_cmp, Q_BLOCK), BF16), pltpu.VMEM((n_cmp, Q_BLOCK), BF16),
                        pltpu.VMEM((MXU_DEPTH, QW), BF16)],
        compiler_params=pltpu.CompilerParams(
            dimension_semantics=("parallel", "parallel", "arbitrary"),
            vmem_limit_bytes=VMEM_LIMIT),
        name="nsa_attention",
    )(qT, gatesT, kcc, vcT, ksa, vsT, kw, vwT, ovlT, g_attn_col)


def _out_kernel(x_ref, yc_ref, ya_ref, mod_ref, wo_ref, g_ref, wg_ref, wu_ref, wd_ref, gf_ref,
                o_ref):
    mix = _dot(yc_ref[...], wo_ref[0:CONV_CH, :]) + _dot(ya_ref[...], wo_ref[CONV_CH:, :])
    x = x_ref[...] + mod_ref[5:6, :] * mix
    x = _ffn_core(x, mod_ref[6:7, :], mod_ref[7:8, :], mod_ref[8:9, :], g_ref[...], wg_ref,
                  wu_ref, wd_ref)
    o_ref[...] = _rms(x, gf_ref[...])


def _out_call(x, yc, ya, mod, wo, g, wg, wu, wd, gf):
    B, S, D = x.shape
    tm = TOKEN_TILE
    tok = lambda w: pl.BlockSpec((None, tm, w), lambda b, i: (b, i, 0))
    return pl.pallas_call(
        _out_kernel,
        grid=(B, S // tm),
        in_specs=[tok(D), tok(CONV_CH), tok(ATTN_WIDTH),
                  pl.BlockSpec((None, N_MOD, D), lambda b, i: (b, 0, 0)),
                  _const_spec(wo.shape), _const_spec(g.shape), _const_spec(wg.shape),
                  _const_spec(wu.shape), _const_spec(wd.shape), _const_spec(gf.shape)],
        out_specs=tok(D),
        out_shape=jax.ShapeDtypeStruct((B, S, D), F32),
        compiler_params=pltpu.CompilerParams(dimension_semantics=("parallel", "parallel"),
                                             vmem_limit_bytes=VMEM_LIMIT),
        name="outproj_ffn2",
    )(x, yc, ya, mod, wo, g, wg, wu, wd, gf)


def _compress_weights(pos_emb, w1, w2):
    half = CMP_BLOCK // 2
    G = N_KV_GROUPS
    eye = jnp.eye(G, dtype=F32)
    w1r = w1.reshape(2, half, HEAD_DIM, CMP_HIDDEN)
    w1big = jnp.einsum('pldc,gh->plgdhc', w1r, eye).reshape(2, half * G * HEAD_DIM, G * CMP_HIDDEN)
    w2big = jnp.einsum('cd,gh->gchd', w2, eye).reshape(G * CMP_HIDDEN, G * HEAD_DIM)
    pe = jnp.broadcast_to(pos_emb.reshape(2, half, 1, HEAD_DIM), (2, half, G, HEAD_DIM))
    return pe.reshape(2, half * G * HEAD_DIM), w1big.astype(BF16), w2big.astype(BF16)


def kernel(x, c, positions, w_ada, b_ada, g_ffn1, w1_gate, w1_up, w1_down, g_mix, w_in, conv_w, cmp_pos_k, cmp_pos_v, w_cmpk1, w_cmpk2, w_cmpv1, w_cmpv2, g_out_conv, g_out_attn, w_out, g_ffn2, w2_gate, w2_up, w2_down, g_final):
    B, S, D = x.shape
    depth = w_ada.shape[0]
    n_slc = S // SLC_BLOCK
    half = CMP_BLOCK // 2
    n_half = S // half
    assert n_slc <= LANES, "selection-block one-hot is one lane tile wide"

    c_pad = jnp.pad(c, ((0, 8 - B), (0, 0)))
    row = lambda a: a.reshape(1, -1)

    freq_half = jnp.power(ROPE_THETA, -2.0 * jnp.arange(ROT_HALF, dtype=F32) / ROT_DIM)
    freq = jnp.tile(freq_half, LANES // ROT_HALF).reshape(1, LANES)
    gidx = np.arange(CONV_CH) // (CONV_CH // CONV_GROUPS)
    gmat = jnp.asarray((gidx[:, None] == gidx[None, :]) / (CONV_CH // CONV_GROUPS), dtype=BF16)
    c0 = np.arange(n_half) * CMP_STRIDE
    s0 = np.arange(LANES) * SLC_BLOCK
    ovlT = ((c0[None, :] <= s0[:, None] + SLC_BLOCK - 1) & (c0[None, :] + CMP_BLOCK - 1 >= s0[:, None]))
    ovlT = jnp.asarray(ovlT, dtype=BF16)
    onehot = jnp.asarray((np.arange(S) // SLC_BLOCK)[:, None] == np.arange(LANES)[None, :], dtype=BF16)
    pos3 = positions.reshape(B, S, 1)

    for l in range(depth):
        mod = _ada_call(c_pad, w_ada[l], row(b_ada[l]))[:B].reshape(B, N_MOD, D)

        x = _ffn_call(x, mod, row(g_ffn1[l]), w1_gate[l].astype(BF16), w1_up[l].astype(BF16),
                      w1_down[l].astype(BF16))

        n_main = w_in.shape[2] - N_KV_GROUPS * 3 * HPG
        gate_cols = [jnp.pad(w_in[l][:, n_main + g * 3 * HPG:n_main + (g + 1) * 3 * HPG],
                             ((0, 0), (0, GATE_PAD - 3 * HPG))) for g in range(N_KV_GROUPS)]
        win = jnp.concatenate([w_in[l][:, :n_main]] + gate_cols, axis=1).astype(BF16)
        (yc, qT, kc, vc, ksa, vsT, kw, vwT, gatesT) = _inproj_call(
            x, pos3, mod, row(g_mix[l]), win, conv_w[l], row(g_out_conv[l]), freq, gmat, onehot)

        pek, wk1, wk2 = _compress_weights(cmp_pos_k[l], w_cmpk1[l], w_cmpk2[l])
        pev, wv1, wv2 = _compress_weights(cmp_pos_v[l], w_cmpv1[l], w_cmpv2[l])
        kcc, vcT = _compress_call(kc.reshape(B, n_half, half * KV_WIDTH),
                                  vc.reshape(B, n_half, half * KV_WIDTH),
                                  pek, pev, wk1, wk2, wv1, wv2)

        ya = _attn_call(min(SLC_TOP_N, n_slc), qT, gatesT, kcc, vcT, ksa, vsT, kw, vwT, ovlT,
                        g_out_attn[l].reshape(ATTN_WIDTH, 1))

        assert l == depth - 1, "final norm is fused into the last layer's output kernel"
        x = _out_call(x, yc, ya, mod, w_out[l].astype(BF16), row(g_ffn2[l]),
                      w2_gate[l].astype(BF16), w2_up[l].astype(BF16), w2_down[l].astype(BF16),
                      row(g_final))
    return x
```

```python
import functools
import math

import numpy as np
import jax
import jax.numpy as jnp
from jax import lax
from jax.experimental import pallas as pl
from jax.experimental.pallas import tpu as pltpu

F32 = jnp.float32
BF16 = jnp.bfloat16

CONV_CH = 512
CONV_GROUPS = 8
N_HEADS = 8
N_KV_GROUPS = 2
HPG = N_HEADS // N_KV_GROUPS
HEAD_DIM = 64
ATTN_WIDTH = N_HEADS * HEAD_DIM
KV_WIDTH = N_KV_GROUPS * HEAD_DIM
ROPE_THETA = 500000.0
ROT_DIM = HEAD_DIM // 4
ROT_HALF = ROT_DIM // 2
CMP_BLOCK = 32
CMP_STRIDE = 16
CMP_HIDDEN = 256
SLC_BLOCK = 64
SLC_TOP_N = 16
WINDOW = 512
Q_BLOCK = 256
MACARON_W = 0.5
N_MOD = 9
EPS = 1e-6
NEG = -1e30
BIG = 1e9

LANES = 128
MXU_DEPTH = 256
VMEM_LIMIT = 56 * 1024 * 1024

TOKEN_TILE = 512
FF_TILE = 256
KEY_CHUNK = 512
WIN_KEYS = WINDOW + Q_BLOCK
GATE_PAD = LANES
QW = HPG * Q_BLOCK
ONES_ROWS = 16
VT_ROWS = HEAD_DIM + ONES_ROWS
ROW_TILE = 16
Q_SCALE = HEAD_DIM ** -0.5 * math.log2(math.e)
M_FLOOR = -1e20


def _dot(a, b):
    return jnp.dot(a, b, preferred_element_type=F32)


def _rms(x, g):
    return x * lax.rsqrt(jnp.mean(x * x, axis=-1, keepdims=True) + EPS) * g


def _split_bf16(x):
    hi = x.astype(BF16)
    lo = (x - hi.astype(F32)).astype(BF16)
    return hi, lo


def _const_spec(shape):
    nd = len(shape)
    return pl.BlockSpec(shape, lambda *_: (0,) * nd, pipeline_mode=pl.Buffered(1))


def _ada_kernel(c_ref, w_ref, b_ref, o_ref):
    c = c_ref[...]
    c_act = c * jax.nn.sigmoid(c)
    o_ref[...] = jnp.dot(c_act, w_ref[...], preferred_element_type=F32,
                         precision=lax.Precision.HIGHEST) + b_ref[...]


def _ada_call(c_pad, w_ada, b_ada):
    rows, d = c_pad.shape
    n = w_ada.shape[1]
    tn = 1024
    return pl.pallas_call(
        _ada_kernel,
        grid=(n // tn,),
        in_specs=[pl.BlockSpec((rows, d), lambda j: (0, 0)),
                  pl.BlockSpec((d, tn), lambda j: (0, j)),
                  pl.BlockSpec((1, tn), lambda j: (0, j))],
        out_specs=pl.BlockSpec((rows, tn), lambda j: (0, j)),
        out_shape=jax.ShapeDtypeStruct((rows, n), F32),
        compiler_params=pltpu.CompilerParams(dimension_semantics=("arbitrary",),
                                             vmem_limit_bytes=VMEM_LIMIT),
        name="adaln_mod",
    )(c_pad, w_ada, b_ada)


def _ffn_core(x, shift, scale, gate, g, wg_ref, wu_ref, wd_ref):
    h = _rms(x, g) * (1.0 + scale) + shift
    hb = h.astype(BF16)
    d_ff = wg_ref.shape[1]
    acc = None
    for j in range(d_ff // FF_TILE):
        sl = slice(j * FF_TILE, (j + 1) * FF_TILE)
        gg = _dot(hb, wg_ref[:, sl])
        uu = _dot(hb, wu_ref[:, sl])
        a = (gg * jax.nn.sigmoid(gg) * uu).astype(BF16)
        d = _dot(a, wd_ref[sl, :])
        acc = d if acc is None else acc + d
    return x + (MACARON_W * gate) * acc


def _ffn_kernel(x_ref, mod_ref, g_ref, wg_ref, wu_ref, wd_ref, o_ref):
    o_ref[...] = _ffn_core(x_ref[...], mod_ref[0:1, :], mod_ref[1:2, :], mod_ref[2:3, :],
                           g_ref[...], wg_ref, wu_ref, wd_ref)


def _ffn_call(x, mod, g, wg, wu, wd):
    B, S, D = x.shape
    tm = TOKEN_TILE
    return pl.pallas_call(
        _ffn_kernel,
        grid=(B, S // tm),
        in_specs=[pl.BlockSpec((None, tm, D), lambda b, i: (b, i, 0)),
                  pl.BlockSpec((None, N_MOD, D), lambda b, i: (b, 0, 0)),
                  _const_spec(g.shape), _const_spec(wg.shape), _const_spec(wu.shape),
                  _const_spec(wd.shape)],
        out_specs=pl.BlockSpec((None, tm, D), lambda b, i: (b, i, 0)),
        out_shape=jax.ShapeDtypeStruct((B, S, D), F32),
        compiler_params=pltpu.CompilerParams(dimension_semantics=("parallel", "parallel"),
                                             vmem_limit_bytes=VMEM_LIMIT),
        name="ffn1",
    )(x, mod, g, wg, wu, wd)


def _inproj_kernel(x_ref, pos_ref, mod_ref, gmix_ref, win_ref, convw_ref, gconv_ref, freq_ref,
                   gmat_ref, onehot_ref, yc_ref, qT_ref, kc_ref, vc_ref, ksa_ref, vsT_ref, kw_ref,
                   vwT_ref, gT_ref, carry_ref):
    tm = x_ref.shape[0]

    @pl.when(pl.program_id(1) == 0)
    def _():
        carry_ref[...] = jnp.zeros_like(carry_ref)

    x = x_ref[...]
    h = _rms(x, gmix_ref[...]) * (1.0 + mod_ref[4:5, :]) + mod_ref[3:4, :]
    hb = h.astype(BF16)

    def proj(c0, width):
        return _dot(hb, win_ref[:, c0:c0 + width])

    cb = proj(0, CONV_CH)
    u = proj(CONV_CH, CONV_CH) * proj(2 * CONV_CH, CONV_CH)
    row = lax.broadcasted_iota(jnp.int32, (tm, 1), 0)
    prev1 = carry_ref[7:8, :]
    prev2 = carry_ref[6:7, :]
    u1 = jnp.where(row >= 1, pltpu.roll(u, 1, 0), prev1)
    u2 = jnp.where(row >= 2, pltpu.roll(u, 2, 0), jnp.where(row == 1, prev1, prev2))
    carry_ref[...] = u[tm - 8:tm, :]
    v = convw_ref[0:1, :] * u2 + convw_ref[1:2, :] * u1 + convw_ref[2:3, :] * u
    y = cb * v
    hi, lo = _split_bf16(y * y)
    ms = _dot(hi, gmat_ref[...]) + _dot(lo, gmat_ref[...])
    yc_ref[...] = (y * lax.rsqrt(ms + EPS) * gconv_ref[...]).astype(yc_ref.dtype)

    ang = pos_ref[...].astype(F32) * freq_ref[...]
    d = lax.broadcasted_iota(jnp.int32, (1, LANES), 1) & (HEAD_DIM - 1)
    cos_t = jnp.where(d < ROT_DIM, jnp.cos(ang), 1.0)
    sin_raw = jnp.sin(ang)
    sin_t = jnp.where(d < ROT_HALF, -sin_raw, jnp.where(d < ROT_DIM, sin_raw, 0.0))
    first_half = d < ROT_HALF

    def rope(t):
        outs = []
        for j in range(t.shape[1] // LANES):
            tj = t[:, j * LANES:(j + 1) * LANES]
            partner = jnp.where(first_half, pltpu.roll(tj, LANES - ROT_HALF, 1),
                                pltpu.roll(tj, ROT_HALF, 1))
            outs.append(tj * cos_t + partner * sin_t)
        return outs[0] if len(outs) == 1 else jnp.concatenate(outs, axis=1)

    c0 = 3 * CONV_CH
    q = rope(proj(c0, ATTN_WIDTH)) * Q_SCALE
    qT_ref[...] = q.T.astype(qT_ref.dtype)
    c0 += ATTN_WIDTH
    kc_ref[...] = rope(proj(c0, KV_WIDTH))
    vc_ref[...] = proj(c0 + KV_WIDTH, KV_WIDTH)
    ks = rope(proj(c0 + 2 * KV_WIDTH, KV_WIDTH)).astype(BF16)
    vsT = proj(c0 + 3 * KV_WIDTH, KV_WIDTH).T.astype(BF16)
    kw = rope(proj(c0 + 4 * KV_WIDTH, KV_WIDTH)).astype(BF16)
    vwT = proj(c0 + 5 * KV_WIDTH, KV_WIDTH).T.astype(BF16)
    pad = jnp.zeros((tm, MXU_DEPTH - LANES - HEAD_DIM), BF16)
    ones = jnp.ones((ONES_ROWS, tm), BF16)
    for g in range(N_KV_GROUPS):
        kg = ks[:, g * HEAD_DIM:(g + 1) * HEAD_DIM]
        ksa_ref[g] = jnp.concatenate([onehot_ref[...], kg, pad], axis=1)
        kw_ref[g] = kw[:, g * HEAD_DIM:(g + 1) * HEAD_DIM]
        vsT_ref[g] = jnp.concatenate([vsT[g * HEAD_DIM:(g + 1) * HEAD_DIM, :], ones], axis=0)
        vwT_ref[g] = jnp.concatenate([vwT[g * HEAD_DIM:(g + 1) * HEAD_DIM, :], ones], axis=0)
    gT_ref[...] = jax.nn.sigmoid(proj(c0 + 6 * KV_WIDTH, N_KV_GROUPS * GATE_PAD)).T


def _inproj_call(x, pos3, mod, gmix, win, convw, gconv, freq, gmat, onehot):
    B, S, D = x.shape
    tm = TOKEN_TILE
    tok = lambda w: pl.BlockSpec((None, tm, w), lambda b, i: (b, i, 0))
    tr = lambda w: pl.BlockSpec((None, w, tm), lambda b, i: (b, 0, i))
    grp = lambda w: pl.BlockSpec((None, N_KV_GROUPS, tm, w), lambda b, i: (b, 0, i, 0))
    grpT = pl.BlockSpec((None, N_KV_GROUPS, VT_ROWS, tm), lambda b, i: (b, 0, 0, i))
    out_shapes = (
        jax.ShapeDtypeStruct((B, S, CONV_CH), BF16),
        jax.ShapeDtypeStruct((B, ATTN_WIDTH, S), BF16),
        jax.ShapeDtypeStruct((B, S, KV_WIDTH), F32),
        jax.ShapeDtypeStruct((B, S, KV_WIDTH), F32),
        jax.ShapeDtypeStruct((B, N_KV_GROUPS, S, MXU_DEPTH), BF16),
        jax.ShapeDtypeStruct((B, N_KV_GROUPS, VT_ROWS, S), BF16),
        jax.ShapeDtypeStruct((B, N_KV_GROUPS, S, HEAD_DIM), BF16),
        jax.ShapeDtypeStruct((B, N_KV_GROUPS, VT_ROWS, S), BF16),
        jax.ShapeDtypeStruct((B, N_KV_GROUPS * GATE_PAD, S), F32),
    )
    return pl.pallas_call(
        _inproj_kernel,
        grid=(B, S // tm),
        in_specs=[tok(D),
                  pl.BlockSpec((None, tm, 1), lambda b, i: (b, i, 0)),
                  pl.BlockSpec((None, N_MOD, D), lambda b, i: (b, 0, 0)),
                  _const_spec(gmix.shape), _const_spec(win.shape), _const_spec(convw.shape),
                  _const_spec(gconv.shape), _const_spec(freq.shape), _const_spec(gmat.shape),
                  pl.BlockSpec((tm, onehot.shape[1]), lambda b, i: (i, 0))],
        out_specs=(tok(CONV_CH), tr(ATTN_WIDTH), tok(KV_WIDTH), tok(KV_WIDTH), grp(MXU_DEPTH),
                   grpT, grp(HEAD_DIM), grpT, tr(N_KV_GROUPS * GATE_PAD)),
        out_shape=out_shapes,
        scratch_shapes=[pltpu.VMEM((8, CONV_CH), F32)],
        compiler_params=pltpu.CompilerParams(dimension_semantics=("arbitrary", "arbitrary"),
                                             vmem_limit_bytes=VMEM_LIMIT),
        name="mixer_inproj",
    )(x, pos3, mod, gmix, win, convw, gconv, freq, gmat, onehot)


def _compress_kernel(kf_ref, vf_ref, pek_ref, pev_ref, wk1_ref, wk2_ref, wv1_ref, wv2_ref,
                     kcc_ref, vcT_ref):
    n = kf_ref.shape[0]

    def mlp(f, pe_ref, w1_ref, w2_ref):
        a = _dot((f + pe_ref[0:1, :]).astype(BF16), w1_ref[0])
        b = _dot((f + pe_ref[1:2, :]).astype(BF16), w1_ref[1])
        hpre = a + pltpu.roll(b, n - 1, 0)
        hid = (hpre * jax.nn.sigmoid(hpre)).astype(BF16)
        return _dot(hid, w2_ref[...])

    kc = mlp(kf_ref[...], pek_ref, wk1_ref, wk2_ref).astype(kcc_ref.dtype)
    for g in range(N_KV_GROUPS):
        kcc_ref[g] = kc[:, g * HEAD_DIM:(g + 1) * HEAD_DIM]
    vcT = mlp(vf_ref[...], pev_ref, wv1_ref, wv2_ref).T.astype(vcT_ref.dtype)
    ones = jnp.ones((ONES_ROWS, n), vcT_ref.dtype)
    for g in range(N_KV_GROUPS):
        vcT_ref[g] = jnp.concatenate([vcT[g * HEAD_DIM:(g + 1) * HEAD_DIM, :], ones], axis=0)


def _compress_call(kf, vf, pek, pev, wk1, wk2, wv1, wv2):
    B, n, width = kf.shape
    flat = pl.BlockSpec((None, n, width), lambda b: (b, 0, 0))
    return pl.pallas_call(
        _compress_kernel,
        grid=(B,),
        in_specs=[flat, flat, _const_spec(pek.shape), _const_spec(pev.shape),
                  _const_spec(wk1.shape), _const_spec(wk2.shape), _const_spec(wv1.shape),
                  _const_spec(wv2.shape)],
        out_specs=(pl.BlockSpec((None, N_KV_GROUPS, n, HEAD_DIM), lambda b: (b, 0, 0, 0)),
                   pl.BlockSpec((None, N_KV_GROUPS, VT_ROWS, n), lambda b: (b, 0, 0, 0))),
        out_shape=(jax.ShapeDtypeStruct((B, N_KV_GROUPS, n, HEAD_DIM), BF16),
                   jax.ShapeDtypeStruct((B, N_KV_GROUPS, VT_ROWS, n), BF16)),
        compiler_params=pltpu.CompilerParams(dimension_semantics=("arbitrary",),
                                             vmem_limit_bytes=VMEM_LIMIT),
        name="kv_compress",
    )(kf, vf, pek, pev, wk1, wk2, wv1, wv2)


def _lane_tiles(x, n):
    return jnp.concatenate([x] * n, axis=1)


def _col_max(s_ref, n_rows, bias_fn):
    groups = ROW_TILE // 8
    mx = [jnp.full((8, s_ref.shape[1]), NEG, F32)] * groups
    for r in range(0, n_rows, ROW_TILE):
        x = s_ref[r:r + ROW_TILE, :]
        if bias_fn is not None:
            x = x + bias_fn(r)
            s_ref[r:r + ROW_TILE, :] = x
        mx = [jnp.maximum(mx[i], x[8 * i:8 * (i + 1), :]) for i in range(groups)]
    while len(mx) > 1:
        mx = [jnp.maximum(a, b) for a, b in zip(mx[0::2], mx[1::2])]
    return jnp.max(mx[0], axis=0, keepdims=True)


def _col_exp2(s_ref, p_ref, n_rows, m, keep_f32=False):
    for r in range(0, n_rows, ROW_TILE):
        p = jnp.exp2(s_ref[r:r + ROW_TILE, :] - m)
        if keep_f32:
            s_ref[r:r + ROW_TILE, :] = p
        p_ref[r:r + ROW_TILE, :] = p.astype(p_ref.dtype)


def _recip_pos(l):
    return 1.0 / jnp.where(l > 0.0, l, 1.0)


def _attn_kernel(top_n, qT_ref, gT_ref, kcc_ref, vcT_ref, ksa_ref, vsT_ref, kw_ref, vwT_ref,
                 ovlT_ref, ga_ref, o_ref, s0_ref, s1_ref, p0_ref, p1_ref, sc_ref, pc_ref, sw_ref,
                 pw_ref, ph_ref, pl_ref, rhs_ref):
    QB = Q_BLOCK
    n_cmp = kcc_ref.shape[0]
    n_slc = ovlT_ref.shape[0]
    t0 = pl.program_id(2) * QB
    tq = t0 + lax.broadcasted_iota(jnp.int32, (1, QB), 1)
    rows = lax.broadcasted_iota(jnp.int32, (ROW_TILE, 1), 0)

    qT = qT_ref[...]
    qcat = jnp.concatenate([qT[h * HEAD_DIM:(h + 1) * HEAD_DIM, :] for h in range(HPG)], axis=1)

    w0 = pl.multiple_of(jnp.maximum(t0 - WINDOW, 0), QB)
    sc_ref[...] = _dot(kcc_ref[...], qcat)
    sw_ref[...] = _dot(kw_ref[pl.ds(w0, WIN_KEYS), :], qcat)

    def cmp_bias(r):
        cmp_end = (r + rows) * CMP_STRIDE + (CMP_BLOCK - 1)
        return _lane_tiles(jnp.where(cmp_end <= tq, 0.0, NEG), HPG)

    m = jnp.maximum(_col_max(sc_ref, n_cmp, cmp_bias), M_FLOOR)
    _col_exp2(sc_ref, pc_ref, n_cmp, m, keep_f32=True)
    o_cmp = _dot(vcT_ref[...], pc_ref[...])
    rl = _recip_pos(o_cmp[HEAD_DIM:HEAD_DIM + 1, :])
    o_cmp = o_cmp[0:HEAD_DIM, :] * rl

    for r in range(0, n_cmp, ROW_TILE):
        pn = sc_ref[r:r + ROW_TILE, :] * rl
        psum = pn[:, 0:QB]
        for h in range(1, HPG):
            psum = psum + pn[:, h * QB:(h + 1) * QB]
        hi, lo = _split_bf16(psum)
        ph_ref[r:r + ROW_TILE, :] = hi
        pl_ref[r:r + ROW_TILE, :] = lo
    imp = _dot(ovlT_ref[...], ph_ref[...]) + _dot(ovlT_ref[...], pl_ref[...])

    def win_bias(r):
        dist = tq - (w0 + r + rows)
        return _lane_tiles(jnp.where((dist >= 0) & (dist < WINDOW), 0.0, NEG), HPG)

    m = _col_max(sw_ref, WIN_KEYS, win_bias)
    _col_exp2(sw_ref, pw_ref, WIN_KEYS, m)
    o_win = _dot(vwT_ref[:, pl.ds(w0, WIN_KEYS)], pw_ref[...])
    o_win = o_win[0:HEAD_DIM, :] * (1.0 / o_win[HEAD_DIM:HEAD_DIM + 1, :])

    blk = lax.broadcasted_iota(jnp.int32, (n_slc, 1), 0)
    cur = tq // SLC_BLOCK
    future = blk > cur
    forced = (blk == 0) | (blk == cur) | (blk == cur - 1)
    score0 = jnp.where(future, -BIG, jnp.where(forced, BIG, imp))
    score = score0
    cum = jnp.zeros((1, QB), F32)
    thr = jnp.zeros((1, QB), F32)
    above = jnp.zeros((1, QB), F32)
    for _ in range(top_n):
        best = jnp.max(score, axis=0, keepdims=True)
        eq = score == best
        unfilled = cum < top_n
        thr = jnp.where(unfilled, best, thr)
        above = jnp.where(unfilled, cum, above)
        cum = cum + jnp.sum(jnp.where(eq, 1.0, 0.0), axis=0, keepdims=True)
        score = jnp.where(eq, -jnp.inf, score)
    ties = score0 == thr
    lower = jnp.where(lax.broadcasted_iota(jnp.int32, (1, n_slc), 1) < blk, 1.0, 0.0).astype(BF16)
    rank = _dot(lower, jnp.where(ties, 1.0, 0.0).astype(BF16))
    picked = (score0 > thr) | (ties & (rank < top_n - above))
    sel_bias = jnp.where(future, NEG, jnp.where(picked, 0.0, NEG)).astype(BF16)
    rhs_ref[...] = jnp.concatenate([_lane_tiles(sel_bias, HPG), qcat,
                                    jnp.zeros((MXU_DEPTH - n_slc - HEAD_DIM, QW), BF16)], axis=0)

    KC = KEY_CHUNK
    last = t0 // KC

    def scores(c, dst_ref):
        k0 = pl.multiple_of(jnp.minimum(c, last) * KC, KC)
        dst_ref[...] = _dot(ksa_ref[pl.ds(k0, KC), :], rhs_ref[...])

    def weighted_values(c, p_ref):
        k0 = pl.multiple_of(jnp.clip(c, 0, last) * KC, KC)
        return _dot(vsT_ref[:, pl.ds(k0, KC)], p_ref[...])

    def softmax(c, src_ref, p_ref, m, acc, causal):
        def causal_bias(r):
            return _lane_tiles(jnp.where(c * KC + r + rows <= tq, 0.0, NEG), HPG)

        m_new = jnp.maximum(m, _col_max(src_ref, KC, causal_bias if causal else None))
        _col_exp2(src_ref, p_ref, KC, m_new)
        return m_new, jnp.exp2(m - m_new) * acc

    def pair(i, carry):
        m, acc = carry
        pending = weighted_values(2 * i - 1, p1_ref)
        scores(2 * i + 1, s1_ref)
        m, acc = softmax(2 * i, s0_ref, p0_ref, m, acc + pending, False)
        pending = weighted_values(2 * i, p0_ref)
        scores(2 * i + 2, s0_ref)
        return softmax(2 * i + 1, s1_ref, p1_ref, m, acc + pending, True)

    p1_ref[...] = jnp.zeros_like(p1_ref)
    scores(0, s0_ref)
    init = (jnp.full((1, QW), NEG, F32), jnp.zeros((VT_ROWS, QW), F32))
    n_pairs = (last + 1) // 2
    m, acc = lax.fori_loop(0, n_pairs, pair, init)
    acc = acc + weighted_values(2 * n_pairs - 1, p1_ref)

    def diagonal(carry):
        m, acc = softmax(last, s0_ref, p0_ref, carry[0], carry[1], True)
        return m, acc + weighted_values(last, p0_ref)

    _, acc = lax.cond(last % 2 == 0, diagonal, lambda cr: cr, (m, acc))
    o_slc = acc[0:HEAD_DIM, :] * (1.0 / acc[HEAD_DIM:HEAD_DIM + 1, :])

    gts = gT_ref[...]
    outs = []
    for h in range(HPG):
        sl = slice(h * QB, (h + 1) * QB)
        o = (gts[3 * h:3 * h + 1, :] * o_cmp[:, sl] + gts[3 * h + 1:3 * h + 2, :] * o_slc[:, sl]
             + gts[3 * h + 2:3 * h + 3, :] * o_win[:, sl])
        o = o * lax.rsqrt(jnp.mean(o * o, axis=0, keepdims=True) + EPS)
        outs.append((o * ga_ref[h * HEAD_DIM:(h + 1) * HEAD_DIM, :]).T)
    o_ref[...] = jnp.concatenate(outs, axis=1).astype(o_ref.dtype)


def _attn_call(top_n, qT, gatesT, kcc, vcT, ksa, vsT, kw, vwT, ovlT, g_attn_col):
    B, _, S = qT.shape
    n_cmp = kcc.shape[2]
    gw = HPG * HEAD_DIM
    rows = lambda n, w: pl.BlockSpec((None, None, n, w), lambda b, g, i: (b, g, 0, 0))
    cols = lambda n: pl.BlockSpec((None, None, VT_ROWS, n), lambda b, g, i: (b, g, 0, 0))
    return pl.pallas_call(
        functools.partial(_attn_kernel, top_n),
        grid=(B, N_KV_GROUPS, S // Q_BLOCK),
        in_specs=[pl.BlockSpec((None, gw, Q_BLOCK), lambda b, g, i: (b, g, i)),
                  pl.BlockSpec((None, GATE_PAD, Q_BLOCK), lambda b, g, i: (b, g, i)),
                  rows(n_cmp, HEAD_DIM), cols(n_cmp), rows(S, MXU_DEPTH), cols(S),
                  rows(S, HEAD_DIM), cols(S),
                  pl.BlockSpec(ovlT.shape, lambda b, g, i: (0, 0)),
                  pl.BlockSpec((gw, 1), lambda b, g, i: (g, 0))],
        out_specs=pl.BlockSpec((None, Q_BLOCK, gw), lambda b, g, i: (b, i, g)),
        out_shape=jax.ShapeDtypeStruct((B, S, ATTN_WIDTH), BF16),
        scratch_shapes=[pltpu.VMEM((KEY_CHUNK, QW), F32), pltpu.VMEM((KEY_CHUNK, QW), F32),
                        pltpu.VMEM((KEY_CHUNK, QW), BF16), pltpu.VMEM((KEY_CHUNK, QW), BF16),
                        pltpu.VMEM((n_cmp, QW), F32), pltpu.VMEM((n_cmp, QW), BF16),
                        pltpu.VMEM((WIN_KEYS, QW), F32), pltpu.VMEM((WIN_KEYS, QW), BF16),
                        pltpu.VMEM((n_cmp, Q_BLOCK), BF16), pltpu.VMEM((n_cmp, Q_BLOCK), BF16),
                        pltpu.VMEM((MXU_DEPTH, QW), BF16)],
        compiler_params=pltpu.CompilerParams(
            dimension_semantics=("parallel", "parallel", "arbitrary"),
            vmem_limit_bytes=VMEM_LIMIT),
        name="nsa_attention",
    )(qT, gatesT, kcc, vcT, ksa, vsT, kw, vwT, ovlT, g_attn_col)


def _out_kernel(x_ref, yc_ref, ya_ref, mod_ref, wo_ref, g_ref, wg_ref, wu_ref, wd_ref, gf_ref,
                o_ref):
    mix = _dot(yc_ref[...], wo_ref[0:CONV_CH, :]) + _dot(ya_ref[...], wo_ref[CONV_CH:, :])
    x = x_ref[...] + mod_ref[5:6, :] * mix
    x = _ffn_core(x, mod_ref[6:7, :], mod_ref[7:8, :], mod_ref[8:9, :], g_ref[...], wg_ref,
                  wu_ref, wd_ref)
    o_ref[...] = _rms(x, gf_ref[...])


def _out_call(x, yc, ya, mod, wo, g, wg, wu, wd, gf):
    B, S, D = x.shape
    tm = TOKEN_TILE
    tok = lambda w: pl.BlockSpec((None, tm, w), lambda b, i: (b, i, 0))
    return pl.pallas_call(
        _out_kernel,
        grid=(B, S // tm),
        in_specs=[tok(D), tok(CONV_CH), tok(ATTN_WIDTH),
                  pl.BlockSpec((None, N_MOD, D), lambda b, i: (b, 0, 0)),
                  _const_spec(wo.shape), _const_spec(g.shape), _const_spec(wg.shape),
                  _const_spec(wu.shape), _const_spec(wd.shape), _const_spec(gf.shape)],
        out_specs=tok(D),
        out_shape=jax.ShapeDtypeStruct((B, S, D), F32),
        compiler_params=pltpu.CompilerParams(dimension_semantics=("parallel", "parallel"),
                                             vmem_limit_bytes=VMEM_LIMIT),
        name="outproj_ffn2",
    )(x, yc, ya, mod, wo, g, wg, wu, wd, gf)


def _compress_weights(pos_emb, w1, w2):
    half = CMP_BLOCK // 2
    G = N_KV_GROUPS
    eye = jnp.eye(G, dtype=F32)
    w1r = w1.reshape(2, half, HEAD_DIM, CMP_HIDDEN)
    w1big = jnp.einsum('pldc,gh->plgdhc', w1r, eye).reshape(2, half * G * HEAD_DIM, G * CMP_HIDDEN)
    w2big = jnp.einsum('cd,gh->gchd', w2, eye).reshape(G * CMP_HIDDEN, G * HEAD_DIM)
    pe = jnp.broadcast_to(pos_emb.reshape(2, half, 1, HEAD_DIM), (2, half, G, HEAD_DIM))
    return pe.reshape(2, half * G * HEAD_DIM), w1big.astype(BF16), w2big.astype(BF16)


def kernel(x, c, positions, w_ada, b_ada, g_ffn1, w1_gate, w1_up, w1_down, g_mix, w_in, conv_w, cmp_pos_k, cmp_pos_v, w_cmpk1, w_cmpk2, w_cmpv1, w_cmpv2, g_out_conv, g_out_attn, w_out, g_ffn2, w2_gate, w2_up, w2_down, g_final):
    B, S, D = x.shape
    depth = w_ada.shape[0]
    n_slc = S // SLC_BLOCK
    half = CMP_BLOCK // 2
    n_half = S // half
    assert n_slc <= LANES, "selection-block one-hot is one lane tile wide"

    c_pad = jnp.pad(c, ((0, 8 - B), (0, 0)))
    row = lambda a: a.reshape(1, -1)

    freq_half = jnp.power(ROPE_THETA, -2.0 * jnp.arange(ROT_HALF, dtype=F32) / ROT_DIM)
    freq = jnp.tile(freq_half, LANES // ROT_HALF).reshape(1, LANES)
    gidx = np.arange(CONV_CH) // (CONV_CH // CONV_GROUPS)
    gmat = jnp.asarray((gidx[:, None] == gidx[None, :]) / (CONV_CH // CONV_GROUPS), dtype=BF16)
    c0 = np.arange(n_half) * CMP_STRIDE
    s0 = np.arange(LANES) * SLC_BLOCK
    ovlT = ((c0[None, :] <= s0[:, None] + SLC_BLOCK - 1) & (c0[None, :] + CMP_BLOCK - 1 >= s0[:, None]))
    ovlT = jnp.asarray(ovlT, dtype=BF16)
    onehot = jnp.asarray((np.arange(S) // SLC_BLOCK)[:, None] == np.arange(LANES)[None, :], dtype=BF16)
    pos3 = positions.reshape(B, S, 1)

    for l in range(depth):
        mod = _ada_call(c_pad, w_ada[l], row(b_ada[l]))[:B].reshape(B, N_MOD, D)

        x = _ffn_call(x, mod, row(g_ffn1[l]), w1_gate[l].astype(BF16), w1_up[l].astype(BF16),
                      w1_down[l].astype(BF16))

        n_main = w_in.shape[2] - N_KV_GROUPS * 3 * HPG
        gate_cols = [jnp.pad(w_in[l][:, n_main + g * 3 * HPG:n_main + (g + 1) * 3 * HPG],
                             ((0, 0), (0, GATE_PAD - 3 * HPG))) for g in range(N_KV_GROUPS)]
        win = jnp.concatenate([w_in[l][:, :n_main]] + gate_cols, axis=1).astype(BF16)
        (yc, qT, kc, vc, ksa, vsT, kw, vwT, gatesT) = _inproj_call(
            x, pos3, mod, row(g_mix[l]), win, conv_w[l], row(g_out_conv[l]), freq, gmat, onehot)

        pek, wk1, wk2 = _compress_weights(cmp_pos_k[l], w_cmpk1[l], w_cmpk2[l])
        pev, wv1, wv2 = _compress_weights(cmp_pos_v[l], w_cmpv1[l], w_cmpv2[l])
        kcc, vcT = _compress_call(kc.reshape(B, n_half, half * KV_WIDTH),
                                  vc.reshape(B, n_half, half * KV_WIDTH),
                                  pek, pev, wk1, wk2, wv1, wv2)

        ya = _attn_call(min(SLC_TOP_N, n_slc), qT, gatesT, kcc, vcT, ksa, vsT, kw, vwT, ovlT,
                        g_out_attn[l].reshape(ATTN_WIDTH, 1))

        assert l == depth - 1, "final norm is fused into the last layer's output kernel"
        x = _out_call(x, yc, ya, mod, w_out[l].astype(BF16), row(g_ffn2[l]),
                      w2_gate[l].astype(BF16), w2_up[l].astype(BF16), w2_down[l].astype(BF16),
                      row(g_final))
    return x
```

```python
import functools
import math

import numpy as np
import jax
import jax.numpy as jnp
from jax import lax
from jax.experimental import pallas as pl
from jax.experimental.pallas import tpu as pltpu

F32 = jnp.float32
BF16 = jnp.bfloat16

CONV_CH = 512
CONV_GROUPS = 8
N_HEADS = 8
N_KV_GROUPS = 2
HPG = N_HEADS // N_KV_GROUPS
HEAD_DIM = 64
ATTN_WIDTH = N_HEADS * HEAD_DIM
KV_WIDTH = N_KV_GROUPS * HEAD_DIM
ROPE_THETA = 500000.0
ROT_DIM = HEAD_DIM // 4
ROT_HALF = ROT_DIM // 2
CMP_BLOCK = 32
CMP_STRIDE = 16
CMP_HIDDEN = 256
SLC_BLOCK = 64
SLC_TOP_N = 16
WINDOW = 512
Q_BLOCK = 256
MACARON_W = 0.5
N_MOD = 9
EPS = 1e-6
NEG = -1e30
BIG = 1e9

LANES = 128
MXU_DEPTH = 256
VMEM_LIMIT = 56 * 1024 * 1024

TOKEN_TILE = 512
FF_TILE = 256
KEY_CHUNK = 256
WIN_KEYS = WINDOW + Q_BLOCK
GATE_PAD = LANES
QW = HPG * Q_BLOCK
ONES_ROWS = 16
VT_ROWS = HEAD_DIM + ONES_ROWS
ROW_TILE = 16
Q_SCALE = HEAD_DIM ** -0.5 * math.log2(math.e)
M_FLOOR = -1e20


def _dot(a, b):
    return jnp.dot(a, b, preferred_element_type=F32)


def _rms(x, g):
    return x * lax.rsqrt(jnp.mean(x * x, axis=-1, keepdims=True) + EPS) * g


def _split_bf16(x):
    hi = x.astype(BF16)
    lo = (x - hi.astype(F32)).astype(BF16)
    return hi, lo


def _const_spec(shape):
    nd = len(shape)
    return pl.BlockSpec(shape, lambda *_: (0,) * nd, pipeline_mode=pl.Buffered(1))


def _ada_kernel(c_ref, w_ref, b_ref, o_ref):
    c = c_ref[...]
    c_act = c * jax.nn.sigmoid(c)
    o_ref[...] = jnp.dot(c_act, w_ref[...], preferred_element_type=F32,
                         precision=lax.Precision.HIGHEST) + b_ref[...]


def _ada_call(c_pad, w_ada, b_ada):
    rows, d = c_pad.shape
    n = w_ada.shape[1]
    tn = 1024
    return pl.pallas_call(
        _ada_kernel,
        grid=(n // tn,),
        in_specs=[pl.BlockSpec((rows, d), lambda j: (0, 0)),
                  pl.BlockSpec((d, tn), lambda j: (0, j)),
                  pl.BlockSpec((1, tn), lambda j: (0, j))],
        out_specs=pl.BlockSpec((rows, tn), lambda j: (0, j)),
        out_shape=jax.ShapeDtypeStruct((rows, n), F32),
        compiler_params=pltpu.CompilerParams(dimension_semantics=("arbitrary",),
                                             vmem_limit_bytes=VMEM_LIMIT),
        name="adaln_mod",
    )(c_pad, w_ada, b_ada)


def _ffn_core(x, shift, scale, gate, g, wg_ref, wu_ref, wd_ref):
    h = _rms(x, g) * (1.0 + scale) + shift
    hb = h.astype(BF16)
    d_ff = wg_ref.shape[1]
    acc = None
    for j in range(d_ff // FF_TILE):
        sl = slice(j * FF_TILE, (j + 1) * FF_TILE)
        gg = _dot(hb, wg_ref[:, sl])
        uu = _dot(hb, wu_ref[:, sl])
        a = (gg * jax.nn.sigmoid(gg) * uu).astype(BF16)
        d = _dot(a, wd_ref[sl, :])
        acc = d if acc is None else acc + d
    return x + (MACARON_W * gate) * acc


def _ffn_kernel(x_ref, mod_ref, g_ref, wg_ref, wu_ref, wd_ref, o_ref):
    o_ref[...] = _ffn_core(x_ref[...], mod_ref[0:1, :], mod_ref[1:2, :], mod_ref[2:3, :],
                           g_ref[...], wg_ref, wu_ref, wd_ref)


def _ffn_call(x, mod, g, wg, wu, wd):
    B, S, D = x.shape
    tm = TOKEN_TILE
    return pl.pallas_call(
        _ffn_kernel,
        grid=(B, S // tm),
        in_specs=[pl.BlockSpec((None, tm, D), lambda b, i: (b, i, 0)),
                  pl.BlockSpec((None, N_MOD, D), lambda b, i: (b, 0, 0)),
                  _const_spec(g.shape), _const_spec(wg.shape), _const_spec(wu.shape),
                  _const_spec(wd.shape)],
        out_specs=pl.BlockSpec((None, tm, D), lambda b, i: (b, i, 0)),
        out_shape=jax.ShapeDtypeStruct((B, S, D), F32),
        compiler_params=pltpu.CompilerParams(dimension_semantics=("parallel", "parallel"),
                                             vmem_limit_bytes=VMEM_LIMIT),
        name="ffn1",
    )(x, mod, g, wg, wu, wd)


def _inproj_kernel(x_ref, pos_ref, mod_ref, gmix_ref, win_ref, convw_ref, gconv_ref, freq_ref,
                   gmat_ref, onehot_ref, yc_ref, qT_ref, kc_ref, vc_ref, ksa_ref, vsT_ref, kw_ref,
                   vwT_ref, gT_ref, carry_ref):
    tm = x_ref.shape[0]

    @pl.when(pl.program_id(1) == 0)
    def _():
        carry_ref[...] = jnp.zeros_like(carry_ref)

    x = x_ref[...]
    h = _rms(x, gmix_ref[...]) * (1.0 + mod_ref[4:5, :]) + mod_ref[3:4, :]
    hb = h.astype(BF16)

    def proj(c0, width):
        return _dot(hb, win_ref[:, c0:c0 + width])

    cb = proj(0, CONV_CH)
    u = proj(CONV_CH, CONV_CH) * proj(2 * CONV_CH, CONV_CH)
    row = lax.broadcasted_iota(jnp.int32, (tm, 1), 0)
    prev1 = carry_ref[7:8, :]
    prev2 = carry_ref[6:7, :]
    u1 = jnp.where(row >= 1, pltpu.roll(u, 1, 0), prev1)
    u2 = jnp.where(row >= 2, pltpu.roll(u, 2, 0), jnp.where(row == 1, prev1, prev2))
    carry_ref[...] = u[tm - 8:tm, :]
    v = convw_ref[0:1, :] * u2 + convw_ref[1:2, :] * u1 + convw_ref[2:3, :] * u
    y = cb * v
    hi, lo = _split_bf16(y * y)
    ms = _dot(hi, gmat_ref[...]) + _dot(lo, gmat_ref[...])
    yc_ref[...] = (y * lax.rsqrt(ms + EPS) * gconv_ref[...]).astype(yc_ref.dtype)

    ang = pos_ref[...].astype(F32) * freq_ref[...]
    d = lax.broadcasted_iota(jnp.int32, (1, LANES), 1) & (HEAD_DIM - 1)
    cos_t = jnp.where(d < ROT_DIM, jnp.cos(ang), 1.0)
    sin_raw = jnp.sin(ang)
    sin_t = jnp.where(d < ROT_HALF, -sin_raw, jnp.where(d < ROT_DIM, sin_raw, 0.0))
    first_half = d < ROT_HALF

    def rope(t):
        outs = []
        for j in range(t.shape[1] // LANES):
            tj = t[:, j * LANES:(j + 1) * LANES]
            partner = jnp.where(first_half, pltpu.roll(tj, LANES - ROT_HALF, 1),
                                pltpu.roll(tj, ROT_HALF, 1))
            outs.append(tj * cos_t + partner * sin_t)
        return outs[0] if len(outs) == 1 else jnp.concatenate(outs, axis=1)

    c0 = 3 * CONV_CH
    q = rope(proj(c0, ATTN_WIDTH)) * Q_SCALE
    qT_ref[...] = q.T.astype(qT_ref.dtype)
    c0 += ATTN_WIDTH
    kc_ref[...] = rope(proj(c0, KV_WIDTH))
    vc_ref[...] = proj(c0 + KV_WIDTH, KV_WIDTH)
    ks = rope(proj(c0 + 2 * KV_WIDTH, KV_WIDTH)).astype(BF16)
    vsT = proj(c0 + 3 * KV_WIDTH, KV_WIDTH).T.astype(BF16)
    kw = rope(proj(c0 + 4 * KV_WIDTH, KV_WIDTH)).astype(BF16)
    vwT = proj(c0 + 5 * KV_WIDTH, KV_WIDTH).T.astype(BF16)
    pad = jnp.zeros((tm, MXU_DEPTH - LANES - HEAD_DIM), BF16)
    ones = jnp.ones((ONES_ROWS, tm), BF16)
    for g in range(N_KV_GROUPS):
        kg = ks[:, g * HEAD_DIM:(g + 1) * HEAD_DIM]
        ksa_ref[g] = jnp.concatenate([onehot_ref[...], kg, pad], axis=1)
        kw_ref[g] = kw[:, g * HEAD_DIM:(g + 1) * HEAD_DIM]
        vsT_ref[g] = jnp.concatenate([vsT[g * HEAD_DIM:(g + 1) * HEAD_DIM, :], ones], axis=0)
        vwT_ref[g] = jnp.concatenate([vwT[g * HEAD_DIM:(g + 1) * HEAD_DIM, :], ones], axis=0)
    gT_ref[...] = jax.nn.sigmoid(proj(c0 + 6 * KV_WIDTH, N_KV_GROUPS * GATE_PAD)).T


def _inproj_call(x, pos3, mod, gmix, win, convw, gconv, freq, gmat, onehot):
    B, S, D = x.shape
    tm = TOKEN_TILE
    tok = lambda w: pl.BlockSpec((None, tm, w), lambda b, i: (b, i, 0))
    tr = lambda w: pl.BlockSpec((None, w, tm), lambda b, i: (b, 0, i))
    grp = lambda w: pl.BlockSpec((None, N_KV_GROUPS, tm, w), lambda b, i: (b, 0, i, 0))
    grpT = pl.BlockSpec((None, N_KV_GROUPS, VT_ROWS, tm), lambda b, i: (b, 0, 0, i))
    out_shapes = (
        jax.ShapeDtypeStruct((B, S, CONV_CH), BF16),
        jax.ShapeDtypeStruct((B, ATTN_WIDTH, S), BF16),
        jax.ShapeDtypeStruct((B, S, KV_WIDTH), F32),
        jax.ShapeDtypeStruct((B, S, KV_WIDTH), F32),
        jax.ShapeDtypeStruct((B, N_KV_GROUPS, S, MXU_DEPTH), BF16),
        jax.ShapeDtypeStruct((B, N_KV_GROUPS, VT_ROWS, S), BF16),
        jax.ShapeDtypeStruct((B, N_KV_GROUPS, S, HEAD_DIM), BF16),
        jax.ShapeDtypeStruct((B, N_KV_GROUPS, VT_ROWS, S), BF16),
        jax.ShapeDtypeStruct((B, N_KV_GROUPS * GATE_PAD, S), F32),
    )
    return pl.pallas_call(
        _inproj_kernel,
        grid=(B, S // tm),
        in_specs=[tok(D),
                  pl.BlockSpec((None, tm, 1), lambda b, i: (b, i, 0)),
                  pl.BlockSpec((None, N_MOD, D), lambda b, i: (b, 0, 0)),
                  _const_spec(gmix.shape), _const_spec(win.shape), _const_spec(convw.shape),
                  _const_spec(gconv.shape), _const_spec(freq.shape), _const_spec(gmat.shape),
                  pl.BlockSpec((tm, onehot.shape[1]), lambda b, i: (i, 0))],
        out_specs=(tok(CONV_CH), tr(ATTN_WIDTH), tok(KV_WIDTH), tok(KV_WIDTH), grp(MXU_DEPTH),
                   grpT, grp(HEAD_DIM), grpT, tr(N_KV_GROUPS * GATE_PAD)),
        out_shape=out_shapes,
        scratch_shapes=[pltpu.VMEM((8, CONV_CH), F32)],
        compiler_params=pltpu.CompilerParams(dimension_semantics=("arbitrary", "arbitrary"),
                                             vmem_limit_bytes=VMEM_LIMIT),
        name="mixer_inproj",
    )(x, pos3, mod, gmix, win, convw, gconv, freq, gmat, onehot)


def _compress_kernel(kf_ref, vf_ref, pek_ref, pev_ref, wk1_ref, wk2_ref, wv1_ref, wv2_ref,
                     kcc_ref, vcT_ref):
    half = CMP_BLOCK // 2
    n = kf_ref.shape[0] // half

    def mlp(x_ref, pe_ref, w1_ref, w2_ref):
        parts = []
        for p in range(2):
            acc = None
            for l0 in range(0, half, 2):
                xs, ws = [], []
                for l in (l0, l0 + 1):
                    row = p * half + l
                    xs.append((x_ref[pl.ds(l, n, stride=half), :]
                               + pe_ref[row:row + 1, :]).astype(BF16))
                    w = w1_ref[row * HEAD_DIM:(row + 1) * HEAD_DIM, :].astype(BF16)
                    z = jnp.zeros_like(w)
                    ws.append(jnp.concatenate([jnp.concatenate([w, z], axis=1),
                                               jnp.concatenate([z, w], axis=1)], axis=0))
                d = _dot(jnp.concatenate(xs, axis=1), jnp.concatenate(ws, axis=0))
                acc = d if acc is None else acc + d
            parts.append(acc)
        hpre = parts[0] + pltpu.roll(parts[1], n - 1, 0)
        hid = (hpre * jax.nn.sigmoid(hpre)).astype(BF16)
        w2 = w2_ref[...].astype(BF16)
        return jnp.concatenate([_dot(hid[:, g * CMP_HIDDEN:(g + 1) * CMP_HIDDEN], w2)
                                for g in range(N_KV_GROUPS)], axis=1)

    kc = mlp(kf_ref, pek_ref, wk1_ref, wk2_ref).astype(kcc_ref.dtype)
    for g in range(N_KV_GROUPS):
        kcc_ref[g] = kc[:, g * HEAD_DIM:(g + 1) * HEAD_DIM]
    vcT = mlp(vf_ref, pev_ref, wv1_ref, wv2_ref).T.astype(vcT_ref.dtype)
    ones = jnp.ones((ONES_ROWS, n), vcT_ref.dtype)
    for g in range(N_KV_GROUPS):
        vcT_ref[g] = jnp.concatenate([vcT[g * HEAD_DIM:(g + 1) * HEAD_DIM, :], ones], axis=0)


def _compress_call(kf, vf, pek, pev, wk1, wk2, wv1, wv2):
    B, S, width = kf.shape
    n = S // (CMP_BLOCK // 2)
    flat = pl.BlockSpec((None, S, width), lambda b: (b, 0, 0))
    return pl.pallas_call(
        _compress_kernel,
        grid=(B,),
        in_specs=[flat, flat, _const_spec(pek.shape), _const_spec(pev.shape),
                  _const_spec(wk1.shape), _const_spec(wk2.shape), _const_spec(wv1.shape),
                  _const_spec(wv2.shape)],
        out_specs=(pl.BlockSpec((None, N_KV_GROUPS, n, HEAD_DIM), lambda b: (b, 0, 0, 0)),
                   pl.BlockSpec((None, N_KV_GROUPS, VT_ROWS, n), lambda b: (b, 0, 0, 0))),
        out_shape=(jax.ShapeDtypeStruct((B, N_KV_GROUPS, n, HEAD_DIM), BF16),
                   jax.ShapeDtypeStruct((B, N_KV_GROUPS, VT_ROWS, n), BF16)),
        compiler_params=pltpu.CompilerParams(dimension_semantics=("arbitrary",),
                                             vmem_limit_bytes=VMEM_LIMIT),
        name="kv_compress",
    )(kf, vf, pek, pev, wk1, wk2, wv1, wv2)


def _lane_tiles(x, n):
    return jnp.concatenate([x] * n, axis=1)


def _col_max(s_ref, n_rows, bias_fn):
    groups = ROW_TILE // 8
    mx = [jnp.full((8, s_ref.shape[1]), NEG, F32)] * groups
    for r in range(0, n_rows, ROW_TILE):
        x = s_ref[r:r + ROW_TILE, :]
        if bias_fn is not None:
            x = x + bias_fn(r)
            s_ref[r:r + ROW_TILE, :] = x
        mx = [jnp.maximum(mx[i], x[8 * i:8 * (i + 1), :]) for i in range(groups)]
    while len(mx) > 1:
        mx = [jnp.maximum(a, b) for a, b in zip(mx[0::2], mx[1::2])]
    return jnp.max(mx[0], axis=0, keepdims=True)


def _col_exp2(s_ref, p_ref, n_rows, m, keep_f32=False):
    for r in range(0, n_rows, ROW_TILE):
        p = jnp.exp2(s_ref[r:r + ROW_TILE, :] - m)
        if keep_f32:
            s_ref[r:r + ROW_TILE, :] = p
        p_ref[r:r + ROW_TILE, :] = p.astype(p_ref.dtype)


def _recip_pos(l):
    return 1.0 / jnp.where(l > 0.0, l, 1.0)


def _attn_kernel(top_n, qT_ref, gT_ref, kcc_ref, vcT_ref, ksa_ref, vsT_ref, kw_ref, vwT_ref,
                 ovlT_ref, ga_ref, o_ref, s0_ref, s1_ref, p0_ref, p1_ref, sc_ref, pc_ref, sw_ref,
                 pw_ref, ph_ref, pl_ref, rhs_ref):
    QB = Q_BLOCK
    n_cmp = kcc_ref.shape[0]
    n_slc = ovlT_ref.shape[0]
    t0 = pl.program_id(2) * QB
    tq = t0 + lax.broadcasted_iota(jnp.int32, (1, QB), 1)
    rows = lax.broadcasted_iota(jnp.int32, (ROW_TILE, 1), 0)

    qT = qT_ref[...]
    qcat = jnp.concatenate([qT[h * HEAD_DIM:(h + 1) * HEAD_DIM, :] for h in range(HPG)], axis=1)

    w0 = pl.multiple_of(jnp.maximum(t0 - WINDOW, 0), QB)
    sc_ref[...] = _dot(kcc_ref[...], qcat)
    sw_ref[...] = _dot(kw_ref[pl.ds(w0, WIN_KEYS), :], qcat)

    def cmp_bias(r):
        cmp_end = (r + rows) * CMP_STRIDE + (CMP_BLOCK - 1)
        return _lane_tiles(jnp.where(cmp_end <= tq, 0.0, NEG), HPG)

    m = jnp.maximum(_col_max(sc_ref, n_cmp, cmp_bias), M_FLOOR)
    _col_exp2(sc_ref, pc_ref, n_cmp, m, keep_f32=True)
    o_cmp = _dot(vcT_ref[...], pc_ref[...])
    rl = _recip_pos(o_cmp[HEAD_DIM:HEAD_DIM + 1, :])
    o_cmp = o_cmp[0:HEAD_DIM, :] * rl

    for r in range(0, n_cmp, ROW_TILE):
        pn = sc_ref[r:r + ROW_TILE, :] * rl
        psum = pn[:, 0:QB]
        for h in range(1, HPG):
            psum = psum + pn[:, h * QB:(h + 1) * QB]
        hi, lo = _split_bf16(psum)
        ph_ref[r:r + ROW_TILE, :] = hi
        pl_ref[r:r + ROW_TILE, :] = lo
    imp = _dot(ovlT_ref[...], ph_ref[...]) + _dot(ovlT_ref[...], pl_ref[...])

    def win_bias(r):
        dist = tq - (w0 + r + rows)
        return _lane_tiles(jnp.where((dist >= 0) & (dist < WINDOW), 0.0, NEG), HPG)

    m = _col_max(sw_ref, WIN_KEYS, win_bias)
    _col_exp2(sw_ref, pw_ref, WIN_KEYS, m)
    o_win = _dot(vwT_ref[:, pl.ds(w0, WIN_KEYS)], pw_ref[...])
    o_win = o_win[0:HEAD_DIM, :] * (1.0 / o_win[HEAD_DIM:HEAD_DIM + 1, :])

    blk = lax.broadcasted_iota(jnp.int32, (n_slc, 1), 0)
    cur = tq // SLC_BLOCK
    future = blk > cur
    forced = (blk == 0) | (blk == cur) | (blk == cur - 1)
    score0 = jnp.where(future, -BIG, jnp.where(forced, BIG, imp))
    score = score0
    cum = jnp.zeros((1, QB), F32)
    thr = jnp.zeros((1, QB), F32)
    above = jnp.zeros((1, QB), F32)
    for _ in range(top_n):
        best = jnp.max(score, axis=0, keepdims=True)
        eq = score == best
        unfilled = cum < top_n
        thr = jnp.where(unfilled, best, thr)
        above = jnp.where(unfilled, cum, above)
        cum = cum + jnp.sum(jnp.where(eq, 1.0, 0.0), axis=0, keepdims=True)
        score = jnp.where(eq, -jnp.inf, score)
    ties = score0 == thr
    lower = jnp.where(lax.broadcasted_iota(jnp.int32, (1, n_slc), 1) < blk, 1.0, 0.0).astype(BF16)
    rank = _dot(lower, jnp.where(ties, 1.0, 0.0).astype(BF16))
    picked = (score0 > thr) | (ties & (rank < top_n - above))
    sel_bias = jnp.where(future, NEG, jnp.where(picked, 0.0, NEG)).astype(BF16)
    rhs_ref[...] = jnp.concatenate([_lane_tiles(sel_bias, HPG), qcat,
                                    jnp.zeros((MXU_DEPTH - n_slc - HEAD_DIM, QW), BF16)], axis=0)

    KC = KEY_CHUNK
    last = t0 // KC

    def scores(c, dst_ref):
        k0 = pl.multiple_of(jnp.minimum(c, last) * KC, KC)
        dst_ref[...] = _dot(ksa_ref[pl.ds(k0, KC), :], rhs_ref[...])

    def weighted_values(c, p_ref):
        k0 = pl.multiple_of(jnp.clip(c, 0, last) * KC, KC)
        return _dot(vsT_ref[:, pl.ds(k0, KC)], p_ref[...])

    def softmax(c, src_ref, p_ref, m, acc, causal):
        def causal_bias(r):
            return _lane_tiles(jnp.where(c * KC + r + rows <= tq, 0.0, NEG), HPG)

        m_new = jnp.maximum(m, _col_max(src_ref, KC, causal_bias if causal else None))
        _col_exp2(src_ref, p_ref, KC, m_new)
        return m_new, jnp.exp2(m - m_new) * acc

    def pair(i, carry):
        m, acc = carry
        pending = weighted_values(2 * i - 1, p1_ref)
        scores(2 * i + 1, s1_ref)
        m, acc = softmax(2 * i, s0_ref, p0_ref, m, acc + pending, False)
        pending = weighted_values(2 * i, p0_ref)
        scores(2 * i + 2, s0_ref)
        return softmax(2 * i + 1, s1_ref, p1_ref, m, acc + pending, True)

    p1_ref[...] = jnp.zeros_like(p1_ref)
    scores(0, s0_ref)
    init = (jnp.full((1, QW), NEG, F32), jnp.zeros((VT_ROWS, QW), F32))
    n_pairs = (last + 1) // 2
    m, acc = lax.fori_loop(0, n_pairs, pair, init)
    acc = acc + weighted_values(2 * n_pairs - 1, p1_ref)

    def diagonal(carry):
        m, acc = softmax(last, s0_ref, p0_ref, carry[0], carry[1], True)
        return m, acc + weighted_values(last, p0_ref)

    _, acc = lax.cond(last % 2 == 0, diagonal, lambda cr: cr, (m, acc))
    o_slc = acc[0:HEAD_DIM, :] * (1.0 / acc[HEAD_DIM:HEAD_DIM + 1, :])

    gts = gT_ref[...]
    outs = []
    for h in range(HPG):
        sl = slice(h * QB, (h + 1) * QB)
        o = (gts[3 * h:3 * h + 1, :] * o_cmp[:, sl] + gts[3 * h + 1:3 * h + 2, :] * o_slc[:, sl]
             + gts[3 * h + 2:3 * h + 3, :] * o_win[:, sl])
        o = o * lax.rsqrt(jnp.mean(o * o, axis=0, keepdims=True) + EPS)
        outs.append((o * ga_ref[h * HEAD_DIM:(h + 1) * HEAD_DIM, :]).T)
    o_ref[...] = jnp.concatenate(outs, axis=1).astype(o_ref.dtype)


def _attn_call(top_n, qT, gatesT, kcc, vcT, ksa, vsT, kw, vwT, ovlT, g_attn_col):
    B, _, S = qT.shape
    n_cmp = kcc.shape[2]
    gw = HPG * HEAD_DIM
    rows = lambda n, w: pl.BlockSpec((None, None, n, w), lambda b, g, i: (b, g, 0, 0))
    cols = lambda n: pl.BlockSpec((None, None, VT_ROWS, n), lambda b, g, i: (b, g, 0, 0))
    return pl.pallas_call(
        functools.partial(_attn_kernel, top_n),
        grid=(B, N_KV_GROUPS, S // Q_BLOCK),
        in_specs=[pl.BlockSpec((None, gw, Q_BLOCK), lambda b, g, i: (b, g, i)),
                  pl.BlockSpec((None, GATE_PAD, Q_BLOCK), lambda b, g, i: (b, g, i)),
                  rows(n_cmp, HEAD_DIM), cols(n_cmp), rows(S, MXU_DEPTH), cols(S),
                  rows(S, HEAD_DIM), cols(S),
                  pl.BlockSpec(ovlT.shape, lambda b, g, i: (0, 0)),
                  pl.BlockSpec((gw, 1), lambda b, g, i: (g, 0))],
        out_specs=pl.BlockSpec((None, Q_BLOCK, gw), lambda b, g, i: (b, i, g)),
        out_shape=jax.ShapeDtypeStruct((B, S, ATTN_WIDTH), BF16),
        scratch_shapes=[pltpu.VMEM((KEY_CHUNK, QW), F32), pltpu.VMEM((KEY_CHUNK, QW), F32),
                        pltpu.VMEM((KEY_CHUNK, QW), BF16), pltpu.VMEM((KEY_CHUNK, QW), BF16),
                        pltpu.VMEM((n_cmp, QW), F32), pltpu.VMEM((n_cmp, QW), BF16),
                        pltpu.VMEM((WIN_KEYS, QW), F32), pltpu.VMEM((WIN_KEYS, QW), BF16),
                        pltpu.VMEM((n_cmp, Q_BLOCK), BF16), pltpu.VMEM((n_cmp, Q_BLOCK), BF16),
                        pltpu.VMEM((MXU_DEPTH, QW), BF16)],
        compiler_params=pltpu.CompilerParams(
            dimension_semantics=("parallel", "parallel", "arbitrary"),
            vmem_limit_bytes=VMEM_LIMIT),
        name="nsa_attention",
    )(qT, gatesT, kcc, vcT, ksa, vsT, kw, vwT, ovlT, g_attn_col)


def _out_kernel(x_ref, yc_ref, ya_ref, mod_ref, wo_ref, g_ref, wg_ref, wu_ref, wd_ref, gf_ref,
                o_ref):
    mix = _dot(yc_ref[...], wo_ref[0:CONV_CH, :]) + _dot(ya_ref[...], wo_ref[CONV_CH:, :])
    x = x_ref[...] + mod_ref[5:6, :] * mix
    x = _ffn_core(x, mod_ref[6:7, :], mod_ref[7:8, :], mod_ref[8:9, :], g_ref[...], wg_ref,
                  wu_ref, wd_ref)
    o_ref[...] = _rms(x, gf_ref[...])


def _out_call(x, yc, ya, mod, wo, g, wg, wu, wd, gf):
    B, S, D = x.shape
    tm = TOKEN_TILE
    tok = lambda w: pl.BlockSpec((None, tm, w), lambda b, i: (b, i, 0))
    return pl.pallas_call(
        _out_kernel,
        grid=(B, S // tm),
        in_specs=[tok(D), tok(CONV_CH), tok(ATTN_WIDTH),
                  pl.BlockSpec((None, N_MOD, D), lambda b, i: (b, 0, 0)),
                  _const_spec(wo.shape), _const_spec(g.shape), _const_spec(wg.shape),
                  _const_spec(wu.shape), _const_spec(wd.shape), _const_spec(gf.shape)],
        out_specs=tok(D),
        out_shape=jax.ShapeDtypeStruct((B, S, D), F32),
        compiler_params=pltpu.CompilerParams(dimension_semantics=("parallel", "parallel"),
                                             vmem_limit_bytes=VMEM_LIMIT),
        name="outproj_ffn2",
    )(x, yc, ya, mod, wo, g, wg, wu, wd, gf)


def kernel(x, c, positions, w_ada, b_ada, g_ffn1, w1_gate, w1_up, w1_down, g_mix, w_in, conv_w, cmp_pos_k, cmp_pos_v, w_cmpk1, w_cmpk2, w_cmpv1, w_cmpv2, g_out_conv, g_out_attn, w_out, g_ffn2, w2_gate, w2_up, w2_down, g_final):
    B, S, D = x.shape
    depth = w_ada.shape[0]
    n_slc = S // SLC_BLOCK
    half = CMP_BLOCK // 2
    n_half = S // half
    assert n_slc <= LANES, "selection-block one-hot is one lane tile wide"

    c_pad = jnp.pad(c, ((0, 8 - B), (0, 0)))
    row = lambda a: a.reshape(1, -1)

    freq_half = jnp.power(ROPE_THETA, -2.0 * jnp.arange(ROT_HALF, dtype=F32) / ROT_DIM)
    freq = jnp.tile(freq_half, LANES // ROT_HALF).reshape(1, LANES)
    gidx = np.arange(CONV_CH) // (CONV_CH // CONV_GROUPS)
    gmat = jnp.asarray((gidx[:, None] == gidx[None, :]) / (CONV_CH // CONV_GROUPS), dtype=BF16)
    c0 = np.arange(n_half) * CMP_STRIDE
    s0 = np.arange(LANES) * SLC_BLOCK
    ovlT = ((c0[None, :] <= s0[:, None] + SLC_BLOCK - 1) & (c0[None, :] + CMP_BLOCK - 1 >= s0[:, None]))
    ovlT = jnp.asarray(ovlT, dtype=BF16)
    onehot = jnp.asarray((np.arange(S) // SLC_BLOCK)[:, None] == np.arange(LANES)[None, :], dtype=BF16)
    pos3 = positions.reshape(B, S, 1)

    for l in range(depth):
        mod = _ada_call(c_pad, w_ada[l], row(b_ada[l]))[:B].reshape(B, N_MOD, D)

        x = _ffn_call(x, mod, row(g_ffn1[l]), w1_gate[l].astype(BF16), w1_up[l].astype(BF16),
                      w1_down[l].astype(BF16))

        n_main = w_in.shape[2] - N_KV_GROUPS * 3 * HPG
        gate_cols = [jnp.pad(w_in[l][:, n_main + g * 3 * HPG:n_main + (g + 1) * 3 * HPG],
                             ((0, 0), (0, GATE_PAD - 3 * HPG))) for g in range(N_KV_GROUPS)]
        win = jnp.concatenate([w_in[l][:, :n_main]] + gate_cols, axis=1).astype(BF16)
        (yc, qT, kc, vc, ksa, vsT, kw, vwT, gatesT) = _inproj_call(
            x, pos3, mod, row(g_mix[l]), win, conv_w[l], row(g_out_conv[l]), freq, gmat, onehot)

        both_groups = lambda pe: jnp.tile(pe, (1, N_KV_GROUPS))
        kcc, vcT = _compress_call(kc, vc, both_groups(cmp_pos_k[l]), both_groups(cmp_pos_v[l]),
                                  w_cmpk1[l], w_cmpk2[l], w_cmpv1[l], w_cmpv2[l])

        ya = _attn_call(min(SLC_TOP_N, n_slc), qT, gatesT, kcc, vcT, ksa, vsT, kw, vwT, ovlT,
                        g_out_attn[l].reshape(ATTN_WIDTH, 1))

        assert l == depth - 1, "final norm is fused into the last layer's output kernel"
        x = _out_call(x, yc, ya, mod, w_out[l].astype(BF16), row(g_ffn2[l]),
                      w2_gate[l].astype(BF16), w2_up[l].astype(BF16), w2_down[l].astype(BF16),
                      row(g_final))
    return x
```

```python
import functools
import math

import numpy as np
import jax
import jax.numpy as jnp
from jax import lax
from jax.experimental import pallas as pl
from jax.experimental.pallas import tpu as pltpu

F32 = jnp.float32
BF16 = jnp.bfloat16

CONV_CH = 512
CONV_GROUPS = 8
N_HEADS = 8
N_KV_GROUPS = 2
HPG = N_HEADS // N_KV_GROUPS
HEAD_DIM = 64
ATTN_WIDTH = N_HEADS * HEAD_DIM
KV_WIDTH = N_KV_GROUPS * HEAD_DIM
ROPE_THETA = 500000.0
ROT_DIM = HEAD_DIM // 4
ROT_HALF = ROT_DIM // 2
CMP_BLOCK = 32
CMP_STRIDE = 16
CMP_HIDDEN = 256
SLC_BLOCK = 64
SLC_TOP_N = 16
WINDOW = 512
Q_BLOCK = 256
MACARON_W = 0.5
N_MOD = 9
EPS = 1e-6
NEG = -1e30
BIG = 1e9

LANES = 128
MXU_DEPTH = 256
VMEM_LIMIT = 56 * 1024 * 1024

TOKEN_TILE = 512
FF_TILE = 256
KEY_CHUNK = 512
WIN_KEYS = WINDOW + Q_BLOCK
GATE_PAD = LANES
QW = HPG * Q_BLOCK
ONES_ROWS = 16
VT_ROWS = HEAD_DIM + ONES_ROWS
ROW_TILE = 16
Q_SCALE = HEAD_DIM ** -0.5 * math.log2(math.e)
M_FLOOR = -1e20


def _dot(a, b):
    return jnp.dot(a, b, preferred_element_type=F32)


def _rms(x, g):
    return x * lax.rsqrt(jnp.mean(x * x, axis=-1, keepdims=True) + EPS) * g


def _split_bf16(x):
    hi = x.astype(BF16)
    lo = (x - hi.astype(F32)).astype(BF16)
    return hi, lo


def _const_spec(shape):
    nd = len(shape)
    return pl.BlockSpec(shape, lambda *_: (0,) * nd, pipeline_mode=pl.Buffered(1))


def _ada_kernel(c_ref, w_ref, b_ref, o_ref):
    c = c_ref[...]
    c_act = c * jax.nn.sigmoid(c)
    o_ref[...] = jnp.dot(c_act, w_ref[...], preferred_element_type=F32,
                         precision=lax.Precision.HIGHEST) + b_ref[...]


def _ada_call(c_pad, w_ada, b_ada):
    rows, d = c_pad.shape
    n = w_ada.shape[1]
    tn = 1024
    return pl.pallas_call(
        _ada_kernel,
        grid=(n // tn,),
        in_specs=[pl.BlockSpec((rows, d), lambda j: (0, 0)),
                  pl.BlockSpec((d, tn), lambda j: (0, j)),
                  pl.BlockSpec((1, tn), lambda j: (0, j))],
        out_specs=pl.BlockSpec((rows, tn), lambda j: (0, j)),
        out_shape=jax.ShapeDtypeStruct((rows, n), F32),
        compiler_params=pltpu.CompilerParams(dimension_semantics=("arbitrary",),
                                             vmem_limit_bytes=VMEM_LIMIT),
        name="adaln_mod",
    )(c_pad, w_ada, b_ada)


def _ffn_core(x, shift, scale, gate, g, wg_ref, wu_ref, wd_ref):
    h = _rms(x, g) * (1.0 + scale) + shift
    hb = h.astype(BF16)
    d_ff = wg_ref.shape[1]
    acc = None
    for j in range(d_ff // FF_TILE):
        sl = slice(j * FF_TILE, (j + 1) * FF_TILE)
        gg = _dot(hb, wg_ref[:, sl])
        uu = _dot(hb, wu_ref[:, sl])
        a = (gg * jax.nn.sigmoid(gg) * uu).astype(BF16)
        d = _dot(a, wd_ref[sl, :])
        acc = d if acc is None else acc + d
    return x + (MACARON_W * gate) * acc


def _ffn_kernel(x_ref, mod_ref, g_ref, wg_ref, wu_ref, wd_ref, o_ref):
    o_ref[...] = _ffn_core(x_ref[...], mod_ref[0:1, :], mod_ref[1:2, :], mod_ref[2:3, :],
                           g_ref[...], wg_ref, wu_ref, wd_ref)


def _ffn_call(x, mod, g, wg, wu, wd):
    B, S, D = x.shape
    tm = TOKEN_TILE
    return pl.pallas_call(
        _ffn_kernel,
        grid=(B, S // tm),
        in_specs=[pl.BlockSpec((None, tm, D), lambda b, i: (b, i, 0)),
                  pl.BlockSpec((None, N_MOD, D), lambda b, i: (b, 0, 0)),
                  _const_spec(g.shape), _const_spec(wg.shape), _const_spec(wu.shape),
                  _const_spec(wd.shape)],
        out_specs=pl.BlockSpec((None, tm, D), lambda b, i: (b, i, 0)),
        out_shape=jax.ShapeDtypeStruct((B, S, D), F32),
        compiler_params=pltpu.CompilerParams(dimension_semantics=("parallel", "parallel"),
                                             vmem_limit_bytes=VMEM_LIMIT),
        name="ffn1",
    )(x, mod, g, wg, wu, wd)


def _rope_table_kernel(pos_ref, freq_ref, cos_ref, sin_ref):
    ang = pos_ref[...].astype(F32) * freq_ref[...]
    cos_ref[...] = jnp.cos(ang)
    sin_ref[...] = jnp.sin(ang)


def _rope_table_call(pos_rep, freq):
    shape = jax.ShapeDtypeStruct(pos_rep.shape, F32)
    full = pl.BlockSpec(pos_rep.shape, lambda: (0, 0))
    return pl.pallas_call(
        _rope_table_kernel,
        in_specs=[full, pl.BlockSpec(freq.shape, lambda: (0, 0))],
        out_specs=(full, full),
        out_shape=(shape, shape),
        compiler_params=pltpu.CompilerParams(vmem_limit_bytes=VMEM_LIMIT),
        name="rope_table",
    )(pos_rep, freq)


def _inproj_kernel(x_ref, rope_ref, mod_ref, gmix_ref, win_ref, convw_ref, gconv_ref,
                   gmat_ref, onehot_ref, yc_ref, qT_ref, kc_ref, vc_ref, ksa_ref, vsT_ref, kw_ref,
                   vwT_ref, gT_ref, carry_ref):
    tm = x_ref.shape[0]

    @pl.when(pl.program_id(1) == 0)
    def _():
        carry_ref[...] = jnp.zeros_like(carry_ref)

    x = x_ref[...]
    h = _rms(x, gmix_ref[...]) * (1.0 + mod_ref[4:5, :]) + mod_ref[3:4, :]
    hb = h.astype(BF16)

    def proj(c0, width):
        return _dot(hb, win_ref[:, c0:c0 + width])

    cb = proj(0, CONV_CH)
    u = proj(CONV_CH, CONV_CH) * proj(2 * CONV_CH, CONV_CH)
    row = lax.broadcasted_iota(jnp.int32, (tm, 1), 0)
    prev1 = carry_ref[7:8, :]
    prev2 = carry_ref[6:7, :]
    u1 = jnp.where(row >= 1, pltpu.roll(u, 1, 0), prev1)
    u2 = jnp.where(row >= 2, pltpu.roll(u, 2, 0), jnp.where(row == 1, prev1, prev2))
    carry_ref[...] = u[tm - 8:tm, :]
    v = convw_ref[0:1, :] * u2 + convw_ref[1:2, :] * u1 + convw_ref[2:3, :] * u
    y = cb * v
    hi, lo = _split_bf16(y * y)
    ms = _dot(hi, gmat_ref[...]) + _dot(lo, gmat_ref[...])
    yc_ref[...] = (y * lax.rsqrt(ms + EPS) * gconv_ref[...]).astype(yc_ref.dtype)

    tab = jnp.concatenate([rope_ref[...], jnp.zeros((tm, LANES - 2 * ROT_HALF), F32)], axis=1)
    d = lax.broadcasted_iota(jnp.int32, (tm, LANES), 1) & (HEAD_DIM - 1)
    fidx = d & (ROT_HALF - 1)
    cos_t = jnp.where(d < ROT_DIM, jnp.take_along_axis(tab, fidx, axis=1), 1.0)
    sin_raw = jnp.take_along_axis(tab, fidx + ROT_HALF, axis=1)
    sin_t = jnp.where(d < ROT_HALF, -sin_raw, jnp.where(d < ROT_DIM, sin_raw, 0.0))
    first_half = d < ROT_HALF

    def rope(t):
        outs = []
        for j in range(t.shape[1] // LANES):
            tj = t[:, j * LANES:(j + 1) * LANES]
            partner = jnp.where(first_half, pltpu.roll(tj, LANES - ROT_HALF, 1),
                                pltpu.roll(tj, ROT_HALF, 1))
            outs.append(tj * cos_t + partner * sin_t)
        return outs[0] if len(outs) == 1 else jnp.concatenate(outs, axis=1)

    c0 = 3 * CONV_CH
    q = rope(proj(c0, ATTN_WIDTH)) * Q_SCALE
    qT_ref[...] = q.T.astype(qT_ref.dtype)
    c0 += ATTN_WIDTH
    kv = proj(c0, 2 * KV_WIDTH)
    kc_ref[...] = rope(kv[:, :KV_WIDTH])
    vc_ref[...] = kv[:, KV_WIDTH:]
    kv = proj(c0 + 2 * KV_WIDTH, 2 * KV_WIDTH)
    ks = rope(kv[:, :KV_WIDTH]).astype(BF16)
    vsT = kv[:, KV_WIDTH:].T.astype(BF16)
    kv = proj(c0 + 4 * KV_WIDTH, 2 * KV_WIDTH)
    kw = rope(kv[:, :KV_WIDTH]).astype(BF16)
    vwT = kv[:, KV_WIDTH:].T.astype(BF16)
    pad = jnp.zeros((tm, MXU_DEPTH - LANES - HEAD_DIM), BF16)
    ones = jnp.ones((ONES_ROWS, tm), BF16)
    for g in range(N_KV_GROUPS):
        kg = ks[:, g * HEAD_DIM:(g + 1) * HEAD_DIM]
        ksa_ref[g] = jnp.concatenate([onehot_ref[...], kg, pad], axis=1)
        kw_ref[g] = kw[:, g * HEAD_DIM:(g + 1) * HEAD_DIM]
        vsT_ref[g] = jnp.concatenate([vsT[g * HEAD_DIM:(g + 1) * HEAD_DIM, :], ones], axis=0)
        vwT_ref[g] = jnp.concatenate([vwT[g * HEAD_DIM:(g + 1) * HEAD_DIM, :], ones], axis=0)
    gT_ref[...] = jax.nn.sigmoid(proj(c0 + 6 * KV_WIDTH, N_KV_GROUPS * GATE_PAD)).T


def _inproj_call(x, rope_tab, mod, gmix, win, convw, gconv, gmat, onehot):
    B, S, D = x.shape
    tm = TOKEN_TILE
    tok = lambda w: pl.BlockSpec((None, tm, w), lambda b, i: (b, i, 0))
    tr = lambda w: pl.BlockSpec((None, w, tm), lambda b, i: (b, 0, i))
    grp = lambda w: pl.BlockSpec((None, N_KV_GROUPS, tm, w), lambda b, i: (b, 0, i, 0))
    grpT = pl.BlockSpec((None, N_KV_GROUPS, VT_ROWS, tm), lambda b, i: (b, 0, 0, i))
    out_shapes = (
        jax.ShapeDtypeStruct((B, S, CONV_CH), BF16),
        jax.ShapeDtypeStruct((B, ATTN_WIDTH, S), BF16),
        jax.ShapeDtypeStruct((B, S, KV_WIDTH), F32),
        jax.ShapeDtypeStruct((B, S, KV_WIDTH), F32),
        jax.ShapeDtypeStruct((B, N_KV_GROUPS, S, MXU_DEPTH), BF16),
        jax.ShapeDtypeStruct((B, N_KV_GROUPS, VT_ROWS, S), BF16),
        jax.ShapeDtypeStruct((B, N_KV_GROUPS, S, HEAD_DIM), BF16),
        jax.ShapeDtypeStruct((B, N_KV_GROUPS, VT_ROWS, S), BF16),
        jax.ShapeDtypeStruct((B, N_KV_GROUPS * GATE_PAD, S), F32),
    )
    return pl.pallas_call(
        _inproj_kernel,
        grid=(B, S // tm),
        in_specs=[tok(D),
                  tok(rope_tab.shape[2]),
                  pl.BlockSpec((None, N_MOD, D), lambda b, i: (b, 0, 0)),
                  _const_spec(gmix.shape), _const_spec(win.shape), _const_spec(convw.shape),
                  _const_spec(gconv.shape), _const_spec(gmat.shape),
                  pl.BlockSpec((tm, onehot.shape[1]), lambda b, i: (i, 0))],
        out_specs=(tok(CONV_CH), tr(ATTN_WIDTH), tok(KV_WIDTH), tok(KV_WIDTH), grp(MXU_DEPTH),
                   grpT, grp(HEAD_DIM), grpT, tr(N_KV_GROUPS * GATE_PAD)),
        out_shape=out_shapes,
        scratch_shapes=[pltpu.VMEM((8, CONV_CH), F32)],
        compiler_params=pltpu.CompilerParams(dimension_semantics=("arbitrary", "arbitrary"),
                                             vmem_limit_bytes=VMEM_LIMIT),
        name="mixer_inproj",
    )(x, rope_tab, mod, gmix, win, convw, gconv, gmat, onehot)


def _compress_kernel(kf_ref, vf_ref, pek_ref, pev_ref, wk1_ref, wk2_ref, wv1_ref, wv2_ref,
                     kcc_ref, vcT_ref):
    half = CMP_BLOCK // 2
    n = kf_ref.shape[0] // half

    def mlp(x_ref, pe_ref, w1_ref, w2_ref):
        parts = []
        for p in range(2):
            acc = None
            for l0 in range(0, half, 2):
                xs, ws = [], []
                for l in (l0, l0 + 1):
                    row = p * half + l
                    xs.append((x_ref[pl.ds(l, n, stride=half), :]
                               + pe_ref[row:row + 1, :]).astype(BF16))
                    w = w1_ref[row * HEAD_DIM:(row + 1) * HEAD_DIM, :].astype(BF16)
                    z = jnp.zeros_like(w)
                    ws.append(jnp.concatenate([jnp.concatenate([w, z], axis=1),
                                               jnp.concatenate([z, w], axis=1)], axis=0))
                d = _dot(jnp.concatenate(xs, axis=1), jnp.concatenate(ws, axis=0))
                acc = d if acc is None else acc + d
            parts.append(acc)
        hpre = parts[0] + pltpu.roll(parts[1], n - 1, 0)
        hid = (hpre * jax.nn.sigmoid(hpre)).astype(BF16)
        w2 = w2_ref[...].astype(BF16)
        return jnp.concatenate([_dot(hid[:, g * CMP_HIDDEN:(g + 1) * CMP_HIDDEN], w2)
                                for g in range(N_KV_GROUPS)], axis=1)

    kc = mlp(kf_ref, pek_ref, wk1_ref, wk2_ref).astype(kcc_ref.dtype)
    for g in range(N_KV_GROUPS):
        kcc_ref[g] = kc[:, g * HEAD_DIM:(g + 1) * HEAD_DIM]
    vcT = mlp(vf_ref, pev_ref, wv1_ref, wv2_ref).T.astype(vcT_ref.dtype)
    ones = jnp.ones((ONES_ROWS, n), vcT_ref.dtype)
    for g in range(N_KV_GROUPS):
        vcT_ref[g] = jnp.concatenate([vcT[g * HEAD_DIM:(g + 1) * HEAD_DIM, :], ones], axis=0)


def _compress_call(kf, vf, pek, pev, wk1, wk2, wv1, wv2):
    B, S, width = kf.shape
    n = S // (CMP_BLOCK // 2)
    flat = pl.BlockSpec((None, S, width), lambda b: (b, 0, 0))
    return pl.pallas_call(
        _compress_kernel,
        grid=(B,),
        in_specs=[flat, flat, _const_spec(pek.shape), _const_spec(pev.shape),
                  _const_spec(wk1.shape), _const_spec(wk2.shape), _const_spec(wv1.shape),
                  _const_spec(wv2.shape)],
        out_specs=(pl.BlockSpec((None, N_KV_GROUPS, n, HEAD_DIM), lambda b: (b, 0, 0, 0)),
                   pl.BlockSpec((None, N_KV_GROUPS, VT_ROWS, n), lambda b: (b, 0, 0, 0))),
        out_shape=(jax.ShapeDtypeStruct((B, N_KV_GROUPS, n, HEAD_DIM), BF16),
                   jax.ShapeDtypeStruct((B, N_KV_GROUPS, VT_ROWS, n), BF16)),
        compiler_params=pltpu.CompilerParams(dimension_semantics=("arbitrary",),
                                             vmem_limit_bytes=VMEM_LIMIT),
        name="kv_compress",
    )(kf, vf, pek, pev, wk1, wk2, wv1, wv2)


def _lane_tiles(x, n):
    return jnp.concatenate([x] * n, axis=1)


def _col_max(s_ref, n_rows, bias_fn):
    groups = ROW_TILE // 8
    mx = [jnp.full((8, s_ref.shape[1]), NEG, F32)] * groups
    for r in range(0, n_rows, ROW_TILE):
        x = s_ref[r:r + ROW_TILE, :]
        if bias_fn is not None:
            x = x + bias_fn(r)
            s_ref[r:r + ROW_TILE, :] = x
        mx = [jnp.maximum(mx[i], x[8 * i:8 * (i + 1), :]) for i in range(groups)]
    while len(mx) > 1:
        mx = [jnp.maximum(a, b) for a, b in zip(mx[0::2], mx[1::2])]
    return jnp.max(mx[0], axis=0, keepdims=True)


def _col_exp2(s_ref, p_ref, n_rows, m, keep_f32=False):
    for r in range(0, n_rows, ROW_TILE):
        p = jnp.exp2(s_ref[r:r + ROW_TILE, :] - m)
        if keep_f32:
            s_ref[r:r + ROW_TILE, :] = p
        p_ref[r:r + ROW_TILE, :] = p.astype(p_ref.dtype)


def _recip_pos(l):
    return 1.0 / jnp.where(l > 0.0, l, 1.0)


def _attn_kernel(top_n, qT_ref, gT_ref, kcc_ref, vcT_ref, ksa_ref, vsT_ref, kw_ref, vwT_ref,
                 ovlT_ref, ga_ref, o_ref, s0_ref, s1_ref, p0_ref, p1_ref, sc_ref, pc_ref, sw_ref,
                 pw_ref, ph_ref, pl_ref, rhs_ref):
    QB = Q_BLOCK
    n_cmp = kcc_ref.shape[0]
    n_slc = ovlT_ref.shape[0]
    t0 = pl.program_id(2) * QB
    tq = t0 + lax.broadcasted_iota(jnp.int32, (1, QB), 1)
    rows = lax.broadcasted_iota(jnp.int32, (ROW_TILE, 1), 0)

    qT = qT_ref[...]
    qcat = jnp.concatenate([qT[h * HEAD_DIM:(h + 1) * HEAD_DIM, :] for h in range(HPG)], axis=1)

    w0 = pl.multiple_of(jnp.maximum(t0 - WINDOW, 0), QB)
    sc_ref[...] = _dot(kcc_ref[...], qcat)
    sw_ref[...] = _dot(kw_ref[pl.ds(w0, WIN_KEYS), :], qcat)

    def cmp_bias(r):
        cmp_end = (r + rows) * CMP_STRIDE + (CMP_BLOCK - 1)
        return _lane_tiles(jnp.where(cmp_end <= tq, 0.0, NEG), HPG)

    m = jnp.maximum(_col_max(sc_ref, n_cmp, cmp_bias), M_FLOOR)
    _col_exp2(sc_ref, pc_ref, n_cmp, m, keep_f32=True)
    o_cmp = _dot(vcT_ref[...], pc_ref[...])
    rl = _recip_pos(o_cmp[HEAD_DIM:HEAD_DIM + 1, :])
    o_cmp = o_cmp[0:HEAD_DIM, :] * rl

    for r in range(0, n_cmp, ROW_TILE):
        pn = sc_ref[r:r + ROW_TILE, :] * rl
        psum = pn[:, 0:QB]
        for h in range(1, HPG):
            psum = psum + pn[:, h * QB:(h + 1) * QB]
        hi, lo = _split_bf16(psum)
        ph_ref[r:r + ROW_TILE, :] = hi
        pl_ref[r:r + ROW_TILE, :] = lo
    imp = _dot(ovlT_ref[...], ph_ref[...]) + _dot(ovlT_ref[...], pl_ref[...])

    def win_bias(r):
        dist = tq - (w0 + r + rows)
        return _lane_tiles(jnp.where((dist >= 0) & (dist < WINDOW), 0.0, NEG), HPG)

    m = _col_max(sw_ref, WIN_KEYS, win_bias)
    _col_exp2(sw_ref, pw_ref, WIN_KEYS, m)
    o_win = _dot(vwT_ref[:, pl.ds(w0, WIN_KEYS)], pw_ref[...])
    o_win = o_win[0:HEAD_DIM, :] * (1.0 / o_win[HEAD_DIM:HEAD_DIM + 1, :])

    blk = lax.broadcasted_iota(jnp.int32, (n_slc, 1), 0)
    cur = tq // SLC_BLOCK
    future = blk > cur
    forced = (blk == 0) | (blk == cur) | (blk == cur - 1)
    score0 = jnp.where(future, -BIG, jnp.where(forced, BIG, imp))
    score = score0
    cum = jnp.zeros((1, QB), F32)
    thr = jnp.zeros((1, QB), F32)
    above = jnp.zeros((1, QB), F32)
    for _ in range(top_n):
        best = jnp.max(score, axis=0, keepdims=True)
        eq = score == best
        unfilled = cum < top_n
        thr = jnp.where(unfilled, best, thr)
        above = jnp.where(unfilled, cum, above)
        cum = cum + jnp.sum(jnp.where(eq, 1.0, 0.0), axis=0, keepdims=True)
        score = jnp.where(eq, -jnp.inf, score)
    ties = score0 == thr
    lower = jnp.where(lax.broadcasted_iota(jnp.int32, (1, n_slc), 1) < blk, 1.0, 0.0).astype(BF16)
    rank = _dot(lower, jnp.where(ties, 1.0, 0.0).astype(BF16))
    picked = (score0 > thr) | (ties & (rank < top_n - above))
    sel_bias = jnp.where(future, NEG, jnp.where(picked, 0.0, NEG)).astype(BF16)
    rhs_ref[...] = jnp.concatenate([_lane_tiles(sel_bias, HPG), qcat,
                                    jnp.zeros((MXU_DEPTH - n_slc - HEAD_DIM, QW), BF16)], axis=0)

    KC = KEY_CHUNK
    last = t0 // KC

    def scores(c, dst_ref):
        k0 = pl.multiple_of(jnp.minimum(c, last) * KC, KC)
        dst_ref[...] = _dot(ksa_ref[pl.ds(k0, KC), :], rhs_ref[...])

    def weighted_values(c, p_ref):
        k0 = pl.multiple_of(jnp.clip(c, 0, last) * KC, KC)
        return _dot(vsT_ref[:, pl.ds(k0, KC)], p_ref[...])

    def softmax(c, src_ref, p_ref, m, acc, causal):
        def causal_bias(r):
            return _lane_tiles(jnp.where(c * KC + r + rows <= tq, 0.0, NEG), HPG)

        m_new = jnp.maximum(m, _col_max(src_ref, KC, causal_bias if causal else None))
        _col_exp2(src_ref, p_ref, KC, m_new)
        return m_new, jnp.exp2(m - m_new) * acc

    def pair(i, carry):
        m, acc = carry
        pending = weighted_values(2 * i - 1, p1_ref)
        scores(2 * i + 1, s1_ref)
        m, acc = softmax(2 * i, s0_ref, p0_ref, m, acc + pending, False)
        pending = weighted_values(2 * i, p0_ref)
        scores(2 * i + 2, s0_ref)
        return softmax(2 * i + 1, s1_ref, p1_ref, m, acc + pending, True)

    p1_ref[...] = jnp.zeros_like(p1_ref)
    scores(0, s0_ref)
    init = (jnp.full((1, QW), NEG, F32), jnp.zeros((VT_ROWS, QW), F32))
    n_pairs = (last + 1) // 2
    m, acc = lax.fori_loop(0, n_pairs, pair, init)
    acc = acc + weighted_values(2 * n_pairs - 1, p1_ref)

    def diagonal(carry):
        m, acc = softmax(last, s0_ref, p0_ref, carry[0], carry[1], True)
        return m, acc + weighted_values(last, p0_ref)

    _, acc = lax.cond(last % 2 == 0, diagonal, lambda cr: cr, (m, acc))
    o_slc = acc[0:HEAD_DIM, :] * (1.0 / acc[HEAD_DIM:HEAD_DIM + 1, :])

    gts = gT_ref[...]
    outs = []
    for h in range(HPG):
        sl = slice(h * QB, (h + 1) * QB)
        o = (gts[3 * h:3 * h + 1, :] * o_cmp[:, sl] + gts[3 * h + 1:3 * h + 2, :] * o_slc[:, sl]
             + gts[3 * h + 2:3 * h + 3, :] * o_win[:, sl])
        o = o * lax.rsqrt(jnp.mean(o * o, axis=0, keepdims=True) + EPS)
        outs.append((o * ga_ref[h * HEAD_DIM:(h + 1) * HEAD_DIM, :]).T)
    o_ref[...] = jnp.concatenate(outs, axis=1).astype(o_ref.dtype)


def _attn_call(top_n, qT, gatesT, kcc, vcT, ksa, vsT, kw, vwT, ovlT, g_attn_col):
    B, _, S = qT.shape
    n_cmp = kcc.shape[2]
    gw = HPG * HEAD_DIM
    rows = lambda n, w: pl.BlockSpec((None, None, n, w), lambda b, g, i: (b, g, 0, 0))
    cols = lambda n: pl.BlockSpec((None, None, VT_ROWS, n), lambda b, g, i: (b, g, 0, 0))
    return pl.pallas_call(
        functools.partial(_attn_kernel, top_n),
        grid=(B, N_KV_GROUPS, S // Q_BLOCK),
        in_specs=[pl.BlockSpec((None, gw, Q_BLOCK), lambda b, g, i: (b, g, i)),
                  pl.BlockSpec((None, GATE_PAD, Q_BLOCK), lambda b, g, i: (b, g, i)),
                  rows(n_cmp, HEAD_DIM), cols(n_cmp), rows(S, MXU_DEPTH), cols(S),
                  rows(S, HEAD_DIM), cols(S),
                  pl.BlockSpec(ovlT.shape, lambda b, g, i: (0, 0)),
                  pl.BlockSpec((gw, 1), lambda b, g, i: (g, 0))],
        out_specs=pl.BlockSpec((None, Q_BLOCK, gw), lambda b, g, i: (b, i, g)),
        out_shape=jax.ShapeDtypeStruct((B, S, ATTN_WIDTH), BF16),
        scratch_shapes=[pltpu.VMEM((KEY_CHUNK, QW), F32), pltpu.VMEM((KEY_CHUNK, QW), F32),
                        pltpu.VMEM((KEY_CHUNK, QW), BF16), pltpu.VMEM((KEY_CHUNK, QW), BF16),
                        pltpu.VMEM((n_cmp, QW), F32), pltpu.VMEM((n_cmp, QW), BF16),
                        pltpu.VMEM((WIN_KEYS, QW), F32), pltpu.VMEM((WIN_KEYS, QW), BF16),
                        pltpu.VMEM((n_cmp, Q_BLOCK), BF16), pltpu.VMEM((n_cmp, Q_BLOCK), BF16),
                        pltpu.VMEM((MXU_DEPTH, QW), BF16)],
        compiler_params=pltpu.CompilerParams(
            dimension_semantics=("parallel", "parallel", "arbitrary"),
            vmem_limit_bytes=VMEM_LIMIT),
        name="nsa_attention",
    )(qT, gatesT, kcc, vcT, ksa, vsT, kw, vwT, ovlT, g_attn_col)


def _out_kernel(x_ref, yc_ref, ya_ref, mod_ref, wo_ref, g_ref, wg_ref, wu_ref, wd_ref, gf_ref,
                o_ref):
    mix = _dot(yc_ref[...], wo_ref[0:CONV_CH, :]) + _dot(ya_ref[...], wo_ref[CONV_CH:, :])
    x = x_ref[...] + mod_ref[5:6, :] * mix
    x = _ffn_core(x, mod_ref[6:7, :], mod_ref[7:8, :], mod_ref[8:9, :], g_ref[...], wg_ref,
                  wu_ref, wd_ref)
    o_ref[...] = _rms(x, gf_ref[...])


def _out_call(x, yc, ya, mod, wo, g, wg, wu, wd, gf):
    B, S, D = x.shape
    tm = TOKEN_TILE
    tok = lambda w: pl.BlockSpec((None, tm, w), lambda b, i: (b, i, 0))
    return pl.pallas_call(
        _out_kernel,
        grid=(B, S // tm),
        in_specs=[tok(D), tok(CONV_CH), tok(ATTN_WIDTH),
                  pl.BlockSpec((None, N_MOD, D), lambda b, i: (b, 0, 0)),
                  _const_spec(wo.shape), _const_spec(g.shape), _const_spec(wg.shape),
                  _const_spec(wu.shape), _const_spec(wd.shape), _const_spec(gf.shape)],
        out_specs=tok(D),
        out_shape=jax.ShapeDtypeStruct((B, S, D), F32),
        compiler_params=pltpu.CompilerParams(dimension_semantics=("parallel", "parallel"),
                                             vmem_limit_bytes=VMEM_LIMIT),
        name="outproj_ffn2",
    )(x, yc, ya, mod, wo, g, wg, wu, wd, gf)


def kernel(x, c, positions, w_ada, b_ada, g_ffn1, w1_gate, w1_up, w1_down, g_mix, w_in, conv_w, cmp_pos_k, cmp_pos_v, w_cmpk1, w_cmpk2, w_cmpv1, w_cmpv2, g_out_conv, g_out_attn, w_out, g_ffn2, w2_gate, w2_up, w2_down, g_final):
    B, S, D = x.shape
    depth = w_ada.shape[0]
    n_slc = S // SLC_BLOCK
    half = CMP_BLOCK // 2
    n_half = S // half
    assert n_slc <= LANES, "selection-block one-hot is one lane tile wide"

    c_pad = jnp.pad(c, ((0, 8 - B), (0, 0)))
    row = lambda a: a.reshape(1, -1)

    freq_half = jnp.power(ROPE_THETA, -2.0 * jnp.arange(ROT_HALF, dtype=F32) / ROT_DIM)
    freq = jnp.tile(freq_half, LANES // ROT_HALF).reshape(1, LANES)
    gidx = np.arange(CONV_CH) // (CONV_CH // CONV_GROUPS)
    gmat = jnp.asarray((gidx[:, None] == gidx[None, :]) / (CONV_CH // CONV_GROUPS), dtype=BF16)
    c0 = np.arange(n_half) * CMP_STRIDE
    s0 = np.arange(LANES) * SLC_BLOCK
    ovlT = ((c0[None, :] <= s0[:, None] + SLC_BLOCK - 1) & (c0[None, :] + CMP_BLOCK - 1 >= s0[:, None]))
    ovlT = jnp.asarray(ovlT, dtype=BF16)
    onehot = jnp.asarray((np.arange(S) // SLC_BLOCK)[:, None] == np.arange(LANES)[None, :], dtype=BF16)
    pos_rep = jnp.repeat(positions.reshape(-1), ROT_HALF).reshape(-1, LANES)
    cos_p, sin_p = _rope_table_call(pos_rep, freq)
    rope_tab = jnp.concatenate([cos_p.reshape(B, S, ROT_HALF), sin_p.reshape(B, S, ROT_HALF)], axis=-1)

    for l in range(depth):
        mod = _ada_call(c_pad, w_ada[l], row(b_ada[l]))[:B].reshape(B, N_MOD, D)

        x = _ffn_call(x, mod, row(g_ffn1[l]), w1_gate[l].astype(BF16), w1_up[l].astype(BF16),
                      w1_down[l].astype(BF16))

        n_main = w_in.shape[2] - N_KV_GROUPS * 3 * HPG
        gate_cols = [jnp.pad(w_in[l][:, n_main + g * 3 * HPG:n_main + (g + 1) * 3 * HPG],
                             ((0, 0), (0, GATE_PAD - 3 * HPG))) for g in range(N_KV_GROUPS)]
        win = jnp.concatenate([w_in[l][:, :n_main]] + gate_cols, axis=1).astype(BF16)
        (yc, qT, kc, vc, ksa, vsT, kw, vwT, gatesT) = _inproj_call(
            x, rope_tab, mod, row(g_mix[l]), win, conv_w[l], row(g_out_conv[l]), gmat, onehot)

        both_groups = lambda pe: jnp.tile(pe, (1, N_KV_GROUPS))
        kcc, vcT = _compress_call(kc, vc, both_groups(cmp_pos_k[l]), both_groups(cmp_pos_v[l]),
                                  w_cmpk1[l], w_cmpk2[l], w_cmpv1[l], w_cmpv2[l])

        ya = _attn_call(min(SLC_TOP_N, n_slc), qT, gatesT, kcc, vcT, ksa, vsT, kw, vwT, ovlT,
                        g_out_attn[l].reshape(ATTN_WIDTH, 1))

        assert l == depth - 1, "final norm is fused into the last layer's output kernel"
        x = _out_call(x, yc, ya, mod, w_out[l].astype(BF16), row(g_ffn2[l]),
                      w2_gate[l].astype(BF16), w2_up[l].astype(BF16), w2_down[l].astype(BF16),
                      row(g_final))
    return x
```

```python
import functools
import math

import numpy as np
import jax
import jax.numpy as jnp
from jax import lax
from jax.experimental import pallas as pl
from jax.experimental.pallas import tpu as pltpu

F32 = jnp.float32
BF16 = jnp.bfloat16

CONV_CH = 512
CONV_GROUPS = 8
N_HEADS = 8
N_KV_GROUPS = 2
HPG = N_HEADS // N_KV_GROUPS
HEAD_DIM = 64
ATTN_WIDTH = N_HEADS * HEAD_DIM
KV_WIDTH = N_KV_GROUPS * HEAD_DIM
ROPE_THETA = 500000.0
ROT_DIM = HEAD_DIM // 4
ROT_HALF = ROT_DIM // 2
CMP_BLOCK = 32
CMP_STRIDE = 16
CMP_HIDDEN = 256
SLC_BLOCK = 64
SLC_TOP_N = 16
WINDOW = 512
Q_BLOCK = 256
MACARON_W = 0.5
N_MOD = 9
EPS = 1e-6
NEG = -1e30
BIG = 1e9

LANES = 128
MXU_DEPTH = 256
VMEM_LIMIT = 56 * 1024 * 1024

TOKEN_TILE = 512
FF_TILE = 256
KEY_CHUNK = 512
WIN_KEYS = WINDOW + Q_BLOCK
GATE_PAD = LANES
QW = HPG * Q_BLOCK
SEL_BLOCK = 512
ONES_ROWS = 16
VT_ROWS = HEAD_DIM + ONES_ROWS
ROW_TILE = 16
Q_SCALE = HEAD_DIM ** -0.5 * math.log2(math.e)
M_FLOOR = -1e20


def _dot(a, b):
    return jnp.dot(a, b, preferred_element_type=F32)


def _rms(x, g):
    return x * lax.rsqrt(jnp.mean(x * x, axis=-1, keepdims=True) + EPS) * g


def _split_bf16(x):
    hi = x.astype(BF16)
    lo = (x - hi.astype(F32)).astype(BF16)
    return hi, lo


def _const_spec(shape):
    nd = len(shape)
    return pl.BlockSpec(shape, lambda *_: (0,) * nd, pipeline_mode=pl.Buffered(1))


def _ada_kernel(c_ref, w_ref, b_ref, o_ref):
    c = c_ref[...]
    c_act = c * jax.nn.sigmoid(c)
    o_ref[...] = jnp.dot(c_act, w_ref[...], preferred_element_type=F32,
                         precision=lax.Precision.HIGHEST) + b_ref[...]


def _ada_call(c_pad, w_ada, b_ada):
    rows, d = c_pad.shape
    n = w_ada.shape[1]
    tn = 1024
    return pl.pallas_call(
        _ada_kernel,
        grid=(n // tn,),
        in_specs=[pl.BlockSpec((rows, d), lambda j: (0, 0)),
                  pl.BlockSpec((d, tn), lambda j: (0, j)),
                  pl.BlockSpec((1, tn), lambda j: (0, j))],
        out_specs=pl.BlockSpec((rows, tn), lambda j: (0, j)),
        out_shape=jax.ShapeDtypeStruct((rows, n), F32),
        compiler_params=pltpu.CompilerParams(dimension_semantics=("arbitrary",),
                                             vmem_limit_bytes=VMEM_LIMIT),
        name="adaln_mod",
    )(c_pad, w_ada, b_ada)


def _ffn_core(x, shift, scale, gate, g, wg_ref, wu_ref, wd_ref):
    h = _rms(x, g) * (1.0 + scale) + shift
    hb = h.astype(BF16)
    d_ff = wg_ref.shape[1]
    acc = None
    for j in range(d_ff // FF_TILE):
        sl = slice(j * FF_TILE, (j + 1) * FF_TILE)
        gg = _dot(hb, wg_ref[:, sl])
        uu = _dot(hb, wu_ref[:, sl])
        a = (gg * jax.nn.sigmoid(gg) * uu).astype(BF16)
        d = _dot(a, wd_ref[sl, :])
        acc = d if acc is None else acc + d
    return x + (MACARON_W * gate) * acc


def _ffn_kernel(x_ref, mod_ref, g_ref, wg_ref, wu_ref, wd_ref, o_ref):
    o_ref[...] = _ffn_core(x_ref[...], mod_ref[0:1, :], mod_ref[1:2, :], mod_ref[2:3, :],
                           g_ref[...], wg_ref, wu_ref, wd_ref)


def _ffn_call(x, mod, g, wg, wu, wd):
    B, S, D = x.shape
    tm = TOKEN_TILE
    return pl.pallas_call(
        _ffn_kernel,
        grid=(B, S // tm),
        in_specs=[pl.BlockSpec((None, tm, D), lambda b, i: (b, i, 0)),
                  pl.BlockSpec((None, N_MOD, D), lambda b, i: (b, 0, 0)),
                  _const_spec(g.shape), _const_spec(wg.shape), _const_spec(wu.shape),
                  _const_spec(wd.shape)],
        out_specs=pl.BlockSpec((None, tm, D), lambda b, i: (b, i, 0)),
        out_shape=jax.ShapeDtypeStruct((B, S, D), F32),
        compiler_params=pltpu.CompilerParams(dimension_semantics=("parallel", "parallel"),
                                             vmem_limit_bytes=VMEM_LIMIT),
        name="ffn1",
    )(x, mod, g, wg, wu, wd)


def _rope_table_kernel(pos_ref, freq_ref, cos_ref, sin_ref):
    ang = pos_ref[...].astype(F32) * freq_ref[...]
    cos_ref[...] = jnp.cos(ang)
    sin_ref[...] = jnp.sin(ang)


def _rope_table_call(pos_rep, freq):
    shape = jax.ShapeDtypeStruct(pos_rep.shape, F32)
    full = pl.BlockSpec(pos_rep.shape, lambda: (0, 0))
    return pl.pallas_call(
        _rope_table_kernel,
        in_specs=[full, pl.BlockSpec(freq.shape, lambda: (0, 0))],
        out_specs=(full, full),
        out_shape=(shape, shape),
        compiler_params=pltpu.CompilerParams(vmem_limit_bytes=VMEM_LIMIT),
        name="rope_table",
    )(pos_rep, freq)


def _inproj_kernel(x_ref, rope_ref, mod_ref, gmix_ref, win_ref, convw_ref, gconv_ref,
                   gmat_ref, onehot_ref, yc_ref, qT_ref, kc_ref, vc_ref, ksa_ref, vsT_ref, kw_ref,
                   vwT_ref, gT_ref, carry_ref):
    tm = x_ref.shape[0]

    @pl.when(pl.program_id(1) == 0)
    def _():
        carry_ref[...] = jnp.zeros_like(carry_ref)

    x = x_ref[...]
    h = _rms(x, gmix_ref[...]) * (1.0 + mod_ref[4:5, :]) + mod_ref[3:4, :]
    hb = h.astype(BF16)

    def proj(c0, width):
        return _dot(hb, win_ref[:, c0:c0 + width])

    cb = proj(0, CONV_CH)
    u = proj(CONV_CH, CONV_CH) * proj(2 * CONV_CH, CONV_CH)
    row = lax.broadcasted_iota(jnp.int32, (tm, 1), 0)
    prev1 = carry_ref[7:8, :]
    prev2 = carry_ref[6:7, :]
    u1 = jnp.where(row >= 1, pltpu.roll(u, 1, 0), prev1)
    u2 = jnp.where(row >= 2, pltpu.roll(u, 2, 0), jnp.where(row == 1, prev1, prev2))
    carry_ref[...] = u[tm - 8:tm, :]
    v = convw_ref[0:1, :] * u2 + convw_ref[1:2, :] * u1 + convw_ref[2:3, :] * u
    y = cb * v
    hi, lo = _split_bf16(y * y)
    ms = _dot(hi, gmat_ref[...]) + _dot(lo, gmat_ref[...])
    yc_ref[...] = (y * lax.rsqrt(ms + EPS) * gconv_ref[...]).astype(yc_ref.dtype)

    tab = jnp.concatenate([rope_ref[...], jnp.zeros((tm, LANES - 2 * ROT_HALF), F32)], axis=1)
    d = lax.broadcasted_iota(jnp.int32, (tm, LANES), 1) & (HEAD_DIM - 1)
    fidx = d & (ROT_HALF - 1)
    cos_t = jnp.where(d < ROT_DIM, jnp.take_along_axis(tab, fidx, axis=1), 1.0)
    sin_raw = jnp.take_along_axis(tab, fidx + ROT_HALF, axis=1)
    sin_t = jnp.where(d < ROT_HALF, -sin_raw, jnp.where(d < ROT_DIM, sin_raw, 0.0))
    first_half = d < ROT_HALF

    def rope(t):
        outs = []
        for j in range(t.shape[1] // LANES):
            tj = t[:, j * LANES:(j + 1) * LANES]
            partner = jnp.where(first_half, pltpu.roll(tj, LANES - ROT_HALF, 1),
                                pltpu.roll(tj, ROT_HALF, 1))
            outs.append(tj * cos_t + partner * sin_t)
        return outs[0] if len(outs) == 1 else jnp.concatenate(outs, axis=1)

    c0 = 3 * CONV_CH
    q = rope(proj(c0, ATTN_WIDTH)) * Q_SCALE
    qT_ref[...] = q.T.astype(qT_ref.dtype)
    c0 += ATTN_WIDTH
    kv = proj(c0, 2 * KV_WIDTH)
    kc_ref[...] = rope(kv[:, :KV_WIDTH])
    vc_ref[...] = kv[:, KV_WIDTH:]
    kv = proj(c0 + 2 * KV_WIDTH, 2 * KV_WIDTH)
    ks = rope(kv[:, :KV_WIDTH]).astype(BF16)
    vsT = kv[:, KV_WIDTH:].T.astype(BF16)
    kv = proj(c0 + 4 * KV_WIDTH, 2 * KV_WIDTH)
    kw = rope(kv[:, :KV_WIDTH]).astype(BF16)
    vwT = kv[:, KV_WIDTH:].T.astype(BF16)
    pad = jnp.zeros((tm, MXU_DEPTH - LANES - HEAD_DIM), BF16)
    ones = jnp.ones((ONES_ROWS, tm), BF16)
    for g in range(N_KV_GROUPS):
        kg = ks[:, g * HEAD_DIM:(g + 1) * HEAD_DIM]
        ksa_ref[g] = jnp.concatenate([onehot_ref[...], kg, pad], axis=1)
        kw_ref[g] = kw[:, g * HEAD_DIM:(g + 1) * HEAD_DIM]
        vsT_ref[g] = jnp.concatenate([vsT[g * HEAD_DIM:(g + 1) * HEAD_DIM, :], ones], axis=0)
        vwT_ref[g] = jnp.concatenate([vwT[g * HEAD_DIM:(g + 1) * HEAD_DIM, :], ones], axis=0)
    gT_ref[...] = jax.nn.sigmoid(proj(c0 + 6 * KV_WIDTH, N_KV_GROUPS * GATE_PAD)).T


def _inproj_call(x, rope_tab, mod, gmix, win, convw, gconv, gmat, onehot):
    B, S, D = x.shape
    tm = TOKEN_TILE
    tok = lambda w: pl.BlockSpec((None, tm, w), lambda b, i: (b, i, 0))
    tr = lambda w: pl.BlockSpec((None, w, tm), lambda b, i: (b, 0, i))
    grp = lambda w: pl.BlockSpec((None, N_KV_GROUPS, tm, w), lambda b, i: (b, 0, i, 0))
    grpT = pl.BlockSpec((None, N_KV_GROUPS, VT_ROWS, tm), lambda b, i: (b, 0, 0, i))
    out_shapes = (
        jax.ShapeDtypeStruct((B, S, CONV_CH), BF16),
        jax.ShapeDtypeStruct((B, ATTN_WIDTH, S), BF16),
        jax.ShapeDtypeStruct((B, S, KV_WIDTH), F32),
        jax.ShapeDtypeStruct((B, S, KV_WIDTH), F32),
        jax.ShapeDtypeStruct((B, N_KV_GROUPS, S, MXU_DEPTH), BF16),
        jax.ShapeDtypeStruct((B, N_KV_GROUPS, VT_ROWS, S), BF16),
        jax.ShapeDtypeStruct((B, N_KV_GROUPS, S, HEAD_DIM), BF16),
        jax.ShapeDtypeStruct((B, N_KV_GROUPS, VT_ROWS, S), BF16),
        jax.ShapeDtypeStruct((B, N_KV_GROUPS * GATE_PAD, S), F32),
    )
    return pl.pallas_call(
        _inproj_kernel,
        grid=(B, S // tm),
        in_specs=[tok(D),
                  tok(rope_tab.shape[2]),
                  pl.BlockSpec((None, N_MOD, D), lambda b, i: (b, 0, 0)),
                  _const_spec(gmix.shape), _const_spec(win.shape), _const_spec(convw.shape),
                  _const_spec(gconv.shape), _const_spec(gmat.shape),
                  pl.BlockSpec((tm, onehot.shape[1]), lambda b, i: (i, 0))],
        out_specs=(tok(CONV_CH), tr(ATTN_WIDTH), tok(KV_WIDTH), tok(KV_WIDTH), grp(MXU_DEPTH),
                   grpT, grp(HEAD_DIM), grpT, tr(N_KV_GROUPS * GATE_PAD)),
        out_shape=out_shapes,
        scratch_shapes=[pltpu.VMEM((8, CONV_CH), F32)],
        compiler_params=pltpu.CompilerParams(dimension_semantics=("arbitrary", "arbitrary"),
                                             vmem_limit_bytes=VMEM_LIMIT),
        name="mixer_inproj",
    )(x, rope_tab, mod, gmix, win, convw, gconv, gmat, onehot)


def _compress_kernel(kf_ref, vf_ref, pek_ref, pev_ref, wk1_ref, wk2_ref, wv1_ref, wv2_ref,
                     kcc_ref, vcT_ref):
    half = CMP_BLOCK // 2
    n = kf_ref.shape[0] // half

    def mlp(x_ref, pe_ref, w1_ref, w2_ref):
        parts = []
        for p in range(2):
            acc = None
            for l0 in range(0, half, 2):
                xs, ws = [], []
                for l in (l0, l0 + 1):
                    row = p * half + l
                    xs.append((x_ref[pl.ds(l, n, stride=half), :]
                               + pe_ref[row:row + 1, :]).astype(BF16))
                    w = w1_ref[row * HEAD_DIM:(row + 1) * HEAD_DIM, :].astype(BF16)
                    z = jnp.zeros_like(w)
                    ws.append(jnp.concatenate([jnp.concatenate([w, z], axis=1),
                                               jnp.concatenate([z, w], axis=1)], axis=0))
                d = _dot(jnp.concatenate(xs, axis=1), jnp.concatenate(ws, axis=0))
                acc = d if acc is None else acc + d
            parts.append(acc)
        hpre = parts[0] + pltpu.roll(parts[1], n - 1, 0)
        hid = (hpre * jax.nn.sigmoid(hpre)).astype(BF16)
        w2 = w2_ref[...].astype(BF16)
        return jnp.concatenate([_dot(hid[:, g * CMP_HIDDEN:(g + 1) * CMP_HIDDEN], w2)
                                for g in range(N_KV_GROUPS)], axis=1)

    kc = mlp(kf_ref, pek_ref, wk1_ref, wk2_ref).astype(kcc_ref.dtype)
    for g in range(N_KV_GROUPS):
        kcc_ref[g] = kc[:, g * HEAD_DIM:(g + 1) * HEAD_DIM]
    vcT = mlp(vf_ref, pev_ref, wv1_ref, wv2_ref).T.astype(vcT_ref.dtype)
    ones = jnp.ones((ONES_ROWS, n), vcT_ref.dtype)
    for g in range(N_KV_GROUPS):
        vcT_ref[g] = jnp.concatenate([vcT[g * HEAD_DIM:(g + 1) * HEAD_DIM, :], ones], axis=0)


def _compress_call(kf, vf, pek, pev, wk1, wk2, wv1, wv2):
    B, S, width = kf.shape
    n = S // (CMP_BLOCK // 2)
    flat = pl.BlockSpec((None, S, width), lambda b: (b, 0, 0))
    return pl.pallas_call(
        _compress_kernel,
        grid=(B,),
        in_specs=[flat, flat, _const_spec(pek.shape), _const_spec(pev.shape),
                  _const_spec(wk1.shape), _const_spec(wk2.shape), _const_spec(wv1.shape),
                  _const_spec(wv2.shape)],
        out_specs=(pl.BlockSpec((None, N_KV_GROUPS, n, HEAD_DIM), lambda b: (b, 0, 0, 0)),
                   pl.BlockSpec((None, N_KV_GROUPS, VT_ROWS, n), lambda b: (b, 0, 0, 0))),
        out_shape=(jax.ShapeDtypeStruct((B, N_KV_GROUPS, n, HEAD_DIM), BF16),
                   jax.ShapeDtypeStruct((B, N_KV_GROUPS, VT_ROWS, n), BF16)),
        compiler_params=pltpu.CompilerParams(dimension_semantics=("arbitrary",),
                                             vmem_limit_bytes=VMEM_LIMIT),
        name="kv_compress",
    )(kf, vf, pek, pev, wk1, wk2, wv1, wv2)


def _lane_tiles(x, n):
    return jnp.concatenate([x] * n, axis=1)


def _col_max(s_ref, n_rows, bias_fn):
    groups = ROW_TILE // 8
    mx = [jnp.full((8, s_ref.shape[1]), NEG, F32)] * groups
    for r in range(0, n_rows, ROW_TILE):
        x = s_ref[r:r + ROW_TILE, :]
        if bias_fn is not None:
            x = x + bias_fn(r)
            s_ref[r:r + ROW_TILE, :] = x
        mx = [jnp.maximum(mx[i], x[8 * i:8 * (i + 1), :]) for i in range(groups)]
    while len(mx) > 1:
        mx = [jnp.maximum(a, b) for a, b in zip(mx[0::2], mx[1::2])]
    return jnp.max(mx[0], axis=0, keepdims=True)


def _col_exp2(s_ref, p_ref, n_rows, m, keep_f32=False):
    for r in range(0, n_rows, ROW_TILE):
        p = jnp.exp2(s_ref[r:r + ROW_TILE, :] - m)
        if keep_f32:
            s_ref[r:r + ROW_TILE, :] = p
        p_ref[r:r + ROW_TILE, :] = p.astype(p_ref.dtype)


def _recip_pos(l):
    return 1.0 / jnp.where(l > 0.0, l, 1.0)


def _select_kernel(top_n, qT_ref, kcc_ref, vcT_ref, ovlT_ref, sel_ref, ocmp_ref, sc_ref, pc_ref,
                   ph_ref, pl_ref):
    QB = SEL_BLOCK
    n_cmp = kcc_ref.shape[0]
    n_slc = ovlT_ref.shape[0]
    t0 = pl.program_id(2) * QB
    tq = t0 + lax.broadcasted_iota(jnp.int32, (1, QB), 1)
    rows = lax.broadcasted_iota(jnp.int32, (ROW_TILE, 1), 0)

    qT = qT_ref[...]
    qcat = jnp.concatenate([qT[h * HEAD_DIM:(h + 1) * HEAD_DIM, :] for h in range(HPG)], axis=1)
    sc_ref[...] = _dot(kcc_ref[...], qcat)

    def cmp_bias(r):
        cmp_end = (r + rows) * CMP_STRIDE + (CMP_BLOCK - 1)
        return _lane_tiles(jnp.where(cmp_end <= tq, 0.0, NEG), HPG)

    m = jnp.maximum(_col_max(sc_ref, n_cmp, cmp_bias), M_FLOOR)
    _col_exp2(sc_ref, pc_ref, n_cmp, m, keep_f32=True)
    o_cmp = _dot(vcT_ref[...], pc_ref[...])
    rl = _recip_pos(o_cmp[HEAD_DIM:HEAD_DIM + 1, :])
    o_cmp = o_cmp[0:HEAD_DIM, :] * rl

    for r in range(0, n_cmp, ROW_TILE):
        pn = sc_ref[r:r + ROW_TILE, :] * rl
        psum = pn[:, 0:QB]
        for h in range(1, HPG):
            psum = psum + pn[:, h * QB:(h + 1) * QB]
        hi, lo = _split_bf16(psum)
        ph_ref[r:r + ROW_TILE, :] = hi
        pl_ref[r:r + ROW_TILE, :] = lo
    imp = _dot(ovlT_ref[...], ph_ref[...]) + _dot(ovlT_ref[...], pl_ref[...])
    for h in range(HPG):
        ocmp_ref[h] = o_cmp[:, h * QB:(h + 1) * QB]

    blk = lax.broadcasted_iota(jnp.int32, (n_slc, 1), 0)
    cur = tq // SLC_BLOCK
    future = blk > cur
    forced = (blk == 0) | (blk == cur) | (blk == cur - 1)
    score0 = jnp.where(future, -BIG, jnp.where(forced, BIG, imp))
    score = score0
    cum = jnp.zeros((1, QB), F32)
    thr = jnp.zeros((1, QB), F32)
    above = jnp.zeros((1, QB), F32)
    for _ in range(top_n):
        best = jnp.max(score, axis=0, keepdims=True)
        eq = score == best
        unfilled = cum < top_n
        thr = jnp.where(unfilled, best, thr)
        above = jnp.where(unfilled, cum, above)
        cum = cum + jnp.sum(jnp.where(eq, 1.0, 0.0), axis=0, keepdims=True)
        score = jnp.where(eq, -jnp.inf, score)
    ties = score0 == thr
    lower = jnp.where(lax.broadcasted_iota(jnp.int32, (1, n_slc), 1) < blk, 1.0, 0.0).astype(BF16)
    rank = _dot(lower, jnp.where(ties, 1.0, 0.0).astype(BF16))
    picked = (score0 > thr) | (ties & (rank < top_n - above))
    sel_ref[...] = jnp.where(future, NEG, jnp.where(picked, 0.0, NEG)).astype(sel_ref.dtype)


def _select_call(top_n, qT, kcc, vcT, ovlT):
    B, _, S = qT.shape
    n_cmp = kcc.shape[2]
    n_slc = ovlT.shape[0]
    gw = HPG * HEAD_DIM
    qw = HPG * SEL_BLOCK
    return pl.pallas_call(
        functools.partial(_select_kernel, top_n),
        grid=(B, N_KV_GROUPS, S // SEL_BLOCK),
        in_specs=[pl.BlockSpec((None, gw, SEL_BLOCK), lambda b, g, i: (b, g, i)),
                  pl.BlockSpec((None, None, n_cmp, HEAD_DIM), lambda b, g, i: (b, g, 0, 0)),
                  pl.BlockSpec((None, None, VT_ROWS, n_cmp), lambda b, g, i: (b, g, 0, 0)),
                  pl.BlockSpec(ovlT.shape, lambda b, g, i: (0, 0))],
        out_specs=(pl.BlockSpec((None, None, n_slc, SEL_BLOCK), lambda b, g, i: (b, g, 0, i)),
                   pl.BlockSpec((None, None, HPG, HEAD_DIM, SEL_BLOCK),
                                lambda b, g, i: (b, g, 0, 0, i))),
        out_shape=(jax.ShapeDtypeStruct((B, N_KV_GROUPS, n_slc, S), BF16),
                   jax.ShapeDtypeStruct((B, N_KV_GROUPS, HPG, HEAD_DIM, S), F32)),
        scratch_shapes=[pltpu.VMEM((n_cmp, qw), F32), pltpu.VMEM((n_cmp, qw), BF16),
                        pltpu.VMEM((n_cmp, SEL_BLOCK), BF16), pltpu.VMEM((n_cmp, SEL_BLOCK), BF16)],
        compiler_params=pltpu.CompilerParams(
            dimension_semantics=("parallel", "parallel", "arbitrary"),
            vmem_limit_bytes=VMEM_LIMIT),
        name="nsa_select",
    )(qT, kcc, vcT, ovlT)


def _attn_kernel(qT_ref, gT_ref, sel_ref, ocmp_ref, ksa_ref, vsT_ref, kw_ref, vwT_ref, ga_ref,
                 o_ref, s0_ref, s1_ref, p0_ref, p1_ref, sw_ref, pw_ref, rhs_ref):
    QB = Q_BLOCK
    n_slc = sel_ref.shape[0]
    t0 = pl.program_id(2) * QB
    tq = t0 + lax.broadcasted_iota(jnp.int32, (1, QB), 1)
    rows = lax.broadcasted_iota(jnp.int32, (ROW_TILE, 1), 0)

    qT = qT_ref[...]
    qcat = jnp.concatenate([qT[h * HEAD_DIM:(h + 1) * HEAD_DIM, :] for h in range(HPG)], axis=1)

    rhs_ref[...] = jnp.concatenate([_lane_tiles(sel_ref[...], HPG), qcat,
                                    jnp.zeros((MXU_DEPTH - n_slc - HEAD_DIM, QW), BF16)], axis=0)

    w0 = pl.multiple_of(jnp.maximum(t0 - WINDOW, 0), QB)
    sw_ref[...] = _dot(kw_ref[pl.ds(w0, WIN_KEYS), :], qcat)

    def win_bias(r):
        dist = tq - (w0 + r + rows)
        return _lane_tiles(jnp.where((dist >= 0) & (dist < WINDOW), 0.0, NEG), HPG)

    KC = KEY_CHUNK
    last = t0 // KC

    def scores(c, dst_ref):
        k0 = pl.multiple_of(jnp.minimum(c, last) * KC, KC)
        dst_ref[...] = _dot(ksa_ref[pl.ds(k0, KC), :], rhs_ref[...])

    def weighted_values(c, p_ref):
        k0 = pl.multiple_of(jnp.clip(c, 0, last) * KC, KC)
        return _dot(vsT_ref[:, pl.ds(k0, KC)], p_ref[...])

    def softmax(c, src_ref, p_ref, m, acc, causal):
        def causal_bias(r):
            return _lane_tiles(jnp.where(c * KC + r + rows <= tq, 0.0, NEG), HPG)

        m_new = jnp.maximum(m, _col_max(src_ref, KC, causal_bias if causal else None))
        _col_exp2(src_ref, p_ref, KC, m_new)
        return m_new, jnp.exp2(m - m_new) * acc

    def pair(i, carry):
        m, acc = carry
        pending = weighted_values(2 * i - 1, p1_ref)
        scores(2 * i + 1, s1_ref)
        m, acc = softmax(2 * i, s0_ref, p0_ref, m, acc + pending, False)
        pending = weighted_values(2 * i, p0_ref)
        scores(2 * i + 2, s0_ref)
        return softmax(2 * i + 1, s1_ref, p1_ref, m, acc + pending, True)

    p1_ref[...] = jnp.zeros_like(p1_ref)
    scores(0, s0_ref)

    m = _col_max(sw_ref, WIN_KEYS, win_bias)
    _col_exp2(sw_ref, pw_ref, WIN_KEYS, m)
    o_win = _dot(vwT_ref[:, pl.ds(w0, WIN_KEYS)], pw_ref[...])
    o_win = o_win[0:HEAD_DIM, :] * (1.0 / o_win[HEAD_DIM:HEAD_DIM + 1, :])

    init =(jnp.full((1, QW), NEG, F32), jnp.zeros((VT_ROWS, QW), F32))
    n_pairs = (last + 1) // 2
    m, acc = lax.fori_loop(0, n_pairs, pair, init)
    acc = acc + weighted_values(2 * n_pairs - 1, p1_ref)

    def diagonal(carry):
        m, acc = softmax(last, s0_ref, p0_ref, carry[0], carry[1], True)
        return m, acc + weighted_values(last, p0_ref)

    _, acc = lax.cond(last % 2 == 0, diagonal, lambda cr: cr, (m, acc))
    o_slc = acc[0:HEAD_DIM, :] * (1.0 / acc[HEAD_DIM:HEAD_DIM + 1, :])

    gts = gT_ref[...]
    outs = []
    for h in range(HPG):
        sl = slice(h * QB, (h + 1) * QB)
        o = (gts[3 * h:3 * h + 1, :] * ocmp_ref[h] + gts[3 * h + 1:3 * h + 2, :] * o_slc[:, sl]
             + gts[3 * h + 2:3 * h + 3, :] * o_win[:, sl])
        o = o * lax.rsqrt(jnp.mean(o * o, axis=0, keepdims=True) + EPS)
        outs.append((o * ga_ref[h * HEAD_DIM:(h + 1) * HEAD_DIM, :]).T)
    o_ref[...] = jnp.concatenate(outs, axis=1).astype(o_ref.dtype)


def _attn_call(qT, gatesT, sel_bias, o_cmp, ksa, vsT, kw, vwT, g_attn_col):
    B, _, S = qT.shape
    n_slc = sel_bias.shape[2]
    gw = HPG * HEAD_DIM
    rows = lambda n, w: pl.BlockSpec((None, None, n, w), lambda b, g, i: (b, g, 0, 0))
    cols = lambda n: pl.BlockSpec((None, None, VT_ROWS, n), lambda b, g, i: (b, g, 0, 0))
    return pl.pallas_call(
        _attn_kernel,
        grid=(B, N_KV_GROUPS, S // Q_BLOCK),
        in_specs=[pl.BlockSpec((None, gw, Q_BLOCK), lambda b, g, i: (b, g, i)),
                  pl.BlockSpec((None, GATE_PAD, Q_BLOCK), lambda b, g, i: (b, g, i)),
                  pl.BlockSpec((None, None, n_slc, Q_BLOCK), lambda b, g, i: (b, g, 0, i)),
                  pl.BlockSpec((None, None, HPG, HEAD_DIM, Q_BLOCK), lambda b, g, i: (b, g, 0, 0, i)),
                  rows(S, MXU_DEPTH), cols(S), rows(S, HEAD_DIM), cols(S),
                  pl.BlockSpec((gw, 1), lambda b, g, i: (g, 0))],
        out_specs=pl.BlockSpec((None, Q_BLOCK, gw), lambda b, g, i: (b, i, g)),
        out_shape=jax.ShapeDtypeStruct((B, S, ATTN_WIDTH), BF16),
        scratch_shapes=[pltpu.VMEM((KEY_CHUNK, QW), F32), pltpu.VMEM((KEY_CHUNK, QW), F32),
                        pltpu.VMEM((KEY_CHUNK, QW), BF16), pltpu.VMEM((KEY_CHUNK, QW), BF16),
                        pltpu.VMEM((WIN_KEYS, QW), F32), pltpu.VMEM((WIN_KEYS, QW), BF16),
                        pltpu.VMEM((MXU_DEPTH, QW), BF16)],
        compiler_params=pltpu.CompilerParams(
            dimension_semantics=("parallel", "parallel", "arbitrary"),
            vmem_limit_bytes=VMEM_LIMIT),
        name="nsa_attention",
    )(qT, gatesT, sel_bias, o_cmp, ksa, vsT, kw, vwT, g_attn_col)


def _out_kernel(x_ref, yc_ref, ya_ref, mod_ref, wo_ref, g_ref, wg_ref, wu_ref, wd_ref, gf_ref,
                o_ref):
    mix = _dot(yc_ref[...], wo_ref[0:CONV_CH, :]) + _dot(ya_ref[...], wo_ref[CONV_CH:, :])
    x = x_ref[...] + mod_ref[5:6, :] * mix
    x = _ffn_core(x, mod_ref[6:7, :], mod_ref[7:8, :], mod_ref[8:9, :], g_ref[...], wg_ref,
                  wu_ref, wd_ref)
    o_ref[...] = _rms(x, gf_ref[...])


def _out_call(x, yc, ya, mod, wo, g, wg, wu, wd, gf):
    B, S, D = x.shape
    tm = TOKEN_TILE
    tok = lambda w: pl.BlockSpec((None, tm, w), lambda b, i: (b, i, 0))
    return pl.pallas_call(
        _out_kernel,
        grid=(B, S // tm),
        in_specs=[tok(D), tok(CONV_CH), tok(ATTN_WIDTH),
                  pl.BlockSpec((None, N_MOD, D), lambda b, i: (b, 0, 0)),
                  _const_spec(wo.shape), _const_spec(g.shape), _const_spec(wg.shape),
                  _const_spec(wu.shape), _const_spec(wd.shape), _const_spec(gf.shape)],
        out_specs=tok(D),
        out_shape=jax.ShapeDtypeStruct((B, S, D), F32),
        compiler_params=pltpu.CompilerParams(dimension_semantics=("parallel", "parallel"),
                                             vmem_limit_bytes=VMEM_LIMIT),
        name="outproj_ffn2",
    )(x, yc, ya, mod, wo, g, wg, wu, wd, gf)


def kernel(x, c, positions, w_ada, b_ada, g_ffn1, w1_gate, w1_up, w1_down, g_mix, w_in, conv_w, cmp_pos_k, cmp_pos_v, w_cmpk1, w_cmpk2, w_cmpv1, w_cmpv2, g_out_conv, g_out_attn, w_out, g_ffn2, w2_gate, w2_up, w2_down, g_final):
    B, S, D = x.shape
    depth = w_ada.shape[0]
    n_slc = S // SLC_BLOCK
    half = CMP_BLOCK // 2
    n_half = S // half
    assert n_slc <= LANES, "selection-block one-hot is one lane tile wide"

    c_pad = jnp.pad(c, ((0, 8 - B), (0, 0)))
    row = lambda a: a.reshape(1, -1)

    freq_half = jnp.power(ROPE_THETA, -2.0 * jnp.arange(ROT_HALF, dtype=F32) / ROT_DIM)
    freq = jnp.tile(freq_half, LANES // ROT_HALF).reshape(1, LANES)
    gidx = np.arange(CONV_CH) // (CONV_CH // CONV_GROUPS)
    gmat = jnp.asarray((gidx[:, None] == gidx[None, :]) / (CONV_CH // CONV_GROUPS), dtype=BF16)
    c0 = np.arange(n_half) * CMP_STRIDE
    s0 = np.arange(LANES) * SLC_BLOCK
    ovlT = ((c0[None, :] <= s0[:, None] + SLC_BLOCK - 1) & (c0[None, :] + CMP_BLOCK - 1 >= s0[:, None]))
    ovlT = jnp.asarray(ovlT, dtype=BF16)
    onehot = jnp.asarray((np.arange(S) // SLC_BLOCK)[:, None] == np.arange(LANES)[None, :], dtype=BF16)
    pos_rep = jnp.repeat(positions.reshape(-1), ROT_HALF).reshape(-1, LANES)
    cos_p, sin_p = _rope_table_call(pos_rep, freq)
    rope_tab = jnp.concatenate([cos_p.reshape(B, S, ROT_HALF), sin_p.reshape(B, S, ROT_HALF)], axis=-1)

    for l in range(depth):
        mod = _ada_call(c_pad, w_ada[l], row(b_ada[l]))[:B].reshape(B, N_MOD, D)

        x = _ffn_call(x, mod, row(g_ffn1[l]), w1_gate[l].astype(BF16), w1_up[l].astype(BF16),
                      w1_down[l].astype(BF16))

        n_main = w_in.shape[2] - N_KV_GROUPS * 3 * HPG
        gate_cols = [jnp.pad(w_in[l][:, n_main + g * 3 * HPG:n_main + (g + 1) * 3 * HPG],
                             ((0, 0), (0, GATE_PAD - 3 * HPG))) for g in range(N_KV_GROUPS)]
        win = jnp.concatenate([w_in[l][:, :n_main]] + gate_cols, axis=1).astype(BF16)
        (yc, qT, kc, vc, ksa, vsT, kw, vwT, gatesT) = _inproj_call(
            x, rope_tab, mod, row(g_mix[l]), win, conv_w[l], row(g_out_conv[l]), gmat, onehot)

        both_groups = lambda pe: jnp.tile(pe, (1, N_KV_GROUPS))
        kcc, vcT = _compress_call(kc, vc, both_groups(cmp_pos_k[l]), both_groups(cmp_pos_v[l]),
                                  w_cmpk1[l], w_cmpk2[l], w_cmpv1[l], w_cmpv2[l])

        sel_bias, o_cmp = _select_call(min(SLC_TOP_N, n_slc), qT, kcc, vcT, ovlT)
        ya = _attn_call(qT, gatesT, sel_bias, o_cmp, ksa, vsT, kw, vwT,
                        g_out_attn[l].reshape(ATTN_WIDTH, 1))

        assert l == depth - 1, "final norm is fused into the last layer's output kernel"
        x = _out_call(x, yc, ya, mod, w_out[l].astype(BF16), row(g_ffn2[l]),
                      w2_gate[l].astype(BF16), w2_up[l].astype(BF16), w2_down[l].astype(BF16),
                      row(g_final))
    return x
```

```python
import functools
import math

import numpy as np
import jax
import jax.numpy as jnp
from jax import lax
from jax.experimental import pallas as pl
from jax.experimental.pallas import tpu as pltpu

F32 = jnp.float32
BF16 = jnp.bfloat16

CONV_CH = 512
CONV_GROUPS = 8
N_HEADS = 8
N_KV_GROUPS = 2
HPG = N_HEADS // N_KV_GROUPS
HEAD_DIM = 64
ATTN_WIDTH = N_HEADS * HEAD_DIM
KV_WIDTH = N_KV_GROUPS * HEAD_DIM
ROPE_THETA = 500000.0
ROT_DIM = HEAD_DIM // 4
ROT_HALF = ROT_DIM // 2
CMP_BLOCK = 32
CMP_STRIDE = 16
CMP_HIDDEN = 256
SLC_BLOCK = 64
SLC_TOP_N = 16
WINDOW = 512
Q_BLOCK = 256
MACARON_W = 0.5
N_MOD = 9
EPS = 1e-6
NEG = -1e30
BIG = 1e9

LANES = 128
MXU_DEPTH = 256
VMEM_LIMIT = 56 * 1024 * 1024

TOKEN_TILE = 512
FFN_TOKEN_TILE = 1024
ADA_COL_TILE = 2304
FF_TILE = 256
KEY_CHUNK = 512
WIN_KEYS = WINDOW + Q_BLOCK
TOKENS_PER_ROW = LANES // ROT_HALF
GATE_PAD = LANES
QW = HPG * Q_BLOCK
SEL_BLOCK = 512
SEL_ROW_STEP = 32
ROW_GROUP = 64
ONES_ROWS = 16
VT_ROWS = HEAD_DIM + ONES_ROWS
ROW_TILE = 16
Q_SCALE = HEAD_DIM ** -0.5 * math.log2(math.e)
M_FLOOR = -1e20


def _dot(a, b):
    return jnp.dot(a, b, preferred_element_type=F32)


def _rms(x, g):
    return x * lax.rsqrt(jnp.mean(x * x, axis=-1, keepdims=True) + EPS) * g


def _split_bf16(x):
    hi = x.astype(BF16)
    lo = (x - hi.astype(F32)).astype(BF16)
    return hi, lo


def _const_spec(shape):
    nd = len(shape)
    return pl.BlockSpec(shape, lambda *_: (0,) * nd, pipeline_mode=pl.Buffered(1))


def _ada_kernel(c_ref, w_ref, b_ref, o_ref):
    c = c_ref[...]
    c_act = c * jax.nn.sigmoid(c)
    o_ref[...] = jnp.dot(c_act, w_ref[...], preferred_element_type=F32,
                         precision=lax.Precision.HIGHEST) + b_ref[...]


def _ada_call(c_pad, w_ada, b_ada):
    rows, d = c_pad.shape
    n = w_ada.shape[1]
    tn = ADA_COL_TILE
    return pl.pallas_call(
        _ada_kernel,
        grid=(n // tn,),
        in_specs=[pl.BlockSpec((rows, d), lambda j: (0, 0)),
                  pl.BlockSpec((d, tn), lambda j: (0, j)),
                  pl.BlockSpec((1, tn), lambda j: (0, j))],
        out_specs=pl.BlockSpec((rows, tn), lambda j: (0, j)),
        out_shape=jax.ShapeDtypeStruct((rows, n), F32),
        compiler_params=pltpu.CompilerParams(dimension_semantics=("arbitrary",),
                                             vmem_limit_bytes=VMEM_LIMIT),
        name="adaln_mod",
    )(c_pad, w_ada, b_ada)


def _ffn_core(x, shift, scale, gate, g, wg_ref, wu_ref, wd_ref):
    h = _rms(x, g) * (1.0 + scale) + shift
    hb = h.astype(BF16)
    d_ff = wg_ref.shape[1]
    acc = None
    for j in range(d_ff // FF_TILE):
        sl = slice(j * FF_TILE, (j + 1) * FF_TILE)
        gg = _dot(hb, wg_ref[:, sl])
        uu = _dot(hb, wu_ref[:, sl])
        a = (gg * jax.nn.sigmoid(gg) * uu).astype(BF16)
        d = _dot(a, wd_ref[sl, :])
        acc = d if acc is None else acc + d
    return x + (MACARON_W * gate) * acc


def _ffn_kernel(x_ref, mod_ref, g_ref, wg_ref, wu_ref, wd_ref, o_ref):
    o_ref[...] = _ffn_core(x_ref[...], mod_ref[0:1, :], mod_ref[1:2, :], mod_ref[2:3, :],
                           g_ref[...], wg_ref, wu_ref, wd_ref)


def _ffn_call(x, mod, g, wg, wu, wd):
    B, S, D = x.shape
    tm = FFN_TOKEN_TILE
    return pl.pallas_call(
        _ffn_kernel,
        grid=(B, S // tm),
        in_specs=[pl.BlockSpec((None, tm, D), lambda b, i: (b, i, 0)),
                  pl.BlockSpec((None, N_MOD, D), lambda b, i: (b, 0, 0)),
                  _const_spec(g.shape), _const_spec(wg.shape), _const_spec(wu.shape),
                  _const_spec(wd.shape)],
        out_specs=pl.BlockSpec((None, tm, D), lambda b, i: (b, i, 0)),
        out_shape=jax.ShapeDtypeStruct((B, S, D), F32),
        compiler_params=pltpu.CompilerParams(dimension_semantics=("parallel", "parallel"),
                                             vmem_limit_bytes=VMEM_LIMIT),
        name="ffn1",
    )(x, mod, g, wg, wu, wd)


def _rope_table_kernel(pos_ref, freq_ref, cos_ref, sin_ref):
    pos = jnp.concatenate([pos_ref[...].astype(F32),
                           jnp.zeros((pos_ref.shape[0], LANES - TOKENS_PER_ROW), F32)], axis=1)
    src = lax.broadcasted_iota(jnp.int32, pos.shape, 1) // ROT_HALF
    ang = jnp.take_along_axis(pos, src, axis=1) * freq_ref[...]
    cos_ref[...] = jnp.cos(ang)
    sin_ref[...] = jnp.sin(ang)


def _rope_table_call(pos_rows, freq):
    rows = pos_rows.shape[0]
    shape = jax.ShapeDtypeStruct((rows, LANES), F32)
    full = pl.BlockSpec((rows, LANES), lambda: (0, 0))
    return pl.pallas_call(
        _rope_table_kernel,
        in_specs=[pl.BlockSpec(pos_rows.shape, lambda: (0, 0)), pl.BlockSpec(freq.shape, lambda: (0, 0))],
        out_specs=(full, full),
        out_shape=(shape, shape),
        compiler_params=pltpu.CompilerParams(vmem_limit_bytes=VMEM_LIMIT),
        name="rope_table",
    )(pos_rows, freq)


def _inproj_kernel(x_ref, cosp_ref, sinp_ref, mod_ref, gmix_ref, win_ref, convw_ref, gconv_ref,
                   gmat_ref, onehot_ref, yc_ref, qT_ref, kc_ref, vc_ref, ksa_ref, vsT_ref, kw_ref,
                   vwT_ref, gT_ref, carry_ref):
    tm = x_ref.shape[0]

    @pl.when(pl.program_id(1) == 0)
    def _():
        carry_ref[...] = jnp.zeros_like(carry_ref)

    x = x_ref[...]
    h = _rms(x, gmix_ref[...]) * (1.0 + mod_ref[4:5, :]) + mod_ref[3:4, :]
    hb = h.astype(BF16)

    def proj(c0, width):
        return _dot(hb, win_ref[:, c0:c0 + width])

    cb = proj(0, CONV_CH)
    u = proj(CONV_CH, CONV_CH) * proj(2 * CONV_CH, CONV_CH)
    row = lax.broadcasted_iota(jnp.int32, (tm, 1), 0)
    prev1 = carry_ref[7:8, :]
    prev2 = carry_ref[6:7, :]
    u1 = jnp.where(row >= 1, pltpu.roll(u, 1, 0), prev1)
    u2 = jnp.where(row >= 2, pltpu.roll(u, 2, 0), jnp.where(row == 1, prev1, prev2))
    carry_ref[...] = u[tm - 8:tm, :]
    v = convw_ref[0:1, :] * u2 + convw_ref[1:2, :] * u1 + convw_ref[2:3, :] * u
    y = cb * v
    hi, lo = _split_bf16(y * y)
    ms = _dot(hi, gmat_ref[...]) + _dot(lo, gmat_ref[...])
    yc_ref[...] = (y * lax.rsqrt(ms + EPS) * gconv_ref[...]).astype(yc_ref.dtype)

    d = lax.broadcasted_iota(jnp.int32, (tm, LANES), 1) & (HEAD_DIM - 1)
    token = lax.broadcasted_iota(jnp.int32, (tm, LANES), 0) & (TOKENS_PER_ROW - 1)
    src_lane = token * ROT_HALF + (d & (ROT_HALF - 1))

    def unpack(packed_ref):
        rows = jnp.broadcast_to(packed_ref[...][:, None, :], (tm // TOKENS_PER_ROW, TOKENS_PER_ROW, LANES))
        return jnp.take_along_axis(rows.reshape(tm, LANES), src_lane, axis=1)

    cos_t = jnp.where(d < ROT_DIM, unpack(cosp_ref), 1.0)
    sin_raw = unpack(sinp_ref)
    sin_t = jnp.where(d < ROT_HALF, -sin_raw, jnp.where(d < ROT_DIM, sin_raw, 0.0))
    first_half = d < ROT_HALF

    def rope(t):
        outs = []
        for j in range(t.shape[1] // LANES):
            tj = t[:, j * LANES:(j + 1) * LANES]
            partner = jnp.where(first_half, pltpu.roll(tj, LANES - ROT_HALF, 1),
                                pltpu.roll(tj, ROT_HALF, 1))
            outs.append(tj * cos_t + partner * sin_t)
        return outs[0] if len(outs) == 1 else jnp.concatenate(outs, axis=1)

    c0 = 3 * CONV_CH
    q = rope(proj(c0, ATTN_WIDTH)) * Q_SCALE
    qT_ref[...] = q.T.astype(qT_ref.dtype)
    c0 += ATTN_WIDTH
    kv = proj(c0, 2 * KV_WIDTH)
    kc_ref[...] = rope(kv[:, :KV_WIDTH])
    vc_ref[...] = kv[:, KV_WIDTH:]
    kv = proj(c0 + 2 * KV_WIDTH, 2 * KV_WIDTH)
    ks = rope(kv[:, :KV_WIDTH]).astype(BF16)
    vsT = kv[:, KV_WIDTH:].T.astype(BF16)
    kv = proj(c0 + 4 * KV_WIDTH, 2 * KV_WIDTH)
    kw = rope(kv[:, :KV_WIDTH]).astype(BF16)
    vwT = kv[:, KV_WIDTH:].T.astype(BF16)
    pad = jnp.zeros((tm, MXU_DEPTH - LANES - HEAD_DIM), BF16)
    ones = jnp.ones((ONES_ROWS, tm), BF16)
    for g in range(N_KV_GROUPS):
        kg = ks[:, g * HEAD_DIM:(g + 1) * HEAD_DIM]
        ksa_ref[g] = jnp.concatenate([onehot_ref[...], kg, pad], axis=1)
        kw_ref[g] = kw[:, g * HEAD_DIM:(g + 1) * HEAD_DIM]
        vsT_ref[g] = jnp.concatenate([vsT[g * HEAD_DIM:(g + 1) * HEAD_DIM, :], ones], axis=0)
        vwT_ref[g] = jnp.concatenate([vwT[g * HEAD_DIM:(g + 1) * HEAD_DIM, :], ones], axis=0)
    gT_ref[...] = jax.nn.sigmoid(proj(c0 + 6 * KV_WIDTH, N_KV_GROUPS * GATE_PAD)).T


def _inproj_call(x, cos_p, sin_p, mod, gmix, win, convw, gconv, gmat, onehot):
    B, S, D = x.shape
    tm = TOKEN_TILE
    tok = lambda w: pl.BlockSpec((None, tm, w), lambda b, i: (b, i, 0))
    tr = lambda w: pl.BlockSpec((None, w, tm), lambda b, i: (b, 0, i))
    grp = lambda w: pl.BlockSpec((None, N_KV_GROUPS, tm, w), lambda b, i: (b, 0, i, 0))
    grpT = pl.BlockSpec((None, N_KV_GROUPS, VT_ROWS, tm), lambda b, i: (b, 0, 0, i))
    packed = pl.BlockSpec((None, tm // TOKENS_PER_ROW, LANES), lambda b, i: (b, i, 0))
    out_shapes = (
        jax.ShapeDtypeStruct((B, S, CONV_CH), BF16),
        jax.ShapeDtypeStruct((B, ATTN_WIDTH, S), BF16),
        jax.ShapeDtypeStruct((B, S, KV_WIDTH), F32),
        jax.ShapeDtypeStruct((B, S, KV_WIDTH), F32),
        jax.ShapeDtypeStruct((B, N_KV_GROUPS, S, MXU_DEPTH), BF16),
        jax.ShapeDtypeStruct((B, N_KV_GROUPS, VT_ROWS, S), BF16),
        jax.ShapeDtypeStruct((B, N_KV_GROUPS, S, HEAD_DIM), BF16),
        jax.ShapeDtypeStruct((B, N_KV_GROUPS, VT_ROWS, S), BF16),
        jax.ShapeDtypeStruct((B, N_KV_GROUPS * GATE_PAD, S), F32),
    )
    return pl.pallas_call(
        _inproj_kernel,
        grid=(B, S // tm),
        in_specs=[tok(D),
                  packed, packed,
                  pl.BlockSpec((None, N_MOD, D), lambda b, i: (b, 0, 0)),
                  _const_spec(gmix.shape), _const_spec(win.shape), _const_spec(convw.shape),
                  _const_spec(gconv.shape), _const_spec(gmat.shape),
                  pl.BlockSpec((tm, onehot.shape[1]), lambda b, i: (i, 0))],
        out_specs=(tok(CONV_CH), tr(ATTN_WIDTH), tok(KV_WIDTH), tok(KV_WIDTH), grp(MXU_DEPTH),
                   grpT, grp(HEAD_DIM), grpT, tr(N_KV_GROUPS * GATE_PAD)),
        out_shape=out_shapes,
        scratch_shapes=[pltpu.VMEM((8, CONV_CH), F32)],
        compiler_params=pltpu.CompilerParams(dimension_semantics=("arbitrary", "arbitrary"),
                                             vmem_limit_bytes=VMEM_LIMIT),
        name="mixer_inproj",
    )(x, cos_p, sin_p, mod, gmix, win, convw, gconv, gmat, onehot)


def _compress_kernel(kf_ref, vf_ref, pek_ref, pev_ref, wk1_ref, wk2_ref, wv1_ref, wv2_ref,
                     kcc_ref, vcT_ref):
    half = CMP_BLOCK // 2
    n = kf_ref.shape[0] // half

    def mlp(x_ref, pe_ref, w1_ref, w2_ref):
        parts = []
        for p in range(2):
            acc = None
            for l0 in range(0, half, 2):
                xs, ws = [], []
                for l in (l0, l0 + 1):
                    row = p * half + l
                    xs.append((x_ref[pl.ds(l, n, stride=half), :]
                               + pe_ref[row:row + 1, :]).astype(BF16))
                    w = w1_ref[row * HEAD_DIM:(row + 1) * HEAD_DIM, :].astype(BF16)
                    z = jnp.zeros_like(w)
                    ws.append(jnp.concatenate([jnp.concatenate([w, z], axis=1),
                                               jnp.concatenate([z, w], axis=1)], axis=0))
                d = _dot(jnp.concatenate(xs, axis=1), jnp.concatenate(ws, axis=0))
                acc = d if acc is None else acc + d
            parts.append(acc)
        hpre = parts[0] + pltpu.roll(parts[1], n - 1, 0)
        hid = (hpre * jax.nn.sigmoid(hpre)).astype(BF16)
        w2 = w2_ref[...].astype(BF16)
        return jnp.concatenate([_dot(hid[:, g * CMP_HIDDEN:(g + 1) * CMP_HIDDEN], w2)
                                for g in range(N_KV_GROUPS)], axis=1)

    kc = mlp(kf_ref, pek_ref, wk1_ref, wk2_ref).astype(kcc_ref.dtype)
    for g in range(N_KV_GROUPS):
        kcc_ref[g] = kc[:, g * HEAD_DIM:(g + 1) * HEAD_DIM]
    vcT = mlp(vf_ref, pev_ref, wv1_ref, wv2_ref).T.astype(vcT_ref.dtype)
    ones = jnp.ones((ONES_ROWS, n), vcT_ref.dtype)
    for g in range(N_KV_GROUPS):
        vcT_ref[g] = jnp.concatenate([vcT[g * HEAD_DIM:(g + 1) * HEAD_DIM, :], ones], axis=0)


def _compress_call(kf, vf, pek, pev, wk1, wk2, wv1, wv2):
    B, S, width = kf.shape
    n = S // (CMP_BLOCK // 2)
    flat = pl.BlockSpec((None, S, width), lambda b: (b, 0, 0))
    return pl.pallas_call(
        _compress_kernel,
        grid=(B,),
        in_specs=[flat, flat, _const_spec(pek.shape), _const_spec(pev.shape),
                  _const_spec(wk1.shape), _const_spec(wk2.shape), _const_spec(wv1.shape),
                  _const_spec(wv2.shape)],
        out_specs=(pl.BlockSpec((None, N_KV_GROUPS, n, HEAD_DIM), lambda b: (b, 0, 0, 0)),
                   pl.BlockSpec((None, N_KV_GROUPS, VT_ROWS, n), lambda b: (b, 0, 0, 0))),
        out_shape=(jax.ShapeDtypeStruct((B, N_KV_GROUPS, n, HEAD_DIM), BF16),
                   jax.ShapeDtypeStruct((B, N_KV_GROUPS, VT_ROWS, n), BF16)),
        compiler_params=pltpu.CompilerParams(dimension_semantics=("arbitrary",),
                                             vmem_limit_bytes=VMEM_LIMIT),
        name="kv_compress",
    )(kf, vf, pek, pev, wk1, wk2, wv1, wv2)


def _lane_tiles(x, n):
    return jnp.concatenate([x] * n, axis=1)


def _col_max(s_ref, n_rows, bias_fn):
    groups = ROW_TILE // 8
    mx = [jnp.full((8, s_ref.shape[1]), NEG, F32)] * groups
    for r in range(0, n_rows, ROW_TILE):
        x = s_ref[r:r + ROW_TILE, :]
        if bias_fn is not None:
            x = x + bias_fn(r)
            s_ref[r:r + ROW_TILE, :] = x
        mx = [jnp.maximum(mx[i], x[8 * i:8 * (i + 1), :]) for i in range(groups)]
    while len(mx) > 1:
        mx = [jnp.maximum(a, b) for a, b in zip(mx[0::2], mx[1::2])]
    return jnp.max(mx[0], axis=0, keepdims=True)


def _col_exp2(s_ref, p_ref, n_rows, m, keep_f32=False):
    for r in range(0, n_rows, ROW_TILE):
        p = jnp.exp2(s_ref[r:r + ROW_TILE, :] - m)
        if keep_f32:
            s_ref[r:r + ROW_TILE, :] = p
        p_ref[r:r + ROW_TILE, :] = p.astype(p_ref.dtype)


def _recip_pos(l):
    return 1.0 / jnp.where(l > 0.0, l, 1.0)


def _select_kernel(top_n, qT_ref, kcc_ref, vcT_ref, ovlT_ref, sel_ref, ocmp_ref, sc_ref, pc_ref,
                   ph_ref, pl_ref):
    QB = SEL_BLOCK
    n_cmp = kcc_ref.shape[0]
    n_slc = ovlT_ref.shape[0]
    t0 = pl.program_id(2) * QB
    tq = t0 + lax.broadcasted_iota(jnp.int32, (1, QB), 1)
    rows = lax.broadcasted_iota(jnp.int32, (ROW_TILE, 1), 0)

    qT = qT_ref[...]
    qcat = jnp.concatenate([qT[h * HEAD_DIM:(h + 1) * HEAD_DIM, :] for h in range(HPG)], axis=1)

    def cmp_bias(r):
        cmp_end = (r + rows) * CMP_STRIDE + (CMP_BLOCK - 1)
        return _lane_tiles(jnp.where(cmp_end <= tq, 0.0, NEG), HPG)

    n_live = jnp.minimum(((t0 + QB) // CMP_STRIDE + ROW_GROUP - 2) // ROW_GROUP, n_cmp // ROW_GROUP)
    tiles = range(0, ROW_GROUP, ROW_TILE)

    def live_rows(body, init=None):
        return lax.fori_loop(0, n_live, lambda i, c: body(pl.multiple_of(i * ROW_GROUP, ROW_GROUP), c),
                             init)

    def max_pass(r0, mx):
        s = _dot(kcc_ref[pl.ds(r0, ROW_GROUP), :], qcat)
        for r in tiles:
            x = s[r:r + ROW_TILE, :] + cmp_bias(r0 + r)
            sc_ref[pl.ds(r0 + r, ROW_TILE), :] = x
            mx = tuple(jnp.maximum(mx[i], x[8 * i:8 * (i + 1), :]) for i in range(len(mx)))
        return mx

    mx = live_rows(max_pass, (jnp.full((8, HPG * QB), NEG, F32),) * (ROW_TILE // 8))
    m = jnp.maximum(jnp.max(functools.reduce(jnp.maximum, mx), axis=0, keepdims=True), M_FLOOR)

    def exp_pass(r0, _):
        for r in tiles:
            p = jnp.exp2(sc_ref[pl.ds(r0 + r, ROW_TILE), :] - m)
            sc_ref[pl.ds(r0 + r, ROW_TILE), :] = p
            pc_ref[pl.ds(r0 + r, ROW_TILE), :] = p.astype(pc_ref.dtype)

    live_rows(exp_pass)

    def zero_pass(i, _):
        r0 = pl.multiple_of(i * ROW_GROUP, ROW_GROUP)
        pc_ref[pl.ds(r0, ROW_GROUP), :] = jnp.zeros((ROW_GROUP, HPG * QB), pc_ref.dtype)
        ph_ref[pl.ds(r0, ROW_GROUP), :] = jnp.zeros((ROW_GROUP, QB), ph_ref.dtype)
        pl_ref[pl.ds(r0, ROW_GROUP), :] = jnp.zeros((ROW_GROUP, QB), pl_ref.dtype)

    lax.fori_loop(n_live, n_cmp // ROW_GROUP, zero_pass, None)

    o_cmp = _dot(vcT_ref[...], pc_ref[...])
    rl = _recip_pos(o_cmp[HEAD_DIM:HEAD_DIM + 1, :])
    o_cmp = o_cmp[0:HEAD_DIM, :] * rl

    def sum_pass(r0, _):
        for r in tiles:
            pn = sc_ref[pl.ds(r0 + r, ROW_TILE), :] * rl
            psum = pn[:, 0:QB]
            for h in range(1, HPG):
                psum = psum + pn[:, h * QB:(h + 1) * QB]
            hi, lo = _split_bf16(psum)
            ph_ref[pl.ds(r0 + r, ROW_TILE), :] = hi
            pl_ref[pl.ds(r0 + r, ROW_TILE), :] = lo

    live_rows(sum_pass)
    imp = _dot(ovlT_ref[...], ph_ref[...]) + _dot(ovlT_ref[...], pl_ref[...])
    for h in range(HPG):
        ocmp_ref[h] = o_cmp[:, h * QB:(h + 1) * QB]

    cur = tq // SLC_BLOCK

    def tree(op, xs):
        while len(xs) > 1:
            xs = [op(*xs[i:i + 2]) if i + 1 < len(xs) else xs[i] for i in range(0, len(xs), 2)]
        return xs[0]

    def select_among(n_rows):
        def run():
            blk = lax.broadcasted_iota(jnp.int32, (n_rows, 1), 0)
            future = blk > cur
            forced = (blk == 0) | (blk == cur) | (blk == cur - 1)
            score0 = jnp.where(future, -BIG, jnp.where(forced, BIG, imp[0:n_rows, :]))
            score = [score0[8 * i:8 * (i + 1), :] for i in range(n_rows // 8)]
            cum = jnp.zeros((1, QB), F32)
            thr = jnp.zeros((1, QB), F32)
            above = jnp.zeros((1, QB), F32)
            for _ in range(top_n):
                best = jnp.max(tree(jnp.maximum, score), axis=0, keepdims=True)
                eq = [s == best for s in score]
                unfilled = cum < top_n
                thr = jnp.where(unfilled, best, thr)
                above = jnp.where(unfilled, cum, above)
                cum = cum + jnp.sum(tree(jnp.add, [jnp.where(e, 1.0, 0.0) for e in eq]),
                                    axis=0, keepdims=True)
                score = [jnp.where(e, -jnp.inf, s) for e, s in zip(eq, score)]
            ties = score0 == thr
            lower = jnp.where(lax.broadcasted_iota(jnp.int32, (1, n_rows), 1) < blk, 1.0, 0.0)
            rank = _dot(lower.astype(BF16), jnp.where(ties, 1.0, 0.0).astype(BF16))
            picked = (score0 > thr) | (ties & (rank < top_n - above))
            sel_ref[0:n_rows, :] = jnp.where(future, NEG, jnp.where(picked, 0.0, NEG)).astype(sel_ref.dtype)
            if n_rows < n_slc:
                sel_ref[n_rows:n_slc, :] = jnp.full((n_slc - n_rows, QB), NEG, sel_ref.dtype)
        return run

    steps = n_slc // SEL_ROW_STEP
    need = jnp.minimum((t0 + QB - 1) // (SLC_BLOCK * SEL_ROW_STEP), steps - 1)
    lax.switch(need, [select_among(SEL_ROW_STEP * (k + 1)) for k in range(steps)])


def _select_call(top_n, qT, kcc, vcT, ovlT):
    B, _, S = qT.shape
    n_cmp = kcc.shape[2]
    n_slc = ovlT.shape[0]
    gw = HPG * HEAD_DIM
    qw = HPG * SEL_BLOCK
    return pl.pallas_call(
        functools.partial(_select_kernel, top_n),
        grid=(B, N_KV_GROUPS, S // SEL_BLOCK),
        in_specs=[pl.BlockSpec((None, gw, SEL_BLOCK), lambda b, g, i: (b, g, i)),
                  pl.BlockSpec((None, None, n_cmp, HEAD_DIM), lambda b, g, i: (b, g, 0, 0)),
                  pl.BlockSpec((None, None, VT_ROWS, n_cmp), lambda b, g, i: (b, g, 0, 0)),
                  pl.BlockSpec(ovlT.shape, lambda b, g, i: (0, 0))],
        out_specs=(pl.BlockSpec((None, None, n_slc, SEL_BLOCK), lambda b, g, i: (b, g, 0, i)),
                   pl.BlockSpec((None, None, HPG, HEAD_DIM, SEL_BLOCK),
                                lambda b, g, i: (b, g, 0, 0, i))),
        out_shape=(jax.ShapeDtypeStruct((B, N_KV_GROUPS, n_slc, S), BF16),
                   jax.ShapeDtypeStruct((B, N_KV_GROUPS, HPG, HEAD_DIM, S), F32)),
        scratch_shapes=[pltpu.VMEM((n_cmp, qw), F32), pltpu.VMEM((n_cmp, qw), BF16),
                        pltpu.VMEM((n_cmp, SEL_BLOCK), BF16), pltpu.VMEM((n_cmp, SEL_BLOCK), BF16)],
        compiler_params=pltpu.CompilerParams(
            dimension_semantics=("parallel", "parallel", "arbitrary"),
            vmem_limit_bytes=VMEM_LIMIT),
        name="nsa_select",
    )(qT, kcc, vcT, ovlT)


def _attn_kernel(qT_ref, gT_ref, sel_ref, ocmp_ref, ksa_ref, vsT_ref, kw_ref, vwT_ref, ga_ref,
                 o_ref, s0_ref, s1_ref, p0_ref, p1_ref, sw_ref, pw_ref, rhs_ref):
    QB = Q_BLOCK
    n_slc = sel_ref.shape[0]
    t0 = pl.program_id(2) * QB
    tq = t0 + lax.broadcasted_iota(jnp.int32, (1, QB), 1)
    rows = lax.broadcasted_iota(jnp.int32, (ROW_TILE, 1), 0)

    qT = qT_ref[...]
    qcat = jnp.concatenate([qT[h * HEAD_DIM:(h + 1) * HEAD_DIM, :] for h in range(HPG)], axis=1)

    rhs_ref[...] = jnp.concatenate([_lane_tiles(sel_ref[...], HPG), qcat,
                                    jnp.zeros((MXU_DEPTH - n_slc - HEAD_DIM, QW), BF16)], axis=0)

    w0 = pl.multiple_of(jnp.maximum(t0 - WINDOW, 0), QB)
    sw_ref[...] = _dot(kw_ref[pl.ds(w0, WIN_KEYS), :], qcat)

    def win_bias(r):
        dist = tq - (w0 + r + rows)
        return _lane_tiles(jnp.where((dist >= 0) & (dist < WINDOW), 0.0, NEG), HPG)

    KC = KEY_CHUNK
    last = t0 // KC

    def scores(c, dst_ref):
        k0 = pl.multiple_of(jnp.minimum(c, last) * KC, KC)
        dst_ref[...] = _dot(ksa_ref[pl.ds(k0, KC), :], rhs_ref[...])

    def weighted_values(c, p_ref):
        k0 = pl.multiple_of(jnp.clip(c, 0, last) * KC, KC)
        return _dot(vsT_ref[:, pl.ds(k0, KC)], p_ref[...])

    def softmax(c, src_ref, p_ref, m, acc, causal):
        def causal_bias(r):
            return _lane_tiles(jnp.where(c * KC + r + rows <= tq, 0.0, NEG), HPG)

        m_new = jnp.maximum(m, _col_max(src_ref, KC, causal_bias if causal else None))
        _col_exp2(src_ref, p_ref, KC, m_new)
        return m_new, jnp.exp2(m - m_new) * acc

    def pair(i, carry):
        m, acc = carry
        pending = weighted_values(2 * i - 1, p1_ref)
        scores(2 * i + 1, s1_ref)
        m, acc = softmax(2 * i, s0_ref, p0_ref, m, acc + pending, False)
        pending = weighted_values(2 * i, p0_ref)
        scores(2 * i + 2, s0_ref)
        return softmax(2 * i + 1, s1_ref, p1_ref, m, acc + pending, True)

    p1_ref[...] = jnp.zeros_like(p1_ref)
    scores(0, s0_ref)

    m = _col_max(sw_ref, WIN_KEYS, win_bias)
    _col_exp2(sw_ref, pw_ref, WIN_KEYS, m)
    o_win = _dot(vwT_ref[:, pl.ds(w0, WIN_KEYS)], pw_ref[...])
    o_win = o_win[0:HEAD_DIM, :] * (1.0 / o_win[HEAD_DIM:HEAD_DIM + 1, :])

    init = (jnp.full((1, QW), NEG, F32), jnp.zeros((VT_ROWS, QW), F32))
    n_pairs = (last + 1) // 2
    carry = lax.fori_loop(0, n_pairs // 2, lambda j, cr: pair(2 * j + 1, pair(2 * j, cr)), init)
    m, acc = lax.cond(n_pairs % 2 == 1, lambda cr: pair(n_pairs - 1, cr), lambda cr: cr, carry)
    acc = acc + weighted_values(2 * n_pairs - 1, p1_ref)

    def diagonal(carry):
        m, acc = softmax(last, s0_ref, p0_ref, carry[0], carry[1], True)
        return m, acc + weighted_values(last, p0_ref)

    _, acc = lax.cond(last % 2 == 0, diagonal, lambda cr: cr, (m, acc))
    o_slc = acc[0:HEAD_DIM, :] * (1.0 / acc[HEAD_DIM:HEAD_DIM + 1, :])

    gts = gT_ref[...]
    outs = []
    for h in range(HPG):
        sl = slice(h * QB, (h + 1) * QB)
        o = (gts[3 * h:3 * h + 1, :] * ocmp_ref[h] + gts[3 * h + 1:3 * h + 2, :] * o_slc[:, sl]
             + gts[3 * h + 2:3 * h + 3, :] * o_win[:, sl])
        o = o * lax.rsqrt(jnp.mean(o * o, axis=0, keepdims=True) + EPS)
        outs.append((o * ga_ref[h * HEAD_DIM:(h + 1) * HEAD_DIM, :]).T)
    o_ref[...] = jnp.concatenate(outs, axis=1).astype(o_ref.dtype)


def _attn_call(qT, gatesT, sel_bias, o_cmp, ksa, vsT, kw, vwT, g_attn_col):
    B, _, S = qT.shape
    n_slc = sel_bias.shape[2]
    gw = HPG * HEAD_DIM
    rows = lambda n, w: pl.BlockSpec((None, None, n, w), lambda b, g, i: (b, g, 0, 0))
    cols = lambda n: pl.BlockSpec((None, None, VT_ROWS, n), lambda b, g, i: (b, g, 0, 0))
    return pl.pallas_call(
        _attn_kernel,
        grid=(B, N_KV_GROUPS, S // Q_BLOCK),
        in_specs=[pl.BlockSpec((None, gw, Q_BLOCK), lambda b, g, i: (b, g, i)),
                  pl.BlockSpec((None, GATE_PAD, Q_BLOCK), lambda b, g, i: (b, g, i)),
                  pl.BlockSpec((None, None, n_slc, Q_BLOCK), lambda b, g, i: (b, g, 0, i)),
                  pl.BlockSpec((None, None, HPG, HEAD_DIM, Q_BLOCK), lambda b, g, i: (b, g, 0, 0, i)),
                  rows(S, MXU_DEPTH), cols(S), rows(S, HEAD_DIM), cols(S),
                  pl.BlockSpec((gw, 1), lambda b, g, i: (g, 0))],
        out_specs=pl.BlockSpec((None, Q_BLOCK, gw), lambda b, g, i: (b, i, g)),
        out_shape=jax.ShapeDtypeStruct((B, S, ATTN_WIDTH), BF16),
        scratch_shapes=[pltpu.VMEM((KEY_CHUNK, QW), F32), pltpu.VMEM((KEY_CHUNK, QW), F32),
                        pltpu.VMEM((KEY_CHUNK, QW), BF16), pltpu.VMEM((KEY_CHUNK, QW), BF16),
                        pltpu.VMEM((WIN_KEYS, QW), F32), pltpu.VMEM((WIN_KEYS, QW), BF16),
                        pltpu.VMEM((MXU_DEPTH, QW), BF16)],
        compiler_params=pltpu.CompilerParams(
            dimension_semantics=("parallel", "parallel", "arbitrary"),
            vmem_limit_bytes=VMEM_LIMIT),
        name="nsa_attention",
    )(qT, gatesT, sel_bias, o_cmp, ksa, vsT, kw, vwT, g_attn_col)


def _out_kernel(x_ref, yc_ref, ya_ref, mod_ref, wo_ref, g_ref, wg_ref, wu_ref, wd_ref, gf_ref,
                o_ref):
    mix = _dot(yc_ref[...], wo_ref[0:CONV_CH, :]) + _dot(ya_ref[...], wo_ref[CONV_CH:, :])
    x = x_ref[...] + mod_ref[5:6, :] * mix
    x = _ffn_core(x, mod_ref[6:7, :], mod_ref[7:8, :], mod_ref[8:9, :], g_ref[...], wg_ref,
                  wu_ref, wd_ref)
    o_ref[...] = _rms(x, gf_ref[...])


def _out_call(x, yc, ya, mod, wo, g, wg, wu, wd, gf):
    B, S, D = x.shape
    tm = FFN_TOKEN_TILE
    tok = lambda w: pl.BlockSpec((None, tm, w), lambda b, i: (b, i, 0))
    return pl.pallas_call(
        _out_kernel,
        grid=(B, S // tm),
        in_specs=[tok(D), tok(CONV_CH), tok(ATTN_WIDTH),
                  pl.BlockSpec((None, N_MOD, D), lambda b, i: (b, 0, 0)),
                  _const_spec(wo.shape), _const_spec(g.shape), _const_spec(wg.shape),
                  _const_spec(wu.shape), _const_spec(wd.shape), _const_spec(gf.shape)],
        out_specs=tok(D),
        out_shape=jax.ShapeDtypeStruct((B, S, D), F32),
        compiler_params=pltpu.CompilerParams(dimension_semantics=("parallel", "parallel"),
                                             vmem_limit_bytes=VMEM_LIMIT),
        name="outproj_ffn2",
    )(x, yc, ya, mod, wo, g, wg, wu, wd, gf)


def kernel(x, c, positions, w_ada, b_ada, g_ffn1, w1_gate, w1_up, w1_down, g_mix, w_in, conv_w, cmp_pos_k, cmp_pos_v, w_cmpk1, w_cmpk2, w_cmpv1, w_cmpv2, g_out_conv, g_out_attn, w_out, g_ffn2, w2_gate, w2_up, w2_down, g_final):
    B, S, D = x.shape
    depth = w_ada.shape[0]
    n_slc = S // SLC_BLOCK
    half = CMP_BLOCK // 2
    n_half = S // half
    assert n_slc <= LANES, "selection-block one-hot is one lane tile wide"

    c_pad = jnp.pad(c, ((0, 8 - B), (0, 0)))
    row = lambda a: a.reshape(1, -1)

    freq_half = jnp.power(ROPE_THETA, -2.0 * jnp.arange(ROT_HALF, dtype=F32) / ROT_DIM)
    freq = jnp.tile(freq_half, LANES // ROT_HALF).reshape(1, LANES)
    gidx = np.arange(CONV_CH) // (CONV_CH // CONV_GROUPS)
    gmat = jnp.asarray((gidx[:, None] == gidx[None, :]) / (CONV_CH // CONV_GROUPS), dtype=BF16)
    c0 = np.arange(n_half) * CMP_STRIDE
    s0 = np.arange(LANES) * SLC_BLOCK
    ovlT = ((c0[None, :] <= s0[:, None] + SLC_BLOCK - 1) & (c0[None, :] + CMP_BLOCK - 1 >= s0[:, None]))
    ovlT = jnp.asarray(ovlT, dtype=BF16)
    onehot = jnp.asarray((np.arange(S) // SLC_BLOCK)[:, None] == np.arange(LANES)[None, :], dtype=BF16)
    cos_p, sin_p = _rope_table_call(positions.reshape(-1, TOKENS_PER_ROW), freq)
    cos_p = cos_p.reshape(B, S // TOKENS_PER_ROW, LANES)
    sin_p = sin_p.reshape(B, S // TOKENS_PER_ROW, LANES)

    for l in range(depth):
        mod = _ada_call(c_pad, w_ada[l], row(b_ada[l]))[:B].reshape(B, N_MOD, D)

        x = _ffn_call(x, mod, row(g_ffn1[l]), w1_gate[l].astype(BF16), w1_up[l].astype(BF16),
                      w1_down[l].astype(BF16))

        n_main = w_in.shape[2] - N_KV_GROUPS * 3 * HPG
        gate_cols = [jnp.pad(w_in[l][:, n_main + g * 3 * HPG:n_main + (g + 1) * 3 * HPG],
                             ((0, 0), (0, GATE_PAD - 3 * HPG))) for g in range(N_KV_GROUPS)]
        win = jnp.concatenate([w_in[l][:, :n_main]] + gate_cols, axis=1).astype(BF16)
        (yc, qT, kc, vc, ksa, vsT, kw, vwT, gatesT) = _inproj_call(
            x, cos_p, sin_p, mod, row(g_mix[l]), win, conv_w[l], row(g_out_conv[l]), gmat, onehot)

        both_groups = lambda pe: jnp.tile(pe, (1, N_KV_GROUPS))
        kcc, vcT = _compress_call(kc, vc, both_groups(cmp_pos_k[l]), both_groups(cmp_pos_v[l]),
                                  w_cmpk1[l], w_cmpk2[l], w_cmpv1[l], w_cmpv2[l])

        sel_bias, o_cmp = _select_call(min(SLC_TOP_N, n_slc), qT, kcc, vcT, ovlT)
        ya = _attn_call(qT, gatesT, sel_bias, o_cmp, ksa, vsT, kw, vwT,
                        g_out_attn[l].reshape(ATTN_WIDTH, 1))

        assert l == depth - 1, "final norm is fused into the last layer's output kernel"
        x = _out_call(x, yc, ya, mod, w_out[l].astype(BF16), row(g_ffn2[l]),
                      w2_gate[l].astype(BF16), w2_up[l].astype(BF16), w2_down[l].astype(BF16),
                      row(g_final))
    return x
```

```python
import functools
import math

import numpy as np
import jax
import jax.numpy as jnp
from jax import lax
from jax.experimental import pallas as pl
from jax.experimental.pallas import tpu as pltpu

F32 = jnp.float32
BF16 = jnp.bfloat16

CONV_CH = 512
CONV_GROUPS = 8
N_HEADS = 8
N_KV_GROUPS = 2
HPG = N_HEADS // N_KV_GROUPS
HEAD_DIM = 64
ATTN_WIDTH = N_HEADS * HEAD_DIM
KV_WIDTH = N_KV_GROUPS * HEAD_DIM
ROPE_THETA = 500000.0
ROT_DIM = HEAD_DIM // 4
ROT_HALF = ROT_DIM // 2
CMP_BLOCK = 32
CMP_STRIDE = 16
CMP_HIDDEN = 256
SLC_BLOCK = 64
SLC_TOP_N = 16
WINDOW = 512
Q_BLOCK = 256
MACARON_W = 0.5
N_MOD = 9
EPS = 1e-6
NEG = -1e30
BIG = 1e9

LANES = 128
MXU_DEPTH = 256
VMEM_LIMIT = 56 * 1024 * 1024

TOKEN_TILE = 512
FFN_TOKEN_TILE = 512
ADA_COL_TILE = 1024
FF_TILE = 256
KEY_CHUNK = 512
WIN_KEYS = WINDOW + Q_BLOCK
TOKENS_PER_ROW = LANES // ROT_HALF
GATE_PAD = LANES
QW = HPG * Q_BLOCK
SEL_BLOCK = 512
SEL_ROW_STEP = 32
ONES_ROWS = 16
VT_ROWS = HEAD_DIM + ONES_ROWS
ROW_TILE = 16
Q_SCALE = HEAD_DIM ** -0.5 * math.log2(math.e)
M_FLOOR = -1e20


def _dot(a, b):
    return jnp.dot(a, b, preferred_element_type=F32)


def _rms(x, g):
    return x * lax.rsqrt(jnp.mean(x * x, axis=-1, keepdims=True) + EPS) * g


def _split_bf16(x):
    hi = x.astype(BF16)
    lo = (x - hi.astype(F32)).astype(BF16)
    return hi, lo


def _const_spec(shape):
    nd = len(shape)
    return pl.BlockSpec(shape, lambda *_: (0,) * nd, pipeline_mode=pl.Buffered(1))


def _ada_kernel(c_ref, w_ref, b_ref, o_ref):
    c = c_ref[...]
    c_act = c * jax.nn.sigmoid(c)
    o_ref[...] = jnp.dot(c_act, w_ref[...], preferred_element_type=F32,
                         precision=lax.Precision.HIGHEST) + b_ref[...]


def _ada_call(c_pad, w_ada, b_ada):
    rows, d = c_pad.shape
    n = w_ada.shape[1]
    tn = ADA_COL_TILE
    return pl.pallas_call(
        _ada_kernel,
        grid=(n // tn,),
        in_specs=[pl.BlockSpec((rows, d), lambda j: (0, 0)),
                  pl.BlockSpec((d, tn), lambda j: (0, j)),
                  pl.BlockSpec((1, tn), lambda j: (0, j))],
        out_specs=pl.BlockSpec((rows, tn), lambda j: (0, j)),
        out_shape=jax.ShapeDtypeStruct((rows, n), F32),
        compiler_params=pltpu.CompilerParams(dimension_semantics=("arbitrary",),
                                             vmem_limit_bytes=VMEM_LIMIT),
        name="adaln_mod",
    )(c_pad, w_ada, b_ada)


def _ffn_core(x, shift, scale, gate, g, wg_ref, wu_ref, wd_ref):
    h = _rms(x, g) * (1.0 + scale) + shift
    hb = h.astype(BF16)
    d_ff = wg_ref.shape[1]
    acc = None
    for j in range(d_ff // FF_TILE):
        sl = slice(j * FF_TILE, (j + 1) * FF_TILE)
        gg = _dot(hb, wg_ref[:, sl])
        uu = _dot(hb, wu_ref[:, sl])
        a = (gg * jax.nn.sigmoid(gg) * uu).astype(BF16)
        d = _dot(a, wd_ref[sl, :])
        acc = d if acc is None else acc + d
    return x + (MACARON_W * gate) * acc


def _ffn_kernel(x_ref, mod_ref, g_ref, wg_ref, wu_ref, wd_ref, o_ref):
    o_ref[...] = _ffn_core(x_ref[...], mod_ref[0:1, :], mod_ref[1:2, :], mod_ref[2:3, :],
                           g_ref[...], wg_ref, wu_ref, wd_ref)


def _ffn_call(x, mod, g, wg, wu, wd):
    B, S, D = x.shape
    tm = FFN_TOKEN_TILE
    return pl.pallas_call(
        _ffn_kernel,
        grid=(B, S // tm),
        in_specs=[pl.BlockSpec((None, tm, D), lambda b, i: (b, i, 0)),
                  pl.BlockSpec((None, N_MOD, D), lambda b, i: (b, 0, 0)),
                  _const_spec(g.shape), _const_spec(wg.shape), _const_spec(wu.shape),
                  _const_spec(wd.shape)],
        out_specs=pl.BlockSpec((None, tm, D), lambda b, i: (b, i, 0)),
        out_shape=jax.ShapeDtypeStruct((B, S, D), F32),
        compiler_params=pltpu.CompilerParams(dimension_semantics=("parallel", "parallel"),
                                             vmem_limit_bytes=VMEM_LIMIT),
        name="ffn1",
    )(x, mod, g, wg, wu, wd)


def _rope_table_kernel(pos_ref, freq_ref, cos_ref, sin_ref):
    pos = jnp.concatenate([pos_ref[...].astype(F32),
                           jnp.zeros((pos_ref.shape[0], LANES - TOKENS_PER_ROW), F32)], axis=1)
    src = lax.broadcasted_iota(jnp.int32, pos.shape, 1) // ROT_HALF
    ang = jnp.take_along_axis(pos, src, axis=1) * freq_ref[...]
    cos_ref[...] = jnp.cos(ang)
    sin_ref[...] = jnp.sin(ang)


def _rope_table_call(pos_rows, freq):
    rows = pos_rows.shape[0]
    shape = jax.ShapeDtypeStruct((rows, LANES), F32)
    full = pl.BlockSpec((rows, LANES), lambda: (0, 0))
    return pl.pallas_call(
        _rope_table_kernel,
        in_specs=[pl.BlockSpec(pos_rows.shape, lambda: (0, 0)), pl.BlockSpec(freq.shape, lambda: (0, 0))],
        out_specs=(full, full),
        out_shape=(shape, shape),
        compiler_params=pltpu.CompilerParams(vmem_limit_bytes=VMEM_LIMIT),
        name="rope_table",
    )(pos_rows, freq)


def _inproj_kernel(x_ref, cosp_ref, sinp_ref, mod_ref, gmix_ref, win_ref, convw_ref, gconv_ref,
                   gmat_ref, onehot_ref, yc_ref, qT_ref, kc_ref, vc_ref, ksa_ref, vsT_ref, kw_ref,
                   vwT_ref, gT_ref, carry_ref):
    tm = x_ref.shape[0]

    @pl.when(pl.program_id(1) == 0)
    def _():
        carry_ref[...] = jnp.zeros_like(carry_ref)

    x = x_ref[...]
    h = _rms(x, gmix_ref[...]) * (1.0 + mod_ref[4:5, :]) + mod_ref[3:4, :]
    hb = h.astype(BF16)

    def proj(c0, width):
        return _dot(hb, win_ref[:, c0:c0 + width])

    cb = proj(0, CONV_CH)
    u = proj(CONV_CH, CONV_CH) * proj(2 * CONV_CH, CONV_CH)
    row = lax.broadcasted_iota(jnp.int32, (tm, 1), 0)
    prev1 = carry_ref[7:8, :]
    prev2 = carry_ref[6:7, :]
    u1 = jnp.where(row >= 1, pltpu.roll(u, 1, 0), prev1)
    u2 = jnp.where(row >= 2, pltpu.roll(u, 2, 0), jnp.where(row == 1, prev1, prev2))
    carry_ref[...] = u[tm - 8:tm, :]
    v = convw_ref[0:1, :] * u2 + convw_ref[1:2, :] * u1 + convw_ref[2:3, :] * u
    y = cb * v
    hi, lo = _split_bf16(y * y)
    ms = _dot(hi, gmat_ref[...]) + _dot(lo, gmat_ref[...])
    yc_ref[...] = (y * lax.rsqrt(ms + EPS) * gconv_ref[...]).astype(yc_ref.dtype)

    d = lax.broadcasted_iota(jnp.int32, (tm, LANES), 1) & (HEAD_DIM - 1)
    token = lax.broadcasted_iota(jnp.int32, (tm, LANES), 0) & (TOKENS_PER_ROW - 1)
    src_lane = token * ROT_HALF + (d & (ROT_HALF - 1))

    def unpack(packed_ref):
        rows = jnp.broadcast_to(packed_ref[...][:, None, :], (tm // TOKENS_PER_ROW, TOKENS_PER_ROW, LANES))
        return jnp.take_along_axis(rows.reshape(tm, LANES), src_lane, axis=1)

    cos_t = jnp.where(d < ROT_DIM, unpack(cosp_ref), 1.0)
    sin_raw = unpack(sinp_ref)
    sin_t = jnp.where(d < ROT_HALF, -sin_raw, jnp.where(d < ROT_DIM, sin_raw, 0.0))
    first_half = d < ROT_HALF

    def rope(t):
        outs = []
        for j in range(t.shape[1] // LANES):
            tj = t[:, j * LANES:(j + 1) * LANES]
            partner = jnp.where(first_half, pltpu.roll(tj, LANES - ROT_HALF, 1),
                                pltpu.roll(tj, ROT_HALF, 1))
            outs.append(tj * cos_t + partner * sin_t)
        return outs[0] if len(outs) == 1 else jnp.concatenate(outs, axis=1)

    c0 = 3 * CONV_CH
    q = rope(proj(c0, ATTN_WIDTH)) * Q_SCALE
    qT_ref[...] = q.T.astype(qT_ref.dtype)
    c0 += ATTN_WIDTH
    kv = proj(c0, 2 * KV_WIDTH)
    kc_ref[...] = rope(kv[:, :KV_WIDTH])
    vc_ref[...] = kv[:, KV_WIDTH:]
    kv = proj(c0 + 2 * KV_WIDTH, 2 * KV_WIDTH)
    ks = rope(kv[:, :KV_WIDTH]).astype(BF16)
    vsT = kv[:, KV_WIDTH:].T.astype(BF16)
    kv = proj(c0 + 4 * KV_WIDTH, 2 * KV_WIDTH)
    kw = rope(kv[:, :KV_WIDTH]).astype(BF16)
    vwT = kv[:, KV_WIDTH:].T.astype(BF16)
    pad = jnp.zeros((tm, MXU_DEPTH - LANES - HEAD_DIM), BF16)
    ones = jnp.ones((ONES_ROWS, tm), BF16)
    for g in range(N_KV_GROUPS):
        kg = ks[:, g * HEAD_DIM:(g + 1) * HEAD_DIM]
        ksa_ref[g] = jnp.concatenate([onehot_ref[...], kg, pad], axis=1)
        kw_ref[g] = kw[:, g * HEAD_DIM:(g + 1) * HEAD_DIM]
        vsT_ref[g] = jnp.concatenate([vsT[g * HEAD_DIM:(g + 1) * HEAD_DIM, :], ones], axis=0)
        vwT_ref[g] = jnp.concatenate([vwT[g * HEAD_DIM:(g + 1) * HEAD_DIM, :], ones], axis=0)
    gT_ref[...] = jax.nn.sigmoid(proj(c0 + 6 * KV_WIDTH, N_KV_GROUPS * GATE_PAD)).T


def _inproj_call(x, cos_p, sin_p, mod, gmix, win, convw, gconv, gmat, onehot):
    B, S, D = x.shape
    tm = TOKEN_TILE
    tok = lambda w: pl.BlockSpec((None, tm, w), lambda b, i: (b, i, 0))
    tr = lambda w: pl.BlockSpec((None, w, tm), lambda b, i: (b, 0, i))
    grp = lambda w: pl.BlockSpec((None, N_KV_GROUPS, tm, w), lambda b, i: (b, 0, i, 0))
    grpT = pl.BlockSpec((None, N_KV_GROUPS, VT_ROWS, tm), lambda b, i: (b, 0, 0, i))
    packed = pl.BlockSpec((None, tm // TOKENS_PER_ROW, LANES), lambda b, i: (b, i, 0))
    out_shapes = (
        jax.ShapeDtypeStruct((B, S, CONV_CH), BF16),
        jax.ShapeDtypeStruct((B, ATTN_WIDTH, S), BF16),
        jax.ShapeDtypeStruct((B, S, KV_WIDTH), F32),
        jax.ShapeDtypeStruct((B, S, KV_WIDTH), F32),
        jax.ShapeDtypeStruct((B, N_KV_GROUPS, S, MXU_DEPTH), BF16),
        jax.ShapeDtypeStruct((B, N_KV_GROUPS, VT_ROWS, S), BF16),
        jax.ShapeDtypeStruct((B, N_KV_GROUPS, S, HEAD_DIM), BF16),
        jax.ShapeDtypeStruct((B, N_KV_GROUPS, VT_ROWS, S), BF16),
        jax.ShapeDtypeStruct((B, N_KV_GROUPS * GATE_PAD, S), F32),
    )
    return pl.pallas_call(
        _inproj_kernel,
        grid=(B, S // tm),
        in_specs=[tok(D),
                  packed, packed,
                  pl.BlockSpec((None, N_MOD, D), lambda b, i: (b, 0, 0)),
                  _const_spec(gmix.shape), _const_spec(win.shape), _const_spec(convw.shape),
                  _const_spec(gconv.shape), _const_spec(gmat.shape),
                  pl.BlockSpec((tm, onehot.shape[1]), lambda b, i: (i, 0))],
        out_specs=(tok(CONV_CH), tr(ATTN_WIDTH), tok(KV_WIDTH), tok(KV_WIDTH), grp(MXU_DEPTH),
                   grpT, grp(HEAD_DIM), grpT, tr(N_KV_GROUPS * GATE_PAD)),
        out_shape=out_shapes,
        scratch_shapes=[pltpu.VMEM((8, CONV_CH), F32)],
        compiler_params=pltpu.CompilerParams(dimension_semantics=("arbitrary", "arbitrary"),
                                             vmem_limit_bytes=VMEM_LIMIT),
        name="mixer_inproj",
    )(x, cos_p, sin_p, mod, gmix, win, convw, gconv, gmat, onehot)


def _compress_kernel(kf_ref, vf_ref, pek_ref, pev_ref, wk1_ref, wk2_ref, wv1_ref, wv2_ref,
                     kcc_ref, vcT_ref):
    half = CMP_BLOCK // 2
    n = kf_ref.shape[0] // half

    def mlp(x_ref, pe_ref, w1_ref, w2_ref):
        parts = []
        for p in range(2):
            acc = None
            for l0 in range(0, half, 2):
                xs, ws = [], []
                for l in (l0, l0 + 1):
                    row = p * half + l
                    xs.append((x_ref[pl.ds(l, n, stride=half), :]
                               + pe_ref[row:row + 1, :]).astype(BF16))
                    w = w1_ref[row * HEAD_DIM:(row + 1) * HEAD_DIM, :].astype(BF16)
                    z = jnp.zeros_like(w)
                    ws.append(jnp.concatenate([jnp.concatenate([w, z], axis=1),
                                               jnp.concatenate([z, w], axis=1)], axis=0))
                d = _dot(jnp.concatenate(xs, axis=1), jnp.concatenate(ws, axis=0))
                acc = d if acc is None else acc + d
            parts.append(acc)
        hpre = parts[0] + pltpu.roll(parts[1], n - 1, 0)
        hid = (hpre * jax.nn.sigmoid(hpre)).astype(BF16)
        w2 = w2_ref[...].astype(BF16)
        return jnp.concatenate([_dot(hid[:, g * CMP_HIDDEN:(g + 1) * CMP_HIDDEN], w2)
                                for g in range(N_KV_GROUPS)], axis=1)

    kc = mlp(kf_ref, pek_ref, wk1_ref, wk2_ref).astype(kcc_ref.dtype)
    for g in range(N_KV_GROUPS):
        kcc_ref[g] = kc[:, g * HEAD_DIM:(g + 1) * HEAD_DIM]
    vcT = mlp(vf_ref, pev_ref, wv1_ref, wv2_ref).T.astype(vcT_ref.dtype)
    ones = jnp.ones((ONES_ROWS, n), vcT_ref.dtype)
    for g in range(N_KV_GROUPS):
        vcT_ref[g] = jnp.concatenate([vcT[g * HEAD_DIM:(g + 1) * HEAD_DIM, :], ones], axis=0)


def _compress_call(kf, vf, pek, pev, wk1, wk2, wv1, wv2):
    B, S, width = kf.shape
    n = S // (CMP_BLOCK // 2)
    flat = pl.BlockSpec((None, S, width), lambda b: (b, 0, 0))
    return pl.pallas_call(
        _compress_kernel,
        grid=(B,),
        in_specs=[flat, flat, _const_spec(pek.shape), _const_spec(pev.shape),
                  _const_spec(wk1.shape), _const_spec(wk2.shape), _const_spec(wv1.shape),
                  _const_spec(wv2.shape)],
        out_specs=(pl.BlockSpec((None, N_KV_GROUPS, n, HEAD_DIM), lambda b: (b, 0, 0, 0)),
                   pl.BlockSpec((None, N_KV_GROUPS, VT_ROWS, n), lambda b: (b, 0, 0, 0))),
        out_shape=(jax.ShapeDtypeStruct((B, N_KV_GROUPS, n, HEAD_DIM), BF16),
                   jax.ShapeDtypeStruct((B, N_KV_GROUPS, VT_ROWS, n), BF16)),
        compiler_params=pltpu.CompilerParams(dimension_semantics=("arbitrary",),
                                             vmem_limit_bytes=VMEM_LIMIT),
        name="kv_compress",
    )(kf, vf, pek, pev, wk1, wk2, wv1, wv2)


def _lane_tiles(x, n):
    return jnp.concatenate([x] * n, axis=1)


def _col_max(s_ref, n_rows, bias_fn):
    groups = ROW_TILE // 8
    mx = [jnp.full((8, s_ref.shape[1]), NEG, F32)] * groups
    for r in range(0, n_rows, ROW_TILE):
        x = s_ref[r:r + ROW_TILE, :]
        if bias_fn is not None:
            x = x + bias_fn(r)
            s_ref[r:r + ROW_TILE, :] = x
        mx = [jnp.maximum(mx[i], x[8 * i:8 * (i + 1), :]) for i in range(groups)]
    while len(mx) > 1:
        mx = [jnp.maximum(a, b) for a, b in zip(mx[0::2], mx[1::2])]
    return jnp.max(mx[0], axis=0, keepdims=True)


def _col_exp2(s_ref, p_ref, n_rows, m, keep_f32=False):
    for r in range(0, n_rows, ROW_TILE):
        p = jnp.exp2(s_ref[r:r + ROW_TILE, :] - m)
        if keep_f32:
            s_ref[r:r + ROW_TILE, :] = p
        p_ref[r:r + ROW_TILE, :] = p.astype(p_ref.dtype)


def _recip_pos(l):
    return 1.0 / jnp.where(l > 0.0, l, 1.0)


def _select_kernel(top_n, qT_ref, kcc_ref, vcT_ref, ovlT_ref, sel_ref, ocmp_ref, sc_ref, pc_ref,
                   ph_ref, pl_ref):
    QB = SEL_BLOCK
    n_cmp = kcc_ref.shape[0]
    n_slc = ovlT_ref.shape[0]
    t0 = pl.program_id(2) * QB
    tq = t0 + lax.broadcasted_iota(jnp.int32, (1, QB), 1)
    rows = lax.broadcasted_iota(jnp.int32, (ROW_TILE, 1), 0)

    qT = qT_ref[...]
    qcat = jnp.concatenate([qT[h * HEAD_DIM:(h + 1) * HEAD_DIM, :] for h in range(HPG)], axis=1)

    def cmp_bias(r):
        cmp_end = (r + rows) * CMP_STRIDE + (CMP_BLOCK - 1)
        return _lane_tiles(jnp.where(cmp_end <= tq, 0.0, NEG), HPG)

    sc_ref[...] = _dot(kcc_ref[...], qcat)
    m = jnp.maximum(_col_max(sc_ref, n_cmp, cmp_bias), M_FLOOR)
    _col_exp2(sc_ref, pc_ref, n_cmp, m, keep_f32=True)
    o_cmp = _dot(vcT_ref[...], pc_ref[...])
    rl = _recip_pos(o_cmp[HEAD_DIM:HEAD_DIM + 1, :])
    o_cmp = o_cmp[0:HEAD_DIM, :] * rl

    for r in range(0, n_cmp, ROW_TILE):
        pn = sc_ref[r:r + ROW_TILE, :] * rl
        psum = pn[:, 0:QB]
        for h in range(1, HPG):
            psum = psum + pn[:, h * QB:(h + 1) * QB]
        hi, lo = _split_bf16(psum)
        ph_ref[r:r + ROW_TILE, :] = hi
        pl_ref[r:r + ROW_TILE, :] = lo
    imp =_dot(ovlT_ref[...], ph_ref[...]) + _dot(ovlT_ref[...], pl_ref[...])
    for h in range(HPG):
        ocmp_ref[h] = o_cmp[:, h * QB:(h + 1) * QB]

    cur = tq // SLC_BLOCK

    def tree(op, xs):
        while len(xs) > 1:
            xs = [op(*xs[i:i + 2]) if i + 1 < len(xs) else xs[i] for i in range(0, len(xs), 2)]
        return xs[0]

    def select_among(n_rows):
        def run():
            blk = lax.broadcasted_iota(jnp.int32, (n_rows, 1), 0)
            future = blk > cur
            forced = (blk == 0) | (blk == cur) | (blk == cur - 1)
            score0 = jnp.where(future, -BIG, jnp.where(forced, BIG, imp[0:n_rows, :]))
            score = [score0[8 * i:8 * (i + 1), :] for i in range(n_rows // 8)]
            cum = jnp.zeros((1, QB), F32)
            thr = jnp.zeros((1, QB), F32)
            above = jnp.zeros((1, QB), F32)
            for _ in range(top_n):
                best = jnp.max(tree(jnp.maximum, score), axis=0, keepdims=True)
                eq = [s == best for s in score]
                unfilled = cum < top_n
                thr = jnp.where(unfilled, best, thr)
                above = jnp.where(unfilled, cum, above)
                cum = cum + jnp.sum(tree(jnp.add, [jnp.where(e, 1.0, 0.0) for e in eq]),
                                    axis=0, keepdims=True)
                score = [jnp.where(e, -jnp.inf, s) for e, s in zip(eq, score)]
            ties = score0 == thr
            lower = jnp.where(lax.broadcasted_iota(jnp.int32, (1, n_rows), 1) < blk, 1.0, 0.0)
            rank = _dot(lower.astype(BF16), jnp.where(ties, 1.0, 0.0).astype(BF16))
            picked = (score0 > thr) | (ties & (rank < top_n - above))
            sel_ref[0:n_rows, :] = jnp.where(future, NEG, jnp.where(picked, 0.0, NEG)).astype(sel_ref.dtype)
            if n_rows < n_slc:
                sel_ref[n_rows:n_slc, :] = jnp.full((n_slc - n_rows, QB), NEG, sel_ref.dtype)
        return run

    steps = n_slc // SEL_ROW_STEP
    need = jnp.minimum((t0 + QB - 1) // (SLC_BLOCK * SEL_ROW_STEP), steps - 1)
    lax.switch(need, [select_among(SEL_ROW_STEP * (k + 1)) for k in range(steps)])


def _select_call(top_n, qT, kcc, vcT, ovlT):
    B, _, S = qT.shape
    n_cmp = kcc.shape[2]
    n_slc = ovlT.shape[0]
    gw = HPG * HEAD_DIM
    qw = HPG * SEL_BLOCK
    return pl.pallas_call(
        functools.partial(_select_kernel, top_n),
        grid=(B, N_KV_GROUPS, S // SEL_BLOCK),
        in_specs=[pl.BlockSpec((None, gw, SEL_BLOCK), lambda b, g, i: (b, g, i)),
                  pl.BlockSpec((None, None, n_cmp, HEAD_DIM), lambda b, g, i: (b, g, 0, 0)),
                  pl.BlockSpec((None, None, VT_ROWS, n_cmp), lambda b, g, i: (b, g, 0, 0)),
                  pl.BlockSpec(ovlT.shape, lambda b, g, i: (0, 0))],
        out_specs=(pl.BlockSpec((None, None, n_slc, SEL_BLOCK), lambda b, g, i: (b, g, 0, i)),
                   pl.BlockSpec((None, None, HPG, HEAD_DIM, SEL_BLOCK),
                                lambda b, g, i: (b, g, 0, 0, i))),
        out_shape=(jax.ShapeDtypeStruct((B, N_KV_GROUPS, n_slc, S), BF16),
                   jax.ShapeDtypeStruct((B, N_KV_GROUPS, HPG, HEAD_DIM, S), F32)),
        scratch_shapes=[pltpu.VMEM((n_cmp, qw), F32), pltpu.VMEM((n_cmp, qw), BF16),
                        pltpu.VMEM((n_cmp, SEL_BLOCK), BF16), pltpu.VMEM((n_cmp, SEL_BLOCK), BF16)],
        compiler_params=pltpu.CompilerParams(
            dimension_semantics=("parallel", "parallel", "arbitrary"),
            vmem_limit_bytes=VMEM_LIMIT),
        name="nsa_select",
    )(qT, kcc, vcT, ovlT)


def _attn_kernel(qT_ref, gT_ref, sel_ref, ocmp_ref, ksa_ref, vsT_ref, kw_ref, vwT_ref, ga_ref,
                 o_ref, s0_ref, s1_ref, p0_ref, p1_ref, sw_ref, pw_ref, rhs_ref):
    QB = Q_BLOCK
    n_slc = sel_ref.shape[0]
    t0 = pl.program_id(2) * QB
    tq = t0 + lax.broadcasted_iota(jnp.int32, (1, QB), 1)
    rows = lax.broadcasted_iota(jnp.int32, (ROW_TILE, 1), 0)

    qT = qT_ref[...]
    qcat = jnp.concatenate([qT[h * HEAD_DIM:(h + 1) * HEAD_DIM, :] for h in range(HPG)], axis=1)

    rhs_ref[...] = jnp.concatenate([_lane_tiles(sel_ref[...], HPG), qcat,
                                    jnp.zeros((MXU_DEPTH - n_slc - HEAD_DIM, QW), BF16)], axis=0)

    w0 = pl.multiple_of(jnp.maximum(t0 - WINDOW, 0), QB)
    sw_ref[...] = _dot(kw_ref[pl.ds(w0, WIN_KEYS), :], qcat)

    def win_bias(r):
        dist = tq - (w0 + r + rows)
        return _lane_tiles(jnp.where((dist >= 0) & (dist < WINDOW), 0.0, NEG), HPG)

    KC = KEY_CHUNK
    last = t0 // KC

    def scores(c, dst_ref):
        k0 = pl.multiple_of(jnp.minimum(c, last) * KC, KC)
        dst_ref[...] = _dot(ksa_ref[pl.ds(k0, KC), :], rhs_ref[...])

    def weighted_values(c, p_ref):
        k0 = pl.multiple_of(jnp.clip(c, 0, last) * KC, KC)
        return _dot(vsT_ref[:, pl.ds(k0, KC)], p_ref[...])

    def softmax(c, src_ref, p_ref, m, acc, causal):
        def causal_bias(r):
            return _lane_tiles(jnp.where(c * KC + r + rows <= tq, 0.0, NEG), HPG)

        m_new = jnp.maximum(m, _col_max(src_ref, KC, causal_bias if causal else None))
        _col_exp2(src_ref, p_ref, KC, m_new)
        return m_new, jnp.exp2(m - m_new) * acc

    def pair(i, carry):
        m, acc = carry
        pending = weighted_values(2 * i - 1, p1_ref)
        scores(2 * i + 1, s1_ref)
        m, acc = softmax(2 * i, s0_ref, p0_ref, m, acc + pending, False)
        pending = weighted_values(2 * i, p0_ref)
        scores(2 * i + 2, s0_ref)
        return softmax(2 * i + 1, s1_ref, p1_ref, m, acc + pending, True)

    p1_ref[...] = jnp.zeros_like(p1_ref)
    scores(0, s0_ref)

    m = _col_max(sw_ref, WIN_KEYS, win_bias)
    _col_exp2(sw_ref, pw_ref, WIN_KEYS, m)
    o_win = _dot(vwT_ref[:, pl.ds(w0, WIN_KEYS)], pw_ref[...])
    o_win = o_win[0:HEAD_DIM, :] * (1.0 / o_win[HEAD_DIM:HEAD_DIM + 1, :])

    init = (jnp.full((1, QW), NEG, F32), jnp.zeros((VT_ROWS, QW), F32))
    n_pairs = (last + 1) // 2
    carry = lax.fori_loop(0, n_pairs // 2, lambda j, cr: pair(2 * j + 1, pair(2 * j, cr)), init)
    m, acc = lax.cond(n_pairs % 2 == 1, lambda cr: pair(n_pairs - 1, cr), lambda cr: cr, carry)
    acc = acc + weighted_values(2 * n_pairs - 1, p1_ref)

    def diagonal(carry):
        m, acc = softmax(last, s0_ref, p0_ref, carry[0], carry[1], True)
        return m, acc + weighted_values(last, p0_ref)

    _, acc = lax.cond(last % 2 == 0, diagonal, lambda cr: cr, (m, acc))
    o_slc = acc[0:HEAD_DIM, :] * (1.0 / acc[HEAD_DIM:HEAD_DIM + 1, :])

    gts = gT_ref[...]
    outs = []
    for h in range(HPG):
        sl = slice(h * QB, (h + 1) * QB)
        o = (gts[3 * h:3 * h + 1, :] * ocmp_ref[h] + gts[3 * h + 1:3 * h + 2, :] * o_slc[:, sl]
             + gts[3 * h + 2:3 * h + 3, :] * o_win[:, sl])
        o = o * lax.rsqrt(jnp.mean(o * o, axis=0, keepdims=True) + EPS)
        outs.append((o * ga_ref[h * HEAD_DIM:(h + 1) * HEAD_DIM, :]).T)
    o_ref[...] = jnp.concatenate(outs, axis=1).astype(o_ref.dtype)


def _attn_call(qT, gatesT, sel_bias, o_cmp, ksa, vsT, kw, vwT, g_attn_col):
    B, _, S = qT.shape
    n_slc = sel_bias.shape[2]
    gw = HPG * HEAD_DIM
    rows = lambda n, w: pl.BlockSpec((None, None, n, w), lambda b, g, i: (b, g, 0, 0))
    cols = lambda n: pl.BlockSpec((None, None, VT_ROWS, n), lambda b, g, i: (b, g, 0, 0))
    return pl.pallas_call(
        _attn_kernel,
        grid=(B, N_KV_GROUPS, S // Q_BLOCK),
        in_specs=[pl.BlockSpec((None, gw, Q_BLOCK), lambda b, g, i: (b, g, i)),
                  pl.BlockSpec((None, GATE_PAD, Q_BLOCK), lambda b, g, i: (b, g, i)),
                  pl.BlockSpec((None, None, n_slc, Q_BLOCK), lambda b, g, i: (b, g, 0, i)),
                  pl.BlockSpec((None, None, HPG, HEAD_DIM, Q_BLOCK), lambda b, g, i: (b, g, 0, 0, i)),
                  rows(S, MXU_DEPTH), cols(S), rows(S, HEAD_DIM), cols(S),
                  pl.BlockSpec((gw, 1), lambda b, g, i: (g, 0))],
        out_specs=pl.BlockSpec((None, Q_BLOCK, gw), lambda b, g, i: (b, i, g)),
        out_shape=jax.ShapeDtypeStruct((B, S, ATTN_WIDTH), BF16),
        scratch_shapes=[pltpu.VMEM((KEY_CHUNK, QW), F32), pltpu.VMEM((KEY_CHUNK, QW), F32),
                        pltpu.VMEM((KEY_CHUNK, QW), BF16), pltpu.VMEM((KEY_CHUNK, QW), BF16),
                        pltpu.VMEM((WIN_KEYS, QW), F32), pltpu.VMEM((WIN_KEYS, QW), BF16),
                        pltpu.VMEM((MXU_DEPTH, QW), BF16)],
        compiler_params=pltpu.CompilerParams(
            dimension_semantics=("parallel", "parallel", "arbitrary"),
            vmem_limit_bytes=VMEM_LIMIT),
        name="nsa_attention",
    )(qT, gatesT, sel_bias, o_cmp, ksa, vsT, kw, vwT, g_attn_col)


def _out_kernel(x_ref, yc_ref, ya_ref, mod_ref, wo_ref, g_ref, wg_ref, wu_ref, wd_ref, gf_ref,
                o_ref):
    mix = _dot(yc_ref[...], wo_ref[0:CONV_CH, :]) + _dot(ya_ref[...], wo_ref[CONV_CH:, :])
    x = x_ref[...] + mod_ref[5:6, :] * mix
    x = _ffn_core(x, mod_ref[6:7, :], mod_ref[7:8, :], mod_ref[8:9, :], g_ref[...], wg_ref,
                  wu_ref, wd_ref)
    o_ref[...] = _rms(x, gf_ref[...])


def _out_call(x, yc, ya, mod, wo, g, wg, wu, wd, gf):
    B, S, D = x.shape
    tm = FFN_TOKEN_TILE
    tok = lambda w: pl.BlockSpec((None, tm, w), lambda b, i: (b, i, 0))
    return pl.pallas_call(
        _out_kernel,
        grid=(B, S // tm),
        in_specs=[tok(D), tok(CONV_CH), tok(ATTN_WIDTH),
                  pl.BlockSpec((None, N_MOD, D), lambda b, i: (b, 0, 0)),
                  _const_spec(wo.shape), _const_spec(g.shape), _const_spec(wg.shape),
                  _const_spec(wu.shape), _const_spec(wd.shape), _const_spec(gf.shape)],
        out_specs=tok(D),
        out_shape=jax.ShapeDtypeStruct((B, S, D), F32),
        compiler_params=pltpu.CompilerParams(dimension_semantics=("parallel", "parallel"),
                                             vmem_limit_bytes=VMEM_LIMIT),
        name="outproj_ffn2",
    )(x, yc, ya, mod, wo, g, wg, wu, wd, gf)


def kernel(x, c, positions, w_ada, b_ada, g_ffn1, w1_gate, w1_up, w1_down, g_mix, w_in, conv_w, cmp_pos_k, cmp_pos_v, w_cmpk1, w_cmpk2, w_cmpv1, w_cmpv2, g_out_conv, g_out_attn, w_out, g_ffn2, w2_gate, w2_up, w2_down, g_final):
    B, S, D = x.shape
    depth = w_ada.shape[0]
    n_slc = S // SLC_BLOCK
    half = CMP_BLOCK // 2
    n_half = S // half
    assert n_slc <= LANES, "selection-block one-hot is one lane tile wide"

    c_pad = jnp.pad(c, ((0, 8 - B), (0, 0)))
    row = lambda a: a.reshape(1, -1)

    freq_half = jnp.power(ROPE_THETA, -2.0 * jnp.arange(ROT_HALF, dtype=F32) / ROT_DIM)
    freq = jnp.tile(freq_half, LANES // ROT_HALF).reshape(1, LANES)
    gidx = np.arange(CONV_CH) // (CONV_CH // CONV_GROUPS)
    gmat = jnp.asarray((gidx[:, None] == gidx[None, :]) / (CONV_CH // CONV_GROUPS), dtype=BF16)
    c0 = np.arange(n_half) * CMP_STRIDE
    s0 = np.arange(LANES) * SLC_BLOCK
    ovlT = ((c0[None, :] <= s0[:, None] + SLC_BLOCK - 1) & (c0[None, :] + CMP_BLOCK - 1 >= s0[:, None]))
    ovlT = jnp.asarray(ovlT, dtype=BF16)
    onehot = jnp.asarray((np.arange(S) // SLC_BLOCK)[:, None] == np.arange(LANES)[None, :], dtype=BF16)
    cos_p, sin_p = _rope_table_call(positions.reshape(-1, TOKENS_PER_ROW), freq)
    cos_p = cos_p.reshape(B, S // TOKENS_PER_ROW, LANES)
    sin_p = sin_p.reshape(B, S // TOKENS_PER_ROW, LANES)

    for l in range(depth):
        mod = _ada_call(c_pad, w_ada[l], row(b_ada[l]))[:B].reshape(B, N_MOD, D)

        x = _ffn_call(x, mod, row(g_ffn1[l]), w1_gate[l].astype(BF16), w1_up[l].astype(BF16),
                      w1_down[l].astype(BF16))

        n_main = w_in.shape[2] - N_KV_GROUPS * 3 * HPG
        gate_cols = [jnp.pad(w_in[l][:, n_main + g * 3 * HPG:n_main + (g + 1) * 3 * HPG],
                             ((0, 0), (0, GATE_PAD - 3 * HPG))) for g in range(N_KV_GROUPS)]
        win = jnp.concatenate([w_in[l][:, :n_main]] + gate_cols, axis=1).astype(BF16)
        (yc, qT, kc, vc, ksa, vsT, kw, vwT, gatesT) = _inproj_call(
            x, cos_p, sin_p, mod, row(g_mix[l]), win, conv_w[l], row(g_out_conv[l]), gmat, onehot)

        both_groups = lambda pe: jnp.tile(pe, (1, N_KV_GROUPS))
        kcc, vcT = _compress_call(kc, vc, both_groups(cmp_pos_k[l]), both_groups(cmp_pos_v[l]),
                                  w_cmpk1[l], w_cmpk2[l], w_cmpv1[l], w_cmpv2[l])

        sel_bias, o_cmp = _select_call(min(SLC_TOP_N, n_slc), qT, kcc, vcT, ovlT)
        ya = _attn_call(qT, gatesT, sel_bias, o_cmp, ksa, vsT, kw, vwT,
                        g_out_attn[l].reshape(ATTN_WIDTH, 1))

        assert l == depth - 1, "final norm is fused into the last layer's output kernel"
        x = _out_call(x, yc, ya, mod, w_out[l].astype(BF16), row(g_ffn2[l]),
                      w2_gate[l].astype(BF16), w2_up[l].astype(BF16), w2_down[l].astype(BF16),
                      row(g_final))
    return x
```

```python
import functools
import math

import numpy as np
import jax
import jax.numpy as jnp
from jax import lax
from jax.experimental import pallas as pl
from jax.experimental.pallas import tpu as pltpu

F32 = jnp.float32
BF16 = jnp.bfloat16

CONV_CH = 512
CONV_GROUPS = 8
N_HEADS = 8
N_KV_GROUPS = 2
HPG = N_HEADS // N_KV_GROUPS
HEAD_DIM = 64
ATTN_WIDTH = N_HEADS * HEAD_DIM
KV_WIDTH = N_KV_GROUPS * HEAD_DIM
ROPE_THETA = 500000.0
ROT_DIM = HEAD_DIM // 4
ROT_HALF = ROT_DIM // 2
CMP_BLOCK = 32
CMP_STRIDE = 16
CMP_HIDDEN = 256
SLC_BLOCK = 64
SLC_TOP_N = 16
WINDOW = 512
Q_BLOCK = 256
MACARON_W = 0.5
N_MOD = 9
EPS = 1e-6
NEG = -1e30
BIG = 1e9

LANES = 128
MXU_DEPTH = 256
VMEM_LIMIT = 56 * 1024 * 1024

TOKEN_TILE = 512
FFN_TOKEN_TILE = 512
ADA_COL_TILE = 1024
FF_TILE = 256
KEY_CHUNK = 512
WIN_KEYS = WINDOW + Q_BLOCK
TOKENS_PER_ROW = LANES // ROT_HALF
GATE_PAD = LANES
QW = HPG * Q_BLOCK
SEL_BLOCK = 512
SEL_ROW_STEP = 32
ONES_ROWS = 16
VT_ROWS = HEAD_DIM + ONES_ROWS
ROW_TILE = 16
Q_SCALE = HEAD_DIM ** -0.5 * math.log2(math.e)
M_FLOOR = -1e20


def _dot(a, b):
    return jnp.dot(a, b, preferred_element_type=F32)


def _rms(x, g):
    return x * lax.rsqrt(jnp.mean(x * x, axis=-1, keepdims=True) + EPS) * g


def _split_bf16(x):
    hi = x.astype(BF16)
    lo = (x - hi.astype(F32)).astype(BF16)
    return hi, lo


def _const_spec(shape):
    nd = len(shape)
    return pl.BlockSpec(shape, lambda *_: (0,) * nd, pipeline_mode=pl.Buffered(1))


def _ada_kernel(c_ref, w_ref, b_ref, o_ref):
    c = c_ref[...]
    c_act = c * jax.nn.sigmoid(c)
    o_ref[...] = jnp.dot(c_act, w_ref[...], preferred_element_type=F32,
                         precision=lax.Precision.HIGHEST) + b_ref[...]


def _ada_call(c_pad, w_ada, b_ada):
    rows, d = c_pad.shape
    n = w_ada.shape[1]
    tn = ADA_COL_TILE
    return pl.pallas_call(
        _ada_kernel,
        grid=(n // tn,),
        in_specs=[pl.BlockSpec((rows, d), lambda j: (0, 0)),
                  pl.BlockSpec((d, tn), lambda j: (0, j)),
                  pl.BlockSpec((1, tn), lambda j: (0, j))],
        out_specs=pl.BlockSpec((rows, tn), lambda j: (0, j)),
        out_shape=jax.ShapeDtypeStruct((rows, n), F32),
        compiler_params=pltpu.CompilerParams(dimension_semantics=("arbitrary",),
                                             vmem_limit_bytes=VMEM_LIMIT),
        name="adaln_mod",
    )(c_pad, w_ada, b_ada)


def _ffn_core(x, shift, scale, gate, g, wg_ref, wu_ref, wd_ref):
    h = _rms(x, g) * (1.0 + scale) + shift
    hb = h.astype(BF16)
    d_ff = wg_ref.shape[1]
    acc = None
    for j in range(d_ff // FF_TILE):
        sl = slice(j * FF_TILE, (j + 1) * FF_TILE)
        gg = _dot(hb, wg_ref[:, sl])
        uu = _dot(hb, wu_ref[:, sl])
        a = (gg * jax.nn.sigmoid(gg) * uu).astype(BF16)
        d = _dot(a, wd_ref[sl, :])
        acc = d if acc is None else acc + d
    return x + (MACARON_W * gate) * acc


def _ffn_kernel(x_ref, mod_ref, g_ref, wg_ref, wu_ref, wd_ref, o_ref):
    o_ref[...] = _ffn_core(x_ref[...], mod_ref[0:1, :], mod_ref[1:2, :], mod_ref[2:3, :],
                           g_ref[...], wg_ref, wu_ref, wd_ref)


def _ffn_call(x, mod, g, wg, wu, wd):
    B, S, D = x.shape
    tm = FFN_TOKEN_TILE
    return pl.pallas_call(
        _ffn_kernel,
        grid=(B, S // tm),
        in_specs=[pl.BlockSpec((None, tm, D), lambda b, i: (b, i, 0)),
                  pl.BlockSpec((None, N_MOD, D), lambda b, i: (b, 0, 0)),
                  _const_spec(g.shape), _const_spec(wg.shape), _const_spec(wu.shape),
                  _const_spec(wd.shape)],
        out_specs=pl.BlockSpec((None, tm, D), lambda b, i: (b, i, 0)),
        out_shape=jax.ShapeDtypeStruct((B, S, D), F32),
        compiler_params=pltpu.CompilerParams(dimension_semantics=("parallel", "parallel"),
                                             vmem_limit_bytes=VMEM_LIMIT),
        name="ffn1",
    )(x, mod, g, wg, wu, wd)


def _rope_table_kernel(pos_ref, freq_ref, cos_ref, sin_ref):
    pos = jnp.concatenate([pos_ref[...].astype(F32),
                           jnp.zeros((pos_ref.shape[0], LANES - TOKENS_PER_ROW), F32)], axis=1)
    src = lax.broadcasted_iota(jnp.int32, pos.shape, 1) // ROT_HALF
    ang = jnp.take_along_axis(pos, src, axis=1) * freq_ref[...]
    cos_ref[...] = jnp.cos(ang)
    sin_ref[...] = jnp.sin(ang)


def _rope_table_call(pos_rows, freq):
    rows = pos_rows.shape[0]
    shape = jax.ShapeDtypeStruct((rows, LANES), F32)
    full = pl.BlockSpec((rows, LANES), lambda: (0, 0))
    return pl.pallas_call(
        _rope_table_kernel,
        in_specs=[pl.BlockSpec(pos_rows.shape, lambda: (0, 0)), pl.BlockSpec(freq.shape, lambda: (0, 0))],
        out_specs=(full, full),
        out_shape=(shape, shape),
        compiler_params=pltpu.CompilerParams(vmem_limit_bytes=VMEM_LIMIT),
        name="rope_table",
    )(pos_rows, freq)


def _inproj_kernel(x_ref, cosp_ref, sinp_ref, mod_ref, gmix_ref, win_ref, convw_ref, gconv_ref,
                   gmat_ref, onehot_ref, yc_ref, qT_ref, kc_ref, vc_ref, ksa_ref, vsT_ref, kw_ref,
                   vwT_ref, gT_ref, carry_ref):
    tm = x_ref.shape[0]

    @pl.when(pl.program_id(1) == 0)
    def _():
        carry_ref[...] = jnp.zeros_like(carry_ref)

    x = x_ref[...]
    h = _rms(x, gmix_ref[...]) * (1.0 + mod_ref[4:5, :]) + mod_ref[3:4, :]
    hb = h.astype(BF16)

    def proj(c0, width):
        return _dot(hb, win_ref[:, c0:c0 + width])

    cb = proj(0, CONV_CH)
    u = proj(CONV_CH, CONV_CH) * proj(2 * CONV_CH, CONV_CH)
    row = lax.broadcasted_iota(jnp.int32, (tm, 1), 0)
    prev1 = carry_ref[7:8, :]
    prev2 = carry_ref[6:7, :]
    u1 = jnp.where(row >= 1, pltpu.roll(u, 1, 0), prev1)
    u2 = jnp.where(row >= 2, pltpu.roll(u, 2, 0), jnp.where(row == 1, prev1, prev2))
    carry_ref[...] = u[tm - 8:tm, :]
    v = convw_ref[0:1, :] * u2 + convw_ref[1:2, :] * u1 + convw_ref[2:3, :] * u
    y = cb * v
    hi, lo = _split_bf16(y * y)
    ms = _dot(hi, gmat_ref[...]) + _dot(lo, gmat_ref[...])
    yc_ref[...] = (y * lax.rsqrt(ms + EPS) * gconv_ref[...]).astype(yc_ref.dtype)

    d = lax.broadcasted_iota(jnp.int32, (tm, LANES), 1) & (HEAD_DIM - 1)
    token = lax.broadcasted_iota(jnp.int32, (tm, LANES), 0) & (TOKENS_PER_ROW - 1)
    src_lane = token * ROT_HALF + (d & (ROT_HALF - 1))

    def unpack(packed_ref):
        rows = jnp.broadcast_to(packed_ref[...][:, None, :], (tm // TOKENS_PER_ROW, TOKENS_PER_ROW, LANES))
        return jnp.take_along_axis(rows.reshape(tm, LANES), src_lane, axis=1)

    cos_t = jnp.where(d < ROT_DIM, unpack(cosp_ref), 1.0)
    sin_raw = unpack(sinp_ref)
    sin_t = jnp.where(d < ROT_HALF, -sin_raw, jnp.where(d < ROT_DIM, sin_raw, 0.0))
    first_half = d < ROT_HALF

    def rope(t):
        outs = []
        for j in range(t.shape[1] // LANES):
            tj = t[:, j * LANES:(j + 1) * LANES]
            partner = jnp.where(first_half, pltpu.roll(tj, LANES - ROT_HALF, 1),
                                pltpu.roll(tj, ROT_HALF, 1))
            outs.append(tj * cos_t + partner * sin_t)
        return outs[0] if len(outs) == 1 else jnp.concatenate(outs, axis=1)

    c0 = 3 * CONV_CH
    q = rope(proj(c0, ATTN_WIDTH)) * Q_SCALE
    qT_ref[...] = q.T.astype(qT_ref.dtype)
    c0 += ATTN_WIDTH
    kv = proj(c0, 2 * KV_WIDTH)
    kc_ref[...] = rope(kv[:, :KV_WIDTH])
    vc_ref[...] = kv[:, KV_WIDTH:]
    kv = proj(c0 + 2 * KV_WIDTH, 2 * KV_WIDTH)
    ks = rope(kv[:, :KV_WIDTH]).astype(BF16)
    vsT = kv[:, KV_WIDTH:].T.astype(BF16)
    kv = proj(c0 + 4 * KV_WIDTH, 2 * KV_WIDTH)
    kw = rope(kv[:, :KV_WIDTH]).astype(BF16)
    vwT = kv[:, KV_WIDTH:].T.astype(BF16)
    pad = jnp.zeros((tm, MXU_DEPTH - LANES - HEAD_DIM), BF16)
    ones = jnp.ones((ONES_ROWS, tm), BF16)
    for g in range(N_KV_GROUPS):
        kg = ks[:, g * HEAD_DIM:(g + 1) * HEAD_DIM]
        ksa_ref[g] = jnp.concatenate([onehot_ref[...], kg, pad], axis=1)
        kw_ref[g] = kw[:, g * HEAD_DIM:(g + 1) * HEAD_DIM]
        vsT_ref[g] = jnp.concatenate([vsT[g * HEAD_DIM:(g + 1) * HEAD_DIM, :], ones], axis=0)
        vwT_ref[g] = jnp.concatenate([vwT[g * HEAD_DIM:(g + 1) * HEAD_DIM, :], ones], axis=0)
    gT_ref[...] = jax.nn.sigmoid(proj(c0 + 6 * KV_WIDTH, N_KV_GROUPS * GATE_PAD)).T


def _inproj_call(x, cos_p, sin_p, mod, gmix, win, convw, gconv, gmat, onehot):
    B, S, D = x.shape
    tm = TOKEN_TILE
    tok = lambda w: pl.BlockSpec((None, tm, w), lambda b, i: (b, i, 0))
    tr = lambda w: pl.BlockSpec((None, w, tm), lambda b, i: (b, 0, i))
    grp = lambda w: pl.BlockSpec((None, N_KV_GROUPS, tm, w), lambda b, i: (b, 0, i, 0))
    grpT = pl.BlockSpec((None, N_KV_GROUPS, VT_ROWS, tm), lambda b, i: (b, 0, 0, i))
    packed = pl.BlockSpec((None, tm // TOKENS_PER_ROW, LANES), lambda b, i: (b, i, 0))
    out_shapes = (
        jax.ShapeDtypeStruct((B, S, CONV_CH), BF16),
        jax.ShapeDtypeStruct((B, ATTN_WIDTH, S), BF16),
        jax.ShapeDtypeStruct((B, S, KV_WIDTH), F32),
        jax.ShapeDtypeStruct((B, S, KV_WIDTH), F32),
        jax.ShapeDtypeStruct((B, N_KV_GROUPS, S, MXU_DEPTH), BF16),
        jax.ShapeDtypeStruct((B, N_KV_GROUPS, VT_ROWS, S), BF16),
        jax.ShapeDtypeStruct((B, N_KV_GROUPS, S, HEAD_DIM), BF16),
        jax.ShapeDtypeStruct((B, N_KV_GROUPS, VT_ROWS, S), BF16),
        jax.ShapeDtypeStruct((B, N_KV_GROUPS * GATE_PAD, S), F32),
    )
    return pl.pallas_call(
        _inproj_kernel,
        grid=(B, S // tm),
        in_specs=[tok(D),
                  packed, packed,
                  pl.BlockSpec((None, N_MOD, D), lambda b, i: (b, 0, 0)),
                  _const_spec(gmix.shape), _const_spec(win.shape), _const_spec(convw.shape),
                  _const_spec(gconv.shape), _const_spec(gmat.shape),
                  pl.BlockSpec((tm, onehot.shape[1]), lambda b, i: (i, 0))],
        out_specs=(tok(CONV_CH), tr(ATTN_WIDTH), tok(KV_WIDTH), tok(KV_WIDTH), grp(MXU_DEPTH),
                   grpT, grp(HEAD_DIM), grpT, tr(N_KV_GROUPS * GATE_PAD)),
        out_shape=out_shapes,
        scratch_shapes=[pltpu.VMEM((8, CONV_CH), F32)],
        compiler_params=pltpu.CompilerParams(dimension_semantics=("arbitrary", "arbitrary"),
                                             vmem_limit_bytes=VMEM_LIMIT),
        name="mixer_inproj",
    )(x, cos_p, sin_p, mod, gmix, win, convw, gconv, gmat, onehot)


def _compress_kernel(kf_ref, vf_ref, pek_ref, pev_ref, wk1_ref, wk2_ref, wv1_ref, wv2_ref,
                     kcc_ref, vcT_ref):
    half = CMP_BLOCK // 2
    n = kf_ref.shape[0] // half

    def mlp(x_ref, pe_ref, w1_ref, w2_ref):
        parts = []
        for p in range(2):
            acc = None
            for l0 in range(0, half, 2):
                xs, ws = [], []
                for l in (l0, l0 + 1):
                    row = p * half + l
                    xs.append((x_ref[pl.ds(l, n, stride=half), :]
                               + pe_ref[row:row + 1, :]).astype(BF16))
                    w = w1_ref[row * HEAD_DIM:(row + 1) * HEAD_DIM, :].astype(BF16)
                    z = jnp.zeros_like(w)
                    ws.append(jnp.concatenate([jnp.concatenate([w, z], axis=1),
                                               jnp.concatenate([z, w], axis=1)], axis=0))
                d = _dot(jnp.concatenate(xs, axis=1), jnp.concatenate(ws, axis=0))
                acc = d if acc is None else acc + d
            parts.append(acc)
        hpre = parts[0] + pltpu.roll(parts[1], n - 1, 0)
        hid = (hpre * jax.nn.sigmoid(hpre)).astype(BF16)
        w2 = w2_ref[...].astype(BF16)
        return jnp.concatenate([_dot(hid[:, g * CMP_HIDDEN:(g + 1) * CMP_HIDDEN], w2)
                                for g in range(N_KV_GROUPS)], axis=1)

    kc = mlp(kf_ref, pek_ref, wk1_ref, wk2_ref).astype(kcc_ref.dtype)
    for g in range(N_KV_GROUPS):
        kcc_ref[g] = kc[:, g * HEAD_DIM:(g + 1) * HEAD_DIM]
    vcT = mlp(vf_ref, pev_ref, wv1_ref, wv2_ref).T.astype(vcT_ref.dtype)
    ones = jnp.ones((ONES_ROWS, n), vcT_ref.dtype)
    for g in range(N_KV_GROUPS):
        vcT_ref[g] = jnp.concatenate([vcT[g * HEAD_DIM:(g + 1) * HEAD_DIM, :], ones], axis=0)


def _compress_call(kf, vf, pek, pev, wk1, wk2, wv1, wv2):
    B, S, width = kf.shape
    n = S // (CMP_BLOCK // 2)
    flat = pl.BlockSpec((None, S, width), lambda b: (b, 0, 0))
    return pl.pallas_call(
        _compress_kernel,
        grid=(B,),
        in_specs=[flat, flat, _const_spec(pek.shape), _const_spec(pev.shape),
                  _const_spec(wk1.shape), _const_spec(wk2.shape), _const_spec(wv1.shape),
                  _const_spec(wv2.shape)],
        out_specs=(pl.BlockSpec((None, N_KV_GROUPS, n, HEAD_DIM), lambda b: (b, 0, 0, 0)),
                   pl.BlockSpec((None, N_KV_GROUPS, VT_ROWS, n), lambda b: (b, 0, 0, 0))),
        out_shape=(jax.ShapeDtypeStruct((B, N_KV_GROUPS, n, HEAD_DIM), BF16),
                   jax.ShapeDtypeStruct((B, N_KV_GROUPS, VT_ROWS, n), BF16)),
        compiler_params=pltpu.CompilerParams(dimension_semantics=("arbitrary",),
                                             vmem_limit_bytes=VMEM_LIMIT),
        name="kv_compress",
    )(kf, vf, pek, pev, wk1, wk2, wv1, wv2)


def _lane_tiles(x, n):
    return jnp.concatenate([x] * n, axis=1)


def _col_max(s_ref, n_rows, bias_fn):
    groups = ROW_TILE // 8
    mx = [jnp.full((8, s_ref.shape[1]), NEG, F32)] * groups
    for r in range(0, n_rows, ROW_TILE):
        x = s_ref[r:r + ROW_TILE, :]
        if bias_fn is not None:
            x = x + bias_fn(r)
            s_ref[r:r + ROW_TILE, :] = x
        mx = [jnp.maximum(mx[i], x[8 * i:8 * (i + 1), :]) for i in range(groups)]
    while len(mx) > 1:
        mx = [jnp.maximum(a, b) for a, b in zip(mx[0::2], mx[1::2])]
    return jnp.max(mx[0], axis=0, keepdims=True)


def _col_exp2(s_ref, p_ref, n_rows, m, keep_f32=False):
    for r in range(0, n_rows, ROW_TILE):
        p = jnp.exp2(s_ref[r:r + ROW_TILE, :] - m)
        if keep_f32:
            s_ref[r:r + ROW_TILE, :] = p
        p_ref[r:r + ROW_TILE, :] = p.astype(p_ref.dtype)


def _recip_pos(l):
    return 1.0 / jnp.where(l > 0.0, l, 1.0)


def _select_kernel(top_n, qT_ref, kcc_ref, vcT_ref, ovlT_ref, sel_ref, ocmp_ref, sc_ref, pc_ref,
                   ph_ref, pl_ref):
    QB = SEL_BLOCK
    n_cmp = kcc_ref.shape[0]
    n_slc = ovlT_ref.shape[0]
    t0 = pl.program_id(2) * QB
    tq = t0 + lax.broadcasted_iota(jnp.int32, (1, QB), 1)
    rows = lax.broadcasted_iota(jnp.int32, (ROW_TILE, 1), 0)

    qT = qT_ref[...]
    qcat = jnp.concatenate([qT[h * HEAD_DIM:(h + 1) * HEAD_DIM, :] for h in range(HPG)], axis=1)

    def cmp_bias(r):
        cmp_end = (r + rows) * CMP_STRIDE + (CMP_BLOCK - 1)
        return _lane_tiles(jnp.where(cmp_end <= tq, 0.0, NEG), HPG)

    cur = tq // SLC_BLOCK

    def tree(op, xs):
        while len(xs) > 1:
            xs = [op(*xs[i:i + 2]) if i + 1 < len(xs) else xs[i] for i in range(0, len(xs), 2)]
        return xs[0]

    def causal_variant(n_rows):
        rows_cmp = min(n_cmp, n_rows * (SLC_BLOCK // CMP_STRIDE))

        def run():
            sc_ref[0:rows_cmp, :] = _dot(kcc_ref[0:rows_cmp, :], qcat)
            m = jnp.maximum(_col_max(sc_ref, rows_cmp, cmp_bias), M_FLOOR)
            _col_exp2(sc_ref, pc_ref, rows_cmp, m, keep_f32=True)
            o_cmp = _dot(vcT_ref[:, 0:rows_cmp], pc_ref[0:rows_cmp, :])
            rl = _recip_pos(o_cmp[HEAD_DIM:HEAD_DIM + 1, :])
            o_cmp = o_cmp[0:HEAD_DIM, :] * rl
            for h in range(HPG):
                ocmp_ref[h] = o_cmp[:, h * QB:(h + 1) * QB]

            for r in range(0, rows_cmp, ROW_TILE):
                pn = sc_ref[r:r + ROW_TILE, :] * rl
                psum = pn[:, 0:QB]
                for h in range(1, HPG):
                    psum = psum + pn[:, h * QB:(h + 1) * QB]
                hi, lo = _split_bf16(psum)
                ph_ref[r:r + ROW_TILE, :] = hi
                pl_ref[r:r + ROW_TILE, :] = lo
            ovl = ovlT_ref[0:n_rows, 0:rows_cmp]
            imp = _dot(ovl, ph_ref[0:rows_cmp, :]) + _dot(ovl, pl_ref[0:rows_cmp, :])

            blk = lax.broadcasted_iota(jnp.int32, (n_rows, 1), 0)
            future = blk > cur
            forced = (blk == 0) | (blk == cur) | (blk == cur - 1)
            score0 = jnp.where(future, -BIG, jnp.where(forced, BIG, imp))
            score = [score0[8 * i:8 * (i + 1), :] for i in range(n_rows // 8)]
            cum = jnp.zeros((1, QB), F32)
            thr = jnp.zeros((1, QB), F32)
            above = jnp.zeros((1, QB), F32)
            for _ in range(top_n):
                best = jnp.max(tree(jnp.maximum, score), axis=0, keepdims=True)
                eq = [s == best for s in score]
                unfilled = cum < top_n
                thr = jnp.where(unfilled, best, thr)
                above = jnp.where(unfilled, cum, above)
                cum = cum + jnp.sum(tree(jnp.add, [jnp.where(e, 1.0, 0.0) for e in eq]),
                                    axis=0, keepdims=True)
                score = [jnp.where(e, -jnp.inf, s) for e, s in zip(eq, score)]
            ties = score0 == thr
            lower = jnp.where(lax.broadcasted_iota(jnp.int32, (1, n_rows), 1) < blk, 1.0, 0.0)
            rank = _dot(lower.astype(BF16), jnp.where(ties, 1.0, 0.0).astype(BF16))
            picked = (score0 > thr) | (ties & (rank < top_n - above))
            sel_ref[0:n_rows, :] = jnp.where(future, NEG, jnp.where(picked, 0.0, NEG)).astype(sel_ref.dtype)
            if n_rows < n_slc:
                sel_ref[n_rows:n_slc, :] = jnp.full((n_slc - n_rows, QB), NEG, sel_ref.dtype)
        return run

    steps = n_slc // SEL_ROW_STEP
    need = jnp.minimum((t0 + QB - 1) // (SLC_BLOCK * SEL_ROW_STEP), steps - 1)
    lax.switch(need, [causal_variant(SEL_ROW_STEP * (k + 1)) for k in range(steps)])


def _select_call(top_n, qT, kcc, vcT, ovlT):
    B, _, S = qT.shape
    n_cmp = kcc.shape[2]
    n_slc = ovlT.shape[0]
    gw = HPG * HEAD_DIM
    qw = HPG * SEL_BLOCK
    return pl.pallas_call(
        functools.partial(_select_kernel, top_n),
        grid=(B, N_KV_GROUPS, S // SEL_BLOCK),
        in_specs=[pl.BlockSpec((None, gw, SEL_BLOCK), lambda b, g, i: (b, g, i)),
                  pl.BlockSpec((None, None, n_cmp, HEAD_DIM), lambda b, g, i: (b, g, 0, 0)),
                  pl.BlockSpec((None, None, VT_ROWS, n_cmp), lambda b, g, i: (b, g, 0, 0)),
                  pl.BlockSpec(ovlT.shape, lambda b, g, i: (0, 0))],
        out_specs=(pl.BlockSpec((None, None, n_slc, SEL_BLOCK), lambda b, g, i: (b, g, 0, i)),
                   pl.BlockSpec((None, None, HPG, HEAD_DIM, SEL_BLOCK),
                                lambda b, g, i: (b, g, 0, 0, i))),
        out_shape=(jax.ShapeDtypeStruct((B, N_KV_GROUPS, n_slc, S), BF16),
                   jax.ShapeDtypeStruct((B, N_KV_GROUPS, HPG, HEAD_DIM, S), F32)),
        scratch_shapes=[pltpu.VMEM((n_cmp, qw), F32), pltpu.VMEM((n_cmp, qw), BF16),
                        pltpu.VMEM((n_cmp, SEL_BLOCK), BF16), pltpu.VMEM((n_cmp, SEL_BLOCK), BF16)],
        compiler_params=pltpu.CompilerParams(
            dimension_semantics=("parallel", "parallel", "arbitrary"),
            vmem_limit_bytes=VMEM_LIMIT),
        name="nsa_select",
    )(qT, kcc, vcT, ovlT)


def _attn_kernel(qT_ref, gT_ref, sel_ref, ocmp_ref, ksa_ref, vsT_ref, kw_ref, vwT_ref, ga_ref,
                 o_ref, s0_ref, s1_ref, p0_ref, p1_ref, sw_ref, pw_ref, rhs_ref):
    QB = Q_BLOCK
    n_slc = sel_ref.shape[0]
    t0 = pl.program_id(2) * QB
    tq = t0 + lax.broadcasted_iota(jnp.int32, (1, QB), 1)
    rows = lax.broadcasted_iota(jnp.int32, (ROW_TILE, 1), 0)

    qT = qT_ref[...]
    qcat = jnp.concatenate([qT[h * HEAD_DIM:(h + 1) * HEAD_DIM, :] for h in range(HPG)], axis=1)

    rhs_ref[...] = jnp.concatenate([_lane_tiles(sel_ref[...], HPG), qcat,
                                    jnp.zeros((MXU_DEPTH - n_slc - HEAD_DIM, QW), BF16)], axis=0)

    w0 = pl.multiple_of(jnp.maximum(t0 - WINDOW, 0), QB)
    sw_ref[...] = _dot(kw_ref[pl.ds(w0, WIN_KEYS), :], qcat)

    def win_bias(r):
        dist = tq - (w0 + r + rows)
        return _lane_tiles(jnp.where((dist >= 0) & (dist < WINDOW), 0.0, NEG), HPG)

    KC = KEY_CHUNK
    last = t0 // KC

    def scores(c, dst_ref):
        k0 = pl.multiple_of(jnp.minimum(c, last) * KC, KC)
        dst_ref[...] = _dot(ksa_ref[pl.ds(k0, KC), :], rhs_ref[...])

    def weighted_values(c, p_ref):
        k0 = pl.multiple_of(jnp.clip(c, 0, last) * KC, KC)
        return _dot(vsT_ref[:, pl.ds(k0, KC)], p_ref[...])

    def softmax(c, src_ref, p_ref, m, acc, causal):
        def causal_bias(r):
            return _lane_tiles(jnp.where(c * KC + r + rows <= tq, 0.0, NEG), HPG)

        m_new = jnp.maximum(m, _col_max(src_ref, KC, causal_bias if causal else None))
        _col_exp2(src_ref, p_ref, KC, m_new)
        return m_new, jnp.exp2(m - m_new) * acc

    def pair(i, carry):
        m, acc = carry
        pending = weighted_values(2 * i - 1, p1_ref)
        scores(2 * i + 1, s1_ref)
        m, acc = softmax(2 * i, s0_ref, p0_ref, m, acc + pending, False)
        pending = weighted_values(2 * i, p0_ref)
        scores(2 * i + 2, s0_ref)
        return softmax(2 * i + 1, s1_ref, p1_ref, m, acc + pending, True)

    p1_ref[...] = jnp.zeros_like(p1_ref)
    scores(0, s0_ref)

    m = _col_max(sw_ref, WIN_KEYS, win_bias)
    _col_exp2(sw_ref, pw_ref, WIN_KEYS, m)
    o_win = _dot(vwT_ref[:, pl.ds(w0, WIN_KEYS)], pw_ref[...])
    o_win = o_win[0:HEAD_DIM, :] * (1.0 / o_win[HEAD_DIM:HEAD_DIM + 1, :])

    init = (jnp.full((1, QW), NEG, F32), jnp.zeros((VT_ROWS, QW), F32))
    n_pairs = (last + 1) // 2
    carry = lax.fori_loop(0, n_pairs // 2, lambda j, cr: pair(2 * j + 1, pair(2 * j, cr)), init)
    m, acc = lax.cond(n_pairs % 2 == 1, lambda cr: pair(n_pairs - 1, cr), lambda cr: cr, carry)
    acc = acc + weighted_values(2 * n_pairs - 1, p1_ref)

    def diagonal(carry):
        m, acc = softmax(last, s0_ref, p0_ref, carry[0], carry[1], True)
        return m, acc + weighted_values(last, p0_ref)

    _, acc = lax.cond(last % 2 == 0, diagonal, lambda cr: cr, (m, acc))
    o_slc = acc[0:HEAD_DIM, :] * (1.0 / acc[HEAD_DIM:HEAD_DIM + 1, :])

    gts = gT_ref[...]
    outs = []
    for h in range(HPG):
        sl = slice(h * QB, (h + 1) * QB)
        o = (gts[3 * h:3 * h + 1, :] * ocmp_ref[h] + gts[3 * h + 1:3 * h + 2, :] * o_slc[:, sl]
             + gts[3 * h + 2:3 * h + 3, :] * o_win[:, sl])
        o = o * lax.rsqrt(jnp.mean(o * o, axis=0, keepdims=True) + EPS)
        outs.append((o * ga_ref[h * HEAD_DIM:(h + 1) * HEAD_DIM, :]).T)
    o_ref[...] = jnp.concatenate(outs, axis=1).astype(o_ref.dtype)


def _attn_call(qT, gatesT, sel_bias, o_cmp, ksa, vsT, kw, vwT, g_attn_col):
    B, _, S = qT.shape
    n_slc = sel_bias.shape[2]
    gw = HPG * HEAD_DIM
    rows = lambda n, w: pl.BlockSpec((None, None, n, w), lambda b, g, i: (b, g, 0, 0))
    cols = lambda n: pl.BlockSpec((None, None, VT_ROWS, n), lambda b, g, i: (b, g, 0, 0))
    return pl.pallas_call(
        _attn_kernel,
        grid=(B, N_KV_GROUPS, S // Q_BLOCK),
        in_specs=[pl.BlockSpec((None, gw, Q_BLOCK), lambda b, g, i: (b, g, i)),
                  pl.BlockSpec((None, GATE_PAD, Q_BLOCK), lambda b, g, i: (b, g, i)),
                  pl.BlockSpec((None, None, n_slc, Q_BLOCK), lambda b, g, i: (b, g, 0, i)),
                  pl.BlockSpec((None, None, HPG, HEAD_DIM, Q_BLOCK), lambda b, g, i: (b, g, 0, 0, i)),
                  rows(S, MXU_DEPTH), cols(S), rows(S, HEAD_DIM), cols(S),
                  pl.BlockSpec((gw, 1), lambda b, g, i: (g, 0))],
        out_specs=pl.BlockSpec((None, Q_BLOCK, gw), lambda b, g, i: (b, i, g)),
        out_shape=jax.ShapeDtypeStruct((B, S, ATTN_WIDTH), BF16),
        scratch_shapes=[pltpu.VMEM((KEY_CHUNK, QW), F32), pltpu.VMEM((KEY_CHUNK, QW), F32),
                        pltpu.VMEM((KEY_CHUNK, QW), BF16), pltpu.VMEM((KEY_CHUNK, QW), BF16),
                        pltpu.VMEM((WIN_KEYS, QW), F32), pltpu.VMEM((WIN_KEYS, QW), BF16),
                        pltpu.VMEM((MXU_DEPTH, QW), BF16)],
        compiler_params=pltpu.CompilerParams(
            dimension_semantics=("parallel", "parallel", "arbitrary"),
            vmem_limit_bytes=VMEM_LIMIT),
        name="nsa_attention",
    )(qT, gatesT, sel_bias, o_cmp, ksa, vsT, kw, vwT, g_attn_col)


def _out_kernel(x_ref, yc_ref, ya_ref, mod_ref, wo_ref, g_ref, wg_ref, wu_ref, wd_ref, gf_ref,
                o_ref):
    mix = _dot(yc_ref[...], wo_ref[0:CONV_CH, :]) + _dot(ya_ref[...], wo_ref[CONV_CH:, :])
    x = x_ref[...] + mod_ref[5:6, :] * mix
    x = _ffn_core(x, mod_ref[6:7, :], mod_ref[7:8, :], mod_ref[8:9, :], g_ref[...], wg_ref,
                  wu_ref, wd_ref)
    o_ref[...] = _rms(x, gf_ref[...])


def _out_call(x, yc, ya, mod, wo, g, wg, wu, wd, gf):
    B, S, D = x.shape
    tm = FFN_TOKEN_TILE
    tok = lambda w: pl.BlockSpec((None, tm, w), lambda b, i: (b, i, 0))
    return pl.pallas_call(
        _out_kernel,
        grid=(B, S // tm),
        in_specs=[tok(D), tok(CONV_CH), tok(ATTN_WIDTH),
                  pl.BlockSpec((None, N_MOD, D), lambda b, i: (b, 0, 0)),
                  _const_spec(wo.shape), _const_spec(g.shape), _const_spec(wg.shape),
                  _const_spec(wu.shape), _const_spec(wd.shape), _const_spec(gf.shape)],
        out_specs=tok(D),
        out_shape=jax.ShapeDtypeStruct((B, S, D), F32),
        compiler_params=pltpu.CompilerParams(dimension_semantics=("parallel", "parallel"),
                                             vmem_limit_bytes=VMEM_LIMIT),
        name="outproj_ffn2",
    )(x, yc, ya, mod, wo, g, wg, wu, wd, gf)


def kernel(x, c, positions, w_ada, b_ada, g_ffn1, w1_gate, w1_up, w1_down, g_mix, w_in, conv_w, cmp_pos_k, cmp_pos_v, w_cmpk1, w_cmpk2, w_cmpv1, w_cmpv2, g_out_conv, g_out_attn, w_out, g_ffn2, w2_gate, w2_up, w2_down, g_final):
    B, S, D = x.shape
    depth = w_ada.shape[0]
    n_slc = S // SLC_BLOCK
    half = CMP_BLOCK // 2
    n_half = S // half
    assert n_slc <= LANES, "selection-block one-hot is one lane tile wide"

    c_pad = jnp.pad(c, ((0, 8 - B), (0, 0)))
    row = lambda a: a.reshape(1, -1)

    freq_half = jnp.power(ROPE_THETA, -2.0 * jnp.arange(ROT_HALF, dtype=F32) / ROT_DIM)
    freq = jnp.tile(freq_half, LANES // ROT_HALF).reshape(1, LANES)
    gidx = np.arange(CONV_CH) // (CONV_CH // CONV_GROUPS)
    gmat = jnp.asarray((gidx[:, None] == gidx[None, :]) / (CONV_CH // CONV_GROUPS), dtype=BF16)
    c0 = np.arange(n_half) * CMP_STRIDE
    s0 = np.arange(LANES) * SLC_BLOCK
    ovlT = ((c0[None, :] <= s0[:, None] + SLC_BLOCK - 1) & (c0[None, :] + CMP_BLOCK - 1 >= s0[:, None]))
    ovlT = jnp.asarray(ovlT, dtype=BF16)
    onehot = jnp.asarray((np.arange(S) // SLC_BLOCK)[:, None] == np.arange(LANES)[None, :], dtype=BF16)
    cos_p, sin_p = _rope_table_call(positions.reshape(-1, TOKENS_PER_ROW), freq)
    cos_p = cos_p.reshape(B, S // TOKENS_PER_ROW, LANES)
    sin_p = sin_p.reshape(B, S // TOKENS_PER_ROW, LANES)

    for l in range(depth):
        mod = _ada_call(c_pad, w_ada[l], row(b_ada[l]))[:B].reshape(B, N_MOD, D)

        x = _ffn_call(x, mod, row(g_ffn1[l]), w1_gate[l].astype(BF16), w1_up[l].astype(BF16),
                      w1_down[l].astype(BF16))

        n_main = w_in.shape[2] - N_KV_GROUPS * 3 * HPG
        gate_cols = [jnp.pad(w_in[l][:, n_main + g * 3 * HPG:n_main + (g + 1) * 3 * HPG],
                             ((0, 0), (0, GATE_PAD - 3 * HPG))) for g in range(N_KV_GROUPS)]
        win = jnp.concatenate([w_in[l][:, :n_main]] + gate_cols, axis=1).astype(BF16)
        (yc, qT, kc, vc, ksa, vsT, kw, vwT, gatesT) = _inproj_call(
            x, cos_p, sin_p, mod, row(g_mix[l]), win, conv_w[l], row(g_out_conv[l]), gmat, onehot)

        both_groups = lambda pe: jnp.tile(pe, (1, N_KV_GROUPS))
        kcc, vcT = _compress_call(kc, vc, both_groups(cmp_pos_k[l]), both_groups(cmp_pos_v[l]),
                                  w_cmpk1[l], w_cmpk2[l], w_cmpv1[l], w_cmpv2[l])

        sel_bias, o_cmp = _select_call(min(SLC_TOP_N, n_slc), qT, kcc, vcT, ovlT)
        ya = _attn_call(qT, gatesT, sel_bias, o_cmp, ksa, vsT, kw, vwT,
                        g_out_attn[l].reshape(ATTN_WIDTH, 1))

        assert l == depth - 1, "final norm is fused into the last layer's output kernel"
        x = _out_call(x, yc, ya, mod, w_out[l].astype(BF16), row(g_ffn2[l]),
                      w2_gate[l].astype(BF16), w2_up[l].astype(BF16), w2_down[l].astype(BF16),
                      row(g_final))
    return x
```

```python
import functools
import math

import numpy as np
import jax
import jax.numpy as jnp
from jax import lax
from jax.experimental import pallas as pl
from jax.experimental.pallas import tpu as pltpu

F32 = jnp.float32
BF16 = jnp.bfloat16

CONV_CH = 512
CONV_GROUPS = 8
N_HEADS = 8
N_KV_GROUPS = 2
HPG = N_HEADS // N_KV_GROUPS
HEAD_DIM = 64
ATTN_WIDTH = N_HEADS * HEAD_DIM
KV_WIDTH = N_KV_GROUPS * HEAD_DIM
ROPE_THETA = 500000.0
ROT_DIM = HEAD_DIM // 4
ROT_HALF = ROT_DIM // 2
CMP_BLOCK = 32
CMP_STRIDE = 16
CMP_HIDDEN = 256
SLC_BLOCK = 64
SLC_TOP_N = 16
WINDOW = 512
Q_BLOCK = 256
MACARON_W = 0.5
N_MOD = 9
EPS = 1e-6
NEG = -1e30
BIG = 1e9

LANES = 128
MXU_DEPTH = 256
VMEM_LIMIT = 56 * 1024 * 1024

TOKEN_TILE = 512
FFN_TOKEN_TILE = 512
ADA_COL_TILE = 1024
FF_TILE = 256
KEY_CHUNK = 512
WIN_KEYS = WINDOW + Q_BLOCK
TOKENS_PER_ROW = LANES // ROT_HALF
GATE_PAD = LANES
QW = HPG * Q_BLOCK
SEL_BLOCK = 512
SEL_ROW_STEP = 32
ONES_ROWS = 16
VT_ROWS = HEAD_DIM + ONES_ROWS
ROW_TILE = 16
Q_SCALE = HEAD_DIM ** -0.5 * math.log2(math.e)
M_FLOOR = -1e20


def _dot(a, b):
    return jnp.dot(a, b, preferred_element_type=F32)


def _rms(x, g):
    return x * lax.rsqrt(jnp.mean(x * x, axis=-1, keepdims=True) + EPS) * g


def _split_bf16(x):
    hi = x.astype(BF16)
    lo = (x - hi.astype(F32)).astype(BF16)
    return hi, lo


def _const_spec(shape):
    nd = len(shape)
    return pl.BlockSpec(shape, lambda *_: (0,) * nd, pipeline_mode=pl.Buffered(1))


def _ada_kernel(c_ref, w_ref, b_ref, o_ref):
    c = c_ref[...]
    c_act = c * jax.nn.sigmoid(c)
    o_ref[...] = jnp.dot(c_act, w_ref[...], preferred_element_type=F32,
                         precision=lax.Precision.HIGHEST) + b_ref[...]


def _ada_call(c_pad, w_ada, b_ada):
    rows, d = c_pad.shape
    n = w_ada.shape[1]
    tn = ADA_COL_TILE
    return pl.pallas_call(
        _ada_kernel,
        grid=(n // tn,),
        in_specs=[pl.BlockSpec((rows, d), lambda j: (0, 0)),
                  pl.BlockSpec((d, tn), lambda j: (0, j)),
                  pl.BlockSpec((1, tn), lambda j: (0, j))],
        out_specs=pl.BlockSpec((rows, tn), lambda j: (0, j)),
        out_shape=jax.ShapeDtypeStruct((rows, n), F32),
        compiler_params=pltpu.CompilerParams(dimension_semantics=("arbitrary",),
                                             vmem_limit_bytes=VMEM_LIMIT),
        name="adaln_mod",
    )(c_pad, w_ada, b_ada)


def _ffn_core(x, shift, scale, gate, g, wg_ref, wu_ref, wd_ref):
    h = _rms(x, g) * (1.0 + scale) + shift
    hb = h.astype(BF16)
    d_ff = wg_ref.shape[1]
    acc = None
    for j in range(d_ff // FF_TILE):
        sl = slice(j * FF_TILE, (j + 1) * FF_TILE)
        gg = _dot(hb, wg_ref[:, sl])
        uu = _dot(hb, wu_ref[:, sl])
        a = (gg * jax.nn.sigmoid(gg) * uu).astype(BF16)
        d = _dot(a, wd_ref[sl, :])
        acc = d if acc is None else acc + d
    return x + (MACARON_W * gate) * acc


def _ffn_kernel(x_ref, mod_ref, g_ref, wg_ref, wu_ref, wd_ref, o_ref):
    o_ref[...] = _ffn_core(x_ref[...], mod_ref[0:1, :], mod_ref[1:2, :], mod_ref[2:3, :],
                           g_ref[...], wg_ref, wu_ref, wd_ref)


def _ffn_call(x, mod, g, wg, wu, wd):
    B, S, D = x.shape
    tm = FFN_TOKEN_TILE
    return pl.pallas_call(
        _ffn_kernel,
        grid=(B, S // tm),
        in_specs=[pl.BlockSpec((None, tm, D), lambda b, i: (b, i, 0)),
                  pl.BlockSpec((None, N_MOD, D), lambda b, i: (b, 0, 0)),
                  _const_spec(g.shape), _const_spec(wg.shape), _const_spec(wu.shape),
                  _const_spec(wd.shape)],
        out_specs=pl.BlockSpec((None, tm, D), lambda b, i: (b, i, 0)),
        out_shape=jax.ShapeDtypeStruct((B, S, D), F32),
        compiler_params=pltpu.CompilerParams(dimension_semantics=("parallel", "parallel"),
                                             vmem_limit_bytes=VMEM_LIMIT),
        name="ffn1",
    )(x, mod, g, wg, wu, wd)


def _rope_table_kernel(pos_ref, freq_ref, cos_ref, sin_ref):
    pos = jnp.concatenate([pos_ref[...].astype(F32),
                           jnp.zeros((pos_ref.shape[0], LANES - TOKENS_PER_ROW), F32)], axis=1)
    src = lax.broadcasted_iota(jnp.int32, pos.shape, 1) // ROT_HALF
    ang = jnp.take_along_axis(pos, src, axis=1) * freq_ref[...]
    cos_ref[...] = jnp.cos(ang)
    sin_ref[...] = jnp.sin(ang)


def _rope_table_call(pos_rows, freq):
    rows = pos_rows.shape[0]
    shape = jax.ShapeDtypeStruct((rows, LANES), F32)
    full = pl.BlockSpec((rows, LANES), lambda: (0, 0))
    return pl.pallas_call(
        _rope_table_kernel,
        in_specs=[pl.BlockSpec(pos_rows.shape, lambda: (0, 0)), pl.BlockSpec(freq.shape, lambda: (0, 0))],
        out_specs=(full, full),
        out_shape=(shape, shape),
        compiler_params=pltpu.CompilerParams(vmem_limit_bytes=VMEM_LIMIT),
        name="rope_table",
    )(pos_rows, freq)


def _inproj_kernel(x_ref, cosp_ref, sinp_ref, mod_ref, gmix_ref, win_ref, wgate_ref, convw_ref, gconv_ref,
                   gmat_ref, onehot_ref, yc_ref, qT_ref, kc_ref, vc_ref, ksa_ref, vsT_ref, kw_ref,
                   vwT_ref, gT_ref, carry_ref):
    tm = x_ref.shape[0]

    @pl.when(pl.program_id(1) == 0)
    def _():
        carry_ref[...] = jnp.zeros_like(carry_ref)

    x = x_ref[...]
    h = _rms(x, gmix_ref[...]) * (1.0 + mod_ref[4:5, :]) + mod_ref[3:4, :]
    hb = h.astype(BF16)

    def proj(c0, width):
        return _dot(hb, win_ref[:, c0:c0 + width])

    d = lax.broadcasted_iota(jnp.int32, (tm, LANES), 1) & (HEAD_DIM - 1)
    token = lax.broadcasted_iota(jnp.int32, (tm, LANES), 0) & (TOKENS_PER_ROW - 1)
    src_lane = token * ROT_HALF + (d & (ROT_HALF - 1))

    def unpack(packed_ref):
        rows = jnp.broadcast_to(packed_ref[...][:, None, :], (tm // TOKENS_PER_ROW, TOKENS_PER_ROW, LANES))
        return jnp.take_along_axis(rows.reshape(tm, LANES), src_lane, axis=1)

    cos_t = jnp.where(d < ROT_DIM, unpack(cosp_ref), 1.0)
    sin_raw = unpack(sinp_ref)
    sin_t = jnp.where(d < ROT_HALF, -sin_raw, jnp.where(d < ROT_DIM, sin_raw, 0.0))
    first_half = d < ROT_HALF

    def rope(t):
        outs = []
        for j in range(t.shape[1] // LANES):
            tj = t[:, j * LANES:(j + 1) * LANES]
            partner = jnp.where(first_half, pltpu.roll(tj, LANES - ROT_HALF, 1),
                                pltpu.roll(tj, ROT_HALF, 1))
            outs.append(tj * cos_t + partner * sin_t)
        return outs[0] if len(outs) == 1 else jnp.concatenate(outs, axis=1)

    c0 = 3 * CONV_CH
    q = rope(proj(c0, ATTN_WIDTH)) * Q_SCALE
    qT_ref[...] = q.T.astype(qT_ref.dtype)
    c0 += ATTN_WIDTH
    kv = proj(c0, 2 * KV_WIDTH)
    kc_ref[...] = rope(kv[:, :KV_WIDTH])
    vc_ref[...] = kv[:, KV_WIDTH:]
    kv = proj(c0 + 2 * KV_WIDTH, 2 * KV_WIDTH)
    ks = rope(kv[:, :KV_WIDTH]).astype(BF16)
    vsT = kv[:, KV_WIDTH:].T.astype(BF16)
    kv = proj(c0 + 4 * KV_WIDTH, 2 * KV_WIDTH)
    kw = rope(kv[:, :KV_WIDTH]).astype(BF16)
    vwT = kv[:, KV_WIDTH:].T.astype(BF16)
    pad = jnp.zeros((tm, MXU_DEPTH - LANES - HEAD_DIM), BF16)
    ones = jnp.ones((ONES_ROWS, tm), BF16)
    for g in range(N_KV_GROUPS):
        kg = ks[:, g * HEAD_DIM:(g + 1) * HEAD_DIM]
        ksa_ref[g] = jnp.concatenate([onehot_ref[...], kg, pad], axis=1)
        kw_ref[g] = kw[:, g * HEAD_DIM:(g + 1) * HEAD_DIM]
        vsT_ref[g] = jnp.concatenate([vsT[g * HEAD_DIM:(g + 1) * HEAD_DIM, :], ones], axis=0)
        vwT_ref[g] = jnp.concatenate([vwT[g * HEAD_DIM:(g + 1) * HEAD_DIM, :], ones], axis=0)
    gT_ref[...] = jax.nn.sigmoid(_dot(hb, wgate_ref[...])).T

    cb = proj(0, CONV_CH)
    u = proj(CONV_CH, CONV_CH) * proj(2 * CONV_CH, CONV_CH)
    row = lax.broadcasted_iota(jnp.int32, (tm, 1), 0)
    prev1 = carry_ref[7:8, :]
    prev2 = carry_ref[6:7, :]
    u1 = jnp.where(row >= 1, pltpu.roll(u, 1, 0), prev1)
    u2 = jnp.where(row >= 2, pltpu.roll(u, 2, 0), jnp.where(row == 1, prev1, prev2))
    carry_ref[...] = u[tm - 8:tm, :]
    v = convw_ref[0:1, :] * u2 + convw_ref[1:2, :] * u1 + convw_ref[2:3, :] * u
    y = cb * v
    hi, lo = _split_bf16(y * y)
    ms = _dot(hi, gmat_ref[...]) + _dot(lo, gmat_ref[...])
    yc_ref[...] = (y * lax.rsqrt(ms + EPS) * gconv_ref[...]).astype(yc_ref.dtype)


def _inproj_call(x, cos_p, sin_p, mod, gmix, win, wgate, convw, gconv, gmat, onehot):
    B, S, D = x.shape
    tm = TOKEN_TILE
    tok = lambda w: pl.BlockSpec((None, tm, w), lambda b, i: (b, i, 0))
    tr = lambda w: pl.BlockSpec((None, w, tm), lambda b, i: (b, 0, i))
    grp = lambda w: pl.BlockSpec((None, N_KV_GROUPS, tm, w), lambda b, i: (b, 0, i, 0))
    grpT = pl.BlockSpec((None, N_KV_GROUPS, VT_ROWS, tm), lambda b, i: (b, 0, 0, i))
    packed = pl.BlockSpec((None, tm // TOKENS_PER_ROW, LANES), lambda b, i: (b, i, 0))
    out_shapes = (
        jax.ShapeDtypeStruct((B, S, CONV_CH), BF16),
        jax.ShapeDtypeStruct((B, ATTN_WIDTH, S), BF16),
        jax.ShapeDtypeStruct((B, S, KV_WIDTH), F32),
        jax.ShapeDtypeStruct((B, S, KV_WIDTH), F32),
        jax.ShapeDtypeStruct((B, N_KV_GROUPS, S, MXU_DEPTH), BF16),
        jax.ShapeDtypeStruct((B, N_KV_GROUPS, VT_ROWS, S), BF16),
        jax.ShapeDtypeStruct((B, N_KV_GROUPS, S, HEAD_DIM), BF16),
        jax.ShapeDtypeStruct((B, N_KV_GROUPS, VT_ROWS, S), BF16),
        jax.ShapeDtypeStruct((B, N_KV_GROUPS * GATE_PAD, S), F32),
    )
    return pl.pallas_call(
        _inproj_kernel,
        grid=(B, S // tm),
        in_specs=[tok(D),
                  packed, packed,
                  pl.BlockSpec((None, N_MOD, D), lambda b, i: (b, 0, 0)),
                  _const_spec(gmix.shape), _const_spec(win.shape), _const_spec(wgate.shape),
                  _const_spec(convw.shape),
                  _const_spec(gconv.shape), _const_spec(gmat.shape),
                  pl.BlockSpec((tm, onehot.shape[1]), lambda b, i: (i, 0))],
        out_specs=(tok(CONV_CH), tr(ATTN_WIDTH), tok(KV_WIDTH), tok(KV_WIDTH), grp(MXU_DEPTH),
                   grpT, grp(HEAD_DIM), grpT, tr(N_KV_GROUPS * GATE_PAD)),
        out_shape=out_shapes,
        scratch_shapes=[pltpu.VMEM((8, CONV_CH), F32)],
        compiler_params=pltpu.CompilerParams(dimension_semantics=("arbitrary", "arbitrary"),
                                             vmem_limit_bytes=VMEM_LIMIT),
        name="mixer_inproj",
    )(x, cos_p, sin_p, mod, gmix, win, wgate, convw, gconv, gmat, onehot)


def _compress_kernel(kf_ref, vf_ref, pek_ref, pev_ref, wk1_ref, wk2_ref, wv1_ref, wv2_ref,
                     kcc_ref, vcT_ref):
    half = CMP_BLOCK // 2
    n = kf_ref.shape[0] // half

    def mlp(x_ref, pe_ref, w1_ref, w2_ref):
        parts = []
        for p in range(2):
            acc = None
            for l0 in range(0, half, 2):
                xs, ws = [], []
                for l in (l0, l0 + 1):
                    row = p * half + l
                    xs.append((x_ref[pl.ds(l, n, stride=half), :]
                               + pe_ref[row:row + 1, :]).astype(BF16))
                    w = w1_ref[row * HEAD_DIM:(row + 1) * HEAD_DIM, :].astype(BF16)
                    z = jnp.zeros_like(w)
                    ws.append(jnp.concatenate([jnp.concatenate([w, z], axis=1),
                                               jnp.concatenate([z, w], axis=1)], axis=0))
                d = _dot(jnp.concatenate(xs, axis=1), jnp.concatenate(ws, axis=0))
                acc = d if acc is None else acc + d
            parts.append(acc)
        hpre = parts[0] + pltpu.roll(parts[1], n - 1, 0)
        hid = (hpre * jax.nn.sigmoid(hpre)).astype(BF16)
        w2 = w2_ref[...].astype(BF16)
        return jnp.concatenate([_dot(hid[:, g * CMP_HIDDEN:(g + 1) * CMP_HIDDEN], w2)
                                for g in range(N_KV_GROUPS)], axis=1)

    kc = mlp(kf_ref, pek_ref, wk1_ref, wk2_ref).astype(kcc_ref.dtype)
    for g in range(N_KV_GROUPS):
        kcc_ref[g] = kc[:, g * HEAD_DIM:(g + 1) * HEAD_DIM]
    vcT = mlp(vf_ref, pev_ref, wv1_ref, wv2_ref).T.astype(vcT_ref.dtype)
    ones = jnp.ones((ONES_ROWS, n), vcT_ref.dtype)
    for g in range(N_KV_GROUPS):
        vcT_ref[g] = jnp.concatenate([vcT[g * HEAD_DIM:(g + 1) * HEAD_DIM, :], ones], axis=0)


def _compress_call(kf, vf, pek, pev, wk1, wk2, wv1, wv2):
    B, S, width = kf.shape
    n = S // (CMP_BLOCK // 2)
    flat = pl.BlockSpec((None, S, width), lambda b: (b, 0, 0))
    return pl.pallas_call(
        _compress_kernel,
        grid=(B,),
        in_specs=[flat, flat, _const_spec(pek.shape), _const_spec(pev.shape),
                  _const_spec(wk1.shape), _const_spec(wk2.shape), _const_spec(wv1.shape),
                  _const_spec(wv2.shape)],
        out_specs=(pl.BlockSpec((None, N_KV_GROUPS, n, HEAD_DIM), lambda b: (b, 0, 0, 0)),
                   pl.BlockSpec((None, N_KV_GROUPS, VT_ROWS, n), lambda b: (b, 0, 0, 0))),
        out_shape=(jax.ShapeDtypeStruct((B, N_KV_GROUPS, n, HEAD_DIM), BF16),
                   jax.ShapeDtypeStruct((B, N_KV_GROUPS, VT_ROWS, n), BF16)),
        compiler_params=pltpu.CompilerParams(dimension_semantics=("arbitrary",),
                                             vmem_limit_bytes=VMEM_LIMIT),
        name="kv_compress",
    )(kf, vf, pek, pev, wk1, wk2, wv1, wv2)


def _lane_tiles(x, n):
    return jnp.concatenate([x] * n, axis=1)


def _col_max(s_ref, n_rows, bias_fn):
    groups = ROW_TILE // 8
    mx = [jnp.full((8, s_ref.shape[1]), NEG, F32)] * groups
    for r in range(0, n_rows, ROW_TILE):
        x = s_ref[r:r + ROW_TILE, :]
        if bias_fn is not None:
            x = x + bias_fn(r)
            s_ref[r:r + ROW_TILE, :] = x
        mx = [jnp.maximum(mx[i], x[8 * i:8 * (i + 1), :]) for i in range(groups)]
    while len(mx) > 1:
        mx = [jnp.maximum(a, b) for a, b in zip(mx[0::2], mx[1::2])]
    return jnp.max(mx[0], axis=0, keepdims=True)


def _col_exp2(s_ref, p_ref, n_rows, m, keep_f32=False):
    for r in range(0, n_rows, ROW_TILE):
        p = jnp.exp2(s_ref[r:r + ROW_TILE, :] - m)
        if keep_f32:
            s_ref[r:r + ROW_TILE, :] = p
        p_ref[r:r + ROW_TILE, :] = p.astype(p_ref.dtype)


def _recip_pos(l):
    return 1.0 / jnp.where(l > 0.0, l, 1.0)


def _select_kernel(top_n, qT_ref, kcc_ref, vcT_ref, ovlT_ref, sel_ref, ocmp_ref, sc_ref, pc_ref,
                   ph_ref, pl_ref):
    QB = SEL_BLOCK
    n_cmp = kcc_ref.shape[0]
    n_slc = ovlT_ref.shape[0]
    t0 = pl.program_id(2) * QB
    tq = t0 + lax.broadcasted_iota(jnp.int32, (1, QB), 1)
    rows = lax.broadcasted_iota(jnp.int32, (ROW_TILE, 1), 0)

    qT = qT_ref[...]
    qcat = jnp.concatenate([qT[h * HEAD_DIM:(h + 1) * HEAD_DIM, :] for h in range(HPG)], axis=1)

    def cmp_bias(r):
        cmp_end = (r + rows) * CMP_STRIDE + (CMP_BLOCK - 1)
        return _lane_tiles(jnp.where(cmp_end <= tq, 0.0, NEG), HPG)

    cur = tq // SLC_BLOCK

    def tree(op, xs):
        while len(xs) > 1:
            xs = [op(*xs[i:i + 2]) if i + 1 < len(xs) else xs[i] for i in range(0, len(xs), 2)]
        return xs[0]

    def causal_variant(n_rows):
        rows_cmp = min(n_cmp, n_rows * (SLC_BLOCK // CMP_STRIDE))

        def run():
            sc_ref[0:rows_cmp, :] = _dot(kcc_ref[0:rows_cmp, :], qcat)
            m = jnp.maximum(_col_max(sc_ref, rows_cmp, cmp_bias), M_FLOOR)
            _col_exp2(sc_ref, pc_ref, rows_cmp, m, keep_f32=True)
            o_cmp = _dot(vcT_ref[:, 0:rows_cmp], pc_ref[0:rows_cmp, :])
            rl = _recip_pos(o_cmp[HEAD_DIM:HEAD_DIM + 1, :])
            o_cmp = o_cmp[0:HEAD_DIM, :] * rl
            for h in range(HPG):
                ocmp_ref[h] = o_cmp[:, h * QB:(h + 1) * QB]

            for r in range(0, rows_cmp, ROW_TILE):
                pn = sc_ref[r:r + ROW_TILE, :] * rl
                psum = pn[:, 0:QB]
                for h in range(1, HPG):
                    psum = psum + pn[:, h * QB:(h + 1) * QB]
                hi, lo = _split_bf16(psum)
                ph_ref[r:r + ROW_TILE, :] = hi
                pl_ref[r:r + ROW_TILE, :] = lo
            ovl = ovlT_ref[0:n_rows, 0:rows_cmp]
            imp = _dot(ovl, ph_ref[0:rows_cmp, :]) + _dot(ovl, pl_ref[0:rows_cmp, :])

            blk = lax.broadcasted_iota(jnp.int32, (n_rows, 1), 0)
            future = blk > cur
            forced = (blk == 0) | (blk == cur) | (blk == cur - 1)
            score0 = jnp.where(future, -BIG, jnp.where(forced, BIG, imp))
            score = [score0[8 * i:8 * (i + 1), :] for i in range(n_rows // 8)]
            cum = jnp.zeros((1, QB), F32)
            thr = jnp.zeros((1, QB), F32)
            above = jnp.zeros((1, QB), F32)
            for _ in range(top_n):
                best = jnp.max(tree(jnp.maximum, score), axis=0, keepdims=True)
                eq = [s == best for s in score]
                unfilled = cum < top_n
                thr = jnp.where(unfilled, best, thr)
                above = jnp.where(unfilled, cum, above)
                cum = cum + jnp.sum(tree(jnp.add, [jnp.where(e, 1.0, 0.0) for e in eq]),
                                    axis=0, keepdims=True)
                score = [jnp.where(e, -jnp.inf, s) for e, s in zip(eq, score)]
            ties = score0 == thr
            lower = jnp.where(lax.broadcasted_iota(jnp.int32, (1, n_rows), 1) < blk, 1.0, 0.0)
            rank = _dot(lower.astype(BF16), jnp.where(ties, 1.0, 0.0).astype(BF16))
            picked = (score0 > thr) | (ties & (rank < top_n - above))
            sel_ref[0:n_rows, :] = jnp.where(future, NEG, jnp.where(picked, 0.0, NEG)).astype(sel_ref.dtype)
            if n_rows < n_slc:
                sel_ref[n_rows:n_slc, :] = jnp.full((n_slc - n_rows, QB), NEG, sel_ref.dtype)
        return run

    steps = n_slc // SEL_ROW_STEP
    need = jnp.minimum((t0 + QB - 1) // (SLC_BLOCK * SEL_ROW_STEP), steps - 1)
    lax.switch(need, [causal_variant(SEL_ROW_STEP * (k + 1)) for k in range(steps)])


def _select_call(top_n, qT, kcc, vcT, ovlT):
    B, _, S = qT.shape
    n_cmp = kcc.shape[2]
    n_slc = ovlT.shape[0]
    gw = HPG * HEAD_DIM
    qw = HPG * SEL_BLOCK
    return pl.pallas_call(
        functools.partial(_select_kernel, top_n),
        grid=(B, N_KV_GROUPS, S // SEL_BLOCK),
        in_specs=[pl.BlockSpec((None, gw, SEL_BLOCK), lambda b, g, i: (b, g, i)),
                  pl.BlockSpec((None, None, n_cmp, HEAD_DIM), lambda b, g, i: (b, g, 0, 0)),
                  pl.BlockSpec((None, None, VT_ROWS, n_cmp), lambda b, g, i: (b, g, 0, 0)),
                  pl.BlockSpec(ovlT.shape, lambda b, g, i: (0, 0))],
        out_specs=(pl.BlockSpec((None, None, n_slc, SEL_BLOCK), lambda b, g, i: (b, g, 0, i)),
                   pl.BlockSpec((None, None, HPG, HEAD_DIM, SEL_BLOCK),
                                lambda b, g, i: (b, g, 0, 0, i))),
        out_shape=(jax.ShapeDtypeStruct((B, N_KV_GROUPS, n_slc, S), BF16),
                   jax.ShapeDtypeStruct((B, N_KV_GROUPS, HPG, HEAD_DIM, S), F32)),
        scratch_shapes=[pltpu.VMEM((n_cmp, qw), F32), pltpu.VMEM((n_cmp, qw), BF16),
                        pltpu.VMEM((n_cmp, SEL_BLOCK), BF16), pltpu.VMEM((n_cmp, SEL_BLOCK), BF16)],
        compiler_params=pltpu.CompilerParams(
            dimension_semantics=("parallel", "parallel", "arbitrary"),
            vmem_limit_bytes=VMEM_LIMIT),
        name="nsa_select",
    )(qT, kcc, vcT, ovlT)


def _attn_kernel(qT_ref, gT_ref, sel_ref, ocmp_ref, ksa_ref, vsT_ref, kw_ref, vwT_ref, ga_ref,
                 o_ref, s0_ref, s1_ref, p0_ref, p1_ref, sw_ref, pw_ref, rhs_ref, acc_ref):
    QB = Q_BLOCK
    n_slc = sel_ref.shape[0]
    t0 = pl.program_id(2) * QB
    tq = t0 + lax.broadcasted_iota(jnp.int32, (1, QB), 1)
    rows = lax.broadcasted_iota(jnp.int32, (ROW_TILE, 1), 0)

    qT = qT_ref[...]
    qcat = jnp.concatenate([qT[h * HEAD_DIM:(h + 1) * HEAD_DIM, :] for h in range(HPG)], axis=1)

    rhs_ref[...] = jnp.concatenate([_lane_tiles(sel_ref[...], HPG), qcat,
                                    jnp.zeros((MXU_DEPTH - n_slc - HEAD_DIM, QW), BF16)], axis=0)

    w0 = pl.multiple_of(jnp.maximum(t0 - WINDOW, 0), QB)
    sw_ref[...] = _dot(kw_ref[pl.ds(w0, WIN_KEYS), :], qcat)

    def win_bias(r):
        dist = tq - (w0 + r + rows)
        return _lane_tiles(jnp.where((dist >= 0) & (dist < WINDOW), 0.0, NEG), HPG)

    KC = KEY_CHUNK
    last = t0 // KC

    def scores(c, dst_ref):
        k0 = pl.multiple_of(jnp.minimum(c, last) * KC, KC)
        dst_ref[...] = _dot(ksa_ref[pl.ds(k0, KC), :], rhs_ref[...])

    def weighted_values(c, p_ref):
        k0 = pl.multiple_of(jnp.clip(c, 0, last) * KC, KC)
        return _dot(vsT_ref[:, pl.ds(k0, KC)], p_ref[...])

    def softmax(c, src_ref, p_ref, m, pending, causal):
        def causal_bias(r):
            return _lane_tiles(jnp.where(c * KC + r + rows <= tq, 0.0, NEG), HPG)

        m_new = jnp.maximum(m, _col_max(src_ref, KC, causal_bias if causal else None))
        acc_ref[...] = jnp.exp2(m - m_new) * (acc_ref[...] + pending)
        _col_exp2(src_ref, p_ref, KC, m_new)
        return m_new

    def pair(i, m):
        pending = weighted_values(2 * i - 1, p1_ref)
        scores(2 * i + 1, s1_ref)
        m = softmax(2 * i, s0_ref, p0_ref, m, pending, False)
        pending = weighted_values(2 * i, p0_ref)
        scores(2 * i + 2, s0_ref)
        return softmax(2 * i + 1, s1_ref, p1_ref, m, pending, True)

    p1_ref[...] = jnp.zeros_like(p1_ref)
    acc_ref[...] = jnp.zeros_like(acc_ref)
    scores(0, s0_ref)

    m = _col_max(sw_ref, WIN_KEYS, win_bias)
    _col_exp2(sw_ref, pw_ref, WIN_KEYS, m)
    o_win = _dot(vwT_ref[:, pl.ds(w0, WIN_KEYS)], pw_ref[...])
    o_win = o_win[0:HEAD_DIM, :] * (1.0 / o_win[HEAD_DIM:HEAD_DIM + 1, :])

    n_pairs = (last + 1) // 2
    m = lax.fori_loop(0, n_pairs // 2, lambda j, mm: pair(2 * j + 1, pair(2 * j, mm)),
                      jnp.full((1, QW), NEG, F32))
    m = lax.cond(n_pairs % 2 == 1, lambda mm: pair(n_pairs - 1, mm), lambda mm: mm, m)
    pending = weighted_values(2 * n_pairs - 1, p1_ref)

    @pl.when(last % 2 == 0)
    def _():
        softmax(last, s0_ref, p0_ref, m, pending, True)
        acc_ref[...] += weighted_values(last, p0_ref)

    @pl.when(last % 2 == 1)
    def _():
        acc_ref[...] += pending

    acc = acc_ref[...]
    o_slc = acc[0:HEAD_DIM, :] * (1.0 / acc[HEAD_DIM:HEAD_DIM + 1, :])

    gts = gT_ref[...]
    outs = []
    for h in range(HPG):
        sl = slice(h * QB, (h + 1) * QB)
        o = (gts[3 * h:3 * h + 1, :] * ocmp_ref[h] + gts[3 * h + 1:3 * h + 2, :] * o_slc[:, sl]
             + gts[3 * h + 2:3 * h + 3, :] * o_win[:, sl])
        o = o * lax.rsqrt(jnp.mean(o * o, axis=0, keepdims=True) + EPS)
        outs.append((o * ga_ref[h * HEAD_DIM:(h + 1) * HEAD_DIM, :]).T)
    o_ref[...] = jnp.concatenate(outs, axis=1).astype(o_ref.dtype)


def _attn_call(qT, gatesT, sel_bias, o_cmp, ksa, vsT, kw, vwT, g_attn_col):
    B, _, S = qT.shape
    n_slc = sel_bias.shape[2]
    gw = HPG * HEAD_DIM
    rows = lambda n, w: pl.BlockSpec((None, None, n, w), lambda b, g, i: (b, g, 0, 0))
    cols = lambda n: pl.BlockSpec((None, None, VT_ROWS, n), lambda b, g, i: (b, g, 0, 0))
    return pl.pallas_call(
        _attn_kernel,
        grid=(B, N_KV_GROUPS, S // Q_BLOCK),
        in_specs=[pl.BlockSpec((None, gw, Q_BLOCK), lambda b, g, i: (b, g, i)),
                  pl.BlockSpec((None, GATE_PAD, Q_BLOCK), lambda b, g, i: (b, g, i)),
                  pl.BlockSpec((None, None, n_slc, Q_BLOCK), lambda b, g, i: (b, g, 0, i)),
                  pl.BlockSpec((None, None, HPG, HEAD_DIM, Q_BLOCK), lambda b, g, i: (b, g, 0, 0, i)),
                  rows(S, MXU_DEPTH), cols(S), rows(S, HEAD_DIM), cols(S),
                  pl.BlockSpec((gw, 1), lambda b, g, i: (g, 0))],
        out_specs=pl.BlockSpec((None, Q_BLOCK, gw), lambda b, g, i: (b, i, g)),
        out_shape=jax.ShapeDtypeStruct((B, S, ATTN_WIDTH), BF16),
        scratch_shapes=[pltpu.VMEM((KEY_CHUNK, QW), F32), pltpu.VMEM((KEY_CHUNK, QW), F32),
                        pltpu.VMEM((KEY_CHUNK, QW), BF16), pltpu.VMEM((KEY_CHUNK, QW), BF16),
                        pltpu.VMEM((WIN_KEYS, QW), F32), pltpu.VMEM((WIN_KEYS, QW), BF16),
                        pltpu.VMEM((MXU_DEPTH, QW), BF16), pltpu.VMEM((VT_ROWS, QW), F32)],
        compiler_params=pltpu.CompilerParams(
            dimension_semantics=("parallel", "parallel", "arbitrary"),
            vmem_limit_bytes=VMEM_LIMIT),
        name="nsa_attention",
    )(qT, gatesT, sel_bias, o_cmp, ksa, vsT, kw, vwT, g_attn_col)


def _out_kernel(x_ref, yc_ref, ya_ref, mod_ref, wo_ref, g_ref, wg_ref, wu_ref, wd_ref, gf_ref,
                o_ref):
    mix = _dot(yc_ref[...], wo_ref[0:CONV_CH, :]) + _dot(ya_ref[...], wo_ref[CONV_CH:, :])
    x = x_ref[...] + mod_ref[5:6, :] * mix
    x = _ffn_core(x, mod_ref[6:7, :], mod_ref[7:8, :], mod_ref[8:9, :], g_ref[...], wg_ref,
                  wu_ref, wd_ref)
    o_ref[...] = _rms(x, gf_ref[...])


def _out_call(x, yc, ya, mod, wo, g, wg, wu, wd, gf):
    B, S, D = x.shape
    tm = FFN_TOKEN_TILE
    tok = lambda w: pl.BlockSpec((None, tm, w), lambda b, i: (b, i, 0))
    return pl.pallas_call(
        _out_kernel,
        grid=(B, S // tm),
        in_specs=[tok(D), tok(CONV_CH), tok(ATTN_WIDTH),
                  pl.BlockSpec((None, N_MOD, D), lambda b, i: (b, 0, 0)),
                  _const_spec(wo.shape), _const_spec(g.shape), _const_spec(wg.shape),
                  _const_spec(wu.shape), _const_spec(wd.shape), _const_spec(gf.shape)],
        out_specs=tok(D),
        out_shape=jax.ShapeDtypeStruct((B, S, D), F32),
        compiler_params=pltpu.CompilerParams(dimension_semantics=("parallel", "parallel"),
                                             vmem_limit_bytes=VMEM_LIMIT),
        name="outproj_ffn2",
    )(x, yc, ya, mod, wo, g, wg, wu, wd, gf)


def kernel(x, c, positions, w_ada, b_ada, g_ffn1, w1_gate, w1_up, w1_down, g_mix, w_in, conv_w, cmp_pos_k, cmp_pos_v, w_cmpk1, w_cmpk2, w_cmpv1, w_cmpv2, g_out_conv, g_out_attn, w_out, g_ffn2, w2_gate, w2_up, w2_down, g_final):
    B, S, D = x.shape
    depth = w_ada.shape[0]
    n_slc = S // SLC_BLOCK
    half = CMP_BLOCK // 2
    n_half = S // half
    assert n_slc <= LANES, "selection-block one-hot is one lane tile wide"

    c_pad = jnp.pad(c, ((0, 8 - B), (0, 0)))
    row = lambda a: a.reshape(1, -1)

    freq_half = jnp.power(ROPE_THETA, -2.0 * jnp.arange(ROT_HALF, dtype=F32) / ROT_DIM)
    freq = jnp.tile(freq_half, LANES // ROT_HALF).reshape(1, LANES)
    gidx = np.arange(CONV_CH) // (CONV_CH // CONV_GROUPS)
    gmat = jnp.asarray((gidx[:, None] == gidx[None, :]) / (CONV_CH // CONV_GROUPS), dtype=BF16)
    c0 = np.arange(n_half) * CMP_STRIDE
    s0 = np.arange(LANES) * SLC_BLOCK
    ovlT = ((c0[None, :] <= s0[:, None] + SLC_BLOCK - 1) & (c0[None, :] + CMP_BLOCK - 1 >= s0[:, None]))
    ovlT = jnp.asarray(ovlT, dtype=BF16)
    onehot = jnp.asarray((np.arange(S) // SLC_BLOCK)[:, None] == np.arange(LANES)[None, :], dtype=BF16)
    cos_p, sin_p = _rope_table_call(positions.reshape(-1, TOKENS_PER_ROW), freq)
    cos_p = cos_p.reshape(B, S // TOKENS_PER_ROW, LANES)
    sin_p = sin_p.reshape(B, S // TOKENS_PER_ROW, LANES)

    for l in range(depth):
        mod = _ada_call(c_pad, w_ada[l], row(b_ada[l]))[:B].reshape(B, N_MOD, D)

        x = _ffn_call(x, mod, row(g_ffn1[l]), w1_gate[l].astype(BF16), w1_up[l].astype(BF16),
                      w1_down[l].astype(BF16))

        n_main = w_in.shape[2] - N_KV_GROUPS * 3 * HPG
        gate_cols = [jnp.pad(w_in[l][:, n_main + g * 3 * HPG:n_main + (g + 1) * 3 * HPG],
                             ((0, 0), (0, GATE_PAD - 3 * HPG))) for g in range(N_KV_GROUPS)]
        wgate = jnp.concatenate(gate_cols, axis=1).astype(BF16)
        win = w_in[l][:, :n_main].astype(BF16)
        (yc, qT, kc, vc, ksa, vsT, kw, vwT, gatesT) = _inproj_call(
            x, cos_p, sin_p, mod, row(g_mix[l]), win, wgate, conv_w[l], row(g_out_conv[l]), gmat,
            onehot)

        both_groups = lambda pe: jnp.tile(pe, (1, N_KV_GROUPS))
        kcc, vcT = _compress_call(kc, vc, both_groups(cmp_pos_k[l]), both_groups(cmp_pos_v[l]),
                                  w_cmpk1[l], w_cmpk2[l], w_cmpv1[l], w_cmpv2[l])

        sel_bias, o_cmp = _select_call(min(SLC_TOP_N, n_slc), qT, kcc, vcT, ovlT)
        ya = _attn_call(qT, gatesT, sel_bias, o_cmp, ksa, vsT, kw, vwT,
                        g_out_attn[l].reshape(ATTN_WIDTH, 1))

        assert l == depth - 1, "final norm is fused into the last layer's output kernel"
        x = _out_call(x, yc, ya, mod, w_out[l].astype(BF16), row(g_ffn2[l]),
                      w2_gate[l].astype(BF16), w2_up[l].astype(BF16), w2_down[l].astype(BF16),
                      row(g_final))
    return x
```

```python
import functools
import math

import numpy as np
import jax
import jax.numpy as jnp
from jax import lax
from jax.experimental import pallas as pl
from jax.experimental.pallas import tpu as pltpu

F32 = jnp.float32
BF16 = jnp.bfloat16

CONV_CH = 512
CONV_GROUPS = 8
N_HEADS = 8
N_KV_GROUPS = 2
HPG = N_HEADS // N_KV_GROUPS
HEAD_DIM = 64
ATTN_WIDTH = N_HEADS * HEAD_DIM
KV_WIDTH = N_KV_GROUPS * HEAD_DIM
ROPE_THETA = 500000.0
ROT_DIM = HEAD_DIM // 4
ROT_HALF = ROT_DIM // 2
CMP_BLOCK = 32
CMP_STRIDE = 16
CMP_HIDDEN = 256
SLC_BLOCK = 64
SLC_TOP_N = 16
WINDOW = 512
Q_BLOCK = 256
MACARON_W = 0.5
N_MOD = 9
EPS = 1e-6
NEG = -1e30
BIG = 1e9

LANES = 128
MXU_DEPTH = 256
VMEM_LIMIT = 56 * 1024 * 1024

TOKEN_TILE = 512
FFN_TOKEN_TILE = 512
ADA_COL_TILE = 1024
FF_TILE = 256
KEY_CHUNK = 512
WIN_KEYS = WINDOW + Q_BLOCK
TOKENS_PER_ROW = LANES // ROT_HALF
GATE_PAD = LANES
QW = HPG * Q_BLOCK
SEL_BLOCK = 512
SEL_ROW_STEP = 32
ONES_ROWS = 16
VT_ROWS = HEAD_DIM + ONES_ROWS
ROW_TILE = 16
Q_SCALE = HEAD_DIM ** -0.5 * math.log2(math.e)
M_FLOOR = -1e20


def _dot(a, b):
    return jnp.dot(a, b, preferred_element_type=F32)


def _rms(x, g):
    return x * lax.rsqrt(jnp.mean(x * x, axis=-1, keepdims=True) + EPS) * g


def _split_bf16(x):
    hi = x.astype(BF16)
    lo = (x - hi.astype(F32)).astype(BF16)
    return hi, lo


def _const_spec(shape):
    nd = len(shape)
    return pl.BlockSpec(shape, lambda *_: (0,) * nd, pipeline_mode=pl.Buffered(1))


def _ada_kernel(c_ref, w_ref, b_ref, o_ref):
    c = c_ref[...]
    c_act = c * jax.nn.sigmoid(c)
    o_ref[...] = jnp.dot(c_act, w_ref[...], preferred_element_type=F32,
                         precision=lax.Precision.HIGHEST) + b_ref[...]


def _ada_call(c_pad, w_ada, b_ada):
    rows, d = c_pad.shape
    n = w_ada.shape[1]
    tn = ADA_COL_TILE
    return pl.pallas_call(
        _ada_kernel,
        grid=(n // tn,),
        in_specs=[pl.BlockSpec((rows, d), lambda j: (0, 0)),
                  pl.BlockSpec((d, tn), lambda j: (0, j)),
                  pl.BlockSpec((1, tn), lambda j: (0, j))],
        out_specs=pl.BlockSpec((rows, tn), lambda j: (0, j)),
        out_shape=jax.ShapeDtypeStruct((rows, n), F32),
        compiler_params=pltpu.CompilerParams(dimension_semantics=("arbitrary",),
                                             vmem_limit_bytes=VMEM_LIMIT),
        name="adaln_mod",
    )(c_pad, w_ada, b_ada)


def _ffn_core(x, shift, scale, gate, g, wg_ref, wu_ref, wd_ref):
    h = _rms(x, g) * (1.0 + scale) + shift
    hb = h.astype(BF16)
    d_ff = wg_ref.shape[1]
    acc = None
    for j in range(d_ff // FF_TILE):
        sl = slice(j * FF_TILE, (j + 1) * FF_TILE)
        gg = _dot(hb, wg_ref[:, sl])
        uu = _dot(hb, wu_ref[:, sl])
        a = (gg * jax.nn.sigmoid(gg) * uu).astype(BF16)
        d = _dot(a, wd_ref[sl, :])
        acc = d if acc is None else acc + d
    return x + (MACARON_W * gate) * acc


def _ffn_kernel(x_ref, mod_ref, g_ref, wg_ref, wu_ref, wd_ref, o_ref):
    o_ref[...] = _ffn_core(x_ref[...], mod_ref[0:1, :], mod_ref[1:2, :], mod_ref[2:3, :],
                           g_ref[...], wg_ref, wu_ref, wd_ref)


def _ffn_call(x, mod, g, wg, wu, wd):
    B, S, D = x.shape
    tm = FFN_TOKEN_TILE
    return pl.pallas_call(
        _ffn_kernel,
        grid=(B, S // tm),
        in_specs=[pl.BlockSpec((None, tm, D), lambda b, i: (b, i, 0)),
                  pl.BlockSpec((None, N_MOD, D), lambda b, i: (b, 0, 0)),
                  _const_spec(g.shape), _const_spec(wg.shape), _const_spec(wu.shape),
                  _const_spec(wd.shape)],
        out_specs=pl.BlockSpec((None, tm, D), lambda b, i: (b, i, 0)),
        out_shape=jax.ShapeDtypeStruct((B, S, D), F32),
        compiler_params=pltpu.CompilerParams(dimension_semantics=("parallel", "parallel"),
                                             vmem_limit_bytes=VMEM_LIMIT),
        name="ffn1",
    )(x, mod, g, wg, wu, wd)


def _rope_table_kernel(pos_ref, freq_ref, cos_ref, sin_ref):
    pos = jnp.concatenate([pos_ref[...].astype(F32),
                           jnp.zeros((pos_ref.shape[0], LANES - TOKENS_PER_ROW), F32)], axis=1)
    src = lax.broadcasted_iota(jnp.int32, pos.shape, 1) // ROT_HALF
    ang = jnp.take_along_axis(pos, src, axis=1) * freq_ref[...]
    cos_ref[...] = jnp.cos(ang)
    sin_ref[...] = jnp.sin(ang)


def _rope_table_call(pos_rows, freq):
    rows = pos_rows.shape[0]
    shape = jax.ShapeDtypeStruct((rows, LANES), F32)
    full = pl.BlockSpec((rows, LANES), lambda: (0, 0))
    return pl.pallas_call(
        _rope_table_kernel,
        in_specs=[pl.BlockSpec(pos_rows.shape, lambda: (0, 0)), pl.BlockSpec(freq.shape, lambda: (0, 0))],
        out_specs=(full, full),
        out_shape=(shape, shape),
        compiler_params=pltpu.CompilerParams(vmem_limit_bytes=VMEM_LIMIT),
        name="rope_table",
    )(pos_rows, freq)


def _inproj_kernel(x_ref, cosp_ref, sinp_ref, mod_ref, gmix_ref, win_ref, wgate_ref, convw_ref, gconv_ref,
                   gmat_ref, onehot_ref, yc_ref, qT_ref, kc_ref, vc_ref, ksa_ref, vsT_ref, kw_ref,
                   vwT_ref, gT_ref, carry_ref):
    tm = x_ref.shape[0]

    @pl.when(pl.program_id(1) == 0)
    def _():
        carry_ref[...] = jnp.zeros_like(carry_ref)

    x = x_ref[...]
    h = _rms(x, gmix_ref[...]) * (1.0 + mod_ref[4:5, :]) + mod_ref[3:4, :]
    hb = h.astype(BF16)

    def proj(c0, width):
        return _dot(hb, win_ref[:, c0:c0 + width])

    d = lax.broadcasted_iota(jnp.int32, (tm, LANES), 1) & (HEAD_DIM - 1)
    token = lax.broadcasted_iota(jnp.int32, (tm, LANES), 0) & (TOKENS_PER_ROW - 1)
    src_lane = token * ROT_HALF + (d & (ROT_HALF - 1))

    def unpack(packed_ref):
        rows = jnp.broadcast_to(packed_ref[...][:, None, :], (tm // TOKENS_PER_ROW, TOKENS_PER_ROW, LANES))
        return jnp.take_along_axis(rows.reshape(tm, LANES), src_lane, axis=1)

    cos_t = jnp.where(d < ROT_DIM, unpack(cosp_ref), 1.0)
    sin_raw = unpack(sinp_ref)
    sin_t = jnp.where(d < ROT_HALF, -sin_raw, jnp.where(d < ROT_DIM, sin_raw, 0.0))
    first_half = d < ROT_HALF

    def rope(t):
        outs = []
        for j in range(t.shape[1] // LANES):
            tj = t[:, j * LANES:(j + 1) * LANES]
            partner = jnp.where(first_half, pltpu.roll(tj, LANES - ROT_HALF, 1),
                                pltpu.roll(tj, ROT_HALF, 1))
            outs.append(tj * cos_t + partner * sin_t)
        return outs[0] if len(outs) == 1 else jnp.concatenate(outs, axis=1)

    c0 = 3 * CONV_CH
    q = rope(proj(c0, ATTN_WIDTH)) * Q_SCALE
    qT_ref[...] = q.T.astype(qT_ref.dtype)
    c0 += ATTN_WIDTH
    kv = proj(c0, 2 * KV_WIDTH)
    kc_ref[...] = rope(kv[:, :KV_WIDTH])
    vc_ref[...] = kv[:, KV_WIDTH:]
    kv = proj(c0 + 2 * KV_WIDTH, 2 * KV_WIDTH)
    ks = rope(kv[:, :KV_WIDTH]).astype(BF16)
    vsT = kv[:, KV_WIDTH:].T.astype(BF16)
    kv = proj(c0 + 4 * KV_WIDTH, 2 * KV_WIDTH)
    kw = rope(kv[:, :KV_WIDTH]).astype(BF16)
    vwT = kv[:, KV_WIDTH:].T.astype(BF16)
    pad = jnp.zeros((tm, MXU_DEPTH - LANES - HEAD_DIM), BF16)
    ones = jnp.ones((ONES_ROWS, tm), BF16)
    for g in range(N_KV_GROUPS):
        kg = ks[:, g * HEAD_DIM:(g + 1) * HEAD_DIM]
        ksa_ref[g] = jnp.concatenate([onehot_ref[...], kg, pad], axis=1)
        kw_ref[g] = kw[:, g * HEAD_DIM:(g + 1) * HEAD_DIM]
        vsT_ref[g] = jnp.concatenate([vsT[g * HEAD_DIM:(g + 1) * HEAD_DIM, :], ones], axis=0)
        vwT_ref[g] = jnp.concatenate([vwT[g * HEAD_DIM:(g + 1) * HEAD_DIM, :], ones], axis=0)
    gT_ref[...] = jax.nn.sigmoid(_dot(hb, wgate_ref[...])).T

    cb = proj(0, CONV_CH)
    u = proj(CONV_CH, CONV_CH) * proj(2 * CONV_CH, CONV_CH)
    row = lax.broadcasted_iota(jnp.int32, (tm, 1), 0)
    prev1 = carry_ref[7:8, :]
    prev2 = carry_ref[6:7, :]
    u1 = jnp.where(row >= 1, pltpu.roll(u, 1, 0), prev1)
    u2 = jnp.where(row >= 2, pltpu.roll(u, 2, 0), jnp.where(row == 1, prev1, prev2))
    carry_ref[...] = u[tm - 8:tm, :]
    v = convw_ref[0:1, :] * u2 + convw_ref[1:2, :] * u1 + convw_ref[2:3, :] * u
    y = cb * v
    hi, lo = _split_bf16(y * y)
    ms = _dot(hi, gmat_ref[...]) + _dot(lo, gmat_ref[...])
    yc_ref[...] = (y * lax.rsqrt(ms + EPS) * gconv_ref[...]).astype(yc_ref.dtype)


def _inproj_call(x, cos_p, sin_p, mod, gmix, win, wgate, convw, gconv, gmat, onehot):
    B, S, D = x.shape
    tm = TOKEN_TILE
    tok = lambda w: pl.BlockSpec((None, tm, w), lambda b, i: (b, i, 0))
    tr = lambda w: pl.BlockSpec((None, w, tm), lambda b, i: (b, 0, i))
    grp = lambda w: pl.BlockSpec((None, N_KV_GROUPS, tm, w), lambda b, i: (b, 0, i, 0))
    grpT = pl.BlockSpec((None, N_KV_GROUPS, VT_ROWS, tm), lambda b, i: (b, 0, 0, i))
    packed = pl.BlockSpec((None, tm // TOKENS_PER_ROW, LANES), lambda b, i: (b, i, 0))
    out_shapes = (
        jax.ShapeDtypeStruct((B, S, CONV_CH), BF16),
        jax.ShapeDtypeStruct((B, ATTN_WIDTH, S), BF16),
        jax.ShapeDtypeStruct((B, S, KV_WIDTH), F32),
        jax.ShapeDtypeStruct((B, S, KV_WIDTH), F32),
        jax.ShapeDtypeStruct((B, N_KV_GROUPS, S, MXU_DEPTH), BF16),
        jax.ShapeDtypeStruct((B, N_KV_GROUPS, VT_ROWS, S), BF16),
        jax.ShapeDtypeStruct((B, N_KV_GROUPS, S, HEAD_DIM), BF16),
        jax.ShapeDtypeStruct((B, N_KV_GROUPS, VT_ROWS, S), BF16),
        jax.ShapeDtypeStruct((B, N_KV_GROUPS * GATE_PAD, S), F32),
    )
    return pl.pallas_call(
        _inproj_kernel,
        grid=(B, S // tm),
        in_specs=[tok(D),
                  packed, packed,
                  pl.BlockSpec((None, N_MOD, D), lambda b, i: (b, 0, 0)),
                  _const_spec(gmix.shape), _const_spec(win.shape), _const_spec(wgate.shape),
                  _const_spec(convw.shape),
                  _const_spec(gconv.shape), _const_spec(gmat.shape),
                  pl.BlockSpec((tm, onehot.shape[1]), lambda b, i: (i, 0))],
        out_specs=(tok(CONV_CH), tr(ATTN_WIDTH), tok(KV_WIDTH), tok(KV_WIDTH), grp(MXU_DEPTH),
                   grpT, grp(HEAD_DIM), grpT, tr(N_KV_GROUPS * GATE_PAD)),
        out_shape=out_shapes,
        scratch_shapes=[pltpu.VMEM((8, CONV_CH), F32)],
        compiler_params=pltpu.CompilerParams(dimension_semantics=("arbitrary", "arbitrary"),
                                             vmem_limit_bytes=VMEM_LIMIT),
        name="mixer_inproj",
    )(x, cos_p, sin_p, mod, gmix, win, wgate, convw, gconv, gmat, onehot)


def _compress_kernel(kf_ref, vf_ref, pek_ref, pev_ref, wk1_ref, wk2_ref, wv1_ref, wv2_ref,
                     kcc_ref, vcT_ref):
    half = CMP_BLOCK // 2
    n = kf_ref.shape[0] // half

    def mlp(x_ref, pe_ref, w1_ref, w2_ref):
        parts = []
        for p in range(2):
            acc = None
            for l0 in range(0, half, 2):
                xs, ws = [], []
                for l in (l0, l0 + 1):
                    row = p * half + l
                    xs.append((x_ref[pl.ds(l, n, stride=half), :]
                               + pe_ref[row:row + 1, :]).astype(BF16))
                    w = w1_ref[row * HEAD_DIM:(row + 1) * HEAD_DIM, :].astype(BF16)
                    z = jnp.zeros_like(w)
                    ws.append(jnp.concatenate([jnp.concatenate([w, z], axis=1),
                                               jnp.concatenate([z, w], axis=1)], axis=0))
                d = _dot(jnp.concatenate(xs, axis=1), jnp.concatenate(ws, axis=0))
                acc = d if acc is None else acc + d
            parts.append(acc)
        hpre = parts[0] + pltpu.roll(parts[1], n - 1, 0)
        hid = (hpre * jax.nn.sigmoid(hpre)).astype(BF16)
        w2 = w2_ref[...].astype(BF16)
        return jnp.concatenate([_dot(hid[:, g * CMP_HIDDEN:(g + 1) * CMP_HIDDEN], w2)
                                for g in range(N_KV_GROUPS)], axis=1)

    kc = mlp(kf_ref, pek_ref, wk1_ref, wk2_ref).astype(kcc_ref.dtype)
    for g in range(N_KV_GROUPS):
        kcc_ref[g] = kc[:, g * HEAD_DIM:(g + 1) * HEAD_DIM]
    vcT = mlp(vf_ref, pev_ref, wv1_ref, wv2_ref).T.astype(vcT_ref.dtype)
    ones = jnp.ones((ONES_ROWS, n), vcT_ref.dtype)
    for g in range(N_KV_GROUPS):
        vcT_ref[g] = jnp.concatenate([vcT[g * HEAD_DIM:(g + 1) * HEAD_DIM, :], ones], axis=0)


def _compress_call(kf, vf, pek, pev, wk1, wk2, wv1, wv2):
    B, S, width = kf.shape
    n = S // (CMP_BLOCK // 2)
    flat = pl.BlockSpec((None, S, width), lambda b: (b, 0, 0))
    return pl.pallas_call(
        _compress_kernel,
        grid=(B,),
        in_specs=[flat, flat, _const_spec(pek.shape), _const_spec(pev.shape),
                  _const_spec(wk1.shape), _const_spec(wk2.shape), _const_spec(wv1.shape),
                  _const_spec(wv2.shape)],
        out_specs=(pl.BlockSpec((None, N_KV_GROUPS, n, HEAD_DIM), lambda b: (b, 0, 0, 0)),
                   pl.BlockSpec((None, N_KV_GROUPS, VT_ROWS, n), lambda b: (b, 0, 0, 0))),
        out_shape=(jax.ShapeDtypeStruct((B, N_KV_GROUPS, n, HEAD_DIM), BF16),
                   jax.ShapeDtypeStruct((B, N_KV_GROUPS, VT_ROWS, n), BF16)),
        compiler_params=pltpu.CompilerParams(dimension_semantics=("arbitrary",),
                                             vmem_limit_bytes=VMEM_LIMIT),
        name="kv_compress",
    )(kf, vf, pek, pev, wk1, wk2, wv1, wv2)


def _lane_tiles(x, n):
    return jnp.concatenate([x] * n, axis=1)


def _col_max(s_ref, n_rows, bias_fn):
    groups = ROW_TILE // 8
    mx = [jnp.full((8, s_ref.shape[1]), NEG, F32)] * groups
    for r in range(0, n_rows, ROW_TILE):
        x = s_ref[r:r + ROW_TILE, :]
        if bias_fn is not None:
            x = x + bias_fn(r)
            s_ref[r:r + ROW_TILE, :] = x
        mx = [jnp.maximum(mx[i], x[8 * i:8 * (i + 1), :]) for i in range(groups)]
    while len(mx) > 1:
        mx = [jnp.maximum(a, b) for a, b in zip(mx[0::2], mx[1::2])]
    return jnp.max(mx[0], axis=0, keepdims=True)


def _col_exp2(s_ref, p_ref, n_rows, m, keep_f32=False):
    for r in range(0, n_rows, ROW_TILE):
        p = jnp.exp2(s_ref[r:r + ROW_TILE, :] - m)
        if keep_f32:
            s_ref[r:r + ROW_TILE, :] = p
        p_ref[r:r + ROW_TILE, :] = p.astype(p_ref.dtype)


def _recip_pos(l):
    return 1.0 / jnp.where(l > 0.0, l, 1.0)


def _select_kernel(top_n, qT_ref, kcc_ref, vcT_ref, ovlT_ref, sel_ref, ocmp_ref, sc_ref, pc_ref,
                   ph_ref, pl_ref):
    QB = SEL_BLOCK
    n_cmp = kcc_ref.shape[0]
    n_slc = ovlT_ref.shape[0]
    t0 = pl.program_id(2) * QB
    tq = t0 + lax.broadcasted_iota(jnp.int32, (1, QB), 1)
    rows = lax.broadcasted_iota(jnp.int32, (ROW_TILE, 1), 0)

    qT = qT_ref[...]
    qcat = jnp.concatenate([qT[h * HEAD_DIM:(h + 1) * HEAD_DIM, :] for h in range(HPG)], axis=1)

    def cmp_bias(r):
        cmp_end = (r + rows) * CMP_STRIDE + (CMP_BLOCK - 1)
        return _lane_tiles(jnp.where(cmp_end <= tq, 0.0, NEG), HPG)

    cur = tq // SLC_BLOCK

    def tree(op, xs):
        while len(xs) > 1:
            xs = [op(*xs[i:i + 2]) if i + 1 < len(xs) else xs[i] for i in range(0, len(xs), 2)]
        return xs[0]

    def causal_variant(n_rows):
        rows_cmp = min(n_cmp, n_rows * (SLC_BLOCK // CMP_STRIDE))

        def run():
            sc_ref[0:rows_cmp, :] = _dot(kcc_ref[0:rows_cmp, :], qcat)
            m = jnp.maximum(_col_max(sc_ref, rows_cmp, cmp_bias), M_FLOOR)
            _col_exp2(sc_ref, pc_ref, rows_cmp, m, keep_f32=True)
            o_cmp = _dot(vcT_ref[:, 0:rows_cmp], pc_ref[0:rows_cmp, :])
            rl = _recip_pos(o_cmp[HEAD_DIM:HEAD_DIM + 1, :])
            o_cmp = o_cmp[0:HEAD_DIM, :] * rl
            for h in range(HPG):
                ocmp_ref[h] = o_cmp[:, h * QB:(h + 1) * QB]

            for r in range(0, rows_cmp, ROW_TILE):
                pn = sc_ref[r:r + ROW_TILE, :] * rl
                psum = pn[:, 0:QB]
                for h in range(1, HPG):
                    psum = psum + pn[:, h * QB:(h + 1) * QB]
                hi, lo = _split_bf16(psum)
                ph_ref[r:r + ROW_TILE, :] = hi
                pl_ref[r:r + ROW_TILE, :] = lo
            ovl = ovlT_ref[0:n_rows, 0:rows_cmp]
            imp = _dot(ovl, ph_ref[0:rows_cmp, :]) + _dot(ovl, pl_ref[0:rows_cmp, :])

            blk = lax.broadcasted_iota(jnp.int32, (n_rows, 1), 0)
            future = blk > cur
            forced = (blk == 0) | (blk == cur) | (blk == cur - 1)
            score0 = jnp.where(future, -BIG, jnp.where(forced, BIG, imp))
            score = [score0[8 * i:8 * (i + 1), :] for i in range(n_rows // 8)]
            cum = jnp.zeros((1, QB), F32)
            thr = jnp.zeros((1, QB), F32)
            above = jnp.zeros((1, QB), F32)
            for _ in range(top_n):
                best = jnp.max(tree(jnp.maximum, score), axis=0, keepdims=True)
                eq = [s == best for s in score]
                unfilled = cum < top_n
                thr = jnp.where(unfilled, best, thr)
                above = jnp.where(unfilled, cum, above)
                cum = cum + jnp.sum(tree(jnp.add, [jnp.where(e, 1.0, 0.0) for e in eq]),
                                    axis=0, keepdims=True)
                score = [jnp.where(e, -jnp.inf, s) for e, s in zip(eq, score)]
            ties = score0 == thr
            lower = jnp.where(lax.broadcasted_iota(jnp.int32, (1, n_rows), 1) < blk, 1.0, 0.0)
            rank = _dot(lower.astype(BF16), jnp.where(ties, 1.0, 0.0).astype(BF16))
            picked = (score0 > thr) | (ties & (rank < top_n - above))
            sel_ref[0:n_rows, :] = jnp.where(future, NEG, jnp.where(picked, 0.0, NEG)).astype(sel_ref.dtype)
            if n_rows < n_slc:
                sel_ref[n_rows:n_slc, :] = jnp.full((n_slc - n_rows, QB), NEG, sel_ref.dtype)
        return run

    steps = n_slc // SEL_ROW_STEP
    need = jnp.minimum((t0 + QB - 1) // (SLC_BLOCK * SEL_ROW_STEP), steps - 1)
    lax.switch(need, [causal_variant(SEL_ROW_STEP * (k + 1)) for k in range(steps)])


def _select_call(top_n, qT, kcc, vcT, ovlT):
    B, _, S = qT.shape
    n_cmp = kcc.shape[2]
    n_slc = ovlT.shape[0]
    gw = HPG * HEAD_DIM
    qw = HPG * SEL_BLOCK
    return pl.pallas_call(
        functools.partial(_select_kernel, top_n),
        grid=(B, N_KV_GROUPS, S // SEL_BLOCK),
        in_specs=[pl.BlockSpec((None, gw, SEL_BLOCK), lambda b, g, i: (b, g, i)),
                  pl.BlockSpec((None, None, n_cmp, HEAD_DIM), lambda b, g, i: (b, g, 0, 0)),
                  pl.BlockSpec((None, None, VT_ROWS, n_cmp), lambda b, g, i: (b, g, 0, 0)),
                  pl.BlockSpec(ovlT.shape, lambda b, g, i: (0, 0))],
        out_specs=(pl.BlockSpec((None, None, n_slc, SEL_BLOCK), lambda b, g, i: (b, g, 0, i)),
                   pl.BlockSpec((None, None, HPG, HEAD_DIM, SEL_BLOCK),
                                lambda b, g, i: (b, g, 0, 0, i))),
        out_shape=(jax.ShapeDtypeStruct((B, N_KV_GROUPS, n_slc, S), BF16),
                   jax.ShapeDtypeStruct((B, N_KV_GROUPS, HPG, HEAD_DIM, S), F32)),
        scratch_shapes=[pltpu.VMEM((n_cmp, qw), F32), pltpu.VMEM((n_cmp, qw), BF16),
                        pltpu.VMEM((n_cmp, SEL_BLOCK), BF16), pltpu.VMEM((n_cmp, SEL_BLOCK), BF16)],
        compiler_params=pltpu.CompilerParams(
            dimension_semantics=("parallel", "parallel", "arbitrary"),
            vmem_limit_bytes=VMEM_LIMIT),
        name="nsa_select",
    )(qT, kcc, vcT, ovlT)


def _attn_kernel(qT_ref, gT_ref, sel_ref, ocmp_ref, ksa_ref, vsT_ref, kw_ref, vwT_ref, ga_ref,
                 o_ref, s0_ref, s1_ref, s2_ref, s3_ref, p0_ref, p1_ref, sw_ref, pw_ref, rhs_ref,
                 acc_ref):
    QB = Q_BLOCK
    n_slc = sel_ref.shape[0]
    t0 = pl.program_id(2) * QB
    tq = t0 + lax.broadcasted_iota(jnp.int32, (1, QB), 1)
    rows = lax.broadcasted_iota(jnp.int32, (ROW_TILE, 1), 0)

    qT = qT_ref[...]
    qcat = jnp.concatenate([qT[h * HEAD_DIM:(h + 1) * HEAD_DIM, :] for h in range(HPG)], axis=1)

    rhs_ref[...] = jnp.concatenate([_lane_tiles(sel_ref[...], HPG), qcat,
                                    jnp.zeros((MXU_DEPTH - n_slc - HEAD_DIM, QW), BF16)], axis=0)

    w0 = pl.multiple_of(jnp.maximum(t0 - WINDOW, 0), QB)
    sw_ref[...] = _dot(kw_ref[pl.ds(w0, WIN_KEYS), :], qcat)

    def win_bias(r):
        dist = tq - (w0 + r + rows)
        return _lane_tiles(jnp.where((dist >= 0) & (dist < WINDOW), 0.0, NEG), HPG)

    KC = KEY_CHUNK
    last = t0 // KC

    def scores(c, dst_ref):
        k0 = pl.multiple_of(jnp.minimum(c, last) * KC, KC)
        dst_ref[...] = _dot(ksa_ref[pl.ds(k0, KC), :], rhs_ref[...])

    def weighted_values(c, p_ref):
        k0 = pl.multiple_of(jnp.clip(c, 0, last) * KC, KC)
        return _dot(vsT_ref[:, pl.ds(k0, KC)], p_ref[...])

    def add_values(c, p_ref):
        acc_ref[...] += weighted_values(c, p_ref)

    def softmax(c, src_ref, p_ref, m, causal):
        def causal_bias(r):
            return _lane_tiles(jnp.where(c * KC + r + rows <= tq, 0.0, NEG), HPG)

        m_new = jnp.maximum(m, _col_max(src_ref, KC, causal_bias if causal else None))
        acc_ref[...] *= jnp.exp2(m - m_new)
        _col_exp2(src_ref, p_ref, KC, m_new)
        return m_new

    def pair(i, m, first_ref, second_ref, next_ref):
        add_values(2 * i - 1, p1_ref)
        scores(2 * i + 1, second_ref)
        m = softmax(2 * i, first_ref, p0_ref, m, False)
        add_values(2 * i, p0_ref)
        scores(2 * i + 2, next_ref)
        return softmax(2 * i + 1, second_ref, p1_ref, m, True)

    def two_pairs(j, m):
        m = pair(2 * j, m, s0_ref, s1_ref, s2_ref)
        return pair(2 * j + 1, m, s2_ref, s3_ref, s0_ref)

    p1_ref[...] = jnp.zeros_like(p1_ref)
    acc_ref[...] = jnp.zeros_like(acc_ref)
    scores(0, s0_ref)

    m = _col_max(sw_ref, WIN_KEYS, win_bias)
    _col_exp2(sw_ref, pw_ref, WIN_KEYS, m)
    o_win = _dot(vwT_ref[:, pl.ds(w0, WIN_KEYS)], pw_ref[...])
    o_win = o_win[0:HEAD_DIM, :] * (1.0 / o_win[HEAD_DIM:HEAD_DIM + 1, :])

    n_pairs = (last + 1) // 2
    m = lax.fori_loop(0, n_pairs // 2, two_pairs, jnp.full((1, QW), NEG, F32))
    m = lax.cond(n_pairs % 2 == 1, lambda mm: pair(n_pairs - 1, mm, s0_ref, s1_ref, s0_ref),
                 lambda mm: mm, m)
    add_values(2 * n_pairs - 1, p1_ref)

    @pl.when(last % 2 == 0)
    def _():
        softmax(last, s0_ref, p0_ref, m, True)
        add_values(last, p0_ref)

    acc = acc_ref[...]
    o_slc = acc[0:HEAD_DIM, :] * (1.0 / acc[HEAD_DIM:HEAD_DIM + 1, :])

    gts = gT_ref[...]
    outs = []
    for h in range(HPG):
        sl = slice(h * QB, (h + 1) * QB)
        o = (gts[3 * h:3 * h + 1, :] * ocmp_ref[h] + gts[3 * h + 1:3 * h + 2, :] * o_slc[:, sl]
             + gts[3 * h + 2:3 * h + 3, :] * o_win[:, sl])
        o = o * lax.rsqrt(jnp.mean(o * o, axis=0, keepdims=True) + EPS)
        outs.append((o * ga_ref[h * HEAD_DIM:(h + 1) * HEAD_DIM, :]).T)
    o_ref[...] = jnp.concatenate(outs, axis=1).astype(o_ref.dtype)


def _attn_call(qT, gatesT, sel_bias, o_cmp, ksa, vsT, kw, vwT, g_attn_col):
    B, _, S = qT.shape
    n_slc = sel_bias.shape[2]
    gw = HPG * HEAD_DIM
    rows = lambda n, w: pl.BlockSpec((None, None, n, w), lambda b, g, i: (b, g, 0, 0))
    cols = lambda n: pl.BlockSpec((None, None, VT_ROWS, n), lambda b, g, i: (b, g, 0, 0))
    return pl.pallas_call(
        _attn_kernel,
        grid=(B, N_KV_GROUPS, S // Q_BLOCK),
        in_specs=[pl.BlockSpec((None, gw, Q_BLOCK), lambda b, g, i: (b, g, i)),
                  pl.BlockSpec((None, GATE_PAD, Q_BLOCK), lambda b, g, i: (b, g, i)),
                  pl.BlockSpec((None, None, n_slc, Q_BLOCK), lambda b, g, i: (b, g, 0, i)),
                  pl.BlockSpec((None, None, HPG, HEAD_DIM, Q_BLOCK), lambda b, g, i: (b, g, 0, 0, i)),
                  rows(S, MXU_DEPTH), cols(S), rows(S, HEAD_DIM), cols(S),
                  pl.BlockSpec((gw, 1), lambda b, g, i: (g, 0))],
        out_specs=pl.BlockSpec((None, Q_BLOCK, gw), lambda b, g, i: (b, i, g)),
        out_shape=jax.ShapeDtypeStruct((B, S, ATTN_WIDTH), BF16),
        scratch_shapes=[pltpu.VMEM((KEY_CHUNK, QW), F32)] * 4 + [
                        pltpu.VMEM((KEY_CHUNK, QW), BF16), pltpu.VMEM((KEY_CHUNK, QW), BF16),
                        pltpu.VMEM((WIN_KEYS, QW), F32), pltpu.VMEM((WIN_KEYS, QW), BF16),
                        pltpu.VMEM((MXU_DEPTH, QW), BF16), pltpu.VMEM((VT_ROWS, QW), F32)],
        compiler_params=pltpu.CompilerParams(
            dimension_semantics=("parallel", "parallel", "arbitrary"),
            vmem_limit_bytes=VMEM_LIMIT),
        name="nsa_attention",
    )(qT, gatesT, sel_bias, o_cmp, ksa, vsT, kw, vwT, g_attn_col)


def _out_kernel(x_ref, yc_ref, ya_ref, mod_ref, wo_ref, g_ref, wg_ref, wu_ref, wd_ref, gf_ref,
                o_ref):
    mix = _dot(yc_ref[...], wo_ref[0:CONV_CH, :]) + _dot(ya_ref[...], wo_ref[CONV_CH:, :])
    x = x_ref[...] + mod_ref[5:6, :] * mix
    x = _ffn_core(x, mod_ref[6:7, :], mod_ref[7:8, :], mod_ref[8:9, :], g_ref[...], wg_ref,
                  wu_ref, wd_ref)
    o_ref[...] = _rms(x, gf_ref[...])


def _out_call(x, yc, ya, mod, wo, g, wg, wu, wd, gf):
    B, S, D = x.shape
    tm = FFN_TOKEN_TILE
    tok = lambda w: pl.BlockSpec((None, tm, w), lambda b, i: (b, i, 0))
    return pl.pallas_call(
        _out_kernel,
        grid=(B, S // tm),
        in_specs=[tok(D), tok(CONV_CH), tok(ATTN_WIDTH),
                  pl.BlockSpec((None, N_MOD, D), lambda b, i: (b, 0, 0)),
                  _const_spec(wo.shape), _const_spec(g.shape), _const_spec(wg.shape),
                  _const_spec(wu.shape), _const_spec(wd.shape), _const_spec(gf.shape)],
        out_specs=tok(D),
        out_shape=jax.ShapeDtypeStruct((B, S, D), F32),
        compiler_params=pltpu.CompilerParams(dimension_semantics=("parallel", "parallel"),
                                             vmem_limit_bytes=VMEM_LIMIT),
        name="outproj_ffn2",
    )(x, yc, ya, mod, wo, g, wg, wu, wd, gf)


def kernel(x, c, positions, w_ada, b_ada, g_ffn1, w1_gate, w1_up, w1_down, g_mix, w_in, conv_w, cmp_pos_k, cmp_pos_v, w_cmpk1, w_cmpk2, w_cmpv1, w_cmpv2, g_out_conv, g_out_attn, w_out, g_ffn2, w2_gate, w2_up, w2_down, g_final):
    B, S, D = x.shape
    depth = w_ada.shape[0]
    n_slc = S // SLC_BLOCK
    half = CMP_BLOCK // 2
    n_half = S // half
    assert n_slc <= LANES, "selection-block one-hot is one lane tile wide"

    c_pad = jnp.pad(c, ((0, 8 - B), (0, 0)))
    row = lambda a: a.reshape(1, -1)

    freq_half = jnp.power(ROPE_THETA, -2.0 * jnp.arange(ROT_HALF, dtype=F32) / ROT_DIM)
    freq = jnp.tile(freq_half, LANES // ROT_HALF).reshape(1, LANES)
    gidx = np.arange(CONV_CH) // (CONV_CH // CONV_GROUPS)
    gmat = jnp.asarray((gidx[:, None] == gidx[None, :]) / (CONV_CH // CONV_GROUPS), dtype=BF16)
    c0 = np.arange(n_half) * CMP_STRIDE
    s0 = np.arange(LANES) * SLC_BLOCK
    ovlT = ((c0[None, :] <= s0[:, None] + SLC_BLOCK - 1) & (c0[None, :] + CMP_BLOCK - 1 >= s0[:, None]))
    ovlT = jnp.asarray(ovlT, dtype=BF16)
    onehot = jnp.asarray((np.arange(S) // SLC_BLOCK)[:, None] == np.arange(LANES)[None, :], dtype=BF16)
    cos_p, sin_p = _rope_table_call(positions.reshape(-1, TOKENS_PER_ROW), freq)
    cos_p = cos_p.reshape(B, S // TOKENS_PER_ROW, LANES)
    sin_p = sin_p.reshape(B, S // TOKENS_PER_ROW, LANES)

    for l in range(depth):
        mod = _ada_call(c_pad, w_ada[l], row(b_ada[l]))[:B].reshape(B, N_MOD, D)

        x = _ffn_call(x, mod, row(g_ffn1[l]), w1_gate[l].astype(BF16), w1_up[l].astype(BF16),
                      w1_down[l].astype(BF16))

        n_main = w_in.shape[2] - N_KV_GROUPS * 3 * HPG
        gate_cols = [jnp.pad(w_in[l][:, n_main + g * 3 * HPG:n_main + (g + 1) * 3 * HPG],
                             ((0, 0), (0, GATE_PAD - 3 * HPG))) for g in range(N_KV_GROUPS)]
        wgate = jnp.concatenate(gate_cols, axis=1).astype(BF16)
        win = w_in[l][:, :n_main].astype(BF16)
        (yc, qT, kc, vc, ksa, vsT, kw, vwT, gatesT) = _inproj_call(
            x, cos_p, sin_p, mod, row(g_mix[l]), win, wgate, conv_w[l], row(g_out_conv[l]), gmat,
            onehot)

        both_groups = lambda pe: jnp.tile(pe, (1, N_KV_GROUPS))
        kcc, vcT = _compress_call(kc, vc, both_groups(cmp_pos_k[l]), both_groups(cmp_pos_v[l]),
                                  w_cmpk1[l], w_cmpk2[l], w_cmpv1[l], w_cmpv2[l])

        sel_bias, o_cmp = _select_call(min(SLC_TOP_N, n_slc), qT, kcc, vcT, ovlT)
        ya = _attn_call(qT, gatesT, sel_bias, o_cmp, ksa, vsT, kw, vwT,
                        g_out_attn[l].reshape(ATTN_WIDTH, 1))

        assert l == depth - 1, "final norm is fused into the last layer's output kernel"
        x = _out_call(x, yc, ya, mod, w_out[l].astype(BF16), row(g_ffn2[l]),
                      w2_gate[l].astype(BF16), w2_up[l].astype(BF16), w2_down[l].astype(BF16),
                      row(g_final))
    return x
```

```python
import functools
import math

import numpy as np
import jax
import jax.numpy as jnp
from jax import lax
from jax.experimental import pallas as pl
from jax.experimental.pallas import tpu as pltpu

F32 = jnp.float32
BF16 = jnp.bfloat16

CONV_CH = 512
CONV_GROUPS = 8
N_HEADS = 8
N_KV_GROUPS = 2
HPG = N_HEADS // N_KV_GROUPS
HEAD_DIM = 64
ATTN_WIDTH = N_HEADS * HEAD_DIM
KV_WIDTH = N_KV_GROUPS * HEAD_DIM
ROPE_THETA = 500000.0
ROT_DIM = HEAD_DIM // 4
ROT_HALF = ROT_DIM // 2
CMP_BLOCK = 32
CMP_STRIDE = 16
CMP_HIDDEN = 256
SLC_BLOCK = 64
SLC_TOP_N = 16
WINDOW = 512
Q_BLOCK = 256
MACARON_W = 0.5
N_MOD = 9
EPS = 1e-6
NEG = -1e30
BIG = 1e9

LANES = 128
MXU_DEPTH = 256
VMEM_LIMIT = 56 * 1024 * 1024

TOKEN_TILE = 512
FFN_TOKEN_TILE = 512
ADA_COL_TILE = 1024
FF_TILE = 256
KEY_CHUNK = 512
WIN_KEYS = WINDOW + Q_BLOCK
TOKENS_PER_ROW = LANES // ROT_HALF
GATE_PAD = LANES
QW = HPG * Q_BLOCK
SEL_BLOCK = 1024
SEL_ROW_STEP = 32
ONES_ROWS = 16
VT_ROWS = HEAD_DIM + ONES_ROWS
ROW_TILE = 16
Q_SCALE = HEAD_DIM ** -0.5 * math.log2(math.e)
M_FLOOR = -1e20


def _dot(a, b):
    return jnp.dot(a, b, preferred_element_type=F32)


def _rms(x, g):
    return x * lax.rsqrt(jnp.mean(x * x, axis=-1, keepdims=True) + EPS) * g


def _split_bf16(x):
    hi = x.astype(BF16)
    lo = (x - hi.astype(F32)).astype(BF16)
    return hi, lo


def _const_spec(shape):
    nd = len(shape)
    return pl.BlockSpec(shape, lambda *_: (0,) * nd, pipeline_mode=pl.Buffered(1))


def _ada_kernel(c_ref, w_ref, b_ref, o_ref):
    c = c_ref[...]
    c_act = c * jax.nn.sigmoid(c)
    o_ref[...] = jnp.dot(c_act, w_ref[...], preferred_element_type=F32,
                         precision=lax.Precision.HIGHEST) + b_ref[...]


def _ada_call(c_pad, w_ada, b_ada):
    rows, d = c_pad.shape
    n = w_ada.shape[1]
    tn = ADA_COL_TILE
    return pl.pallas_call(
        _ada_kernel,
        grid=(n // tn,),
        in_specs=[pl.BlockSpec((rows, d), lambda j: (0, 0)),
                  pl.BlockSpec((d, tn), lambda j: (0, j)),
                  pl.BlockSpec((1, tn), lambda j: (0, j))],
        out_specs=pl.BlockSpec((rows, tn), lambda j: (0, j)),
        out_shape=jax.ShapeDtypeStruct((rows, n), F32),
        compiler_params=pltpu.CompilerParams(dimension_semantics=("arbitrary",),
                                             vmem_limit_bytes=VMEM_LIMIT),
        name="adaln_mod",
    )(c_pad, w_ada, b_ada)


def _ffn_core(x, shift, scale, gate, g, wg_ref, wu_ref, wd_ref):
    h = _rms(x, g) * (1.0 + scale) + shift
    hb = h.astype(BF16)
    d_ff = wg_ref.shape[1]
    acc = None
    for j in range(d_ff // FF_TILE):
        sl = slice(j * FF_TILE, (j + 1) * FF_TILE)
        gg = _dot(hb, wg_ref[:, sl])
        uu = _dot(hb, wu_ref[:, sl])
        a = (gg * jax.nn.sigmoid(gg) * uu).astype(BF16)
        d = _dot(a, wd_ref[sl, :])
        acc = d if acc is None else acc + d
    return x + (MACARON_W * gate) * acc


def _ffn_kernel(x_ref, mod_ref, g_ref, wg_ref, wu_ref, wd_ref, o_ref):
    o_ref[...] = _ffn_core(x_ref[...], mod_ref[0:1, :], mod_ref[1:2, :], mod_ref[2:3, :],
                           g_ref[...], wg_ref, wu_ref, wd_ref)


def _ffn_call(x, mod, g, wg, wu, wd):
    B, S, D = x.shape
    tm = FFN_TOKEN_TILE
    return pl.pallas_call(
        _ffn_kernel,
        grid=(B, S // tm),
        in_specs=[pl.BlockSpec((None, tm, D), lambda b, i: (b, i, 0)),
                  pl.BlockSpec((None, N_MOD, D), lambda b, i: (b, 0, 0)),
                  _const_spec(g.shape), _const_spec(wg.shape), _const_spec(wu.shape),
                  _const_spec(wd.shape)],
        out_specs=pl.BlockSpec((None, tm, D), lambda b, i: (b, i, 0)),
        out_shape=jax.ShapeDtypeStruct((B, S, D), F32),
        compiler_params=pltpu.CompilerParams(dimension_semantics=("parallel", "parallel"),
                                             vmem_limit_bytes=VMEM_LIMIT),
        name="ffn1",
    )(x, mod, g, wg, wu, wd)


def _rope_table_kernel(pos_ref, freq_ref, cos_ref, sin_ref):
    pos = jnp.concatenate([pos_ref[...].astype(F32),
                           jnp.zeros((pos_ref.shape[0], LANES - TOKENS_PER_ROW), F32)], axis=1)
    src = lax.broadcasted_iota(jnp.int32, pos.shape, 1) // ROT_HALF
    ang = jnp.take_along_axis(pos, src, axis=1) * freq_ref[...]
    cos_ref[...] = jnp.cos(ang)
    sin_ref[...] = jnp.sin(ang)


def _rope_table_call(pos_rows, freq):
    rows = pos_rows.shape[0]
    shape = jax.ShapeDtypeStruct((rows, LANES), F32)
    full = pl.BlockSpec((rows, LANES), lambda: (0, 0))
    return pl.pallas_call(
        _rope_table_kernel,
        in_specs=[pl.BlockSpec(pos_rows.shape, lambda: (0, 0)), pl.BlockSpec(freq.shape, lambda: (0, 0))],
        out_specs=(full, full),
        out_shape=(shape, shape),
        compiler_params=pltpu.CompilerParams(vmem_limit_bytes=VMEM_LIMIT),
        name="rope_table",
    )(pos_rows, freq)


def _inproj_kernel(x_ref, cosp_ref, sinp_ref, mod_ref, gmix_ref, win_ref, wgate_ref, convw_ref, gconv_ref,
                   gmat_ref, onehot_ref, yc_ref, qT_ref, kc_ref, vc_ref, ksa_ref, vsT_ref, kw_ref,
                   vwT_ref, gT_ref, carry_ref):
    tm = x_ref.shape[0]

    @pl.when(pl.program_id(1) == 0)
    def _():
        carry_ref[...] = jnp.zeros_like(carry_ref)

    x = x_ref[...]
    h = _rms(x, gmix_ref[...]) * (1.0 + mod_ref[4:5, :]) + mod_ref[3:4, :]
    hb = h.astype(BF16)

    def proj(c0, width):
        return _dot(hb, win_ref[:, c0:c0 + width])

    d = lax.broadcasted_iota(jnp.int32, (tm, LANES), 1) & (HEAD_DIM - 1)
    token = lax.broadcasted_iota(jnp.int32, (tm, LANES), 0) & (TOKENS_PER_ROW - 1)
    src_lane = token * ROT_HALF + (d & (ROT_HALF - 1))

    def unpack(packed_ref):
        rows = jnp.broadcast_to(packed_ref[...][:, None, :], (tm // TOKENS_PER_ROW, TOKENS_PER_ROW, LANES))
        return jnp.take_along_axis(rows.reshape(tm, LANES), src_lane, axis=1)

    cos_t = jnp.where(d < ROT_DIM, unpack(cosp_ref), 1.0)
    sin_raw = unpack(sinp_ref)
    sin_t = jnp.where(d < ROT_HALF, -sin_raw, jnp.where(d < ROT_DIM, sin_raw, 0.0))
    first_half = d < ROT_HALF

    def rope(t):
        outs = []
        for j in range(t.shape[1] // LANES):
            tj = t[:, j * LANES:(j + 1) * LANES]
            partner = jnp.where(first_half, pltpu.roll(tj, LANES - ROT_HALF, 1),
                                pltpu.roll(tj, ROT_HALF, 1))
            outs.append(tj * cos_t + partner * sin_t)
        return outs[0] if len(outs) == 1 else jnp.concatenate(outs, axis=1)

    c0 = 3 * CONV_CH
    q = rope(proj(c0, ATTN_WIDTH)) * Q_SCALE
    qT_ref[...] = q.T.astype(qT_ref.dtype)
    c0 += ATTN_WIDTH
    kv = proj(c0, 2 * KV_WIDTH)
    kc_ref[...] = rope(kv[:, :KV_WIDTH])
    vc_ref[...] = kv[:, KV_WIDTH:]
    kv = proj(c0 + 2 * KV_WIDTH, 2 * KV_WIDTH)
    ks = rope(kv[:, :KV_WIDTH]).astype(BF16)
    vsT = kv[:, KV_WIDTH:].T.astype(BF16)
    kv = proj(c0 + 4 * KV_WIDTH, 2 * KV_WIDTH)
    kw = rope(kv[:, :KV_WIDTH]).astype(BF16)
    vwT = kv[:, KV_WIDTH:].T.astype(BF16)
    pad = jnp.zeros((tm, MXU_DEPTH - LANES - HEAD_DIM), BF16)
    ones = jnp.ones((ONES_ROWS, tm), BF16)
    for g in range(N_KV_GROUPS):
        kg = ks[:, g * HEAD_DIM:(g + 1) * HEAD_DIM]
        ksa_ref[g] = jnp.concatenate([onehot_ref[...], kg, pad], axis=1)
        kw_ref[g] = kw[:, g * HEAD_DIM:(g + 1) * HEAD_DIM]
        vsT_ref[g] = jnp.concatenate([vsT[g * HEAD_DIM:(g + 1) * HEAD_DIM, :], ones], axis=0)
        vwT_ref[g] = jnp.concatenate([vwT[g * HEAD_DIM:(g + 1) * HEAD_DIM, :], ones], axis=0)
    gT_ref[...] = jax.nn.sigmoid(_dot(hb, wgate_ref[...])).T

    cb = proj(0, CONV_CH)
    u = proj(CONV_CH, CONV_CH) * proj(2 * CONV_CH, CONV_CH)
    row = lax.broadcasted_iota(jnp.int32, (tm, 1), 0)
    prev1 = carry_ref[7:8, :]
    prev2 = carry_ref[6:7, :]
    u1 = jnp.where(row >= 1, pltpu.roll(u, 1, 0), prev1)
    u2 = jnp.where(row >= 2, pltpu.roll(u, 2, 0), jnp.where(row == 1, prev1, prev2))
    carry_ref[...] = u[tm - 8:tm, :]
    v = convw_ref[0:1, :] * u2 + convw_ref[1:2, :] * u1 + convw_ref[2:3, :] * u
    y = cb * v
    hi, lo = _split_bf16(y * y)
    ms = _dot(hi, gmat_ref[...]) + _dot(lo, gmat_ref[...])
    yc_ref[...] = (y * lax.rsqrt(ms + EPS) * gconv_ref[...]).astype(yc_ref.dtype)


def _inproj_call(x, cos_p, sin_p, mod, gmix, win, wgate, convw, gconv, gmat, onehot):
    B, S, D = x.shape
    tm = TOKEN_TILE
    tok = lambda w: pl.BlockSpec((None, tm, w), lambda b, i: (b, i, 0))
    tr = lambda w: pl.BlockSpec((None, w, tm), lambda b, i: (b, 0, i))
    grp = lambda w: pl.BlockSpec((None, N_KV_GROUPS, tm, w), lambda b, i: (b, 0, i, 0))
    grpT = pl.BlockSpec((None, N_KV_GROUPS, VT_ROWS, tm), lambda b, i: (b, 0, 0, i))
    packed = pl.BlockSpec((None, tm // TOKENS_PER_ROW, LANES), lambda b, i: (b, i, 0))
    out_shapes = (
        jax.ShapeDtypeStruct((B, S, CONV_CH), BF16),
        jax.ShapeDtypeStruct((B, ATTN_WIDTH, S), BF16),
        jax.ShapeDtypeStruct((B, S, KV_WIDTH), F32),
        jax.ShapeDtypeStruct((B, S, KV_WIDTH), F32),
        jax.ShapeDtypeStruct((B, N_KV_GROUPS, S, MXU_DEPTH), BF16),
        jax.ShapeDtypeStruct((B, N_KV_GROUPS, VT_ROWS, S), BF16),
        jax.ShapeDtypeStruct((B, N_KV_GROUPS, S, HEAD_DIM), BF16),
        jax.ShapeDtypeStruct((B, N_KV_GROUPS, VT_ROWS, S), BF16),
        jax.ShapeDtypeStruct((B, N_KV_GROUPS * GATE_PAD, S), F32),
    )
    return pl.pallas_call(
        _inproj_kernel,
        grid=(B, S // tm),
        in_specs=[tok(D),
                  packed, packed,
                  pl.BlockSpec((None, N_MOD, D), lambda b, i: (b, 0, 0)),
                  _const_spec(gmix.shape), _const_spec(win.shape), _const_spec(wgate.shape),
                  _const_spec(convw.shape),
                  _const_spec(gconv.shape), _const_spec(gmat.shape),
                  pl.BlockSpec((tm, onehot.shape[1]), lambda b, i: (i, 0))],
        out_specs=(tok(CONV_CH), tr(ATTN_WIDTH), tok(KV_WIDTH), tok(KV_WIDTH), grp(MXU_DEPTH),
                   grpT, grp(HEAD_DIM), grpT, tr(N_KV_GROUPS * GATE_PAD)),
        out_shape=out_shapes,
        scratch_shapes=[pltpu.VMEM((8, CONV_CH), F32)],
        compiler_params=pltpu.CompilerParams(dimension_semantics=("arbitrary", "arbitrary"),
                                             vmem_limit_bytes=VMEM_LIMIT),
        name="mixer_inproj",
    )(x, cos_p, sin_p, mod, gmix, win, wgate, convw, gconv, gmat, onehot)


def _compress_kernel(kf_ref, vf_ref, pek_ref, pev_ref, wk1_ref, wk2_ref, wv1_ref, wv2_ref,
                     kcc_ref, vcT_ref):
    half = CMP_BLOCK // 2
    n = kf_ref.shape[0] // half

    def mlp(x_ref, pe_ref, w1_ref, w2_ref):
        parts = []
        for p in range(2):
            acc = None
            for l0 in range(0, half, 2):
                xs, ws = [], []
                for l in (l0, l0 + 1):
                    row = p * half + l
                    xs.append((x_ref[pl.ds(l, n, stride=half), :]
                               + pe_ref[row:row + 1, :]).astype(BF16))
                    w = w1_ref[row * HEAD_DIM:(row + 1) * HEAD_DIM, :].astype(BF16)
                    z = jnp.zeros_like(w)
                    ws.append(jnp.concatenate([jnp.concatenate([w, z], axis=1),
                                               jnp.concatenate([z, w], axis=1)], axis=0))
                d = _dot(jnp.concatenate(xs, axis=1), jnp.concatenate(ws, axis=0))
                acc = d if acc is None else acc + d
            parts.append(acc)
        hpre = parts[0] + pltpu.roll(parts[1], n - 1, 0)
        hid = (hpre * jax.nn.sigmoid(hpre)).astype(BF16)
        w2 = w2_ref[...].astype(BF16)
        return jnp.concatenate([_dot(hid[:, g * CMP_HIDDEN:(g + 1) * CMP_HIDDEN], w2)
                                for g in range(N_KV_GROUPS)], axis=1)

    kc = mlp(kf_ref, pek_ref, wk1_ref, wk2_ref).astype(kcc_ref.dtype)
    for g in range(N_KV_GROUPS):
        kcc_ref[g] = kc[:, g * HEAD_DIM:(g + 1) * HEAD_DIM]
    vcT = mlp(vf_ref, pev_ref, wv1_ref, wv2_ref).T.astype(vcT_ref.dtype)
    ones = jnp.ones((ONES_ROWS, n), vcT_ref.dtype)
    for g in range(N_KV_GROUPS):
        vcT_ref[g] = jnp.concatenate([vcT[g * HEAD_DIM:(g + 1) * HEAD_DIM, :], ones], axis=0)


def _compress_call(kf, vf, pek, pev, wk1, wk2, wv1, wv2):
    B, S, width = kf.shape
    n = S // (CMP_BLOCK // 2)
    flat = pl.BlockSpec((None, S, width), lambda b: (b, 0, 0))
    return pl.pallas_call(
        _compress_kernel,
        grid=(B,),
        in_specs=[flat, flat, _const_spec(pek.shape), _const_spec(pev.shape),
                  _const_spec(wk1.shape), _const_spec(wk2.shape), _const_spec(wv1.shape),
                  _const_spec(wv2.shape)],
        out_specs=(pl.BlockSpec((None, N_KV_GROUPS, n, HEAD_DIM), lambda b: (b, 0, 0, 0)),
                   pl.BlockSpec((None, N_KV_GROUPS, VT_ROWS, n), lambda b: (b, 0, 0, 0))),
        out_shape=(jax.ShapeDtypeStruct((B, N_KV_GROUPS, n, HEAD_DIM), BF16),
                   jax.ShapeDtypeStruct((B, N_KV_GROUPS, VT_ROWS, n), BF16)),
        compiler_params=pltpu.CompilerParams(dimension_semantics=("arbitrary",),
                                             vmem_limit_bytes=VMEM_LIMIT),
        name="kv_compress",
    )(kf, vf, pek, pev, wk1, wk2, wv1, wv2)


def _lane_tiles(x, n):
    return jnp.concatenate([x] * n, axis=1)


def _col_max(s_ref, n_rows, bias_fn):
    groups = ROW_TILE // 8
    mx = [jnp.full((8, s_ref.shape[1]), NEG, F32)] * groups
    for r in range(0, n_rows, ROW_TILE):
        x = s_ref[r:r + ROW_TILE, :]
        if bias_fn is not None:
            x = x + bias_fn(r)
            s_ref[r:r + ROW_TILE, :] = x
        mx = [jnp.maximum(mx[i], x[8 * i:8 * (i + 1), :]) for i in range(groups)]
    while len(mx) > 1:
        mx = [jnp.maximum(a, b) for a, b in zip(mx[0::2], mx[1::2])]
    return jnp.max(mx[0], axis=0, keepdims=True)


def _col_exp2(s_ref, p_ref, n_rows, m, keep_f32=False):
    for r in range(0, n_rows, ROW_TILE):
        p = jnp.exp2(s_ref[r:r + ROW_TILE, :] - m)
        if keep_f32:
            s_ref[r:r + ROW_TILE, :] = p
        p_ref[r:r + ROW_TILE, :] = p.astype(p_ref.dtype)


def _recip_pos(l):
    return 1.0 / jnp.where(l > 0.0, l, 1.0)


def _select_kernel(top_n, qT_ref, kcc_ref, vcT_ref, ovlT_ref, sel_ref, ocmp_ref, sc_ref, pc_ref,
                   ph_ref, pl_ref):
    QB = SEL_BLOCK
    n_cmp = kcc_ref.shape[0]
    n_slc = ovlT_ref.shape[0]
    t0 = pl.program_id(2) * QB
    tq = t0 + lax.broadcasted_iota(jnp.int32, (1, QB), 1)
    rows = lax.broadcasted_iota(jnp.int32, (ROW_TILE, 1), 0)

    qT = qT_ref[...]
    qcat = jnp.concatenate([qT[h * HEAD_DIM:(h + 1) * HEAD_DIM, :] for h in range(HPG)], axis=1)

    def cmp_bias(r):
        cmp_end = (r + rows) * CMP_STRIDE + (CMP_BLOCK - 1)
        return _lane_tiles(jnp.where(cmp_end <= tq, 0.0, NEG), HPG)

    cur = tq // SLC_BLOCK

    def tree(op, xs):
        while len(xs) > 1:
            xs = [op(*xs[i:i + 2]) if i + 1 < len(xs) else xs[i] for i in range(0, len(xs), 2)]
        return xs[0]

    def causal_variant(n_rows):
        rows_cmp = min(n_cmp, n_rows * (SLC_BLOCK // CMP_STRIDE))

        def run():
            sc_ref[0:rows_cmp, :] = _dot(kcc_ref[0:rows_cmp, :], qcat)
            m = jnp.maximum(_col_max(sc_ref, rows_cmp, cmp_bias), M_FLOOR)
            _col_exp2(sc_ref, pc_ref, rows_cmp, m, keep_f32=True)
            o_cmp = _dot(vcT_ref[:, 0:rows_cmp], pc_ref[0:rows_cmp, :])
            rl = _recip_pos(o_cmp[HEAD_DIM:HEAD_DIM + 1, :])
            o_cmp = o_cmp[0:HEAD_DIM, :] * rl
            for h in range(HPG):
                ocmp_ref[h] = o_cmp[:, h * QB:(h + 1) * QB]

            for r in range(0, rows_cmp, ROW_TILE):
                pn = sc_ref[r:r + ROW_TILE, :] * rl
                psum = pn[:, 0:QB]
                for h in range(1, HPG):
                    psum = psum + pn[:, h * QB:(h + 1) * QB]
                hi, lo = _split_bf16(psum)
                ph_ref[r:r + ROW_TILE, :] = hi
                pl_ref[r:r + ROW_TILE, :] = lo
            ovl = ovlT_ref[0:n_rows, 0:rows_cmp]
            imp = _dot(ovl, ph_ref[0:rows_cmp, :]) + _dot(ovl, pl_ref[0:rows_cmp, :])

            blk = lax.broadcasted_iota(jnp.int32, (n_rows, 1), 0)
            future = blk > cur
            forced = (blk == 0) | (blk == cur) | (blk == cur - 1)
            score0 = jnp.where(future, -BIG, jnp.where(forced, BIG, imp))
            score = [score0[8 * i:8 * (i + 1), :] for i in range(n_rows // 8)]
            cum = jnp.zeros((1, QB), F32)
            thr = jnp.zeros((1, QB), F32)
            above = jnp.zeros((1, QB), F32)
            for _ in range(top_n):
                best = jnp.max(tree(jnp.maximum, score), axis=0, keepdims=True)
                eq = [s == best for s in score]
                unfilled = cum < top_n
                thr = jnp.where(unfilled, best, thr)
                above = jnp.where(unfilled, cum, above)
                cum = cum + jnp.sum(tree(jnp.add, [jnp.where(e, 1.0, 0.0) for e in eq]),
                                    axis=0, keepdims=True)
                score = [jnp.where(e, -jnp.inf, s) for e, s in zip(eq, score)]
            ties = score0 == thr
            lower = jnp.where(lax.broadcasted_iota(jnp.int32, (1, n_rows), 1) < blk, 1.0, 0.0)
            rank = _dot(lower.astype(BF16), jnp.where(ties, 1.0, 0.0).astype(BF16))
            picked = (score0 > thr) | (ties & (rank < top_n - above))
            sel_ref[0:n_rows, :] = jnp.where(future, NEG, jnp.where(picked, 0.0, NEG)).astype(sel_ref.dtype)
            if n_rows < n_slc:
                sel_ref[n_rows:n_slc, :] = jnp.full((n_slc - n_rows, QB), NEG, sel_ref.dtype)
        return run

    steps = n_slc // SEL_ROW_STEP
    need = jnp.minimum((t0 + QB - 1) // (SLC_BLOCK * SEL_ROW_STEP), steps - 1)
    lax.switch(need, [causal_variant(SEL_ROW_STEP * (k + 1)) for k in range(steps)])


def _select_call(top_n, qT, kcc, vcT, ovlT):
    B, _, S = qT.shape
    n_cmp = kcc.shape[2]
    n_slc = ovlT.shape[0]
    gw = HPG * HEAD_DIM
    qw = HPG * SEL_BLOCK
    return pl.pallas_call(
        functools.partial(_select_kernel, top_n),
        grid=(B, N_KV_GROUPS, S // SEL_BLOCK),
        in_specs=[pl.BlockSpec((None, gw, SEL_BLOCK), lambda b, g, i: (b, g, i)),
                  pl.BlockSpec((None, None, n_cmp, HEAD_DIM), lambda b, g, i: (b, g, 0, 0)),
                  pl.BlockSpec((None, None, VT_ROWS, n_cmp), lambda b, g, i: (b, g, 0, 0)),
                  pl.BlockSpec(ovlT.shape, lambda b, g, i: (0, 0))],
        out_specs=(pl.BlockSpec((None, None, n_slc, SEL_BLOCK), lambda b, g, i: (b, g, 0, i)),
                   pl.BlockSpec((None, None, HPG, HEAD_DIM, SEL_BLOCK),
                                lambda b, g, i: (b, g, 0, 0, i))),
        out_shape=(jax.ShapeDtypeStruct((B, N_KV_GROUPS, n_slc, S), BF16),
                   jax.ShapeDtypeStruct((B, N_KV_GROUPS, HPG, HEAD_DIM, S), F32)),
        scratch_shapes=[pltpu.VMEM((n_cmp, qw), F32), pltpu.VMEM((n_cmp, qw), BF16),
                        pltpu.VMEM((n_cmp, SEL_BLOCK), BF16), pltpu.VMEM((n_cmp, SEL_BLOCK), BF16)],
        compiler_params=pltpu.CompilerParams(
            dimension_semantics=("parallel", "parallel", "arbitrary"),
            vmem_limit_bytes=VMEM_LIMIT),
        name="nsa_select",
    )(qT, kcc, vcT, ovlT)


def _attn_kernel(qT_ref, gT_ref, sel_ref, ocmp_ref, ksa_ref, vsT_ref, kw_ref, vwT_ref, ga_ref,
                 o_ref, s0_ref, s1_ref, p0_ref, p1_ref, sw_ref, pw_ref, rhs_ref, acc_ref):
    QB = Q_BLOCK
    n_slc = sel_ref.shape[0]
    t0 = pl.program_id(2) * QB
    tq = t0 + lax.broadcasted_iota(jnp.int32, (1, QB), 1)
    rows = lax.broadcasted_iota(jnp.int32, (ROW_TILE, 1), 0)

    qT = qT_ref[...]
    qcat = jnp.concatenate([qT[h * HEAD_DIM:(h + 1) * HEAD_DIM, :] for h in range(HPG)], axis=1)

    rhs_ref[...] = jnp.concatenate([_lane_tiles(sel_ref[...], HPG), qcat,
                                    jnp.zeros((MXU_DEPTH - n_slc - HEAD_DIM, QW), BF16)], axis=0)

    w0 = pl.multiple_of(jnp.maximum(t0 - WINDOW, 0), QB)
    sw_ref[...] = _dot(kw_ref[pl.ds(w0, WIN_KEYS), :], qcat)

    def win_bias(r):
        dist = tq - (w0 + r + rows)
        return _lane_tiles(jnp.where((dist >= 0) & (dist < WINDOW), 0.0, NEG), HPG)

    KC = KEY_CHUNK
    last = t0 // KC

    def scores(c, dst_ref):
        k0 = pl.multiple_of(jnp.minimum(c, last) * KC, KC)
        dst_ref[...] = _dot(ksa_ref[pl.ds(k0, KC), :], rhs_ref[...])

    def weighted_values(c, p_ref):
        k0 = pl.multiple_of(jnp.clip(c, 0, last) * KC, KC)
        return _dot(vsT_ref[:, pl.ds(k0, KC)], p_ref[...])

    def softmax(c, src_ref, p_ref, m, pending, causal):
        def causal_bias(r):
            return _lane_tiles(jnp.where(c * KC + r + rows <= tq, 0.0, NEG), HPG)

        m_new = jnp.maximum(m, _col_max(src_ref, KC, causal_bias if causal else None))
        acc_ref[...] = jnp.exp2(m - m_new) * (acc_ref[...] + pending)
        _col_exp2(src_ref, p_ref, KC, m_new)
        return m_new

    def pair(i, m):
        pending = weighted_values(2 * i - 1, p1_ref)
        scores(2 * i + 1, s1_ref)
        m = softmax(2 * i, s0_ref, p0_ref, m, pending, False)
        pending = weighted_values(2 * i, p0_ref)
        scores(2 * i + 2, s0_ref)
        return softmax(2 * i + 1, s1_ref, p1_ref, m, pending, True)

    p1_ref[...] = jnp.zeros_like(p1_ref)
    acc_ref[...] = jnp.zeros_like(acc_ref)
    scores(0, s0_ref)

    m = _col_max(sw_ref, WIN_KEYS, win_bias)
    _col_exp2(sw_ref, pw_ref, WIN_KEYS, m)
    o_win = _dot(vwT_ref[:, pl.ds(w0, WIN_KEYS)], pw_ref[...])
    o_win = o_win[0:HEAD_DIM, :] * (1.0 / o_win[HEAD_DIM:HEAD_DIM + 1, :])

    n_pairs = (last + 1) // 2
    m = lax.fori_loop(0, n_pairs // 2, lambda j, mm: pair(2 * j + 1, pair(2 * j, mm)),
                      jnp.full((1, QW), NEG, F32))
    m = lax.cond(n_pairs % 2 == 1, lambda mm: pair(n_pairs - 1, mm), lambda mm: mm, m)
    pending = weighted_values(2 * n_pairs - 1, p1_ref)

    @pl.when(last % 2 == 0)
    def _():
        softmax(last, s0_ref, p0_ref, m, pending, True)
        acc_ref[...] += weighted_values(last, p0_ref)

    @pl.when(last % 2 == 1)
    def _():
        acc_ref[...] += pending

    acc = acc_ref[...]
    o_slc = acc[0:HEAD_DIM, :] * (1.0 / acc[HEAD_DIM:HEAD_DIM + 1, :])

    gts = gT_ref[...]
    outs = []
    for h in range(HPG):
        sl = slice(h * QB, (h + 1) * QB)
        o = (gts[3 * h:3 * h + 1, :] * ocmp_ref[h] + gts[3 * h + 1:3 * h + 2, :] * o_slc[:, sl]
             + gts[3 * h + 2:3 * h + 3, :] * o_win[:, sl])
        o = o * lax.rsqrt(jnp.mean(o * o, axis=0, keepdims=True) + EPS)
        outs.append((o * ga_ref[h * HEAD_DIM:(h + 1) * HEAD_DIM, :]).T)
    o_ref[...] = jnp.concatenate(outs, axis=1).astype(o_ref.dtype)


def _attn_call(qT, gatesT, sel_bias, o_cmp, ksa, vsT, kw, vwT, g_attn_col):
    B, _, S = qT.shape
    n_slc = sel_bias.shape[2]
    gw = HPG * HEAD_DIM
    rows = lambda n, w: pl.BlockSpec((None, None, n, w), lambda b, g, i: (b, g, 0, 0))
    cols = lambda n: pl.BlockSpec((None, None, VT_ROWS, n), lambda b, g, i: (b, g, 0, 0))
    return pl.pallas_call(
        _attn_kernel,
        grid=(B, N_KV_GROUPS, S // Q_BLOCK),
        in_specs=[pl.BlockSpec((None, gw, Q_BLOCK), lambda b, g, i: (b, g, i)),
                  pl.BlockSpec((None, GATE_PAD, Q_BLOCK), lambda b, g, i: (b, g, i)),
                  pl.BlockSpec((None, None, n_slc, Q_BLOCK), lambda b, g, i: (b, g, 0, i)),
                  pl.BlockSpec((None, None, HPG, HEAD_DIM, Q_BLOCK), lambda b, g, i: (b, g, 0, 0, i)),
                  rows(S, MXU_DEPTH), cols(S), rows(S, HEAD_DIM), cols(S),
                  pl.BlockSpec((gw, 1), lambda b, g, i: (g, 0))],
        out_specs=pl.BlockSpec((None, Q_BLOCK, gw), lambda b, g, i: (b, i, g)),
        out_shape=jax.ShapeDtypeStruct((B, S, ATTN_WIDTH), BF16),
        scratch_shapes=[pltpu.VMEM((KEY_CHUNK, QW), F32), pltpu.VMEM((KEY_CHUNK, QW), F32),
                        pltpu.VMEM((KEY_CHUNK, QW), BF16), pltpu.VMEM((KEY_CHUNK, QW), BF16),
                        pltpu.VMEM((WIN_KEYS, QW), F32), pltpu.VMEM((WIN_KEYS, QW), BF16),
                        pltpu.VMEM((MXU_DEPTH, QW), BF16), pltpu.VMEM((VT_ROWS, QW), F32)],
        compiler_params=pltpu.CompilerParams(
            dimension_semantics=("parallel", "parallel", "arbitrary"),
            vmem_limit_bytes=VMEM_LIMIT),
        name="nsa_attention",
    )(qT, gatesT, sel_bias, o_cmp, ksa, vsT, kw, vwT, g_attn_col)


def _out_kernel(x_ref, yc_ref, ya_ref, mod_ref, wo_ref, g_ref, wg_ref, wu_ref, wd_ref, gf_ref,
                o_ref):
    mix = _dot(yc_ref[...], wo_ref[0:CONV_CH, :]) + _dot(ya_ref[...], wo_ref[CONV_CH:, :])
    x = x_ref[...] + mod_ref[5:6, :] * mix
    x = _ffn_core(x, mod_ref[6:7, :], mod_ref[7:8, :], mod_ref[8:9, :], g_ref[...], wg_ref,
                  wu_ref, wd_ref)
    o_ref[...] = _rms(x, gf_ref[...])


def _out_call(x, yc, ya, mod, wo, g, wg, wu, wd, gf):
    B, S, D = x.shape
    tm = FFN_TOKEN_TILE
    tok = lambda w: pl.BlockSpec((None, tm, w), lambda b, i: (b, i, 0))
    return pl.pallas_call(
        _out_kernel,
        grid=(B, S // tm),
        in_specs=[tok(D), tok(CONV_CH), tok(ATTN_WIDTH),
                  pl.BlockSpec((None, N_MOD, D), lambda b, i: (b, 0, 0)),
                  _const_spec(wo.shape), _const_spec(g.shape), _const_spec(wg.shape),
                  _const_spec(wu.shape), _const_spec(wd.shape), _const_spec(gf.shape)],
        out_specs=tok(D),
        out_shape=jax.ShapeDtypeStruct((B, S, D), F32),
        compiler_params=pltpu.CompilerParams(dimension_semantics=("parallel", "parallel"),
                                             vmem_limit_bytes=VMEM_LIMIT),
        name="outproj_ffn2",
    )(x, yc, ya, mod, wo, g, wg, wu, wd, gf)


def kernel(x, c, positions, w_ada, b_ada, g_ffn1, w1_gate, w1_up, w1_down, g_mix, w_in, conv_w, cmp_pos_k, cmp_pos_v, w_cmpk1, w_cmpk2, w_cmpv1, w_cmpv2, g_out_conv, g_out_attn, w_out, g_ffn2, w2_gate, w2_up, w2_down, g_final):
    B, S, D = x.shape
    depth = w_ada.shape[0]
    n_slc = S // SLC_BLOCK
    half = CMP_BLOCK // 2
    n_half = S // half
    assert n_slc <= LANES, "selection-block one-hot is one lane tile wide"

    c_pad = jnp.pad(c, ((0, 8 - B), (0, 0)))
    row = lambda a: a.reshape(1, -1)

    freq_half = jnp.power(ROPE_THETA, -2.0 * jnp.arange(ROT_HALF, dtype=F32) / ROT_DIM)
    freq = jnp.tile(freq_half, LANES // ROT_HALF).reshape(1, LANES)
    gidx = np.arange(CONV_CH) // (CONV_CH // CONV_GROUPS)
    gmat = jnp.asarray((gidx[:, None] == gidx[None, :]) / (CONV_CH // CONV_GROUPS), dtype=BF16)
    c0 = np.arange(n_half) * CMP_STRIDE
    s0 = np.arange(LANES) * SLC_BLOCK
    ovlT = ((c0[None, :] <= s0[:, None] + SLC_BLOCK - 1) & (c0[None, :] + CMP_BLOCK - 1 >= s0[:, None]))
    ovlT = jnp.asarray(ovlT, dtype=BF16)
    onehot = jnp.asarray((np.arange(S) // SLC_BLOCK)[:, None] == np.arange(LANES)[None, :], dtype=BF16)
    cos_p, sin_p = _rope_table_call(positions.reshape(-1, TOKENS_PER_ROW), freq)
    cos_p = cos_p.reshape(B, S // TOKENS_PER_ROW, LANES)
    sin_p = sin_p.reshape(B, S // TOKENS_PER_ROW, LANES)

    for l in range(depth):
        mod = _ada_call(c_pad, w_ada[l], row(b_ada[l]))[:B].reshape(B, N_MOD, D)

        x = _ffn_call(x, mod, row(g_ffn1[l]), w1_gate[l].astype(BF16), w1_up[l].astype(BF16),
                      w1_down[l].astype(BF16))

        n_main = w_in.shape[2] - N_KV_GROUPS * 3 * HPG
        gate_cols = [jnp.pad(w_in[l][:, n_main + g * 3 * HPG:n_main + (g + 1) * 3 * HPG],
                             ((0, 0), (0, GATE_PAD - 3 * HPG))) for g in range(N_KV_GROUPS)]
        wgate = jnp.concatenate(gate_cols, axis=1).astype(BF16)
        win = w_in[l][:, :n_main].astype(BF16)
        (yc, qT, kc, vc, ksa, vsT, kw, vwT, gatesT) = _inproj_call(
            x, cos_p, sin_p, mod, row(g_mix[l]), win, wgate, conv_w[l], row(g_out_conv[l]), gmat,
            onehot)

        both_groups = lambda pe: jnp.tile(pe, (1, N_KV_GROUPS))
        kcc, vcT = _compress_call(kc, vc, both_groups(cmp_pos_k[l]), both_groups(cmp_pos_v[l]),
                                  w_cmpk1[l], w_cmpk2[l], w_cmpv1[l], w_cmpv2[l])

        sel_bias, o_cmp = _select_call(min(SLC_TOP_N, n_slc), qT, kcc, vcT, ovlT)
        ya = _attn_call(qT, gatesT, sel_bias, o_cmp, ksa, vsT, kw, vwT,
                        g_out_attn[l].reshape(ATTN_WIDTH, 1))

        assert l == depth - 1, "final norm is fused into the last layer's output kernel"
        x = _out_call(x, yc, ya, mod, w_out[l].astype(BF16), row(g_ffn2[l]),
                      w2_gate[l].astype(BF16), w2_up[l].astype(BF16), w2_down[l].astype(BF16),
                      row(g_final))
    return x
```

```python
import functools
import math

import numpy as np
import jax
import jax.numpy as jnp
from jax import lax
from jax.experimental import pallas as pl
from jax.experimental.pallas import tpu as pltpu

F32 = jnp.float32
BF16 = jnp.bfloat16

CONV_CH = 512
CONV_GROUPS = 8
N_HEADS = 8
N_KV_GROUPS = 2
HPG = N_HEADS // N_KV_GROUPS
HEAD_DIM = 64
ATTN_WIDTH = N_HEADS * HEAD_DIM
KV_WIDTH = N_KV_GROUPS * HEAD_DIM
ROPE_THETA = 500000.0
ROT_DIM = HEAD_DIM // 4
ROT_HALF = ROT_DIM // 2
CMP_BLOCK = 32
CMP_STRIDE = 16
CMP_HIDDEN = 256
SLC_BLOCK = 64
SLC_TOP_N = 16
WINDOW = 512
Q_BLOCK = 256
MACARON_W = 0.5
N_MOD = 9
EPS = 1e-6
NEG = -1e30
BIG = 1e9

LANES = 128
MXU_DEPTH = 256
VMEM_LIMIT = 56 * 1024 * 1024

TOKEN_TILE = 512
FFN_TOKEN_TILE = 512
ADA_COL_TILE = 1024
FF_TILE = 256
KEY_CHUNK = 512
WIN_KEYS = WINDOW + Q_BLOCK
TOKENS_PER_ROW = LANES // ROT_HALF
GATE_PAD = LANES
QW = HPG * Q_BLOCK
SEL_BLOCK = 1024
SEL_ROW_STEP = 32
ONES_ROWS = 16
VT_ROWS = HEAD_DIM + ONES_ROWS
ROW_TILE = 16
Q_SCALE = HEAD_DIM ** -0.5 * math.log2(math.e)
M_FLOOR = -1e20


def _dot(a, b):
    return jnp.dot(a, b, preferred_element_type=F32)


def _rms(x, g):
    return x * lax.rsqrt(jnp.mean(x * x, axis=-1, keepdims=True) + EPS) * g


def _split_bf16(x):
    hi = x.astype(BF16)
    lo = (x - hi.astype(F32)).astype(BF16)
    return hi, lo


def _const_spec(shape):
    nd = len(shape)
    return pl.BlockSpec(shape, lambda *_: (0,) * nd, pipeline_mode=pl.Buffered(1))


def _ada_kernel(c_ref, w_ref, b_ref, o_ref):
    c = c_ref[...]
    c_act = c * jax.nn.sigmoid(c)
    o_ref[...] = _dot(c_act.astype(BF16), w_ref[...].astype(BF16)) + b_ref[...]


def _ada_call(c_pad, w_ada, b_ada):
    rows, d = c_pad.shape
    n = w_ada.shape[1]
    tn = ADA_COL_TILE
    return pl.pallas_call(
        _ada_kernel,
        grid=(n // tn,),
        in_specs=[pl.BlockSpec((rows, d), lambda j: (0, 0)),
                  pl.BlockSpec((d, tn), lambda j: (0, j)),
                  pl.BlockSpec((1, tn), lambda j: (0, j))],
        out_specs=pl.BlockSpec((rows, tn), lambda j: (0, j)),
        out_shape=jax.ShapeDtypeStruct((rows, n), F32),
        compiler_params=pltpu.CompilerParams(dimension_semantics=("arbitrary",),
                                             vmem_limit_bytes=VMEM_LIMIT),
        name="adaln_mod",
    )(c_pad, w_ada, b_ada)


def _ffn_core(x, shift, scale, gate, g, wg_ref, wu_ref, wd_ref, a_ref):
    h = _rms(x, g) * (1.0 + scale) + shift
    hb = h.astype(BF16)
    d_ff = wg_ref.shape[1]
    for j in range(d_ff // FF_TILE):
        sl = slice(j * FF_TILE, (j + 1) * FF_TILE)
        gg = _dot(hb, wg_ref[:, sl])
        uu = _dot(hb, wu_ref[:, sl])
        a_ref[:, sl] = (gg * jax.nn.sigmoid(gg) * uu).astype(a_ref.dtype)
    return x + (MACARON_W * gate) * _dot(a_ref[...], wd_ref[...])


def _ffn_kernel(x_ref, mod_ref, g_ref, wg_ref, wu_ref, wd_ref, o_ref, a_ref):
    o_ref[...] = _ffn_core(x_ref[...], mod_ref[0:1, :], mod_ref[1:2, :], mod_ref[2:3, :],
                           g_ref[...], wg_ref, wu_ref, wd_ref, a_ref)


def _ffn_call(x, mod, g, wg, wu, wd):
    B, S, D = x.shape
    tm = FFN_TOKEN_TILE
    return pl.pallas_call(
        _ffn_kernel,
        grid=(B, S // tm),
        in_specs=[pl.BlockSpec((None, tm, D), lambda b, i: (b, i, 0)),
                  pl.BlockSpec((None, N_MOD, D), lambda b, i: (b, 0, 0)),
                  _const_spec(g.shape), _const_spec(wg.shape), _const_spec(wu.shape),
                  _const_spec(wd.shape)],
        out_specs=pl.BlockSpec((None, tm, D), lambda b, i: (b, i, 0)),
        out_shape=jax.ShapeDtypeStruct((B, S, D), F32),
        scratch_shapes=[pltpu.VMEM((tm, wg.shape[1]), BF16)],
        compiler_params=pltpu.CompilerParams(dimension_semantics=("parallel", "parallel"),
                                             vmem_limit_bytes=VMEM_LIMIT),
        name="ffn1",
    )(x, mod, g, wg, wu, wd)


def _rope_table_kernel(pos_ref, freq_ref, cos_ref, sin_ref):
    pos = jnp.concatenate([pos_ref[...].astype(F32),
                           jnp.zeros((pos_ref.shape[0], LANES - TOKENS_PER_ROW), F32)], axis=1)
    src = lax.broadcasted_iota(jnp.int32, pos.shape, 1) // ROT_HALF
    ang = jnp.take_along_axis(pos, src, axis=1) * freq_ref[...]
    cos_ref[...] = jnp.cos(ang)
    sin_ref[...] = jnp.sin(ang)


def _rope_table_call(pos_rows, freq):
    rows = pos_rows.shape[0]
    shape = jax.ShapeDtypeStruct((rows, LANES), F32)
    full = pl.BlockSpec((rows, LANES), lambda: (0, 0))
    return pl.pallas_call(
        _rope_table_kernel,
        in_specs=[pl.BlockSpec(pos_rows.shape, lambda: (0, 0)), pl.BlockSpec(freq.shape, lambda: (0, 0))],
        out_specs=(full, full),
        out_shape=(shape, shape),
        compiler_params=pltpu.CompilerParams(vmem_limit_bytes=VMEM_LIMIT),
        name="rope_table",
    )(pos_rows, freq)


def _inproj_kernel(x_ref, cosp_ref, sinp_ref, mod_ref, gmix_ref, win_ref, wgate_ref, convw_ref, gconv_ref,
                   gmat_ref, onehot_ref, yc_ref, qT_ref, kc_ref, vc_ref, ksa_ref, vsT_ref, kw_ref,
                   vwT_ref, gT_ref, carry_ref):
    tm = x_ref.shape[0]

    @pl.when(pl.program_id(1) == 0)
    def _():
        carry_ref[...] = jnp.zeros_like(carry_ref)

    x = x_ref[...]
    h = _rms(x, gmix_ref[...]) * (1.0 + mod_ref[4:5, :]) + mod_ref[3:4, :]
    hb = h.astype(BF16)

    def proj(c0, width):
        return _dot(hb, win_ref[:, c0:c0 + width])

    d = lax.broadcasted_iota(jnp.int32, (tm, LANES), 1) & (HEAD_DIM - 1)
    token = lax.broadcasted_iota(jnp.int32, (tm, LANES), 0) & (TOKENS_PER_ROW - 1)
    src_lane = token * ROT_HALF + (d & (ROT_HALF - 1))

    def unpack(packed_ref):
        rows = jnp.broadcast_to(packed_ref[...][:, None, :], (tm // TOKENS_PER_ROW, TOKENS_PER_ROW, LANES))
        return jnp.take_along_axis(rows.reshape(tm, LANES), src_lane, axis=1)

    cos_t = jnp.where(d < ROT_DIM, unpack(cosp_ref), 1.0)
    sin_raw = unpack(sinp_ref)
    sin_t = jnp.where(d < ROT_HALF, -sin_raw, jnp.where(d < ROT_DIM, sin_raw, 0.0))
    first_half = d < ROT_HALF

    def rope(t):
        outs = []
        for j in range(t.shape[1] // LANES):
            tj = t[:, j * LANES:(j + 1) * LANES]
            partner = jnp.where(first_half, pltpu.roll(tj, LANES - ROT_HALF, 1),
                                pltpu.roll(tj, ROT_HALF, 1))
            outs.append(tj * cos_t + partner * sin_t)
        return outs[0] if len(outs) == 1 else jnp.concatenate(outs, axis=1)

    c0 = 3 * CONV_CH
    q = rope(proj(c0, ATTN_WIDTH)) * Q_SCALE
    qT_ref[...] = q.T.astype(qT_ref.dtype)
    c0 += ATTN_WIDTH
    kv = proj(c0, 2 * KV_WIDTH)
    kc_ref[...] = rope(kv[:, :KV_WIDTH])
    vc_ref[...] = kv[:, KV_WIDTH:]
    kv = proj(c0 + 2 * KV_WIDTH, 2 * KV_WIDTH)
    ks = rope(kv[:, :KV_WIDTH]).astype(BF16)
    vsT = kv[:, KV_WIDTH:].T.astype(BF16)
    kv = proj(c0 + 4 * KV_WIDTH, 2 * KV_WIDTH)
    kw = rope(kv[:, :KV_WIDTH]).astype(BF16)
    vwT = kv[:, KV_WIDTH:].T.astype(BF16)
    pad = jnp.zeros((tm, MXU_DEPTH - LANES - HEAD_DIM), BF16)
    ones = jnp.ones((ONES_ROWS, tm), BF16)
    for g in range(N_KV_GROUPS):
        kg = ks[:, g * HEAD_DIM:(g + 1) * HEAD_DIM]
        ksa_ref[g] = jnp.concatenate([onehot_ref[...], kg, pad], axis=1)
        kw_ref[g] = kw[:, g * HEAD_DIM:(g + 1) * HEAD_DIM]
        vsT_ref[g] = jnp.concatenate([vsT[g * HEAD_DIM:(g + 1) * HEAD_DIM, :], ones], axis=0)
        vwT_ref[g] = jnp.concatenate([vwT[g * HEAD_DIM:(g + 1) * HEAD_DIM, :], ones], axis=0)
    gT_ref[...] = jax.nn.sigmoid(_dot(hb, wgate_ref[...])).T

    cb = proj(0, CONV_CH)
    u = proj(CONV_CH, CONV_CH) * proj(2 * CONV_CH, CONV_CH)
    row = lax.broadcasted_iota(jnp.int32, (tm, 1), 0)
    prev1 = carry_ref[7:8, :]
    prev2 = carry_ref[6:7, :]
    u1 = jnp.where(row >= 1, pltpu.roll(u, 1, 0), prev1)
    u2 = jnp.where(row >= 2, pltpu.roll(u, 2, 0), jnp.where(row == 1, prev1, prev2))
    carry_ref[...] = u[tm - 8:tm, :]
    v = convw_ref[0:1, :] * u2 + convw_ref[1:2, :] * u1 + convw_ref[2:3, :] * u
    y = cb * v
    hi, lo = _split_bf16(y * y)
    ms = _dot(hi, gmat_ref[...]) + _dot(lo, gmat_ref[...])
    yc_ref[...] = (y * lax.rsqrt(ms + EPS) * gconv_ref[...]).astype(yc_ref.dtype)


def _inproj_call(x, cos_p, sin_p, mod, gmix, win, wgate, convw, gconv, gmat, onehot):
    B, S, D = x.shape
    tm = TOKEN_TILE
    tok = lambda w: pl.BlockSpec((None, tm, w), lambda b, i: (b, i, 0))
    tr = lambda w: pl.BlockSpec((None, w, tm), lambda b, i: (b, 0, i))
    grp = lambda w: pl.BlockSpec((None, N_KV_GROUPS, tm, w), lambda b, i: (b, 0, i, 0))
    grpT = pl.BlockSpec((None, N_KV_GROUPS, VT_ROWS, tm), lambda b, i: (b, 0, 0, i))
    packed = pl.BlockSpec((None, tm // TOKENS_PER_ROW, LANES), lambda b, i: (b, i, 0))
    out_shapes = (
        jax.ShapeDtypeStruct((B, S, CONV_CH), BF16),
        jax.ShapeDtypeStruct((B, ATTN_WIDTH, S), BF16),
        jax.ShapeDtypeStruct((B, S, KV_WIDTH), F32),
        jax.ShapeDtypeStruct((B, S, KV_WIDTH), F32),
        jax.ShapeDtypeStruct((B, N_KV_GROUPS, S, MXU_DEPTH), BF16),
        jax.ShapeDtypeStruct((B, N_KV_GROUPS, VT_ROWS, S), BF16),
        jax.ShapeDtypeStruct((B, N_KV_GROUPS, S, HEAD_DIM), BF16),
        jax.ShapeDtypeStruct((B, N_KV_GROUPS, VT_ROWS, S), BF16),
        jax.ShapeDtypeStruct((B, N_KV_GROUPS * GATE_PAD, S), F32),
    )
    return pl.pallas_call(
        _inproj_kernel,
        grid=(B, S // tm),
        in_specs=[tok(D),
                  packed, packed,
                  pl.BlockSpec((None, N_MOD, D), lambda b, i: (b, 0, 0)),
                  _const_spec(gmix.shape), _const_spec(win.shape), _const_spec(wgate.shape),
                  _const_spec(convw.shape),
                  _const_spec(gconv.shape), _const_spec(gmat.shape),
                  pl.BlockSpec((tm, onehot.shape[1]), lambda b, i: (i, 0))],
        out_specs=(tok(CONV_CH), tr(ATTN_WIDTH), tok(KV_WIDTH), tok(KV_WIDTH), grp(MXU_DEPTH),
                   grpT, grp(HEAD_DIM), grpT, tr(N_KV_GROUPS * GATE_PAD)),
        out_shape=out_shapes,
        scratch_shapes=[pltpu.VMEM((8, CONV_CH), F32)],
        compiler_params=pltpu.CompilerParams(dimension_semantics=("arbitrary", "arbitrary"),
                                             vmem_limit_bytes=VMEM_LIMIT),
        name="mixer_inproj",
    )(x, cos_p, sin_p, mod, gmix, win, wgate, convw, gconv, gmat, onehot)


def _compress_kernel(kf_ref, vf_ref, pek_ref, pev_ref, wk1_ref, wk2_ref, wv1_ref, wv2_ref,
                     kcc_ref, vcT_ref):
    half = CMP_BLOCK // 2
    n = kf_ref.shape[0] // half

    def mlp(x_ref, pe_ref, w1_ref, w2_ref):
        parts = []
        for p in range(2):
            acc = None
            for l0 in range(0, half, 2):
                xs, ws = [], []
                for l in (l0, l0 + 1):
                    row = p * half + l
                    xs.append((x_ref[pl.ds(l, n, stride=half), :]
                               + pe_ref[row:row + 1, :]).astype(BF16))
                    w = w1_ref[row * HEAD_DIM:(row + 1) * HEAD_DIM, :].astype(BF16)
                    z = jnp.zeros_like(w)
                    ws.append(jnp.concatenate([jnp.concatenate([w, z], axis=1),
                                               jnp.concatenate([z, w], axis=1)], axis=0))
                d = _dot(jnp.concatenate(xs, axis=1), jnp.concatenate(ws, axis=0))
                acc = d if acc is None else acc + d
            parts.append(acc)
        hpre = parts[0] + pltpu.roll(parts[1], n - 1, 0)
        hid = (hpre * jax.nn.sigmoid(hpre)).astype(BF16)
        w2 = w2_ref[...].astype(BF16)
        return jnp.concatenate([_dot(hid[:, g * CMP_HIDDEN:(g + 1) * CMP_HIDDEN], w2)
                                for g in range(N_KV_GROUPS)], axis=1)

    kc = mlp(kf_ref, pek_ref, wk1_ref, wk2_ref).astype(kcc_ref.dtype)
    for g in range(N_KV_GROUPS):
        kcc_ref[g] = kc[:, g * HEAD_DIM:(g + 1) * HEAD_DIM]
    vcT = mlp(vf_ref, pev_ref, wv1_ref, wv2_ref).T.astype(vcT_ref.dtype)
    ones = jnp.ones((ONES_ROWS, n), vcT_ref.dtype)
    for g in range(N_KV_GROUPS):
        vcT_ref[g] = jnp.concatenate([vcT[g * HEAD_DIM:(g + 1) * HEAD_DIM, :], ones], axis=0)


def _compress_call(kf, vf, pek, pev, wk1, wk2, wv1, wv2):
    B, S, width = kf.shape
    n = S // (CMP_BLOCK // 2)
    flat = pl.BlockSpec((None, S, width), lambda b: (b, 0, 0))
    return pl.pallas_call(
        _compress_kernel,
        grid=(B,),
        in_specs=[flat, flat, _const_spec(pek.shape), _const_spec(pev.shape),
                  _const_spec(wk1.shape), _const_spec(wk2.shape), _const_spec(wv1.shape),
                  _const_spec(wv2.shape)],
        out_specs=(pl.BlockSpec((None, N_KV_GROUPS, n, HEAD_DIM), lambda b: (b, 0, 0, 0)),
                   pl.BlockSpec((None, N_KV_GROUPS, VT_ROWS, n), lambda b: (b, 0, 0, 0))),
        out_shape=(jax.ShapeDtypeStruct((B, N_KV_GROUPS, n, HEAD_DIM), BF16),
                   jax.ShapeDtypeStruct((B, N_KV_GROUPS, VT_ROWS, n), BF16)),
        compiler_params=pltpu.CompilerParams(dimension_semantics=("arbitrary",),
                                             vmem_limit_bytes=VMEM_LIMIT),
        name="kv_compress",
    )(kf, vf, pek, pev, wk1, wk2, wv1, wv2)


def _lane_tiles(x, n):
    return jnp.concatenate([x] * n, axis=1)


def _col_max(s_ref, n_rows, bias_fn):
    groups = ROW_TILE // 8
    mx = [jnp.full((8, s_ref.shape[1]), NEG, F32)] * groups
    for r in range(0, n_rows, ROW_TILE):
        x = s_ref[r:r + ROW_TILE, :]
        if bias_fn is not None:
            x = x + bias_fn(r)
            s_ref[r:r + ROW_TILE, :] = x
        mx = [jnp.maximum(mx[i], x[8 * i:8 * (i + 1), :]) for i in range(groups)]
    while len(mx) > 1:
        mx = [jnp.maximum(a, b) for a, b in zip(mx[0::2], mx[1::2])]
    return jnp.max(mx[0], axis=0, keepdims=True)


def _col_exp2(s_ref, p_ref, n_rows, m, keep_f32=False):
    for r in range(0, n_rows, ROW_TILE):
        p = jnp.exp2(s_ref[r:r + ROW_TILE, :] - m)
        if keep_f32:
            s_ref[r:r + ROW_TILE, :] = p
        p_ref[r:r + ROW_TILE, :] = p.astype(p_ref.dtype)


def _recip_pos(l):
    return 1.0 / jnp.where(l > 0.0, l, 1.0)


def _select_kernel(top_n, qT_ref, kcc_ref, vcT_ref, ovlT_ref, sel_ref, ocmp_ref, sc_ref, pc_ref,
                   ph_ref, pl_ref):
    QB = SEL_BLOCK
    n_cmp = kcc_ref.shape[0]
    n_slc = ovlT_ref.shape[0]
    t0 = pl.program_id(2) * QB
    tq = t0 + lax.broadcasted_iota(jnp.int32, (1, QB), 1)
    rows = lax.broadcasted_iota(jnp.int32, (ROW_TILE, 1), 0)

    qT = qT_ref[...]
    qcat = jnp.concatenate([qT[h * HEAD_DIM:(h + 1) * HEAD_DIM, :] for h in range(HPG)], axis=1)

    def cmp_bias(r):
        cmp_end = (r + rows) * CMP_STRIDE + (CMP_BLOCK - 1)
        return _lane_tiles(jnp.where(cmp_end <= tq, 0.0, NEG), HPG)

    cur = tq // SLC_BLOCK

    def tree(op, xs):
        while len(xs) > 1:
            xs = [op(*xs[i:i + 2]) if i + 1 < len(xs) else xs[i] for i in range(0, len(xs), 2)]
        return xs[0]

    def causal_variant(n_rows):
        rows_cmp = min(n_cmp, n_rows * (SLC_BLOCK // CMP_STRIDE))

        def run():
            sc_ref[0:rows_cmp, :] = _dot(kcc_ref[0:rows_cmp, :], qcat)
            m = jnp.maximum(_col_max(sc_ref, rows_cmp, cmp_bias), M_FLOOR)
            _col_exp2(sc_ref, pc_ref, rows_cmp, m, keep_f32=True)
            o_cmp = _dot(vcT_ref[:, 0:rows_cmp], pc_ref[0:rows_cmp, :])
            rl = _recip_pos(o_cmp[HEAD_DIM:HEAD_DIM + 1, :])
            o_cmp = o_cmp[0:HEAD_DIM, :] * rl
            for h in range(HPG):
                ocmp_ref[h] = o_cmp[:, h * QB:(h + 1) * QB]

            for r in range(0, rows_cmp, ROW_TILE):
                pn = sc_ref[r:r + ROW_TILE, :] * rl
                psum = pn[:, 0:QB]
                for h in range(1, HPG):
                    psum = psum + pn[:, h * QB:(h + 1) * QB]
                hi, lo = _split_bf16(psum)
                ph_ref[r:r + ROW_TILE, :] = hi
                pl_ref[r:r + ROW_TILE, :] = lo
            ovl = ovlT_ref[0:n_rows, 0:rows_cmp]
            imp = _dot(ovl, ph_ref[0:rows_cmp, :]) + _dot(ovl, pl_ref[0:rows_cmp, :])

            blk = lax.broadcasted_iota(jnp.int32, (n_rows, 1), 0)
            future = blk > cur
            forced = (blk == 0) | (blk == cur) | (blk == cur - 1)
            score0 = jnp.where(future, -BIG, jnp.where(forced, BIG, imp))
            score = [score0[8 * i:8 * (i + 1), :] for i in range(n_rows // 8)]
            cum = jnp.zeros((1, QB), F32)
            thr = jnp.zeros((1, QB), F32)
            above = jnp.zeros((1, QB), F32)
            for _ in range(top_n):
                best = jnp.max(tree(jnp.maximum, score), axis=0, keepdims=True)
                eq = [s == best for s in score]
                unfilled = cum < top_n
                thr = jnp.where(unfilled, best, thr)
                above = jnp.where(unfilled, cum, above)
                cum = cum + jnp.sum(tree(jnp.add, [jnp.where(e, 1.0, 0.0) for e in eq]),
                                    axis=0, keepdims=True)
                score = [jnp.where(e, -jnp.inf, s) for e, s in zip(eq, score)]
            ties = score0 == thr
            lower = jnp.where(lax.broadcasted_iota(jnp.int32, (1, n_rows), 1) < blk, 1.0, 0.0)
            rank = _dot(lower.astype(BF16), jnp.where(ties, 1.0, 0.0).astype(BF16))
            picked = (score0 > thr) | (ties & (rank < top_n - above))
            sel_ref[0:n_rows, :] = jnp.where(future, NEG, jnp.where(picked, 0.0, NEG)).astype(sel_ref.dtype)
            if n_rows < n_slc:
                sel_ref[n_rows:n_slc, :] = jnp.full((n_slc - n_rows, QB), NEG, sel_ref.dtype)
        return run

    steps = n_slc // SEL_ROW_STEP
    need = jnp.minimum((t0 + QB - 1) // (SLC_BLOCK * SEL_ROW_STEP), steps - 1)
    lax.switch(need, [causal_variant(SEL_ROW_STEP * (k + 1)) for k in range(steps)])


def _select_call(top_n, qT, kcc, vcT, ovlT):
    B, _, S = qT.shape
    n_cmp = kcc.shape[2]
    n_slc = ovlT.shape[0]
    gw = HPG * HEAD_DIM
    qw = HPG * SEL_BLOCK
    return pl.pallas_call(
        functools.partial(_select_kernel, top_n),
        grid=(B, N_KV_GROUPS, S // SEL_BLOCK),
        in_specs=[pl.BlockSpec((None, gw, SEL_BLOCK), lambda b, g, i: (b, g, i)),
                  pl.BlockSpec((None, None, n_cmp, HEAD_DIM), lambda b, g, i: (b, g, 0, 0)),
                  pl.BlockSpec((None, None, VT_ROWS, n_cmp), lambda b, g, i: (b, g, 0, 0)),
                  pl.BlockSpec(ovlT.shape, lambda b, g, i: (0, 0))],
        out_specs=(pl.BlockSpec((None, None, n_slc, SEL_BLOCK), lambda b, g, i: (b, g, 0, i)),
                   pl.BlockSpec((None, None, HPG, HEAD_DIM, SEL_BLOCK),
                                lambda b, g, i: (b, g, 0, 0, i))),
        out_shape=(jax.ShapeDtypeStruct((B, N_KV_GROUPS, n_slc, S), BF16),
                   jax.ShapeDtypeStruct((B, N_KV_GROUPS, HPG, HEAD_DIM, S), F32)),
        scratch_shapes=[pltpu.VMEM((n_cmp, qw), F32), pltpu.VMEM((n_cmp, qw), BF16),
                        pltpu.VMEM((n_cmp, SEL_BLOCK), BF16), pltpu.VMEM((n_cmp, SEL_BLOCK), BF16)],
        compiler_params=pltpu.CompilerParams(
            dimension_semantics=("parallel", "parallel", "arbitrary"),
            vmem_limit_bytes=VMEM_LIMIT),
        name="nsa_select",
    )(qT, kcc, vcT, ovlT)


def _attn_kernel(qT_ref, gT_ref, sel_ref, ocmp_ref, ksa_ref, vsT_ref, kw_ref, vwT_ref, ga_ref,
                 o_ref, s0_ref, s1_ref, p0_ref, p1_ref, sw_ref, pw_ref, rhs_ref, acc_ref):
    QB = Q_BLOCK
    n_slc = sel_ref.shape[0]
    t0 = pl.program_id(2) * QB
    tq = t0 + lax.broadcasted_iota(jnp.int32, (1, QB), 1)
    rows = lax.broadcasted_iota(jnp.int32, (ROW_TILE, 1), 0)

    qT = qT_ref[...]
    qcat = jnp.concatenate([qT[h * HEAD_DIM:(h + 1) * HEAD_DIM, :] for h in range(HPG)], axis=1)

    rhs_ref[...] = jnp.concatenate([_lane_tiles(sel_ref[...], HPG), qcat,
                                    jnp.zeros((MXU_DEPTH - n_slc - HEAD_DIM, QW), BF16)], axis=0)

    w0 = pl.multiple_of(jnp.maximum(t0 - WINDOW, 0), QB)
    sw_ref[...] = _dot(kw_ref[pl.ds(w0, WIN_KEYS), :], qcat)

    def win_bias(r):
        dist = tq - (w0 + r + rows)
        return _lane_tiles(jnp.where((dist >= 0) & (dist < WINDOW), 0.0, NEG), HPG)

    KC = KEY_CHUNK
    last = t0 // KC

    def scores(c, dst_ref):
        k0 = pl.multiple_of(jnp.minimum(c, last) * KC, KC)
        dst_ref[...] = _dot(ksa_ref[pl.ds(k0, KC), :], rhs_ref[...])

    def weighted_values(c, p_ref):
        k0 = pl.multiple_of(jnp.clip(c, 0, last) * KC, KC)
        return _dot(vsT_ref[:, pl.ds(k0, KC)], p_ref[...])

    def softmax(c, src_ref, p_ref, m, pending, causal):
        def causal_bias(r):
            return _lane_tiles(jnp.where(c * KC + r + rows <= tq, 0.0, NEG), HPG)

        m_new = jnp.maximum(m, _col_max(src_ref, KC, causal_bias if causal else None))
        acc_ref[...] = jnp.exp2(m - m_new) * (acc_ref[...] + pending)
        _col_exp2(src_ref, p_ref, KC, m_new)
        return m_new

    def pair(i, m):
        pending = weighted_values(2 * i - 1, p1_ref)
        scores(2 * i + 1, s1_ref)
        m = softmax(2 * i, s0_ref, p0_ref, m, pending, False)
        pending = weighted_values(2 * i, p0_ref)
        scores(2 * i + 2, s0_ref)
        return softmax(2 * i + 1, s1_ref, p1_ref, m, pending, True)

    p1_ref[...] = jnp.zeros_like(p1_ref)
    acc_ref[...] = jnp.zeros_like(acc_ref)
    scores(0, s0_ref)

    m = _col_max(sw_ref, WIN_KEYS, win_bias)
    _col_exp2(sw_ref, pw_ref, WIN_KEYS, m)
    o_win = _dot(vwT_ref[:, pl.ds(w0, WIN_KEYS)], pw_ref[...])
    o_win = o_win[0:HEAD_DIM, :] * (1.0 / o_win[HEAD_DIM:HEAD_DIM + 1, :])

    n_pairs = (last + 1) // 2
    m = lax.fori_loop(0, n_pairs // 2, lambda j, mm: pair(2 * j + 1, pair(2 * j, mm)),
                      jnp.full((1, QW), NEG, F32))
    m = lax.cond(n_pairs % 2 == 1, lambda mm: pair(n_pairs - 1, mm), lambda mm: mm, m)
    pending = weighted_values(2 * n_pairs - 1, p1_ref)

    @pl.when(last % 2 == 0)
    def _():
        softmax(last, s0_ref, p0_ref, m, pending, True)
        acc_ref[...] += weighted_values(last, p0_ref)

    @pl.when(last % 2 == 1)
    def _():
        acc_ref[...] += pending

    acc = acc_ref[...]
    o_slc = acc[0:HEAD_DIM, :] * (1.0 / acc[HEAD_DIM:HEAD_DIM + 1, :])

    gts = gT_ref[...]
    outs = []
    for h in range(HPG):
        sl = slice(h * QB, (h + 1) * QB)
        o = (gts[3 * h:3 * h + 1, :] * ocmp_ref[h] + gts[3 * h + 1:3 * h + 2, :] * o_slc[:, sl]
             + gts[3 * h + 2:3 * h + 3, :] * o_win[:, sl])
        o = o * lax.rsqrt(jnp.mean(o * o, axis=0, keepdims=True) + EPS)
        outs.append((o * ga_ref[h * HEAD_DIM:(h + 1) * HEAD_DIM, :]).T)
    o_ref[...] = jnp.concatenate(outs, axis=1).astype(o_ref.dtype)


def _attn_call(qT, gatesT, sel_bias, o_cmp, ksa, vsT, kw, vwT, g_attn_col):
    B, _, S = qT.shape
    n_slc = sel_bias.shape[2]
    gw = HPG * HEAD_DIM
    rows = lambda n, w: pl.BlockSpec((None, None, n, w), lambda b, g, i: (b, g, 0, 0))
    cols = lambda n: pl.BlockSpec((None, None, VT_ROWS, n), lambda b, g, i: (b, g, 0, 0))
    return pl.pallas_call(
        _attn_kernel,
        grid=(B, N_KV_GROUPS, S // Q_BLOCK),
        in_specs=[pl.BlockSpec((None, gw, Q_BLOCK), lambda b, g, i: (b, g, i)),
                  pl.BlockSpec((None, GATE_PAD, Q_BLOCK), lambda b, g, i: (b, g, i)),
                  pl.BlockSpec((None, None, n_slc, Q_BLOCK), lambda b, g, i: (b, g, 0, i)),
                  pl.BlockSpec((None, None, HPG, HEAD_DIM, Q_BLOCK), lambda b, g, i: (b, g, 0, 0, i)),
                  rows(S, MXU_DEPTH), cols(S), rows(S, HEAD_DIM), cols(S),
                  pl.BlockSpec((gw, 1), lambda b, g, i: (g, 0))],
        out_specs=pl.BlockSpec((None, Q_BLOCK, gw), lambda b, g, i: (b, i, g)),
        out_shape=jax.ShapeDtypeStruct((B, S, ATTN_WIDTH), BF16),
        scratch_shapes=[pltpu.VMEM((KEY_CHUNK, QW), F32), pltpu.VMEM((KEY_CHUNK, QW), F32),
                        pltpu.VMEM((KEY_CHUNK, QW), BF16), pltpu.VMEM((KEY_CHUNK, QW), BF16),
                        pltpu.VMEM((WIN_KEYS, QW), F32), pltpu.VMEM((WIN_KEYS, QW), BF16),
                        pltpu.VMEM((MXU_DEPTH, QW), BF16), pltpu.VMEM((VT_ROWS, QW), F32)],
        compiler_params=pltpu.CompilerParams(
            dimension_semantics=("parallel", "parallel", "arbitrary"),
            vmem_limit_bytes=VMEM_LIMIT),
        name="nsa_attention",
    )(qT, gatesT, sel_bias, o_cmp, ksa, vsT, kw, vwT, g_attn_col)


def _out_kernel(x_ref, yc_ref, ya_ref, mod_ref, wo_ref, g_ref, wg_ref, wu_ref, wd_ref, gf_ref,
                o_ref, a_ref):
    mix = _dot(yc_ref[...], wo_ref[0:CONV_CH, :]) + _dot(ya_ref[...], wo_ref[CONV_CH:, :])
    x = x_ref[...] + mod_ref[5:6, :] * mix
    x = _ffn_core(x, mod_ref[6:7, :], mod_ref[7:8, :], mod_ref[8:9, :], g_ref[...], wg_ref,
                  wu_ref, wd_ref, a_ref)
    o_ref[...] = _rms(x, gf_ref[...])


def _out_call(x, yc, ya, mod, wo, g, wg, wu, wd, gf):
    B, S, D = x.shape
    tm = FFN_TOKEN_TILE
    tok = lambda w: pl.BlockSpec((None, tm, w), lambda b, i: (b, i, 0))
    return pl.pallas_call(
        _out_kernel,
        grid=(B, S // tm),
        in_specs=[tok(D), tok(CONV_CH), tok(ATTN_WIDTH),
                  pl.BlockSpec((None, N_MOD, D), lambda b, i: (b, 0, 0)),
                  _const_spec(wo.shape), _const_spec(g.shape), _const_spec(wg.shape),
                  _const_spec(wu.shape), _const_spec(wd.shape), _const_spec(gf.shape)],
        out_specs=tok(D),
        out_shape=jax.ShapeDtypeStruct((B, S, D), F32),
        scratch_shapes=[pltpu.VMEM((tm, wg.shape[1]), BF16)],
        compiler_params=pltpu.CompilerParams(dimension_semantics=("parallel", "parallel"),
                                             vmem_limit_bytes=VMEM_LIMIT),
        name="outproj_ffn2",
    )(x, yc, ya, mod, wo, g, wg, wu, wd, gf)


def kernel(x, c, positions, w_ada, b_ada, g_ffn1, w1_gate, w1_up, w1_down, g_mix, w_in, conv_w, cmp_pos_k, cmp_pos_v, w_cmpk1, w_cmpk2, w_cmpv1, w_cmpv2, g_out_conv, g_out_attn, w_out, g_ffn2, w2_gate, w2_up, w2_down, g_final):
    B, S, D = x.shape
    depth = w_ada.shape[0]
    n_slc = S // SLC_BLOCK
    half = CMP_BLOCK // 2
    n_half = S // half
    assert n_slc <= LANES, "selection-block one-hot is one lane tile wide"

    c_pad = jnp.pad(c, ((0, 8 - B), (0, 0)))
    row = lambda a: a.reshape(1, -1)

    freq_half = jnp.power(ROPE_THETA, -2.0 * jnp.arange(ROT_HALF, dtype=F32) / ROT_DIM)
    freq = jnp.tile(freq_half, LANES // ROT_HALF).reshape(1, LANES)
    gidx = np.arange(CONV_CH) // (CONV_CH // CONV_GROUPS)
    gmat = jnp.asarray((gidx[:, None] == gidx[None, :]) / (CONV_CH // CONV_GROUPS), dtype=BF16)
    c0 = np.arange(n_half) * CMP_STRIDE
    s0 = np.arange(LANES) * SLC_BLOCK
    ovlT = ((c0[None, :] <= s0[:, None] + SLC_BLOCK - 1) & (c0[None, :] + CMP_BLOCK - 1 >= s0[:, None]))
    ovlT = jnp.asarray(ovlT, dtype=BF16)
    onehot = jnp.asarray((np.arange(S) // SLC_BLOCK)[:, None] == np.arange(LANES)[None, :], dtype=BF16)
    cos_p, sin_p = _rope_table_call(positions.reshape(-1, TOKENS_PER_ROW), freq)
    cos_p = cos_p.reshape(B, S // TOKENS_PER_ROW, LANES)
    sin_p = sin_p.reshape(B, S // TOKENS_PER_ROW, LANES)

    for l in range(depth):
        mod = _ada_call(c_pad, w_ada[l], row(b_ada[l]))[:B].reshape(B, N_MOD, D)

        x = _ffn_call(x, mod, row(g_ffn1[l]), w1_gate[l].astype(BF16), w1_up[l].astype(BF16),
                      w1_down[l].astype(BF16))

        n_main = w_in.shape[2] - N_KV_GROUPS * 3 * HPG
        gate_cols = [jnp.pad(w_in[l][:, n_main + g * 3 * HPG:n_main + (g + 1) * 3 * HPG],
                             ((0, 0), (0, GATE_PAD - 3 * HPG))) for g in range(N_KV_GROUPS)]
        wgate = jnp.concatenate(gate_cols, axis=1).astype(BF16)
        win = w_in[l][:, :n_main].astype(BF16)
        (yc, qT, kc, vc, ksa, vsT, kw, vwT, gatesT) = _inproj_call(
            x, cos_p, sin_p, mod, row(g_mix[l]), win, wgate, conv_w[l], row(g_out_conv[l]), gmat,
            onehot)

        both_groups = lambda pe: jnp.tile(pe, (1, N_KV_GROUPS))
        kcc, vcT = _compress_call(kc, vc, both_groups(cmp_pos_k[l]), both_groups(cmp_pos_v[l]),
                                  w_cmpk1[l], w_cmpk2[l], w_cmpv1[l], w_cmpv2[l])

        sel_bias, o_cmp = _select_call(min(SLC_TOP_N, n_slc), qT, kcc, vcT, ovlT)
        ya = _attn_call(qT, gatesT, sel_bias, o_cmp, ksa, vsT, kw, vwT,
                        g_out_attn[l].reshape(ATTN_WIDTH, 1))

        assert l == depth - 1, "final norm is fused into the last layer's output kernel"
        x = _out_call(x, yc, ya, mod, w_out[l].astype(BF16), row(g_ffn2[l]),
                      w2_gate[l].astype(BF16), w2_up[l].astype(BF16), w2_down[l].astype(BF16),
                      row(g_final))
    return x
```

```python
import functools
import math

import numpy as np
import jax
import jax.numpy as jnp
from jax import lax
from jax.experimental import pallas as pl
from jax.experimental.pallas import tpu as pltpu

F32 = jnp.float32
BF16 = jnp.bfloat16

CONV_CH = 512
CONV_GROUPS = 8
N_HEADS = 8
N_KV_GROUPS = 2
HPG = N_HEADS // N_KV_GROUPS
HEAD_DIM = 64
ATTN_WIDTH = N_HEADS * HEAD_DIM
KV_WIDTH = N_KV_GROUPS * HEAD_DIM
ROPE_THETA = 500000.0
ROT_DIM = HEAD_DIM // 4
ROT_HALF = ROT_DIM // 2
CMP_BLOCK = 32
CMP_STRIDE = 16
CMP_HIDDEN = 256
SLC_BLOCK = 64
SLC_TOP_N = 16
MAX_FORCED = 3
WINDOW = 512
Q_BLOCK = 256
MACARON_W = 0.5
N_MOD = 9
EPS = 1e-6
NEG = -1e30
BIG = 1e9

LANES = 128
MXU_DEPTH = 256
VMEM_LIMIT = 56 * 1024 * 1024

TOKEN_TILE = 512
FFN_TOKEN_TILE = 512
ADA_COL_TILE = 1024
FF_TILE = 256
KEY_CHUNK = 512
WIN_KEYS = WINDOW + Q_BLOCK
TOKENS_PER_ROW = LANES // ROT_HALF
GATE_PAD = LANES
QW = HPG * Q_BLOCK
SEL_BLOCK = 1024
SEL_ROW_STEP = 32
ONES_ROWS = 16
VT_ROWS = HEAD_DIM + ONES_ROWS
ROW_TILE = 16
Q_SCALE = HEAD_DIM ** -0.5 * math.log2(math.e)
M_FLOOR = -1e20


def _dot(a, b):
    return jnp.dot(a, b, preferred_element_type=F32)


def _rms(x, g):
    return x * lax.rsqrt(jnp.mean(x * x, axis=-1, keepdims=True) + EPS) * g


def _split_bf16(x):
    hi = x.astype(BF16)
    lo = (x - hi.astype(F32)).astype(BF16)
    return hi, lo


def _const_spec(shape):
    nd = len(shape)
    return pl.BlockSpec(shape, lambda *_: (0,) * nd, pipeline_mode=pl.Buffered(1))


def _ada_kernel(c_ref, w_ref, b_ref, o_ref):
    c = c_ref[...]
    c_act = c * jax.nn.sigmoid(c)
    o_ref[...] = _dot(c_act.astype(BF16), w_ref[...].astype(BF16)) + b_ref[...]


def _ada_call(c_pad, w_ada, b_ada):
    rows, d = c_pad.shape
    n = w_ada.shape[1]
    tn = ADA_COL_TILE
    return pl.pallas_call(
        _ada_kernel,
        grid=(n // tn,),
        in_specs=[pl.BlockSpec((rows, d), lambda j: (0, 0)),
                  pl.BlockSpec((d, tn), lambda j: (0, j)),
                  pl.BlockSpec((1, tn), lambda j: (0, j))],
        out_specs=pl.BlockSpec((rows, tn), lambda j: (0, j)),
        out_shape=jax.ShapeDtypeStruct((rows, n), F32),
        compiler_params=pltpu.CompilerParams(dimension_semantics=("arbitrary",),
                                             vmem_limit_bytes=VMEM_LIMIT),
        name="adaln_mod",
    )(c_pad, w_ada, b_ada)


def _ffn_core(x, shift, scale, gate, g, wg_ref, wu_ref, wd_ref, a_ref):
    h = _rms(x, g) * (1.0 + scale) + shift
    hb = h.astype(BF16)
    d_ff = wg_ref.shape[1]
    for j in range(d_ff // FF_TILE):
        sl = slice(j * FF_TILE, (j + 1) * FF_TILE)
        gg = _dot(hb, wg_ref[:, sl])
        uu = _dot(hb, wu_ref[:, sl])
        a_ref[:, sl] = (gg * jax.nn.sigmoid(gg) * uu).astype(a_ref.dtype)
    return x + (MACARON_W * gate) * _dot(a_ref[...], wd_ref[...])


def _ffn_kernel(x_ref, mod_ref, g_ref, wg_ref, wu_ref, wd_ref, o_ref, a_ref):
    o_ref[...] = _ffn_core(x_ref[...], mod_ref[0:1, :], mod_ref[1:2, :], mod_ref[2:3, :],
                           g_ref[...], wg_ref, wu_ref, wd_ref, a_ref)


def _ffn_call(x, mod, g, wg, wu, wd):
    B, S, D = x.shape
    tm = FFN_TOKEN_TILE
    return pl.pallas_call(
        _ffn_kernel,
        grid=(B, S // tm),
        in_specs=[pl.BlockSpec((None, tm, D), lambda b, i: (b, i, 0)),
                  pl.BlockSpec((None, N_MOD, D), lambda b, i: (b, 0, 0)),
                  _const_spec(g.shape), _const_spec(wg.shape), _const_spec(wu.shape),
                  _const_spec(wd.shape)],
        out_specs=pl.BlockSpec((None, tm, D), lambda b, i: (b, i, 0)),
        out_shape=jax.ShapeDtypeStruct((B, S, D), F32),
        scratch_shapes=[pltpu.VMEM((tm, wg.shape[1]), BF16)],
        compiler_params=pltpu.CompilerParams(dimension_semantics=("parallel", "parallel"),
                                             vmem_limit_bytes=VMEM_LIMIT),
        name="ffn1",
    )(x, mod, g, wg, wu, wd)


def _rope_table_kernel(pos_ref, freq_ref, cos_ref, sin_ref):
    pos = jnp.concatenate([pos_ref[...].astype(F32),
                           jnp.zeros((pos_ref.shape[0], LANES - TOKENS_PER_ROW), F32)], axis=1)
    src = lax.broadcasted_iota(jnp.int32, pos.shape, 1) // ROT_HALF
    ang = jnp.take_along_axis(pos, src, axis=1) * freq_ref[...]
    cos_ref[...] = jnp.cos(ang)
    sin_ref[...] = jnp.sin(ang)


def _rope_table_call(pos_rows, freq):
    rows = pos_rows.shape[0]
    shape = jax.ShapeDtypeStruct((rows, LANES), F32)
    full = pl.BlockSpec((rows, LANES), lambda: (0, 0))
    return pl.pallas_call(
        _rope_table_kernel,
        in_specs=[pl.BlockSpec(pos_rows.shape, lambda: (0, 0)), pl.BlockSpec(freq.shape, lambda: (0, 0))],
        out_specs=(full, full),
        out_shape=(shape, shape),
        compiler_params=pltpu.CompilerParams(vmem_limit_bytes=VMEM_LIMIT),
        name="rope_table",
    )(pos_rows, freq)


def _inproj_kernel(x_ref, cosp_ref, sinp_ref, mod_ref, gmix_ref, win_ref, wgate_ref, convw_ref, gconv_ref,
                   gmat_ref, onehot_ref, yc_ref, qT_ref, kc_ref, vc_ref, ksa_ref, vsT_ref, kw_ref,
                   vwT_ref, gT_ref, carry_ref):
    tm = x_ref.shape[0]

    @pl.when(pl.program_id(1) == 0)
    def _():
        carry_ref[...] = jnp.zeros_like(carry_ref)

    x = x_ref[...]
    h = _rms(x, gmix_ref[...]) * (1.0 + mod_ref[4:5, :]) + mod_ref[3:4, :]
    hb = h.astype(BF16)

    def proj(c0, width):
        return _dot(hb, win_ref[:, c0:c0 + width])

    d = lax.broadcasted_iota(jnp.int32, (tm, LANES), 1) & (HEAD_DIM - 1)
    token = lax.broadcasted_iota(jnp.int32, (tm, LANES), 0) & (TOKENS_PER_ROW - 1)
    src_lane = token * ROT_HALF + (d & (ROT_HALF - 1))

    def unpack(packed_ref):
        rows = jnp.broadcast_to(packed_ref[...][:, None, :], (tm // TOKENS_PER_ROW, TOKENS_PER_ROW, LANES))
        return jnp.take_along_axis(rows.reshape(tm, LANES), src_lane, axis=1)

    cos_t = jnp.where(d < ROT_DIM, unpack(cosp_ref), 1.0)
    sin_raw = unpack(sinp_ref)
    sin_t = jnp.where(d < ROT_HALF, -sin_raw, jnp.where(d < ROT_DIM, sin_raw, 0.0))
    first_half = d < ROT_HALF

    def rope(t):
        outs = []
        for j in range(t.shape[1] // LANES):
            tj = t[:, j * LANES:(j + 1) * LANES]
            partner = jnp.where(first_half, pltpu.roll(tj, LANES - ROT_HALF, 1),
                                pltpu.roll(tj, ROT_HALF, 1))
            outs.append(tj * cos_t + partner * sin_t)
        return outs[0] if len(outs) == 1 else jnp.concatenate(outs, axis=1)

    c0 = 3 * CONV_CH
    q = rope(proj(c0, ATTN_WIDTH)) * Q_SCALE
    qT_ref[...] = q.T.astype(qT_ref.dtype)
    c0 += ATTN_WIDTH
    kv = proj(c0, 2 * KV_WIDTH)
    kc_ref[...] = rope(kv[:, :KV_WIDTH])
    vc_ref[...] = kv[:, KV_WIDTH:]
    kv = proj(c0 + 2 * KV_WIDTH, 2 * KV_WIDTH)
    ks = rope(kv[:, :KV_WIDTH]).astype(BF16)
    vsT = kv[:, KV_WIDTH:].T.astype(BF16)
    kv = proj(c0 + 4 * KV_WIDTH, 2 * KV_WIDTH)
    kw = rope(kv[:, :KV_WIDTH]).astype(BF16)
    vwT = kv[:, KV_WIDTH:].T.astype(BF16)
    pad = jnp.zeros((tm, MXU_DEPTH - LANES - HEAD_DIM), BF16)
    ones = jnp.ones((ONES_ROWS, tm), BF16)
    for g in range(N_KV_GROUPS):
        kg = ks[:, g * HEAD_DIM:(g + 1) * HEAD_DIM]
        ksa_ref[g] = jnp.concatenate([onehot_ref[...], kg, pad], axis=1)
        kw_ref[g] = kw[:, g * HEAD_DIM:(g + 1) * HEAD_DIM]
        vsT_ref[g] = jnp.concatenate([vsT[g * HEAD_DIM:(g + 1) * HEAD_DIM, :], ones], axis=0)
        vwT_ref[g] = jnp.concatenate([vwT[g * HEAD_DIM:(g + 1) * HEAD_DIM, :], ones], axis=0)
    gT_ref[...] = jax.nn.sigmoid(_dot(hb, wgate_ref[...])).T

    cb = proj(0, CONV_CH)
    u = proj(CONV_CH, CONV_CH) * proj(2 * CONV_CH, CONV_CH)
    row = lax.broadcasted_iota(jnp.int32, (tm, 1), 0)
    prev1 = carry_ref[7:8, :]
    prev2 = carry_ref[6:7, :]
    u1 = jnp.where(row >= 1, pltpu.roll(u, 1, 0), prev1)
    u2 = jnp.where(row >= 2, pltpu.roll(u, 2, 0), jnp.where(row == 1, prev1, prev2))
    carry_ref[...] = u[tm - 8:tm, :]
    v = convw_ref[0:1, :] * u2 + convw_ref[1:2, :] * u1 + convw_ref[2:3, :] * u
    y = cb * v
    hi, lo = _split_bf16(y * y)
    ms = _dot(hi, gmat_ref[...]) + _dot(lo, gmat_ref[...])
    yc_ref[...] = (y * lax.rsqrt(ms + EPS) * gconv_ref[...]).astype(yc_ref.dtype)


def _inproj_call(x, cos_p, sin_p, mod, gmix, win, wgate, convw, gconv, gmat, onehot):
    B, S, D = x.shape
    tm = TOKEN_TILE
    tok = lambda w: pl.BlockSpec((None, tm, w), lambda b, i: (b, i, 0))
    tr = lambda w: pl.BlockSpec((None, w, tm), lambda b, i: (b, 0, i))
    grp = lambda w: pl.BlockSpec((None, N_KV_GROUPS, tm, w), lambda b, i: (b, 0, i, 0))
    grpT = pl.BlockSpec((None, N_KV_GROUPS, VT_ROWS, tm), lambda b, i: (b, 0, 0, i))
    packed = pl.BlockSpec((None, tm // TOKENS_PER_ROW, LANES), lambda b, i: (b, i, 0))
    out_shapes = (
        jax.ShapeDtypeStruct((B, S, CONV_CH), BF16),
        jax.ShapeDtypeStruct((B, ATTN_WIDTH, S), BF16),
        jax.ShapeDtypeStruct((B, S, KV_WIDTH), F32),
        jax.ShapeDtypeStruct((B, S, KV_WIDTH), F32),
        jax.ShapeDtypeStruct((B, N_KV_GROUPS, S, MXU_DEPTH), BF16),
        jax.ShapeDtypeStruct((B, N_KV_GROUPS, VT_ROWS, S), BF16),
        jax.ShapeDtypeStruct((B, N_KV_GROUPS, S, HEAD_DIM), BF16),
        jax.ShapeDtypeStruct((B, N_KV_GROUPS, VT_ROWS, S), BF16),
        jax.ShapeDtypeStruct((B, N_KV_GROUPS * GATE_PAD, S), F32),
    )
    return pl.pallas_call(
        _inproj_kernel,
        grid=(B, S // tm),
        in_specs=[tok(D),
                  packed, packed,
                  pl.BlockSpec((None, N_MOD, D), lambda b, i: (b, 0, 0)),
                  _const_spec(gmix.shape), _const_spec(win.shape), _const_spec(wgate.shape),
                  _const_spec(convw.shape),
                  _const_spec(gconv.shape), _const_spec(gmat.shape),
                  pl.BlockSpec((tm, onehot.shape[1]), lambda b, i: (i, 0))],
        out_specs=(tok(CONV_CH), tr(ATTN_WIDTH), tok(KV_WIDTH), tok(KV_WIDTH), grp(MXU_DEPTH),
                   grpT, grp(HEAD_DIM), grpT, tr(N_KV_GROUPS * GATE_PAD)),
        out_shape=out_shapes,
        scratch_shapes=[pltpu.VMEM((8, CONV_CH), F32)],
        compiler_params=pltpu.CompilerParams(dimension_semantics=("arbitrary", "arbitrary"),
                                             vmem_limit_bytes=VMEM_LIMIT),
        name="mixer_inproj",
    )(x, cos_p, sin_p, mod, gmix, win, wgate, convw, gconv, gmat, onehot)


def _compress_kernel(kf_ref, vf_ref, pek_ref, pev_ref, wk1_ref, wk2_ref, wv1_ref, wv2_ref,
                     kcc_ref, vcT_ref):
    half = CMP_BLOCK // 2
    n = kf_ref.shape[0] // half

    def mlp(x_ref, pe_ref, w1_ref, w2_ref):
        parts = []
        for p in range(2):
            acc = None
            for l0 in range(0, half, 2):
                xs, ws = [], []
                for l in (l0, l0 + 1):
                    row = p * half + l
                    xs.append((x_ref[pl.ds(l, n, stride=half), :]
                               + pe_ref[row:row + 1, :]).astype(BF16))
                    w = w1_ref[row * HEAD_DIM:(row + 1) * HEAD_DIM, :].astype(BF16)
                    z = jnp.zeros_like(w)
                    ws.append(jnp.concatenate([jnp.concatenate([w, z], axis=1),
                                               jnp.concatenate([z, w], axis=1)], axis=0))
                d = _dot(jnp.concatenate(xs, axis=1), jnp.concatenate(ws, axis=0))
                acc = d if acc is None else acc + d
            parts.append(acc)
        hpre = parts[0] + pltpu.roll(parts[1], n - 1, 0)
        hid = (hpre * jax.nn.sigmoid(hpre)).astype(BF16)
        w2 = w2_ref[...].astype(BF16)
        return jnp.concatenate([_dot(hid[:, g * CMP_HIDDEN:(g + 1) * CMP_HIDDEN], w2)
                                for g in range(N_KV_GROUPS)], axis=1)

    kc = mlp(kf_ref, pek_ref, wk1_ref, wk2_ref).astype(kcc_ref.dtype)
    for g in range(N_KV_GROUPS):
        kcc_ref[g] = kc[:, g * HEAD_DIM:(g + 1) * HEAD_DIM]
    vcT = mlp(vf_ref, pev_ref, wv1_ref, wv2_ref).T.astype(vcT_ref.dtype)
    ones = jnp.ones((ONES_ROWS, n), vcT_ref.dtype)
    for g in range(N_KV_GROUPS):
        vcT_ref[g] = jnp.concatenate([vcT[g * HEAD_DIM:(g + 1) * HEAD_DIM, :], ones], axis=0)


def _compress_call(kf, vf, pek, pev, wk1, wk2, wv1, wv2):
    B, S, width = kf.shape
    n = S // (CMP_BLOCK // 2)
    flat = pl.BlockSpec((None, S, width), lambda b: (b, 0, 0))
    return pl.pallas_call(
        _compress_kernel,
        grid=(B,),
        in_specs=[flat, flat, _const_spec(pek.shape), _const_spec(pev.shape),
                  _const_spec(wk1.shape), _const_spec(wk2.shape), _const_spec(wv1.shape),
                  _const_spec(wv2.shape)],
        out_specs=(pl.BlockSpec((None, N_KV_GROUPS, n, HEAD_DIM), lambda b: (b, 0, 0, 0)),
                   pl.BlockSpec((None, N_KV_GROUPS, VT_ROWS, n), lambda b: (b, 0, 0, 0))),
        out_shape=(jax.ShapeDtypeStruct((B, N_KV_GROUPS, n, HEAD_DIM), BF16),
                   jax.ShapeDtypeStruct((B, N_KV_GROUPS, VT_ROWS, n), BF16)),
        compiler_params=pltpu.CompilerParams(dimension_semantics=("arbitrary",),
                                             vmem_limit_bytes=VMEM_LIMIT),
        name="kv_compress",
    )(kf, vf, pek, pev, wk1, wk2, wv1, wv2)


def _lane_tiles(x, n):
    return jnp.concatenate([x] * n, axis=1)


def _col_max(s_ref, n_rows, bias_fn):
    groups = ROW_TILE // 8
    mx = [jnp.full((8, s_ref.shape[1]), NEG, F32)] * groups
    for r in range(0, n_rows, ROW_TILE):
        x = s_ref[r:r + ROW_TILE, :]
        if bias_fn is not None:
            x = x + bias_fn(r)
            s_ref[r:r + ROW_TILE, :] = x
        mx = [jnp.maximum(mx[i], x[8 * i:8 * (i + 1), :]) for i in range(groups)]
    while len(mx) > 1:
        mx = [jnp.maximum(a, b) for a, b in zip(mx[0::2], mx[1::2])]
    return jnp.max(mx[0], axis=0, keepdims=True)


def _col_exp2(s_ref, p_ref, n_rows, m, keep_f32=False):
    for r in range(0, n_rows, ROW_TILE):
        p = jnp.exp2(s_ref[r:r + ROW_TILE, :] - m)
        if keep_f32:
            s_ref[r:r + ROW_TILE, :] = p
        p_ref[r:r + ROW_TILE, :] = p.astype(p_ref.dtype)


def _recip_pos(l):
    return 1.0 / jnp.where(l > 0.0, l, 1.0)


def _select_kernel(top_n, qT_ref, kcc_ref, vcT_ref, ovlT_ref, sel_ref, ocmp_ref, sc_ref, pc_ref,
                   ph_ref, pl_ref):
    QB = SEL_BLOCK
    n_cmp = kcc_ref.shape[0]
    n_slc = ovlT_ref.shape[0]
    t0 = pl.program_id(2) * QB
    tq = t0 + lax.broadcasted_iota(jnp.int32, (1, QB), 1)
    rows = lax.broadcasted_iota(jnp.int32, (ROW_TILE, 1), 0)

    qT = qT_ref[...]
    qcat = jnp.concatenate([qT[h * HEAD_DIM:(h + 1) * HEAD_DIM, :] for h in range(HPG)], axis=1)

    def cmp_bias(r):
        cmp_end = (r + rows) * CMP_STRIDE + (CMP_BLOCK - 1)
        return _lane_tiles(jnp.where(cmp_end <= tq, 0.0, NEG), HPG)

    cur = tq // SLC_BLOCK

    def tree(op, xs):
        while len(xs) > 1:
            xs = [op(*xs[i:i + 2]) if i + 1 < len(xs) else xs[i] for i in range(0, len(xs), 2)]
        return xs[0]

    def causal_variant(n_rows):
        rows_cmp = min(n_cmp, n_rows * (SLC_BLOCK // CMP_STRIDE))

        def run():
            sc_ref[0:rows_cmp, :] = _dot(kcc_ref[0:rows_cmp, :], qcat)
            m = jnp.maximum(_col_max(sc_ref, rows_cmp, cmp_bias), M_FLOOR)
            _col_exp2(sc_ref, pc_ref, rows_cmp, m, keep_f32=True)
            o_cmp = _dot(vcT_ref[:, 0:rows_cmp], pc_ref[0:rows_cmp, :])
            rl = _recip_pos(o_cmp[HEAD_DIM:HEAD_DIM + 1, :])
            o_cmp = o_cmp[0:HEAD_DIM, :] * rl
            for h in range(HPG):
                ocmp_ref[h] = o_cmp[:, h * QB:(h + 1) * QB]

            for r in range(0, rows_cmp, ROW_TILE):
                pn = sc_ref[r:r + ROW_TILE, :] * rl
                psum = pn[:, 0:QB]
                for h in range(1, HPG):
                    psum = psum + pn[:, h * QB:(h + 1) * QB]
                hi, lo = _split_bf16(psum)
                ph_ref[r:r + ROW_TILE, :] = hi
                pl_ref[r:r + ROW_TILE, :] = lo
            ovl = ovlT_ref[0:n_rows, 0:rows_cmp]
            imp = _dot(ovl, ph_ref[0:rows_cmp, :]) + _dot(ovl, pl_ref[0:rows_cmp, :])

            blk = lax.broadcasted_iota(jnp.int32, (n_rows, 1), 0)
            future = blk > cur
            forced = (blk == 0) | (blk == cur) | (blk == cur - 1)
            groups = range(n_rows // 8)

            def store(picked):
                sel_ref[0:n_rows, :] = jnp.where(future, NEG, jnp.where(picked, 0.0, NEG)).astype(sel_ref.dtype)

            def with_ties():
                score0 = jnp.where(future, -BIG, jnp.where(forced, BIG, imp))
                score = [score0[8 * i:8 * (i + 1), :] for i in groups]
                cum = jnp.zeros((1, QB), F32)
                thr = jnp.zeros((1, QB), F32)
                above = jnp.zeros((1, QB), F32)
                for _ in range(top_n):
                    best = jnp.max(tree(jnp.maximum, score), axis=0, keepdims=True)
                    eq = [s == best for s in score]
                    unfilled = cum < top_n
                    thr = jnp.where(unfilled, best, thr)
                    above = jnp.where(unfilled, cum, above)
                    cum = cum + jnp.sum(tree(jnp.add, [jnp.where(e, 1.0, 0.0) for e in eq]),
                                        axis=0, keepdims=True)
                    score = [jnp.where(e, -jnp.inf, s) for e, s in zip(eq, score)]
                ties = score0 == thr
                lower = jnp.where(lax.broadcasted_iota(jnp.int32, (1, n_rows), 1) < blk, 1.0, 0.0)
                rank = _dot(lower.astype(BF16), jnp.where(ties, 1.0, 0.0).astype(BF16))
                store((score0 > thr) | (ties & (rank < top_n - above)))

            if top_n > MAX_FORCED:
                n_forced = 1.0 + jnp.where(cur >= 1, 1.0, 0.0) + jnp.where(cur >= 2, 1.0, 0.0)
                wanted = top_n - n_forced
                take_all = (cur.astype(F32) + 1.0 - n_forced) <= wanted
                cand = jnp.where(future | forced, -BIG, imp)
                score = [cand[8 * i:8 * (i + 1), :] for i in groups]
                for _ in range(top_n - MAX_FORCED):
                    cut = jnp.max(tree(jnp.maximum, score), axis=0, keepdims=True)
                    score = [jnp.where(s == cut, -jnp.inf, s) for s in score]
                above_cut = cand >= cut
                count = jnp.sum(jnp.where(above_cut, 1.0, 0.0), axis=0, keepdims=True)
                exact = jnp.min(jnp.where(take_all | (count == wanted), 1.0, 0.0)) > 0.5
                lax.cond(exact, lambda: store(forced | take_all | above_cut), with_ties)
            else:
                with_ties()
            if n_rows < n_slc:
                sel_ref[n_rows:n_slc, :] = jnp.full((n_slc - n_rows, QB), NEG, sel_ref.dtype)
        return run

    steps = n_slc // SEL_ROW_STEP
    need = jnp.minimum((t0 + QB - 1) // (SLC_BLOCK * SEL_ROW_STEP), steps - 1)
    lax.switch(need, [causal_variant(SEL_ROW_STEP * (k + 1)) for k in range(steps)])


def _select_call(top_n, qT, kcc, vcT, ovlT):
    B, _, S = qT.shape
    n_cmp = kcc.shape[2]
    n_slc = ovlT.shape[0]
    gw = HPG * HEAD_DIM
    qw = HPG * SEL_BLOCK
    return pl.pallas_call(
        functools.partial(_select_kernel, top_n),
        grid=(B, N_KV_GROUPS, S // SEL_BLOCK),
        in_specs=[pl.BlockSpec((None, gw, SEL_BLOCK), lambda b, g, i: (b, g, i)),
                  pl.BlockSpec((None, None, n_cmp, HEAD_DIM), lambda b, g, i: (b, g, 0, 0)),
                  pl.BlockSpec((None, None, VT_ROWS, n_cmp), lambda b, g, i: (b, g, 0, 0)),
                  pl.BlockSpec(ovlT.shape, lambda b, g, i: (0, 0))],
        out_specs=(pl.BlockSpec((None, None, n_slc, SEL_BLOCK), lambda b, g, i: (b, g, 0, i)),
                   pl.BlockSpec((None, None, HPG, HEAD_DIM, SEL_BLOCK),
                                lambda b, g, i: (b, g, 0, 0, i))),
        out_shape=(jax.ShapeDtypeStruct((B, N_KV_GROUPS, n_slc, S), BF16),
                   jax.ShapeDtypeStruct((B, N_KV_GROUPS, HPG, HEAD_DIM, S), F32)),
        scratch_shapes=[pltpu.VMEM((n_cmp, qw), F32), pltpu.VMEM((n_cmp, qw), BF16),
                        pltpu.VMEM((n_cmp, SEL_BLOCK), BF16), pltpu.VMEM((n_cmp, SEL_BLOCK), BF16)],
        compiler_params=pltpu.CompilerParams(
            dimension_semantics=("parallel", "parallel", "arbitrary"),
            vmem_limit_bytes=VMEM_LIMIT),
        name="nsa_select",
    )(qT, kcc, vcT, ovlT)


def _attn_kernel(qT_ref, gT_ref, sel_ref, ocmp_ref, ksa_ref, vsT_ref, kw_ref, vwT_ref, ga_ref,
                 o_ref, s0_ref, s1_ref, p0_ref, p1_ref, sw_ref, pw_ref, rhs_ref, acc_ref):
    QB = Q_BLOCK
    n_slc = sel_ref.shape[0]
    t0 = pl.program_id(2) * QB
    tq = t0 + lax.broadcasted_iota(jnp.int32, (1, QB), 1)
    rows = lax.broadcasted_iota(jnp.int32, (ROW_TILE, 1), 0)

    qT = qT_ref[...]
    qcat = jnp.concatenate([qT[h * HEAD_DIM:(h + 1) * HEAD_DIM, :] for h in range(HPG)], axis=1)

    rhs_ref[...] = jnp.concatenate([_lane_tiles(sel_ref[...], HPG), qcat,
                                    jnp.zeros((MXU_DEPTH - n_slc - HEAD_DIM, QW), BF16)], axis=0)

    w0 = pl.multiple_of(jnp.maximum(t0 - WINDOW, 0), QB)
    sw_ref[...] = _dot(kw_ref[pl.ds(w0, WIN_KEYS), :], qcat)

    def win_bias(r):
        dist = tq - (w0 + r + rows)
        return _lane_tiles(jnp.where((dist >= 0) & (dist < WINDOW), 0.0, NEG), HPG)

    KC = KEY_CHUNK
    last = t0 // KC

    def scores(c, dst_ref):
        k0 = pl.multiple_of(jnp.minimum(c, last) * KC, KC)
        dst_ref[...] = _dot(ksa_ref[pl.ds(k0, KC), :], rhs_ref[...])

    def weighted_values(c, p_ref):
        k0 = pl.multiple_of(jnp.clip(c, 0, last) * KC, KC)
        return _dot(vsT_ref[:, pl.ds(k0, KC)], p_ref[...])

    def softmax(c, src_ref, p_ref, m, pending, causal):
        def causal_bias(r):
            return _lane_tiles(jnp.where(c * KC + r + rows <= tq, 0.0, NEG), HPG)

        m_new = jnp.maximum(m, _col_max(src_ref, KC, causal_bias if causal else None))
        acc_ref[...] = jnp.exp2(m - m_new) * (acc_ref[...] + pending)
        _col_exp2(src_ref, p_ref, KC, m_new)
        return m_new

    def pair(i, m):
        pending = weighted_values(2 * i - 1, p1_ref)
        scores(2 * i + 1, s1_ref)
        m = softmax(2 * i, s0_ref, p0_ref, m, pending, False)
        pending = weighted_values(2 * i, p0_ref)
        scores(2 * i + 2, s0_ref)
        return softmax(2 * i + 1, s1_ref, p1_ref, m, pending, True)

    p1_ref[...] = jnp.zeros_like(p1_ref)
    acc_ref[...] = jnp.zeros_like(acc_ref)
    scores(0, s0_ref)

    m = _col_max(sw_ref, WIN_KEYS, win_bias)
    _col_exp2(sw_ref, pw_ref, WIN_KEYS, m)
    o_win = _dot(vwT_ref[:, pl.ds(w0, WIN_KEYS)], pw_ref[...])
    o_win = o_win[0:HEAD_DIM, :] * (1.0 / o_win[HEAD_DIM:HEAD_DIM + 1, :])

    n_pairs = (last + 1) // 2
    m = lax.fori_loop(0, n_pairs // 2, lambda j, mm: pair(2 * j + 1, pair(2 * j, mm)),
                      jnp.full((1, QW), NEG, F32))
    m = lax.cond(n_pairs % 2 == 1, lambda mm: pair(n_pairs - 1, mm), lambda mm: mm, m)
    pending = weighted_values(2 * n_pairs - 1, p1_ref)

    @pl.when(last % 2 == 0)
    def _():
        softmax(last, s0_ref, p0_ref, m, pending, True)
        acc_ref[...] += weighted_values(last, p0_ref)

    @pl.when(last % 2 == 1)
    def _():
        acc_ref[...] += pending

    acc = acc_ref[...]
    o_slc = acc[0:HEAD_DIM, :] * (1.0 / acc[HEAD_DIM:HEAD_DIM + 1, :])

    gts = gT_ref[...]
    outs = []
    for h in range(HPG):
        sl = slice(h * QB, (h + 1) * QB)
        o = (gts[3 * h:3 * h + 1, :] * ocmp_ref[h] + gts[3 * h + 1:3 * h + 2, :] * o_slc[:, sl]
             + gts[3 * h + 2:3 * h + 3, :] * o_win[:, sl])
        o = o * lax.rsqrt(jnp.mean(o * o, axis=0, keepdims=True) + EPS)
        outs.append((o * ga_ref[h * HEAD_DIM:(h + 1) * HEAD_DIM, :]).T)
    o_ref[...] = jnp.concatenate(outs, axis=1).astype(o_ref.dtype)


def _attn_call(qT, gatesT, sel_bias, o_cmp, ksa, vsT, kw, vwT, g_attn_col):
    B, _, S = qT.shape
    n_slc = sel_bias.shape[2]
    gw = HPG * HEAD_DIM
    rows = lambda n, w: pl.BlockSpec((None, None, n, w), lambda b, g, i: (b, g, 0, 0))
    cols = lambda n: pl.BlockSpec((None, None, VT_ROWS, n), lambda b, g, i: (b, g, 0, 0))
    return pl.pallas_call(
        _attn_kernel,
        grid=(B, N_KV_GROUPS, S // Q_BLOCK),
        in_specs=[pl.BlockSpec((None, gw, Q_BLOCK), lambda b, g, i: (b, g, i)),
                  pl.BlockSpec((None, GATE_PAD, Q_BLOCK), lambda b, g, i: (b, g, i)),
                  pl.BlockSpec((None, None, n_slc, Q_BLOCK), lambda b, g, i: (b, g, 0, i)),
                  pl.BlockSpec((None, None, HPG, HEAD_DIM, Q_BLOCK), lambda b, g, i: (b, g, 0, 0, i)),
                  rows(S, MXU_DEPTH), cols(S), rows(S, HEAD_DIM), cols(S),
                  pl.BlockSpec((gw, 1), lambda b, g, i: (g, 0))],
        out_specs=pl.BlockSpec((None, Q_BLOCK, gw), lambda b, g, i: (b, i, g)),
        out_shape=jax.ShapeDtypeStruct((B, S, ATTN_WIDTH), BF16),
        scratch_shapes=[pltpu.VMEM((KEY_CHUNK, QW), F32), pltpu.VMEM((KEY_CHUNK, QW), F32),
                        pltpu.VMEM((KEY_CHUNK, QW), BF16), pltpu.VMEM((KEY_CHUNK, QW), BF16),
                        pltpu.VMEM((WIN_KEYS, QW), F32), pltpu.VMEM((WIN_KEYS, QW), BF16),
                        pltpu.VMEM((MXU_DEPTH, QW), BF16), pltpu.VMEM((VT_ROWS, QW), F32)],
        compiler_params=pltpu.CompilerParams(
            dimension_semantics=("parallel", "parallel", "arbitrary"),
            vmem_limit_bytes=VMEM_LIMIT),
        name="nsa_attention",
    )(qT, gatesT, sel_bias, o_cmp, ksa, vsT, kw, vwT, g_attn_col)


def _out_kernel(x_ref, yc_ref, ya_ref, mod_ref, wo_ref, g_ref, wg_ref, wu_ref, wd_ref, gf_ref,
                o_ref, a_ref):
    mix = _dot(yc_ref[...], wo_ref[0:CONV_CH, :]) + _dot(ya_ref[...], wo_ref[CONV_CH:, :])
    x = x_ref[...] + mod_ref[5:6, :] * mix
    x = _ffn_core(x, mod_ref[6:7, :], mod_ref[7:8, :], mod_ref[8:9, :], g_ref[...], wg_ref,
                  wu_ref, wd_ref, a_ref)
    o_ref[...] = _rms(x, gf_ref[...])


def _out_call(x, yc, ya, mod, wo, g, wg, wu, wd, gf):
    B, S, D = x.shape
    tm = FFN_TOKEN_TILE
    tok = lambda w: pl.BlockSpec((None, tm, w), lambda b, i: (b, i, 0))
    return pl.pallas_call(
        _out_kernel,
        grid=(B, S // tm),
        in_specs=[tok(D), tok(CONV_CH), tok(ATTN_WIDTH),
                  pl.BlockSpec((None, N_MOD, D), lambda b, i: (b, 0, 0)),
                  _const_spec(wo.shape), _const_spec(g.shape), _const_spec(wg.shape),
                  _const_spec(wu.shape), _const_spec(wd.shape), _const_spec(gf.shape)],
        out_specs=tok(D),
        out_shape=jax.ShapeDtypeStruct((B, S, D), F32),
        scratch_shapes=[pltpu.VMEM((tm, wg.shape[1]), BF16)],
        compiler_params=pltpu.CompilerParams(dimension_semantics=("parallel", "parallel"),
                                             vmem_limit_bytes=VMEM_LIMIT),
        name="outproj_ffn2",
    )(x, yc, ya, mod, wo, g, wg, wu, wd, gf)


def kernel(x, c, positions, w_ada, b_ada, g_ffn1, w1_gate, w1_up, w1_down, g_mix, w_in, conv_w, cmp_pos_k, cmp_pos_v, w_cmpk1, w_cmpk2, w_cmpv1, w_cmpv2, g_out_conv, g_out_attn, w_out, g_ffn2, w2_gate, w2_up, w2_down, g_final):
    B, S, D = x.shape
    depth = w_ada.shape[0]
    n_slc = S // SLC_BLOCK
    half = CMP_BLOCK // 2
    n_half = S // half
    assert n_slc <= LANES, "selection-block one-hot is one lane tile wide"

    c_pad = jnp.pad(c, ((0, 8 - B), (0, 0)))
    row = lambda a: a.reshape(1, -1)

    freq_half = jnp.power(ROPE_THETA, -2.0 * jnp.arange(ROT_HALF, dtype=F32) / ROT_DIM)
    freq = jnp.tile(freq_half, LANES // ROT_HALF).reshape(1, LANES)
    gidx = np.arange(CONV_CH) // (CONV_CH // CONV_GROUPS)
    gmat = jnp.asarray((gidx[:, None] == gidx[None, :]) / (CONV_CH // CONV_GROUPS), dtype=BF16)
    c0 = np.arange(n_half) * CMP_STRIDE
    s0 = np.arange(LANES) * SLC_BLOCK
    ovlT = ((c0[None, :] <= s0[:, None] + SLC_BLOCK - 1) & (c0[None, :] + CMP_BLOCK - 1 >= s0[:, None]))
    ovlT = jnp.asarray(ovlT, dtype=BF16)
    onehot = jnp.asarray((np.arange(S) // SLC_BLOCK)[:, None] == np.arange(LANES)[None, :], dtype=BF16)
    cos_p, sin_p = _rope_table_call(positions.reshape(-1, TOKENS_PER_ROW), freq)
    cos_p = cos_p.reshape(B, S // TOKENS_PER_ROW, LANES)
    sin_p = sin_p.reshape(B, S // TOKENS_PER_ROW, LANES)

    for l in range(depth):
        mod = _ada_call(c_pad, w_ada[l], row(b_ada[l]))[:B].reshape(B, N_MOD, D)

        x = _ffn_call(x, mod, row(g_ffn1[l]), w1_gate[l].astype(BF16), w1_up[l].astype(BF16),
                      w1_down[l].astype(BF16))

        n_main = w_in.shape[2] - N_KV_GROUPS * 3 * HPG
        gate_cols = [jnp.pad(w_in[l][:, n_main + g * 3 * HPG:n_main + (g + 1) * 3 * HPG],
                             ((0, 0), (0, GATE_PAD - 3 * HPG))) for g in range(N_KV_GROUPS)]
        wgate = jnp.concatenate(gate_cols, axis=1).astype(BF16)
        win = w_in[l].astype(BF16)
        (yc, qT, kc, vc, ksa, vsT, kw, vwT, gatesT) = _inproj_call(
            x, cos_p, sin_p, mod, row(g_mix[l]), win, wgate, conv_w[l], row(g_out_conv[l]), gmat,
            onehot)

        both_groups = lambda pe: jnp.tile(pe, (1, N_KV_GROUPS))
        kcc, vcT = _compress_call(kc, vc, both_groups(cmp_pos_k[l]), both_groups(cmp_pos_v[l]),
                                  w_cmpk1[l], w_cmpk2[l], w_cmpv1[l], w_cmpv2[l])

        sel_bias, o_cmp = _select_call(min(SLC_TOP_N, n_slc), qT, kcc, vcT, ovlT)
        ya = _attn_call(qT, gatesT, sel_bias, o_cmp, ksa, vsT, kw, vwT,
                        g_out_attn[l].reshape(ATTN_WIDTH, 1))

        assert l == depth - 1, "final norm is fused into the last layer's output kernel"
        x = _out_call(x, yc, ya, mod, w_out[l].astype(BF16), row(g_ffn2[l]),
                      w2_gate[l].astype(BF16), w2_up[l].astype(BF16), w2_down[l].astype(BF16),
                      row(g_final))
    return x
```

```python
import functools
import math

import numpy as np
import jax
import jax.numpy as jnp
from jax import lax
from jax.experimental import pallas as pl
from jax.experimental.pallas import tpu as pltpu

F32 = jnp.float32
BF16 = jnp.bfloat16

CONV_CH = 512
CONV_GROUPS = 8
N_HEADS = 8
N_KV_GROUPS = 2
HPG = N_HEADS // N_KV_GROUPS
HEAD_DIM = 64
ATTN_WIDTH = N_HEADS * HEAD_DIM
KV_WIDTH = N_KV_GROUPS * HEAD_DIM
ROPE_THETA = 500000.0
ROT_DIM = HEAD_DIM // 4
ROT_HALF = ROT_DIM // 2
CMP_BLOCK = 32
CMP_STRIDE = 16
CMP_HIDDEN = 256
SLC_BLOCK = 64
SLC_TOP_N = 16
MAX_FORCED = 3
WINDOW = 512
Q_BLOCK = 256
MACARON_W = 0.5
N_MOD = 9
EPS = 1e-6
NEG = -1e30
BIG = 1e9

LANES = 128
MXU_DEPTH = 256
VMEM_LIMIT = 56 * 1024 * 1024

TOKEN_TILE = 512
FFN_TOKEN_TILE = 512
ADA_COL_TILE = 1024
FF_TILE = 256
KEY_CHUNK = 512
WIN_KEYS = WINDOW + Q_BLOCK
TOKENS_PER_ROW = LANES // ROT_HALF
GATE_PAD = LANES
QW = HPG * Q_BLOCK
SEL_BLOCK = 1024
SEL_ROW_STEP = 32
ONES_ROWS = 16
VT_ROWS = HEAD_DIM + ONES_ROWS
ROW_TILE = 16
Q_SCALE = HEAD_DIM ** -0.5 * math.log2(math.e)
M_FLOOR = -1e20


def _dot(a, b):
    return jnp.dot(a, b, preferred_element_type=F32)


def _rms(x, g):
    return x * lax.rsqrt(jnp.mean(x * x, axis=-1, keepdims=True) + EPS) * g


def _split_bf16(x):
    hi = x.astype(BF16)
    lo = (x - hi.astype(F32)).astype(BF16)
    return hi, lo


def _const_spec(shape):
    nd = len(shape)
    return pl.BlockSpec(shape, lambda *_: (0,) * nd, pipeline_mode=pl.Buffered(1))


def _ada_kernel(c_ref, w_ref, b_ref, o_ref):
    c = c_ref[...]
    c_act = c * jax.nn.sigmoid(c)
    o_ref[...] = _dot(c_act.astype(BF16), w_ref[...].astype(BF16)) + b_ref[...]


def _ada_call(c_pad, w_ada, b_ada):
    rows, d = c_pad.shape
    n = w_ada.shape[1]
    tn = ADA_COL_TILE
    return pl.pallas_call(
        _ada_kernel,
        grid=(n // tn,),
        in_specs=[pl.BlockSpec((rows, d), lambda j: (0, 0)),
                  pl.BlockSpec((d, tn), lambda j: (0, j)),
                  pl.BlockSpec((1, tn), lambda j: (0, j))],
        out_specs=pl.BlockSpec((rows, tn), lambda j: (0, j)),
        out_shape=jax.ShapeDtypeStruct((rows, n), F32),
        compiler_params=pltpu.CompilerParams(dimension_semantics=("arbitrary",),
                                             vmem_limit_bytes=VMEM_LIMIT),
        name="adaln_mod",
    )(c_pad, w_ada, b_ada)


def _ffn_core(x, shift, scale, gate, g, wg_ref, wu_ref, wd_ref, a_ref):
    h = _rms(x, g) * (1.0 + scale) + shift
    hb = h.astype(BF16)
    d_ff = wg_ref.shape[1]
    for j in range(d_ff // FF_TILE):
        sl = slice(j * FF_TILE, (j + 1) * FF_TILE)
        gg = _dot(hb, wg_ref[:, sl])
        uu = _dot(hb, wu_ref[:, sl])
        a_ref[:, sl] = (gg * jax.nn.sigmoid(gg) * uu).astype(a_ref.dtype)
    return x + (MACARON_W * gate) * _dot(a_ref[...], wd_ref[...])


def _ffn_kernel(x_ref, mod_ref, g_ref, wg_ref, wu_ref, wd_ref, o_ref, a_ref):
    o_ref[...] = _ffn_core(x_ref[...], mod_ref[0:1, :], mod_ref[1:2, :], mod_ref[2:3, :],
                           g_ref[...], wg_ref, wu_ref, wd_ref, a_ref)


def _ffn_call(x, mod, g, wg, wu, wd):
    B, S, D = x.shape
    tm = FFN_TOKEN_TILE
    return pl.pallas_call(
        _ffn_kernel,
        grid=(B, S // tm),
        in_specs=[pl.BlockSpec((None, tm, D), lambda b, i: (b, i, 0)),
                  pl.BlockSpec((None, N_MOD, D), lambda b, i: (b, 0, 0)),
                  _const_spec(g.shape), _const_spec(wg.shape), _const_spec(wu.shape),
                  _const_spec(wd.shape)],
        out_specs=pl.BlockSpec((None, tm, D), lambda b, i: (b, i, 0)),
        out_shape=jax.ShapeDtypeStruct((B, S, D), F32),
        scratch_shapes=[pltpu.VMEM((tm, wg.shape[1]), BF16)],
        compiler_params=pltpu.CompilerParams(dimension_semantics=("parallel", "parallel"),
                                             vmem_limit_bytes=VMEM_LIMIT),
        name="ffn1",
    )(x, mod, g, wg, wu, wd)


def _rope_table_kernel(pos_ref, freq_ref, cos_ref, sin_ref):
    pos = jnp.concatenate([pos_ref[...].astype(F32),
                           jnp.zeros((pos_ref.shape[0], LANES - TOKENS_PER_ROW), F32)], axis=1)
    src = lax.broadcasted_iota(jnp.int32, pos.shape, 1) // ROT_HALF
    ang = jnp.take_along_axis(pos, src, axis=1) * freq_ref[...]
    cos_ref[...] = jnp.cos(ang)
    sin_ref[...] = jnp.sin(ang)


def _rope_table_call(pos_rows, freq):
    rows = pos_rows.shape[0]
    shape = jax.ShapeDtypeStruct((rows, LANES), F32)
    full = pl.BlockSpec((rows, LANES), lambda: (0, 0))
    return pl.pallas_call(
        _rope_table_kernel,
        in_specs=[pl.BlockSpec(pos_rows.shape, lambda: (0, 0)), pl.BlockSpec(freq.shape, lambda: (0, 0))],
        out_specs=(full, full),
        out_shape=(shape, shape),
        compiler_params=pltpu.CompilerParams(vmem_limit_bytes=VMEM_LIMIT),
        name="rope_table",
    )(pos_rows, freq)


def _inproj_kernel(x_ref, cosp_ref, sinp_ref, mod_ref, gmix_ref, win_ref, wgate_ref, convw_ref, gconv_ref,
                   gmat_ref, onehot_ref, yc_ref, qT_ref, kc_ref, vc_ref, ksa_ref, vsT_ref, kw_ref,
                   vwT_ref, gT_ref, carry_ref):
    tm = x_ref.shape[0]

    @pl.when(pl.program_id(1) == 0)
    def _():
        carry_ref[...] = jnp.zeros_like(carry_ref)

    x = x_ref[...]
    h = _rms(x, gmix_ref[...]) * (1.0 + mod_ref[4:5, :]) + mod_ref[3:4, :]
    hb = h.astype(BF16)

    def proj(c0, width):
        return _dot(hb, win_ref[:, c0:c0 + width])

    d = lax.broadcasted_iota(jnp.int32, (tm, LANES), 1) & (HEAD_DIM - 1)
    token = lax.broadcasted_iota(jnp.int32, (tm, LANES), 0) & (TOKENS_PER_ROW - 1)
    src_lane = token * ROT_HALF + (d & (ROT_HALF - 1))

    def unpack(packed_ref):
        rows = jnp.broadcast_to(packed_ref[...][:, None, :], (tm // TOKENS_PER_ROW, TOKENS_PER_ROW, LANES))
        return jnp.take_along_axis(rows.reshape(tm, LANES), src_lane, axis=1)

    cos_t = jnp.where(d < ROT_DIM, unpack(cosp_ref), 1.0)
    sin_raw = unpack(sinp_ref)
    sin_t = jnp.where(d < ROT_HALF, -sin_raw, jnp.where(d < ROT_DIM, sin_raw, 0.0))
    first_half = d < ROT_HALF

    def rope(t):
        outs = []
        for j in range(t.shape[1] // LANES):
            tj = t[:, j * LANES:(j + 1) * LANES]
            partner = jnp.where(first_half, pltpu.roll(tj, LANES - ROT_HALF, 1),
                                pltpu.roll(tj, ROT_HALF, 1))
            outs.append(tj * cos_t + partner * sin_t)
        return outs[0] if len(outs) == 1 else jnp.concatenate(outs, axis=1)

    c0 = 3 * CONV_CH
    q = rope(proj(c0, ATTN_WIDTH)) * Q_SCALE
    qT_ref[...] = q.T.astype(qT_ref.dtype)
    c0 += ATTN_WIDTH
    kv = proj(c0, 2 * KV_WIDTH)
    kc_ref[...] = rope(kv[:, :KV_WIDTH])
    vc_ref[...] = kv[:, KV_WIDTH:]
    kv = proj(c0 + 2 * KV_WIDTH, 2 * KV_WIDTH)
    ks = rope(kv[:, :KV_WIDTH]).astype(BF16)
    vsT = kv[:, KV_WIDTH:].T.astype(BF16)
    kv = proj(c0 + 4 * KV_WIDTH, 2 * KV_WIDTH)
    kw = rope(kv[:, :KV_WIDTH]).astype(BF16)
    vwT = kv[:, KV_WIDTH:].T.astype(BF16)
    pad = jnp.zeros((tm, MXU_DEPTH - LANES - HEAD_DIM), BF16)
    ones = jnp.ones((ONES_ROWS, tm), BF16)
    for g in range(N_KV_GROUPS):
        kg = ks[:, g * HEAD_DIM:(g + 1) * HEAD_DIM]
        ksa_ref[g] = jnp.concatenate([onehot_ref[...], kg, pad], axis=1)
        kw_ref[g] = kw[:, g * HEAD_DIM:(g + 1) * HEAD_DIM]
        vsT_ref[g] = jnp.concatenate([vsT[g * HEAD_DIM:(g + 1) * HEAD_DIM, :], ones], axis=0)
        vwT_ref[g] = jnp.concatenate([vwT[g * HEAD_DIM:(g + 1) * HEAD_DIM, :], ones], axis=0)
    gT_ref[...] = jax.nn.sigmoid(_dot(hb, wgate_ref[...])).T

    cb = proj(0, CONV_CH)
    u = proj(CONV_CH, CONV_CH) * proj(2 * CONV_CH, CONV_CH)
    row = lax.broadcasted_iota(jnp.int32, (tm, 1), 0)
    prev1 = carry_ref[7:8, :]
    prev2 = carry_ref[6:7, :]
    u1 = jnp.where(row >= 1, pltpu.roll(u, 1, 0), prev1)
    u2 = jnp.where(row >= 2, pltpu.roll(u, 2, 0), jnp.where(row == 1, prev1, prev2))
    carry_ref[...] = u[tm - 8:tm, :]
    v = convw_ref[0:1, :] * u2 + convw_ref[1:2, :] * u1 + convw_ref[2:3, :] * u
    y = cb * v
    hi, lo = _split_bf16(y * y)
    ms = _dot(hi, gmat_ref[...]) + _dot(lo, gmat_ref[...])
    yc_ref[...] = (y * lax.rsqrt(ms + EPS) * gconv_ref[...]).astype(yc_ref.dtype)


def _inproj_call(x, cos_p, sin_p, mod, gmix, win, wgate, convw, gconv, gmat, onehot):
    B, S, D = x.shape
    tm = TOKEN_TILE
    tok = lambda w: pl.BlockSpec((None, tm, w), lambda b, i: (b, i, 0))
    tr = lambda w: pl.BlockSpec((None, w, tm), lambda b, i: (b, 0, i))
    grp = lambda w: pl.BlockSpec((None, N_KV_GROUPS, tm, w), lambda b, i: (b, 0, i, 0))
    grpT = pl.BlockSpec((None, N_KV_GROUPS, VT_ROWS, tm), lambda b, i: (b, 0, 0, i))
    packed = pl.BlockSpec((None, tm // TOKENS_PER_ROW, LANES), lambda b, i: (b, i, 0))
    out_shapes = (
        jax.ShapeDtypeStruct((B, S, CONV_CH), BF16),
        jax.ShapeDtypeStruct((B, ATTN_WIDTH, S), BF16),
        jax.ShapeDtypeStruct((B, S, KV_WIDTH), F32),
        jax.ShapeDtypeStruct((B, S, KV_WIDTH), F32),
        jax.ShapeDtypeStruct((B, N_KV_GROUPS, S, MXU_DEPTH), BF16),
        jax.ShapeDtypeStruct((B, N_KV_GROUPS, VT_ROWS, S), BF16),
        jax.ShapeDtypeStruct((B, N_KV_GROUPS, S, HEAD_DIM), BF16),
        jax.ShapeDtypeStruct((B, N_KV_GROUPS, VT_ROWS, S), BF16),
        jax.ShapeDtypeStruct((B, N_KV_GROUPS * GATE_PAD, S), F32),
    )
    return pl.pallas_call(
        _inproj_kernel,
        grid=(B, S // tm),
        in_specs=[tok(D),
                  packed, packed,
                  pl.BlockSpec((None, N_MOD, D), lambda b, i: (b, 0, 0)),
                  _const_spec(gmix.shape), _const_spec(win.shape), _const_spec(wgate.shape),
                  _const_spec(convw.shape),
                  _const_spec(gconv.shape), _const_spec(gmat.shape),
                  pl.BlockSpec((tm, onehot.shape[1]), lambda b, i: (i, 0))],
        out_specs=(tok(CONV_CH), tr(ATTN_WIDTH), tok(KV_WIDTH), tok(KV_WIDTH), grp(MXU_DEPTH),
                   grpT, grp(HEAD_DIM), grpT, tr(N_KV_GROUPS * GATE_PAD)),
        out_shape=out_shapes,
        scratch_shapes=[pltpu.VMEM((8, CONV_CH), F32)],
        compiler_params=pltpu.CompilerParams(dimension_semantics=("arbitrary", "arbitrary"),
                                             vmem_limit_bytes=VMEM_LIMIT),
        name="mixer_inproj",
    )(x, cos_p, sin_p, mod, gmix, win, wgate, convw, gconv, gmat, onehot)


def _compress_kernel(kf_ref, vf_ref, pek_ref, pev_ref, wk1_ref, wk2_ref, wv1_ref, wv2_ref,
                     kcc_ref, vcT_ref):
    half = CMP_BLOCK // 2
    n = kf_ref.shape[0] // half

    def mlp(x_ref, pe_ref, w1_ref, w2_ref):
        parts = []
        for p in range(2):
            acc = None
            for l0 in range(0, half, 2):
                xs, ws = [], []
                for l in (l0, l0 + 1):
                    row = p * half + l
                    xs.append((x_ref[pl.ds(l, n, stride=half), :]
                               + pe_ref[row:row + 1, :]).astype(BF16))
                    w = w1_ref[row * HEAD_DIM:(row + 1) * HEAD_DIM, :].astype(BF16)
                    z = jnp.zeros_like(w)
                    ws.append(jnp.concatenate([jnp.concatenate([w, z], axis=1),
                                               jnp.concatenate([z, w], axis=1)], axis=0))
                d = _dot(jnp.concatenate(xs, axis=1), jnp.concatenate(ws, axis=0))
                acc = d if acc is None else acc + d
            parts.append(acc)
        hpre = parts[0] + pltpu.roll(parts[1], n - 1, 0)
        hid = (hpre * jax.nn.sigmoid(hpre)).astype(BF16)
        w2 = w2_ref[...].astype(BF16)
        return jnp.concatenate([_dot(hid[:, g * CMP_HIDDEN:(g + 1) * CMP_HIDDEN], w2)
                                for g in range(N_KV_GROUPS)], axis=1)

    kc = mlp(kf_ref, pek_ref, wk1_ref, wk2_ref).astype(kcc_ref.dtype)
    for g in range(N_KV_GROUPS):
        kcc_ref[g] = kc[:, g * HEAD_DIM:(g + 1) * HEAD_DIM]
    vcT = mlp(vf_ref, pev_ref, wv1_ref, wv2_ref).T.astype(vcT_ref.dtype)
    ones = jnp.ones((ONES_ROWS, n), vcT_ref.dtype)
    for g in range(N_KV_GROUPS):
        vcT_ref[g] = jnp.concatenate([vcT[g * HEAD_DIM:(g + 1) * HEAD_DIM, :], ones], axis=0)


def _compress_call(kf, vf, pek, pev, wk1, wk2, wv1, wv2):
    B, S, width = kf.shape
    n = S // (CMP_BLOCK // 2)
    flat = pl.BlockSpec((None, S, width), lambda b: (b, 0, 0))
    return pl.pallas_call(
        _compress_kernel,
        grid=(B,),
        in_specs=[flat, flat, _const_spec(pek.shape), _const_spec(pev.shape),
                  _const_spec(wk1.shape), _const_spec(wk2.shape), _const_spec(wv1.shape),
                  _const_spec(wv2.shape)],
        out_specs=(pl.BlockSpec((None, N_KV_GROUPS, n, HEAD_DIM), lambda b: (b, 0, 0, 0)),
                   pl.BlockSpec((None, N_KV_GROUPS, VT_ROWS, n), lambda b: (b, 0, 0, 0))),
        out_shape=(jax.ShapeDtypeStruct((B, N_KV_GROUPS, n, HEAD_DIM), BF16),
                   jax.ShapeDtypeStruct((B, N_KV_GROUPS, VT_ROWS, n), BF16)),
        compiler_params=pltpu.CompilerParams(dimension_semantics=("arbitrary",),
                                             vmem_limit_bytes=VMEM_LIMIT),
        name="kv_compress",
    )(kf, vf, pek, pev, wk1, wk2, wv1, wv2)


def _lane_tiles(x, n):
    return jnp.concatenate([x] * n, axis=1)


def _col_max(s_ref, n_rows, bias_fn):
    groups = ROW_TILE // 8
    mx = [jnp.full((8, s_ref.shape[1]), NEG, F32)] * groups
    for r in range(0, n_rows, ROW_TILE):
        x = s_ref[r:r + ROW_TILE, :]
        if bias_fn is not None:
            x = x + bias_fn(r)
            s_ref[r:r + ROW_TILE, :] = x
        mx = [jnp.maximum(mx[i], x[8 * i:8 * (i + 1), :]) for i in range(groups)]
    while len(mx) > 1:
        mx = [jnp.maximum(a, b) for a, b in zip(mx[0::2], mx[1::2])]
    return jnp.max(mx[0], axis=0, keepdims=True)


def _col_exp2(s_ref, p_ref, n_rows, m, keep_f32=False):
    for r in range(0, n_rows, ROW_TILE):
        p = jnp.exp2(s_ref[r:r + ROW_TILE, :] - m)
        if keep_f32:
            s_ref[r:r + ROW_TILE, :] = p
        p_ref[r:r + ROW_TILE, :] = p.astype(p_ref.dtype)


def _recip_pos(l):
    return 1.0 / jnp.where(l > 0.0, l, 1.0)


def _select_kernel(top_n, qT_ref, kcc_ref, vcT_ref, ovlT_ref, sel_ref, ocmp_ref, sc_ref, pc_ref,
                   ph_ref, pl_ref):
    QB = SEL_BLOCK
    n_cmp = kcc_ref.shape[0]
    n_slc = ovlT_ref.shape[0]
    t0 = pl.program_id(2) * QB
    tq = t0 + lax.broadcasted_iota(jnp.int32, (1, QB), 1)
    rows = lax.broadcasted_iota(jnp.int32, (ROW_TILE, 1), 0)

    qT = qT_ref[...]
    qcat = jnp.concatenate([qT[h * HEAD_DIM:(h + 1) * HEAD_DIM, :] for h in range(HPG)], axis=1)

    def cmp_bias(r):
        cmp_end = (r + rows) * CMP_STRIDE + (CMP_BLOCK - 1)
        return _lane_tiles(jnp.where(cmp_end <= tq, 0.0, NEG), HPG)

    cur = tq // SLC_BLOCK

    def tree(op, xs):
        while len(xs) > 1:
            xs = [op(*xs[i:i + 2]) if i + 1 < len(xs) else xs[i] for i in range(0, len(xs), 2)]
        return xs[0]

    def causal_variant(n_rows):
        rows_cmp = min(n_cmp, n_rows * (SLC_BLOCK // CMP_STRIDE))

        def run():
            sc_ref[0:rows_cmp, :] = _dot(kcc_ref[0:rows_cmp, :], qcat)
            m = jnp.maximum(_col_max(sc_ref, rows_cmp, cmp_bias), M_FLOOR)
            _col_exp2(sc_ref, pc_ref, rows_cmp, m, keep_f32=True)
            o_cmp = _dot(vcT_ref[:, 0:rows_cmp], pc_ref[0:rows_cmp, :])
            rl = _recip_pos(o_cmp[HEAD_DIM:HEAD_DIM + 1, :])
            o_cmp = o_cmp[0:HEAD_DIM, :] * rl
            for h in range(HPG):
                ocmp_ref[h] = o_cmp[:, h * QB:(h + 1) * QB]

            for r in range(0, rows_cmp, ROW_TILE):
                pn = sc_ref[r:r + ROW_TILE, :] * rl
                psum = pn[:, 0:QB]
                for h in range(1, HPG):
                    psum = psum + pn[:, h * QB:(h + 1) * QB]
                hi, lo = _split_bf16(psum)
                ph_ref[r:r + ROW_TILE, :] = hi
                pl_ref[r:r + ROW_TILE, :] = lo
            ovl = ovlT_ref[0:n_rows, 0:rows_cmp]
            imp = _dot(ovl, ph_ref[0:rows_cmp, :]) + _dot(ovl, pl_ref[0:rows_cmp, :])

            blk = lax.broadcasted_iota(jnp.int32, (n_rows, 1), 0)
            future = blk > cur
            forced = (blk == 0) | (blk == cur) | (blk == cur - 1)
            groups = range(n_rows // 8)

            def store(picked):
                sel_ref[0:n_rows, :] = jnp.where(future, NEG, jnp.where(picked, 0.0, NEG)).astype(sel_ref.dtype)

            def with_ties():
                score0 = jnp.where(future, -BIG, jnp.where(forced, BIG, imp))
                score = [score0[8 * i:8 * (i + 1), :] for i in groups]
                cum = jnp.zeros((1, QB), F32)
                thr = jnp.zeros((1, QB), F32)
                above = jnp.zeros((1, QB), F32)
                for _ in range(top_n):
                    best = jnp.max(tree(jnp.maximum, score), axis=0, keepdims=True)
                    eq = [s == best for s in score]
                    unfilled = cum < top_n
                    thr = jnp.where(unfilled, best, thr)
                    above = jnp.where(unfilled, cum, above)
                    cum = cum + jnp.sum(tree(jnp.add, [jnp.where(e, 1.0, 0.0) for e in eq]),
                                        axis=0, keepdims=True)
                    score = [jnp.where(e, -jnp.inf, s) for e, s in zip(eq, score)]
                ties = score0 == thr
                lower = jnp.where(lax.broadcasted_iota(jnp.int32, (1, n_rows), 1) < blk, 1.0, 0.0)
                rank = _dot(lower.astype(BF16), jnp.where(ties, 1.0, 0.0).astype(BF16))
                store((score0 > thr) | (ties & (rank < top_n - above)))

            if top_n > MAX_FORCED:
                n_forced = 1.0 + jnp.where(cur >= 1, 1.0, 0.0) + jnp.where(cur >= 2, 1.0, 0.0)
                wanted = top_n - n_forced
                take_all = (cur.astype(F32) + 1.0 - n_forced) <= wanted
                cand = jnp.where(future | forced, -BIG, imp)
                score = [cand[8 * i:8 * (i + 1), :] for i in groups]
                for _ in range(top_n - MAX_FORCED):
                    cut = jnp.max(tree(jnp.maximum, score), axis=0, keepdims=True)
                    score = [jnp.where(s == cut, -jnp.inf, s) for s in score]
                above_cut = cand >= cut
                count = jnp.sum(jnp.where(above_cut, 1.0, 0.0), axis=0, keepdims=True)
                exact = jnp.min(jnp.where(take_all | (count == wanted), 1.0, 0.0)) > 0.5
                lax.cond(exact, lambda: store(forced | take_all | above_cut), with_ties)
            else:
                with_ties()
            if n_rows < n_slc:
                sel_ref[n_rows:n_slc, :] = jnp.full((n_slc - n_rows, QB), NEG, sel_ref.dtype)
        return run

    steps = n_slc // SEL_ROW_STEP
    need = jnp.minimum((t0 + QB - 1) // (SLC_BLOCK * SEL_ROW_STEP), steps - 1)
    lax.switch(need, [causal_variant(SEL_ROW_STEP * (k + 1)) for k in range(steps)])


def _select_call(top_n, qT, kcc, vcT, ovlT):
    B, _, S = qT.shape
    n_cmp = kcc.shape[2]
    n_slc = ovlT.shape[0]
    gw = HPG * HEAD_DIM
    qw = HPG * SEL_BLOCK
    return pl.pallas_call(
        functools.partial(_select_kernel, top_n),
        grid=(B, N_KV_GROUPS, S // SEL_BLOCK),
        in_specs=[pl.BlockSpec((None, gw, SEL_BLOCK), lambda b, g, i: (b, g, i)),
                  pl.BlockSpec((None, None, n_cmp, HEAD_DIM), lambda b, g, i: (b, g, 0, 0)),
                  pl.BlockSpec((None, None, VT_ROWS, n_cmp), lambda b, g, i: (b, g, 0, 0)),
                  pl.BlockSpec(ovlT.shape, lambda b, g, i: (0, 0))],
        out_specs=(pl.BlockSpec((None, None, n_slc, SEL_BLOCK), lambda b, g, i: (b, g, 0, i)),
                   pl.BlockSpec((None, None, HPG, HEAD_DIM, SEL_BLOCK),
                                lambda b, g, i: (b, g, 0, 0, i))),
        out_shape=(jax.ShapeDtypeStruct((B, N_KV_GROUPS, n_slc, S), BF16),
                   jax.ShapeDtypeStruct((B, N_KV_GROUPS, HPG, HEAD_DIM, S), F32)),
        scratch_shapes=[pltpu.VMEM((n_cmp, qw), F32), pltpu.VMEM((n_cmp, qw), BF16),
                        pltpu.VMEM((n_cmp, SEL_BLOCK), BF16), pltpu.VMEM((n_cmp, SEL_BLOCK), BF16)],
        compiler_params=pltpu.CompilerParams(
            dimension_semantics=("parallel", "parallel", "arbitrary"),
            vmem_limit_bytes=VMEM_LIMIT),
        name="nsa_select",
    )(qT, kcc, vcT, ovlT)


def _attn_kernel(qT_ref, gT_ref, sel_ref, ocmp_ref, ksa_ref, vsT_ref, kw_ref, vwT_ref, ga_ref,
                 o_ref, s0_ref, s1_ref, p0_ref, p1_ref, sw_ref, pw_ref, rhs_ref, acc_ref):
    QB = Q_BLOCK
    n_slc = sel_ref.shape[0]
    t0 = pl.program_id(2) * QB
    tq = t0 + lax.broadcasted_iota(jnp.int32, (1, QB), 1)
    rows = lax.broadcasted_iota(jnp.int32, (ROW_TILE, 1), 0)

    qT = qT_ref[...]
    qcat = jnp.concatenate([qT[h * HEAD_DIM:(h + 1) * HEAD_DIM, :] for h in range(HPG)], axis=1)

    rhs_ref[...] = jnp.concatenate([_lane_tiles(sel_ref[...], HPG), qcat,
                                    jnp.zeros((MXU_DEPTH - n_slc - HEAD_DIM, QW), BF16)], axis=0)

    w0 = pl.multiple_of(jnp.maximum(t0 - WINDOW, 0), QB)
    sw_ref[...] = _dot(kw_ref[pl.ds(w0, WIN_KEYS), :], qcat)

    def win_bias(r):
        dist = tq - (w0 + r + rows)
        return _lane_tiles(jnp.where((dist >= 0) & (dist < WINDOW), 0.0, NEG), HPG)

    KC = KEY_CHUNK
    last = t0 // KC

    def scores(c, dst_ref):
        k0 = pl.multiple_of(jnp.minimum(c, last) * KC, KC)
        dst_ref[...] = _dot(ksa_ref[pl.ds(k0, KC), :], rhs_ref[...])

    def weighted_values(c, p_ref):
        k0 = pl.multiple_of(jnp.clip(c, 0, last) * KC, KC)
        return _dot(vsT_ref[:, pl.ds(k0, KC)], p_ref[...])

    def softmax(c, src_ref, p_ref, m, pending, causal):
        def causal_bias(r):
            return _lane_tiles(jnp.where(c * KC + r + rows <= tq, 0.0, NEG), HPG)

        m_new = jnp.maximum(m, _col_max(src_ref, KC, causal_bias if causal else None))
        acc_ref[...] = jnp.exp2(m - m_new) * (acc_ref[...] + pending)
        _col_exp2(src_ref, p_ref, KC, m_new)
        return m_new

    def pair(i, m, final=False):
        pending = weighted_values(2 * i - 1, p1_ref)
        scores(2 * i + 1, s1_ref)
        m = softmax(2 * i, s0_ref, p0_ref, m, pending, False)
        pending = weighted_values(2 * i, p0_ref)
        if not final:
            scores(2 * i + 2, s0_ref)
        return softmax(2 * i + 1, s1_ref, p1_ref, m, pending, final)

    p1_ref[...] = jnp.zeros_like(p1_ref)
    acc_ref[...] = jnp.zeros_like(acc_ref)
    scores(0, s0_ref)

    m = _col_max(sw_ref, WIN_KEYS, win_bias)
    _col_exp2(sw_ref, pw_ref, WIN_KEYS, m)
    o_win = _dot(vwT_ref[:, pl.ds(w0, WIN_KEYS)], pw_ref[...])
    o_win = o_win[0:HEAD_DIM, :] * (1.0 / o_win[HEAD_DIM:HEAD_DIM + 1, :])

    full_pairs = last // 2
    m = lax.fori_loop(0, full_pairs // 2, lambda j, mm: pair(2 * j + 1, pair(2 * j, mm)),
                      jnp.full((1, QW), NEG, F32))
    m = lax.cond(full_pairs % 2 == 1, lambda mm: pair(full_pairs - 1, mm), lambda mm: mm, m)

    @pl.when(last % 2 == 0)
    def _():
        pending = weighted_values(last - 1, p1_ref)
        softmax(last, s0_ref, p0_ref, m, pending, True)
        acc_ref[...] += weighted_values(last, p0_ref)

    @pl.when(last % 2 == 1)
    def _():
        pair(full_pairs, m, final=True)
        acc_ref[...] += weighted_values(last, p1_ref)

    acc = acc_ref[...]
    o_slc = acc[0:HEAD_DIM, :] * (1.0 / acc[HEAD_DIM:HEAD_DIM + 1, :])

    gts = gT_ref[...]
    outs = []
    for h in range(HPG):
        sl = slice(h * QB, (h + 1) * QB)
        o = (gts[3 * h:3 * h + 1, :] * ocmp_ref[h] + gts[3 * h + 1:3 * h + 2, :] * o_slc[:, sl]
             + gts[3 * h + 2:3 * h + 3, :] * o_win[:, sl])
        o = o * lax.rsqrt(jnp.mean(o * o, axis=0, keepdims=True) + EPS)
        outs.append((o * ga_ref[h * HEAD_DIM:(h + 1) * HEAD_DIM, :]).T)
    o_ref[...] = jnp.concatenate(outs, axis=1).astype(o_ref.dtype)


def _attn_call(qT, gatesT, sel_bias, o_cmp, ksa, vsT, kw, vwT, g_attn_col):
    B, _, S = qT.shape
    n_slc = sel_bias.shape[2]
    gw = HPG * HEAD_DIM
    rows = lambda n, w: pl.BlockSpec((None, None, n, w), lambda b, g, i: (b, g, 0, 0))
    cols = lambda n: pl.BlockSpec((None, None, VT_ROWS, n), lambda b, g, i: (b, g, 0, 0))
    return pl.pallas_call(
        _attn_kernel,
        grid=(B, N_KV_GROUPS, S // Q_BLOCK),
        in_specs=[pl.BlockSpec((None, gw, Q_BLOCK), lambda b, g, i: (b, g, i)),
                  pl.BlockSpec((None, GATE_PAD, Q_BLOCK), lambda b, g, i: (b, g, i)),
                  pl.BlockSpec((None, None, n_slc, Q_BLOCK), lambda b, g, i: (b, g, 0, i)),
                  pl.BlockSpec((None, None, HPG, HEAD_DIM, Q_BLOCK), lambda b, g, i: (b, g, 0, 0, i)),
                  rows(S, MXU_DEPTH), cols(S), rows(S, HEAD_DIM), cols(S),
                  pl.BlockSpec((gw, 1), lambda b, g, i: (g, 0))],
        out_specs=pl.BlockSpec((None, Q_BLOCK, gw), lambda b, g, i: (b, i, g)),
        out_shape=jax.ShapeDtypeStruct((B, S, ATTN_WIDTH), BF16),
        scratch_shapes=[pltpu.VMEM((KEY_CHUNK, QW), F32), pltpu.VMEM((KEY_CHUNK, QW), F32),
                        pltpu.VMEM((KEY_CHUNK, QW), BF16), pltpu.VMEM((KEY_CHUNK, QW), BF16),
                        pltpu.VMEM((WIN_KEYS, QW), F32), pltpu.VMEM((WIN_KEYS, QW), BF16),
                        pltpu.VMEM((MXU_DEPTH, QW), BF16), pltpu.VMEM((VT_ROWS, QW), F32)],
        compiler_params=pltpu.CompilerParams(
            dimension_semantics=("parallel", "parallel", "arbitrary"),
            vmem_limit_bytes=VMEM_LIMIT),
        name="nsa_attention",
    )(qT, gatesT, sel_bias, o_cmp, ksa, vsT, kw, vwT, g_attn_col)


def _out_kernel(x_ref, yc_ref, ya_ref, mod_ref, wo_ref, g_ref, wg_ref, wu_ref, wd_ref, gf_ref,
                o_ref, a_ref):
    mix = _dot(yc_ref[...], wo_ref[0:CONV_CH, :]) + _dot(ya_ref[...], wo_ref[CONV_CH:, :])
    x = x_ref[...] + mod_ref[5:6, :] * mix
    x = _ffn_core(x, mod_ref[6:7, :], mod_ref[7:8, :], mod_ref[8:9, :], g_ref[...], wg_ref,
                  wu_ref, wd_ref, a_ref)
    o_ref[...] = _rms(x, gf_ref[...])


def _out_call(x, yc, ya, mod, wo, g, wg, wu, wd, gf):
    B, S, D = x.shape
    tm = FFN_TOKEN_TILE
    tok = lambda w: pl.BlockSpec((None, tm, w), lambda b, i: (b, i, 0))
    return pl.pallas_call(
        _out_kernel,
        grid=(B, S // tm),
        in_specs=[tok(D), tok(CONV_CH), tok(ATTN_WIDTH),
                  pl.BlockSpec((None, N_MOD, D), lambda b, i: (b, 0, 0)),
                  _const_spec(wo.shape), _const_spec(g.shape), _const_spec(wg.shape),
                  _const_spec(wu.shape), _const_spec(wd.shape), _const_spec(gf.shape)],
        out_specs=tok(D),
        out_shape=jax.ShapeDtypeStruct((B, S, D), F32),
        scratch_shapes=[pltpu.VMEM((tm, wg.shape[1]), BF16)],
        compiler_params=pltpu.CompilerParams(dimension_semantics=("parallel", "parallel"),
                                             vmem_limit_bytes=VMEM_LIMIT),
        name="outproj_ffn2",
    )(x, yc, ya, mod, wo, g, wg, wu, wd, gf)


def kernel(x, c, positions, w_ada, b_ada, g_ffn1, w1_gate, w1_up, w1_down, g_mix, w_in, conv_w, cmp_pos_k, cmp_pos_v, w_cmpk1, w_cmpk2, w_cmpv1, w_cmpv2, g_out_conv, g_out_attn, w_out, g_ffn2, w2_gate, w2_up, w2_down, g_final):
    B, S, D = x.shape
    depth = w_ada.shape[0]
    n_slc = S // SLC_BLOCK
    half = CMP_BLOCK // 2
    n_half = S // half
    assert n_slc <= LANES, "selection-block one-hot is one lane tile wide"

    c_pad = jnp.pad(c, ((0, 8 - B), (0, 0)))
    row = lambda a: a.reshape(1, -1)

    freq_half = jnp.power(ROPE_THETA, -2.0 * jnp.arange(ROT_HALF, dtype=F32) / ROT_DIM)
    freq = jnp.tile(freq_half, LANES // ROT_HALF).reshape(1, LANES)
    gidx = np.arange(CONV_CH) // (CONV_CH // CONV_GROUPS)
    gmat = jnp.asarray((gidx[:, None] == gidx[None, :]) / (CONV_CH // CONV_GROUPS), dtype=BF16)
    c0 = np.arange(n_half) * CMP_STRIDE
    s0 = np.arange(LANES) * SLC_BLOCK
    ovlT = ((c0[None, :] <= s0[:, None] + SLC_BLOCK - 1) & (c0[None, :] + CMP_BLOCK - 1 >= s0[:, None]))
    ovlT = jnp.asarray(ovlT, dtype=BF16)
    onehot = jnp.asarray((np.arange(S) // SLC_BLOCK)[:, None] == np.arange(LANES)[None, :], dtype=BF16)
    cos_p, sin_p = _rope_table_call(positions.reshape(-1, TOKENS_PER_ROW), freq)
    cos_p = cos_p.reshape(B, S // TOKENS_PER_ROW, LANES)
    sin_p = sin_p.reshape(B, S // TOKENS_PER_ROW, LANES)

    for l in range(depth):
        mod = _ada_call(c_pad, w_ada[l], row(b_ada[l]))[:B].reshape(B, N_MOD, D)

        x = _ffn_call(x, mod, row(g_ffn1[l]), w1_gate[l].astype(BF16), w1_up[l].astype(BF16),
                      w1_down[l].astype(BF16))

        n_main = w_in.shape[2] - N_KV_GROUPS * 3 * HPG
        gate_cols = [jnp.pad(w_in[l][:, n_main + g * 3 * HPG:n_main + (g + 1) * 3 * HPG],
                             ((0, 0), (0, GATE_PAD - 3 * HPG))) for g in range(N_KV_GROUPS)]
        wgate = jnp.concatenate(gate_cols, axis=1).astype(BF16)
        win = w_in[l].astype(BF16)
        (yc, qT, kc, vc, ksa, vsT, kw, vwT, gatesT) = _inproj_call(
            x, cos_p, sin_p, mod, row(g_mix[l]), win, wgate, conv_w[l], row(g_out_conv[l]), gmat,
            onehot)

        both_groups = lambda pe: jnp.tile(pe, (1, N_KV_GROUPS))
        kcc, vcT = _compress_call(kc, vc, both_groups(cmp_pos_k[l]), both_groups(cmp_pos_v[l]),
                                  w_cmpk1[l], w_cmpk2[l], w_cmpv1[l], w_cmpv2[l])

        sel_bias, o_cmp = _select_call(min(SLC_TOP_N, n_slc), qT, kcc, vcT, ovlT)
        ya = _attn_call(qT, gatesT, sel_bias, o_cmp, ksa, vsT, kw, vwT,
                        g_out_attn[l].reshape(ATTN_WIDTH, 1))

        assert l == depth - 1, "final norm is fused into the last layer's output kernel"
        x = _out_call(x, yc, ya, mod, w_out[l].astype(BF16), row(g_ffn2[l]),
                      w2_gate[l].astype(BF16), w2_up[l].astype(BF16), w2_down[l].astype(BF16),
                      row(g_final))
    return x
```

```python
import functools
import math

import numpy as np
import jax
import jax.numpy as jnp
from jax import lax
from jax.experimental import pallas as pl
from jax.experimental.pallas import tpu as pltpu

F32 = jnp.float32
BF16 = jnp.bfloat16

CONV_CH = 512
CONV_GROUPS = 8
N_HEADS = 8
N_KV_GROUPS = 2
HPG = N_HEADS // N_KV_GROUPS
HEAD_DIM = 64
ATTN_WIDTH = N_HEADS * HEAD_DIM
KV_WIDTH = N_KV_GROUPS * HEAD_DIM
ROPE_THETA = 500000.0
ROT_DIM = HEAD_DIM // 4
ROT_HALF = ROT_DIM // 2
CMP_BLOCK = 32
CMP_STRIDE = 16
CMP_HIDDEN = 256
SLC_BLOCK = 64
SLC_TOP_N = 16
MAX_FORCED = 3
WINDOW = 512
Q_BLOCK = 256
MACARON_W = 0.5
N_MOD = 9
EPS = 1e-6
NEG = -1e30
BIG = 1e9

LANES = 128
MXU_DEPTH = 256
VMEM_LIMIT = 58 * 1024 * 1024

TOKEN_TILE = 512
FFN_TOKEN_TILE = 512
ADA_COL_TILE = 1024
FF_TILE = 256
KEY_CHUNK = 512
WIN_KEYS = WINDOW + Q_BLOCK
TOKENS_PER_ROW = LANES // ROT_HALF
GATE_PAD = LANES
QW = HPG * Q_BLOCK
SEL_BLOCK = 1024
SEL_ROW_STEP = 32
ONES_ROWS = 16
VT_ROWS = HEAD_DIM + ONES_ROWS
ROW_TILE = 16
Q_SCALE = HEAD_DIM ** -0.5 * math.log2(math.e)
M_FLOOR = -1e20


def _dot(a, b):
    return jnp.dot(a, b, preferred_element_type=F32)


def _rms(x, g):
    return x * lax.rsqrt(jnp.mean(x * x, axis=-1, keepdims=True) + EPS) * g


def _split_bf16(x):
    hi = x.astype(BF16)
    lo = (x - hi.astype(F32)).astype(BF16)
    return hi, lo


def _const_spec(shape):
    nd = len(shape)
    return pl.BlockSpec(shape, lambda *_: (0,) * nd, pipeline_mode=pl.Buffered(1))


def _ada_kernel(c_ref, w_ref, b_ref, o_ref):
    c = c_ref[...]
    c_act = c * jax.nn.sigmoid(c)
    o_ref[...] = _dot(c_act.astype(BF16), w_ref[...].astype(BF16)) + b_ref[...]


def _ada_call(c_pad, w_ada, b_ada):
    rows, d = c_pad.shape
    n = w_ada.shape[1]
    tn = ADA_COL_TILE
    return pl.pallas_call(
        _ada_kernel,
        grid=(n // tn,),
        in_specs=[pl.BlockSpec((rows, d), lambda j: (0, 0)),
                  pl.BlockSpec((d, tn), lambda j: (0, j)),
                  pl.BlockSpec((1, tn), lambda j: (0, j))],
        out_specs=pl.BlockSpec((rows, tn), lambda j: (0, j)),
        out_shape=jax.ShapeDtypeStruct((rows, n), F32),
        compiler_params=pltpu.CompilerParams(dimension_semantics=("arbitrary",),
                                             vmem_limit_bytes=VMEM_LIMIT),
        name="adaln_mod",
    )(c_pad, w_ada, b_ada)


def _ffn_core(x, shift, scale, gate, g, wg_ref, wu_ref, wd_ref, a_ref):
    h = _rms(x, g) * (1.0 + scale) + shift
    hb = h.astype(BF16)
    d_ff = wg_ref.shape[1]
    for j in range(d_ff // FF_TILE):
        sl = slice(j * FF_TILE, (j + 1) * FF_TILE)
        gg = _dot(hb, wg_ref[:, sl])
        uu = _dot(hb, wu_ref[:, sl])
        a_ref[:, sl] = (gg * jax.nn.sigmoid(gg) * uu).astype(a_ref.dtype)
    return x + (MACARON_W * gate) * _dot(a_ref[...], wd_ref[...])


def _cast_once(src_refs, dst_refs):
    @pl.when((pl.program_id(0) == 0) & (pl.program_id(1) == 0))
    def _():
        for src, dst in zip(src_refs, dst_refs):
            for c in range(0, src.shape[1], FF_TILE):
                dst[:, c:c + FF_TILE] = src[:, c:c + FF_TILE].astype(dst.dtype)


def _ffn_kernel(x_ref, mod_ref, g_ref, wg_ref, wu_ref, wd_ref, o_ref, a_ref, wgb_ref, wub_ref):
    _cast_once((wg_ref, wu_ref), (wgb_ref, wub_ref))
    o_ref[...] = _ffn_core(x_ref[...], mod_ref[0:1, :], mod_ref[1:2, :], mod_ref[2:3, :],
                           g_ref[...], wgb_ref, wub_ref, wd_ref, a_ref)


def _ffn_call(x, mod, g, wg, wu, wd):
    B, S, D = x.shape
    tm = FFN_TOKEN_TILE
    return pl.pallas_call(
        _ffn_kernel,
        grid=(B, S // tm),
        in_specs=[pl.BlockSpec((None, tm, D), lambda b, i: (b, i, 0)),
                  pl.BlockSpec((None, N_MOD, D), lambda b, i: (b, 0, 0)),
                  _const_spec(g.shape), _const_spec(wg.shape), _const_spec(wu.shape),
                  _const_spec(wd.shape)],
        out_specs=pl.BlockSpec((None, tm, D), lambda b, i: (b, i, 0)),
        out_shape=jax.ShapeDtypeStruct((B, S, D), F32),
        scratch_shapes=[pltpu.VMEM((tm, wg.shape[1]), BF16), pltpu.VMEM(wg.shape, BF16),
                        pltpu.VMEM(wu.shape, BF16)],
        compiler_params=pltpu.CompilerParams(dimension_semantics=("arbitrary", "arbitrary"),
                                             vmem_limit_bytes=VMEM_LIMIT),
        name="ffn1",
    )(x, mod, g, wg, wu, wd)


def _rope_table_kernel(pos_ref, freq_ref, cos_ref, sin_ref):
    pos = jnp.concatenate([pos_ref[...].astype(F32),
                           jnp.zeros((pos_ref.shape[0], LANES - TOKENS_PER_ROW), F32)], axis=1)
    src = lax.broadcasted_iota(jnp.int32, pos.shape, 1) // ROT_HALF
    ang = jnp.take_along_axis(pos, src, axis=1) * freq_ref[...]
    cos_ref[...] = jnp.cos(ang)
    sin_ref[...] = jnp.sin(ang)


def _rope_table_call(pos_rows, freq):
    rows = pos_rows.shape[0]
    shape = jax.ShapeDtypeStruct((rows, LANES), F32)
    full = pl.BlockSpec((rows, LANES), lambda: (0, 0))
    return pl.pallas_call(
        _rope_table_kernel,
        in_specs=[pl.BlockSpec(pos_rows.shape, lambda: (0, 0)), pl.BlockSpec(freq.shape, lambda: (0, 0))],
        out_specs=(full, full),
        out_shape=(shape, shape),
        compiler_params=pltpu.CompilerParams(vmem_limit_bytes=VMEM_LIMIT),
        name="rope_table",
    )(pos_rows, freq)


def _inproj_kernel(x_ref, cosp_ref, sinp_ref, mod_ref, gmix_ref, win_ref, wgate_ref, convw_ref, gconv_ref,
                   gmat_ref, onehot_ref, yc_ref, qT_ref, kc_ref, vc_ref, ksa_ref, vsT_ref, kw_ref,
                   vwT_ref, gT_ref, carry_ref):
    tm = x_ref.shape[0]

    @pl.when(pl.program_id(1) == 0)
    def _():
        carry_ref[...] = jnp.zeros_like(carry_ref)

    x = x_ref[...]
    h = _rms(x, gmix_ref[...]) * (1.0 + mod_ref[4:5, :]) + mod_ref[3:4, :]
    hb = h.astype(BF16)

    def proj(c0, width):
        return _dot(hb, win_ref[:, c0:c0 + width])

    d = lax.broadcasted_iota(jnp.int32, (tm, LANES), 1) & (HEAD_DIM - 1)
    token = lax.broadcasted_iota(jnp.int32, (tm, LANES), 0) & (TOKENS_PER_ROW - 1)
    src_lane = token * ROT_HALF + (d & (ROT_HALF - 1))

    def unpack(packed_ref):
        rows = jnp.broadcast_to(packed_ref[...][:, None, :], (tm // TOKENS_PER_ROW, TOKENS_PER_ROW, LANES))
        return jnp.take_along_axis(rows.reshape(tm, LANES), src_lane, axis=1)

    cos_t = jnp.where(d < ROT_DIM, unpack(cosp_ref), 1.0)
    sin_raw = unpack(sinp_ref)
    sin_t = jnp.where(d < ROT_HALF, -sin_raw, jnp.where(d < ROT_DIM, sin_raw, 0.0))
    first_half = d < ROT_HALF

    def rope(t):
        outs = []
        for j in range(t.shape[1] // LANES):
            tj = t[:, j * LANES:(j + 1) * LANES]
            partner = jnp.where(first_half, pltpu.roll(tj, LANES - ROT_HALF, 1),
                                pltpu.roll(tj, ROT_HALF, 1))
            outs.append(tj * cos_t + partner * sin_t)
        return outs[0] if len(outs) == 1 else jnp.concatenate(outs, axis=1)

    c0 = 3 * CONV_CH
    q = rope(proj(c0, ATTN_WIDTH)) * Q_SCALE
    qT_ref[...] = q.T.astype(qT_ref.dtype)
    c0 += ATTN_WIDTH
    kv = proj(c0, 2 * KV_WIDTH)
    kc_ref[...] = rope(kv[:, :KV_WIDTH])
    vc_ref[...] = kv[:, KV_WIDTH:]
    kv = proj(c0 + 2 * KV_WIDTH, 2 * KV_WIDTH)
    ks = rope(kv[:, :KV_WIDTH]).astype(BF16)
    vsT = kv[:, KV_WIDTH:].T.astype(BF16)
    kv = proj(c0 + 4 * KV_WIDTH, 2 * KV_WIDTH)
    kw = rope(kv[:, :KV_WIDTH]).astype(BF16)
    vwT = kv[:, KV_WIDTH:].T.astype(BF16)
    pad = jnp.zeros((tm, MXU_DEPTH - LANES - HEAD_DIM), BF16)
    ones = jnp.ones((ONES_ROWS, tm), BF16)
    for g in range(N_KV_GROUPS):
        kg = ks[:, g * HEAD_DIM:(g + 1) * HEAD_DIM]
        ksa_ref[g] = jnp.concatenate([onehot_ref[...], kg, pad], axis=1)
        kw_ref[g] = kw[:, g * HEAD_DIM:(g + 1) * HEAD_DIM]
        vsT_ref[g] = jnp.concatenate([vsT[g * HEAD_DIM:(g + 1) * HEAD_DIM, :], ones], axis=0)
        vwT_ref[g] = jnp.concatenate([vwT[g * HEAD_DIM:(g + 1) * HEAD_DIM, :], ones], axis=0)
    gT_ref[...] = jax.nn.sigmoid(_dot(hb, wgate_ref[...])).T

    cb = proj(0, CONV_CH)
    u = proj(CONV_CH, CONV_CH) * proj(2 * CONV_CH, CONV_CH)
    row = lax.broadcasted_iota(jnp.int32, (tm, 1), 0)
    prev1 = carry_ref[7:8, :]
    prev2 = carry_ref[6:7, :]
    u1 = jnp.where(row >= 1, pltpu.roll(u, 1, 0), prev1)
    u2 = jnp.where(row >= 2, pltpu.roll(u, 2, 0), jnp.where(row == 1, prev1, prev2))
    carry_ref[...] = u[tm - 8:tm, :]
    v = convw_ref[0:1, :] * u2 + convw_ref[1:2, :] * u1 + convw_ref[2:3, :] * u
    y = cb * v
    hi, lo = _split_bf16(y * y)
    ms = _dot(hi, gmat_ref[...]) + _dot(lo, gmat_ref[...])
    yc_ref[...] = (y * lax.rsqrt(ms + EPS) * gconv_ref[...]).astype(yc_ref.dtype)


def _inproj_call(x, cos_p, sin_p, mod, gmix, win, wgate, convw, gconv, gmat, onehot):
    B, S, D = x.shape
    tm = TOKEN_TILE
    tok = lambda w: pl.BlockSpec((None, tm, w), lambda b, i: (b, i, 0))
    tr = lambda w: pl.BlockSpec((None, w, tm), lambda b, i: (b, 0, i))
    grp = lambda w: pl.BlockSpec((None, N_KV_GROUPS, tm, w), lambda b, i: (b, 0, i, 0))
    grpT = pl.BlockSpec((None, N_KV_GROUPS, VT_ROWS, tm), lambda b, i: (b, 0, 0, i))
    packed = pl.BlockSpec((None, tm // TOKENS_PER_ROW, LANES), lambda b, i: (b, i, 0))
    out_shapes = (
        jax.ShapeDtypeStruct((B, S, CONV_CH), BF16),
        jax.ShapeDtypeStruct((B, ATTN_WIDTH, S), BF16),
        jax.ShapeDtypeStruct((B, S, KV_WIDTH), F32),
        jax.ShapeDtypeStruct((B, S, KV_WIDTH), F32),
        jax.ShapeDtypeStruct((B, N_KV_GROUPS, S, MXU_DEPTH), BF16),
        jax.ShapeDtypeStruct((B, N_KV_GROUPS, VT_ROWS, S), BF16),
        jax.ShapeDtypeStruct((B, N_KV_GROUPS, S, HEAD_DIM), BF16),
        jax.ShapeDtypeStruct((B, N_KV_GROUPS, VT_ROWS, S), BF16),
        jax.ShapeDtypeStruct((B, N_KV_GROUPS * GATE_PAD, S), F32),
    )
    return pl.pallas_call(
        _inproj_kernel,
        grid=(B, S // tm),
        in_specs=[tok(D),
                  packed, packed,
                  pl.BlockSpec((None, N_MOD, D), lambda b, i: (b, 0, 0)),
                  _const_spec(gmix.shape), _const_spec(win.shape), _const_spec(wgate.shape),
                  _const_spec(convw.shape),
                  _const_spec(gconv.shape), _const_spec(gmat.shape),
                  pl.BlockSpec((tm, onehot.shape[1]), lambda b, i: (i, 0))],
        out_specs=(tok(CONV_CH), tr(ATTN_WIDTH), tok(KV_WIDTH), tok(KV_WIDTH), grp(MXU_DEPTH),
                   grpT, grp(HEAD_DIM), grpT, tr(N_KV_GROUPS * GATE_PAD)),
        out_shape=out_shapes,
        scratch_shapes=[pltpu.VMEM((8, CONV_CH), F32)],
        compiler_params=pltpu.CompilerParams(dimension_semantics=("arbitrary", "arbitrary"),
                                             vmem_limit_bytes=VMEM_LIMIT),
        name="mixer_inproj",
    )(x, cos_p, sin_p, mod, gmix, win, wgate, convw, gconv, gmat, onehot)


def _compress_kernel(kf_ref, vf_ref, pek_ref, pev_ref, wk1_ref, wk2_ref, wv1_ref, wv2_ref,
                     kcc_ref, vcT_ref):
    half = CMP_BLOCK // 2
    n = kf_ref.shape[0] // half

    def mlp(x_ref, pe_ref, w1_ref, w2_ref):
        parts = []
        for p in range(2):
            acc = None
            for l0 in range(0, half, 2):
                xs, ws = [], []
                for l in (l0, l0 + 1):
                    row = p * half + l
                    xs.append((x_ref[pl.ds(l, n, stride=half), :]
                               + pe_ref[row:row + 1, :]).astype(BF16))
                    w = w1_ref[row * HEAD_DIM:(row + 1) * HEAD_DIM, :].astype(BF16)
                    z = jnp.zeros_like(w)
                    ws.append(jnp.concatenate([jnp.concatenate([w, z], axis=1),
                                               jnp.concatenate([z, w], axis=1)], axis=0))
                d = _dot(jnp.concatenate(xs, axis=1), jnp.concatenate(ws, axis=0))
                acc = d if acc is None else acc + d
            parts.append(acc)
        hpre = parts[0] + pltpu.roll(parts[1], n - 1, 0)
        hid = (hpre * jax.nn.sigmoid(hpre)).astype(BF16)
        w2 = w2_ref[...].astype(BF16)
        return jnp.concatenate([_dot(hid[:, g * CMP_HIDDEN:(g + 1) * CMP_HIDDEN], w2)
                                for g in range(N_KV_GROUPS)], axis=1)

    kc = mlp(kf_ref, pek_ref, wk1_ref, wk2_ref).astype(kcc_ref.dtype)
    for g in range(N_KV_GROUPS):
        kcc_ref[g] = kc[:, g * HEAD_DIM:(g + 1) * HEAD_DIM]
    vcT = mlp(vf_ref, pev_ref, wv1_ref, wv2_ref).T.astype(vcT_ref.dtype)
    ones = jnp.ones((ONES_ROWS, n), vcT_ref.dtype)
    for g in range(N_KV_GROUPS):
        vcT_ref[g] = jnp.concatenate([vcT[g * HEAD_DIM:(g + 1) * HEAD_DIM, :], ones], axis=0)


def _compress_call(kf, vf, pek, pev, wk1, wk2, wv1, wv2):
    B, S, width = kf.shape
    n = S // (CMP_BLOCK // 2)
    flat = pl.BlockSpec((None, S, width), lambda b: (b, 0, 0))
    return pl.pallas_call(
        _compress_kernel,
        grid=(B,),
        in_specs=[flat, flat, _const_spec(pek.shape), _const_spec(pev.shape),
                  _const_spec(wk1.shape), _const_spec(wk2.shape), _const_spec(wv1.shape),
                  _const_spec(wv2.shape)],
        out_specs=(pl.BlockSpec((None, N_KV_GROUPS, n, HEAD_DIM), lambda b: (b, 0, 0, 0)),
                   pl.BlockSpec((None, N_KV_GROUPS, VT_ROWS, n), lambda b: (b, 0, 0, 0))),
        out_shape=(jax.ShapeDtypeStruct((B, N_KV_GROUPS, n, HEAD_DIM), BF16),
                   jax.ShapeDtypeStruct((B, N_KV_GROUPS, VT_ROWS, n), BF16)),
        compiler_params=pltpu.CompilerParams(dimension_semantics=("arbitrary",),
                                             vmem_limit_bytes=VMEM_LIMIT),
        name="kv_compress",
    )(kf, vf, pek, pev, wk1, wk2, wv1, wv2)


def _lane_tiles(x, n):
    return jnp.concatenate([x] * n, axis=1)


def _col_max(s_ref, n_rows, bias_fn):
    groups = ROW_TILE // 8
    mx = [jnp.full((8, s_ref.shape[1]), NEG, F32)] * groups
    for r in range(0, n_rows, ROW_TILE):
        x = s_ref[r:r + ROW_TILE, :]
        if bias_fn is not None:
            x = x + bias_fn(r)
            s_ref[r:r + ROW_TILE, :] = x
        mx = [jnp.maximum(mx[i], x[8 * i:8 * (i + 1), :]) for i in range(groups)]
    while len(mx) > 1:
        mx = [jnp.maximum(a, b) for a, b in zip(mx[0::2], mx[1::2])]
    return jnp.max(mx[0], axis=0, keepdims=True)


def _col_exp2(s_ref, p_ref, n_rows, m, keep_f32=False):
    for r in range(0, n_rows, ROW_TILE):
        p = jnp.exp2(s_ref[r:r + ROW_TILE, :] - m)
        if keep_f32:
            s_ref[r:r + ROW_TILE, :] = p
        p_ref[r:r + ROW_TILE, :] = p.astype(p_ref.dtype)


def _recip_pos(l):
    return 1.0 / jnp.where(l > 0.0, l, 1.0)


def _select_kernel(top_n, qT_ref, kcc_ref, vcT_ref, ovlT_ref, sel_ref, ocmp_ref, sc_ref, pc_ref,
                   ph_ref, pl_ref):
    QB = SEL_BLOCK
    n_cmp = kcc_ref.shape[0]
    n_slc = ovlT_ref.shape[0]
    t0 = pl.program_id(2) * QB
    tq = t0 + lax.broadcasted_iota(jnp.int32, (1, QB), 1)
    rows = lax.broadcasted_iota(jnp.int32, (ROW_TILE, 1), 0)

    qT = qT_ref[...]
    qcat = jnp.concatenate([qT[h * HEAD_DIM:(h + 1) * HEAD_DIM, :] for h in range(HPG)], axis=1)

    def cmp_bias(r):
        cmp_end = (r + rows) * CMP_STRIDE + (CMP_BLOCK - 1)
        return _lane_tiles(jnp.where(cmp_end <= tq, 0.0, NEG), HPG)

    cur = tq // SLC_BLOCK

    def tree(op, xs):
        while len(xs) > 1:
            xs = [op(*xs[i:i + 2]) if i + 1 < len(xs) else xs[i] for i in range(0, len(xs), 2)]
        return xs[0]

    def causal_variant(n_rows):
        rows_cmp = min(n_cmp, n_rows * (SLC_BLOCK // CMP_STRIDE))

        def run():
            sc_ref[0:rows_cmp, :] = _dot(kcc_ref[0:rows_cmp, :], qcat)
            m = jnp.maximum(_col_max(sc_ref, rows_cmp, cmp_bias), M_FLOOR)
            _col_exp2(sc_ref, pc_ref, rows_cmp, m, keep_f32=True)
            o_cmp = _dot(vcT_ref[:, 0:rows_cmp], pc_ref[0:rows_cmp, :])
            rl = _recip_pos(o_cmp[HEAD_DIM:HEAD_DIM + 1, :])
            o_cmp = o_cmp[0:HEAD_DIM, :] * rl
            for h in range(HPG):
                ocmp_ref[h] = o_cmp[:, h * QB:(h + 1) * QB]

            for r in range(0, rows_cmp, ROW_TILE):
                pn = sc_ref[r:r + ROW_TILE, :] * rl
                psum = pn[:, 0:QB]
                for h in range(1, HPG):
                    psum = psum + pn[:, h * QB:(h + 1) * QB]
                hi, lo = _split_bf16(psum)
                ph_ref[r:r + ROW_TILE, :] = hi
                pl_ref[r:r + ROW_TILE, :] = lo
            ovl = ovlT_ref[0:n_rows, 0:rows_cmp]
            imp = _dot(ovl, ph_ref[0:rows_cmp, :]) + _dot(ovl, pl_ref[0:rows_cmp, :])

            blk = lax.broadcasted_iota(jnp.int32, (n_rows, 1), 0)
            future = blk > cur
            forced = (blk == 0) | (blk == cur) | (blk == cur - 1)
            groups = range(n_rows // 8)

            def store(picked):
                sel_ref[0:n_rows, :] = jnp.where(future, NEG, jnp.where(picked, 0.0, NEG)).astype(sel_ref.dtype)

            def with_ties():
                score0 = jnp.where(future, -BIG, jnp.where(forced, BIG, imp))
                score = [score0[8 * i:8 * (i + 1), :] for i in groups]
                cum = jnp.zeros((1, QB), F32)
                thr = jnp.zeros((1, QB), F32)
                above = jnp.zeros((1, QB), F32)
                for _ in range(top_n):
                    best = jnp.max(tree(jnp.maximum, score), axis=0, keepdims=True)
                    eq = [s == best for s in score]
                    unfilled = cum < top_n
                    thr = jnp.where(unfilled, best, thr)
                    above = jnp.where(unfilled, cum, above)
                    cum = cum + jnp.sum(tree(jnp.add, [jnp.where(e, 1.0, 0.0) for e in eq]),
                                        axis=0, keepdims=True)
                    score = [jnp.where(e, -jnp.inf, s) for e, s in zip(eq, score)]
                ties = score0 == thr
                lower = jnp.where(lax.broadcasted_iota(jnp.int32, (1, n_rows), 1) < blk, 1.0, 0.0)
                rank = _dot(lower.astype(BF16), jnp.where(ties, 1.0, 0.0).astype(BF16))
                store((score0 > thr) | (ties & (rank < top_n - above)))

            if top_n > MAX_FORCED:
                n_forced = 1.0 + jnp.where(cur >= 1, 1.0, 0.0) + jnp.where(cur >= 2, 1.0, 0.0)
                wanted = top_n - n_forced
                take_all = (cur.astype(F32) + 1.0 - n_forced) <= wanted
                cand = jnp.where(future | forced, -BIG, imp)
                score = [cand[8 * i:8 * (i + 1), :] for i in groups]
                for _ in range(top_n - MAX_FORCED):
                    cut = jnp.max(tree(jnp.maximum, score), axis=0, keepdims=True)
                    score = [jnp.where(s == cut, -jnp.inf, s) for s in score]
                above_cut = cand >= cut
                count = jnp.sum(jnp.where(above_cut, 1.0, 0.0), axis=0, keepdims=True)
                exact = jnp.min(jnp.where(take_all | (count == wanted), 1.0, 0.0)) > 0.5
                lax.cond(exact, lambda: store(forced | take_all | above_cut), with_ties)
            else:
                with_ties()
            if n_rows < n_slc:
                sel_ref[n_rows:n_slc, :] = jnp.full((n_slc - n_rows, QB), NEG, sel_ref.dtype)
        return run

    steps = n_slc // SEL_ROW_STEP
    need = jnp.minimum((t0 + QB - 1) // (SLC_BLOCK * SEL_ROW_STEP), steps - 1)
    lax.switch(need, [causal_variant(SEL_ROW_STEP * (k + 1)) for k in range(steps)])


def _select_call(top_n, qT, kcc, vcT, ovlT):
    B, _, S = qT.shape
    n_cmp = kcc.shape[2]
    n_slc = ovlT.shape[0]
    gw = HPG * HEAD_DIM
    qw = HPG * SEL_BLOCK
    return pl.pallas_call(
        functools.partial(_select_kernel, top_n),
        grid=(B, N_KV_GROUPS, S // SEL_BLOCK),
        in_specs=[pl.BlockSpec((None, gw, SEL_BLOCK), lambda b, g, i: (b, g, i)),
                  pl.BlockSpec((None, None, n_cmp, HEAD_DIM), lambda b, g, i: (b, g, 0, 0)),
                  pl.BlockSpec((None, None, VT_ROWS, n_cmp), lambda b, g, i: (b, g, 0, 0)),
                  pl.BlockSpec(ovlT.shape, lambda b, g, i: (0, 0))],
        out_specs=(pl.BlockSpec((None, None, n_slc, SEL_BLOCK), lambda b, g, i: (b, g, 0, i)),
                   pl.BlockSpec((None, None, HPG, HEAD_DIM, SEL_BLOCK),
                                lambda b, g, i: (b, g, 0, 0, i))),
        out_shape=(jax.ShapeDtypeStruct((B, N_KV_GROUPS, n_slc, S), BF16),
                   jax.ShapeDtypeStruct((B, N_KV_GROUPS, HPG, HEAD_DIM, S), F32)),
        scratch_shapes=[pltpu.VMEM((n_cmp, qw), F32), pltpu.VMEM((n_cmp, qw), BF16),
                        pltpu.VMEM((n_cmp, SEL_BLOCK), BF16), pltpu.VMEM((n_cmp, SEL_BLOCK), BF16)],
        compiler_params=pltpu.CompilerParams(
            dimension_semantics=("parallel", "parallel", "arbitrary"),
            vmem_limit_bytes=VMEM_LIMIT),
        name="nsa_select",
    )(qT, kcc, vcT, ovlT)


def _attn_kernel(qT_ref, gT_ref, sel_ref, ocmp_ref, ksa_ref, vsT_ref, kw_ref, vwT_ref, ga_ref,
                 o_ref, s0_ref, s1_ref, p0_ref, p1_ref, sw_ref, pw_ref, rhs_ref, acc_ref):
    QB = Q_BLOCK
    n_slc = sel_ref.shape[0]
    t0 = pl.program_id(2) * QB
    tq = t0 + lax.broadcasted_iota(jnp.int32, (1, QB), 1)
    rows = lax.broadcasted_iota(jnp.int32, (ROW_TILE, 1), 0)

    qT = qT_ref[...]
    qcat = jnp.concatenate([qT[h * HEAD_DIM:(h + 1) * HEAD_DIM, :] for h in range(HPG)], axis=1)

    rhs_ref[...] = jnp.concatenate([_lane_tiles(sel_ref[...], HPG), qcat,
                                    jnp.zeros((MXU_DEPTH - n_slc - HEAD_DIM, QW), BF16)], axis=0)

    w0 = pl.multiple_of(jnp.maximum(t0 - WINDOW, 0), QB)
    sw_ref[...] = _dot(kw_ref[pl.ds(w0, WIN_KEYS), :], qcat)

    def win_bias(r):
        dist = tq - (w0 + r + rows)
        return _lane_tiles(jnp.where((dist >= 0) & (dist < WINDOW), 0.0, NEG), HPG)

    KC = KEY_CHUNK
    last = t0 // KC

    def scores(c, dst_ref):
        k0 = pl.multiple_of(jnp.minimum(c, last) * KC, KC)
        dst_ref[...] = _dot(ksa_ref[pl.ds(k0, KC), :], rhs_ref[...])

    def weighted_values(c, p_ref):
        k0 = pl.multiple_of(jnp.clip(c, 0, last) * KC, KC)
        return _dot(vsT_ref[:, pl.ds(k0, KC)], p_ref[...])

    def softmax(c, src_ref, p_ref, m, pending, causal):
        def causal_bias(r):
            return _lane_tiles(jnp.where(c * KC + r + rows <= tq, 0.0, NEG), HPG)

        m_new = jnp.maximum(m, _col_max(src_ref, KC, causal_bias if causal else None))
        acc_ref[...] = jnp.exp2(m - m_new) * (acc_ref[...] + pending)
        _col_exp2(src_ref, p_ref, KC, m_new)
        return m_new

    def pair(i, m, final=False):
        pending = weighted_values(2 * i - 1, p1_ref)
        scores(2 * i + 1, s1_ref)
        m = softmax(2 * i, s0_ref, p0_ref, m, pending, False)
        pending = weighted_values(2 * i, p0_ref)
        if not final:
            scores(2 * i + 2, s0_ref)
        return softmax(2 * i + 1, s1_ref, p1_ref, m, pending, final)

    p1_ref[...] = jnp.zeros_like(p1_ref)
    acc_ref[...] = jnp.zeros_like(acc_ref)
    scores(0, s0_ref)

    m = _col_max(sw_ref, WIN_KEYS, win_bias)
    _col_exp2(sw_ref, pw_ref, WIN_KEYS, m)
    o_win = _dot(vwT_ref[:, pl.ds(w0, WIN_KEYS)], pw_ref[...])
    o_win = o_win[0:HEAD_DIM, :] * (1.0 / o_win[HEAD_DIM:HEAD_DIM + 1, :])

    full_pairs = last // 2
    m = lax.fori_loop(0, full_pairs // 2, lambda j, mm: pair(2 * j + 1, pair(2 * j, mm)),
                      jnp.full((1, QW), NEG, F32))
    m = lax.cond(full_pairs % 2 == 1, lambda mm: pair(full_pairs - 1, mm), lambda mm: mm, m)

    @pl.when(last % 2 == 0)
    def _():
        pending = weighted_values(last - 1, p1_ref)
        softmax(last, s0_ref, p0_ref, m, pending, True)
        acc_ref[...] += weighted_values(last, p0_ref)

    @pl.when(last % 2 == 1)
    def _():
        pair(full_pairs, m, final=True)
        acc_ref[...] += weighted_values(last, p1_ref)

    acc = acc_ref[...]
    o_slc = acc[0:HEAD_DIM, :] * (1.0 / acc[HEAD_DIM:HEAD_DIM + 1, :])

    gts = gT_ref[...]
    outs = []
    for h in range(HPG):
        sl = slice(h * QB, (h + 1) * QB)
        o = (gts[3 * h:3 * h + 1, :] * ocmp_ref[h] + gts[3 * h + 1:3 * h + 2, :] * o_slc[:, sl]
             + gts[3 * h + 2:3 * h + 3, :] * o_win[:, sl])
        o = o * lax.rsqrt(jnp.mean(o * o, axis=0, keepdims=True) + EPS)
        outs.append((o * ga_ref[h * HEAD_DIM:(h + 1) * HEAD_DIM, :]).T)
    o_ref[...] = jnp.concatenate(outs, axis=1).astype(o_ref.dtype)


def _attn_call(qT, gatesT, sel_bias, o_cmp, ksa, vsT, kw, vwT, g_attn_col):
    B, _, S = qT.shape
    n_slc = sel_bias.shape[2]
    gw = HPG * HEAD_DIM
    rows = lambda n, w: pl.BlockSpec((None, None, n, w), lambda b, g, i: (b, g, 0, 0))
    cols = lambda n: pl.BlockSpec((None, None, VT_ROWS, n), lambda b, g, i: (b, g, 0, 0))
    return pl.pallas_call(
        _attn_kernel,
        grid=(B, N_KV_GROUPS, S // Q_BLOCK),
        in_specs=[pl.BlockSpec((None, gw, Q_BLOCK), lambda b, g, i: (b, g, i)),
                  pl.BlockSpec((None, GATE_PAD, Q_BLOCK), lambda b, g, i: (b, g, i)),
                  pl.BlockSpec((None, None, n_slc, Q_BLOCK), lambda b, g, i: (b, g, 0, i)),
                  pl.BlockSpec((None, None, HPG, HEAD_DIM, Q_BLOCK), lambda b, g, i: (b, g, 0, 0, i)),
                  rows(S, MXU_DEPTH), cols(S), rows(S, HEAD_DIM), cols(S),
                  pl.BlockSpec((gw, 1), lambda b, g, i: (g, 0))],
        out_specs=pl.BlockSpec((None, Q_BLOCK, gw), lambda b, g, i: (b, i, g)),
        out_shape=jax.ShapeDtypeStruct((B, S, ATTN_WIDTH), BF16),
        scratch_shapes=[pltpu.VMEM((KEY_CHUNK, QW), F32), pltpu.VMEM((KEY_CHUNK, QW), F32),
                        pltpu.VMEM((KEY_CHUNK, QW), BF16), pltpu.VMEM((KEY_CHUNK, QW), BF16),
                        pltpu.VMEM((WIN_KEYS, QW), F32), pltpu.VMEM((WIN_KEYS, QW), BF16),
                        pltpu.VMEM((MXU_DEPTH, QW), BF16), pltpu.VMEM((VT_ROWS, QW), F32)],
        compiler_params=pltpu.CompilerParams(
            dimension_semantics=("parallel", "parallel", "arbitrary"),
            vmem_limit_bytes=VMEM_LIMIT),
        name="nsa_attention",
    )(qT, gatesT, sel_bias, o_cmp, ksa, vsT, kw, vwT, g_attn_col)


def _out_kernel(x_ref, yc_ref, ya_ref, mod_ref, wo_ref, g_ref, wg_ref, wu_ref, wd_ref, gf_ref,
                o_ref, a_ref, wgb_ref, wub_ref):
    _cast_once((wg_ref, wu_ref), (wgb_ref, wub_ref))
    mix = _dot(yc_ref[...], wo_ref[0:CONV_CH, :]) + _dot(ya_ref[...], wo_ref[CONV_CH:, :])
    x = x_ref[...] + mod_ref[5:6, :] * mix
    x = _ffn_core(x, mod_ref[6:7, :], mod_ref[7:8, :], mod_ref[8:9, :], g_ref[...], wgb_ref,
                  wub_ref, wd_ref, a_ref)
    o_ref[...] = _rms(x, gf_ref[...])


def _out_call(x, yc, ya, mod, wo, g, wg, wu, wd, gf):
    B, S, D = x.shape
    tm = FFN_TOKEN_TILE
    tok = lambda w: pl.BlockSpec((None, tm, w), lambda b, i: (b, i, 0))
    return pl.pallas_call(
        _out_kernel,
        grid=(B, S // tm),
        in_specs=[tok(D), tok(CONV_CH), tok(ATTN_WIDTH),
                  pl.BlockSpec((None, N_MOD, D), lambda b, i: (b, 0, 0)),
                  _const_spec(wo.shape), _const_spec(g.shape), _const_spec(wg.shape),
                  _const_spec(wu.shape), _const_spec(wd.shape), _const_spec(gf.shape)],
        out_specs=tok(D),
        out_shape=jax.ShapeDtypeStruct((B, S, D), F32),
        scratch_shapes=[pltpu.VMEM((tm, wg.shape[1]), BF16), pltpu.VMEM(wg.shape, BF16),
                        pltpu.VMEM(wu.shape, BF16)],
        compiler_params=pltpu.CompilerParams(dimension_semantics=("arbitrary", "arbitrary"),
                                             vmem_limit_bytes=VMEM_LIMIT),
        name="outproj_ffn2",
    )(x, yc, ya, mod, wo, g, wg, wu, wd, gf)


def kernel(x, c, positions, w_ada, b_ada, g_ffn1, w1_gate, w1_up, w1_down, g_mix, w_in, conv_w, cmp_pos_k, cmp_pos_v, w_cmpk1, w_cmpk2, w_cmpv1, w_cmpv2, g_out_conv, g_out_attn, w_out, g_ffn2, w2_gate, w2_up, w2_down, g_final):
    B, S, D = x.shape
    depth = w_ada.shape[0]
    n_slc = S // SLC_BLOCK
    half = CMP_BLOCK // 2
    n_half = S // half
    assert n_slc <= LANES, "selection-block one-hot is one lane tile wide"

    c_pad = jnp.pad(c, ((0, 8 - B), (0, 0)))
    row = lambda a: a.reshape(1, -1)

    freq_half = jnp.power(ROPE_THETA, -2.0 * jnp.arange(ROT_HALF, dtype=F32) / ROT_DIM)
    freq = jnp.tile(freq_half, LANES // ROT_HALF).reshape(1, LANES)
    gidx = np.arange(CONV_CH) // (CONV_CH // CONV_GROUPS)
    gmat = jnp.asarray((gidx[:, None] == gidx[None, :]) / (CONV_CH // CONV_GROUPS), dtype=BF16)
    c0 = np.arange(n_half) * CMP_STRIDE
    s0 = np.arange(LANES) * SLC_BLOCK
    ovlT = ((c0[None, :] <= s0[:, None] + SLC_BLOCK - 1) & (c0[None, :] + CMP_BLOCK - 1 >= s0[:, None]))
    ovlT = jnp.asarray(ovlT, dtype=BF16)
    onehot = jnp.asarray((np.arange(S) // SLC_BLOCK)[:, None] == np.arange(LANES)[None, :], dtype=BF16)
    cos_p, sin_p = _rope_table_call(positions.reshape(-1, TOKENS_PER_ROW), freq)
    cos_p = cos_p.reshape(B, S // TOKENS_PER_ROW, LANES)
    sin_p = sin_p.reshape(B, S // TOKENS_PER_ROW, LANES)

    for l in range(depth):
        mod = _ada_call(c_pad, w_ada[l], row(b_ada[l]))[:B].reshape(B, N_MOD, D)

        x = _ffn_call(x, mod, row(g_ffn1[l]), w1_gate[l], w1_up[l], w1_down[l].astype(BF16))

        n_main = w_in.shape[2] - N_KV_GROUPS * 3 * HPG
        gate_cols = [jnp.pad(w_in[l][:, n_main + g * 3 * HPG:n_main + (g + 1) * 3 * HPG],
                             ((0, 0), (0, GATE_PAD - 3 * HPG))) for g in range(N_KV_GROUPS)]
        wgate = jnp.concatenate(gate_cols, axis=1).astype(BF16)
        win = w_in[l].astype(BF16)
        (yc, qT, kc, vc, ksa, vsT, kw, vwT, gatesT) = _inproj_call(
            x, cos_p, sin_p, mod, row(g_mix[l]), win, wgate, conv_w[l], row(g_out_conv[l]), gmat,
            onehot)

        both_groups = lambda pe: jnp.tile(pe, (1, N_KV_GROUPS))
        kcc, vcT = _compress_call(kc, vc, both_groups(cmp_pos_k[l]), both_groups(cmp_pos_v[l]),
                                  w_cmpk1[l], w_cmpk2[l], w_cmpv1[l], w_cmpv2[l])

        sel_bias, o_cmp = _select_call(min(SLC_TOP_N, n_slc), qT, kcc, vcT, ovlT)
        ya = _attn_call(qT, gatesT, sel_bias, o_cmp, ksa, vsT, kw, vwT,
                        g_out_attn[l].reshape(ATTN_WIDTH, 1))

        assert l == depth - 1, "final norm is fused into the last layer's output kernel"
        x = _out_call(x, yc, ya, mod, w_out[l].astype(BF16), row(g_ffn2[l]),
                      w2_gate[l], w2_up[l], w2_down[l].astype(BF16),
                      row(g_final))
    return x
```

```python
import functools
import math

import numpy as np
import jax
import jax.numpy as jnp
from jax import lax
from jax.experimental import pallas as pl
from jax.experimental.pallas import tpu as pltpu

F32 = jnp.float32
BF16 = jnp.bfloat16

CONV_CH = 512
CONV_GROUPS = 8
N_HEADS = 8
N_KV_GROUPS = 2
HPG = N_HEADS // N_KV_GROUPS
HEAD_DIM = 64
ATTN_WIDTH = N_HEADS * HEAD_DIM
KV_WIDTH = N_KV_GROUPS * HEAD_DIM
ROPE_THETA = 500000.0
ROT_DIM = HEAD_DIM // 4
ROT_HALF = ROT_DIM // 2
CMP_BLOCK = 32
CMP_STRIDE = 16
CMP_HIDDEN = 256
SLC_BLOCK = 64
SLC_TOP_N = 16
MAX_FORCED = 3
WINDOW = 512
Q_BLOCK = 256
MACARON_W = 0.5
N_MOD = 9
EPS = 1e-6
NEG = -1e30
BIG = 1e9

LANES = 128
MXU_DEPTH = 256
VMEM_LIMIT = 58 * 1024 * 1024

TOKEN_TILE = 512
FFN_TOKEN_TILE = 512
ADA_COL_TILE = 1024
FF_TILE = 256
KEY_CHUNK = 512
WIN_KEYS = WINDOW + Q_BLOCK
TOKENS_PER_ROW = LANES // ROT_HALF
GATE_PAD = LANES
QW = HPG * Q_BLOCK
SEL_BLOCK = 1024
SEL_ROW_STEP = 32
ONES_ROWS = 16
VT_ROWS = HEAD_DIM + ONES_ROWS
ROW_TILE = 16
Q_SCALE = HEAD_DIM ** -0.5 * math.log2(math.e)
M_FLOOR = -1e20


def _dot(a, b):
    return jnp.dot(a, b, preferred_element_type=F32)


def _rms(x, g):
    return x * lax.rsqrt(jnp.mean(x * x, axis=-1, keepdims=True) + EPS) * g


def _split_bf16(x):
    hi = x.astype(BF16)
    lo = (x - hi.astype(F32)).astype(BF16)
    return hi, lo


def _const_spec(shape):
    nd = len(shape)
    return pl.BlockSpec(shape, lambda *_: (0,) * nd, pipeline_mode=pl.Buffered(1))


def _ada_kernel(c_ref, w_ref, b_ref, o_ref):
    c = c_ref[...]
    c_act = c * jax.nn.sigmoid(c)
    o_ref[...] = _dot(c_act.astype(BF16), w_ref[...].astype(BF16)) + b_ref[...]


def _ada_call(c_pad, w_ada, b_ada):
    rows, d = c_pad.shape
    n = w_ada.shape[1]
    tn = ADA_COL_TILE
    return pl.pallas_call(
        _ada_kernel,
        grid=(n // tn,),
        in_specs=[pl.BlockSpec((rows, d), lambda j: (0, 0)),
                  pl.BlockSpec((d, tn), lambda j: (0, j)),
                  pl.BlockSpec((1, tn), lambda j: (0, j))],
        out_specs=pl.BlockSpec((rows, tn), lambda j: (0, j)),
        out_shape=jax.ShapeDtypeStruct((rows, n), F32),
        compiler_params=pltpu.CompilerParams(dimension_semantics=("arbitrary",),
                                             vmem_limit_bytes=VMEM_LIMIT),
        name="adaln_mod",
    )(c_pad, w_ada, b_ada)


def _ffn_core(x, shift, scale, gate, g, wg_ref, wu_ref, wd_ref, a_ref):
    h = _rms(x, g) * (1.0 + scale) + shift
    hb = h.astype(BF16)
    d_ff = wg_ref.shape[1]
    for j in range(d_ff // FF_TILE):
        sl = slice(j * FF_TILE, (j + 1) * FF_TILE)
        gg = _dot(hb, wg_ref[:, sl])
        uu = _dot(hb, wu_ref[:, sl])
        a_ref[:, sl] = (gg * jax.nn.sigmoid(gg) * uu).astype(a_ref.dtype)
    return x + (MACARON_W * gate) * _dot(a_ref[...], wd_ref[...])


def _cast_once(src_refs, dst_refs):
    @pl.when((pl.program_id(0) == 0) & (pl.program_id(1) == 0))
    def _():
        for src, dst in zip(src_refs, dst_refs):
            for c in range(0, dst.shape[1], FF_TILE):
                dst[:, c:c + FF_TILE] = src[:, c:c + FF_TILE].astype(dst.dtype)


def _ffn_kernel(x_ref, mod_ref, g_ref, wg_ref, wu_ref, wd_ref, o_ref, a_ref, wgb_ref, wub_ref):
    _cast_once((wg_ref, wu_ref), (wgb_ref, wub_ref))
    o_ref[...] = _ffn_core(x_ref[...], mod_ref[0:1, :], mod_ref[1:2, :], mod_ref[2:3, :],
                           g_ref[...], wgb_ref, wub_ref, wd_ref, a_ref)


def _ffn_call(x, mod, g, wg, wu, wd):
    B, S, D = x.shape
    tm = FFN_TOKEN_TILE
    return pl.pallas_call(
        _ffn_kernel,
        grid=(B, S // tm),
        in_specs=[pl.BlockSpec((None, tm, D), lambda b, i: (b, i, 0)),
                  pl.BlockSpec((None, N_MOD, D), lambda b, i: (b, 0, 0)),
                  _const_spec(g.shape), _const_spec(wg.shape), _const_spec(wu.shape),
                  _const_spec(wd.shape)],
        out_specs=pl.BlockSpec((None, tm, D), lambda b, i: (b, i, 0)),
        out_shape=jax.ShapeDtypeStruct((B, S, D), F32),
        scratch_shapes=[pltpu.VMEM((tm, wg.shape[1]), BF16), pltpu.VMEM(wg.shape, BF16),
                        pltpu.VMEM(wu.shape, BF16)],
        compiler_params=pltpu.CompilerParams(dimension_semantics=("arbitrary", "arbitrary"),
                                             vmem_limit_bytes=VMEM_LIMIT),
        name="ffn1",
    )(x, mod, g, wg, wu, wd)


def _rope_table_kernel(pos_ref, freq_ref, cos_ref, sin_ref):
    pos = jnp.concatenate([pos_ref[...].astype(F32),
                           jnp.zeros((pos_ref.shape[0], LANES - TOKENS_PER_ROW), F32)], axis=1)
    src = lax.broadcasted_iota(jnp.int32, pos.shape, 1) // ROT_HALF
    ang = jnp.take_along_axis(pos, src, axis=1) * freq_ref[...]
    cos_ref[...] = jnp.cos(ang)
    sin_ref[...] = jnp.sin(ang)


def _rope_table_call(pos_rows, freq):
    rows = pos_rows.shape[0]
    shape = jax.ShapeDtypeStruct((rows, LANES), F32)
    full = pl.BlockSpec((rows, LANES), lambda: (0, 0))
    return pl.pallas_call(
        _rope_table_kernel,
        in_specs=[pl.BlockSpec(pos_rows.shape, lambda: (0, 0)), pl.BlockSpec(freq.shape, lambda: (0, 0))],
        out_specs=(full, full),
        out_shape=(shape, shape),
        compiler_params=pltpu.CompilerParams(vmem_limit_bytes=VMEM_LIMIT),
        name="rope_table",
    )(pos_rows, freq)


def _inproj_kernel(x_ref, cosp_ref, sinp_ref, mod_ref, gmix_ref, win_ref, wgate_ref, convw_ref, gconv_ref,
                   gmat_ref, onehot_ref, yc_ref, qT_ref, kc_ref, vc_ref, ksa_ref, vsT_ref, kw_ref,
                   vwT_ref, gT_ref, carry_ref, winb_ref):
    tm = x_ref.shape[0]
    _cast_once((win_ref,), (winb_ref,))

    @pl.when(pl.program_id(1) == 0)
    def _():
        carry_ref[...] = jnp.zeros_like(carry_ref)

    x = x_ref[...]
    h = _rms(x, gmix_ref[...]) * (1.0 + mod_ref[4:5, :]) + mod_ref[3:4, :]
    hb = h.astype(BF16)

    def proj(c0, width):
        return _dot(hb, winb_ref[:, c0:c0 + width])

    d = lax.broadcasted_iota(jnp.int32, (tm, LANES), 1) & (HEAD_DIM - 1)
    token = lax.broadcasted_iota(jnp.int32, (tm, LANES), 0) & (TOKENS_PER_ROW - 1)
    src_lane = token * ROT_HALF + (d & (ROT_HALF - 1))

    def unpack(packed_ref):
        rows = jnp.broadcast_to(packed_ref[...][:, None, :], (tm // TOKENS_PER_ROW, TOKENS_PER_ROW, LANES))
        return jnp.take_along_axis(rows.reshape(tm, LANES), src_lane, axis=1)

    cos_t = jnp.where(d < ROT_DIM, unpack(cosp_ref), 1.0)
    sin_raw = unpack(sinp_ref)
    sin_t = jnp.where(d < ROT_HALF, -sin_raw, jnp.where(d < ROT_DIM, sin_raw, 0.0))
    first_half = d < ROT_HALF

    def rope(t):
        outs = []
        for j in range(t.shape[1] // LANES):
            tj = t[:, j * LANES:(j + 1) * LANES]
            partner = jnp.where(first_half, pltpu.roll(tj, LANES - ROT_HALF, 1),
                                pltpu.roll(tj, ROT_HALF, 1))
            outs.append(tj * cos_t + partner * sin_t)
        return outs[0] if len(outs) == 1 else jnp.concatenate(outs, axis=1)

    c0 = 3 * CONV_CH
    q = rope(proj(c0, ATTN_WIDTH)) * Q_SCALE
    qT_ref[...] = q.T.astype(qT_ref.dtype)
    c0 += ATTN_WIDTH
    kv = proj(c0, 2 * KV_WIDTH)
    kc_ref[...] = rope(kv[:, :KV_WIDTH])
    vc_ref[...] = kv[:, KV_WIDTH:]
    kv = proj(c0 + 2 * KV_WIDTH, 2 * KV_WIDTH)
    ks = rope(kv[:, :KV_WIDTH]).astype(BF16)
    vsT = kv[:, KV_WIDTH:].T.astype(BF16)
    kv = proj(c0 + 4 * KV_WIDTH, 2 * KV_WIDTH)
    kw = rope(kv[:, :KV_WIDTH]).astype(BF16)
    vwT = kv[:, KV_WIDTH:].T.astype(BF16)
    pad = jnp.zeros((tm, MXU_DEPTH - LANES - HEAD_DIM), BF16)
    ones = jnp.ones((ONES_ROWS, tm), BF16)
    for g in range(N_KV_GROUPS):
        kg = ks[:, g * HEAD_DIM:(g + 1) * HEAD_DIM]
        ksa_ref[g] = jnp.concatenate([onehot_ref[...], kg, pad], axis=1)
        kw_ref[g] = kw[:, g * HEAD_DIM:(g + 1) * HEAD_DIM]
        vsT_ref[g] = jnp.concatenate([vsT[g * HEAD_DIM:(g + 1) * HEAD_DIM, :], ones], axis=0)
        vwT_ref[g] = jnp.concatenate([vwT[g * HEAD_DIM:(g + 1) * HEAD_DIM, :], ones], axis=0)
    gT_ref[...] = jax.nn.sigmoid(_dot(hb, wgate_ref[...])).T

    cb = proj(0, CONV_CH)
    u = proj(CONV_CH, CONV_CH) * proj(2 * CONV_CH, CONV_CH)
    row = lax.broadcasted_iota(jnp.int32, (tm, 1), 0)
    prev1 = carry_ref[7:8, :]
    prev2 = carry_ref[6:7, :]
    u1 = jnp.where(row >= 1, pltpu.roll(u, 1, 0), prev1)
    u2 = jnp.where(row >= 2, pltpu.roll(u, 2, 0), jnp.where(row == 1, prev1, prev2))
    carry_ref[...] = u[tm - 8:tm, :]
    v = convw_ref[0:1, :] * u2 + convw_ref[1:2, :] * u1 + convw_ref[2:3, :] * u
    y = cb * v
    hi, lo = _split_bf16(y * y)
    ms = _dot(hi, gmat_ref[...]) + _dot(lo, gmat_ref[...])
    yc_ref[...] = (y * lax.rsqrt(ms + EPS) * gconv_ref[...]).astype(yc_ref.dtype)


def _inproj_call(x, cos_p, sin_p, mod, gmix, win, wgate, convw, gconv, gmat, onehot):
    B, S, D = x.shape
    tm = TOKEN_TILE
    tok = lambda w: pl.BlockSpec((None, tm, w), lambda b, i: (b, i, 0))
    tr = lambda w: pl.BlockSpec((None, w, tm), lambda b, i: (b, 0, i))
    grp = lambda w: pl.BlockSpec((None, N_KV_GROUPS, tm, w), lambda b, i: (b, 0, i, 0))
    grpT = pl.BlockSpec((None, N_KV_GROUPS, VT_ROWS, tm), lambda b, i: (b, 0, 0, i))
    packed = pl.BlockSpec((None, tm // TOKENS_PER_ROW, LANES), lambda b, i: (b, i, 0))
    out_shapes = (
        jax.ShapeDtypeStruct((B, S, CONV_CH), BF16),
        jax.ShapeDtypeStruct((B, ATTN_WIDTH, S), BF16),
        jax.ShapeDtypeStruct((B, S, KV_WIDTH), F32),
        jax.ShapeDtypeStruct((B, S, KV_WIDTH), F32),
        jax.ShapeDtypeStruct((B, N_KV_GROUPS, S, MXU_DEPTH), BF16),
        jax.ShapeDtypeStruct((B, N_KV_GROUPS, VT_ROWS, S), BF16),
        jax.ShapeDtypeStruct((B, N_KV_GROUPS, S, HEAD_DIM), BF16),
        jax.ShapeDtypeStruct((B, N_KV_GROUPS, VT_ROWS, S), BF16),
        jax.ShapeDtypeStruct((B, N_KV_GROUPS * GATE_PAD, S), F32),
    )
    return pl.pallas_call(
        _inproj_kernel,
        grid=(B, S // tm),
        in_specs=[tok(D),
                  packed, packed,
                  pl.BlockSpec((None, N_MOD, D), lambda b, i: (b, 0, 0)),
                  _const_spec(gmix.shape), _const_spec(win.shape), _const_spec(wgate.shape),
                  _const_spec(convw.shape),
                  _const_spec(gconv.shape), _const_spec(gmat.shape),
                  pl.BlockSpec((tm, onehot.shape[1]), lambda b, i: (i, 0))],
        out_specs=(tok(CONV_CH), tr(ATTN_WIDTH), tok(KV_WIDTH), tok(KV_WIDTH), grp(MXU_DEPTH),
                   grpT, grp(HEAD_DIM), grpT, tr(N_KV_GROUPS * GATE_PAD)),
        out_shape=out_shapes,
        scratch_shapes=[pltpu.VMEM((8, CONV_CH), F32),
                        pltpu.VMEM((D, 3 * CONV_CH + ATTN_WIDTH + 6 * KV_WIDTH), BF16)],
        compiler_params=pltpu.CompilerParams(dimension_semantics=("arbitrary", "arbitrary"),
                                             vmem_limit_bytes=VMEM_LIMIT),
        name="mixer_inproj",
    )(x, cos_p, sin_p, mod, gmix, win, wgate, convw, gconv, gmat, onehot)


def _compress_kernel(kf_ref, vf_ref, pek_ref, pev_ref, wk1_ref, wk2_ref, wv1_ref, wv2_ref,
                     kcc_ref, vcT_ref):
    half = CMP_BLOCK // 2
    n = kf_ref.shape[0] // half

    def mlp(x_ref, pe_ref, w1_ref, w2_ref):
        parts = []
        for p in range(2):
            acc = None
            for l0 in range(0, half, 2):
                xs, ws = [], []
                for l in (l0, l0 + 1):
                    row = p * half + l
                    xs.append((x_ref[pl.ds(l, n, stride=half), :]
                               + pe_ref[row:row + 1, :]).astype(BF16))
                    w = w1_ref[row * HEAD_DIM:(row + 1) * HEAD_DIM, :].astype(BF16)
                    z = jnp.zeros_like(w)
                    ws.append(jnp.concatenate([jnp.concatenate([w, z], axis=1),
                                               jnp.concatenate([z, w], axis=1)], axis=0))
                d = _dot(jnp.concatenate(xs, axis=1), jnp.concatenate(ws, axis=0))
                acc = d if acc is None else acc + d
            parts.append(acc)
        hpre = parts[0] + pltpu.roll(parts[1], n - 1, 0)
        hid = (hpre * jax.nn.sigmoid(hpre)).astype(BF16)
        w2 = w2_ref[...].astype(BF16)
        return jnp.concatenate([_dot(hid[:, g * CMP_HIDDEN:(g + 1) * CMP_HIDDEN], w2)
                                for g in range(N_KV_GROUPS)], axis=1)

    kc = mlp(kf_ref, pek_ref, wk1_ref, wk2_ref).astype(kcc_ref.dtype)
    for g in range(N_KV_GROUPS):
        kcc_ref[g] = kc[:, g * HEAD_DIM:(g + 1) * HEAD_DIM]
    vcT = mlp(vf_ref, pev_ref, wv1_ref, wv2_ref).T.astype(vcT_ref.dtype)
    ones = jnp.ones((ONES_ROWS, n), vcT_ref.dtype)
    for g in range(N_KV_GROUPS):
        vcT_ref[g] = jnp.concatenate([vcT[g * HEAD_DIM:(g + 1) * HEAD_DIM, :], ones], axis=0)


def _compress_call(kf, vf, pek, pev, wk1, wk2, wv1, wv2):
    B, S, width = kf.shape
    n = S // (CMP_BLOCK // 2)
    flat = pl.BlockSpec((None, S, width), lambda b: (b, 0, 0))
    return pl.pallas_call(
        _compress_kernel,
        grid=(B,),
        in_specs=[flat, flat, _const_spec(pek.shape), _const_spec(pev.shape),
                  _const_spec(wk1.shape), _const_spec(wk2.shape), _const_spec(wv1.shape),
                  _const_spec(wv2.shape)],
        out_specs=(pl.BlockSpec((None, N_KV_GROUPS, n, HEAD_DIM), lambda b: (b, 0, 0, 0)),
                   pl.BlockSpec((None, N_KV_GROUPS, VT_ROWS, n), lambda b: (b, 0, 0, 0))),
        out_shape=(jax.ShapeDtypeStruct((B, N_KV_GROUPS, n, HEAD_DIM), BF16),
                   jax.ShapeDtypeStruct((B, N_KV_GROUPS, VT_ROWS, n), BF16)),
        compiler_params=pltpu.CompilerParams(dimension_semantics=("arbitrary",),
                                             vmem_limit_bytes=VMEM_LIMIT),
        name="kv_compress",
    )(kf, vf, pek, pev, wk1, wk2, wv1, wv2)


def _lane_tiles(x, n):
    return jnp.concatenate([x] * n, axis=1)


def _col_max(s_ref, n_rows, bias_fn):
    groups = ROW_TILE // 8
    mx = [jnp.full((8, s_ref.shape[1]), NEG, F32)] * groups
    for r in range(0, n_rows, ROW_TILE):
        x = s_ref[r:r + ROW_TILE, :]
        if bias_fn is not None:
            x = x + bias_fn(r)
            s_ref[r:r + ROW_TILE, :] = x
        mx = [jnp.maximum(mx[i], x[8 * i:8 * (i + 1), :]) for i in range(groups)]
    while len(mx) > 1:
        mx = [jnp.maximum(a, b) for a, b in zip(mx[0::2], mx[1::2])]
    return jnp.max(mx[0], axis=0, keepdims=True)


def _col_exp2(s_ref, p_ref, n_rows, m, keep_f32=False):
    for r in range(0, n_rows, ROW_TILE):
        p = jnp.exp2(s_ref[r:r + ROW_TILE, :] - m)
        if keep_f32:
            s_ref[r:r + ROW_TILE, :] = p
        p_ref[r:r + ROW_TILE, :] = p.astype(p_ref.dtype)


def _recip_pos(l):
    return 1.0 / jnp.where(l > 0.0, l, 1.0)


def _select_kernel(top_n, qT_ref, kcc_ref, vcT_ref, ovlT_ref, sel_ref, ocmp_ref, sc_ref, pc_ref,
                   ph_ref, pl_ref):
    QB = SEL_BLOCK
    n_cmp = kcc_ref.shape[0]
    n_slc = ovlT_ref.shape[0]
    t0 = pl.program_id(2) * QB
    tq = t0 + lax.broadcasted_iota(jnp.int32, (1, QB), 1)
    rows = lax.broadcasted_iota(jnp.int32, (ROW_TILE, 1), 0)

    qT = qT_ref[...]
    qcat = jnp.concatenate([qT[h * HEAD_DIM:(h + 1) * HEAD_DIM, :] for h in range(HPG)], axis=1)

    def cmp_bias(r):
        cmp_end = (r + rows) * CMP_STRIDE + (CMP_BLOCK - 1)
        return _lane_tiles(jnp.where(cmp_end <= tq, 0.0, NEG), HPG)

    cur = tq // SLC_BLOCK

    def tree(op, xs):
        while len(xs) > 1:
            xs = [op(*xs[i:i + 2]) if i + 1 < len(xs) else xs[i] for i in range(0, len(xs), 2)]
        return xs[0]

    def causal_variant(n_rows):
        rows_cmp = min(n_cmp, n_rows * (SLC_BLOCK // CMP_STRIDE))

        def run():
            sc_ref[0:rows_cmp, :] = _dot(kcc_ref[0:rows_cmp, :], qcat)
            m = jnp.maximum(_col_max(sc_ref, rows_cmp, cmp_bias), M_FLOOR)
            _col_exp2(sc_ref, pc_ref, rows_cmp, m, keep_f32=True)
            o_cmp = _dot(vcT_ref[:, 0:rows_cmp], pc_ref[0:rows_cmp, :])
            rl = _recip_pos(o_cmp[HEAD_DIM:HEAD_DIM + 1, :])
            o_cmp = o_cmp[0:HEAD_DIM, :] * rl
            for h in range(HPG):
                ocmp_ref[h] = o_cmp[:, h * QB:(h + 1) * QB]

            for r in range(0, rows_cmp, ROW_TILE):
                pn = sc_ref[r:r + ROW_TILE, :] * rl
                psum = pn[:, 0:QB]
                for h in range(1, HPG):
                    psum = psum + pn[:, h * QB:(h + 1) * QB]
                hi, lo = _split_bf16(psum)
                ph_ref[r:r + ROW_TILE, :] = hi
                pl_ref[r:r + ROW_TILE, :] = lo
            ovl = ovlT_ref[0:n_rows, 0:rows_cmp]
            imp = _dot(ovl, ph_ref[0:rows_cmp, :]) + _dot(ovl, pl_ref[0:rows_cmp, :])

            blk = lax.broadcasted_iota(jnp.int32, (n_rows, 1), 0)
            future = blk > cur
            forced = (blk == 0) | (blk == cur) | (blk == cur - 1)
            groups = range(n_rows // 8)

            def store(picked):
                sel_ref[0:n_rows, :] = jnp.where(future, NEG, jnp.where(picked, 0.0, NEG)).astype(sel_ref.dtype)

            def with_ties():
                score0 = jnp.where(future, -BIG, jnp.where(forced, BIG, imp))
                score = [score0[8 * i:8 * (i + 1), :] for i in groups]
                cum = jnp.zeros((1, QB), F32)
                thr = jnp.zeros((1, QB), F32)
                above = jnp.zeros((1, QB), F32)
                for _ in range(top_n):
                    best = jnp.max(tree(jnp.maximum, score), axis=0, keepdims=True)
                    eq = [s == best for s in score]
                    unfilled = cum < top_n
                    thr = jnp.where(unfilled, best, thr)
                    above = jnp.where(unfilled, cum, above)
                    cum = cum + jnp.sum(tree(jnp.add, [jnp.where(e, 1.0, 0.0) for e in eq]),
                                        axis=0, keepdims=True)
                    score = [jnp.where(e, -jnp.inf, s) for e, s in zip(eq, score)]
                ties = score0 == thr
                lower = jnp.where(lax.broadcasted_iota(jnp.int32, (1, n_rows), 1) < blk, 1.0, 0.0)
                rank = _dot(lower.astype(BF16), jnp.where(ties, 1.0, 0.0).astype(BF16))
                store((score0 > thr) | (ties & (rank < top_n - above)))

            if top_n > MAX_FORCED:
                n_forced = 1.0 + jnp.where(cur >= 1, 1.0, 0.0) + jnp.where(cur >= 2, 1.0, 0.0)
                wanted = top_n - n_forced
                take_all = (cur.astype(F32) + 1.0 - n_forced) <= wanted
                cand = jnp.where(future | forced, -BIG, imp)
                score = [cand[8 * i:8 * (i + 1), :] for i in groups]
                for _ in range(top_n - MAX_FORCED):
                    cut = jnp.max(tree(jnp.maximum, score), axis=0, keepdims=True)
                    score = [jnp.where(s == cut, -jnp.inf, s) for s in score]
                above_cut = cand >= cut
                count = jnp.sum(jnp.where(above_cut, 1.0, 0.0), axis=0, keepdims=True)
                exact = jnp.min(jnp.where(take_all | (count == wanted), 1.0, 0.0)) > 0.5
                lax.cond(exact, lambda: store(forced | take_all | above_cut), with_ties)
            else:
                with_ties()
            if n_rows < n_slc:
                sel_ref[n_rows:n_slc, :] = jnp.full((n_slc - n_rows, QB), NEG, sel_ref.dtype)
        return run

    steps = n_slc // SEL_ROW_STEP
    need = jnp.minimum((t0 + QB - 1) // (SLC_BLOCK * SEL_ROW_STEP), steps - 1)
    lax.switch(need, [causal_variant(SEL_ROW_STEP * (k + 1)) for k in range(steps)])


def _select_call(top_n, qT, kcc, vcT, ovlT):
    B, _, S = qT.shape
    n_cmp = kcc.shape[2]
    n_slc = ovlT.shape[0]
    gw = HPG * HEAD_DIM
    qw = HPG * SEL_BLOCK
    return pl.pallas_call(
        functools.partial(_select_kernel, top_n),
        grid=(B, N_KV_GROUPS, S // SEL_BLOCK),
        in_specs=[pl.BlockSpec((None, gw, SEL_BLOCK), lambda b, g, i: (b, g, i)),
                  pl.BlockSpec((None, None, n_cmp, HEAD_DIM), lambda b, g, i: (b, g, 0, 0)),
                  pl.BlockSpec((None, None, VT_ROWS, n_cmp), lambda b, g, i: (b, g, 0, 0)),
                  pl.BlockSpec(ovlT.shape, lambda b, g, i: (0, 0))],
        out_specs=(pl.BlockSpec((None, None, n_slc, SEL_BLOCK), lambda b, g, i: (b, g, 0, i)),
                   pl.BlockSpec((None, None, HPG, HEAD_DIM, SEL_BLOCK),
                                lambda b, g, i: (b, g, 0, 0, i))),
        out_shape=(jax.ShapeDtypeStruct((B, N_KV_GROUPS, n_slc, S), BF16),
                   jax.ShapeDtypeStruct((B, N_KV_GROUPS, HPG, HEAD_DIM, S), F32)),
        scratch_shapes=[pltpu.VMEM((n_cmp, qw), F32), pltpu.VMEM((n_cmp, qw), BF16),
                        pltpu.VMEM((n_cmp, SEL_BLOCK), BF16), pltpu.VMEM((n_cmp, SEL_BLOCK), BF16)],
        compiler_params=pltpu.CompilerParams(
            dimension_semantics=("parallel", "parallel", "arbitrary"),
            vmem_limit_bytes=VMEM_LIMIT),
        name="nsa_select",
    )(qT, kcc, vcT, ovlT)


def _attn_kernel(qT_ref, gT_ref, sel_ref, ocmp_ref, ksa_ref, vsT_ref, kw_ref, vwT_ref, ga_ref,
                 o_ref, s0_ref, s1_ref, p0_ref, p1_ref, sw_ref, pw_ref, rhs_ref, acc_ref):
    QB = Q_BLOCK
    n_slc = sel_ref.shape[0]
    t0 = pl.program_id(2) * QB
    tq = t0 + lax.broadcasted_iota(jnp.int32, (1, QB), 1)
    rows = lax.broadcasted_iota(jnp.int32, (ROW_TILE, 1), 0)

    qT = qT_ref[...]
    qcat = jnp.concatenate([qT[h * HEAD_DIM:(h + 1) * HEAD_DIM, :] for h in range(HPG)], axis=1)

    rhs_ref[...] = jnp.concatenate([_lane_tiles(sel_ref[...], HPG), qcat,
                                    jnp.zeros((MXU_DEPTH - n_slc - HEAD_DIM, QW), BF16)], axis=0)

    w0 = pl.multiple_of(jnp.maximum(t0 - WINDOW, 0), QB)
    sw_ref[...] = _dot(kw_ref[pl.ds(w0, WIN_KEYS), :], qcat)

    def win_bias(r):
        dist = tq - (w0 + r + rows)
        return _lane_tiles(jnp.where((dist >= 0) & (dist < WINDOW), 0.0, NEG), HPG)

    KC = KEY_CHUNK
    last = t0 // KC

    def scores(c, dst_ref):
        k0 = pl.multiple_of(jnp.minimum(c, last) * KC, KC)
        dst_ref[...] = _dot(ksa_ref[pl.ds(k0, KC), :], rhs_ref[...])

    def weighted_values(c, p_ref):
        k0 = pl.multiple_of(jnp.clip(c, 0, last) * KC, KC)
        return _dot(vsT_ref[:, pl.ds(k0, KC)], p_ref[...])

    def softmax(c, src_ref, p_ref, m, pending, causal):
        def causal_bias(r):
            return _lane_tiles(jnp.where(c * KC + r + rows <= tq, 0.0, NEG), HPG)

        m_new = jnp.maximum(m, _col_max(src_ref, KC, causal_bias if causal else None))
        acc_ref[...] = jnp.exp2(m - m_new) * (acc_ref[...] + pending)
        _col_exp2(src_ref, p_ref, KC, m_new)
        return m_new

    def pair(i, m, final=False):
        pending = weighted_values(2 * i - 1, p1_ref)
        scores(2 * i + 1, s1_ref)
        m = softmax(2 * i, s0_ref, p0_ref, m, pending, False)
        pending = weighted_values(2 * i, p0_ref)
        if not final:
            scores(2 * i + 2, s0_ref)
        return softmax(2 * i + 1, s1_ref, p1_ref, m, pending, final)

    p1_ref[...] = jnp.zeros_like(p1_ref)
    acc_ref[...] = jnp.zeros_like(acc_ref)
    scores(0, s0_ref)

    m = _col_max(sw_ref, WIN_KEYS, win_bias)
    _col_exp2(sw_ref, pw_ref, WIN_KEYS, m)
    o_win = _dot(vwT_ref[:, pl.ds(w0, WIN_KEYS)], pw_ref[...])
    o_win = o_win[0:HEAD_DIM, :] * (1.0 / o_win[HEAD_DIM:HEAD_DIM + 1, :])

    full_pairs = last // 2
    m = lax.fori_loop(0, full_pairs // 2, lambda j, mm: pair(2 * j + 1, pair(2 * j, mm)),
                      jnp.full((1, QW), NEG, F32))
    m = lax.cond(full_pairs % 2 == 1, lambda mm: pair(full_pairs - 1, mm), lambda mm: mm, m)

    @pl.when(last % 2 == 0)
    def _():
        pending = weighted_values(last - 1, p1_ref)
        softmax(last, s0_ref, p0_ref, m, pending, True)
        acc_ref[...] += weighted_values(last, p0_ref)

    @pl.when(last % 2 == 1)
    def _():
        pair(full_pairs, m, final=True)
        acc_ref[...] += weighted_values(last, p1_ref)

    acc = acc_ref[...]
    o_slc = acc[0:HEAD_DIM, :] * (1.0 / acc[HEAD_DIM:HEAD_DIM + 1, :])

    gts = gT_ref[...]
    outs = []
    for h in range(HPG):
        sl = slice(h * QB, (h + 1) * QB)
        o = (gts[3 * h:3 * h + 1, :] * ocmp_ref[h] + gts[3 * h + 1:3 * h + 2, :] * o_slc[:, sl]
             + gts[3 * h + 2:3 * h + 3, :] * o_win[:, sl])
        o = o * lax.rsqrt(jnp.mean(o * o, axis=0, keepdims=True) + EPS)
        outs.append((o * ga_ref[h * HEAD_DIM:(h + 1) * HEAD_DIM, :]).T)
    o_ref[...] = jnp.concatenate(outs, axis=1).astype(o_ref.dtype)


def _attn_call(qT, gatesT, sel_bias, o_cmp, ksa, vsT, kw, vwT, g_attn_col):
    B, _, S = qT.shape
    n_slc = sel_bias.shape[2]
    gw = HPG * HEAD_DIM
    rows = lambda n, w: pl.BlockSpec((None, None, n, w), lambda b, g, i: (b, g, 0, 0))
    cols = lambda n: pl.BlockSpec((None, None, VT_ROWS, n), lambda b, g, i: (b, g, 0, 0))
    return pl.pallas_call(
        _attn_kernel,
        grid=(B, N_KV_GROUPS, S // Q_BLOCK),
        in_specs=[pl.BlockSpec((None, gw, Q_BLOCK), lambda b, g, i: (b, g, i)),
                  pl.BlockSpec((None, GATE_PAD, Q_BLOCK), lambda b, g, i: (b, g, i)),
                  pl.BlockSpec((None, None, n_slc, Q_BLOCK), lambda b, g, i: (b, g, 0, i)),
                  pl.BlockSpec((None, None, HPG, HEAD_DIM, Q_BLOCK), lambda b, g, i: (b, g, 0, 0, i)),
                  rows(S, MXU_DEPTH), cols(S), rows(S, HEAD_DIM), cols(S),
                  pl.BlockSpec((gw, 1), lambda b, g, i: (g, 0))],
        out_specs=pl.BlockSpec((None, Q_BLOCK, gw), lambda b, g, i: (b, i, g)),
        out_shape=jax.ShapeDtypeStruct((B, S, ATTN_WIDTH), BF16),
        scratch_shapes=[pltpu.VMEM((KEY_CHUNK, QW), F32), pltpu.VMEM((KEY_CHUNK, QW), F32),
                        pltpu.VMEM((KEY_CHUNK, QW), BF16), pltpu.VMEM((KEY_CHUNK, QW), BF16),
                        pltpu.VMEM((WIN_KEYS, QW), F32), pltpu.VMEM((WIN_KEYS, QW), BF16),
                        pltpu.VMEM((MXU_DEPTH, QW), BF16), pltpu.VMEM((VT_ROWS, QW), F32)],
        compiler_params=pltpu.CompilerParams(
            dimension_semantics=("parallel", "parallel", "arbitrary"),
            vmem_limit_bytes=VMEM_LIMIT),
        name="nsa_attention",
    )(qT, gatesT, sel_bias, o_cmp, ksa, vsT, kw, vwT, g_attn_col)


def _out_kernel(x_ref, yc_ref, ya_ref, mod_ref, wo_ref, g_ref, wg_ref, wu_ref, wd_ref, gf_ref,
                o_ref, a_ref, wgb_ref, wub_ref):
    _cast_once((wg_ref, wu_ref), (wgb_ref, wub_ref))
    mix = _dot(yc_ref[...], wo_ref[0:CONV_CH, :]) + _dot(ya_ref[...], wo_ref[CONV_CH:, :])
    x = x_ref[...] + mod_ref[5:6, :] * mix
    x = _ffn_core(x, mod_ref[6:7, :], mod_ref[7:8, :], mod_ref[8:9, :], g_ref[...], wgb_ref,
                  wub_ref, wd_ref, a_ref)
    o_ref[...] = _rms(x, gf_ref[...])


def _out_call(x, yc, ya, mod, wo, g, wg, wu, wd, gf):
    B, S, D = x.shape
    tm = FFN_TOKEN_TILE
    tok = lambda w: pl.BlockSpec((None, tm, w), lambda b, i: (b, i, 0))
    return pl.pallas_call(
        _out_kernel,
        grid=(B, S // tm),
        in_specs=[tok(D), tok(CONV_CH), tok(ATTN_WIDTH),
                  pl.BlockSpec((None, N_MOD, D), lambda b, i: (b, 0, 0)),
                  _const_spec(wo.shape), _const_spec(g.shape), _const_spec(wg.shape),
                  _const_spec(wu.shape), _const_spec(wd.shape), _const_spec(gf.shape)],
        out_specs=tok(D),
        out_shape=jax.ShapeDtypeStruct((B, S, D), F32),
        scratch_shapes=[pltpu.VMEM((tm, wg.shape[1]), BF16), pltpu.VMEM(wg.shape, BF16),
                        pltpu.VMEM(wu.shape, BF16)],
        compiler_params=pltpu.CompilerParams(dimension_semantics=("arbitrary", "arbitrary"),
                                             vmem_limit_bytes=VMEM_LIMIT),
        name="outproj_ffn2",
    )(x, yc, ya, mod, wo, g, wg, wu, wd, gf)


def kernel(x, c, positions, w_ada, b_ada, g_ffn1, w1_gate, w1_up, w1_down, g_mix, w_in, conv_w, cmp_pos_k, cmp_pos_v, w_cmpk1, w_cmpk2, w_cmpv1, w_cmpv2, g_out_conv, g_out_attn, w_out, g_ffn2, w2_gate, w2_up, w2_down, g_final):
    B, S, D = x.shape
    depth = w_ada.shape[0]
    n_slc = S // SLC_BLOCK
    half = CMP_BLOCK // 2
    n_half = S // half
    assert n_slc <= LANES, "selection-block one-hot is one lane tile wide"

    c_pad = jnp.pad(c, ((0, 8 - B), (0, 0)))
    row = lambda a: a.reshape(1, -1)

    freq_half = jnp.power(ROPE_THETA, -2.0 * jnp.arange(ROT_HALF, dtype=F32) / ROT_DIM)
    freq = jnp.tile(freq_half, LANES // ROT_HALF).reshape(1, LANES)
    gidx = np.arange(CONV_CH) // (CONV_CH // CONV_GROUPS)
    gmat = jnp.asarray((gidx[:, None] == gidx[None, :]) / (CONV_CH // CONV_GROUPS), dtype=BF16)
    c0 = np.arange(n_half) * CMP_STRIDE
    s0 = np.arange(LANES) * SLC_BLOCK
    ovlT = ((c0[None, :] <= s0[:, None] + SLC_BLOCK - 1) & (c0[None, :] + CMP_BLOCK - 1 >= s0[:, None]))
    ovlT = jnp.asarray(ovlT, dtype=BF16)
    onehot = jnp.asarray((np.arange(S) // SLC_BLOCK)[:, None] == np.arange(LANES)[None, :], dtype=BF16)
    cos_p, sin_p = _rope_table_call(positions.reshape(-1, TOKENS_PER_ROW), freq)
    cos_p = cos_p.reshape(B, S // TOKENS_PER_ROW, LANES)
    sin_p = sin_p.reshape(B, S // TOKENS_PER_ROW, LANES)

    for l in range(depth):
        mod = _ada_call(c_pad, w_ada[l], row(b_ada[l]))[:B].reshape(B, N_MOD, D)

        x = _ffn_call(x, mod, row(g_ffn1[l]), w1_gate[l], w1_up[l], w1_down[l].astype(BF16))

        n_main = w_in.shape[2] - N_KV_GROUPS * 3 * HPG
        gate_cols = [jnp.pad(w_in[l][:, n_main + g * 3 * HPG:n_main + (g + 1) * 3 * HPG],
                             ((0, 0), (0, GATE_PAD - 3 * HPG))) for g in range(N_KV_GROUPS)]
        wgate = jnp.concatenate(gate_cols, axis=1).astype(BF16)
        win = w_in[l]
        (yc, qT, kc, vc, ksa, vsT, kw, vwT, gatesT) = _inproj_call(
            x, cos_p, sin_p, mod, row(g_mix[l]), win, wgate, conv_w[l], row(g_out_conv[l]), gmat,
            onehot)

        both_groups = lambda pe: jnp.tile(pe, (1, N_KV_GROUPS))
        kcc, vcT = _compress_call(kc, vc, both_groups(cmp_pos_k[l]), both_groups(cmp_pos_v[l]),
                                  w_cmpk1[l], w_cmpk2[l], w_cmpv1[l], w_cmpv2[l])

        sel_bias, o_cmp = _select_call(min(SLC_TOP_N, n_slc), qT, kcc, vcT, ovlT)
        ya = _attn_call(qT, gatesT, sel_bias, o_cmp, ksa, vsT, kw, vwT,
                        g_out_attn[l].reshape(ATTN_WIDTH, 1))

        assert l == depth - 1, "final norm is fused into the last layer's output kernel"
        x = _out_call(x, yc, ya, mod, w_out[l].astype(BF16), row(g_ffn2[l]),
                      w2_gate[l], w2_up[l], w2_down[l].astype(BF16),
                      row(g_final))
    return x
```

```python
import functools
import math

import numpy as np
import jax
import jax.numpy as jnp
from jax import lax
from jax.experimental import pallas as pl
from jax.experimental.pallas import tpu as pltpu

F32 = jnp.float32
BF16 = jnp.bfloat16

CONV_CH = 512
CONV_GROUPS = 8
N_HEADS = 8
N_KV_GROUPS = 2
HPG = N_HEADS // N_KV_GROUPS
HEAD_DIM = 64
ATTN_WIDTH = N_HEADS * HEAD_DIM
KV_WIDTH = N_KV_GROUPS * HEAD_DIM
ROPE_THETA = 500000.0
ROT_DIM = HEAD_DIM // 4
ROT_HALF = ROT_DIM // 2
CMP_BLOCK = 32
CMP_STRIDE = 16
CMP_HIDDEN = 256
SLC_BLOCK = 64
SLC_TOP_N = 16
MAX_FORCED = 3
WINDOW = 512
Q_BLOCK = 256
MACARON_W = 0.5
N_MOD = 9
EPS = 1e-6
NEG = -1e30
BIG = 1e9

LANES = 128
MXU_DEPTH = 256
VMEM_LIMIT = 58 * 1024 * 1024

TOKEN_TILE = 512
FFN_TOKEN_TILE = 512
ADA_COL_TILE = 1024
FF_TILE = 256
KEY_CHUNK = 512
WIN_KEYS = WINDOW + Q_BLOCK
TOKENS_PER_ROW = LANES // ROT_HALF
GATE_PAD = LANES
QW = HPG * Q_BLOCK
SEL_BLOCK = 1024
SEL_ROW_STEP = 32
ONES_ROWS = 16
VT_ROWS = HEAD_DIM + ONES_ROWS
ROW_TILE = 16
Q_SCALE = HEAD_DIM ** -0.5 * math.log2(math.e)
M_FLOOR = -1e20


def _dot(a, b):
    return jnp.dot(a, b, preferred_element_type=F32)


def _rms(x, g):
    return x * lax.rsqrt(jnp.mean(x * x, axis=-1, keepdims=True) + EPS) * g


def _split_bf16(x):
    hi = x.astype(BF16)
    lo = (x - hi.astype(F32)).astype(BF16)
    return hi, lo


def _const_spec(shape):
    nd = len(shape)
    return pl.BlockSpec(shape, lambda *_: (0,) * nd, pipeline_mode=pl.Buffered(1))


def _ada_kernel(c_ref, w_ref, b_ref, o_ref):
    c = c_ref[...]
    c_act = c * jax.nn.sigmoid(c)
    o_ref[...] = _dot(c_act.astype(BF16), w_ref[...].astype(BF16)) + b_ref[...]


def _ada_call(c_pad, w_ada, b_ada):
    rows, d = c_pad.shape
    n = w_ada.shape[1]
    tn = ADA_COL_TILE
    return pl.pallas_call(
        _ada_kernel,
        grid=(n // tn,),
        in_specs=[pl.BlockSpec((rows, d), lambda j: (0, 0)),
                  pl.BlockSpec((d, tn), lambda j: (0, j)),
                  pl.BlockSpec((1, tn), lambda j: (0, j))],
        out_specs=pl.BlockSpec((rows, tn), lambda j: (0, j)),
        out_shape=jax.ShapeDtypeStruct((rows, n), F32),
        compiler_params=pltpu.CompilerParams(dimension_semantics=("arbitrary",),
                                             vmem_limit_bytes=VMEM_LIMIT),
        name="adaln_mod",
    )(c_pad, w_ada, b_ada)


def _ffn_core(x, shift, scale, gate, g, wg_ref, wu_ref, wd_ref, a_ref):
    h = _rms(x, g) * (1.0 + scale) + shift
    hb = h.astype(BF16)
    d_ff = wg_ref.shape[1]
    for j in range(d_ff // FF_TILE):
        sl = slice(j * FF_TILE, (j + 1) * FF_TILE)
        gg = _dot(hb, wg_ref[:, sl])
        uu = _dot(hb, wu_ref[:, sl])
        a_ref[:, sl] = (gg * jax.nn.sigmoid(gg) * uu).astype(a_ref.dtype)
    return x + (MACARON_W * gate) * _dot(a_ref[...], wd_ref[...])


def _cast_once(src_refs, dst_refs):
    @pl.when((pl.program_id(0) == 0) & (pl.program_id(1) == 0))
    def _():
        for src, dst in zip(src_refs, dst_refs):
            for c in range(0, src.shape[1], FF_TILE):
                dst[:, c:c + FF_TILE] = src[:, c:c + FF_TILE].astype(dst.dtype)


def _ffn_kernel(x_ref, mod_ref, g_ref, wg_ref, wu_ref, wd_ref, o_ref, a_ref, wgb_ref, wub_ref):
    _cast_once((wg_ref, wu_ref), (wgb_ref, wub_ref))
    o_ref[...] = _ffn_core(x_ref[...], mod_ref[0:1, :], mod_ref[1:2, :], mod_ref[2:3, :],
                           g_ref[...], wgb_ref, wub_ref, wd_ref, a_ref)


def _ffn_call(x, mod, g, wg, wu, wd):
    B, S, D = x.shape
    tm = FFN_TOKEN_TILE
    return pl.pallas_call(
        _ffn_kernel,
        grid=(B, S // tm),
        in_specs=[pl.BlockSpec((None, tm, D), lambda b, i: (b, i, 0)),
                  pl.BlockSpec((None, N_MOD, D), lambda b, i: (b, 0, 0)),
                  _const_spec(g.shape), _const_spec(wg.shape), _const_spec(wu.shape),
                  _const_spec(wd.shape)],
        out_specs=pl.BlockSpec((None, tm, D), lambda b, i: (b, i, 0)),
        out_shape=jax.ShapeDtypeStruct((B, S, D), F32),
        scratch_shapes=[pltpu.VMEM((tm, wg.shape[1]), BF16), pltpu.VMEM(wg.shape, BF16),
                        pltpu.VMEM(wu.shape, BF16)],
        compiler_params=pltpu.CompilerParams(dimension_semantics=("arbitrary", "arbitrary"),
                                             vmem_limit_bytes=VMEM_LIMIT),
        name="ffn1",
    )(x, mod, g, wg, wu, wd)


def _rope_table_kernel(pos_ref, freq_ref, cos_ref, sin_ref):
    pos = jnp.concatenate([pos_ref[...].astype(F32),
                           jnp.zeros((pos_ref.shape[0], LANES - TOKENS_PER_ROW), F32)], axis=1)
    src = lax.broadcasted_iota(jnp.int32, pos.shape, 1) // ROT_HALF
    ang = jnp.take_along_axis(pos, src, axis=1) * freq_ref[...]
    cos_ref[...] = jnp.cos(ang)
    sin_ref[...] = jnp.sin(ang)


def _rope_table_call(pos_rows, freq):
    rows = pos_rows.shape[0]
    shape = jax.ShapeDtypeStruct((rows, LANES), F32)
    full = pl.BlockSpec((rows, LANES), lambda: (0, 0))
    return pl.pallas_call(
        _rope_table_kernel,
        in_specs=[pl.BlockSpec(pos_rows.shape, lambda: (0, 0)), pl.BlockSpec(freq.shape, lambda: (0, 0))],
        out_specs=(full, full),
        out_shape=(shape, shape),
        compiler_params=pltpu.CompilerParams(vmem_limit_bytes=VMEM_LIMIT),
        name="rope_table",
    )(pos_rows, freq)


def _inproj_kernel(x_ref, cosp_ref, sinp_ref, mod_ref, gmix_ref, win_ref, wgate_ref, convw_ref, gconv_ref,
                   gmat_ref, onehot_ref, yc_ref, qT_ref, kc_ref, vc_ref, ksa_ref, vsT_ref, kw_ref,
                   vwT_ref, gT_ref, carry_ref):
    tm = x_ref.shape[0]

    @pl.when(pl.program_id(1) == 0)
    def _():
        carry_ref[...] = jnp.zeros_like(carry_ref)

    x = x_ref[...]
    h = _rms(x, gmix_ref[...]) * (1.0 + mod_ref[4:5, :]) + mod_ref[3:4, :]
    hb = h.astype(BF16)

    def proj(c0, width):
        return _dot(hb, win_ref[:, c0:c0 + width])

    d = lax.broadcasted_iota(jnp.int32, (tm, LANES), 1) & (HEAD_DIM - 1)
    token = lax.broadcasted_iota(jnp.int32, (tm, LANES), 0) & (TOKENS_PER_ROW - 1)
    src_lane = token * ROT_HALF + (d & (ROT_HALF - 1))

    def unpack(packed_ref):
        rows = jnp.broadcast_to(packed_ref[...][:, None, :], (tm // TOKENS_PER_ROW, TOKENS_PER_ROW, LANES))
        return jnp.take_along_axis(rows.reshape(tm, LANES), src_lane, axis=1)

    cos_t = jnp.where(d < ROT_DIM, unpack(cosp_ref), 1.0)
    sin_raw = unpack(sinp_ref)
    sin_t = jnp.where(d < ROT_HALF, -sin_raw, jnp.where(d < ROT_DIM, sin_raw, 0.0))
    first_half = d < ROT_HALF

    def rope(t):
        outs = []
        for j in range(t.shape[1] // LANES):
            tj = t[:, j * LANES:(j + 1) * LANES]
            partner = jnp.where(first_half, pltpu.roll(tj, LANES - ROT_HALF, 1),
                                pltpu.roll(tj, ROT_HALF, 1))
            outs.append(tj * cos_t + partner * sin_t)
        return outs[0] if len(outs) == 1 else jnp.concatenate(outs, axis=1)

    c0 = 3 * CONV_CH
    q = rope(proj(c0, ATTN_WIDTH)) * Q_SCALE
    qT_ref[...] = q.T.astype(qT_ref.dtype)
    c0 += ATTN_WIDTH
    kv = proj(c0, 2 * KV_WIDTH)
    kc_ref[...] = rope(kv[:, :KV_WIDTH])
    vc_ref[...] = kv[:, KV_WIDTH:]
    kv = proj(c0 + 2 * KV_WIDTH, 2 * KV_WIDTH)
    ks = rope(kv[:, :KV_WIDTH]).astype(BF16)
    vsT = kv[:, KV_WIDTH:].T.astype(BF16)
    kv = proj(c0 + 4 * KV_WIDTH, 2 * KV_WIDTH)
    kw = rope(kv[:, :KV_WIDTH]).astype(BF16)
    vwT = kv[:, KV_WIDTH:].T.astype(BF16)
    pad = jnp.zeros((tm, MXU_DEPTH - LANES - HEAD_DIM), BF16)
    ones = jnp.ones((ONES_ROWS, tm), BF16)
    for g in range(N_KV_GROUPS):
        kg = ks[:, g * HEAD_DIM:(g + 1) * HEAD_DIM]
        ksa_ref[g] = jnp.concatenate([onehot_ref[...], kg, pad], axis=1)
        kw_ref[g] = kw[:, g * HEAD_DIM:(g + 1) * HEAD_DIM]
        vsT_ref[g] = jnp.concatenate([vsT[g * HEAD_DIM:(g + 1) * HEAD_DIM, :], ones], axis=0)
        vwT_ref[g] = jnp.concatenate([vwT[g * HEAD_DIM:(g + 1) * HEAD_DIM, :], ones], axis=0)
    gT_ref[...] = jax.nn.sigmoid(_dot(hb, wgate_ref[...])).T

    cb = proj(0, CONV_CH)
    u = proj(CONV_CH, CONV_CH) * proj(2 * CONV_CH, CONV_CH)
    row = lax.broadcasted_iota(jnp.int32, (tm, 1), 0)
    prev1 = carry_ref[7:8, :]
    prev2 = carry_ref[6:7, :]
    u1 = jnp.where(row >= 1, pltpu.roll(u, 1, 0), prev1)
    u2 = jnp.where(row >= 2, pltpu.roll(u, 2, 0), jnp.where(row == 1, prev1, prev2))
    carry_ref[...] = u[tm - 8:tm, :]
    v = convw_ref[0:1, :] * u2 + convw_ref[1:2, :] * u1 + convw_ref[2:3, :] * u
    y = cb * v
    ms = _dot((y * y).astype(BF16), gmat_ref[...])
    yc_ref[...] = (y * lax.rsqrt(ms + EPS) * gconv_ref[...]).astype(yc_ref.dtype)


def _inproj_call(x, cos_p, sin_p, mod, gmix, win, wgate, convw, gconv, gmat, onehot):
    B, S, D = x.shape
    tm = TOKEN_TILE
    tok = lambda w: pl.BlockSpec((None, tm, w), lambda b, i: (b, i, 0))
    tr = lambda w: pl.BlockSpec((None, w, tm), lambda b, i: (b, 0, i))
    grp = lambda w: pl.BlockSpec((None, N_KV_GROUPS, tm, w), lambda b, i: (b, 0, i, 0))
    grpT = pl.BlockSpec((None, N_KV_GROUPS, VT_ROWS, tm), lambda b, i: (b, 0, 0, i))
    packed = pl.BlockSpec((None, tm // TOKENS_PER_ROW, LANES), lambda b, i: (b, i, 0))
    out_shapes = (
        jax.ShapeDtypeStruct((B, S, CONV_CH), BF16),
        jax.ShapeDtypeStruct((B, ATTN_WIDTH, S), BF16),
        jax.ShapeDtypeStruct((B, S, KV_WIDTH), F32),
        jax.ShapeDtypeStruct((B, S, KV_WIDTH), F32),
        jax.ShapeDtypeStruct((B, N_KV_GROUPS, S, MXU_DEPTH), BF16),
        jax.ShapeDtypeStruct((B, N_KV_GROUPS, VT_ROWS, S), BF16),
        jax.ShapeDtypeStruct((B, N_KV_GROUPS, S, HEAD_DIM), BF16),
        jax.ShapeDtypeStruct((B, N_KV_GROUPS, VT_ROWS, S), BF16),
        jax.ShapeDtypeStruct((B, N_KV_GROUPS * GATE_PAD, S), F32),
    )
    return pl.pallas_call(
        _inproj_kernel,
        grid=(B, S // tm),
        in_specs=[tok(D),
                  packed, packed,
                  pl.BlockSpec((None, N_MOD, D), lambda b, i: (b, 0, 0)),
                  _const_spec(gmix.shape), _const_spec(win.shape), _const_spec(wgate.shape),
                  _const_spec(convw.shape),
                  _const_spec(gconv.shape), _const_spec(gmat.shape),
                  pl.BlockSpec((tm, onehot.shape[1]), lambda b, i: (i, 0))],
        out_specs=(tok(CONV_CH), tr(ATTN_WIDTH), tok(KV_WIDTH), tok(KV_WIDTH), grp(MXU_DEPTH),
                   grpT, grp(HEAD_DIM), grpT, tr(N_KV_GROUPS * GATE_PAD)),
        out_shape=out_shapes,
        scratch_shapes=[pltpu.VMEM((8, CONV_CH), F32)],
        compiler_params=pltpu.CompilerParams(dimension_semantics=("arbitrary", "arbitrary"),
                                             vmem_limit_bytes=VMEM_LIMIT),
        name="mixer_inproj",
    )(x, cos_p, sin_p, mod, gmix, win, wgate, convw, gconv, gmat, onehot)


def _compress_kernel(kf_ref, vf_ref, pek_ref, pev_ref, wk1_ref, wk2_ref, wv1_ref, wv2_ref,
                     kcc_ref, vcT_ref):
    half = CMP_BLOCK // 2
    n = kf_ref.shape[0] // half

    def mlp(x_ref, pe_ref, w1_ref, w2_ref):
        parts = []
        for p in range(2):
            acc = None
            for l0 in range(0, half, 2):
                xs, ws = [], []
                for l in (l0, l0 + 1):
                    row = p * half + l
                    xs.append((x_ref[pl.ds(l, n, stride=half), :]
                               + pe_ref[row:row + 1, :]).astype(BF16))
                    w = w1_ref[row * HEAD_DIM:(row + 1) * HEAD_DIM, :].astype(BF16)
                    z = jnp.zeros_like(w)
                    ws.append(jnp.concatenate([jnp.concatenate([w, z], axis=1),
                                               jnp.concatenate([z, w], axis=1)], axis=0))
                d = _dot(jnp.concatenate(xs, axis=1), jnp.concatenate(ws, axis=0))
                acc = d if acc is None else acc + d
            parts.append(acc)
        hpre = parts[0] + pltpu.roll(parts[1], n - 1, 0)
        hid = (hpre * jax.nn.sigmoid(hpre)).astype(BF16)
        w2 = w2_ref[...].astype(BF16)
        return jnp.concatenate([_dot(hid[:, g * CMP_HIDDEN:(g + 1) * CMP_HIDDEN], w2)
                                for g in range(N_KV_GROUPS)], axis=1)

    kc = mlp(kf_ref, pek_ref, wk1_ref, wk2_ref).astype(kcc_ref.dtype)
    for g in range(N_KV_GROUPS):
        kcc_ref[g] = kc[:, g * HEAD_DIM:(g + 1) * HEAD_DIM]
    vcT = mlp(vf_ref, pev_ref, wv1_ref, wv2_ref).T.astype(vcT_ref.dtype)
    ones = jnp.ones((ONES_ROWS, n), vcT_ref.dtype)
    for g in range(N_KV_GROUPS):
        vcT_ref[g] = jnp.concatenate([vcT[g * HEAD_DIM:(g + 1) * HEAD_DIM, :], ones], axis=0)


def _compress_call(kf, vf, pek, pev, wk1, wk2, wv1, wv2):
    B, S, width = kf.shape
    n = S // (CMP_BLOCK // 2)
    flat = pl.BlockSpec((None, S, width), lambda b: (b, 0, 0))
    return pl.pallas_call(
        _compress_kernel,
        grid=(B,),
        in_specs=[flat, flat, _const_spec(pek.shape), _const_spec(pev.shape),
                  _const_spec(wk1.shape), _const_spec(wk2.shape), _const_spec(wv1.shape),
                  _const_spec(wv2.shape)],
        out_specs=(pl.BlockSpec((None, N_KV_GROUPS, n, HEAD_DIM), lambda b: (b, 0, 0, 0)),
                   pl.BlockSpec((None, N_KV_GROUPS, VT_ROWS, n), lambda b: (b, 0, 0, 0))),
        out_shape=(jax.ShapeDtypeStruct((B, N_KV_GROUPS, n, HEAD_DIM), BF16),
                   jax.ShapeDtypeStruct((B, N_KV_GROUPS, VT_ROWS, n), BF16)),
        compiler_params=pltpu.CompilerParams(dimension_semantics=("arbitrary",),
                                             vmem_limit_bytes=VMEM_LIMIT),
        name="kv_compress",
    )(kf, vf, pek, pev, wk1, wk2, wv1, wv2)


def _lane_tiles(x, n):
    return jnp.concatenate([x] * n, axis=1)


def _col_max(s_ref, n_rows, bias_fn, lane_tile=None):
    width = s_ref.shape[1]
    lt = lane_tile or width
    groups = ROW_TILE // 8
    mx = {c: [jnp.full((8, lt), NEG, F32)] * groups for c in range(0, width, lt)}
    for r in range(0, n_rows, ROW_TILE):
        bias = None if bias_fn is None else bias_fn(r)
        for c in range(0, width, lt):
            x = s_ref[r:r + ROW_TILE, c:c + lt]
            if bias is not None:
                x = x + bias
                s_ref[r:r + ROW_TILE, c:c + lt] = x
            mx[c] = [jnp.maximum(mx[c][i], x[8 * i:8 * (i + 1), :]) for i in range(groups)]
    out = []
    for c in range(0, width, lt):
        m = mx[c]
        while len(m) > 1:
            m = [jnp.maximum(a, b) for a, b in zip(m[0::2], m[1::2])]
        out.append(jnp.max(m[0], axis=0, keepdims=True))
    return out[0] if len(out) == 1 else jnp.concatenate(out, axis=1)


def _col_exp2(s_ref, p_ref, n_rows, m, keep_f32=False, lane_tile=None):
    width = s_ref.shape[1]
    lt = lane_tile or width
    for r in range(0, n_rows, ROW_TILE):
        for c in range(0, width, lt):
            p = jnp.exp2(s_ref[r:r + ROW_TILE, c:c + lt] - m[:, c:c + lt])
            if keep_f32:
                s_ref[r:r + ROW_TILE, c:c + lt] = p
            p_ref[r:r + ROW_TILE, c:c + lt] = p.astype(p_ref.dtype)


def _recip_pos(l):
    return 1.0 / jnp.where(l > 0.0, l, 1.0)


def _select_kernel(top_n, qT_ref, kcc_ref, vcT_ref, ovlT_ref, sel_ref, ocmp_ref, sc_ref, pc_ref,
                   ph_ref, pl_ref):
    QB = SEL_BLOCK
    n_cmp = kcc_ref.shape[0]
    n_slc = ovlT_ref.shape[0]
    t0 = pl.program_id(2) * QB
    tq = t0 + lax.broadcasted_iota(jnp.int32, (1, QB), 1)
    rows = lax.broadcasted_iota(jnp.int32, (ROW_TILE, 1), 0)

    qT = qT_ref[...]
    qcat = jnp.concatenate([qT[h * HEAD_DIM:(h + 1) * HEAD_DIM, :] for h in range(HPG)], axis=1)

    def cmp_bias(r):
        cmp_end = (r + rows) * CMP_STRIDE + (CMP_BLOCK - 1)
        return jnp.where(cmp_end <= tq, 0.0, NEG)

    cur = tq // SLC_BLOCK

    def tree(op, xs):
        while len(xs) > 1:
            xs = [op(*xs[i:i + 2]) if i + 1 < len(xs) else xs[i] for i in range(0, len(xs), 2)]
        return xs[0]

    def causal_variant(n_rows):
        rows_cmp = min(n_cmp, n_rows * (SLC_BLOCK // CMP_STRIDE))

        def run():
            sc_ref[0:rows_cmp, :] = _dot(kcc_ref[0:rows_cmp, :], qcat)
            m = jnp.maximum(_col_max(sc_ref, rows_cmp, cmp_bias, lane_tile=QB), M_FLOOR)
            _col_exp2(sc_ref, pc_ref, rows_cmp, m, keep_f32=True, lane_tile=QB)
            o_cmp = _dot(vcT_ref[:, 0:rows_cmp], pc_ref[0:rows_cmp, :])
            rl = _recip_pos(o_cmp[HEAD_DIM:HEAD_DIM + 1, :])
            o_cmp = o_cmp[0:HEAD_DIM, :] * rl
            for h in range(HPG):
                ocmp_ref[h] = o_cmp[:, h * QB:(h + 1) * QB]

            for r in range(0, rows_cmp, ROW_TILE):
                psum = sc_ref[r:r + ROW_TILE, 0:QB] * rl[:, 0:QB]
                for h in range(1, HPG):
                    sl = slice(h * QB, (h + 1) * QB)
                    psum = psum + sc_ref[r:r + ROW_TILE, sl] * rl[:, sl]
                hi, lo = _split_bf16(psum)
                ph_ref[r:r + ROW_TILE, :] = hi
                pl_ref[r:r + ROW_TILE, :] = lo
            ovl = ovlT_ref[0:n_rows, 0:rows_cmp]
            imp = _dot(ovl, ph_ref[0:rows_cmp, :]) + _dot(ovl, pl_ref[0:rows_cmp, :])

            blk = lax.broadcasted_iota(jnp.int32, (n_rows, 1), 0)
            future = blk > cur
            forced = (blk == 0) | (blk == cur) | (blk == cur - 1)
            groups = range(n_rows // 8)

            def store(picked):
                sel_ref[0:n_rows, :] = jnp.where(future, NEG, jnp.where(picked, 0.0, NEG)).astype(sel_ref.dtype)

            def with_ties():
                score0 = jnp.where(future, -BIG, jnp.where(forced, BIG, imp))
                score = [score0[8 * i:8 * (i + 1), :] for i in groups]
                cum = jnp.zeros((1, QB), F32)
                thr = jnp.zeros((1, QB), F32)
                above = jnp.zeros((1, QB), F32)
                for _ in range(top_n):
                    best = jnp.max(tree(jnp.maximum, score), axis=0, keepdims=True)
                    eq = [s == best for s in score]
                    unfilled = cum < top_n
                    thr = jnp.where(unfilled, best, thr)
                    above = jnp.where(unfilled, cum, above)
                    cum = cum + jnp.sum(tree(jnp.add, [jnp.where(e, 1.0, 0.0) for e in eq]),
                                        axis=0, keepdims=True)
                    score = [jnp.where(e, -jnp.inf, s) for e, s in zip(eq, score)]
                ties = score0 == thr
                lower = jnp.where(lax.broadcasted_iota(jnp.int32, (1, n_rows), 1) < blk, 1.0, 0.0)
                rank = _dot(lower.astype(BF16), jnp.where(ties, 1.0, 0.0).astype(BF16))
                store((score0 > thr) | (ties & (rank < top_n - above)))

            if top_n > MAX_FORCED:
                n_forced = 1.0 + jnp.where(cur >= 1, 1.0, 0.0) + jnp.where(cur >= 2, 1.0, 0.0)
                wanted = top_n - n_forced
                take_all = (cur.astype(F32) + 1.0 - n_forced) <= wanted
                cand = jnp.where(future | forced, -BIG, imp)
                score = [cand[8 * i:8 * (i + 1), :] for i in groups]
                for _ in range(top_n - MAX_FORCED):
                    cut = jnp.max(tree(jnp.maximum, score), axis=0, keepdims=True)
                    score = [jnp.where(s == cut, -jnp.inf, s) for s in score]
                above_cut = cand >= cut
                count = jnp.sum(jnp.where(above_cut, 1.0, 0.0), axis=0, keepdims=True)
                exact = jnp.min(jnp.where(take_all | (count == wanted), 1.0, 0.0)) > 0.5
                lax.cond(exact, lambda: store(forced | take_all | above_cut), with_ties)
            else:
                with_ties()
            if n_rows < n_slc:
                sel_ref[n_rows:n_slc, :] = jnp.full((n_slc - n_rows, QB), NEG, sel_ref.dtype)
        return run

    steps = n_slc // SEL_ROW_STEP
    need = jnp.minimum((t0 + QB - 1) // (SLC_BLOCK * SEL_ROW_STEP), steps - 1)
    lax.switch(need, [causal_variant(SEL_ROW_STEP * (k + 1)) for k in range(steps)])


def _select_call(top_n, qT, kcc, vcT, ovlT):
    B, _, S = qT.shape
    n_cmp = kcc.shape[2]
    n_slc = ovlT.shape[0]
    gw = HPG * HEAD_DIM
    qw = HPG * SEL_BLOCK
    return pl.pallas_call(
        functools.partial(_select_kernel, top_n),
        grid=(B, N_KV_GROUPS, S // SEL_BLOCK),
        in_specs=[pl.BlockSpec((None, gw, SEL_BLOCK), lambda b, g, i: (b, g, i)),
                  pl.BlockSpec((None, None, n_cmp, HEAD_DIM), lambda b, g, i: (b, g, 0, 0)),
                  pl.BlockSpec((None, None, VT_ROWS, n_cmp), lambda b, g, i: (b, g, 0, 0)),
                  pl.BlockSpec(ovlT.shape, lambda b, g, i: (0, 0))],
        out_specs=(pl.BlockSpec((None, None, n_slc, SEL_BLOCK), lambda b, g, i: (b, g, 0, i)),
                   pl.BlockSpec((None, None, HPG, HEAD_DIM, SEL_BLOCK),
                                lambda b, g, i: (b, g, 0, 0, i))),
        out_shape=(jax.ShapeDtypeStruct((B, N_KV_GROUPS, n_slc, S), BF16),
                   jax.ShapeDtypeStruct((B, N_KV_GROUPS, HPG, HEAD_DIM, S), F32)),
        scratch_shapes=[pltpu.VMEM((n_cmp, qw), F32), pltpu.VMEM((n_cmp, qw), BF16),
                        pltpu.VMEM((n_cmp, SEL_BLOCK), BF16), pltpu.VMEM((n_cmp, SEL_BLOCK), BF16)],
        compiler_params=pltpu.CompilerParams(
            dimension_semantics=("parallel", "parallel", "arbitrary"),
            vmem_limit_bytes=VMEM_LIMIT),
        name="nsa_select",
    )(qT, kcc, vcT, ovlT)


def _attn_kernel(qT_ref, gT_ref, sel_ref, ocmp_ref, ksa_ref, vsT_ref, kw_ref, vwT_ref, ga_ref,
                 o_ref, s0_ref, s1_ref, p0_ref, p1_ref, sw_ref, pw_ref, rhs_ref, acc_ref):
    QB = Q_BLOCK
    n_slc = sel_ref.shape[0]
    t0 = pl.program_id(2) * QB
    tq = t0 + lax.broadcasted_iota(jnp.int32, (1, QB), 1)
    rows = lax.broadcasted_iota(jnp.int32, (ROW_TILE, 1), 0)

    qT = qT_ref[...]
    qcat = jnp.concatenate([qT[h * HEAD_DIM:(h + 1) * HEAD_DIM, :] for h in range(HPG)], axis=1)

    rhs_ref[...] = jnp.concatenate([_lane_tiles(sel_ref[...], HPG), qcat,
                                    jnp.zeros((MXU_DEPTH - n_slc - HEAD_DIM, QW), BF16)], axis=0)

    w0 = pl.multiple_of(jnp.maximum(t0 - WINDOW, 0), QB)
    sw_ref[...] = _dot(kw_ref[pl.ds(w0, WIN_KEYS), :], qcat)

    def win_bias(r):
        dist = tq - (w0 + r + rows)
        return _lane_tiles(jnp.where((dist >= 0) & (dist < WINDOW), 0.0, NEG), HPG)

    KC = KEY_CHUNK
    last = t0 // KC

    def scores(c, dst_ref):
        k0 = pl.multiple_of(jnp.minimum(c, last) * KC, KC)
        dst_ref[...] = _dot(ksa_ref[pl.ds(k0, KC), :], rhs_ref[...])

    def weighted_values(c, p_ref):
        k0 = pl.multiple_of(jnp.clip(c, 0, last) * KC, KC)
        return _dot(vsT_ref[:, pl.ds(k0, KC)], p_ref[...])

    def softmax(c, src_ref, p_ref, m, pending, causal):
        def causal_bias(r):
            return _lane_tiles(jnp.where(c * KC + r + rows <= tq, 0.0, NEG), HPG)

        m_new = jnp.maximum(m, _col_max(src_ref, KC, causal_bias if causal else None))
        acc_ref[...] = jnp.exp2(m - m_new) * (acc_ref[...] + pending)
        _col_exp2(src_ref, p_ref, KC, m_new)
        return m_new

    def pair(i, m, final=False):
        pending = weighted_values(2 * i - 1, p1_ref)
        scores(2 * i + 1, s1_ref)
        m = softmax(2 * i, s0_ref, p0_ref, m, pending, False)
        pending = weighted_values(2 * i, p0_ref)
        if not final:
            scores(2 * i + 2, s0_ref)
        return softmax(2 * i + 1, s1_ref, p1_ref, m, pending, final)

    p1_ref[...] = jnp.zeros_like(p1_ref)
    acc_ref[...] = jnp.zeros_like(acc_ref)
    scores(0, s0_ref)

    m = _col_max(sw_ref, WIN_KEYS, win_bias)
    _col_exp2(sw_ref, pw_ref, WIN_KEYS, m)
    o_win = _dot(vwT_ref[:, pl.ds(w0, WIN_KEYS)], pw_ref[...])
    o_win = o_win[0:HEAD_DIM, :] * (1.0 / o_win[HEAD_DIM:HEAD_DIM + 1, :])

    full_pairs = last // 2
    m = lax.fori_loop(0, full_pairs // 2, lambda j, mm: pair(2 * j + 1, pair(2 * j, mm)),
                      jnp.full((1, QW), NEG, F32))
    m = lax.cond(full_pairs % 2 == 1, lambda mm: pair(full_pairs - 1, mm), lambda mm: mm, m)

    @pl.when(last % 2 == 0)
    def _():
        pending = weighted_values(last - 1, p1_ref)
        softmax(last, s0_ref, p0_ref, m, pending, True)
        acc_ref[...] += weighted_values(last, p0_ref)

    @pl.when(last % 2 == 1)
    def _():
        pair(full_pairs, m, final=True)
        acc_ref[...] += weighted_values(last, p1_ref)

    acc = acc_ref[...]
    o_slc = acc[0:HEAD_DIM, :] * (1.0 / acc[HEAD_DIM:HEAD_DIM + 1, :])

    gts = gT_ref[...]
    outs = []
    for h in range(HPG):
        sl = slice(h * QB, (h + 1) * QB)
        o = (gts[3 * h:3 * h + 1, :] * ocmp_ref[h] + gts[3 * h + 1:3 * h + 2, :] * o_slc[:, sl]
             + gts[3 * h + 2:3 * h + 3, :] * o_win[:, sl])
        o = o * lax.rsqrt(jnp.mean(o * o, axis=0, keepdims=True) + EPS)
        outs.append((o * ga_ref[h * HEAD_DIM:(h + 1) * HEAD_DIM, :]).T)
    o_ref[...] = jnp.concatenate(outs, axis=1).astype(o_ref.dtype)


def _attn_call(qT, gatesT, sel_bias, o_cmp, ksa, vsT, kw, vwT, g_attn_col):
    B, _, S = qT.shape
    n_slc = sel_bias.shape[2]
    gw = HPG * HEAD_DIM
    rows = lambda n, w: pl.BlockSpec((None, None, n, w), lambda b, g, i: (b, g, 0, 0))
    cols = lambda n: pl.BlockSpec((None, None, VT_ROWS, n), lambda b, g, i: (b, g, 0, 0))
    return pl.pallas_call(
        _attn_kernel,
        grid=(B, N_KV_GROUPS, S // Q_BLOCK),
        in_specs=[pl.BlockSpec((None, gw, Q_BLOCK), lambda b, g, i: (b, g, i)),
                  pl.BlockSpec((None, GATE_PAD, Q_BLOCK), lambda b, g, i: (b, g, i)),
                  pl.BlockSpec((None, None, n_slc, Q_BLOCK), lambda b, g, i: (b, g, 0, i)),
                  pl.BlockSpec((None, None, HPG, HEAD_DIM, Q_BLOCK), lambda b, g, i: (b, g, 0, 0, i)),
                  rows(S, MXU_DEPTH), cols(S), rows(S, HEAD_DIM), cols(S),
                  pl.BlockSpec((gw, 1), lambda b, g, i: (g, 0))],
        out_specs=pl.BlockSpec((None, Q_BLOCK, gw), lambda b, g, i: (b, i, g)),
        out_shape=jax.ShapeDtypeStruct((B, S, ATTN_WIDTH), BF16),
        scratch_shapes=[pltpu.VMEM((KEY_CHUNK, QW), F32), pltpu.VMEM((KEY_CHUNK, QW), F32),
                        pltpu.VMEM((KEY_CHUNK, QW), BF16), pltpu.VMEM((KEY_CHUNK, QW), BF16),
                        pltpu.VMEM((WIN_KEYS, QW), F32), pltpu.VMEM((WIN_KEYS, QW), BF16),
                        pltpu.VMEM((MXU_DEPTH, QW), BF16), pltpu.VMEM((VT_ROWS, QW), F32)],
        compiler_params=pltpu.CompilerParams(
            dimension_semantics=("parallel", "parallel", "arbitrary"),
            vmem_limit_bytes=VMEM_LIMIT),
        name="nsa_attention",
    )(qT, gatesT, sel_bias, o_cmp, ksa, vsT, kw, vwT, g_attn_col)


def _out_kernel(x_ref, yc_ref, ya_ref, mod_ref, wo_ref, g_ref, wg_ref, wu_ref, wd_ref, gf_ref,
                o_ref, a_ref, wgb_ref, wub_ref):
    _cast_once((wg_ref, wu_ref), (wgb_ref, wub_ref))
    mix = _dot(yc_ref[...], wo_ref[0:CONV_CH, :]) + _dot(ya_ref[...], wo_ref[CONV_CH:, :])
    x = x_ref[...] + mod_ref[5:6, :] * mix
    x = _ffn_core(x, mod_ref[6:7, :], mod_ref[7:8, :], mod_ref[8:9, :], g_ref[...], wgb_ref,
                  wub_ref, wd_ref, a_ref)
    o_ref[...] = _rms(x, gf_ref[...])


def _out_call(x, yc, ya, mod, wo, g, wg, wu, wd, gf):
    B, S, D = x.shape
    tm = FFN_TOKEN_TILE
    tok = lambda w: pl.BlockSpec((None, tm, w), lambda b, i: (b, i, 0))
    return pl.pallas_call(
        _out_kernel,
        grid=(B, S // tm),
        in_specs=[tok(D), tok(CONV_CH), tok(ATTN_WIDTH),
                  pl.BlockSpec((None, N_MOD, D), lambda b, i: (b, 0, 0)),
                  _const_spec(wo.shape), _const_spec(g.shape), _const_spec(wg.shape),
                  _const_spec(wu.shape), _const_spec(wd.shape), _const_spec(gf.shape)],
        out_specs=tok(D),
        out_shape=jax.ShapeDtypeStruct((B, S, D), F32),
        scratch_shapes=[pltpu.VMEM((tm, wg.shape[1]), BF16), pltpu.VMEM(wg.shape, BF16),
                        pltpu.VMEM(wu.shape, BF16)],
        compiler_params=pltpu.CompilerParams(dimension_semantics=("arbitrary", "arbitrary"),
                                             vmem_limit_bytes=VMEM_LIMIT),
        name="outproj_ffn2",
    )(x, yc, ya, mod, wo, g, wg, wu, wd, gf)


def kernel(x, c, positions, w_ada, b_ada, g_ffn1, w1_gate, w1_up, w1_down, g_mix, w_in, conv_w, cmp_pos_k, cmp_pos_v, w_cmpk1, w_cmpk2, w_cmpv1, w_cmpv2, g_out_conv, g_out_attn, w_out, g_ffn2, w2_gate, w2_up, w2_down, g_final):
    B, S, D = x.shape
    depth = w_ada.shape[0]
    n_slc = S // SLC_BLOCK
    half = CMP_BLOCK // 2
    n_half = S // half
    assert n_slc <= LANES, "selection-block one-hot is one lane tile wide"

    c_pad = jnp.pad(c, ((0, 8 - B), (0, 0)))
    row = lambda a: a.reshape(1, -1)

    freq_half = jnp.power(ROPE_THETA, -2.0 * jnp.arange(ROT_HALF, dtype=F32) / ROT_DIM)
    freq = jnp.tile(freq_half, LANES // ROT_HALF).reshape(1, LANES)
    gidx = np.arange(CONV_CH) // (CONV_CH // CONV_GROUPS)
    gmat = jnp.asarray((gidx[:, None] == gidx[None, :]) / (CONV_CH // CONV_GROUPS), dtype=BF16)
    c0 = np.arange(n_half) * CMP_STRIDE
    s0 = np.arange(LANES) * SLC_BLOCK
    ovlT = ((c0[None, :] <= s0[:, None] + SLC_BLOCK - 1) & (c0[None, :] + CMP_BLOCK - 1 >= s0[:, None]))
    ovlT = jnp.asarray(ovlT, dtype=BF16)
    onehot = jnp.asarray((np.arange(S) // SLC_BLOCK)[:, None] == np.arange(LANES)[None, :], dtype=BF16)
    cos_p, sin_p = _rope_table_call(positions.reshape(-1, TOKENS_PER_ROW), freq)
    cos_p = cos_p.reshape(B, S // TOKENS_PER_ROW, LANES)
    sin_p = sin_p.reshape(B, S // TOKENS_PER_ROW, LANES)

    for l in range(depth):
        mod = _ada_call(c_pad, w_ada[l], row(b_ada[l]))[:B].reshape(B, N_MOD, D)

        x = _ffn_call(x, mod, row(g_ffn1[l]), w1_gate[l], w1_up[l], w1_down[l].astype(BF16))

        n_main = w_in.shape[2] - N_KV_GROUPS * 3 * HPG
        gate_cols = [jnp.pad(w_in[l][:, n_main + g * 3 * HPG:n_main + (g + 1) * 3 * HPG],
                             ((0, 0), (0, GATE_PAD - 3 * HPG))) for g in range(N_KV_GROUPS)]
        wgate = jnp.concatenate(gate_cols, axis=1).astype(BF16)
        win = w_in[l].astype(BF16)
        (yc, qT, kc, vc, ksa, vsT, kw, vwT, gatesT) = _inproj_call(
            x, cos_p, sin_p, mod, row(g_mix[l]), win, wgate, conv_w[l], row(g_out_conv[l]), gmat,
            onehot)

        both_groups = lambda pe: jnp.tile(pe, (1, N_KV_GROUPS))
        kcc, vcT = _compress_call(kc, vc, both_groups(cmp_pos_k[l]), both_groups(cmp_pos_v[l]),
                                  w_cmpk1[l], w_cmpk2[l], w_cmpv1[l], w_cmpv2[l])

        sel_bias, o_cmp = _select_call(min(SLC_TOP_N, n_slc), qT, kcc, vcT, ovlT)
        ya = _attn_call(qT, gatesT, sel_bias, o_cmp, ksa, vsT, kw, vwT,
                        g_out_attn[l].reshape(ATTN_WIDTH, 1))

        assert l == depth - 1, "final norm is fused into the last layer's output kernel"
        x = _out_call(x, yc, ya, mod, w_out[l].astype(BF16), row(g_ffn2[l]),
                      w2_gate[l], w2_up[l], w2_down[l].astype(BF16),
                      row(g_final))
    return x
```

```python
import functools
import math

import numpy as np
import jax
import jax.numpy as jnp
from jax import lax
from jax.experimental import pallas as pl
from jax.experimental.pallas import tpu as pltpu

F32 = jnp.float32
BF16 = jnp.bfloat16

CONV_CH = 512
CONV_GROUPS = 8
N_HEADS = 8
N_KV_GROUPS = 2
HPG = N_HEADS // N_KV_GROUPS
HEAD_DIM = 64
ATTN_WIDTH = N_HEADS * HEAD_DIM
KV_WIDTH = N_KV_GROUPS * HEAD_DIM
ROPE_THETA = 500000.0
ROT_DIM = HEAD_DIM // 4
ROT_HALF = ROT_DIM // 2
CMP_BLOCK = 32
CMP_STRIDE = 16
CMP_HIDDEN = 256
SLC_BLOCK = 64
SLC_TOP_N = 16
MAX_FORCED = 3
WINDOW = 512
Q_BLOCK = 256
MACARON_W = 0.5
N_MOD = 9
EPS = 1e-6
NEG = -1e30
BIG = 1e9

LANES = 128
MXU_DEPTH = 256
VMEM_LIMIT = 58 * 1024 * 1024

TOKEN_TILE = 512
FFN_TOKEN_TILE = 512
ADA_COL_TILE = 1024
FF_TILE = 256
KEY_CHUNK = 512
WIN_KEYS = WINDOW + Q_BLOCK
TOKENS_PER_ROW = LANES // ROT_HALF
GATE_PAD = LANES
QW = HPG * Q_BLOCK
SEL_BLOCK = 1024
SEL_ROW_STEP = 32
ONES_ROWS = 16
VT_ROWS = HEAD_DIM + ONES_ROWS
ROW_TILE = 16
Q_SCALE = HEAD_DIM ** -0.5 * math.log2(math.e)
M_FLOOR = -1e20


def _dot(a, b):
    return jnp.dot(a, b, preferred_element_type=F32)


def _rms(x, g):
    return x * lax.rsqrt(jnp.mean(x * x, axis=-1, keepdims=True) + EPS) * g


def _split_bf16(x):
    hi = x.astype(BF16)
    lo = (x - hi.astype(F32)).astype(BF16)
    return hi, lo


def _const_spec(shape):
    nd = len(shape)
    return pl.BlockSpec(shape, lambda *_: (0,) * nd, pipeline_mode=pl.Buffered(1))


def _ada_kernel(c_ref, w_ref, b_ref, o_ref):
    c = c_ref[...]
    c_act = c * jax.nn.sigmoid(c)
    o_ref[...] = _dot(c_act.astype(BF16), w_ref[...].astype(BF16)) + b_ref[...]


def _ada_call(c_pad, w_ada, b_ada):
    rows, d = c_pad.shape
    n = w_ada.shape[1]
    tn = ADA_COL_TILE
    return pl.pallas_call(
        _ada_kernel,
        grid=(n // tn,),
        in_specs=[pl.BlockSpec((rows, d), lambda j: (0, 0)),
                  pl.BlockSpec((d, tn), lambda j: (0, j)),
                  pl.BlockSpec((1, tn), lambda j: (0, j))],
        out_specs=pl.BlockSpec((rows, tn), lambda j: (0, j)),
        out_shape=jax.ShapeDtypeStruct((rows, n), F32),
        compiler_params=pltpu.CompilerParams(dimension_semantics=("arbitrary",),
                                             vmem_limit_bytes=VMEM_LIMIT),
        name="adaln_mod",
    )(c_pad, w_ada, b_ada)


def _ffn_core(x, shift, scale, gate, g, wg_ref, wu_ref, wd_ref, a_ref):
    h = _rms(x, g) * (1.0 + scale) + shift
    hb = h.astype(BF16)
    d_ff = wg_ref.shape[1]
    for j in range(d_ff // FF_TILE):
        sl = slice(j * FF_TILE, (j + 1) * FF_TILE)
        gg = _dot(hb, wg_ref[:, sl])
        uu = _dot(hb, wu_ref[:, sl])
        a_ref[:, sl] = (gg * jax.nn.sigmoid(gg) * uu).astype(a_ref.dtype)
    return x + (MACARON_W * gate) * _dot(a_ref[...], wd_ref[...])


def _cast_once(src_refs, dst_refs):
    @pl.when((pl.program_id(0) == 0) & (pl.program_id(1) == 0))
    def _():
        for src, dst in zip(src_refs, dst_refs):
            for c in range(0, src.shape[1], FF_TILE):
                dst[:, c:c + FF_TILE] = src[:, c:c + FF_TILE].astype(dst.dtype)


def _ffn_kernel(x_ref, mod_ref, g_ref, wg_ref, wu_ref, wd_ref, o_ref, a_ref, wgb_ref, wub_ref):
    _cast_once((wg_ref, wu_ref), (wgb_ref, wub_ref))
    o_ref[...] = _ffn_core(x_ref[...], mod_ref[0:1, :], mod_ref[1:2, :], mod_ref[2:3, :],
                           g_ref[...], wgb_ref, wub_ref, wd_ref, a_ref)


def _ffn_call(x, mod, g, wg, wu, wd):
    B, S, D = x.shape
    tm = FFN_TOKEN_TILE
    return pl.pallas_call(
        _ffn_kernel,
        grid=(B, S // tm),
        in_specs=[pl.BlockSpec((None, tm, D), lambda b, i: (b, i, 0)),
                  pl.BlockSpec((None, N_MOD, D), lambda b, i: (b, 0, 0)),
                  _const_spec(g.shape), _const_spec(wg.shape), _const_spec(wu.shape),
                  _const_spec(wd.shape)],
        out_specs=pl.BlockSpec((None, tm, D), lambda b, i: (b, i, 0)),
        out_shape=jax.ShapeDtypeStruct((B, S, D), F32),
        scratch_shapes=[pltpu.VMEM((tm, wg.shape[1]), BF16), pltpu.VMEM(wg.shape, BF16),
                        pltpu.VMEM(wu.shape, BF16)],
        compiler_params=pltpu.CompilerParams(dimension_semantics=("arbitrary", "arbitrary"),
                                             vmem_limit_bytes=VMEM_LIMIT),
        name="ffn1",
    )(x, mod, g, wg, wu, wd)


def _rope_table_kernel(pos_ref, freq_ref, cos_ref, sin_ref):
    pos = jnp.concatenate([pos_ref[...].astype(F32),
                           jnp.zeros((pos_ref.shape[0], LANES - TOKENS_PER_ROW), F32)], axis=1)
    src = lax.broadcasted_iota(jnp.int32, pos.shape, 1) // ROT_HALF
    ang = jnp.take_along_axis(pos, src, axis=1) * freq_ref[...]
    cos_ref[...] = jnp.cos(ang)
    sin_ref[...] = jnp.sin(ang)


def _rope_table_call(pos_rows, freq):
    rows = pos_rows.shape[0]
    shape = jax.ShapeDtypeStruct((rows, LANES), F32)
    full = pl.BlockSpec((rows, LANES), lambda: (0, 0))
    return pl.pallas_call(
        _rope_table_kernel,
        in_specs=[pl.BlockSpec(pos_rows.shape, lambda: (0, 0)), pl.BlockSpec(freq.shape, lambda: (0, 0))],
        out_specs=(full, full),
        out_shape=(shape, shape),
        compiler_params=pltpu.CompilerParams(vmem_limit_bytes=VMEM_LIMIT),
        name="rope_table",
    )(pos_rows, freq)


def _inproj_kernel(x_ref, cosp_ref, sinp_ref, mod_ref, gmix_ref, win_ref, wgate_ref, convw_ref, gconv_ref,
                   gmat_ref, onehot_ref, yc_ref, qT_ref, kc_ref, vc_ref, ksa_ref, vsT_ref, kw_ref,
                   vwT_ref, gT_ref, carry_ref):
    tm = x_ref.shape[0]

    @pl.when(pl.program_id(1) == 0)
    def _():
        carry_ref[...] = jnp.zeros_like(carry_ref)

    x = x_ref[...]
    h = _rms(x, gmix_ref[...]) * (1.0 + mod_ref[4:5, :]) + mod_ref[3:4, :]
    hb = h.astype(BF16)

    def proj(c0, width):
        return _dot(hb, win_ref[:, c0:c0 + width])

    d = lax.broadcasted_iota(jnp.int32, (tm, LANES), 1) & (HEAD_DIM - 1)
    token = lax.broadcasted_iota(jnp.int32, (tm, LANES), 0) & (TOKENS_PER_ROW - 1)
    src_lane = token * ROT_HALF + (d & (ROT_HALF - 1))

    def unpack(packed_ref):
        rows = jnp.broadcast_to(packed_ref[...][:, None, :], (tm // TOKENS_PER_ROW, TOKENS_PER_ROW, LANES))
        return jnp.take_along_axis(rows.reshape(tm, LANES), src_lane, axis=1)

    cos_t = jnp.where(d < ROT_DIM, unpack(cosp_ref), 1.0)
    sin_raw = unpack(sinp_ref)
    sin_t = jnp.where(d < ROT_HALF, -sin_raw, jnp.where(d < ROT_DIM, sin_raw, 0.0))
    first_half = d < ROT_HALF

    def rope(t):
        outs = []
        for j in range(t.shape[1] // LANES):
            tj = t[:, j * LANES:(j + 1) * LANES]
            partner = jnp.where(first_half, pltpu.roll(tj, LANES - ROT_HALF, 1),
                                pltpu.roll(tj, ROT_HALF, 1))
            outs.append(tj * cos_t + partner * sin_t)
        return outs[0] if len(outs) == 1 else jnp.concatenate(outs, axis=1)

    c0 = 3 * CONV_CH
    q = rope(proj(c0, ATTN_WIDTH)) * Q_SCALE
    qT_ref[...] = q.T.astype(qT_ref.dtype)
    c0 += ATTN_WIDTH
    kv = proj(c0, 2 * KV_WIDTH)
    kc_ref[...] = rope(kv[:, :KV_WIDTH])
    vc_ref[...] = kv[:, KV_WIDTH:]
    kv = proj(c0 + 2 * KV_WIDTH, 2 * KV_WIDTH)
    ks = rope(kv[:, :KV_WIDTH]).astype(BF16)
    vsT = kv[:, KV_WIDTH:].T.astype(BF16)
    kv = proj(c0 + 4 * KV_WIDTH, 2 * KV_WIDTH)
    kw = rope(kv[:, :KV_WIDTH]).astype(BF16)
    vwT = kv[:, KV_WIDTH:].T.astype(BF16)
    pad = jnp.zeros((tm, MXU_DEPTH - LANES - HEAD_DIM), BF16)
    ones = jnp.ones((ONES_ROWS, tm), BF16)
    for g in range(N_KV_GROUPS):
        kg = ks[:, g * HEAD_DIM:(g + 1) * HEAD_DIM]
        ksa_ref[g] = jnp.concatenate([onehot_ref[...], kg, pad], axis=1)
        kw_ref[g] = kw[:, g * HEAD_DIM:(g + 1) * HEAD_DIM]
        vsT_ref[g] = jnp.concatenate([vsT[g * HEAD_DIM:(g + 1) * HEAD_DIM, :], ones], axis=0)
        vwT_ref[g] = jnp.concatenate([vwT[g * HEAD_DIM:(g + 1) * HEAD_DIM, :], ones], axis=0)
    gT_ref[...] = jax.nn.sigmoid(_dot(hb, wgate_ref[...])).T

    cb = proj(0, CONV_CH)
    u = proj(CONV_CH, CONV_CH) * proj(2 * CONV_CH, CONV_CH)
    row = lax.broadcasted_iota(jnp.int32, (tm, 1), 0)
    prev1 = carry_ref[7:8, :]
    prev2 = carry_ref[6:7, :]
    u1 = jnp.where(row >= 1, pltpu.roll(u, 1, 0), prev1)
    u2 = jnp.where(row >= 2, pltpu.roll(u, 2, 0), jnp.where(row == 1, prev1, prev2))
    carry_ref[...] = u[tm - 8:tm, :]
    v = convw_ref[0:1, :] * u2 + convw_ref[1:2, :] * u1 + convw_ref[2:3, :] * u
    y = cb * v
    ms = _dot((y * y).astype(BF16), gmat_ref[...])
    yc_ref[...] = (y * lax.rsqrt(ms + EPS) * gconv_ref[...]).astype(yc_ref.dtype)


def _inproj_call(x, cos_p, sin_p, mod, gmix, win, wgate, convw, gconv, gmat, onehot):
    B, S, D = x.shape
    tm = TOKEN_TILE
    tok = lambda w: pl.BlockSpec((None, tm, w), lambda b, i: (b, i, 0))
    tr = lambda w: pl.BlockSpec((None, w, tm), lambda b, i: (b, 0, i))
    grp = lambda w: pl.BlockSpec((None, N_KV_GROUPS, tm, w), lambda b, i: (b, 0, i, 0))
    grpT = pl.BlockSpec((None, N_KV_GROUPS, VT_ROWS, tm), lambda b, i: (b, 0, 0, i))
    packed = pl.BlockSpec((None, tm // TOKENS_PER_ROW, LANES), lambda b, i: (b, i, 0))
    out_shapes = (
        jax.ShapeDtypeStruct((B, S, CONV_CH), BF16),
        jax.ShapeDtypeStruct((B, ATTN_WIDTH, S), BF16),
        jax.ShapeDtypeStruct((B, S, KV_WIDTH), F32),
        jax.ShapeDtypeStruct((B, S, KV_WIDTH), F32),
        jax.ShapeDtypeStruct((B, N_KV_GROUPS, S, MXU_DEPTH), BF16),
        jax.ShapeDtypeStruct((B, N_KV_GROUPS, VT_ROWS, S), BF16),
        jax.ShapeDtypeStruct((B, N_KV_GROUPS, S, HEAD_DIM), BF16),
        jax.ShapeDtypeStruct((B, N_KV_GROUPS, VT_ROWS, S), BF16),
        jax.ShapeDtypeStruct((B, N_KV_GROUPS * GATE_PAD, S), F32),
    )
    return pl.pallas_call(
        _inproj_kernel,
        grid=(B, S // tm),
        in_specs=[tok(D),
                  packed, packed,
                  pl.BlockSpec((None, N_MOD, D), lambda b, i: (b, 0, 0)),
                  _const_spec(gmix.shape), _const_spec(win.shape), _const_spec(wgate.shape),
                  _const_spec(convw.shape),
                  _const_spec(gconv.shape), _const_spec(gmat.shape),
                  pl.BlockSpec((tm, onehot.shape[1]), lambda b, i: (i, 0))],
        out_specs=(tok(CONV_CH), tr(ATTN_WIDTH), tok(KV_WIDTH), tok(KV_WIDTH), grp(MXU_DEPTH),
                   grpT, grp(HEAD_DIM), grpT, tr(N_KV_GROUPS * GATE_PAD)),
        out_shape=out_shapes,
        scratch_shapes=[pltpu.VMEM((8, CONV_CH), F32)],
        compiler_params=pltpu.CompilerParams(dimension_semantics=("arbitrary", "arbitrary"),
                                             vmem_limit_bytes=VMEM_LIMIT),
        name="mixer_inproj",
    )(x, cos_p, sin_p, mod, gmix, win, wgate, convw, gconv, gmat, onehot)


def _compress_kernel(kf_ref, vf_ref, pek_ref, pev_ref, wk1_ref, wk2_ref, wv1_ref, wv2_ref,
                     kcc_ref, vcT_ref):
    half = CMP_BLOCK // 2
    n = kf_ref.shape[0] // half

    def mlp(x_ref, pe_ref, w1_ref, w2_ref):
        parts = []
        for p in range(2):
            acc = None
            for l0 in range(0, half, 2):
                xs, ws = [], []
                for l in (l0, l0 + 1):
                    row = p * half + l
                    xs.append((x_ref[pl.ds(l, n, stride=half), :]
                               + pe_ref[row:row + 1, :]).astype(BF16))
                    w = w1_ref[row * HEAD_DIM:(row + 1) * HEAD_DIM, :].astype(BF16)
                    z = jnp.zeros_like(w)
                    ws.append(jnp.concatenate([jnp.concatenate([w, z], axis=1),
                                               jnp.concatenate([z, w], axis=1)], axis=0))
                d = _dot(jnp.concatenate(xs, axis=1), jnp.concatenate(ws, axis=0))
                acc = d if acc is None else acc + d
            parts.append(acc)
        hpre = parts[0] + pltpu.roll(parts[1], n - 1, 0)
        hid = (hpre * jax.nn.sigmoid(hpre)).astype(BF16)
        w2 = w2_ref[...].astype(BF16)
        return jnp.concatenate([_dot(hid[:, g * CMP_HIDDEN:(g + 1) * CMP_HIDDEN], w2)
                                for g in range(N_KV_GROUPS)], axis=1)

    kc = mlp(kf_ref, pek_ref, wk1_ref, wk2_ref).astype(kcc_ref.dtype)
    for g in range(N_KV_GROUPS):
        kcc_ref[g] = kc[:, g * HEAD_DIM:(g + 1) * HEAD_DIM]
    vcT = mlp(vf_ref, pev_ref, wv1_ref, wv2_ref).T.astype(vcT_ref.dtype)
    ones = jnp.ones((ONES_ROWS, n), vcT_ref.dtype)
    for g in range(N_KV_GROUPS):
        vcT_ref[g] = jnp.concatenate([vcT[g * HEAD_DIM:(g + 1) * HEAD_DIM, :], ones], axis=0)


def _compress_call(kf, vf, pek, pev, wk1, wk2, wv1, wv2):
    B, S, width = kf.shape
    n = S // (CMP_BLOCK // 2)
    flat = pl.BlockSpec((None, S, width), lambda b: (b, 0, 0))
    return pl.pallas_call(
        _compress_kernel,
        grid=(B,),
        in_specs=[flat, flat, _const_spec(pek.shape), _const_spec(pev.shape),
                  _const_spec(wk1.shape), _const_spec(wk2.shape), _const_spec(wv1.shape),
                  _const_spec(wv2.shape)],
        out_specs=(pl.BlockSpec((None, N_KV_GROUPS, n, HEAD_DIM), lambda b: (b, 0, 0, 0)),
                   pl.BlockSpec((None, N_KV_GROUPS, VT_ROWS, n), lambda b: (b, 0, 0, 0))),
        out_shape=(jax.ShapeDtypeStruct((B, N_KV_GROUPS, n, HEAD_DIM), BF16),
                   jax.ShapeDtypeStruct((B, N_KV_GROUPS, VT_ROWS, n), BF16)),
        compiler_params=pltpu.CompilerParams(dimension_semantics=("arbitrary",),
                                             vmem_limit_bytes=VMEM_LIMIT),
        name="kv_compress",
    )(kf, vf, pek, pev, wk1, wk2, wv1, wv2)


def _lane_tiles(x, n):
    return jnp.concatenate([x] * n, axis=1)


def _col_max(s_ref, n_rows, bias_fn, lane_tile=None):
    width = s_ref.shape[1]
    lt = lane_tile or width
    groups = ROW_TILE // 8
    mx = {c: [jnp.full((8, lt), NEG, F32)] * groups for c in range(0, width, lt)}
    for r in range(0, n_rows, ROW_TILE):
        bias = None if bias_fn is None else bias_fn(r)
        for c in range(0, width, lt):
            x = s_ref[r:r + ROW_TILE, c:c + lt]
            if bias is not None:
                x = x + bias
                s_ref[r:r + ROW_TILE, c:c + lt] = x
            mx[c] = [jnp.maximum(mx[c][i], x[8 * i:8 * (i + 1), :]) for i in range(groups)]
    out = []
    for c in range(0, width, lt):
        m = mx[c]
        while len(m) > 1:
            m = [jnp.maximum(a, b) for a, b in zip(m[0::2], m[1::2])]
        out.append(jnp.max(m[0], axis=0, keepdims=True))
    return out[0] if len(out) == 1 else jnp.concatenate(out, axis=1)


def _col_exp2(s_ref, p_ref, n_rows, m, keep_f32=False, lane_tile=None):
    width = s_ref.shape[1]
    lt = lane_tile or width
    for r in range(0, n_rows, ROW_TILE):
        for c in range(0, width, lt):
            p = jnp.exp2(s_ref[r:r + ROW_TILE, c:c + lt] - m[:, c:c + lt])
            if keep_f32:
                s_ref[r:r + ROW_TILE, c:c + lt] = p
            p_ref[r:r + ROW_TILE, c:c + lt] = p.astype(p_ref.dtype)


def _recip_pos(l):
    return 1.0 / jnp.where(l > 0.0, l, 1.0)


def _select_kernel(top_n, qT_ref, kcc_ref, vcT_ref, ovlT_ref, sel_ref, ocmp_ref, sc_ref, pc_ref,
                   ph_ref, pl_ref):
    QB = SEL_BLOCK
    n_cmp = kcc_ref.shape[0]
    n_slc = ovlT_ref.shape[0]
    t0 = pl.program_id(2) * QB
    tq = t0 + lax.broadcasted_iota(jnp.int32, (1, QB), 1)
    rows = lax.broadcasted_iota(jnp.int32, (ROW_TILE, 1), 0)

    qT = qT_ref[...]
    qcat = jnp.concatenate([qT[h * HEAD_DIM:(h + 1) * HEAD_DIM, :] for h in range(HPG)], axis=1)

    def cmp_bias(r):
        cmp_end = (r + rows) * CMP_STRIDE + (CMP_BLOCK - 1)
        return jnp.where(cmp_end <= tq, 0.0, NEG)

    cur = tq // SLC_BLOCK

    def tree(op, xs):
        while len(xs) > 1:
            xs = [op(*xs[i:i + 2]) if i + 1 < len(xs) else xs[i] for i in range(0, len(xs), 2)]
        return xs[0]

    def causal_variant(n_rows):
        rows_cmp = min(n_cmp, n_rows * (SLC_BLOCK // CMP_STRIDE))

        def run():
            sc_ref[0:rows_cmp, :] = _dot(kcc_ref[0:rows_cmp, :], qcat)
            m = jnp.maximum(_col_max(sc_ref, rows_cmp, cmp_bias, lane_tile=QB), M_FLOOR)
            _col_exp2(sc_ref, pc_ref, rows_cmp, m, keep_f32=True, lane_tile=QB)
            o_cmp = _dot(vcT_ref[:, 0:rows_cmp], pc_ref[0:rows_cmp, :])
            rl = _recip_pos(o_cmp[HEAD_DIM:HEAD_DIM + 1, :])
            o_cmp = o_cmp[0:HEAD_DIM, :] * rl
            for h in range(HPG):
                ocmp_ref[h] = o_cmp[:, h * QB:(h + 1) * QB]

            for r in range(0, rows_cmp, ROW_TILE):
                psum = sc_ref[r:r + ROW_TILE, 0:QB] * rl[:, 0:QB]
                for h in range(1, HPG):
                    sl = slice(h * QB, (h + 1) * QB)
                    psum = psum + sc_ref[r:r + ROW_TILE, sl] * rl[:, sl]
                hi, lo = _split_bf16(psum)
                ph_ref[r:r + ROW_TILE, :] = hi
                pl_ref[r:r + ROW_TILE, :] = lo
            ovl = ovlT_ref[0:n_rows, 0:rows_cmp]
            imp = _dot(ovl, ph_ref[0:rows_cmp, :]) + _dot(ovl, pl_ref[0:rows_cmp, :])

            blk = lax.broadcasted_iota(jnp.int32, (n_rows, 1), 0)
            future = blk > cur
            forced = (blk == 0) | (blk == cur) | (blk == cur - 1)
            groups = range(n_rows // 8)

            def store(picked):
                sel_ref[0:n_rows, :] = jnp.where(future, NEG, jnp.where(picked, 0.0, NEG)).astype(sel_ref.dtype)

            def with_ties():
                score0 = jnp.where(future, -BIG, jnp.where(forced, BIG, imp))
                score = [score0[8 * i:8 * (i + 1), :] for i in groups]
                cum = jnp.zeros((1, QB), F32)
                thr = jnp.zeros((1, QB), F32)
                above = jnp.zeros((1, QB), F32)
                for _ in range(top_n):
                    best = jnp.max(tree(jnp.maximum, score), axis=0, keepdims=True)
                    eq = [s == best for s in score]
                    unfilled = cum < top_n
                    thr = jnp.where(unfilled, best, thr)
                    above = jnp.where(unfilled, cum, above)
                    cum = cum + jnp.sum(tree(jnp.add, [jnp.where(e, 1.0, 0.0) for e in eq]),
                                        axis=0, keepdims=True)
                    score = [jnp.where(e, -jnp.inf, s) for e, s in zip(eq, score)]
                ties = score0 == thr
                lower = jnp.where(lax.broadcasted_iota(jnp.int32, (1, n_rows), 1) < blk, 1.0, 0.0)
                rank = _dot(lower.astype(BF16), jnp.where(ties, 1.0, 0.0).astype(BF16))
                store((score0 > thr) | (ties & (rank < top_n - above)))

            if top_n > MAX_FORCED:
                n_forced = 1.0 + jnp.where(cur >= 1, 1.0, 0.0) + jnp.where(cur >= 2, 1.0, 0.0)
                wanted = top_n - n_forced
                take_all = (cur.astype(F32) + 1.0 - n_forced) <= wanted
                cand = jnp.where(future | forced, -BIG, imp)
                score = [cand[8 * i:8 * (i + 1), :] for i in groups]
                for _ in range(top_n - MAX_FORCED):
                    cut = jnp.max(tree(jnp.maximum, score), axis=0, keepdims=True)
                    score = [jnp.where(s == cut, -jnp.inf, s) for s in score]
                above_cut = cand >= cut
                count = jnp.sum(jnp.where(above_cut, 1.0, 0.0), axis=0, keepdims=True)
                exact = jnp.min(jnp.where(take_all | (count == wanted), 1.0, 0.0)) > 0.5
                lax.cond(exact, lambda: store(forced | take_all | above_cut), with_ties)
            else:
                with_ties()
            if n_rows < n_slc:
                sel_ref[n_rows:n_slc, :] = jnp.full((n_slc - n_rows, QB), NEG, sel_ref.dtype)
        return run

    steps = n_slc // SEL_ROW_STEP
    need = jnp.minimum((t0 + QB - 1) // (SLC_BLOCK * SEL_ROW_STEP), steps - 1)
    lax.switch(need, [causal_variant(SEL_ROW_STEP * (k + 1)) for k in range(steps)])


def _select_call(top_n, qT, kcc, vcT, ovlT):
    B, _, S = qT.shape
    n_cmp = kcc.shape[2]
    n_slc = ovlT.shape[0]
    gw = HPG * HEAD_DIM
    qw = HPG * SEL_BLOCK
    return pl.pallas_call(
        functools.partial(_select_kernel, top_n),
        grid=(B, N_KV_GROUPS, S // SEL_BLOCK),
        in_specs=[pl.BlockSpec((None, gw, SEL_BLOCK), lambda b, g, i: (b, g, i)),
                  pl.BlockSpec((None, None, n_cmp, HEAD_DIM), lambda b, g, i: (b, g, 0, 0)),
                  pl.BlockSpec((None, None, VT_ROWS, n_cmp), lambda b, g, i: (b, g, 0, 0)),
                  pl.BlockSpec(ovlT.shape, lambda b, g, i: (0, 0))],
        out_specs=(pl.BlockSpec((None, None, n_slc, SEL_BLOCK), lambda b, g, i: (b, g, 0, i)),
                   pl.BlockSpec((None, None, HPG, HEAD_DIM, SEL_BLOCK),
                                lambda b, g, i: (b, g, 0, 0, i))),
        out_shape=(jax.ShapeDtypeStruct((B, N_KV_GROUPS, n_slc, S), BF16),
                   jax.ShapeDtypeStruct((B, N_KV_GROUPS, HPG, HEAD_DIM, S), F32)),
        scratch_shapes=[pltpu.VMEM((n_cmp, qw), F32), pltpu.VMEM((n_cmp, qw), BF16),
                        pltpu.VMEM((n_cmp, SEL_BLOCK), BF16), pltpu.VMEM((n_cmp, SEL_BLOCK), BF16)],
        compiler_params=pltpu.CompilerParams(
            dimension_semantics=("parallel", "parallel", "arbitrary"),
            vmem_limit_bytes=VMEM_LIMIT),
        name="nsa_select",
    )(qT, kcc, vcT, ovlT)


def _attn_kernel(qT_ref, gT_ref, sel_ref, ocmp_ref, ksa_ref, vsT_ref, kw_ref, vwT_ref, ga_ref,
                 o_ref, s0_ref, s1_ref, p0_ref, p1_ref, sw_ref, pw_ref, rhs_ref, acc_ref):
    QB = Q_BLOCK
    n_slc = sel_ref.shape[0]
    t0 = pl.program_id(2) * QB
    tq = t0 + lax.broadcasted_iota(jnp.int32, (1, QB), 1)
    rows = lax.broadcasted_iota(jnp.int32, (ROW_TILE, 1), 0)

    qT = qT_ref[...]
    qcat = jnp.concatenate([qT[h * HEAD_DIM:(h + 1) * HEAD_DIM, :] for h in range(HPG)], axis=1)

    rhs_ref[...] = jnp.concatenate([_lane_tiles(sel_ref[...], HPG), qcat,
                                    jnp.zeros((MXU_DEPTH - n_slc - HEAD_DIM, QW), BF16)], axis=0)

    w0 = pl.multiple_of(jnp.maximum(t0 - WINDOW, 0), QB)
    sw_ref[...] = _dot(kw_ref[pl.ds(w0, WIN_KEYS), :], qcat)

    def win_bias(r):
        dist = tq - (w0 + r + rows)
        return _lane_tiles(jnp.where((dist >= 0) & (dist < WINDOW), 0.0, NEG), HPG)

    KC = KEY_CHUNK
    last = t0 // KC

    def scores(c, dst_ref):
        k0 = pl.multiple_of(jnp.minimum(c, last) * KC, KC)
        dst_ref[...] = _dot(ksa_ref[pl.ds(k0, KC), :], rhs_ref[...])

    def weighted_values(c, p_ref):
        k0 = pl.multiple_of(jnp.clip(c, 0, last) * KC, KC)
        return _dot(vsT_ref[:, pl.ds(k0, KC)], p_ref[...])

    def softmax(c, src_ref, p_ref, m, pending, causal):
        def causal_bias(r):
            return _lane_tiles(jnp.where(c * KC + r + rows <= tq, 0.0, NEG), HPG)

        m_new = jnp.maximum(m, _col_max(src_ref, KC, causal_bias if causal else None))
        acc = acc_ref[...] if pending is None else acc_ref[...] + pending
        acc_ref[...] = jnp.exp2(m - m_new) * acc
        _col_exp2(src_ref, p_ref, KC, m_new)
        return m_new

    def pair(i, m, final=False, first=False):
        pending = None if first else weighted_values(2 * i - 1, p1_ref)
        scores(2 * i + 1, s1_ref)
        m = softmax(2 * i, s0_ref, p0_ref, m, pending, False)
        pending = weighted_values(2 * i, p0_ref)
        if not final:
            scores(2 * i + 2, s0_ref)
        return softmax(2 * i + 1, s1_ref, p1_ref, m, pending, final)

    p1_ref[...] = jnp.zeros_like(p1_ref)
    acc_ref[...] = jnp.zeros_like(acc_ref)
    scores(0, s0_ref)

    m = _col_max(sw_ref, WIN_KEYS, win_bias)
    _col_exp2(sw_ref, pw_ref, WIN_KEYS, m)
    o_win = _dot(vwT_ref[:, pl.ds(w0, WIN_KEYS)], pw_ref[...])
    o_win = o_win[0:HEAD_DIM, :] * (1.0 / o_win[HEAD_DIM:HEAD_DIM + 1, :])

    full_pairs = last // 2
    m = lax.cond(full_pairs >= 1, lambda mm: pair(0, mm, first=True), lambda mm: mm,
                 jnp.full((1, QW), NEG, F32))
    rest = jnp.maximum(full_pairs - 1, 0)
    m = lax.fori_loop(0, rest // 2, lambda j, mm: pair(2 * j + 2, pair(2 * j + 1, mm)), m)
    m = lax.cond(rest % 2 == 1, lambda mm: pair(full_pairs - 1, mm), lambda mm: mm, m)

    @pl.when(last % 2 == 0)
    def _():
        pending = weighted_values(last - 1, p1_ref)
        softmax(last, s0_ref, p0_ref, m, pending, True)
        acc_ref[...] += weighted_values(last, p0_ref)

    @pl.when(last % 2 == 1)
    def _():
        pair(full_pairs, m, final=True)
        acc_ref[...] += weighted_values(last, p1_ref)

    acc = acc_ref[...]
    o_slc = acc[0:HEAD_DIM, :] * (1.0 / acc[HEAD_DIM:HEAD_DIM + 1, :])

    gts = gT_ref[...]
    outs = []
    for h in range(HPG):
        sl = slice(h * QB, (h + 1) * QB)
        o = (gts[3 * h:3 * h + 1, :] * ocmp_ref[h] + gts[3 * h + 1:3 * h + 2, :] * o_slc[:, sl]
             + gts[3 * h + 2:3 * h + 3, :] * o_win[:, sl])
        o = o * lax.rsqrt(jnp.mean(o * o, axis=0, keepdims=True) + EPS)
        outs.append((o * ga_ref[h * HEAD_DIM:(h + 1) * HEAD_DIM, :]).T)
    o_ref[...] = jnp.concatenate(outs, axis=1).astype(o_ref.dtype)


def _attn_call(qT, gatesT, sel_bias, o_cmp, ksa, vsT, kw, vwT, g_attn_col):
    B, _, S = qT.shape
    n_slc = sel_bias.shape[2]
    gw = HPG * HEAD_DIM
    rows = lambda n, w: pl.BlockSpec((None, None, n, w), lambda b, g, i: (b, g, 0, 0))
    cols = lambda n: pl.BlockSpec((None, None, VT_ROWS, n), lambda b, g, i: (b, g, 0, 0))
    return pl.pallas_call(
        _attn_kernel,
        grid=(B, N_KV_GROUPS, S // Q_BLOCK),
        in_specs=[pl.BlockSpec((None, gw, Q_BLOCK), lambda b, g, i: (b, g, i)),
                  pl.BlockSpec((None, GATE_PAD, Q_BLOCK), lambda b, g, i: (b, g, i)),
                  pl.BlockSpec((None, None, n_slc, Q_BLOCK), lambda b, g, i: (b, g, 0, i)),
                  pl.BlockSpec((None, None, HPG, HEAD_DIM, Q_BLOCK), lambda b, g, i: (b, g, 0, 0, i)),
                  rows(S, MXU_DEPTH), cols(S), rows(S, HEAD_DIM), cols(S),
                  pl.BlockSpec((gw, 1), lambda b, g, i: (g, 0))],
        out_specs=pl.BlockSpec((None, Q_BLOCK, gw), lambda b, g, i: (b, i, g)),
        out_shape=jax.ShapeDtypeStruct((B, S, ATTN_WIDTH), BF16),
        scratch_shapes=[pltpu.VMEM((KEY_CHUNK, QW), F32), pltpu.VMEM((KEY_CHUNK, QW), F32),
                        pltpu.VMEM((KEY_CHUNK, QW), BF16), pltpu.VMEM((KEY_CHUNK, QW), BF16),
                        pltpu.VMEM((WIN_KEYS, QW), F32), pltpu.VMEM((WIN_KEYS, QW), BF16),
                        pltpu.VMEM((MXU_DEPTH, QW), BF16), pltpu.VMEM((VT_ROWS, QW), F32)],
        compiler_params=pltpu.CompilerParams(
            dimension_semantics=("parallel", "parallel", "arbitrary"),
            vmem_limit_bytes=VMEM_LIMIT),
        name="nsa_attention",
    )(qT, gatesT, sel_bias, o_cmp, ksa, vsT, kw, vwT, g_attn_col)


def _out_kernel(x_ref, yc_ref, ya_ref, mod_ref, wo_ref, g_ref, wg_ref, wu_ref, wd_ref, gf_ref,
                o_ref, a_ref, wgb_ref, wub_ref):
    _cast_once((wg_ref, wu_ref), (wgb_ref, wub_ref))
    mix = _dot(yc_ref[...], wo_ref[0:CONV_CH, :]) + _dot(ya_ref[...], wo_ref[CONV_CH:, :])
    x = x_ref[...] + mod_ref[5:6, :] * mix
    x = _ffn_core(x, mod_ref[6:7, :], mod_ref[7:8, :], mod_ref[8:9, :], g_ref[...], wgb_ref,
                  wub_ref, wd_ref, a_ref)
    o_ref[...] = _rms(x, gf_ref[...])


def _out_call(x, yc, ya, mod, wo, g, wg, wu, wd, gf):
    B, S, D = x.shape
    tm = FFN_TOKEN_TILE
    tok = lambda w: pl.BlockSpec((None, tm, w), lambda b, i: (b, i, 0))
    return pl.pallas_call(
        _out_kernel,
        grid=(B, S // tm),
        in_specs=[tok(D), tok(CONV_CH), tok(ATTN_WIDTH),
                  pl.BlockSpec((None, N_MOD, D), lambda b, i: (b, 0, 0)),
                  _const_spec(wo.shape), _const_spec(g.shape), _const_spec(wg.shape),
                  _const_spec(wu.shape), _const_spec(wd.shape), _const_spec(gf.shape)],
        out_specs=tok(D),
        out_shape=jax.ShapeDtypeStruct((B, S, D), F32),
        scratch_shapes=[pltpu.VMEM((tm, wg.shape[1]), BF16), pltpu.VMEM(wg.shape, BF16),
                        pltpu.VMEM(wu.shape, BF16)],
        compiler_params=pltpu.CompilerParams(dimension_semantics=("arbitrary", "arbitrary"),
                                             vmem_limit_bytes=VMEM_LIMIT),
        name="outproj_ffn2",
    )(x, yc, ya, mod, wo, g, wg, wu, wd, gf)


def kernel(x, c, positions, w_ada, b_ada, g_ffn1, w1_gate, w1_up, w1_down, g_mix, w_in, conv_w, cmp_pos_k, cmp_pos_v, w_cmpk1, w_cmpk2, w_cmpv1, w_cmpv2, g_out_conv, g_out_attn, w_out, g_ffn2, w2_gate, w2_up, w2_down, g_final):
    B, S, D = x.shape
    depth = w_ada.shape[0]
    n_slc = S // SLC_BLOCK
    half = CMP_BLOCK // 2
    n_half = S // half
    assert n_slc <= LANES, "selection-block one-hot is one lane tile wide"

    c_pad = jnp.pad(c, ((0, 8 - B), (0, 0)))
    row = lambda a: a.reshape(1, -1)

    freq_half = jnp.power(ROPE_THETA, -2.0 * jnp.arange(ROT_HALF, dtype=F32) / ROT_DIM)
    freq = jnp.tile(freq_half, LANES // ROT_HALF).reshape(1, LANES)
    gidx = np.arange(CONV_CH) // (CONV_CH // CONV_GROUPS)
    gmat = jnp.asarray((gidx[:, None] == gidx[None, :]) / (CONV_CH // CONV_GROUPS), dtype=BF16)
    c0 = np.arange(n_half) * CMP_STRIDE
    s0 = np.arange(LANES) * SLC_BLOCK
    ovlT = ((c0[None, :] <= s0[:, None] + SLC_BLOCK - 1) & (c0[None, :] + CMP_BLOCK - 1 >= s0[:, None]))
    ovlT = jnp.asarray(ovlT, dtype=BF16)
    onehot = jnp.asarray((np.arange(S) // SLC_BLOCK)[:, None] == np.arange(LANES)[None, :], dtype=BF16)
    cos_p, sin_p = _rope_table_call(positions.reshape(-1, TOKENS_PER_ROW), freq)
    cos_p = cos_p.reshape(B, S // TOKENS_PER_ROW, LANES)
    sin_p = sin_p.reshape(B, S // TOKENS_PER_ROW, LANES)

    for l in range(depth):
        mod = _ada_call(c_pad, w_ada[l], row(b_ada[l]))[:B].reshape(B, N_MOD, D)

        x = _ffn_call(x, mod, row(g_ffn1[l]), w1_gate[l], w1_up[l], w1_down[l].astype(BF16))

        n_main = w_in.shape[2] - N_KV_GROUPS * 3 * HPG
        gate_cols = [jnp.pad(w_in[l][:, n_main + g * 3 * HPG:n_main + (g + 1) * 3 * HPG],
                             ((0, 0), (0, GATE_PAD - 3 * HPG))) for g in range(N_KV_GROUPS)]
        wgate = jnp.concatenate(gate_cols, axis=1).astype(BF16)
        win = w_in[l].astype(BF16)
        (yc, qT, kc, vc, ksa, vsT, kw, vwT, gatesT) = _inproj_call(
            x, cos_p, sin_p, mod, row(g_mix[l]), win, wgate, conv_w[l], row(g_out_conv[l]), gmat,
            onehot)

        both_groups = lambda pe: jnp.tile(pe, (1, N_KV_GROUPS))
        kcc, vcT = _compress_call(kc, vc, both_groups(cmp_pos_k[l]), both_groups(cmp_pos_v[l]),
                                  w_cmpk1[l], w_cmpk2[l], w_cmpv1[l], w_cmpv2[l])

        sel_bias, o_cmp = _select_call(min(SLC_TOP_N, n_slc), qT, kcc, vcT, ovlT)
        ya = _attn_call(qT, gatesT, sel_bias, o_cmp, ksa, vsT, kw, vwT,
                        g_out_attn[l].reshape(ATTN_WIDTH, 1))

        assert l == depth - 1, "final norm is fused into the last layer's output kernel"
        x = _out_call(x, yc, ya, mod, w_out[l].astype(BF16), row(g_ffn2[l]),
                      w2_gate[l], w2_up[l], w2_down[l].astype(BF16),
                      row(g_final))
    return x
```

```python
import functools
import math

import numpy as np
import jax
import jax.numpy as jnp
from jax import lax
from jax.experimental import pallas as pl
from jax.experimental.pallas import tpu as pltpu

F32 = jnp.float32
BF16 = jnp.bfloat16

CONV_CH = 512
CONV_GROUPS = 8
N_HEADS = 8
N_KV_GROUPS = 2
HPG = N_HEADS // N_KV_GROUPS
HEAD_DIM = 64
ATTN_WIDTH = N_HEADS * HEAD_DIM
KV_WIDTH = N_KV_GROUPS * HEAD_DIM
ROPE_THETA = 500000.0
ROT_DIM = HEAD_DIM // 4
ROT_HALF = ROT_DIM // 2
CMP_BLOCK = 32
CMP_STRIDE = 16
CMP_HIDDEN = 256
SLC_BLOCK = 64
SLC_TOP_N = 16
MAX_FORCED = 3
WINDOW = 512
Q_BLOCK = 256
MACARON_W = 0.5
N_MOD = 9
EPS = 1e-6
NEG = -1e30
BIG = 1e9

LANES = 128
MXU_DEPTH = 256
VMEM_LIMIT = 58 * 1024 * 1024

TOKEN_TILE = 512
FFN_TOKEN_TILE = 512
ADA_COL_TILE = 1024
FF_TILE = 256
KEY_CHUNK = 512
WIN_KEYS = WINDOW + Q_BLOCK
TOKENS_PER_ROW = LANES // ROT_HALF
GATE_PAD = LANES
QW = HPG * Q_BLOCK
SEL_BLOCK = 2048
SEL_ROW_STEP = 32
ONES_ROWS = 16
VT_ROWS = HEAD_DIM + ONES_ROWS
ROW_TILE = 16
Q_SCALE = HEAD_DIM ** -0.5 * math.log2(math.e)
M_FLOOR = -1e20


def _dot(a, b):
    return jnp.dot(a, b, preferred_element_type=F32)


def _rms(x, g):
    return x * lax.rsqrt(jnp.mean(x * x, axis=-1, keepdims=True) + EPS) * g


def _split_bf16(x):
    hi = x.astype(BF16)
    lo = (x - hi.astype(F32)).astype(BF16)
    return hi, lo


def _const_spec(shape):
    nd = len(shape)
    return pl.BlockSpec(shape, lambda *_: (0,) * nd, pipeline_mode=pl.Buffered(1))


def _ada_kernel(c_ref, w_ref, b_ref, o_ref):
    c = c_ref[...]
    c_act = c * jax.nn.sigmoid(c)
    o_ref[...] = _dot(c_act.astype(BF16), w_ref[...].astype(BF16)) + b_ref[...]


def _ada_call(c_pad, w_ada, b_ada):
    rows, d = c_pad.shape
    n = w_ada.shape[1]
    tn = ADA_COL_TILE
    return pl.pallas_call(
        _ada_kernel,
        grid=(n // tn,),
        in_specs=[pl.BlockSpec((rows, d), lambda j: (0, 0)),
                  pl.BlockSpec((d, tn), lambda j: (0, j)),
                  pl.BlockSpec((1, tn), lambda j: (0, j))],
        out_specs=pl.BlockSpec((rows, tn), lambda j: (0, j)),
        out_shape=jax.ShapeDtypeStruct((rows, n), F32),
        compiler_params=pltpu.CompilerParams(dimension_semantics=("arbitrary",),
                                             vmem_limit_bytes=VMEM_LIMIT),
        name="adaln_mod",
    )(c_pad, w_ada, b_ada)


def _ffn_core(x, shift, scale, gate, g, wg_ref, wu_ref, wd_ref, a_ref):
    h = _rms(x, g) * (1.0 + scale) + shift
    hb = h.astype(BF16)
    d_ff = wg_ref.shape[1]
    for j in range(d_ff // FF_TILE):
        sl = slice(j * FF_TILE, (j + 1) * FF_TILE)
        gg = _dot(hb, wg_ref[:, sl])
        uu = _dot(hb, wu_ref[:, sl])
        a_ref[:, sl] = (gg * jax.nn.sigmoid(gg) * uu).astype(a_ref.dtype)
    return x + (MACARON_W * gate) * _dot(a_ref[...], wd_ref[...])


def _cast_once(src_refs, dst_refs):
    @pl.when((pl.program_id(0) == 0) & (pl.program_id(1) == 0))
    def _():
        for src, dst in zip(src_refs, dst_refs):
            for c in range(0, src.shape[1], FF_TILE):
                dst[:, c:c + FF_TILE] = src[:, c:c + FF_TILE].astype(dst.dtype)


def _ffn_kernel(x_ref, mod_ref, g_ref, wg_ref, wu_ref, wd_ref, o_ref, a_ref, wgb_ref, wub_ref):
    _cast_once((wg_ref, wu_ref), (wgb_ref, wub_ref))
    o_ref[...] = _ffn_core(x_ref[...], mod_ref[0:1, :], mod_ref[1:2, :], mod_ref[2:3, :],
                           g_ref[...], wgb_ref, wub_ref, wd_ref, a_ref)


def _ffn_call(x, mod, g, wg, wu, wd):
    B, S, D = x.shape
    tm = FFN_TOKEN_TILE
    return pl.pallas_call(
        _ffn_kernel,
        grid=(B, S // tm),
        in_specs=[pl.BlockSpec((None, tm, D), lambda b, i: (b, i, 0)),
                  pl.BlockSpec((None, N_MOD, D), lambda b, i: (b, 0, 0)),
                  _const_spec(g.shape), _const_spec(wg.shape), _const_spec(wu.shape),
                  _const_spec(wd.shape)],
        out_specs=pl.BlockSpec((None, tm, D), lambda b, i: (b, i, 0)),
        out_shape=jax.ShapeDtypeStruct((B, S, D), F32),
        scratch_shapes=[pltpu.VMEM((tm, wg.shape[1]), BF16), pltpu.VMEM(wg.shape, BF16),
                        pltpu.VMEM(wu.shape, BF16)],
        compiler_params=pltpu.CompilerParams(dimension_semantics=("arbitrary", "arbitrary"),
                                             vmem_limit_bytes=VMEM_LIMIT),
        name="ffn1",
    )(x, mod, g, wg, wu, wd)


def _rope_table_kernel(pos_ref, freq_ref, cos_ref, sin_ref):
    pos = jnp.concatenate([pos_ref[...].astype(F32),
                           jnp.zeros((pos_ref.shape[0], LANES - TOKENS_PER_ROW), F32)], axis=1)
    src = lax.broadcasted_iota(jnp.int32, pos.shape, 1) // ROT_HALF
    ang = jnp.take_along_axis(pos, src, axis=1) * freq_ref[...]
    cos_ref[...] = jnp.cos(ang)
    sin_ref[...] = jnp.sin(ang)


def _rope_table_call(pos_rows, freq):
    rows = pos_rows.shape[0]
    shape = jax.ShapeDtypeStruct((rows, LANES), F32)
    full = pl.BlockSpec((rows, LANES), lambda: (0, 0))
    return pl.pallas_call(
        _rope_table_kernel,
        in_specs=[pl.BlockSpec(pos_rows.shape, lambda: (0, 0)), pl.BlockSpec(freq.shape, lambda: (0, 0))],
        out_specs=(full, full),
        out_shape=(shape, shape),
        compiler_params=pltpu.CompilerParams(vmem_limit_bytes=VMEM_LIMIT),
        name="rope_table",
    )(pos_rows, freq)


def _inproj_kernel(x_ref, cosp_ref, sinp_ref, mod_ref, gmix_ref, win_ref, wgate_ref, convw_ref, gconv_ref,
                   gmat_ref, onehot_ref, yc_ref, qT_ref, kc_ref, vc_ref, ksa_ref, vsT_ref, kw_ref,
                   vwT_ref, gT_ref, carry_ref):
    tm = x_ref.shape[0]

    @pl.when(pl.program_id(1) == 0)
    def _():
        carry_ref[...] = jnp.zeros_like(carry_ref)

    x = x_ref[...]
    h = _rms(x, gmix_ref[...]) * (1.0 + mod_ref[4:5, :]) + mod_ref[3:4, :]
    hb = h.astype(BF16)

    def proj(c0, width):
        return _dot(hb, win_ref[:, c0:c0 + width])

    d = lax.broadcasted_iota(jnp.int32, (tm, LANES), 1) & (HEAD_DIM - 1)
    token = lax.broadcasted_iota(jnp.int32, (tm, LANES), 0) & (TOKENS_PER_ROW - 1)
    src_lane = token * ROT_HALF + (d & (ROT_HALF - 1))

    def unpack(packed_ref):
        rows = jnp.broadcast_to(packed_ref[...][:, None, :], (tm // TOKENS_PER_ROW, TOKENS_PER_ROW, LANES))
        return jnp.take_along_axis(rows.reshape(tm, LANES), src_lane, axis=1)

    cos_t = jnp.where(d < ROT_DIM, unpack(cosp_ref), 1.0)
    sin_raw = unpack(sinp_ref)
    sin_t = jnp.where(d < ROT_HALF, -sin_raw, jnp.where(d < ROT_DIM, sin_raw, 0.0))
    first_half = d < ROT_HALF

    def rope(t):
        outs = []
        for j in range(t.shape[1] // LANES):
            tj = t[:, j * LANES:(j + 1) * LANES]
            partner = jnp.where(first_half, pltpu.roll(tj, LANES - ROT_HALF, 1),
                                pltpu.roll(tj, ROT_HALF, 1))
            outs.append(tj * cos_t + partner * sin_t)
        return outs[0] if len(outs) == 1 else jnp.concatenate(outs, axis=1)

    c0 = 3 * CONV_CH
    q = rope(proj(c0, ATTN_WIDTH)) * Q_SCALE
    qT_ref[...] = q.T.astype(qT_ref.dtype)
    c0 += ATTN_WIDTH
    kv = proj(c0, 2 * KV_WIDTH)
    kc_ref[...] = rope(kv[:, :KV_WIDTH])
    vc_ref[...] = kv[:, KV_WIDTH:]
    kv = proj(c0 + 2 * KV_WIDTH, 2 * KV_WIDTH)
    ks = rope(kv[:, :KV_WIDTH]).astype(BF16)
    vsT = kv[:, KV_WIDTH:].T.astype(BF16)
    kv = proj(c0 + 4 * KV_WIDTH, 2 * KV_WIDTH)
    kw = rope(kv[:, :KV_WIDTH]).astype(BF16)
    vwT = kv[:, KV_WIDTH:].T.astype(BF16)
    pad = jnp.zeros((tm, MXU_DEPTH - LANES - HEAD_DIM), BF16)
    ones = jnp.ones((ONES_ROWS, tm), BF16)
    for g in range(N_KV_GROUPS):
        kg = ks[:, g * HEAD_DIM:(g + 1) * HEAD_DIM]
        ksa_ref[g] = jnp.concatenate([onehot_ref[...], kg, pad], axis=1)
        kw_ref[g] = kw[:, g * HEAD_DIM:(g + 1) * HEAD_DIM]
        vsT_ref[g] = jnp.concatenate([vsT[g * HEAD_DIM:(g + 1) * HEAD_DIM, :], ones], axis=0)
        vwT_ref[g] = jnp.concatenate([vwT[g * HEAD_DIM:(g + 1) * HEAD_DIM, :], ones], axis=0)
    gT_ref[...] = jax.nn.sigmoid(_dot(hb, wgate_ref[...])).T

    cb = proj(0, CONV_CH)
    u = proj(CONV_CH, CONV_CH) * proj(2 * CONV_CH, CONV_CH)
    row = lax.broadcasted_iota(jnp.int32, (tm, 1), 0)
    prev1 = carry_ref[7:8, :]
    prev2 = carry_ref[6:7, :]
    u1 = jnp.where(row >= 1, pltpu.roll(u, 1, 0), prev1)
    u2 = jnp.where(row >= 2, pltpu.roll(u, 2, 0), jnp.where(row == 1, prev1, prev2))
    carry_ref[...] = u[tm - 8:tm, :]
    v = convw_ref[0:1, :] * u2 + convw_ref[1:2, :] * u1 + convw_ref[2:3, :] * u
    y = cb * v
    ms = _dot((y * y).astype(BF16), gmat_ref[...])
    yc_ref[...] = (y * lax.rsqrt(ms + EPS) * gconv_ref[...]).astype(yc_ref.dtype)


def _inproj_call(x, cos_p, sin_p, mod, gmix, win, wgate, convw, gconv, gmat, onehot):
    B, S, D = x.shape
    tm = TOKEN_TILE
    tok = lambda w: pl.BlockSpec((None, tm, w), lambda b, i: (b, i, 0))
    tr = lambda w: pl.BlockSpec((None, w, tm), lambda b, i: (b, 0, i))
    grp = lambda w: pl.BlockSpec((None, N_KV_GROUPS, tm, w), lambda b, i: (b, 0, i, 0))
    grpT = pl.BlockSpec((None, N_KV_GROUPS, VT_ROWS, tm), lambda b, i: (b, 0, 0, i))
    packed = pl.BlockSpec((None, tm // TOKENS_PER_ROW, LANES), lambda b, i: (b, i, 0))
    out_shapes = (
        jax.ShapeDtypeStruct((B, S, CONV_CH), BF16),
        jax.ShapeDtypeStruct((B, ATTN_WIDTH, S), BF16),
        jax.ShapeDtypeStruct((B, S, KV_WIDTH), F32),
        jax.ShapeDtypeStruct((B, S, KV_WIDTH), F32),
        jax.ShapeDtypeStruct((B, N_KV_GROUPS, S, MXU_DEPTH), BF16),
        jax.ShapeDtypeStruct((B, N_KV_GROUPS, VT_ROWS, S), BF16),
        jax.ShapeDtypeStruct((B, N_KV_GROUPS, S, HEAD_DIM), BF16),
        jax.ShapeDtypeStruct((B, N_KV_GROUPS, VT_ROWS, S), BF16),
        jax.ShapeDtypeStruct((B, N_KV_GROUPS * GATE_PAD, S), F32),
    )
    return pl.pallas_call(
        _inproj_kernel,
        grid=(B, S // tm),
        in_specs=[tok(D),
                  packed, packed,
                  pl.BlockSpec((None, N_MOD, D), lambda b, i: (b, 0, 0)),
                  _const_spec(gmix.shape), _const_spec(win.shape), _const_spec(wgate.shape),
                  _const_spec(convw.shape),
                  _const_spec(gconv.shape), _const_spec(gmat.shape),
                  pl.BlockSpec((tm, onehot.shape[1]), lambda b, i: (i, 0))],
        out_specs=(tok(CONV_CH), tr(ATTN_WIDTH), tok(KV_WIDTH), tok(KV_WIDTH), grp(MXU_DEPTH),
                   grpT, grp(HEAD_DIM), grpT, tr(N_KV_GROUPS * GATE_PAD)),
        out_shape=out_shapes,
        scratch_shapes=[pltpu.VMEM((8, CONV_CH), F32)],
        compiler_params=pltpu.CompilerParams(dimension_semantics=("arbitrary", "arbitrary"),
                                             vmem_limit_bytes=VMEM_LIMIT),
        name="mixer_inproj",
    )(x, cos_p, sin_p, mod, gmix, win, wgate, convw, gconv, gmat, onehot)


def _compress_kernel(kf_ref, vf_ref, pek_ref, pev_ref, wk1_ref, wk2_ref, wv1_ref, wv2_ref,
                     kcc_ref, vcT_ref):
    half = CMP_BLOCK // 2
    n = kf_ref.shape[0] // half

    def mlp(x_ref, pe_ref, w1_ref, w2_ref):
        parts = []
        for p in range(2):
            acc = None
            for l0 in range(0, half, 2):
                xs, ws = [], []
                for l in (l0, l0 + 1):
                    row = p * half + l
                    xs.append((x_ref[pl.ds(l, n, stride=half), :]
                               + pe_ref[row:row + 1, :]).astype(BF16))
                    w = w1_ref[row * HEAD_DIM:(row + 1) * HEAD_DIM, :].astype(BF16)
                    z = jnp.zeros_like(w)
                    ws.append(jnp.concatenate([jnp.concatenate([w, z], axis=1),
                                               jnp.concatenate([z, w], axis=1)], axis=0))
                d = _dot(jnp.concatenate(xs, axis=1), jnp.concatenate(ws, axis=0))
                acc = d if acc is None else acc + d
            parts.append(acc)
        hpre = parts[0] + pltpu.roll(parts[1], n - 1, 0)
        hid = (hpre * jax.nn.sigmoid(hpre)).astype(BF16)
        w2 = w2_ref[...].astype(BF16)
        return jnp.concatenate([_dot(hid[:, g * CMP_HIDDEN:(g + 1) * CMP_HIDDEN], w2)
                                for g in range(N_KV_GROUPS)], axis=1)

    kc = mlp(kf_ref, pek_ref, wk1_ref, wk2_ref).astype(kcc_ref.dtype)
    for g in range(N_KV_GROUPS):
        kcc_ref[g] = kc[:, g * HEAD_DIM:(g + 1) * HEAD_DIM]
    vcT = mlp(vf_ref, pev_ref, wv1_ref, wv2_ref).T.astype(vcT_ref.dtype)
    ones = jnp.ones((ONES_ROWS, n), vcT_ref.dtype)
    for g in range(N_KV_GROUPS):
        vcT_ref[g] = jnp.concatenate([vcT[g * HEAD_DIM:(g + 1) * HEAD_DIM, :], ones], axis=0)


def _compress_call(kf, vf, pek, pev, wk1, wk2, wv1, wv2):
    B, S, width = kf.shape
    n = S // (CMP_BLOCK // 2)
    flat = pl.BlockSpec((None, S, width), lambda b: (b, 0, 0))
    return pl.pallas_call(
        _compress_kernel,
        grid=(B,),
        in_specs=[flat, flat, _const_spec(pek.shape), _const_spec(pev.shape),
                  _const_spec(wk1.shape), _const_spec(wk2.shape), _const_spec(wv1.shape),
                  _const_spec(wv2.shape)],
        out_specs=(pl.BlockSpec((None, N_KV_GROUPS, n, HEAD_DIM), lambda b: (b, 0, 0, 0)),
                   pl.BlockSpec((None, N_KV_GROUPS, VT_ROWS, n), lambda b: (b, 0, 0, 0))),
        out_shape=(jax.ShapeDtypeStruct((B, N_KV_GROUPS, n, HEAD_DIM), BF16),
                   jax.ShapeDtypeStruct((B, N_KV_GROUPS, VT_ROWS, n), BF16)),
        compiler_params=pltpu.CompilerParams(dimension_semantics=("arbitrary",),
                                             vmem_limit_bytes=VMEM_LIMIT),
        name="kv_compress",
    )(kf, vf, pek, pev, wk1, wk2, wv1, wv2)


def _lane_tiles(x, n):
    return jnp.concatenate([x] * n, axis=1)


def _col_max(s_ref, n_rows, bias_fn, lane_tile=None):
    width = s_ref.shape[1]
    lt = lane_tile or width
    groups = ROW_TILE // 8
    mx = {c: [jnp.full((8, lt), NEG, F32)] * groups for c in range(0, width, lt)}
    for r in range(0, n_rows, ROW_TILE):
        bias = None if bias_fn is None else bias_fn(r)
        for c in range(0, width, lt):
            x = s_ref[r:r + ROW_TILE, c:c + lt]
            if bias is not None:
                x = x + bias
                s_ref[r:r + ROW_TILE, c:c + lt] = x
            mx[c] = [jnp.maximum(mx[c][i], x[8 * i:8 * (i + 1), :]) for i in range(groups)]
    out = []
    for c in range(0, width, lt):
        m = mx[c]
        while len(m) > 1:
            m = [jnp.maximum(a, b) for a, b in zip(m[0::2], m[1::2])]
        out.append(jnp.max(m[0], axis=0, keepdims=True))
    return out[0] if len(out) == 1 else jnp.concatenate(out, axis=1)


def _col_exp2(s_ref, p_ref, n_rows, m, keep_f32=False, lane_tile=None):
    width = s_ref.shape[1]
    lt = lane_tile or width
    for r in range(0, n_rows, ROW_TILE):
        for c in range(0, width, lt):
            p = jnp.exp2(s_ref[r:r + ROW_TILE, c:c + lt] - m[:, c:c + lt])
            if keep_f32:
                s_ref[r:r + ROW_TILE, c:c + lt] = p
            p_ref[r:r + ROW_TILE, c:c + lt] = p.astype(p_ref.dtype)


def _recip_pos(l):
    return 1.0 / jnp.where(l > 0.0, l, 1.0)


def _select_kernel(top_n, qT_ref, kcc_ref, vcT_ref, ovlT_ref, sel_ref, ocmp_ref, sc_ref, pc_ref,
                   ph_ref, pl_ref):
    QB = SEL_BLOCK
    n_cmp = kcc_ref.shape[0]
    n_slc = ovlT_ref.shape[0]
    t0 = pl.program_id(2) * QB
    tq = t0 + lax.broadcasted_iota(jnp.int32, (1, QB), 1)
    rows = lax.broadcasted_iota(jnp.int32, (ROW_TILE, 1), 0)

    qT = qT_ref[...]
    qcat = jnp.concatenate([qT[h * HEAD_DIM:(h + 1) * HEAD_DIM, :] for h in range(HPG)], axis=1)

    def cmp_bias(r):
        cmp_end = (r + rows) * CMP_STRIDE + (CMP_BLOCK - 1)
        return jnp.where(cmp_end <= tq, 0.0, NEG)

    cur = tq // SLC_BLOCK

    def tree(op, xs):
        while len(xs) > 1:
            xs = [op(*xs[i:i + 2]) if i + 1 < len(xs) else xs[i] for i in range(0, len(xs), 2)]
        return xs[0]

    def causal_variant(n_rows):
        rows_cmp = min(n_cmp, n_rows * (SLC_BLOCK // CMP_STRIDE))

        def run():
            sc_ref[0:rows_cmp, :] = _dot(kcc_ref[0:rows_cmp, :], qcat)
            m = jnp.maximum(_col_max(sc_ref, rows_cmp, cmp_bias, lane_tile=QB), M_FLOOR)
            _col_exp2(sc_ref, pc_ref, rows_cmp, m, keep_f32=True, lane_tile=QB)
            o_cmp = _dot(vcT_ref[:, 0:rows_cmp], pc_ref[0:rows_cmp, :])
            rl = _recip_pos(o_cmp[HEAD_DIM:HEAD_DIM + 1, :])
            o_cmp = o_cmp[0:HEAD_DIM, :] * rl
            for h in range(HPG):
                ocmp_ref[h] = o_cmp[:, h * QB:(h + 1) * QB]

            for r in range(0, rows_cmp, ROW_TILE):
                psum = sc_ref[r:r + ROW_TILE, 0:QB] * rl[:, 0:QB]
                for h in range(1, HPG):
                    sl = slice(h * QB, (h + 1) * QB)
                    psum = psum + sc_ref[r:r + ROW_TILE, sl] * rl[:, sl]
                hi, lo = _split_bf16(psum)
                ph_ref[r:r + ROW_TILE, :] = hi
                pl_ref[r:r + ROW_TILE, :] = lo
            ovl = ovlT_ref[0:n_rows, 0:rows_cmp]
            imp = _dot(ovl, ph_ref[0:rows_cmp, :]) + _dot(ovl, pl_ref[0:rows_cmp, :])

            blk = lax.broadcasted_iota(jnp.int32, (n_rows, 1), 0)
            future = blk > cur
            forced = (blk == 0) | (blk == cur) | (blk == cur - 1)
            groups = range(n_rows // 8)

            def store(picked):
                sel_ref[0:n_rows, :] = jnp.where(future, NEG, jnp.where(picked, 0.0, NEG)).astype(sel_ref.dtype)

            def with_ties():
                score0 = jnp.where(future, -BIG, jnp.where(forced, BIG, imp))
                score = [score0[8 * i:8 * (i + 1), :] for i in groups]
                cum = jnp.zeros((1, QB), F32)
                thr = jnp.zeros((1, QB), F32)
                above = jnp.zeros((1, QB), F32)
                for _ in range(top_n):
                    best = jnp.max(tree(jnp.maximum, score), axis=0, keepdims=True)
                    eq = [s == best for s in score]
                    unfilled = cum < top_n
                    thr = jnp.where(unfilled, best, thr)
                    above = jnp.where(unfilled, cum, above)
                    cum = cum + jnp.sum(tree(jnp.add, [jnp.where(e, 1.0, 0.0) for e in eq]),
                                        axis=0, keepdims=True)
                    score = [jnp.where(e, -jnp.inf, s) for e, s in zip(eq, score)]
                ties = score0 == thr
                lower = jnp.where(lax.broadcasted_iota(jnp.int32, (1, n_rows), 1) < blk, 1.0, 0.0)
                rank = _dot(lower.astype(BF16), jnp.where(ties, 1.0, 0.0).astype(BF16))
                store((score0 > thr) | (ties & (rank < top_n - above)))

            if top_n > MAX_FORCED:
                n_forced = 1.0 + jnp.where(cur >= 1, 1.0, 0.0) + jnp.where(cur >= 2, 1.0, 0.0)
                wanted = top_n - n_forced
                take_all = (cur.astype(F32) + 1.0 - n_forced) <= wanted
                cand = jnp.where(future | forced, -BIG, imp)
                score = [cand[8 * i:8 * (i + 1), :] for i in groups]
                for _ in range(top_n - MAX_FORCED):
                    cut = jnp.max(tree(jnp.maximum, score), axis=0, keepdims=True)
                    score = [jnp.where(s == cut, -jnp.inf, s) for s in score]
                above_cut = cand >= cut
                count = jnp.sum(jnp.where(above_cut, 1.0, 0.0), axis=0, keepdims=True)
                exact = jnp.min(jnp.where(take_all | (count == wanted), 1.0, 0.0)) > 0.5
                lax.cond(exact, lambda: store(forced | take_all | above_cut), with_ties)
            else:
                with_ties()
            if n_rows < n_slc:
                sel_ref[n_rows:n_slc, :] = jnp.full((n_slc - n_rows, QB), NEG, sel_ref.dtype)
        return run

    steps = n_slc // SEL_ROW_STEP
    need = jnp.minimum((t0 + QB - 1) // (SLC_BLOCK * SEL_ROW_STEP), steps - 1)
    lax.switch(need, [causal_variant(SEL_ROW_STEP * (k + 1)) for k in range(steps)])


def _select_call(top_n, qT, kcc, vcT, ovlT):
    B, _, S = qT.shape
    n_cmp = kcc.shape[2]
    n_slc = ovlT.shape[0]
    gw = HPG * HEAD_DIM
    qw = HPG * SEL_BLOCK
    return pl.pallas_call(
        functools.partial(_select_kernel, top_n),
        grid=(B, N_KV_GROUPS, S // SEL_BLOCK),
        in_specs=[pl.BlockSpec((None, gw, SEL_BLOCK), lambda b, g, i: (b, g, i)),
                  pl.BlockSpec((None, None, n_cmp, HEAD_DIM), lambda b, g, i: (b, g, 0, 0)),
                  pl.BlockSpec((None, None, VT_ROWS, n_cmp), lambda b, g, i: (b, g, 0, 0)),
                  pl.BlockSpec(ovlT.shape, lambda b, g, i: (0, 0))],
        out_specs=(pl.BlockSpec((None, None, n_slc, SEL_BLOCK), lambda b, g, i: (b, g, 0, i)),
                   pl.BlockSpec((None, None, HPG, HEAD_DIM, SEL_BLOCK),
                                lambda b, g, i: (b, g, 0, 0, i))),
        out_shape=(jax.ShapeDtypeStruct((B, N_KV_GROUPS, n_slc, S), BF16),
                   jax.ShapeDtypeStruct((B, N_KV_GROUPS, HPG, HEAD_DIM, S), F32)),
        scratch_shapes=[pltpu.VMEM((n_cmp, qw), F32), pltpu.VMEM((n_cmp, qw), BF16),
                        pltpu.VMEM((n_cmp, SEL_BLOCK), BF16), pltpu.VMEM((n_cmp, SEL_BLOCK), BF16)],
        compiler_params=pltpu.CompilerParams(
            dimension_semantics=("parallel", "parallel", "arbitrary"),
            vmem_limit_bytes=VMEM_LIMIT),
        name="nsa_select",
    )(qT, kcc, vcT, ovlT)


def _attn_kernel(qT_ref, gT_ref, sel_ref, ocmp_ref, ksa_ref, vsT_ref, kw_ref, vwT_ref, ga_ref,
                 o_ref, s0_ref, s1_ref, p0_ref, p1_ref, sw_ref, pw_ref, rhs_ref, acc_ref):
    QB = Q_BLOCK
    n_slc = sel_ref.shape[0]
    t0 = pl.program_id(2) * QB
    tq = t0 + lax.broadcasted_iota(jnp.int32, (1, QB), 1)
    rows = lax.broadcasted_iota(jnp.int32, (ROW_TILE, 1), 0)

    qT = qT_ref[...]
    qcat = jnp.concatenate([qT[h * HEAD_DIM:(h + 1) * HEAD_DIM, :] for h in range(HPG)], axis=1)

    rhs_ref[...] = jnp.concatenate([_lane_tiles(sel_ref[...], HPG), qcat,
                                    jnp.zeros((MXU_DEPTH - n_slc - HEAD_DIM, QW), BF16)], axis=0)

    w0 = pl.multiple_of(jnp.maximum(t0 - WINDOW, 0), QB)
    sw_ref[...] = _dot(kw_ref[pl.ds(w0, WIN_KEYS), :], qcat)

    def win_bias(r):
        dist = tq - (w0 + r + rows)
        return _lane_tiles(jnp.where((dist >= 0) & (dist < WINDOW), 0.0, NEG), HPG)

    KC = KEY_CHUNK
    last = t0 // KC

    def scores(c, dst_ref):
        k0 = pl.multiple_of(jnp.minimum(c, last) * KC, KC)
        dst_ref[...] = _dot(ksa_ref[pl.ds(k0, KC), :], rhs_ref[...])

    def weighted_values(c, p_ref):
        k0 = pl.multiple_of(jnp.clip(c, 0, last) * KC, KC)
        return _dot(vsT_ref[:, pl.ds(k0, KC)], p_ref[...])

    def softmax(c, src_ref, p_ref, m, pending, causal):
        def causal_bias(r):
            return _lane_tiles(jnp.where(c * KC + r + rows <= tq, 0.0, NEG), HPG)

        m_new = jnp.maximum(m, _col_max(src_ref, KC, causal_bias if causal else None))
        acc = acc_ref[...] if pending is None else acc_ref[...] + pending
        acc_ref[...] = jnp.exp2(m - m_new) * acc
        _col_exp2(src_ref, p_ref, KC, m_new)
        return m_new

    def pair(i, m, final=False, first=False):
        pending = None if first else weighted_values(2 * i - 1, p1_ref)
        scores(2 * i + 1, s1_ref)
        m = softmax(2 * i, s0_ref, p0_ref, m, pending, False)
        pending = weighted_values(2 * i, p0_ref)
        if not final:
            scores(2 * i + 2, s0_ref)
        return softmax(2 * i + 1, s1_ref, p1_ref, m, pending, final)

    p1_ref[...] = jnp.zeros_like(p1_ref)
    acc_ref[...] = jnp.zeros_like(acc_ref)
    scores(0, s0_ref)

    m = _col_max(sw_ref, WIN_KEYS, win_bias)
    _col_exp2(sw_ref, pw_ref, WIN_KEYS, m)
    o_win = _dot(vwT_ref[:, pl.ds(w0, WIN_KEYS)], pw_ref[...])
    o_win = o_win[0:HEAD_DIM, :] * (1.0 / o_win[HEAD_DIM:HEAD_DIM + 1, :])

    full_pairs = last // 2
    m = lax.cond(full_pairs >= 1, lambda mm: pair(0, mm, first=True), lambda mm: mm,
                 jnp.full((1, QW), NEG, F32))
    rest = jnp.maximum(full_pairs - 1, 0)
    m = lax.fori_loop(0, rest // 2, lambda j, mm: pair(2 * j + 2, pair(2 * j + 1, mm)), m)
    m = lax.cond(rest % 2 == 1, lambda mm: pair(full_pairs - 1, mm), lambda mm: mm, m)

    @pl.when(last % 2 == 0)
    def _():
        pending = weighted_values(last - 1, p1_ref)
        softmax(last, s0_ref, p0_ref, m, pending, True)
        acc_ref[...] += weighted_values(last, p0_ref)

    @pl.when(last % 2 == 1)
    def _():
        pair(full_pairs, m, final=True)
        acc_ref[...] += weighted_values(last, p1_ref)

    acc = acc_ref[...]
    o_slc = acc[0:HEAD_DIM, :] * (1.0 / acc[HEAD_DIM:HEAD_DIM + 1, :])

    gts = gT_ref[...]
    outs = []
    for h in range(HPG):
        sl = slice(h * QB, (h + 1) * QB)
        o = (gts[3 * h:3 * h + 1, :] * ocmp_ref[h] + gts[3 * h + 1:3 * h + 2, :] * o_slc[:, sl]
             + gts[3 * h + 2:3 * h + 3, :] * o_win[:, sl])
        o = o * lax.rsqrt(jnp.mean(o * o, axis=0, keepdims=True) + EPS)
        outs.append((o * ga_ref[h * HEAD_DIM:(h + 1) * HEAD_DIM, :]).T)
    o_ref[...] = jnp.concatenate(outs, axis=1).astype(o_ref.dtype)


def _attn_call(qT, gatesT, sel_bias, o_cmp, ksa, vsT, kw, vwT, g_attn_col):
    B, _, S = qT.shape
    n_slc = sel_bias.shape[2]
    gw = HPG * HEAD_DIM
    rows = lambda n, w: pl.BlockSpec((None, None, n, w), lambda b, g, i: (b, g, 0, 0))
    cols = lambda n: pl.BlockSpec((None, None, VT_ROWS, n), lambda b, g, i: (b, g, 0, 0))
    return pl.pallas_call(
        _attn_kernel,
        grid=(B, N_KV_GROUPS, S // Q_BLOCK),
        in_specs=[pl.BlockSpec((None, gw, Q_BLOCK), lambda b, g, i: (b, g, i)),
                  pl.BlockSpec((None, GATE_PAD, Q_BLOCK), lambda b, g, i: (b, g, i)),
                  pl.BlockSpec((None, None, n_slc, Q_BLOCK), lambda b, g, i: (b, g, 0, i)),
                  pl.BlockSpec((None, None, HPG, HEAD_DIM, Q_BLOCK), lambda b, g, i: (b, g, 0, 0, i)),
                  rows(S, MXU_DEPTH), cols(S), rows(S, HEAD_DIM), cols(S),
                  pl.BlockSpec((gw, 1), lambda b, g, i: (g, 0))],
        out_specs=pl.BlockSpec((None, Q_BLOCK, gw), lambda b, g, i: (b, i, g)),
        out_shape=jax.ShapeDtypeStruct((B, S, ATTN_WIDTH), BF16),
        scratch_shapes=[pltpu.VMEM((KEY_CHUNK, QW), F32), pltpu.VMEM((KEY_CHUNK, QW), F32),
                        pltpu.VMEM((KEY_CHUNK, QW), BF16), pltpu.VMEM((KEY_CHUNK, QW), BF16),
                        pltpu.VMEM((WIN_KEYS, QW), F32), pltpu.VMEM((WIN_KEYS, QW), BF16),
                        pltpu.VMEM((MXU_DEPTH, QW), BF16), pltpu.VMEM((VT_ROWS, QW), F32)],
        compiler_params=pltpu.CompilerParams(
            dimension_semantics=("parallel", "parallel", "arbitrary"),
            vmem_limit_bytes=VMEM_LIMIT),
        name="nsa_attention",
    )(qT, gatesT, sel_bias, o_cmp, ksa, vsT, kw, vwT, g_attn_col)


def _out_kernel(x_ref, yc_ref, ya_ref, mod_ref, wo_ref, g_ref, wg_ref, wu_ref, wd_ref, gf_ref,
                o_ref, a_ref, wgb_ref, wub_ref):
    _cast_once((wg_ref, wu_ref), (wgb_ref, wub_ref))
    mix = _dot(yc_ref[...], wo_ref[0:CONV_CH, :]) + _dot(ya_ref[...], wo_ref[CONV_CH:, :])
    x = x_ref[...] + mod_ref[5:6, :] * mix
    x = _ffn_core(x, mod_ref[6:7, :], mod_ref[7:8, :], mod_ref[8:9, :], g_ref[...], wgb_ref,
                  wub_ref, wd_ref, a_ref)
    o_ref[...] = _rms(x, gf_ref[...])


def _out_call(x, yc, ya, mod, wo, g, wg, wu, wd, gf):
    B, S, D = x.shape
    tm = FFN_TOKEN_TILE
    tok = lambda w: pl.BlockSpec((None, tm, w), lambda b, i: (b, i, 0))
    return pl.pallas_call(
        _out_kernel,
        grid=(B, S // tm),
        in_specs=[tok(D), tok(CONV_CH), tok(ATTN_WIDTH),
                  pl.BlockSpec((None, N_MOD, D), lambda b, i: (b, 0, 0)),
                  _const_spec(wo.shape), _const_spec(g.shape), _const_spec(wg.shape),
                  _const_spec(wu.shape), _const_spec(wd.shape), _const_spec(gf.shape)],
        out_specs=tok(D),
        out_shape=jax.ShapeDtypeStruct((B, S, D), F32),
        scratch_shapes=[pltpu.VMEM((tm, wg.shape[1]), BF16), pltpu.VMEM(wg.shape, BF16),
                        pltpu.VMEM(wu.shape, BF16)],
        compiler_params=pltpu.CompilerParams(dimension_semantics=("arbitrary", "arbitrary"),
                                             vmem_limit_bytes=VMEM_LIMIT),
        name="outproj_ffn2",
    )(x, yc, ya, mod, wo, g, wg, wu, wd, gf)


def kernel(x, c, positions, w_ada, b_ada, g_ffn1, w1_gate, w1_up, w1_down, g_mix, w_in, conv_w, cmp_pos_k, cmp_pos_v, w_cmpk1, w_cmpk2, w_cmpv1, w_cmpv2, g_out_conv, g_out_attn, w_out, g_ffn2, w2_gate, w2_up, w2_down, g_final):
    B, S, D = x.shape
    depth = w_ada.shape[0]
    n_slc = S // SLC_BLOCK
    half = CMP_BLOCK // 2
    n_half = S // half
    assert n_slc <= LANES, "selection-block one-hot is one lane tile wide"

    c_pad = jnp.pad(c, ((0, 8 - B), (0, 0)))
    row = lambda a: a.reshape(1, -1)

    freq_half = jnp.power(ROPE_THETA, -2.0 * jnp.arange(ROT_HALF, dtype=F32) / ROT_DIM)
    freq = jnp.tile(freq_half, LANES // ROT_HALF).reshape(1, LANES)
    gidx = np.arange(CONV_CH) // (CONV_CH // CONV_GROUPS)
    gmat = jnp.asarray((gidx[:, None] == gidx[None, :]) / (CONV_CH // CONV_GROUPS), dtype=BF16)
    c0 = np.arange(n_half) * CMP_STRIDE
    s0 = np.arange(LANES) * SLC_BLOCK
    ovlT = ((c0[None, :] <= s0[:, None] + SLC_BLOCK - 1) & (c0[None, :] + CMP_BLOCK - 1 >= s0[:, None]))
    ovlT = jnp.asarray(ovlT, dtype=BF16)
    onehot = jnp.asarray((np.arange(S) // SLC_BLOCK)[:, None] == np.arange(LANES)[None, :], dtype=BF16)
    cos_p, sin_p = _rope_table_call(positions.reshape(-1, TOKENS_PER_ROW), freq)
    cos_p = cos_p.reshape(B, S // TOKENS_PER_ROW, LANES)
    sin_p = sin_p.reshape(B, S // TOKENS_PER_ROW, LANES)

    for l in range(depth):
        mod = _ada_call(c_pad, w_ada[l], row(b_ada[l]))[:B].reshape(B, N_MOD, D)

        x = _ffn_call(x, mod, row(g_ffn1[l]), w1_gate[l], w1_up[l], w1_down[l].astype(BF16))

        n_main = w_in.shape[2] - N_KV_GROUPS * 3 * HPG
        gate_cols = [jnp.pad(w_in[l][:, n_main + g * 3 * HPG:n_main + (g + 1) * 3 * HPG],
                             ((0, 0), (0, GATE_PAD - 3 * HPG))) for g in range(N_KV_GROUPS)]
        wgate = jnp.concatenate(gate_cols, axis=1).astype(BF16)
        win = w_in[l].astype(BF16)
        (yc, qT, kc, vc, ksa, vsT, kw, vwT, gatesT) = _inproj_call(
            x, cos_p, sin_p, mod, row(g_mix[l]), win, wgate, conv_w[l], row(g_out_conv[l]), gmat,
            onehot)

        both_groups = lambda pe: jnp.tile(pe, (1, N_KV_GROUPS))
        kcc, vcT = _compress_call(kc, vc, both_groups(cmp_pos_k[l]), both_groups(cmp_pos_v[l]),
                                  w_cmpk1[l], w_cmpk2[l], w_cmpv1[l], w_cmpv2[l])

        sel_bias, o_cmp = _select_call(min(SLC_TOP_N, n_slc), qT, kcc, vcT, ovlT)
        ya = _attn_call(qT, gatesT, sel_bias, o_cmp, ksa, vsT, kw, vwT,
                        g_out_attn[l].reshape(ATTN_WIDTH, 1))

        assert l == depth - 1, "final norm is fused into the last layer's output kernel"
        x = _out_call(x, yc, ya, mod, w_out[l].astype(BF16), row(g_ffn2[l]),
                      w2_gate[l], w2_up[l], w2_down[l].astype(BF16),
                      row(g_final))
    return x
```

```python
import functools
import math

import numpy as np
import jax
import jax.numpy as jnp
from jax import lax
from jax.experimental import pallas as pl
from jax.experimental.pallas import tpu as pltpu

F32 = jnp.float32
BF16 = jnp.bfloat16

CONV_CH = 512
CONV_GROUPS = 8
N_HEADS = 8
N_KV_GROUPS = 2
HPG = N_HEADS // N_KV_GROUPS
HEAD_DIM = 64
ATTN_WIDTH = N_HEADS * HEAD_DIM
KV_WIDTH = N_KV_GROUPS * HEAD_DIM
ROPE_THETA = 500000.0
ROT_DIM = HEAD_DIM // 4
ROT_HALF = ROT_DIM // 2
CMP_BLOCK = 32
CMP_STRIDE = 16
CMP_HIDDEN = 256
SLC_BLOCK = 64
SLC_TOP_N = 16
MAX_FORCED = 3
WINDOW = 512
Q_BLOCK = 256
MACARON_W = 0.5
N_MOD = 9
EPS = 1e-6
NEG = -1e30
BIG = 1e9

LANES = 128
MXU_DEPTH = 256
VMEM_LIMIT = 58 * 1024 * 1024

TOKEN_TILE = 512
FFN_TOKEN_TILE = 512
ADA_COL_TILE = 1024
FF_TILE = 256
KEY_CHUNK = 512
WIN_KEYS = WINDOW + Q_BLOCK
TOKENS_PER_ROW = LANES // ROT_HALF
GATE_PAD = LANES
QW = HPG * Q_BLOCK
SEL_BLOCK = 1024
SEL_ROW_STEP = 32
ONES_ROWS = 16
VT_ROWS = HEAD_DIM + ONES_ROWS
ROW_TILE = 16
Q_SCALE = HEAD_DIM ** -0.5 * math.log2(math.e)
M_FLOOR = -1e20


def _dot(a, b):
    return jnp.dot(a, b, preferred_element_type=F32)


def _rms(x, g):
    return x * lax.rsqrt(jnp.mean(x * x, axis=-1, keepdims=True) + EPS) * g


def _split_bf16(x):
    hi = x.astype(BF16)
    lo = (x - hi.astype(F32)).astype(BF16)
    return hi, lo


def _const_spec(shape):
    nd = len(shape)
    return pl.BlockSpec(shape, lambda *_: (0,) * nd, pipeline_mode=pl.Buffered(1))


def _ada_kernel(c_ref, w_ref, b_ref, o_ref):
    c = c_ref[...]
    c_act = c * jax.nn.sigmoid(c)
    o_ref[...] = _dot(c_act.astype(BF16), w_ref[...].astype(BF16)) + b_ref[...]


def _ada_call(c_pad, w_ada, b_ada):
    rows, d = c_pad.shape
    n = w_ada.shape[1]
    tn = ADA_COL_TILE
    return pl.pallas_call(
        _ada_kernel,
        grid=(n // tn,),
        in_specs=[pl.BlockSpec((rows, d), lambda j: (0, 0)),
                  pl.BlockSpec((d, tn), lambda j: (0, j)),
                  pl.BlockSpec((1, tn), lambda j: (0, j))],
        out_specs=pl.BlockSpec((rows, tn), lambda j: (0, j)),
        out_shape=jax.ShapeDtypeStruct((rows, n), F32),
        compiler_params=pltpu.CompilerParams(dimension_semantics=("arbitrary",),
                                             vmem_limit_bytes=VMEM_LIMIT),
        name="adaln_mod",
    )(c_pad, w_ada, b_ada)


def _ffn_core(x, shift, scale, gate, g, wg_ref, wu_ref, wd_ref, a_ref):
    h = _rms(x, g) * (1.0 + scale) + shift
    hb = h.astype(BF16)
    d_ff = wg_ref.shape[1]
    for j in range(d_ff // FF_TILE):
        sl = slice(j * FF_TILE, (j + 1) * FF_TILE)
        gg = _dot(hb, wg_ref[:, sl])
        uu = _dot(hb, wu_ref[:, sl])
        a_ref[:, sl] = (gg * jax.nn.sigmoid(gg) * uu).astype(a_ref.dtype)
    return x + (MACARON_W * gate) * _dot(a_ref[...], wd_ref[...])


def _cast_once(src_refs, dst_refs):
    @pl.when((pl.program_id(0) == 0) & (pl.program_id(1) == 0))
    def _():
        for src, dst in zip(src_refs, dst_refs):
            for c in range(0, src.shape[1], FF_TILE):
                dst[:, c:c + FF_TILE] = src[:, c:c + FF_TILE].astype(dst.dtype)


def _ffn_kernel(x_ref, mod_ref, g_ref, wg_ref, wu_ref, wd_ref, o_ref, a_ref, wgb_ref, wub_ref):
    _cast_once((wg_ref, wu_ref), (wgb_ref, wub_ref))
    o_ref[...] = _ffn_core(x_ref[...], mod_ref[0:1, :], mod_ref[1:2, :], mod_ref[2:3, :],
                           g_ref[...], wgb_ref, wub_ref, wd_ref, a_ref)


def _ffn_call(x, mod, g, wg, wu, wd):
    B, S, D = x.shape
    tm = FFN_TOKEN_TILE
    return pl.pallas_call(
        _ffn_kernel,
        grid=(B, S // tm),
        in_specs=[pl.BlockSpec((None, tm, D), lambda b, i: (b, i, 0)),
                  pl.BlockSpec((None, N_MOD, D), lambda b, i: (b, 0, 0)),
                  _const_spec(g.shape), _const_spec(wg.shape), _const_spec(wu.shape),
                  _const_spec(wd.shape)],
        out_specs=pl.BlockSpec((None, tm, D), lambda b, i: (b, i, 0)),
        out_shape=jax.ShapeDtypeStruct((B, S, D), F32),
        scratch_shapes=[pltpu.VMEM((tm, wg.shape[1]), BF16), pltpu.VMEM(wg.shape, BF16),
                        pltpu.VMEM(wu.shape, BF16)],
        compiler_params=pltpu.CompilerParams(dimension_semantics=("arbitrary", "arbitrary"),
                                             vmem_limit_bytes=VMEM_LIMIT),
        name="ffn1",
    )(x, mod, g, wg, wu, wd)


def _rope_table_kernel(pos_ref, freq_ref, cos_ref, sin_ref):
    pos = jnp.concatenate([pos_ref[...].astype(F32),
                           jnp.zeros((pos_ref.shape[0], LANES - TOKENS_PER_ROW), F32)], axis=1)
    src = lax.broadcasted_iota(jnp.int32, pos.shape, 1) // ROT_HALF
    ang = jnp.take_along_axis(pos, src, axis=1) * freq_ref[...]
    cos_ref[...] = jnp.cos(ang)
    sin_ref[...] = jnp.sin(ang)


def _rope_table_call(pos_rows, freq):
    rows = pos_rows.shape[0]
    shape = jax.ShapeDtypeStruct((rows, LANES), F32)
    full = pl.BlockSpec((rows, LANES), lambda: (0, 0))
    return pl.pallas_call(
        _rope_table_kernel,
        in_specs=[pl.BlockSpec(pos_rows.shape, lambda: (0, 0)), pl.BlockSpec(freq.shape, lambda: (0, 0))],
        out_specs=(full, full),
        out_shape=(shape, shape),
        compiler_params=pltpu.CompilerParams(vmem_limit_bytes=VMEM_LIMIT),
        name="rope_table",
    )(pos_rows, freq)


def _inproj_kernel(x_ref, cosp_ref, sinp_ref, mod_ref, gmix_ref, win_ref, wgate_ref, convw_ref, gconv_ref,
                   gmat_ref, onehot_ref, yc_ref, qT_ref, kc_ref, vc_ref, ksa_ref, vsT_ref, kw_ref,
                   vwT_ref, gT_ref, carry_ref):
    tm = x_ref.shape[0]

    @pl.when(pl.program_id(1) == 0)
    def _():
        carry_ref[...] = jnp.zeros_like(carry_ref)

    x = x_ref[...]
    h = _rms(x, gmix_ref[...]) * (1.0 + mod_ref[4:5, :]) + mod_ref[3:4, :]
    hb = h.astype(BF16)

    def proj(c0, width):
        return _dot(hb, win_ref[:, c0:c0 + width])

    d = lax.broadcasted_iota(jnp.int32, (tm, LANES), 1) & (HEAD_DIM - 1)
    token = lax.broadcasted_iota(jnp.int32, (tm, LANES), 0) & (TOKENS_PER_ROW - 1)
    src_lane = token * ROT_HALF + (d & (ROT_HALF - 1))

    def unpack(packed_ref):
        rows = jnp.broadcast_to(packed_ref[...][:, None, :], (tm // TOKENS_PER_ROW, TOKENS_PER_ROW, LANES))
        return jnp.take_along_axis(rows.reshape(tm, LANES), src_lane, axis=1)

    cos_t = jnp.where(d < ROT_DIM, unpack(cosp_ref), 1.0)
    sin_raw = unpack(sinp_ref)
    sin_t = jnp.where(d < ROT_HALF, -sin_raw, jnp.where(d < ROT_DIM, sin_raw, 0.0))
    first_half = d < ROT_HALF

    def rope(t):
        outs = []
        for j in range(t.shape[1] // LANES):
            tj = t[:, j * LANES:(j + 1) * LANES]
            partner = jnp.where(first_half, pltpu.roll(tj, LANES - ROT_HALF, 1),
                                pltpu.roll(tj, ROT_HALF, 1))
            outs.append(tj * cos_t + partner * sin_t)
        return outs[0] if len(outs) == 1 else jnp.concatenate(outs, axis=1)

    c0 = 3 * CONV_CH
    q = rope(proj(c0, ATTN_WIDTH)) * Q_SCALE
    qT_ref[...] = q.T.astype(qT_ref.dtype)
    c0 += ATTN_WIDTH
    kv = proj(c0, 2 * KV_WIDTH)
    kc_ref[...] = rope(kv[:, :KV_WIDTH])
    vc_ref[...] = kv[:, KV_WIDTH:]
    kv = proj(c0 + 2 * KV_WIDTH, 2 * KV_WIDTH)
    ks = rope(kv[:, :KV_WIDTH]).astype(BF16)
    vsT = kv[:, KV_WIDTH:].T.astype(BF16)
    kv = proj(c0 + 4 * KV_WIDTH, 2 * KV_WIDTH)
    kw = rope(kv[:, :KV_WIDTH]).astype(BF16)
    vwT = kv[:, KV_WIDTH:].T.astype(BF16)
    pad = jnp.zeros((tm, MXU_DEPTH - LANES - HEAD_DIM), BF16)
    ones = jnp.ones((ONES_ROWS, tm), BF16)
    for g in range(N_KV_GROUPS):
        kg = ks[:, g * HEAD_DIM:(g + 1) * HEAD_DIM]
        ksa_ref[g] = jnp.concatenate([onehot_ref[...], kg, pad], axis=1)
        kw_ref[g] = kw[:, g * HEAD_DIM:(g + 1) * HEAD_DIM]
        vsT_ref[g] = jnp.concatenate([vsT[g * HEAD_DIM:(g + 1) * HEAD_DIM, :], ones], axis=0)
        vwT_ref[g] = jnp.concatenate([vwT[g * HEAD_DIM:(g + 1) * HEAD_DIM, :], ones], axis=0)
    gT_ref[...] = jax.nn.sigmoid(_dot(hb, wgate_ref[...])).T

    cb = proj(0, CONV_CH)
    u = proj(CONV_CH, CONV_CH) * proj(2 * CONV_CH, CONV_CH)
    row = lax.broadcasted_iota(jnp.int32, (tm, 1), 0)
    prev1 = carry_ref[7:8, :]
    prev2 = carry_ref[6:7, :]
    u1 = jnp.where(row >= 1, pltpu.roll(u, 1, 0), prev1)
    u2 = jnp.where(row >= 2, pltpu.roll(u, 2, 0), jnp.where(row == 1, prev1, prev2))
    carry_ref[...] = u[tm - 8:tm, :]
    v = convw_ref[0:1, :] * u2 + convw_ref[1:2, :] * u1 + convw_ref[2:3, :] * u
    y = cb * v
    ms = _dot((y * y).astype(BF16), gmat_ref[...])
    yc_ref[...] = (y * lax.rsqrt(ms + EPS) * gconv_ref[...]).astype(yc_ref.dtype)


def _inproj_call(x, cos_p, sin_p, mod, gmix, win, wgate, convw, gconv, gmat, onehot):
    B, S, D = x.shape
    tm = TOKEN_TILE
    tok = lambda w: pl.BlockSpec((None, tm, w), lambda b, i: (b, i, 0))
    tr = lambda w: pl.BlockSpec((None, w, tm), lambda b, i: (b, 0, i))
    grp = lambda w: pl.BlockSpec((None, N_KV_GROUPS, tm, w), lambda b, i: (b, 0, i, 0))
    grpT = pl.BlockSpec((None, N_KV_GROUPS, VT_ROWS, tm), lambda b, i: (b, 0, 0, i))
    packed = pl.BlockSpec((None, tm // TOKENS_PER_ROW, LANES), lambda b, i: (b, i, 0))
    out_shapes = (
        jax.ShapeDtypeStruct((B, S, CONV_CH), BF16),
        jax.ShapeDtypeStruct((B, ATTN_WIDTH, S), BF16),
        jax.ShapeDtypeStruct((B, S, KV_WIDTH), F32),
        jax.ShapeDtypeStruct((B, S, KV_WIDTH), F32),
        jax.ShapeDtypeStruct((B, N_KV_GROUPS, S, MXU_DEPTH), BF16),
        jax.ShapeDtypeStruct((B, N_KV_GROUPS, VT_ROWS, S), BF16),
        jax.ShapeDtypeStruct((B, N_KV_GROUPS, S, HEAD_DIM), BF16),
        jax.ShapeDtypeStruct((B, N_KV_GROUPS, VT_ROWS, S), BF16),
        jax.ShapeDtypeStruct((B, N_KV_GROUPS * GATE_PAD, S), F32),
    )
    return pl.pallas_call(
        _inproj_kernel,
        grid=(B, S // tm),
        in_specs=[tok(D),
                  packed, packed,
                  pl.BlockSpec((None, N_MOD, D), lambda b, i: (b, 0, 0)),
                  _const_spec(gmix.shape), _const_spec(win.shape), _const_spec(wgate.shape),
                  _const_spec(convw.shape),
                  _const_spec(gconv.shape), _const_spec(gmat.shape),
                  pl.BlockSpec((tm, onehot.shape[1]), lambda b, i: (i, 0))],
        out_specs=(tok(CONV_CH), tr(ATTN_WIDTH), tok(KV_WIDTH), tok(KV_WIDTH), grp(MXU_DEPTH),
                   grpT, grp(HEAD_DIM), grpT, tr(N_KV_GROUPS * GATE_PAD)),
        out_shape=out_shapes,
        scratch_shapes=[pltpu.VMEM((8, CONV_CH), F32)],
        compiler_params=pltpu.CompilerParams(dimension_semantics=("arbitrary", "arbitrary"),
                                             vmem_limit_bytes=VMEM_LIMIT),
        name="mixer_inproj",
    )(x, cos_p, sin_p, mod, gmix, win, wgate, convw, gconv, gmat, onehot)


def _compress_kernel(kf_ref, vf_ref, pek_ref, pev_ref, wk1_ref, wk2_ref, wv1_ref, wv2_ref,
                     kcc_ref, vcT_ref):
    half = CMP_BLOCK // 2
    n = kf_ref.shape[0] // half

    def mlp(x_ref, pe_ref, w1_ref, w2_ref):
        parts = []
        for p in range(2):
            acc = None
            for l0 in range(0, half, 2):
                xs, ws = [], []
                for l in (l0, l0 + 1):
                    row = p * half + l
                    xs.append((x_ref[pl.ds(l, n, stride=half), :]
                               + pe_ref[row:row + 1, :]).astype(BF16))
                    w = w1_ref[row * HEAD_DIM:(row + 1) * HEAD_DIM, :].astype(BF16)
                    z = jnp.zeros_like(w)
                    ws.append(jnp.concatenate([jnp.concatenate([w, z], axis=1),
                                               jnp.concatenate([z, w], axis=1)], axis=0))
                d = _dot(jnp.concatenate(xs, axis=1), jnp.concatenate(ws, axis=0))
                acc = d if acc is None else acc + d
            parts.append(acc)
        hpre = parts[0] + pltpu.roll(parts[1], n - 1, 0)
        hid = (hpre * jax.nn.sigmoid(hpre)).astype(BF16)
        w2 = w2_ref[...].astype(BF16)
        return jnp.concatenate([_dot(hid[:, g * CMP_HIDDEN:(g + 1) * CMP_HIDDEN], w2)
                                for g in range(N_KV_GROUPS)], axis=1)

    kc = mlp(kf_ref, pek_ref, wk1_ref, wk2_ref).astype(kcc_ref.dtype)
    for g in range(N_KV_GROUPS):
        kcc_ref[g] = kc[:, g * HEAD_DIM:(g + 1) * HEAD_DIM]
    vcT = mlp(vf_ref, pev_ref, wv1_ref, wv2_ref).T.astype(vcT_ref.dtype)
    ones = jnp.ones((ONES_ROWS, n), vcT_ref.dtype)
    for g in range(N_KV_GROUPS):
        vcT_ref[g] = jnp.concatenate([vcT[g * HEAD_DIM:(g + 1) * HEAD_DIM, :], ones], axis=0)


def _compress_call(kf, vf, pek, pev, wk1, wk2, wv1, wv2):
    B, S, width = kf.shape
    n = S // (CMP_BLOCK // 2)
    flat = pl.BlockSpec((None, S, width), lambda b: (b, 0, 0))
    return pl.pallas_call(
        _compress_kernel,
        grid=(B,),
        in_specs=[flat, flat, _const_spec(pek.shape), _const_spec(pev.shape),
                  _const_spec(wk1.shape), _const_spec(wk2.shape), _const_spec(wv1.shape),
                  _const_spec(wv2.shape)],
        out_specs=(pl.BlockSpec((None, N_KV_GROUPS, n, HEAD_DIM), lambda b: (b, 0, 0, 0)),
                   pl.BlockSpec((None, N_KV_GROUPS, VT_ROWS, n), lambda b: (b, 0, 0, 0))),
        out_shape=(jax.ShapeDtypeStruct((B, N_KV_GROUPS, n, HEAD_DIM), BF16),
                   jax.ShapeDtypeStruct((B, N_KV_GROUPS, VT_ROWS, n), BF16)),
        compiler_params=pltpu.CompilerParams(dimension_semantics=("arbitrary",),
                                             vmem_limit_bytes=VMEM_LIMIT),
        name="kv_compress",
    )(kf, vf, pek, pev, wk1, wk2, wv1, wv2)


def _lane_tiles(x, n):
    return jnp.concatenate([x] * n, axis=1)


def _col_max(s_ref, n_rows, bias_fn, lane_tile=None):
    width = s_ref.shape[1]
    lt = lane_tile or width
    groups = ROW_TILE // 8
    mx = {c: [jnp.full((8, lt), NEG, F32)] * groups for c in range(0, width, lt)}
    for r in range(0, n_rows, ROW_TILE):
        bias = None if bias_fn is None else bias_fn(r)
        for c in range(0, width, lt):
            x = s_ref[r:r + ROW_TILE, c:c + lt]
            if bias is not None:
                x = x + bias
                s_ref[r:r + ROW_TILE, c:c + lt] = x
            mx[c] = [jnp.maximum(mx[c][i], x[8 * i:8 * (i + 1), :]) for i in range(groups)]
    out = []
    for c in range(0, width, lt):
        m = mx[c]
        while len(m) > 1:
            m = [jnp.maximum(a, b) for a, b in zip(m[0::2], m[1::2])]
        out.append(jnp.max(m[0], axis=0, keepdims=True))
    return out[0] if len(out) == 1 else jnp.concatenate(out, axis=1)


def _col_exp2(s_ref, p_ref, n_rows, m, keep_f32=False, lane_tile=None):
    width = s_ref.shape[1]
    lt = lane_tile or width
    for r in range(0, n_rows, ROW_TILE):
        for c in range(0, width, lt):
            p = jnp.exp2(s_ref[r:r + ROW_TILE, c:c + lt] - m[:, c:c + lt])
            if keep_f32:
                s_ref[r:r + ROW_TILE, c:c + lt] = p
            p_ref[r:r + ROW_TILE, c:c + lt] = p.astype(p_ref.dtype)


def _recip_pos(l):
    return 1.0 / jnp.where(l > 0.0, l, 1.0)


def _select_kernel(top_n, qT_ref, kcc_ref, vcT_ref, ovlT_ref, sel_ref, ocmp_ref, sc_ref, pc_ref,
                   ph_ref, pl_ref):
    QB = SEL_BLOCK
    n_cmp = kcc_ref.shape[0]
    n_slc = ovlT_ref.shape[0]
    t0 = pl.program_id(2) * QB
    tq = t0 + lax.broadcasted_iota(jnp.int32, (1, QB), 1)
    rows = lax.broadcasted_iota(jnp.int32, (ROW_TILE, 1), 0)

    qT = qT_ref[...]
    qcat = jnp.concatenate([qT[h * HEAD_DIM:(h + 1) * HEAD_DIM, :] for h in range(HPG)], axis=1)

    def cmp_bias(r):
        cmp_end = (r + rows) * CMP_STRIDE + (CMP_BLOCK - 1)
        return jnp.where(cmp_end <= tq, 0.0, NEG)

    cur = tq // SLC_BLOCK

    def tree(op, xs):
        while len(xs) > 1:
            xs = [op(*xs[i:i + 2]) if i + 1 < len(xs) else xs[i] for i in range(0, len(xs), 2)]
        return xs[0]

    def causal_variant(n_rows):
        rows_cmp = min(n_cmp, n_rows * (SLC_BLOCK // CMP_STRIDE))

        def run():
            sc_ref[0:rows_cmp, :] = _dot(kcc_ref[0:rows_cmp, :], qcat)
            m = jnp.maximum(_col_max(sc_ref, rows_cmp, cmp_bias, lane_tile=QB), M_FLOOR)
            _col_exp2(sc_ref, pc_ref, rows_cmp, m, keep_f32=True, lane_tile=QB)
            o_cmp = _dot(vcT_ref[:, 0:rows_cmp], pc_ref[0:rows_cmp, :])
            rl = _recip_pos(o_cmp[HEAD_DIM:HEAD_DIM + 1, :])
            o_cmp = o_cmp[0:HEAD_DIM, :] * rl
            for h in range(HPG):
                ocmp_ref[h] = o_cmp[:, h * QB:(h + 1) * QB]

            for r in range(0, rows_cmp, ROW_TILE):
                psum = sc_ref[r:r + ROW_TILE, 0:QB] * rl[:, 0:QB]
                for h in range(1, HPG):
                    sl = slice(h * QB, (h + 1) * QB)
                    psum = psum + sc_ref[r:r + ROW_TILE, sl] * rl[:, sl]
                hi, lo = _split_bf16(psum)
                ph_ref[r:r + ROW_TILE, :] = hi
                pl_ref[r:r + ROW_TILE, :] = lo
            ovl = ovlT_ref[0:n_rows, 0:rows_cmp]
            imp = _dot(ovl, ph_ref[0:rows_cmp, :]) + _dot(ovl, pl_ref[0:rows_cmp, :])

            blk = lax.broadcasted_iota(jnp.int32, (n_rows, 1), 0)
            future = blk > cur
            forced = (blk == 0) | (blk == cur) | (blk == cur - 1)
            groups = range(n_rows // 8)

            def store(picked):
                sel_ref[0:n_rows, :] = jnp.where(future, NEG, jnp.where(picked, 0.0, NEG)).astype(sel_ref.dtype)

            def with_ties():
                score0 = jnp.where(future, -BIG, jnp.where(forced, BIG, imp))
                score = [score0[8 * i:8 * (i + 1), :] for i in groups]
                cum = jnp.zeros((1, QB), F32)
                thr = jnp.zeros((1, QB), F32)
                above = jnp.zeros((1, QB), F32)
                for _ in range(top_n):
                    best = jnp.max(tree(jnp.maximum, score), axis=0, keepdims=True)
                    eq = [s == best for s in score]
                    unfilled = cum < top_n
                    thr = jnp.where(unfilled, best, thr)
                    above = jnp.where(unfilled, cum, above)
                    cum = cum + jnp.sum(tree(jnp.add, [jnp.where(e, 1.0, 0.0) for e in eq]),
                                        axis=0, keepdims=True)
                    score = [jnp.where(e, -jnp.inf, s) for e, s in zip(eq, score)]
                ties = score0 == thr
                lower = jnp.where(lax.broadcasted_iota(jnp.int32, (1, n_rows), 1) < blk, 1.0, 0.0)
                rank = _dot(lower.astype(BF16), jnp.where(ties, 1.0, 0.0).astype(BF16))
                store((score0 > thr) | (ties & (rank < top_n - above)))

            if top_n > MAX_FORCED:
                n_forced = 1.0 + jnp.where(cur >= 1, 1.0, 0.0) + jnp.where(cur >= 2, 1.0, 0.0)
                wanted = top_n - n_forced
                take_all = (cur.astype(F32) + 1.0 - n_forced) <= wanted
                cand = jnp.where(future | forced, -BIG, imp)
                score = [cand[8 * i:8 * (i + 1), :] for i in groups]
                for _ in range(top_n - MAX_FORCED):
                    cut = jnp.max(tree(jnp.maximum, score), axis=0, keepdims=True)
                    score = [jnp.where(s == cut, -jnp.inf, s) for s in score]
                above_cut = cand >= cut
                count = jnp.sum(jnp.where(above_cut, 1.0, 0.0), axis=0, keepdims=True)
                exact = jnp.min(jnp.where(take_all | (count == wanted), 1.0, 0.0)) > 0.5
                lax.cond(exact, lambda: store(forced | take_all | above_cut), with_ties)
            else:
                with_ties()
            if n_rows < n_slc:
                sel_ref[n_rows:n_slc, :] = jnp.full((n_slc - n_rows, QB), NEG, sel_ref.dtype)
        return run

    steps = n_slc // SEL_ROW_STEP
    need = jnp.minimum((t0 + QB - 1) // (SLC_BLOCK * SEL_ROW_STEP), steps - 1)
    lax.switch(need, [causal_variant(SEL_ROW_STEP * (k + 1)) for k in range(steps)])


def _select_call(top_n, qT, kcc, vcT, ovlT):
    B, _, S = qT.shape
    n_cmp = kcc.shape[2]
    n_slc = ovlT.shape[0]
    gw = HPG * HEAD_DIM
    qw = HPG * SEL_BLOCK
    return pl.pallas_call(
        functools.partial(_select_kernel, top_n),
        grid=(B, N_KV_GROUPS, S // SEL_BLOCK),
        in_specs=[pl.BlockSpec((None, gw, SEL_BLOCK), lambda b, g, i: (b, g, i)),
                  pl.BlockSpec((None, None, n_cmp, HEAD_DIM), lambda b, g, i: (b, g, 0, 0)),
                  pl.BlockSpec((None, None, VT_ROWS, n_cmp), lambda b, g, i: (b, g, 0, 0)),
                  pl.BlockSpec(ovlT.shape, lambda b, g, i: (0, 0))],
        out_specs=(pl.BlockSpec((None, None, n_slc, SEL_BLOCK), lambda b, g, i: (b, g, 0, i)),
                   pl.BlockSpec((None, None, HPG, HEAD_DIM, SEL_BLOCK),
                                lambda b, g, i: (b, g, 0, 0, i))),
        out_shape=(jax.ShapeDtypeStruct((B, N_KV_GROUPS, n_slc, S), BF16),
                   jax.ShapeDtypeStruct((B, N_KV_GROUPS, HPG, HEAD_DIM, S), F32)),
        scratch_shapes=[pltpu.VMEM((n_cmp, qw), F32), pltpu.VMEM((n_cmp, qw), BF16),
                        pltpu.VMEM((n_cmp, SEL_BLOCK), BF16), pltpu.VMEM((n_cmp, SEL_BLOCK), BF16)],
        compiler_params=pltpu.CompilerParams(
            dimension_semantics=("parallel", "parallel", "arbitrary"),
            vmem_limit_bytes=VMEM_LIMIT),
        name="nsa_select",
    )(qT, kcc, vcT, ovlT)


def _attn_kernel(qT_ref, gT_ref, sel_ref, ocmp_ref, ksa_ref, vsT_ref, kw_ref, vwT_ref, ga_ref,
                 o_ref, s0_ref, s1_ref, p0_ref, p1_ref, sw_ref, pw_ref, rhs_ref, acc_ref):
    QB = Q_BLOCK
    n_slc = sel_ref.shape[0]
    t0 = pl.program_id(2) * QB
    tq = t0 + lax.broadcasted_iota(jnp.int32, (1, QB), 1)
    rows = lax.broadcasted_iota(jnp.int32, (ROW_TILE, 1), 0)

    qT = qT_ref[...]
    qcat = jnp.concatenate([qT[h * HEAD_DIM:(h + 1) * HEAD_DIM, :] for h in range(HPG)], axis=1)

    rhs_ref[...] = jnp.concatenate([_lane_tiles(sel_ref[...], HPG), qcat,
                                    jnp.zeros((MXU_DEPTH - n_slc - HEAD_DIM, QW), BF16)], axis=0)

    w0 = pl.multiple_of(jnp.maximum(t0 - WINDOW, 0), QB)
    sw_ref[...] = _dot(kw_ref[pl.ds(w0, WIN_KEYS), :], qcat)

    def win_bias(r):
        dist = tq - (w0 + r + rows)
        return _lane_tiles(jnp.where((dist >= 0) & (dist < WINDOW), 0.0, NEG), HPG)

    KC = KEY_CHUNK
    last = t0 // KC

    def scores(c, dst_ref):
        k0 = pl.multiple_of(jnp.minimum(c, last) * KC, KC)
        dst_ref[...] = _dot(ksa_ref[pl.ds(k0, KC), :], rhs_ref[...])

    def weighted_values(c, p_ref):
        k0 = pl.multiple_of(jnp.clip(c, 0, last) * KC, KC)
        return _dot(vsT_ref[:, pl.ds(k0, KC)], p_ref[...])

    def softmax(c, src_ref, p_ref, m, pending, causal):
        def causal_bias(r):
            return _lane_tiles(jnp.where(c * KC + r + rows <= tq, 0.0, NEG), HPG)

        m_new = jnp.maximum(m, _col_max(src_ref, KC, causal_bias if causal else None))
        acc = acc_ref[...] if pending is None else acc_ref[...] + pending
        acc_ref[...] = jnp.exp2(m - m_new) * acc
        _col_exp2(src_ref, p_ref, KC, m_new)
        return m_new

    def pair(i, m, final=False, first=False):
        pending = None if first else weighted_values(2 * i - 1, p1_ref)
        scores(2 * i + 1, s1_ref)
        m = softmax(2 * i, s0_ref, p0_ref, m, pending, False)
        pending = weighted_values(2 * i, p0_ref)
        if not final:
            scores(2 * i + 2, s0_ref)
        return softmax(2 * i + 1, s1_ref, p1_ref, m, pending, final)

    p1_ref[...] = jnp.zeros_like(p1_ref)
    acc_ref[...] = jnp.zeros_like(acc_ref)
    scores(0, s0_ref)

    m = _col_max(sw_ref, WIN_KEYS, win_bias)
    _col_exp2(sw_ref, pw_ref, WIN_KEYS, m)
    o_win = _dot(vwT_ref[:, pl.ds(w0, WIN_KEYS)], pw_ref[...])
    o_win = o_win[0:HEAD_DIM, :] * (1.0 / o_win[HEAD_DIM:HEAD_DIM + 1, :])

    full_pairs = last // 2
    m = lax.cond(full_pairs >= 1, lambda mm: pair(0, mm, first=True), lambda mm: mm,
                 jnp.full((1, QW), NEG, F32))
    rest = jnp.maximum(full_pairs - 1, 0)
    m = lax.fori_loop(0, rest // 2, lambda j, mm: pair(2 * j + 2, pair(2 * j + 1, mm)), m)
    m = lax.cond(rest % 2 == 1, lambda mm: pair(full_pairs - 1, mm), lambda mm: mm, m)

    @pl.when(last % 2 == 0)
    def _():
        pending = weighted_values(last - 1, p1_ref)
        softmax(last, s0_ref, p0_ref, m, pending, True)
        acc_ref[...] += weighted_values(last, p0_ref)

    @pl.when(last % 2 == 1)
    def _():
        pair(full_pairs, m, final=True)
        acc_ref[...] += weighted_values(last, p1_ref)

    acc = acc_ref[...]
    o_slc = acc[0:HEAD_DIM, :] * (1.0 / acc[HEAD_DIM:HEAD_DIM + 1, :])

    gts = gT_ref[...]
    outs = []
    for h in range(HPG):
        sl = slice(h * QB, (h + 1) * QB)
        o = (gts[3 * h:3 * h + 1, :] * ocmp_ref[h] + gts[3 * h + 1:3 * h + 2, :] * o_slc[:, sl]
             + gts[3 * h + 2:3 * h + 3, :] * o_win[:, sl])
        o = o * lax.rsqrt(jnp.mean(o * o, axis=0, keepdims=True) + EPS)
        outs.append((o * ga_ref[h * HEAD_DIM:(h + 1) * HEAD_DIM, :]).T)
    o_ref[...] = jnp.concatenate(outs, axis=1).astype(o_ref.dtype)


def _attn_call(qT, gatesT, sel_bias, o_cmp, ksa, vsT, kw, vwT, g_attn_col):
    B, _, S = qT.shape
    n_slc = sel_bias.shape[2]
    gw = HPG * HEAD_DIM
    rows = lambda n, w: pl.BlockSpec((None, None, n, w), lambda b, g, i: (b, g, 0, 0))
    cols = lambda n: pl.BlockSpec((None, None, VT_ROWS, n), lambda b, g, i: (b, g, 0, 0))
    return pl.pallas_call(
        _attn_kernel,
        grid=(B, N_KV_GROUPS, S // Q_BLOCK),
        in_specs=[pl.BlockSpec((None, gw, Q_BLOCK), lambda b, g, i: (b, g, i)),
                  pl.BlockSpec((None, GATE_PAD, Q_BLOCK), lambda b, g, i: (b, g, i)),
                  pl.BlockSpec((None, None, n_slc, Q_BLOCK), lambda b, g, i: (b, g, 0, i)),
                  pl.BlockSpec((None, None, HPG, HEAD_DIM, Q_BLOCK), lambda b, g, i: (b, g, 0, 0, i)),
                  rows(S, MXU_DEPTH), cols(S), rows(S, HEAD_DIM), cols(S),
                  pl.BlockSpec((gw, 1), lambda b, g, i: (g, 0))],
        out_specs=pl.BlockSpec((None, Q_BLOCK, gw), lambda b, g, i: (b, i, g)),
        out_shape=jax.ShapeDtypeStruct((B, S, ATTN_WIDTH), BF16),
        scratch_shapes=[pltpu.VMEM((KEY_CHUNK, QW), F32), pltpu.VMEM((KEY_CHUNK, QW), F32),
                        pltpu.VMEM((KEY_CHUNK, QW), BF16), pltpu.VMEM((KEY_CHUNK, QW), BF16),
                        pltpu.VMEM((WIN_KEYS, QW), F32), pltpu.VMEM((WIN_KEYS, QW), BF16),
                        pltpu.VMEM((MXU_DEPTH, QW), BF16), pltpu.VMEM((VT_ROWS, QW), F32)],
        compiler_params=pltpu.CompilerParams(
            dimension_semantics=("parallel", "parallel", "arbitrary"),
            vmem_limit_bytes=VMEM_LIMIT),
        name="nsa_attention",
    )(qT, gatesT, sel_bias, o_cmp, ksa, vsT, kw, vwT, g_attn_col)


def _out_kernel(x_ref, yc_ref, ya_ref, mod_ref, wo_ref, g_ref, wg_ref, wu_ref, wd_ref, gf_ref,
                o_ref, a_ref, wgb_ref, wub_ref):
    _cast_once((wg_ref, wu_ref), (wgb_ref, wub_ref))
    mix = _dot(yc_ref[...], wo_ref[0:CONV_CH, :]) + _dot(ya_ref[...], wo_ref[CONV_CH:, :])
    x = x_ref[...] + mod_ref[5:6, :] * mix
    x = _ffn_core(x, mod_ref[6:7, :], mod_ref[7:8, :], mod_ref[8:9, :], g_ref[...], wgb_ref,
                  wub_ref, wd_ref, a_ref)
    o_ref[...] = _rms(x, gf_ref[...])


def _out_call(x, yc, ya, mod, wo, g, wg, wu, wd, gf):
    B, S, D = x.shape
    tm = FFN_TOKEN_TILE
    tok = lambda w: pl.BlockSpec((None, tm, w), lambda b, i: (b, i, 0))
    return pl.pallas_call(
        _out_kernel,
        grid=(B, S // tm),
        in_specs=[tok(D), tok(CONV_CH), tok(ATTN_WIDTH),
                  pl.BlockSpec((None, N_MOD, D), lambda b, i: (b, 0, 0)),
                  _const_spec(wo.shape), _const_spec(g.shape), _const_spec(wg.shape),
                  _const_spec(wu.shape), _const_spec(wd.shape), _const_spec(gf.shape)],
        out_specs=tok(D),
        out_shape=jax.ShapeDtypeStruct((B, S, D), F32),
        scratch_shapes=[pltpu.VMEM((tm, wg.shape[1]), BF16), pltpu.VMEM(wg.shape, BF16),
                        pltpu.VMEM(wu.shape, BF16)],
        compiler_params=pltpu.CompilerParams(dimension_semantics=("arbitrary", "arbitrary"),
                                             vmem_limit_bytes=VMEM_LIMIT),
        name="outproj_ffn2",
    )(x, yc, ya, mod, wo, g, wg, wu, wd, gf)


def kernel(x, c, positions, w_ada, b_ada, g_ffn1, w1_gate, w1_up, w1_down, g_mix, w_in, conv_w, cmp_pos_k, cmp_pos_v, w_cmpk1, w_cmpk2, w_cmpv1, w_cmpv2, g_out_conv, g_out_attn, w_out, g_ffn2, w2_gate, w2_up, w2_down, g_final):
    B, S, D = x.shape
    depth = w_ada.shape[0]
    n_slc = S // SLC_BLOCK
    half = CMP_BLOCK // 2
    n_half = S // half
    assert n_slc <= LANES, "selection-block one-hot is one lane tile wide"

    c_pad = jnp.pad(c, ((0, 8 - B), (0, 0)))
    row = lambda a: a.reshape(1, -1)

    freq_half = jnp.power(ROPE_THETA, -2.0 * jnp.arange(ROT_HALF, dtype=F32) / ROT_DIM)
    freq = jnp.tile(freq_half, LANES // ROT_HALF).reshape(1, LANES)
    gidx = np.arange(CONV_CH) // (CONV_CH // CONV_GROUPS)
    gmat = jnp.asarray((gidx[:, None] == gidx[None, :]) / (CONV_CH // CONV_GROUPS), dtype=BF16)
    c0 = np.arange(n_half) * CMP_STRIDE
    s0 = np.arange(LANES) * SLC_BLOCK
    ovlT = ((c0[None, :] <= s0[:, None] + SLC_BLOCK - 1) & (c0[None, :] + CMP_BLOCK - 1 >= s0[:, None]))
    ovlT = jnp.asarray(ovlT, dtype=BF16)
    onehot = jnp.asarray((np.arange(S) // SLC_BLOCK)[:, None] == np.arange(LANES)[None, :], dtype=BF16)
    cos_p, sin_p = _rope_table_call(positions.reshape(-1, TOKENS_PER_ROW), freq)
    cos_p = cos_p.reshape(B, S // TOKENS_PER_ROW, LANES)
    sin_p = sin_p.reshape(B, S // TOKENS_PER_ROW, LANES)

    for l in range(depth):
        mod = _ada_call(c_pad, w_ada[l], row(b_ada[l]))[:B].reshape(B, N_MOD, D)

        x = _ffn_call(x, mod, row(g_ffn1[l]), w1_gate[l], w1_up[l], w1_down[l].astype(BF16))

        n_main = w_in.shape[2] - N_KV_GROUPS * 3 * HPG
        gate_cols = [jnp.pad(w_in[l][:, n_main + g * 3 * HPG:n_main + (g + 1) * 3 * HPG],
                             ((0, 0), (0, GATE_PAD - 3 * HPG))) for g in range(N_KV_GROUPS)]
        wgate = jnp.concatenate(gate_cols, axis=1).astype(BF16)
        win = w_in[l].astype(BF16)
        (yc, qT, kc, vc, ksa, vsT, kw, vwT, gatesT) = _inproj_call(
            x, cos_p, sin_p, mod, row(g_mix[l]), win, wgate, conv_w[l], row(g_out_conv[l]), gmat,
            onehot)

        both_groups = lambda pe: jnp.tile(pe, (1, N_KV_GROUPS))
        kcc, vcT = _compress_call(kc, vc, both_groups(cmp_pos_k[l]), both_groups(cmp_pos_v[l]),
                                  w_cmpk1[l], w_cmpk2[l], w_cmpv1[l], w_cmpv2[l])

        sel_bias, o_cmp = _select_call(min(SLC_TOP_N, n_slc), qT, kcc, vcT, ovlT)
        ya = _attn_call(qT, gatesT, sel_bias, o_cmp, ksa, vsT, kw, vwT,
                        g_out_attn[l].reshape(ATTN_WIDTH, 1))

        assert l == depth - 1, "final norm is fused into the last layer's output kernel"
        x = _out_call(x, yc, ya, mod, w_out[l].astype(BF16), row(g_ffn2[l]),
                      w2_gate[l], w2_up[l], w2_down[l].astype(BF16),
                      row(g_final))
    return x
```

```python
import functools
import math

import numpy as np
import jax
import jax.numpy as jnp
from jax import lax
from jax.experimental import pallas as pl
from jax.experimental.pallas import tpu as pltpu

F32 = jnp.float32
BF16 = jnp.bfloat16

CONV_CH = 512
CONV_GROUPS = 8
N_HEADS = 8
N_KV_GROUPS = 2
HPG = N_HEADS // N_KV_GROUPS
HEAD_DIM = 64
ATTN_WIDTH = N_HEADS * HEAD_DIM
KV_WIDTH = N_KV_GROUPS * HEAD_DIM
ROPE_THETA = 500000.0
ROT_DIM = HEAD_DIM // 4
ROT_HALF = ROT_DIM // 2
CMP_BLOCK = 32
CMP_STRIDE = 16
CMP_HIDDEN = 256
SLC_BLOCK = 64
SLC_TOP_N = 16
MAX_FORCED = 3
WINDOW = 512
Q_BLOCK = 256
MACARON_W = 0.5
N_MOD = 9
EPS = 1e-6
NEG = -1e30
BIG = 1e9

LANES = 128
MXU_DEPTH = 256
VMEM_LIMIT = 58 * 1024 * 1024

TOKEN_TILE = 512
FFN_TOKEN_TILE = 512
ADA_COL_TILE = 1024
FF_TILE = 256
KEY_CHUNK = 512
WIN_KEYS = WINDOW + Q_BLOCK
TOKENS_PER_ROW = LANES // ROT_HALF
GATE_PAD = LANES
QW = HPG * Q_BLOCK
SEL_BLOCK = 1024
SEL_ROW_STEP = 32
ONES_ROWS = 16
VT_ROWS = HEAD_DIM + ONES_ROWS
ROW_TILE = 16
Q_SCALE = HEAD_DIM ** -0.5 * math.log2(math.e)
M_FLOOR = -1e20


def _dot(a, b):
    return jnp.dot(a, b, preferred_element_type=F32)


def _rms(x, g):
    return x * lax.rsqrt(jnp.mean(x * x, axis=-1, keepdims=True) + EPS) * g


def _split_bf16(x):
    hi = x.astype(BF16)
    lo = (x - hi.astype(F32)).astype(BF16)
    return hi, lo


def _const_spec(shape):
    nd = len(shape)
    return pl.BlockSpec(shape, lambda *_: (0,) * nd, pipeline_mode=pl.Buffered(1))


def _ada_kernel(c_ref, w_ref, b_ref, o_ref):
    c = c_ref[...]
    c_act = c * jax.nn.sigmoid(c)
    o_ref[...] = _dot(c_act.astype(BF16), w_ref[...].astype(BF16)) + b_ref[...]


def _ada_call(c_pad, w_ada, b_ada):
    rows, d = c_pad.shape
    n = w_ada.shape[1]
    tn = ADA_COL_TILE
    return pl.pallas_call(
        _ada_kernel,
        grid=(n // tn,),
        in_specs=[pl.BlockSpec((rows, d), lambda j: (0, 0)),
                  pl.BlockSpec((d, tn), lambda j: (0, j)),
                  pl.BlockSpec((1, tn), lambda j: (0, j))],
        out_specs=pl.BlockSpec((rows, tn), lambda j: (0, j)),
        out_shape=jax.ShapeDtypeStruct((rows, n), F32),
        compiler_params=pltpu.CompilerParams(dimension_semantics=("arbitrary",),
                                             vmem_limit_bytes=VMEM_LIMIT),
        name="adaln_mod",
    )(c_pad, w_ada, b_ada)


def _ffn_core(x, shift, scale, gate, g, wg_ref, wu_ref, wd_ref, a_ref):
    h = _rms(x, g) * (1.0 + scale) + shift
    hb = h.astype(BF16)
    d_ff = wg_ref.shape[1]
    for j in range(d_ff // FF_TILE):
        sl = slice(j * FF_TILE, (j + 1) * FF_TILE)
        gg = _dot(hb, wg_ref[:, sl])
        uu = _dot(hb, wu_ref[:, sl])
        a_ref[:, sl] = (gg * jax.nn.sigmoid(gg) * uu).astype(a_ref.dtype)
    return x + (MACARON_W * gate) * _dot(a_ref[...], wd_ref[...])


def _cast_once(src_refs, dst_refs):
    @pl.when((pl.program_id(0) == 0) & (pl.program_id(1) == 0))
    def _():
        for src, dst in zip(src_refs, dst_refs):
            for c in range(0, src.shape[1], FF_TILE):
                dst[:, c:c + FF_TILE] = src[:, c:c + FF_TILE].astype(dst.dtype)


def _ffn_kernel(x_ref, mod_ref, g_ref, wg_ref, wu_ref, wd_ref, o_ref, a_ref, wgb_ref, wub_ref):
    _cast_once((wg_ref, wu_ref), (wgb_ref, wub_ref))
    o_ref[...] = _ffn_core(x_ref[...], mod_ref[0:1, :], mod_ref[1:2, :], mod_ref[2:3, :],
                           g_ref[...], wgb_ref, wub_ref, wd_ref, a_ref)


def _ffn_call(x, mod, g, wg, wu, wd):
    B, S, D = x.shape
    tm = FFN_TOKEN_TILE
    return pl.pallas_call(
        _ffn_kernel,
        grid=(B, S // tm),
        in_specs=[pl.BlockSpec((None, tm, D), lambda b, i: (b, i, 0)),
                  pl.BlockSpec((None, N_MOD, D), lambda b, i: (b, 0, 0)),
                  _const_spec(g.shape), _const_spec(wg.shape), _const_spec(wu.shape),
                  _const_spec(wd.shape)],
        out_specs=pl.BlockSpec((None, tm, D), lambda b, i: (b, i, 0)),
        out_shape=jax.ShapeDtypeStruct((B, S, D), F32),
        scratch_shapes=[pltpu.VMEM((tm, wg.shape[1]), BF16), pltpu.VMEM(wg.shape, BF16),
                        pltpu.VMEM(wu.shape, BF16)],
        compiler_params=pltpu.CompilerParams(dimension_semantics=("arbitrary", "arbitrary"),
                                             vmem_limit_bytes=VMEM_LIMIT),
        name="ffn1",
    )(x, mod, g, wg, wu, wd)


def _rope_table_kernel(pos_ref, freq_ref, cos_ref, sin_ref):
    pos = jnp.concatenate([pos_ref[...].astype(F32),
                           jnp.zeros((pos_ref.shape[0], LANES - TOKENS_PER_ROW), F32)], axis=1)
    src = lax.broadcasted_iota(jnp.int32, pos.shape, 1) // ROT_HALF
    ang = jnp.take_along_axis(pos, src, axis=1) * freq_ref[...]
    cos_ref[...] = jnp.cos(ang)
    sin_ref[...] = jnp.sin(ang)


def _rope_table_call(pos_rows, freq):
    rows = pos_rows.shape[0]
    shape = jax.ShapeDtypeStruct((rows, LANES), F32)
    full = pl.BlockSpec((rows, LANES), lambda: (0, 0))
    return pl.pallas_call(
        _rope_table_kernel,
        in_specs=[pl.BlockSpec(pos_rows.shape, lambda: (0, 0)), pl.BlockSpec(freq.shape, lambda: (0, 0))],
        out_specs=(full, full),
        out_shape=(shape, shape),
        compiler_params=pltpu.CompilerParams(vmem_limit_bytes=VMEM_LIMIT),
        name="rope_table",
    )(pos_rows, freq)


def _inproj_kernel(x_ref, cosp_ref, sinp_ref, mod_ref, gmix_ref, win_ref, wgate_ref, convw_ref, gconv_ref,
                   gmat_ref, onehot_ref, yc_ref, qT_ref, kc_ref, vc_ref, ksa_ref, vsT_ref, kw_ref,
                   vwT_ref, gT_ref, carry_ref):
    tm = x_ref.shape[0]

    @pl.when(pl.program_id(1) == 0)
    def _():
        carry_ref[...] = jnp.zeros_like(carry_ref)

    x = x_ref[...]
    h = _rms(x, gmix_ref[...]) * (1.0 + mod_ref[4:5, :]) + mod_ref[3:4, :]
    hb = h.astype(BF16)

    def proj(c0, width):
        return _dot(hb, win_ref[:, c0:c0 + width])

    d = lax.broadcasted_iota(jnp.int32, (tm, LANES), 1) & (HEAD_DIM - 1)
    token = lax.broadcasted_iota(jnp.int32, (tm, LANES), 0) & (TOKENS_PER_ROW - 1)
    src_lane = token * ROT_HALF + (d & (ROT_HALF - 1))

    def unpack(packed_ref):
        rows = jnp.broadcast_to(packed_ref[...][:, None, :], (tm // TOKENS_PER_ROW, TOKENS_PER_ROW, LANES))
        return jnp.take_along_axis(rows.reshape(tm, LANES), src_lane, axis=1)

    cos_t = jnp.where(d < ROT_DIM, unpack(cosp_ref), 1.0)
    sin_raw = unpack(sinp_ref)
    sin_t = jnp.where(d < ROT_HALF, -sin_raw, jnp.where(d < ROT_DIM, sin_raw, 0.0))
    first_half = d < ROT_HALF

    def rope(t):
        outs = []
        for j in range(t.shape[1] // LANES):
            tj = t[:, j * LANES:(j + 1) * LANES]
            partner = jnp.where(first_half, pltpu.roll(tj, LANES - ROT_HALF, 1),
                                pltpu.roll(tj, ROT_HALF, 1))
            outs.append(tj * cos_t + partner * sin_t)
        return outs[0] if len(outs) == 1 else jnp.concatenate(outs, axis=1)

    c0 = 3 * CONV_CH
    q = rope(proj(c0, ATTN_WIDTH)) * Q_SCALE
    qT_ref[...] = q.T.astype(qT_ref.dtype)
    c0 += ATTN_WIDTH
    kv = proj(c0, 2 * KV_WIDTH)
    kc_ref[...] = rope(kv[:, :KV_WIDTH])
    vc_ref[...] = kv[:, KV_WIDTH:]
    kv = proj(c0 + 2 * KV_WIDTH, 2 * KV_WIDTH)
    ks = rope(kv[:, :KV_WIDTH]).astype(BF16)
    vsT = kv[:, KV_WIDTH:].T.astype(BF16)
    kv = proj(c0 + 4 * KV_WIDTH, 2 * KV_WIDTH)
    kw = rope(kv[:, :KV_WIDTH]).astype(BF16)
    vwT = kv[:, KV_WIDTH:].T.astype(BF16)
    pad = jnp.zeros((tm, MXU_DEPTH - LANES - HEAD_DIM), BF16)
    ones = jnp.ones((ONES_ROWS, tm), BF16)
    for g in range(N_KV_GROUPS):
        kg = ks[:, g * HEAD_DIM:(g + 1) * HEAD_DIM]
        ksa_ref[g] = jnp.concatenate([onehot_ref[...], kg, pad], axis=1)
        kw_ref[g] = kw[:, g * HEAD_DIM:(g + 1) * HEAD_DIM]
        vsT_ref[g] = jnp.concatenate([vsT[g * HEAD_DIM:(g + 1) * HEAD_DIM, :], ones], axis=0)
        vwT_ref[g] = jnp.concatenate([vwT[g * HEAD_DIM:(g + 1) * HEAD_DIM, :], ones], axis=0)
    gT_ref[...] = jax.nn.sigmoid(_dot(hb, wgate_ref[...])).T

    cb = proj(0, CONV_CH)
    u = proj(CONV_CH, CONV_CH) * proj(2 * CONV_CH, CONV_CH)
    row = lax.broadcasted_iota(jnp.int32, (tm, 1), 0)
    prev1 = carry_ref[7:8, :]
    prev2 = carry_ref[6:7, :]
    u1 = jnp.where(row >= 1, pltpu.roll(u, 1, 0), prev1)
    u2 = jnp.where(row >= 2, pltpu.roll(u, 2, 0), jnp.where(row == 1, prev1, prev2))
    carry_ref[...] = u[tm - 8:tm, :]
    v = convw_ref[0:1, :] * u2 + convw_ref[1:2, :] * u1 + convw_ref[2:3, :] * u
    y = cb * v
    ms = _dot((y * y).astype(BF16), gmat_ref[...])
    yc_ref[...] = (y * lax.rsqrt(ms + EPS) * gconv_ref[...]).astype(yc_ref.dtype)


def _inproj_call(x, cos_p, sin_p, mod, gmix, win, wgate, convw, gconv, gmat, onehot):
    B, S, D = x.shape
    tm = TOKEN_TILE
    tok = lambda w: pl.BlockSpec((None, tm, w), lambda b, i: (b, i, 0))
    tr = lambda w: pl.BlockSpec((None, w, tm), lambda b, i: (b, 0, i))
    grp = lambda w: pl.BlockSpec((None, N_KV_GROUPS, tm, w), lambda b, i: (b, 0, i, 0))
    grpT = pl.BlockSpec((None, N_KV_GROUPS, VT_ROWS, tm), lambda b, i: (b, 0, 0, i))
    packed = pl.BlockSpec((None, tm // TOKENS_PER_ROW, LANES), lambda b, i: (b, i, 0))
    out_shapes = (
        jax.ShapeDtypeStruct((B, S, CONV_CH), BF16),
        jax.ShapeDtypeStruct((B, ATTN_WIDTH, S), BF16),
        jax.ShapeDtypeStruct((B, S, KV_WIDTH), F32),
        jax.ShapeDtypeStruct((B, S, KV_WIDTH), F32),
        jax.ShapeDtypeStruct((B, N_KV_GROUPS, S, MXU_DEPTH), BF16),
        jax.ShapeDtypeStruct((B, N_KV_GROUPS, VT_ROWS, S), BF16),
        jax.ShapeDtypeStruct((B, N_KV_GROUPS, S, HEAD_DIM), BF16),
        jax.ShapeDtypeStruct((B, N_KV_GROUPS, VT_ROWS, S), BF16),
        jax.ShapeDtypeStruct((B, N_KV_GROUPS * GATE_PAD, S), F32),
    )
    return pl.pallas_call(
        _inproj_kernel,
        grid=(B, S // tm),
        in_specs=[tok(D),
                  packed, packed,
                  pl.BlockSpec((None, N_MOD, D), lambda b, i: (b, 0, 0)),
                  _const_spec(gmix.shape), _const_spec(win.shape), _const_spec(wgate.shape),
                  _const_spec(convw.shape),
                  _const_spec(gconv.shape), _const_spec(gmat.shape),
                  pl.BlockSpec((tm, onehot.shape[1]), lambda b, i: (i, 0))],
        out_specs=(tok(CONV_CH), tr(ATTN_WIDTH), tok(KV_WIDTH), tok(KV_WIDTH), grp(MXU_DEPTH),
                   grpT, grp(HEAD_DIM), grpT, tr(N_KV_GROUPS * GATE_PAD)),
        out_shape=out_shapes,
        scratch_shapes=[pltpu.VMEM((8, CONV_CH), F32)],
        compiler_params=pltpu.CompilerParams(dimension_semantics=("arbitrary", "arbitrary"),
                                             vmem_limit_bytes=VMEM_LIMIT),
        name="mixer_inproj",
    )(x, cos_p, sin_p, mod, gmix, win, wgate, convw, gconv, gmat, onehot)


def _compress_kernel(kf_ref, vf_ref, pek_ref, pev_ref, wk1_ref, wk2_ref, wv1_ref, wv2_ref,
                     kcc_ref, vcT_ref):
    half = CMP_BLOCK // 2
    n = kf_ref.shape[0] // half

    def mlp(x_ref, pe_ref, w1_ref, w2_ref):
        parts = []
        for p in range(2):
            acc = None
            for l0 in range(0, half, 2):
                xs, ws = [], []
                for l in (l0, l0 + 1):
                    row = p * half + l
                    xs.append((x_ref[pl.ds(l, n, stride=half), :]
                               + pe_ref[row:row + 1, :]).astype(BF16))
                    w = w1_ref[row * HEAD_DIM:(row + 1) * HEAD_DIM, :].astype(BF16)
                    z = jnp.zeros_like(w)
                    ws.append(jnp.concatenate([jnp.concatenate([w, z], axis=1),
                                               jnp.concatenate([z, w], axis=1)], axis=0))
                d = _dot(jnp.concatenate(xs, axis=1), jnp.concatenate(ws, axis=0))
                acc = d if acc is None else acc + d
            parts.append(acc)
        hpre = parts[0] + pltpu.roll(parts[1], n - 1, 0)
        hid = (hpre * jax.nn.sigmoid(hpre)).astype(BF16)
        w2 = w2_ref[...].astype(BF16)
        return jnp.concatenate([_dot(hid[:, g * CMP_HIDDEN:(g + 1) * CMP_HIDDEN], w2)
                                for g in range(N_KV_GROUPS)], axis=1)

    kc = mlp(kf_ref, pek_ref, wk1_ref, wk2_ref).astype(kcc_ref.dtype)
    for g in range(N_KV_GROUPS):
        kcc_ref[g] = kc[:, g * HEAD_DIM:(g + 1) * HEAD_DIM]
    vcT = mlp(vf_ref, pev_ref, wv1_ref, wv2_ref).T.astype(vcT_ref.dtype)
    ones = jnp.ones((ONES_ROWS, n), vcT_ref.dtype)
    for g in range(N_KV_GROUPS):
        vcT_ref[g] = jnp.concatenate([vcT[g * HEAD_DIM:(g + 1) * HEAD_DIM, :], ones], axis=0)


def _compress_call(kf, vf, pek, pev, wk1, wk2, wv1, wv2):
    B, S, width = kf.shape
    n = S // (CMP_BLOCK // 2)
    flat = pl.BlockSpec((None, S, width), lambda b: (b, 0, 0))
    return pl.pallas_call(
        _compress_kernel,
        grid=(B,),
        in_specs=[flat, flat, _const_spec(pek.shape), _const_spec(pev.shape),
                  _const_spec(wk1.shape), _const_spec(wk2.shape), _const_spec(wv1.shape),
                  _const_spec(wv2.shape)],
        out_specs=(pl.BlockSpec((None, N_KV_GROUPS, n, HEAD_DIM), lambda b: (b, 0, 0, 0)),
                   pl.BlockSpec((None, N_KV_GROUPS, VT_ROWS, n), lambda b: (b, 0, 0, 0))),
        out_shape=(jax.ShapeDtypeStruct((B, N_KV_GROUPS, n, HEAD_DIM), BF16),
                   jax.ShapeDtypeStruct((B, N_KV_GROUPS, VT_ROWS, n), BF16)),
        compiler_params=pltpu.CompilerParams(dimension_semantics=("arbitrary",),
                                             vmem_limit_bytes=VMEM_LIMIT),
        name="kv_compress",
    )(kf, vf, pek, pev, wk1, wk2, wv1, wv2)


def _lane_tiles(x, n):
    return jnp.concatenate([x] * n, axis=1)


def _col_max(s_ref, n_rows, bias_fn, lane_tile=None):
    width = s_ref.shape[1]
    lt = lane_tile or width
    groups = ROW_TILE // 8
    mx = {c: [jnp.full((8, lt), NEG, F32)] * groups for c in range(0, width, lt)}
    for r in range(0, n_rows, ROW_TILE):
        bias = None if bias_fn is None else bias_fn(r)
        for c in range(0, width, lt):
            x = s_ref[r:r + ROW_TILE, c:c + lt]
            if bias is not None:
                x = x + bias
                s_ref[r:r + ROW_TILE, c:c + lt] = x
            mx[c] = [jnp.maximum(mx[c][i], x[8 * i:8 * (i + 1), :]) for i in range(groups)]
    out = []
    for c in range(0, width, lt):
        m = mx[c]
        while len(m) > 1:
            m = [jnp.maximum(a, b) for a, b in zip(m[0::2], m[1::2])]
        out.append(jnp.max(m[0], axis=0, keepdims=True))
    return out[0] if len(out) == 1 else jnp.concatenate(out, axis=1)


def _col_exp2(s_ref, p_ref, n_rows, m, keep_f32=False, lane_tile=None):
    width = s_ref.shape[1]
    lt = lane_tile or width
    for r in range(0, n_rows, ROW_TILE):
        for c in range(0, width, lt):
            p = jnp.exp2(s_ref[r:r + ROW_TILE, c:c + lt] - m[:, c:c + lt])
            if keep_f32:
                s_ref[r:r + ROW_TILE, c:c + lt] = p
            p_ref[r:r + ROW_TILE, c:c + lt] = p.astype(p_ref.dtype)


def _recip_pos(l):
    return 1.0 / jnp.where(l > 0.0, l, 1.0)


def _select_kernel(top_n, qT_ref, kcc_ref, vcT_ref, ovlT_ref, sel_ref, ocmp_ref, sc_ref, pc_ref,
                   ph_ref, pl_ref):
    QB = SEL_BLOCK
    n_cmp = kcc_ref.shape[0]
    n_slc = ovlT_ref.shape[0]
    t0 = pl.program_id(2) * QB
    tq = t0 + lax.broadcasted_iota(jnp.int32, (1, QB), 1)
    rows = lax.broadcasted_iota(jnp.int32, (ROW_TILE, 1), 0)

    qT = qT_ref[...]
    qcat = jnp.concatenate([qT[h * HEAD_DIM:(h + 1) * HEAD_DIM, :] for h in range(HPG)], axis=1)

    def cmp_bias(r):
        cmp_end = (r + rows) * CMP_STRIDE + (CMP_BLOCK - 1)
        return jnp.where(cmp_end <= tq, 0.0, NEG)

    cur = tq // SLC_BLOCK

    def tree(op, xs):
        while len(xs) > 1:
            xs = [op(*xs[i:i + 2]) if i + 1 < len(xs) else xs[i] for i in range(0, len(xs), 2)]
        return xs[0]

    def causal_variant(n_rows):
        rows_cmp = min(n_cmp, n_rows * (SLC_BLOCK // CMP_STRIDE))

        def run():
            sc_ref[0:rows_cmp, :] = _dot(kcc_ref[0:rows_cmp, :], qcat)
            m = jnp.maximum(_col_max(sc_ref, rows_cmp, cmp_bias, lane_tile=QB), M_FLOOR)
            _col_exp2(sc_ref, pc_ref, rows_cmp, m, keep_f32=True, lane_tile=QB)
            o_cmp = _dot(vcT_ref[:, 0:rows_cmp], pc_ref[0:rows_cmp, :])
            rl = _recip_pos(o_cmp[HEAD_DIM:HEAD_DIM + 1, :])
            o_cmp = o_cmp[0:HEAD_DIM, :] * rl
            for h in range(HPG):
                ocmp_ref[h] = o_cmp[:, h * QB:(h + 1) * QB]

            for r in range(0, rows_cmp, ROW_TILE):
                psum = sc_ref[r:r + ROW_TILE, 0:QB] * rl[:, 0:QB]
                for h in range(1, HPG):
                    sl = slice(h * QB, (h + 1) * QB)
                    psum = psum + sc_ref[r:r + ROW_TILE, sl] * rl[:, sl]
                hi, lo = _split_bf16(psum)
                ph_ref[r:r + ROW_TILE, :] = hi
                pl_ref[r:r + ROW_TILE, :] = lo
            ovl = ovlT_ref[0:n_rows, 0:rows_cmp]
            imp = _dot(ovl, ph_ref[0:rows_cmp, :]) + _dot(ovl, pl_ref[0:rows_cmp, :])

            blk = lax.broadcasted_iota(jnp.int32, (n_rows, 1), 0)
            future = blk > cur
            forced = (blk == 0) | (blk == cur) | (blk == cur - 1)
            groups = range(n_rows // 8)

            def store(picked):
                sel_ref[0:n_rows, :] = jnp.where(future, NEG, jnp.where(picked, 0.0, NEG)).astype(sel_ref.dtype)

            def with_ties():
                score0 = jnp.where(future, -BIG, jnp.where(forced, BIG, imp))
                score = [score0[8 * i:8 * (i + 1), :] for i in groups]
                cum = jnp.zeros((1, QB), F32)
                thr = jnp.zeros((1, QB), F32)
                above = jnp.zeros((1, QB), F32)
                for _ in range(top_n):
                    best = jnp.max(tree(jnp.maximum, score), axis=0, keepdims=True)
                    eq = [s == best for s in score]
                    unfilled = cum < top_n
                    thr = jnp.where(unfilled, best, thr)
                    above = jnp.where(unfilled, cum, above)
                    cum = cum + jnp.sum(tree(jnp.add, [jnp.where(e, 1.0, 0.0) for e in eq]),
                                        axis=0, keepdims=True)
                    score = [jnp.where(e, -jnp.inf, s) for e, s in zip(eq, score)]
                ties = score0 == thr
                lower = jnp.where(lax.broadcasted_iota(jnp.int32, (1, n_rows), 1) < blk, 1.0, 0.0)
                rank = _dot(lower.astype(BF16), jnp.where(ties, 1.0, 0.0).astype(BF16))
                store((score0 > thr) | (ties & (rank < top_n - above)))

            if top_n > MAX_FORCED:
                n_forced = 1.0 + jnp.where(cur >= 1, 1.0, 0.0) + jnp.where(cur >= 2, 1.0, 0.0)
                wanted = top_n - n_forced
                take_all = (cur.astype(F32) + 1.0 - n_forced) <= wanted
                cand = jnp.where(future | forced, -BIG, imp)
                score = [cand[8 * i:8 * (i + 1), :] for i in groups]
                for _ in range(top_n - MAX_FORCED):
                    cut = jnp.max(tree(jnp.maximum, score), axis=0, keepdims=True)
                    score = [jnp.where(s == cut, -jnp.inf, s) for s in score]
                above_cut = cand >= cut
                count = jnp.sum(jnp.where(above_cut, 1.0, 0.0), axis=0, keepdims=True)
                exact = jnp.min(jnp.where(take_all | (count == wanted), 1.0, 0.0)) > 0.5
                lax.cond(exact, lambda: store(forced | take_all | above_cut), with_ties)
            else:
                with_ties()
            if n_rows < n_slc:
                sel_ref[n_rows:n_slc, :] = jnp.full((n_slc - n_rows, QB), NEG, sel_ref.dtype)
        return run

    steps = n_slc // SEL_ROW_STEP
    need = jnp.minimum((t0 + QB - 1) // (SLC_BLOCK * SEL_ROW_STEP), steps - 1)
    lax.switch(need, [causal_variant(SEL_ROW_STEP * (k + 1)) for k in range(steps)])


def _select_call(top_n, qT, kcc, vcT, ovlT):
    B, _, S = qT.shape
    n_cmp = kcc.shape[2]
    n_slc = ovlT.shape[0]
    gw = HPG * HEAD_DIM
    qw = HPG * SEL_BLOCK
    return pl.pallas_call(
        functools.partial(_select_kernel, top_n),
        grid=(B, N_KV_GROUPS, S // SEL_BLOCK),
        in_specs=[pl.BlockSpec((None, gw, SEL_BLOCK), lambda b, g, i: (b, g, i)),
                  pl.BlockSpec((None, None, n_cmp, HEAD_DIM), lambda b, g, i: (b, g, 0, 0)),
                  pl.BlockSpec((None, None, VT_ROWS, n_cmp), lambda b, g, i: (b, g, 0, 0)),
                  pl.BlockSpec(ovlT.shape, lambda b, g, i: (0, 0))],
        out_specs=(pl.BlockSpec((None, None, n_slc, SEL_BLOCK), lambda b, g, i: (b, g, 0, i)),
                   pl.BlockSpec((None, None, HPG, HEAD_DIM, SEL_BLOCK),
                                lambda b, g, i: (b, g, 0, 0, i))),
        out_shape=(jax.ShapeDtypeStruct((B, N_KV_GROUPS, n_slc, S), BF16),
                   jax.ShapeDtypeStruct((B, N_KV_GROUPS, HPG, HEAD_DIM, S), F32)),
        scratch_shapes=[pltpu.VMEM((n_cmp, qw), F32), pltpu.VMEM((n_cmp, qw), BF16),
                        pltpu.VMEM((n_cmp, SEL_BLOCK), BF16), pltpu.VMEM((n_cmp, SEL_BLOCK), BF16)],
        compiler_params=pltpu.CompilerParams(
            dimension_semantics=("parallel", "parallel", "arbitrary"),
            vmem_limit_bytes=VMEM_LIMIT),
        name="nsa_select",
    )(qT, kcc, vcT, ovlT)


def _attn_kernel(qT_ref, gT_ref, sel_ref, ocmp_ref, ksa_ref, vsT_ref, kw_ref, vwT_ref, ga_ref,
                 o_ref, s0_ref, s1_ref, p0_ref, p1_ref, sw_ref, pw_ref, rhs_ref, acc_ref):
    QB = Q_BLOCK
    n_slc = sel_ref.shape[0]
    t0 = pl.program_id(2) * QB
    tq = t0 + lax.broadcasted_iota(jnp.int32, (1, QB), 1)
    rows = lax.broadcasted_iota(jnp.int32, (ROW_TILE, 1), 0)

    qT = qT_ref[...]
    qcat = jnp.concatenate([qT[h * HEAD_DIM:(h + 1) * HEAD_DIM, :] for h in range(HPG)], axis=1)

    rhs_ref[...] = jnp.concatenate([_lane_tiles(sel_ref[...], HPG), qcat,
                                    jnp.zeros((MXU_DEPTH - n_slc - HEAD_DIM, QW), BF16)], axis=0)

    w0 = pl.multiple_of(jnp.maximum(t0 - WINDOW, 0), QB)
    sw_ref[...] = _dot(kw_ref[pl.ds(w0, WIN_KEYS), :], qcat)

    def win_bias(r):
        dist = tq - (w0 + r + rows)
        return _lane_tiles(jnp.where((dist >= 0) & (dist < WINDOW), 0.0, NEG), HPG)

    KC = KEY_CHUNK
    last = t0 // KC

    def scores(c, dst_ref):
        k0 = pl.multiple_of(jnp.minimum(c, last) * KC, KC)
        dst_ref[...] = _dot(ksa_ref[pl.ds(k0, KC), :], rhs_ref[...])

    def weighted_values(c, p_ref):
        k0 = pl.multiple_of(jnp.clip(c, 0, last) * KC, KC)
        return _dot(vsT_ref[:, pl.ds(k0, KC)], p_ref[...])

    def softmax(c, src_ref, p_ref, m, pending, causal):
        def causal_bias(r):
            return _lane_tiles(jnp.where(c * KC + r + rows <= tq, 0.0, NEG), HPG)

        m_new = jnp.maximum(m, _col_max(src_ref, KC, causal_bias if causal else None))
        acc = acc_ref[...] if pending is None else acc_ref[...] + pending
        acc_ref[...] = jnp.exp2(m - m_new) * acc
        _col_exp2(src_ref, p_ref, KC, m_new)
        return m_new

    def pair(i, m, final=False, first=False):
        pending = None if first else weighted_values(2 * i - 1, p1_ref)
        scores(2 * i + 1, s1_ref)
        m = softmax(2 * i, s0_ref, p0_ref, m, pending, False)
        pending = weighted_values(2 * i, p0_ref)
        if not final:
            scores(2 * i + 2, s0_ref)
        return softmax(2 * i + 1, s1_ref, p1_ref, m, pending, final)

    p1_ref[...] = jnp.zeros_like(p1_ref)
    acc_ref[...] = jnp.zeros_like(acc_ref)
    scores(0, s0_ref)

    m = _col_max(sw_ref, WIN_KEYS, win_bias)
    _col_exp2(sw_ref, pw_ref, WIN_KEYS, m)
    o_win = _dot(vwT_ref[:, pl.ds(w0, WIN_KEYS)], pw_ref[...])
    o_win = o_win[0:HEAD_DIM, :] * (1.0 / o_win[HEAD_DIM:HEAD_DIM + 1, :])

    full_pairs = last // 2
    m = lax.cond(full_pairs >= 1, lambda mm: pair(0, mm, first=True), lambda mm: mm,
                 jnp.full((1, QW), NEG, F32))
    rest = jnp.maximum(full_pairs - 1, 0)
    m = lax.fori_loop(0, rest, lambda j, mm: pair(j + 1, mm), m)

    @pl.when(last % 2 == 0)
    def _():
        pending = weighted_values(last - 1, p1_ref)
        softmax(last, s0_ref, p0_ref, m, pending, True)
        acc_ref[...] += weighted_values(last, p0_ref)

    @pl.when(last % 2 == 1)
    def _():
        pair(full_pairs, m, final=True)
        acc_ref[...] += weighted_values(last, p1_ref)

    acc = acc_ref[...]
    o_slc = acc[0:HEAD_DIM, :] * (1.0 / acc[HEAD_DIM:HEAD_DIM + 1, :])

    gts = gT_ref[...]
    outs = []
    for h in range(HPG):
        sl = slice(h * QB, (h + 1) * QB)
        o = (gts[3 * h:3 * h + 1, :] * ocmp_ref[h] + gts[3 * h + 1:3 * h + 2, :] * o_slc[:, sl]
             + gts[3 * h + 2:3 * h + 3, :] * o_win[:, sl])
        o = o * lax.rsqrt(jnp.mean(o * o, axis=0, keepdims=True) + EPS)
        outs.append((o * ga_ref[h * HEAD_DIM:(h + 1) * HEAD_DIM, :]).T)
    o_ref[...] = jnp.concatenate(outs, axis=1).astype(o_ref.dtype)


def _attn_call(qT, gatesT, sel_bias, o_cmp, ksa, vsT, kw, vwT, g_attn_col):
    B, _, S = qT.shape
    n_slc = sel_bias.shape[2]
    gw = HPG * HEAD_DIM
    rows = lambda n, w: pl.BlockSpec((None, None, n, w), lambda b, g, i: (b, g, 0, 0))
    cols = lambda n: pl.BlockSpec((None, None, VT_ROWS, n), lambda b, g, i: (b, g, 0, 0))
    return pl.pallas_call(
        _attn_kernel,
        grid=(B, N_KV_GROUPS, S // Q_BLOCK),
        in_specs=[pl.BlockSpec((None, gw, Q_BLOCK), lambda b, g, i: (b, g, i)),
                  pl.BlockSpec((None, GATE_PAD, Q_BLOCK), lambda b, g, i: (b, g, i)),
                  pl.BlockSpec((None, None, n_slc, Q_BLOCK), lambda b, g, i: (b, g, 0, i)),
                  pl.BlockSpec((None, None, HPG, HEAD_DIM, Q_BLOCK), lambda b, g, i: (b, g, 0, 0, i)),
                  rows(S, MXU_DEPTH), cols(S), rows(S, HEAD_DIM), cols(S),
                  pl.BlockSpec((gw, 1), lambda b, g, i: (g, 0))],
        out_specs=pl.BlockSpec((None, Q_BLOCK, gw), lambda b, g, i: (b, i, g)),
        out_shape=jax.ShapeDtypeStruct((B, S, ATTN_WIDTH), BF16),
        scratch_shapes=[pltpu.VMEM((KEY_CHUNK, QW), F32), pltpu.VMEM((KEY_CHUNK, QW), F32),
                        pltpu.VMEM((KEY_CHUNK, QW), BF16), pltpu.VMEM((KEY_CHUNK, QW), BF16),
                        pltpu.VMEM((WIN_KEYS, QW), F32), pltpu.VMEM((WIN_KEYS, QW), BF16),
                        pltpu.VMEM((MXU_DEPTH, QW), BF16), pltpu.VMEM((VT_ROWS, QW), F32)],
        compiler_params=pltpu.CompilerParams(
            dimension_semantics=("parallel", "parallel", "arbitrary"),
            vmem_limit_bytes=VMEM_LIMIT),
        name="nsa_attention",
    )(qT, gatesT, sel_bias, o_cmp, ksa, vsT, kw, vwT, g_attn_col)


def _out_kernel(x_ref, yc_ref, ya_ref, mod_ref, wo_ref, g_ref, wg_ref, wu_ref, wd_ref, gf_ref,
                o_ref, a_ref, wgb_ref, wub_ref):
    _cast_once((wg_ref, wu_ref), (wgb_ref, wub_ref))
    mix = _dot(yc_ref[...], wo_ref[0:CONV_CH, :]) + _dot(ya_ref[...], wo_ref[CONV_CH:, :])
    x = x_ref[...] + mod_ref[5:6, :] * mix
    x = _ffn_core(x, mod_ref[6:7, :], mod_ref[7:8, :], mod_ref[8:9, :], g_ref[...], wgb_ref,
                  wub_ref, wd_ref, a_ref)
    o_ref[...] = _rms(x, gf_ref[...])


def _out_call(x, yc, ya, mod, wo, g, wg, wu, wd, gf):
    B, S, D = x.shape
    tm = FFN_TOKEN_TILE
    tok = lambda w: pl.BlockSpec((None, tm, w), lambda b, i: (b, i, 0))
    return pl.pallas_call(
        _out_kernel,
        grid=(B, S // tm),
        in_specs=[tok(D), tok(CONV_CH), tok(ATTN_WIDTH),
                  pl.BlockSpec((None, N_MOD, D), lambda b, i: (b, 0, 0)),
                  _const_spec(wo.shape), _const_spec(g.shape), _const_spec(wg.shape),
                  _const_spec(wu.shape), _const_spec(wd.shape), _const_spec(gf.shape)],
        out_specs=tok(D),
        out_shape=jax.ShapeDtypeStruct((B, S, D), F32),
        scratch_shapes=[pltpu.VMEM((tm, wg.shape[1]), BF16), pltpu.VMEM(wg.shape, BF16),
                        pltpu.VMEM(wu.shape, BF16)],
        compiler_params=pltpu.CompilerParams(dimension_semantics=("arbitrary", "arbitrary"),
                                             vmem_limit_bytes=VMEM_LIMIT),
        name="outproj_ffn2",
    )(x, yc, ya, mod, wo, g, wg, wu, wd, gf)


def kernel(x, c, positions, w_ada, b_ada, g_ffn1, w1_gate, w1_up, w1_down, g_mix, w_in, conv_w, cmp_pos_k, cmp_pos_v, w_cmpk1, w_cmpk2, w_cmpv1, w_cmpv2, g_out_conv, g_out_attn, w_out, g_ffn2, w2_gate, w2_up, w2_down, g_final):
    B, S, D = x.shape
    depth = w_ada.shape[0]
    n_slc = S // SLC_BLOCK
    half = CMP_BLOCK // 2
    n_half = S // half
    assert n_slc <= LANES, "selection-block one-hot is one lane tile wide"

    c_pad = jnp.pad(c, ((0, 8 - B), (0, 0)))
    row = lambda a: a.reshape(1, -1)

    freq_half = jnp.power(ROPE_THETA, -2.0 * jnp.arange(ROT_HALF, dtype=F32) / ROT_DIM)
    freq = jnp.tile(freq_half, LANES // ROT_HALF).reshape(1, LANES)
    gidx = np.arange(CONV_CH) // (CONV_CH // CONV_GROUPS)
    gmat = jnp.asarray((gidx[:, None] == gidx[None, :]) / (CONV_CH // CONV_GROUPS), dtype=BF16)
    c0 = np.arange(n_half) * CMP_STRIDE
    s0 = np.arange(LANES) * SLC_BLOCK
    ovlT = ((c0[None, :] <= s0[:, None] + SLC_BLOCK - 1) & (c0[None, :] + CMP_BLOCK - 1 >= s0[:, None]))
    ovlT = jnp.asarray(ovlT, dtype=BF16)
    onehot = jnp.asarray((np.arange(S) // SLC_BLOCK)[:, None] == np.arange(LANES)[None, :], dtype=BF16)
    cos_p, sin_p = _rope_table_call(positions.reshape(-1, TOKENS_PER_ROW), freq)
    cos_p = cos_p.reshape(B, S // TOKENS_PER_ROW, LANES)
    sin_p = sin_p.reshape(B, S // TOKENS_PER_ROW, LANES)

    for l in range(depth):
        mod = _ada_call(c_pad, w_ada[l], row(b_ada[l]))[:B].reshape(B, N_MOD, D)

        x = _ffn_call(x, mod, row(g_ffn1[l]), w1_gate[l], w1_up[l], w1_down[l].astype(BF16))

        n_main = w_in.shape[2] - N_KV_GROUPS * 3 * HPG
        gate_cols = [jnp.pad(w_in[l][:, n_main + g * 3 * HPG:n_main + (g + 1) * 3 * HPG],
                             ((0, 0), (0, GATE_PAD - 3 * HPG))) for g in range(N_KV_GROUPS)]
        wgate = jnp.concatenate(gate_cols, axis=1).astype(BF16)
        win = w_in[l].astype(BF16)
        (yc, qT, kc, vc, ksa, vsT, kw, vwT, gatesT) = _inproj_call(
            x, cos_p, sin_p, mod, row(g_mix[l]), win, wgate, conv_w[l], row(g_out_conv[l]), gmat,
            onehot)

        both_groups = lambda pe: jnp.tile(pe, (1, N_KV_GROUPS))
        kcc, vcT = _compress_call(kc, vc, both_groups(cmp_pos_k[l]), both_groups(cmp_pos_v[l]),
                                  w_cmpk1[l], w_cmpk2[l], w_cmpv1[l], w_cmpv2[l])

        sel_bias, o_cmp = _select_call(min(SLC_TOP_N, n_slc), qT, kcc, vcT, ovlT)
        ya = _attn_call(qT, gatesT, sel_bias, o_cmp, ksa, vsT, kw, vwT,
                        g_out_attn[l].reshape(ATTN_WIDTH, 1))

        assert l == depth - 1, "final norm is fused into the last layer's output kernel"
        x = _out_call(x, yc, ya, mod, w_out[l].astype(BF16), row(g_ffn2[l]),
                      w2_gate[l], w2_up[l], w2_down[l].astype(BF16),
                      row(g_final))
    return x
```
